```python
import math
import jax, jax.numpy as jnp
from jax import lax
import numpy as np

D_MODEL = 1024
BATCH = 32
SEQ = 256
DEPTH = 2
DEC_BATCH = 2
DEC_SEQ = 2048
PAST_LEN = 512

GRID_W = 64
HEAD_DIM = 64
NA_HEADS = 4
NA_WIDTH = NA_HEADS * HEAD_DIM
WIN_ROWS = 8
WIN_COLS = 16
Q_COLS = 16
K_COLS = 32
DIFF_HEADS = 4
DIFF_V_DIM = 2 * HEAD_DIM
DIFF_WIDTH = DIFF_HEADS * DIFF_V_DIM
HY_WIDTH = D_MODEL - NA_WIDTH - DIFF_WIDTH
HY_ORDER = 2
HY_EMB = 33
HY_BANDS = (HY_EMB - 1) // 2
HY_FILTER_HIDDEN = 64
HY_FAST_DECAY = 0.3
HY_SLOW_DECAY = 1.5
HY_DECAY_TARGET = 1e-2
MIX_WIDTH = NA_WIDTH + DIFF_WIDTH + HY_WIDTH
PROJ_WIDTH = 3 * NA_WIDTH + 3 * DIFF_WIDTH + 3 * HY_WIDTH
FFN_DIM = 2816
N_EXPERTS = 8
TOP_K = 2
EXPERT_DIM = 3584
N_DENSE = (DEPTH + 1) // 2
N_MOE = DEPTH // 2
ROPE_BASE = 10000.0
Q_BLOCK = 128
EPS = 1e-6

kernel_name = 'hybrid_na_diff_hyena_diffusion_step'

F32 = jnp.float32


def rmsnorm(x, g):
    xf = x.astype(F32)
    y = xf * lax.rsqrt(jnp.mean(xf * xf, axis=-1, keepdims=True) + EPS)
    return y.astype(x.dtype) * g


def modulation(cond, w, b):
    m = jax.nn.silu(cond) @ w + b
    return jnp.split(m[..., None, :], 6, axis=-1)


def split_heads(p):
    B, L, _ = p.shape
    cuts = [NA_WIDTH, 2 * NA_WIDTH, 3 * NA_WIDTH, 3 * NA_WIDTH + DIFF_WIDTH,
            3 * NA_WIDTH + 2 * DIFF_WIDTH, 3 * NA_WIDTH + 3 * DIFF_WIDTH]
    na_q, na_k, na_v, dq, dk, dv, hy = jnp.split(p, cuts, axis=-1)
    na = lambda a: a.reshape(B, L, NA_HEADS, HEAD_DIM).transpose(0, 2, 1, 3)
    dqk = lambda a: a.reshape(B, L, DIFF_HEADS, 2, HEAD_DIM).transpose(0, 2, 3, 1, 4)
    dv = dv.reshape(B, L, DIFF_HEADS, DIFF_V_DIM).transpose(0, 2, 1, 3)
    return na(na_q), na(na_k), na(na_v), dqk(dq), dqk(dk), dv, hy


def axial_rope(x):
    L = x.shape[-2]
    t = jnp.arange(L)
    half = HEAD_DIM // 2
    quarter = half // 2
    inv = ROPE_BASE ** (-jnp.arange(quarter, dtype=F32) * 2.0 / half)

    def rot(xa, pos):
        ang = pos.astype(F32)[:, None] * inv[None, :]
        ang = jnp.concatenate([ang, ang], axis=-1)
        rh = jnp.concatenate([-xa[..., quarter:], xa[..., :quarter]], axis=-1)
        return xa * jnp.cos(ang) + rh * jnp.sin(ang)

    out = jnp.concatenate([rot(x[..., :half], t // GRID_W), rot(x[..., half:], t % GRID_W)], axis=-1)
    return out.astype(x.dtype)


def blocked_attend(q, k, v):
    B, H, Lq, d = q.shape
    nb = Lq // Q_BLOCK
    qb = jnp.moveaxis(q.reshape(B, H, nb, Q_BLOCK, d), 2, 0)

    def one(qi):
        s = jnp.einsum('bhqd,bhkd->bhqk', qi, k).astype(F32) * d ** -0.5
        p = jax.nn.softmax(s, axis=-1).astype(v.dtype)
        return jnp.einsum('bhqk,bhkv->bhqv', p, v)

    o = lax.map(one, qb)
    return jnp.moveaxis(o, 0, 2).reshape(B, H, Lq, v.shape[-1])


def neighbourhood_attend(q, k, v, kc, vc, rpb):
    B, H, L, dh = q.shape
    rows = L // GRID_W
    wr = min(WIN_ROWS, rows)
    nqb = GRID_W // Q_COLS
    r = jnp.arange(rows)
    row_idx = jnp.clip(r - wr // 2, 0, rows - wr)[:, None] + jnp.arange(wr)[None, :]
    blk = jnp.arange(nqb)
    key_col = jnp.clip(blk * Q_COLS - WIN_COLS // 2, 0, GRID_W - K_COLS)[:, None] + jnp.arange(K_COLS)[None, :]
    q_col = blk[:, None] * Q_COLS + jnp.arange(Q_COLS)[None, :]
    q_start = jnp.clip(q_col - WIN_COLS // 2, 0, GRID_W - WIN_COLS)
    in_win = (key_col[:, None, :] >= q_start[:, :, None]) & (key_col[:, None, :] < q_start[:, :, None] + WIN_COLS)
    ri = row_idx[:, None, :, None]
    ci = key_col[None, :, None, :]
    kg = k.reshape(B, H, rows, GRID_W, dh)[:, :, ri, ci].reshape(B, H, rows, nqb, wr * K_COLS, dh)
    vg = v.reshape(B, H, rows, GRID_W, dh)[:, :, ri, ci].reshape(B, H, rows, nqb, wr * K_COLS, dh)
    dr = row_idx - r[:, None] + (WIN_ROWS - 1)
    dc = jnp.clip(key_col[:, None, :] - q_col[:, :, None] + (WIN_COLS - 1), 0, 2 * WIN_COLS - 2)
    bias = rpb.astype(F32)[:, dr[:, None, None, :, None], dc[None, :, :, None, :]]
    bias = jnp.where(in_win[None, :, :, None, :], bias, -jnp.inf).reshape(H, rows, nqb, Q_COLS, wr * K_COLS)
    qb = q.reshape(B, H, rows, nqb, Q_COLS, dh)
    scale = dh ** -0.5
    s_loc = jnp.einsum('bhrnqd,bhrnkd->bhrnqk', qb, kg).astype(F32) * scale + bias[None]
    s_ctx = jnp.einsum('bhrnqd,bhcd->bhrnqc', qb, kc).astype(F32) * scale
    p = jax.nn.softmax(jnp.concatenate([s_loc, s_ctx], axis=-1), axis=-1).astype(v.dtype)
    nloc = wr * K_COLS
    o = (jnp.einsum('bhrnqk,bhrnkd->bhrnqd', p[..., :nloc], vg)
         + jnp.einsum('bhrnqc,bhcd->bhrnqd', p[..., nloc:], vc))
    return o.reshape(B, H, L, dh)


def diff_lambda(lq1, lk1, lq2, lk2, lam_init):
    return (jnp.exp(jnp.sum(lq1.astype(F32) * lk1.astype(F32)))
            - jnp.exp(jnp.sum(lq2.astype(F32) * lk2.astype(F32))) + lam_init)


def diff_attend(q, k, v, lam):
    B, H, _, Lq, dh = q.shape
    nb = Lq // Q_BLOCK
    qb = jnp.moveaxis(q.reshape(B, H, 2, nb, Q_BLOCK, dh), 3, 0)

    def one(qi):
        s = jnp.einsum('bhiqd,bhikd->bhiqk', qi, k).astype(F32) * dh ** -0.5
        p = jax.nn.softmax(s, axis=-1)
        a = (p[:, :, 0] - lam * p[:, :, 1]).astype(v.dtype)
        return jnp.einsum('bhqk,bhkv->bhqv', a, v)

    o = lax.map(one, qb)
    return jnp.moveaxis(o, 0, 2).reshape(B, H, Lq, v.shape[-1])


def hyena_filter_spectra(L, f_w1, f_b1, f_w2, f_b2, f_freq, f_w3):
    t = jnp.linspace(0.0, 1.0, L, dtype=F32)[:, None]
    pos = jnp.arange(L, dtype=F32)[:, None]
    bands = jnp.linspace(1e-4, HY_BANDS - 1, HY_BANDS, dtype=F32)[None, :]
    ang = 2.0 * math.pi * bands * pos / L
    z = jnp.concatenate([t, jnp.cos(ang), -jnp.sin(ang)], axis=-1)
    freq = f_freq.astype(F32)
    hdn = jnp.sin(freq * (z @ f_w1.astype(F32) + f_b1.astype(F32)))
    hdn = jnp.sin(freq * (hdn @ f_w2.astype(F32) + f_b2.astype(F32)))
    h = (hdn @ f_w3.astype(F32)).reshape(L, HY_ORDER, 2, HY_WIDTH)
    min_decay = math.log(HY_DECAY_TARGET) / HY_SLOW_DECAY
    max_decay = math.log(HY_DECAY_TARGET) / HY_FAST_DECAY
    deltas = jnp.abs(jnp.linspace(min_decay, max_decay, HY_WIDTH, dtype=F32))
    h = h * jnp.exp(-t * deltas[None, :])[:, None, None, :]
    kbuf = jnp.concatenate([h[:, :, 0], jnp.zeros((1, HY_ORDER, HY_WIDTH), F32), h[:0:-1, :, 1]], axis=0)
    kbuf = kbuf / jnp.sum(jnp.abs(kbuf), axis=0, keepdims=True)
    return jnp.fft.rfft(kbuf, axis=0)


def long_conv(x, spec, d):
    L = x.shape[1]
    xf = x.astype(F32)
    y = jnp.fft.irfft(jnp.fft.rfft(xf, n=2 * L, axis=1) * spec[None], n=2 * L, axis=1)[:, :L]
    return (y + xf * d.astype(F32)).astype(x.dtype)


def hyena(u, conv_w, conv_b, d, f_w1, f_b1, f_w2, f_b2, f_freq, f_w3):
    L = u.shape[1]
    up = jnp.pad(u, ((0, 0), (1, 1), (0, 0)))
    u = up[:, :L] * conv_w[0] + up[:, 1:L + 1] * conv_w[1] + up[:, 2:] * conv_w[2] + conv_b
    v, g1, g2 = jnp.split(u, 3, axis=-1)
    spec = hyena_filter_spectra(L, f_w1, f_b1, f_w2, f_b2, f_freq, f_w3)
    z = g1 * long_conv(v, spec[:, 0], d[0])
    return g2 * long_conv(z, spec[:, 1], d[1])


def merge_mixers(na_o, d_o, hy_o, subln_g, lam_init, w_out):
    B, _, L, _ = na_o.shape
    na = na_o.transpose(0, 2, 1, 3).reshape(B, L, NA_WIDTH)
    dif = (rmsnorm(d_o, subln_g) * (1.0 - lam_init)).transpose(0, 2, 1, 3).reshape(B, L, DIFF_WIDTH)
    return jnp.concatenate([na, dif, hy_o], axis=-1) @ w_out


def swiglu(h, wg, wu, wd):
    return (jax.nn.silu(h @ wg) * (h @ wu)) @ wd


def moe_swiglu(h, router, wg, wu, wd):
    logits = (h @ router).astype(F32)
    top_v, top_i = lax.top_k(logits, TOP_K)
    w = jax.nn.softmax(top_v, axis=-1)
    comb = jnp.sum(jax.nn.one_hot(top_i, N_EXPERTS, dtype=F32) * w[..., None], axis=-2).astype(h.dtype)
    out = jnp.zeros_like(h)
    for e in range(N_EXPERTS):
        out = out + comb[..., e:e + 1] * swiglu(h, wg[e], wu[e], wd[e])
    return out


def channel_mixer(h, l, ffn_w_gate, ffn_w_up, ffn_w_down, moe_router, moe_w_gate, moe_w_up, moe_w_down):
    i = l // 2
    if l % 2 == 0:
        return swiglu(h, ffn_w_gate[i], ffn_w_up[i], ffn_w_down[i])
    return moe_swiglu(h, moe_router[i], moe_w_gate[i], moe_w_up[i], moe_w_down[i])


def setup_inputs(seed: int = 0) -> dict:
    key = jax.random.key(seed)
    ks = iter(jax.random.split(key, 40))
    nrm = lambda shape, s: s * jax.random.normal(next(ks), shape, F32)
    gain = lambda shape: 1.0 + nrm(shape, 0.02)
    D = D_MODEL
    return {
        'x_prompt': nrm((BATCH, SEQ, D), 1.0),
        'x_sample': nrm((DEC_BATCH, DEC_SEQ, D), 1.0),
        'cache_na_k': nrm((DEC_BATCH, DEPTH, NA_HEADS, PAST_LEN, HEAD_DIM), 1.0),
        'cache_na_v': nrm((DEC_BATCH, DEPTH, NA_HEADS, PAST_LEN, HEAD_DIM), 1.0),
        'cache_diff_k': nrm((DEC_BATCH, DEPTH, DIFF_HEADS, 2, PAST_LEN, HEAD_DIM), 1.0),
        'cache_diff_v': nrm((DEC_BATCH, DEPTH, DIFF_HEADS, PAST_LEN, DIFF_V_DIM), 1.0),
        'c': nrm((DEC_BATCH, D), 1.0),
        'c_ctx': nrm((D,), 1.0),
        'w_in': nrm((DEPTH, D, PROJ_WIDTH), D ** -0.5),
        'w_out': nrm((DEPTH, MIX_WIDTH, D), MIX_WIDTH ** -0.5),
        'ada_w': nrm((DEPTH, D, 6 * D), 0.5 * D ** -0.5),
        'ada_b': nrm((DEPTH, 6 * D), 0.01),
        'norm_mix_g': gain((DEPTH, D)),
        'norm_ffn_g': gain((DEPTH, D)),
        'na_rpb': nrm((DEPTH, NA_HEADS, 2 * WIN_ROWS - 1, 2 * WIN_COLS - 1), 0.1),
        'diff_lq1': nrm((DEPTH, HEAD_DIM), 0.1),
        'diff_lk1': nrm((DEPTH, HEAD_DIM), 0.1),
        'diff_lq2': nrm((DEPTH, HEAD_DIM), 0.1),
        'diff_lk2': nrm((DEPTH, HEAD_DIM), 0.1),
        'diff_subln_g': gain((DEPTH, DIFF_V_DIM)),
        'hy_conv_w': nrm((DEPTH, 3, 3 * HY_WIDTH), 3 ** -0.5),
        'hy_conv_b': nrm((DEPTH, 3 * HY_WIDTH), 0.01),
        'hy_d': nrm((DEPTH, HY_ORDER, HY_WIDTH), 1.0),
        'hy_f_w1': nrm((DEPTH, HY_EMB, HY_FILTER_HIDDEN), HY_EMB ** -0.5),
        'hy_f_b1': nrm((DEPTH, HY_FILTER_HIDDEN), 0.1),
        'hy_f_w2': nrm((DEPTH, HY_FILTER_HIDDEN, HY_FILTER_HIDDEN), HY_FILTER_HIDDEN ** -0.5),
        'hy_f_b2': nrm((DEPTH, HY_FILTER_HIDDEN), 0.1),
        'hy_f_freq': gain((DEPTH, HY_FILTER_HIDDEN)),
        'hy_f_w3': nrm((DEPTH, HY_FILTER_HIDDEN, HY_ORDER * 2 * HY_WIDTH), HY_FILTER_HIDDEN ** -0.5),
        'ffn_w_gate': nrm((N_DENSE, D, FFN_DIM), D ** -0.5),
        'ffn_w_up': nrm((N_DENSE, D, FFN_DIM), D ** -0.5),
        'ffn_w_down': nrm((N_DENSE, FFN_DIM, D), FFN_DIM ** -0.5),
        'moe_router': nrm((N_MOE, D, N_EXPERTS), D ** -0.5),
        'moe_w_gate': nrm((N_MOE, N_EXPERTS, D, EXPERT_DIM), D ** -0.5),
        'moe_w_up': nrm((N_MOE, N_EXPERTS, D, EXPERT_DIM), D ** -0.5),
        'moe_w_down': nrm((N_MOE, N_EXPERTS, EXPERT_DIM, D), EXPERT_DIM ** -0.5),
        'final_norm_g': gain((D,)),
    }


def reference(x_prompt, x_sample, cache_na_k, cache_na_v, cache_diff_k, cache_diff_v, c, c_ctx,
              w_in, w_out, ada_w, ada_b, norm_mix_g, norm_ffn_g, na_rpb,
              diff_lq1, diff_lk1, diff_lq2, diff_lk2, diff_subln_g,
              hy_conv_w, hy_conv_b, hy_d, hy_f_w1, hy_f_b1, hy_f_w2, hy_f_b2, hy_f_freq, hy_f_w3,
              ffn_w_gate, ffn_w_up, ffn_w_down, moe_router, moe_w_gate, moe_w_up, moe_w_down,
              final_norm_g):
    xp, xs = x_prompt, x_sample
    na_ks, na_vs, d_ks, d_vs = [], [], [], []
    for l in range(DEPTH):
        lam_init = 0.8 - 0.6 * math.exp(-0.3 * l)
        lam = diff_lambda(diff_lq1[l], diff_lk1[l], diff_lq2[l], diff_lk2[l], lam_init)
        hy_args = (hy_conv_w[l], hy_conv_b[l], hy_d[l], hy_f_w1[l], hy_f_b1[l], hy_f_w2[l],
                   hy_f_b2[l], hy_f_freq[l], hy_f_w3[l])
        ffn_args = (ffn_w_gate, ffn_w_up, ffn_w_down, moe_router, moe_w_gate, moe_w_up, moe_w_down)

        sh1, sc1, g1, sh2, sc2, g2 = modulation(c_ctx, ada_w[l], ada_b[l])
        h = rmsnorm(xp, norm_mix_g[l]) * (1.0 + sc1) + sh1
        na_q, na_k, na_v, dq, dk, dv, hy = split_heads(h @ w_in[l])
        na_o = blocked_attend(na_q, na_k, na_v)
        d_o = diff_attend(dq, dk, dv, lam)
        hy_o = hyena(hy, *hy_args)
        xp = xp + g1 * merge_mixers(na_o, d_o, hy_o, diff_subln_g[l], lam_init, w_out[l])
        na_ks.append(na_k)
        na_vs.append(na_v)
        d_ks.append(dk)
        d_vs.append(dv)
        h = rmsnorm(xp, norm_ffn_g[l]) * (1.0 + sc2) + sh2
        xp = xp + g2 * channel_mixer(h, l, *ffn_args)

        sh1, sc1, g1, sh2, sc2, g2 = modulation(c, ada_w[l], ada_b[l])
        h = rmsnorm(xs, norm_mix_g[l]) * (1.0 + sc1) + sh1
        na_q, na_k, na_v, dq, dk, dv, hy = split_heads(h @ w_in[l])
        na_o = neighbourhood_attend(na_q, na_k, na_v, cache_na_k[:, l], cache_na_v[:, l], na_rpb[l])
        k_all = jnp.concatenate([axial_rope(dk), cache_diff_k[:, l]], axis=3)
        v_all = jnp.concatenate([dv, cache_diff_v[:, l]], axis=2)
        d_o = diff_attend(axial_rope(dq), k_all, v_all, lam)
        hy_o = hyena(hy, *hy_args)
        xs = xs + g1 * merge_mixers(na_o, d_o, hy_o, diff_subln_g[l], lam_init, w_out[l])
        h = rmsnorm(xs, norm_ffn_g[l]) * (1.0 + sc2) + sh2
        xs = xs + g2 * channel_mixer(h, l, *ffn_args)

    y_prompt = rmsnorm(xp, final_norm_g)
    y_sample = rmsnorm(xs, final_norm_g)
    new_na_k = jnp.stack(na_ks, axis=1)
    new_na_v = jnp.stack(na_vs, axis=1)
    new_diff_k = jnp.stack(d_ks, axis=1)
    new_diff_v = jnp.stack(d_vs, axis=1)
    return (y_prompt, y_sample, new_na_k, new_na_v, new_diff_k, new_diff_v)
```

```python
import functools
import math

import numpy as np
import jax
import jax.numpy as jnp
from jax import lax
from jax.experimental import pallas as pl
from jax.experimental.pallas import tpu as pltpu

F32 = jnp.float32
BF16 = jnp.bfloat16

D = 1024
BATCH, SEQ = 32, 256
DEC_BATCH, DEC_SEQ = 2, 2048
DEPTH = 2
PAST = 512
GRID_W = 64
GRID_ROWS = DEC_SEQ // GRID_W
HEAD_DIM = 64
NA_HEADS = 4
DIFF_HEADS = 4
DIFF_V = 128
WIN_ROWS, WIN_COLS = 8, 16
HY_W = 256
HY_EMB = 33
HY_BANDS = 16
HY_HID = 64
PROJ = 3072
FFN = 2816
N_EXP = 8
EXP_DIM = 3584
EPS = 1e-6
ROPE_BASE = 10000.0

NP_TOK = BATCH * SEQ
NS_TOK = DEC_BATCH * DEC_SEQ
N_TOK = NP_TOK + NS_TOK

C_NAQ, C_NAK, C_NAV = 0, 256, 512
C_DQ, C_DK, C_DV = 768, 1280, 1792
C_HY = 2304

TM = 1024
VMEM_LIMIT = 56 * 1024 * 1024

MOE_TM = 1024
MOE_TF = 512
MOE_ROWS = 2 * N_TOK + N_EXP * MOE_TM
MOE_TILES = MOE_ROWS // MOE_TM


def _cparams(sem):
    return pltpu.CompilerParams(dimension_semantics=sem, vmem_limit_bytes=VMEM_LIMIT)


def _dot(a, b):
    return jnp.dot(a, b, preferred_element_type=F32)


def _dot_nt(a, b):
    return lax.dot_general(a, b, (((1,), (1,)), ((), ())), preferred_element_type=F32)


def _split(a):
    hi = a.astype(BF16)
    lo = (a - hi.astype(F32)).astype(BF16)
    return hi, lo


def _dot3(a, b):
    ah, al = _split(a)
    bh, bl = _split(b)
    return _dot(ah, bh) + (_dot(ah, bl) + _dot(al, bh))


def _silu(x):
    return x / (1.0 + jnp.exp(-x))


def _mod_row(i, tm):
    t = i * tm
    return jnp.where(t < NP_TOK, 0, 1 + (t - NP_TOK) // DEC_SEQ)


def _mod_spec(k, tm):
    return pl.BlockSpec((None, None, 1, D), lambda i, *_: (k, _mod_row(i, tm), 0, 0))


def _norm_mod(x, g, sc, sh):
    y = x * lax.rsqrt(jnp.mean(x * x, axis=-1, keepdims=True) + EPS)
    return (y * g) * (1.0 + sc) + sh


def _mod_kernel(c_ref, w_ref, b_ref, o_ref):
    o_ref[...] = _dot3(_silu(c_ref[...]), w_ref[...]) + b_ref[...]


def _modulation(cond8, ada_w, ada_b):
    tn = 1536
    return pl.pallas_call(
        _mod_kernel,
        grid=(DEPTH, 6 * D // tn),
        in_specs=[pl.BlockSpec((8, D), lambda l, j: (0, 0)),
                  pl.BlockSpec((None, D, tn), lambda l, j: (l, 0, j)),
                  pl.BlockSpec((None, 1, tn), lambda l, j: (l, 0, j))],
        out_specs=pl.BlockSpec((None, 8, tn), lambda l, j: (l, 0, j)),
        out_shape=jax.ShapeDtypeStruct((DEPTH, 8, 6 * D), F32),
        compiler_params=_cparams(("arbitrary", "arbitrary")),
    )(cond8, ada_w, ada_b.reshape(DEPTH, 1, 6 * D))


def _inproj_kernel(x_ref, g_ref, sc_ref, sh_ref, w_ref, o_ref, h_scr):
    @pl.when(pl.program_id(1) == 0)
    def _():
        h_scr[...] = _norm_mod(x_ref[...], g_ref[...], sc_ref[...], sh_ref[...]).astype(BF16)

    o_ref[...] = _dot(h_scr[...], w_ref[...].astype(BF16))


def _inproj(x, g, mod6, w_in, l):
    tn = 768
    return pl.pallas_call(
        _inproj_kernel,
        grid=(N_TOK // TM, PROJ // tn),
        in_specs=[pl.BlockSpec((TM, D), lambda i, j: (i, 0)),
                  pl.BlockSpec((1, D), lambda i, j: (0, 0)),
                  _mod_spec(1, TM), _mod_spec(0, TM),
                  pl.BlockSpec((None, D, tn), lambda i, j: (l, 0, j))],
        out_specs=pl.BlockSpec((TM, tn), lambda i, j: (i, j)),
        out_shape=jax.ShapeDtypeStruct((N_TOK, PROJ), F32),
        scratch_shapes=[pltpu.VMEM((TM, D), BF16)],
        compiler_params=_cparams(("arbitrary", "arbitrary")),
    )(x, g, mod6, mod6, w_in)


def _lam(lq1, lk1, lq2, lk2, lam_init):
    return (jnp.exp(jnp.sum(lq1[...] * lk1[...], axis=-1, keepdims=True))
            - jnp.exp(jnp.sum(lq2[...] * lk2[...], axis=-1, keepdims=True)) + lam_init)


def _softmax_parts(s):
    m = jnp.max(s, axis=-1, keepdims=True)
    e = jnp.exp(s - m)
    return e, jnp.sum(e, axis=-1, keepdims=True)


def _diff_head(q1, q2, k1, k2, v, lam, g, lam_init):
    scale = HEAD_DIM ** -0.5
    e1, l1 = _softmax_parts(_dot_nt(q1, k1) * scale)
    e2, l2 = _softmax_parts(_dot_nt(q2, k2) * scale)
    a = e1 * (1.0 / l1) - (lam * (1.0 / l2)) * e2
    o = _dot(a.astype(BF16), v)
    o = o * lax.rsqrt(jnp.mean(o * o, axis=-1, keepdims=True) + EPS)
    return (o * g) * (1.0 - lam_init)


def _prompt_attn_kernel(pa_ref, pb_ref, pc_ref, lq1, lk1, lq2, lk2, g_ref, o_ref, *, lam_init):
    lam = _lam(lq1, lk1, lq2, lk2, lam_init)
    g = g_ref[...]
    scale = HEAD_DIM ** -0.5

    def col(c0, w):
        ref = (pa_ref, pb_ref, pc_ref)[c0 // 768]
        o = c0 % 768
        return ref[:, o:o + w].astype(BF16)

    for h in range(NA_HEADS):
        q = col(C_NAQ + 64 * h, 64)
        k = col(C_NAK + 64 * h, 64)
        v = col(C_NAV + 64 * h, 64)
        e, l = _softmax_parts(_dot_nt(q, k) * scale)
        o = _dot(e.astype(BF16), v) * (1.0 / l)
        o_ref[:, 64 * h:64 * h + 64] = o.astype(BF16)
    for h in range(DIFF_HEADS):
        q1 = col(C_DQ + 128 * h, 64)
        q2 = col(C_DQ + 128 * h + 64, 64)
        k1 = col(C_DK + 128 * h, 64)
        k2 = col(C_DK + 128 * h + 64, 64)
        v = col(C_DV + 128 * h, 128)
        o = _diff_head(q1, q2, k1, k2, v, lam, g, lam_init)
        o_ref[:, 256 + 128 * h:384 + 128 * h] = o.astype(BF16)


def _prompt_attn(p, lq1, lk1, lq2, lk2, g, lam_init):
    vec = lambda n: pl.BlockSpec((1, n), lambda b: (0, 0))
    return pl.pallas_call(
        functools.partial(_prompt_attn_kernel, lam_init=lam_init),
        grid=(BATCH,),
        in_specs=[pl.BlockSpec((SEQ, 768), lambda b: (b, 0)),
                  pl.BlockSpec((SEQ, 768), lambda b: (b, 1)),
                  pl.BlockSpec((SEQ, 768), lambda b: (b, 2)),
                  vec(64), vec(64), vec(64), vec(64), vec(128)],
        out_specs=pl.BlockSpec((SEQ, 768), lambda b: (b, 0)),
        out_shape=jax.ShapeDtypeStruct((NP_TOK, 768), BF16),
        compiler_params=_cparams(("arbitrary",)),
    )(p, p, p, lq1, lk1, lq2, lk2, g)


def _bias_kernel(rpb_ref, o_ref):
    h = pl.program_id(0)
    case = pl.program_id(1)
    qc = lax.broadcasted_iota(jnp.int32, (GRID_W, GRID_W), 0)
    kc = lax.broadcasted_iota(jnp.int32, (GRID_W, GRID_W), 1)
    delta = jnp.clip(kc - qc + (WIN_COLS - 1), 0, 2 * WIN_COLS - 2)
    qs = jnp.clip(qc - WIN_COLS // 2, 0, GRID_W - WIN_COLS)
    in_win = (kc >= qs) & (kc < qs + WIN_COLS)
    for i in range(WIN_ROWS):
        dr = i - case + (WIN_ROWS - 1)
        base = (h * (2 * WIN_ROWS - 1) + dr) * (2 * WIN_COLS - 1)
        acc = jnp.zeros((GRID_W, GRID_W), F32)
        for d in range(2 * WIN_COLS - 1):
            acc = jnp.where(delta == d, rpb_ref[base + d], acc)
        o_ref[:, i * GRID_W:(i + 1) * GRID_W] = jnp.where(in_win, acc, -jnp.inf)


def _na_bias(rpb):
    return pl.pallas_call(
        _bias_kernel,
        grid=(NA_HEADS, WIN_ROWS),
        in_specs=[pl.BlockSpec(memory_space=pltpu.SMEM)],
        out_specs=pl.BlockSpec((None, None, GRID_W, WIN_ROWS * GRID_W), lambda h, c: (h, c, 0, 0)),
        out_shape=jax.ShapeDtypeStruct((NA_HEADS, WIN_ROWS, GRID_W, WIN_ROWS * GRID_W), F32),
        compiler_params=_cparams(("arbitrary", "arbitrary")),
    )(rpb.reshape(-1))


def _na_kernel(q_ref, kv_ref, kc_ref, vc_ref, bias_ref, o_ref):
    r = pl.program_id(1)
    start = jnp.clip(r - WIN_ROWS // 2, 0, GRID_ROWS - WIN_ROWS)
    case = r - start
    row0 = pl.multiple_of(start * GRID_W, GRID_W)
    nloc = WIN_ROWS * GRID_W
    scale = HEAD_DIM ** -0.5
    for h in range(NA_HEADS):
        q = q_ref[:, C_NAQ + 64 * h:C_NAQ + 64 * h + 64].astype(BF16)
        k = kv_ref[pl.ds(row0, nloc), C_NAK + 64 * h:C_NAK + 64 * h + 64].astype(BF16)
        v = kv_ref[pl.ds(row0, nloc), C_NAV + 64 * h:C_NAV + 64 * h + 64].astype(BF16)
        s_loc = _dot_nt(q, k) * scale + bias_ref[h, pl.ds(case, 1)][0]
        s_ctx = _dot_nt(q, kc_ref[h].astype(BF16)) * scale
        m = jnp.maximum(jnp.max(s_loc, axis=-1, keepdims=True), jnp.max(s_ctx, axis=-1, keepdims=True))
        e_loc = jnp.exp(s_loc - m)
        e_ctx = jnp.exp(s_ctx - m)
        l = jnp.sum(e_loc, axis=-1, keepdims=True) + jnp.sum(e_ctx, axis=-1, keepdims=True)
        o = (_dot(e_loc.astype(BF16), v) + _dot(e_ctx.astype(BF16), vc_ref[h].astype(BF16))) * (1.0 / l)
        o_ref[:, 64 * h:64 * h + 64] = o.astype(BF16)


def _na_attn(p, cache_k, cache_v, bias, l):
    qblk0 = NP_TOK // GRID_W
    kvblk0 = NP_TOK // DEC_SEQ
    return pl.pallas_call(
        _na_kernel,
        grid=(DEC_BATCH, GRID_ROWS),
        in_specs=[pl.BlockSpec((GRID_W, 768), lambda b, r: (qblk0 + b * GRID_ROWS + r, 0)),
                  pl.BlockSpec((DEC_SEQ, 768), lambda b, r: (kvblk0 + b, 0)),
                  pl.BlockSpec((None, None, NA_HEADS, PAST, HEAD_DIM), lambda b, r: (b, l, 0, 0, 0)),
                  pl.BlockSpec((None, None, NA_HEADS, PAST, HEAD_DIM), lambda b, r: (b, l, 0, 0, 0)),
                  pl.BlockSpec((NA_HEADS, WIN_ROWS, GRID_W, WIN_ROWS * GRID_W), lambda b, r: (0, 0, 0, 0))],
        out_specs=pl.BlockSpec((GRID_W, 256), lambda b, r: (b * GRID_ROWS + r, 0)),
        out_shape=jax.ShapeDtypeStruct((NS_TOK, 256), BF16),
        compiler_params=_cparams(("arbitrary", "arbitrary")),
    )(p, p, cache_k, cache_v, bias)


@functools.lru_cache(None)
def _rope_tables():
    t = np.arange(DEC_SEQ)
    lane = np.arange(128)
    dd = lane % HEAD_DIM
    pos = np.where(dd[None, :] < 32, (t // GRID_W)[:, None], (t % GRID_W)[:, None]).astype(np.float64)
    inv = ROPE_BASE ** (-(dd % 16).astype(np.float64) * 2.0 / 32.0)
    ang = pos * inv[None, :]
    first = (dd % 32) < 16
    cos = np.cos(ang)
    s_up = np.where(first[None, :], -np.sin(ang), 0.0)
    s_dn = np.where(first[None, :], 0.0, np.sin(ang))
    return tuple(np.asarray(a, np.float32) for a in (cos, s_up, s_dn))


def _rope(x, cos, s_up, s_dn):
    return x * cos + pltpu.roll(x, 112, axis=1) * s_up + pltpu.roll(x, 16, axis=1) * s_dn


def _sdiff_kernel(q_ref, k_ref, v_ref, ck_ref, cv_ref, cos_ref, sup_ref, sdn_ref,
                  lq1, lk1, lq2, lk2, g_ref, o_ref, k1_scr, k2_scr, v_scr, *, lam_init, tq):
    qb = pl.program_id(2)

    @pl.when(qb == 0)
    def _():
        kr = _rope(k_ref[...], cos_ref[...], sup_ref[...], sdn_ref[...])
        k1_scr[0:DEC_SEQ, :] = kr[:, :64].astype(BF16)
        k2_scr[0:DEC_SEQ, :] = kr[:, 64:].astype(BF16)
        k1_scr[DEC_SEQ:, :] = ck_ref[0].astype(BF16)
        k2_scr[DEC_SEQ:, :] = ck_ref[1].astype(BF16)
        v_scr[0:DEC_SEQ, :] = v_ref[...].astype(BF16)
        v_scr[DEC_SEQ:, :] = cv_ref[...].astype(BF16)

    r0 = pl.multiple_of(qb * tq, tq)
    rows = pl.ds(r0, tq)
    qr = _rope(q_ref[...], cos_ref[rows, :], sup_ref[rows, :], sdn_ref[rows, :])
    lam = _lam(lq1, lk1, lq2, lk2, lam_init)
    o = _diff_head(qr[:, :64].astype(BF16), qr[:, 64:].astype(BF16), k1_scr[...], k2_scr[...],
                   v_scr[...], lam, g_ref[...], lam_init)
    o_ref[...] = o.astype(BF16)


def _sdiff_attn(p, cache_k, cache_v, lq1, lk1, lq2, lk2, g, lam_init, l):
    tq = 256
    nq = DEC_SEQ // tq
    cos, s_up, s_dn = (jnp.asarray(a) for a in _rope_tables())
    vec = lambda n: pl.BlockSpec((1, n), lambda b, h, q: (0, 0))
    tab = pl.BlockSpec((DEC_SEQ, 128), lambda b, h, q: (0, 0))
    kvblk0 = NP_TOK // DEC_SEQ
    return pl.pallas_call(
        functools.partial(_sdiff_kernel, lam_init=lam_init, tq=tq),
        grid=(DEC_BATCH, DIFF_HEADS, nq),
        in_specs=[pl.BlockSpec((tq, 128), lambda b, h, q: (NP_TOK // tq + b * nq + q, C_DQ // 128 + h)),
                  pl.BlockSpec((DEC_SEQ, 128), lambda b, h, q: (kvblk0 + b, C_DK // 128 + h)),
                  pl.BlockSpec((DEC_SEQ, 128), lambda b, h, q: (kvblk0 + b, C_DV // 128 + h)),
                  pl.BlockSpec((None, None, None, 2, PAST, HEAD_DIM), lambda b, h, q: (b, l, h, 0, 0, 0)),
                  pl.BlockSpec((None, None, None, PAST, DIFF_V), lambda b, h, q: (b, l, h, 0, 0)),
                  tab, tab, tab, vec(64), vec(64), vec(64), vec(64), vec(128)],
        out_specs=pl.BlockSpec((tq, 128), lambda b, h, q: (b * nq + q, h)),
        out_shape=jax.ShapeDtypeStruct((NS_TOK, DIFF_HEADS * DIFF_V), BF16),
        scratch_shapes=[pltpu.VMEM((DEC_SEQ + PAST, HEAD_DIM), BF16),
                        pltpu.VMEM((DEC_SEQ + PAST, HEAD_DIM), BF16),
                        pltpu.VMEM((DEC_SEQ + PAST, DIFF_V), BF16)],
        compiler_params=_cparams(("arbitrary", "arbitrary", "arbitrary")),
    )(p, p, p, cache_k, cache_v, cos, s_up, s_dn, lq1, lk1, lq2, lk2, g)


@functools.lru_cache(None)
def _dft_consts(L):
    n = 2 * L
    k = np.arange(L)
    ang = 2.0 * np.pi * ((k[:, None] * k[None, :]) % n) / n
    alt = (-1.0) ** k
    fa = np.cos(ang)
    fb = -np.sin(ang)
    fb[0, :] = alt
    wgt = np.full((L,), 2.0 / n)
    wgt[0] = 1.0 / n
    ga = fa * wgt[:, None]
    gb = fb * wgt[:, None]
    gb[0, :] = alt / n
    f = np.concatenate([fa, fb], axis=0)
    g = np.concatenate([ga.T, gb.T], axis=1)
    return np.asarray(f, dtype=BF16), np.asarray(g, dtype=BF16)


@functools.lru_cache(None)
def _filter_consts(L):
    f32 = np.float32
    t = np.linspace(0.0, 1.0, L, dtype=f32)[:, None]
    pos = np.arange(L, dtype=f32)[:, None]
    bands = np.linspace(1e-4, HY_BANDS - 1, HY_BANDS, dtype=f32)[None, :]
    ang = f32(2.0 * math.pi) * bands * pos / f32(L)
    z = np.zeros((L, 128), f32)
    z[:, 0:1] = t
    z[:, 1:1 + HY_BANDS] = np.cos(ang)
    z[:, 1 + HY_BANDS:HY_EMB] = -np.sin(ang)
    min_decay = math.log(1e-2) / 1.5
    max_decay = math.log(1e-2) / 0.3
    deltas = np.abs(np.linspace(min_decay, max_decay, HY_W, dtype=f32))
    decay = np.exp(-t * deltas[None, :]).astype(f32)
    return z, decay


def _spectra_kernel(z_ref, w1_ref, b1_ref, w2_ref, b2_ref, fr_ref, w3_ref, dec_ref, fa_ref, fb_ref,
                    sa_ref, sb_ref, filt_scr, *, L, kc):
    j = pl.program_id(0)

    @pl.when(j == 0)
    def _():
        fr = fr_ref[...]
        hdn = jnp.sin(fr * (_dot3(z_ref[...], w1_ref[...]) + b1_ref[...]))
        hdn = jnp.sin(fr * (_dot3(hdn, w2_ref[...]) + b2_ref[...]))
        dec = dec_ref[...]
        not_first = lax.broadcasted_iota(jnp.int32, (L, HY_W), 0) > 0
        for o in range(2):
            hf = _dot3(hdn, w3_ref[:, 512 * o:512 * o + 256]) * dec
            hb = jnp.where(not_first, _dot3(hdn, w3_ref[:, 512 * o + 256:512 * o + 512]) * dec, 0.0)
            nrm = (jnp.sum(jnp.abs(hf), axis=0, keepdims=True)
                   + jnp.sum(jnp.abs(hb), axis=0, keepdims=True))
            filt_scr[:, 512 * o:512 * o + 256] = (hf / nrm).astype(BF16)
            filt_scr[:, 512 * o + 256:512 * o + 512] = (hb / nrm).astype(BF16)

    ta = _dot(fa_ref[...], filt_scr[...])
    tb = _dot(fb_ref[...], filt_scr[...])
    first = (lax.broadcasted_iota(jnp.int32, (kc, HY_W), 0) + j * kc) == 0
    for o in range(2):
        af, ab = ta[:, 512 * o:512 * o + 256], ta[:, 512 * o + 256:512 * o + 512]
        bf, bb = tb[:, 512 * o:512 * o + 256], tb[:, 512 * o + 256:512 * o + 512]
        sa_ref[:, 256 * o:256 * o + 256] = af + ab
        sb_ref[:, 256 * o:256 * o + 256] = jnp.where(first, bf + bb, bf - bb)


def _hy_spectra(L, kc, w1p, b1, w2, b2, fr, w3):
    z, decay = _filter_consts(L)
    f = jnp.asarray(_dft_consts(L)[0])
    nj = L // kc
    full = lambda shape: pl.BlockSpec(shape, lambda j: tuple(0 for _ in shape))
    return pl.pallas_call(
        functools.partial(_spectra_kernel, L=L, kc=kc),
        grid=(nj,),
        in_specs=[full((L, 128)), full((128, HY_HID)), full((1, HY_HID)), full((HY_HID, HY_HID)),
                  full((1, HY_HID)), full((1, HY_HID)), full((HY_HID, 4 * HY_W)), full((L, HY_W)),
                  pl.BlockSpec((kc, L), lambda j: (j, 0)),
                  pl.BlockSpec((kc, L), lambda j: (j + nj, 0))],
        out_specs=[pl.BlockSpec((kc, 2 * HY_W), lambda j: (j, 0)),
                   pl.BlockSpec((kc, 2 * HY_W), lambda j: (j, 0))],
        out_shape=[jax.ShapeDtypeStruct((L, 2 * HY_W), F32)] * 2,
        scratch_shapes=[pltpu.VMEM((L, 4 * HY_W), BF16)],
        compiler_params=_cparams(("arbitrary",)),
    )(jnp.asarray(z), w1p, b1, w2, b2, fr, w3, jnp.asarray(decay), f, f)


def _hyconv_kernel(*refs, L, nb, kc, nj):
    u_refs = refs[:nb]
    (cw_ref, cb_ref, d_ref, fa_ref, fb_ref, ga_ref, gb_ref, sa_ref, sb_ref, o_ref,
     xin_scr, x_scr, g1_scr, g2_scr, y_scr) = refs[nb:]
    o = pl.program_id(1)
    j = pl.program_id(2)

    @pl.when((o == 0) & (j == 0))
    def _():
        row = lax.broadcasted_iota(jnp.int32, (L, HY_W), 0)
        for i in range(nb):
            cols = slice(HY_W * i, HY_W * (i + 1))
            for part, dst in enumerate((x_scr, g1_scr, g2_scr)):
                pc = slice(HY_W * part, HY_W * (part + 1))
                u = u_refs[i][:, pc]
                up = jnp.where(row == 0, 0.0, pltpu.roll(u, 1, axis=0))
                un = jnp.where(row == L - 1, 0.0, pltpu.roll(u, L - 1, axis=0))
                dst[:, cols] = (up * cw_ref[0:1, pc] + u * cw_ref[1:2, pc] + un * cw_ref[2:3, pc]
                                + cb_ref[:, pc])
            xin_scr[:, cols] = x_scr[:, cols].astype(BF16)

    @pl.when(j == 0)
    def _():
        y_scr[...] = jnp.zeros_like(y_scr)

    xa = _dot(fa_ref[...], xin_scr[...])
    xb = _dot(fb_ref[...], xin_scr[...])
    sa = jnp.concatenate([sa_ref[...]] * nb, axis=1)
    sb = jnp.concatenate([sb_ref[...]] * nb, axis=1)
    first = (lax.broadcasted_iota(jnp.int32, (kc, nb * HY_W), 0) + j * kc) == 0
    ya = jnp.where(first, xa * sa, xa * sa - xb * sb)
    yb = jnp.where(first, xb * sb, xa * sb + xb * sa)
    y_scr[...] += _dot(ga_ref[...], ya.astype(BF16)) + _dot(gb_ref[...], yb.astype(BF16))

    @pl.when(j == nj - 1)
    def _():
        dvec = jnp.concatenate([d_ref[pl.ds(o, 1), :]] * nb, axis=1)
        y = y_scr[...] + x_scr[...] * dvec

        @pl.when(o == 0)
        def _():
            zz = g1_scr[...] * y
            x_scr[...] = zz
            xin_scr[...] = zz.astype(BF16)

        @pl.when(o == 1)
        def _():
            res = g2_scr[...] * y
            for i in range(nb):
                o_ref[L * i:L * (i + 1), :] = res[:, HY_W * i:HY_W * (i + 1)].astype(BF16)


def _hy_conv(p, L, nb, kc, row_blk0, n_seq, conv_w, conv_b, d, sa, sb):
    f, g = (jnp.asarray(a) for a in _dft_consts(L))
    nj = L // kc
    u_spec = lambda i: pl.BlockSpec((L, 3 * HY_W), lambda bg, o, j: (row_blk0 + bg * nb + i, C_HY // 768),
                                    pipeline_mode=pl.Buffered(1))
    small = lambda shape: pl.BlockSpec(shape, lambda bg, o, j: (0, 0))
    return pl.pallas_call(
        functools.partial(_hyconv_kernel, L=L, nb=nb, kc=kc, nj=nj),
        grid=(n_seq // nb, 2, nj),
        in_specs=[u_spec(i) for i in range(nb)] + [
            small((3, 3 * HY_W)), small((1, 3 * HY_W)), small((2, HY_W)),
            pl.BlockSpec((kc, L), lambda bg, o, j: (j, 0)),
            pl.BlockSpec((kc, L), lambda bg, o, j: (j + nj, 0)),
            pl.BlockSpec((L, kc), lambda bg, o, j: (0, j)),
            pl.BlockSpec((L, kc), lambda bg, o, j: (0, j + nj)),
            pl.BlockSpec((kc, HY_W), lambda bg, o, j: (j, o)),
            pl.BlockSpec((kc, HY_W), lambda bg, o, j: (j, o))],
        out_specs=pl.BlockSpec((nb * L, HY_W), lambda bg, o, j: (bg, 0)),
        out_shape=jax.ShapeDtypeStruct((n_seq * L, HY_W), BF16),
        scratch_shapes=[pltpu.VMEM((L, nb * HY_W), BF16)] + [pltpu.VMEM((L, nb * HY_W), F32)] * 4,
        compiler_params=_cparams(("arbitrary", "arbitrary", "arbitrary")),
    )(*([p] * nb), conv_w, conv_b, d, f, f, g, g, sa, sb)


def _outproj_kernel(x_ref, mix_ref, g1_ref, w_ref, o_ref):
    o_ref[...] = x_ref[...] + g1_ref[...] * _dot(mix_ref[...], w_ref[...].astype(BF16))


def _outproj(x, mix, mod6, w_out, l):
    return pl.pallas_call(
        _outproj_kernel,
        grid=(N_TOK // TM,),
        in_specs=[pl.BlockSpec((TM, D), lambda i: (i, 0)),
                  pl.BlockSpec((TM, D), lambda i: (i, 0)),
                  _mod_spec(2, TM),
                  pl.BlockSpec((None, D, D), lambda i: (l, 0, 0))],
        out_specs=pl.BlockSpec((TM, D), lambda i: (i, 0)),
        out_shape=jax.ShapeDtypeStruct((N_TOK, D), F32),
        compiler_params=_cparams(("arbitrary",)),
    )(x, mix, mod6, w_out)


def _ffn_kernel(x_ref, g_ref, sc_ref, sh_ref, gate_ref, wg_ref, wu_ref, wd_ref, o_ref, h_scr, *, nj):
    j = pl.program_id(1)

    @pl.when(j == 0)
    def _():
        h_scr[...] = _norm_mod(x_ref[...], g_ref[...], sc_ref[...], sh_ref[...]).astype(BF16)
        o_ref[...] = jnp.zeros_like(o_ref)

    h = h_scr[...]
    a = _silu(_dot(h, wg_ref[...].astype(BF16))) * _dot(h, wu_ref[...].astype(BF16))
    o_ref[...] += _dot(a.astype(BF16), wd_ref[...].astype(BF16))

    @pl.when(j == nj - 1)
    def _():
        o_ref[...] = x_ref[...] + gate_ref[...] * o_ref[...]


def _dense_ffn(x, g, mod6, wg, wu, wd, i_ffn):
    tf = 256
    nj = FFN // tf
    return pl.pallas_call(
        functools.partial(_ffn_kernel, nj=nj),
        grid=(N_TOK // TM, nj),
        in_specs=[pl.BlockSpec((TM, D), lambda i, j: (i, 0)),
                  pl.BlockSpec((1, D), lambda i, j: (0, 0)),
                  _mod_spec(4, TM), _mod_spec(3, TM), _mod_spec(5, TM),
                  pl.BlockSpec((None, D, tf), lambda i, j: (i_ffn, 0, j)),
                  pl.BlockSpec((None, D, tf), lambda i, j: (i_ffn, 0, j)),
                  pl.BlockSpec((None, tf, D), lambda i, j: (i_ffn, j, 0))],
        out_specs=pl.BlockSpec((TM, D), lambda i, j: (i, 0)),
        out_shape=jax.ShapeDtypeStruct((N_TOK, D), F32),
        scratch_shapes=[pltpu.VMEM((TM, D), BF16)],
        compiler_params=_cparams(("arbitrary", "arbitrary")),
    )(x, g, mod6, mod6, mod6, wg, wu, wd)


def _router_kernel(x_ref, g_ref, sc_ref, sh_ref, wr_ref, h_ref, r_ref):
    h = _norm_mod(x_ref[...], g_ref[...], sc_ref[...], sh_ref[...])
    h_ref[...] = h
    lane = lax.broadcasted_iota(jnp.int32, (TM, 128), 1)
    lg = jnp.where(lane < N_EXP, _dot3(h, wr_ref[...]), -jnp.inf)
    m1 = jnp.max(lg, axis=-1, keepdims=True)
    i1 = jnp.min(jnp.where(lg == m1, lane, 128), axis=-1, keepdims=True)
    lg2 = jnp.where(lane == i1, -jnp.inf, lg)
    m2 = jnp.max(lg2, axis=-1, keepdims=True)
    i2 = jnp.min(jnp.where(lg2 == m2, lane, 128), axis=-1, keepdims=True)
    e = jnp.exp(m2 - m1)
    w1 = 1.0 / (1.0 + e)
    w2 = e / (1.0 + e)
    r_ref[...] = jnp.where(lane == 0, i1.astype(F32),
                           jnp.where(lane == 1, i2.astype(F32),
                                     jnp.where(lane == 2, w1, jnp.where(lane == 3, w2, 0.0))))


def _router(x, g, mod6, wr_pad):
    return pl.pallas_call(
        _router_kernel,
        grid=(N_TOK // TM,),
        in_specs=[pl.BlockSpec((TM, D), lambda i: (i, 0)),
                  pl.BlockSpec((1, D), lambda i: (0, 0)),
                  _mod_spec(4, TM), _mod_spec(3, TM),
                  pl.BlockSpec((D, 128), lambda i: (0, 0))],
        out_specs=[pl.BlockSpec((TM, D), lambda i: (i, 0)),
                   pl.BlockSpec((TM, 128), lambda i: (i, 0))],
        out_shape=[jax.ShapeDtypeStruct((N_TOK, D), F32), jax.ShapeDtypeStruct((N_TOK, 128), F32)],
        compiler_params=_cparams(("arbitrary",)),
    )(x, g, mod6, mod6, wr_pad)


ROWCOPY_CHUNK = 128


def _rowcopy_kernel(sidx_ref, didx_ref, src_ref, init_ref, dst_ref, sem, *, n):
    del init_ref

    def row_copy(a):
        return pltpu.make_async_copy(src_ref.at[pl.ds(sidx_ref[a], 1)],
                                     dst_ref.at[pl.ds(didx_ref[a], 1)], sem)

    def issue(c):
        def body(u, carry):
            row_copy(c * ROWCOPY_CHUNK + u).start()
            return carry
        lax.fori_loop(0, ROWCOPY_CHUNK, body, 0)

    def drain(c):
        def body(u, carry):
            row_copy(c * ROWCOPY_CHUNK + u).wait()
            return carry
        lax.fori_loop(0, ROWCOPY_CHUNK, body, 0)

    nchunk = n // ROWCOPY_CHUNK
    issue(0)

    def step(c, carry):
        issue(c)
        drain(c - 1)
        return carry
    lax.fori_loop(1, nchunk, step, 0)
    drain(nchunk - 1)


def _rowcopy(sidx, didx, src, init):
    n = sidx.shape[0]
    return pl.pallas_call(
        functools.partial(_rowcopy_kernel, n=n),
        grid_spec=pltpu.PrefetchScalarGridSpec(
            num_scalar_prefetch=2,
            grid=(1,),
            in_specs=[pl.BlockSpec(memory_space=pl.ANY), pl.BlockSpec(memory_space=pl.ANY)],
            out_specs=pl.BlockSpec(memory_space=pl.ANY),
            scratch_shapes=[pltpu.SemaphoreType.DMA(())]),
        out_shape=jax.ShapeDtypeStruct(init.shape, init.dtype),
        input_output_aliases={3: 0},
        compiler_params=pltpu.CompilerParams(dimension_semantics=("arbitrary",), has_side_effects=True),
    )(sidx, didx, src, init)


def _experts_kernel(te_ref, used_ref, xs_ref, wg_ref, wu_ref, wd_ref, o_ref, xb_scr):
    i = pl.program_id(0)
    j = pl.program_id(1)

    @pl.when(i < used_ref[0])
    def _():
        @pl.when(j == 0)
        def _():
            xb_scr[...] = xs_ref[...].astype(BF16)
            o_ref[...] = jnp.zeros_like(o_ref)

        xb = xb_scr[...]
        a = _silu(_dot(xb, wg_ref[...].astype(BF16))) * _dot(xb, wu_ref[...].astype(BF16))
        o_ref[...] += _dot(a.astype(BF16), wd_ref[...].astype(BF16))


def _experts(tile_expert, used, xs, wg, wu, wd, i_moe):
    nj = EXP_DIM // MOE_TF

    def tile(i, te, used):
        return jnp.minimum(i, used[0] - 1)

    def chunk(i, j, used):
        return jnp.where(i < used[0], j, nj - 1)

    return pl.pallas_call(
        _experts_kernel,
        grid_spec=pltpu.PrefetchScalarGridSpec(
            num_scalar_prefetch=2,
            grid=(MOE_TILES, nj),
            in_specs=[pl.BlockSpec((MOE_TM, D), lambda i, j, te, used: (tile(i, te, used), 0)),
                      pl.BlockSpec((None, None, D, MOE_TF),
                                   lambda i, j, te, used: (i_moe, te[tile(i, te, used)], 0, chunk(i, j, used))),
                      pl.BlockSpec((None, None, D, MOE_TF),
                                   lambda i, j, te, used: (i_moe, te[tile(i, te, used)], 0, chunk(i, j, used))),
                      pl.BlockSpec((None, None, MOE_TF, D),
                                   lambda i, j, te, used: (i_moe, te[tile(i, te, used)], chunk(i, j, used), 0))],
            out_specs=pl.BlockSpec((MOE_TM, D), lambda i, j, te, used: (tile(i, te, used), 0)),
            scratch_shapes=[pltpu.VMEM((MOE_TM, D), BF16)]),
        out_shape=jax.ShapeDtypeStruct((MOE_ROWS, D), F32),
        compiler_params=_cparams(("arbitrary", "arbitrary")),
    )(tile_expert, used, xs, wg, wu, wd)


def _combine_kernel(x_ref, y0_ref, y1_ref, r_ref, gate_ref, fg_ref, o_ref, *, final):
    r = r_ref[...]
    y = r[:, 2:3] * y0_ref[...] + r[:, 3:4] * y1_ref[...]
    x = x_ref[...] + gate_ref[...] * y
    if final:
        x = (x * lax.rsqrt(jnp.mean(x * x, axis=-1, keepdims=True) + EPS)) * fg_ref[...]
    o_ref[...] = x


def _combine(x, y01, r, mod6, final_g, final):
    nt = N_TOK // TM
    return pl.pallas_call(
        functools.partial(_combine_kernel, final=final),
        grid=(nt,),
        in_specs=[pl.BlockSpec((TM, D), lambda i: (i, 0)),
                  pl.BlockSpec((TM, D), lambda i: (i, 0)),
                  pl.BlockSpec((TM, D), lambda i: (i + nt, 0)),
                  pl.BlockSpec((TM, 128), lambda i: (i, 0)),
                  _mod_spec(5, TM),
                  pl.BlockSpec((1, D), lambda i: (0, 0))],
        out_specs=pl.BlockSpec((TM, D), lambda i: (i, 0)),
        out_shape=jax.ShapeDtypeStruct((N_TOK, D), F32),
        compiler_params=_cparams(("arbitrary",)),
    )(x, y01, y01, r, mod6, final_g)


def _moe(x, g, mod6, router, wg, wu, wd, i_moe, final_g, final):
    wr_pad = jnp.pad(router, ((0, 0), (0, 128 - N_EXP)))
    h, r = _router(x, g, mod6, wr_pad)

    i12 = r[:, 0:2].astype(jnp.int32)
    mask = jnp.sum((i12[:, :, None] == jnp.arange(N_EXP)[None, None, :]).astype(jnp.int32), axis=1)
    csum = jnp.cumsum(mask, axis=0)
    counts = csum[-1]
    padded = ((counts + MOE_TM - 1) // MOE_TM) * MOE_TM
    ends = jnp.cumsum(padded)
    starts = ends - padded
    rowpos = starts[i12] + jnp.take_along_axis(csum, i12, axis=1) - 1
    tile_expert = jnp.minimum(
        jnp.searchsorted(ends, jnp.arange(MOE_TILES, dtype=jnp.int32) * MOE_TM, side="right"),
        N_EXP - 1).astype(jnp.int32)
    used = (ends[-1:] // MOE_TM).astype(jnp.int32)

    tok = jnp.arange(N_TOK, dtype=jnp.int32)
    rows_km = rowpos.T.reshape(-1).astype(jnp.int32)
    xs = _rowcopy(jnp.concatenate([tok, tok]), rows_km, h, jnp.zeros((MOE_ROWS, D), F32))
    ys = _experts(tile_expert, used, xs, wg, wu, wd, i_moe)
    y01 = _rowcopy(rows_km, jnp.arange(2 * N_TOK, dtype=jnp.int32), ys, jnp.zeros((2 * N_TOK, D), F32))
    return _combine(x, y01, r, mod6, final_g, final)


def _final_norm_kernel(x_ref, g_ref, o_ref):
    x = x_ref[...]
    o_ref[...] = (x * lax.rsqrt(jnp.mean(x * x, axis=-1, keepdims=True) + EPS)) * g_ref[...]


def _final_norm(x, g):
    return pl.pallas_call(
        _final_norm_kernel,
        grid=(N_TOK // TM,),
        in_specs=[pl.BlockSpec((TM, D), lambda i: (i, 0)), pl.BlockSpec((1, D), lambda i: (0, 0))],
        out_specs=pl.BlockSpec((TM, D), lambda i: (i, 0)),
        out_shape=jax.ShapeDtypeStruct((N_TOK, D), F32),
        compiler_params=_cparams(("arbitrary",)),
    )(x, g)


def _cache_leaves(p):
    pp = p[:NP_TOK].reshape(BATCH, SEQ, PROJ)
    na = lambda a: a.reshape(BATCH, SEQ, NA_HEADS, HEAD_DIM).transpose(0, 2, 1, 3)
    dk = pp[:, :, C_DK:C_DV].reshape(BATCH, SEQ, DIFF_HEADS, 2, HEAD_DIM).transpose(0, 2, 3, 1, 4)
    dv = pp[:, :, C_DV:C_HY].reshape(BATCH, SEQ, DIFF_HEADS, DIFF_V).transpose(0, 2, 1, 3)
    return na(pp[:, :, C_NAK:C_NAV]), na(pp[:, :, C_NAV:C_DQ]), dk, dv


def kernel(x_prompt, x_sample, cache_na_k, cache_na_v, cache_diff_k, cache_diff_v, c, c_ctx, w_in, w_out, ada_w, ada_b, norm_mix_g, norm_ffn_g, na_rpb, diff_lq1, diff_lk1, diff_lq2, diff_lk2, diff_subln_g, hy_conv_w, hy_conv_b, hy_d, hy_f_w1, hy_f_b1, hy_f_w2, hy_f_b2, hy_f_freq, hy_f_w3, ffn_w_gate, ffn_w_up, ffn_w_down, moe_router, moe_w_gate, moe_w_up, moe_w_down, final_norm_g):
    x = jnp.concatenate([x_prompt.reshape(NP_TOK, D), x_sample.reshape(NS_TOK, D)], axis=0)
    cond8 = jnp.concatenate([c_ctx[None, :], c, jnp.zeros((5, D), F32)], axis=0)
    mods = _modulation(cond8, ada_w, ada_b)
    final_g = final_norm_g.reshape(1, D)

    leaves = []
    for l in range(DEPTH):
        lam_init = 0.8 - 0.6 * math.exp(-0.3 * l)
        mod6 = mods[l].reshape(8, 6, D).transpose(1, 0, 2).reshape(6, 8, 1, D)
        row = lambda a: a[l].reshape(1, -1)
        lq1, lk1, lq2, lk2, subg = row(diff_lq1), row(diff_lk1), row(diff_lq2), row(diff_lk2), row(diff_subln_g)

        p = _inproj(x, row(norm_mix_g), mod6, w_in, l)
        leaves.append(_cache_leaves(p))

        mix_p = _prompt_attn(p, lq1, lk1, lq2, lk2, subg, lam_init)
        bias = _na_bias(na_rpb[l])
        na_s = _na_attn(p, cache_na_k, cache_na_v, bias, l)
        d_s = _sdiff_attn(p, cache_diff_k, cache_diff_v, lq1, lk1, lq2, lk2, subg, lam_init, l)

        w1p = jnp.pad(hy_f_w1[l], ((0, 128 - HY_EMB), (0, 0)))
        fargs = (w1p, row(hy_f_b1), hy_f_w2[l], row(hy_f_b2), row(hy_f_freq), hy_f_w3[l])
        cargs = (hy_conv_w[l], row(hy_conv_b), hy_d[l])
        sa_p, sb_p = _hy_spectra(SEQ, SEQ, *fargs)
        hy_p = _hy_conv(p, SEQ, 2, SEQ, 0, BATCH, *cargs, sa_p, sb_p)
        sa_s, sb_s = _hy_spectra(DEC_SEQ, 512, *fargs)
        hy_s = _hy_conv(p, DEC_SEQ, 1, 512, NP_TOK // DEC_SEQ, DEC_BATCH, *cargs, sa_s, sb_s)

        mix = jnp.concatenate([jnp.concatenate([mix_p, hy_p], axis=1),
                               jnp.concatenate([na_s, d_s, hy_s], axis=1)], axis=0)
        x = _outproj(x, mix, mod6, w_out, l)

        if l % 2 == 0:
            x = _dense_ffn(x, row(norm_ffn_g), mod6, ffn_w_gate, ffn_w_up, ffn_w_down, l // 2)
            if l == DEPTH - 1:
                x = _final_norm(x, final_g)
        else:
            x = _moe(x, row(norm_ffn_g), mod6, moe_router[l // 2], moe_w_gate, moe_w_up, moe_w_down,
                     l // 2, final_g, l == DEPTH - 1)

    y_prompt = x[:NP_TOK].reshape(BATCH, SEQ, D)
    y_sample = x[NP_TOK:].reshape(DEC_BATCH, DEC_SEQ, D)
    stack = lambda k: jnp.stack([leaves[l][k] for l in range(DEPTH)], axis=1)
    return (y_prompt, y_sample, stack(0), stack(1), stack(2), stack(3))
```

```python
import functools
import math

import numpy as np
import jax
import jax.numpy as jnp
from jax import lax
from jax.experimental import pallas as pl
from jax.experimental.pallas import tpu as pltpu

F32 = jnp.float32
BF16 = jnp.bfloat16

D = 1024
BATCH, SEQ = 32, 256
DEC_BATCH, DEC_SEQ = 2, 2048
DEPTH = 2
PAST = 512
GRID_W = 64
GRID_ROWS = DEC_SEQ // GRID_W
HEAD_DIM = 64
NA_HEADS = 4
DIFF_HEADS = 4
DIFF_V = 128
WIN_ROWS, WIN_COLS = 8, 16
HY_W = 256
HY_EMB = 33
HY_BANDS = 16
HY_HID = 64
PROJ = 3072
FFN = 2816
N_EXP = 8
EXP_DIM = 3584
EPS = 1e-6
ROPE_BASE = 10000.0

NP_TOK = BATCH * SEQ
NS_TOK = DEC_BATCH * DEC_SEQ
N_TOK = NP_TOK + NS_TOK

C_NAQ, C_NAK, C_NAV = 0, 256, 512
C_DQ, C_DK, C_DV = 768, 1280, 1792
C_HY = 2304

TM = 1024
VMEM_LIMIT = 56 * 1024 * 1024

MOE_TM = 1024
MOE_TF = 512
MOE_ROWS = 2 * N_TOK + N_EXP * MOE_TM
MOE_TS = 256
MOE_TB = 512
MOE_ALIGN = 16
MOE_WIN = MOE_TB + MOE_ALIGN
MOE_YROWS = MOE_ROWS + MOE_TM


def _cparams(sem):
    return pltpu.CompilerParams(dimension_semantics=sem, vmem_limit_bytes=VMEM_LIMIT)


def _dot(a, b):
    return jnp.dot(a, b, preferred_element_type=F32)


def _dot_nt(a, b):
    return lax.dot_general(a, b, (((1,), (1,)), ((), ())), preferred_element_type=F32)


def _split(a):
    hi = a.astype(BF16)
    lo = (a - hi.astype(F32)).astype(BF16)
    return hi, lo


def _dot3(a, b):
    ah, al = _split(a)
    bh, bl = _split(b)
    return _dot(ah, bh) + (_dot(ah, bl) + _dot(al, bh))


def _silu(x):
    return x / (1.0 + jnp.exp(-x))


def _mod_row(i, tm):
    t = i * tm
    return jnp.where(t < NP_TOK, 0, 1 + (t - NP_TOK) // DEC_SEQ)


def _mod_spec(k, tm):
    return pl.BlockSpec((None, None, 1, D), lambda i, *_: (k, _mod_row(i, tm), 0, 0))


def _norm_mod(x, g, sc, sh):
    y = x * lax.rsqrt(jnp.mean(x * x, axis=-1, keepdims=True) + EPS)
    return (y * g) * (1.0 + sc) + sh


def _mod_kernel(c_ref, w_ref, b_ref, o_ref):
    o_ref[...] = _dot3(_silu(c_ref[...]), w_ref[...]) + b_ref[...]


def _modulation(cond8, ada_w, ada_b):
    tn = 1536
    return pl.pallas_call(
        _mod_kernel,
        grid=(DEPTH, 6 * D // tn),
        in_specs=[pl.BlockSpec((8, D), lambda l, j: (0, 0)),
                  pl.BlockSpec((None, D, tn), lambda l, j: (l, 0, j)),
                  pl.BlockSpec((None, 1, tn), lambda l, j: (l, 0, j))],
        out_specs=pl.BlockSpec((None, 8, tn), lambda l, j: (l, 0, j)),
        out_shape=jax.ShapeDtypeStruct((DEPTH, 8, 6 * D), F32),
        compiler_params=_cparams(("arbitrary", "arbitrary")),
    )(cond8, ada_w, ada_b.reshape(DEPTH, 1, 6 * D))


def _inproj_kernel(x_ref, g_ref, sc_ref, sh_ref, w_ref, o_ref, h_scr):
    @pl.when(pl.program_id(1) == 0)
    def _():
        h_scr[...] = _norm_mod(x_ref[...], g_ref[...], sc_ref[...], sh_ref[...]).astype(BF16)

    o_ref[...] = _dot(h_scr[...], w_ref[...].astype(BF16))


def _inproj(x, g, mod6, w_in, l):
    tn = 768
    return pl.pallas_call(
        _inproj_kernel,
        grid=(N_TOK // TM, PROJ // tn),
        in_specs=[pl.BlockSpec((TM, D), lambda i, j: (i, 0)),
                  pl.BlockSpec((1, D), lambda i, j: (0, 0)),
                  _mod_spec(1, TM), _mod_spec(0, TM),
                  pl.BlockSpec((None, D, tn), lambda i, j: (l, 0, j))],
        out_specs=pl.BlockSpec((TM, tn), lambda i, j: (i, j)),
        out_shape=jax.ShapeDtypeStruct((N_TOK, PROJ), F32),
        scratch_shapes=[pltpu.VMEM((TM, D), BF16)],
        compiler_params=_cparams(("arbitrary", "arbitrary")),
    )(x, g, mod6, mod6, w_in)


def _lam(lq1, lk1, lq2, lk2, lam_init):
    return (jnp.exp(jnp.sum(lq1[...] * lk1[...], axis=-1, keepdims=True))
            - jnp.exp(jnp.sum(lq2[...] * lk2[...], axis=-1, keepdims=True)) + lam_init)


def _softmax_parts(s):
    m = jnp.max(s, axis=-1, keepdims=True)
    e = jnp.exp(s - m)
    return e, jnp.sum(e, axis=-1, keepdims=True)


def _diff_head(q1, q2, k1, k2, v, lam, g, lam_init):
    scale = HEAD_DIM ** -0.5
    e1, l1 = _softmax_parts(_dot_nt(q1, k1) * scale)
    e2, l2 = _softmax_parts(_dot_nt(q2, k2) * scale)
    a = e1 * (1.0 / l1) - (lam * (1.0 / l2)) * e2
    o = _dot(a.astype(BF16), v)
    o = o * lax.rsqrt(jnp.mean(o * o, axis=-1, keepdims=True) + EPS)
    return (o * g) * (1.0 - lam_init)


def _prompt_attn_kernel(pa_ref, pb_ref, pc_ref, lq1, lk1, lq2, lk2, g_ref, o_ref, *, lam_init):
    lam = _lam(lq1, lk1, lq2, lk2, lam_init)
    g = g_ref[...]
    scale = HEAD_DIM ** -0.5

    def col(c0, w):
        ref = (pa_ref, pb_ref, pc_ref)[c0 // 768]
        o = c0 % 768
        return ref[:, o:o + w].astype(BF16)

    for h in range(NA_HEADS):
        q = col(C_NAQ + 64 * h, 64)
        k = col(C_NAK + 64 * h, 64)
        v = col(C_NAV + 64 * h, 64)
        e, l = _softmax_parts(_dot_nt(q, k) * scale)
        o = _dot(e.astype(BF16), v) * (1.0 / l)
        o_ref[:, 64 * h:64 * h + 64] = o.astype(BF16)
    for h in range(DIFF_HEADS):
        q1 = col(C_DQ + 128 * h, 64)
        q2 = col(C_DQ + 128 * h + 64, 64)
        k1 = col(C_DK + 128 * h, 64)
        k2 = col(C_DK + 128 * h + 64, 64)
        v = col(C_DV + 128 * h, 128)
        o = _diff_head(q1, q2, k1, k2, v, lam, g, lam_init)
        o_ref[:, 256 + 128 * h:384 + 128 * h] = o.astype(BF16)


def _prompt_attn(p, lq1, lk1, lq2, lk2, g, lam_init):
    vec = lambda n: pl.BlockSpec((1, n), lambda b: (0, 0))
    return pl.pallas_call(
        functools.partial(_prompt_attn_kernel, lam_init=lam_init),
        grid=(BATCH,),
        in_specs=[pl.BlockSpec((SEQ, 768), lambda b: (b, 0)),
                  pl.BlockSpec((SEQ, 768), lambda b: (b, 1)),
                  pl.BlockSpec((SEQ, 768), lambda b: (b, 2)),
                  vec(64), vec(64), vec(64), vec(64), vec(128)],
        out_specs=pl.BlockSpec((SEQ, 768), lambda b: (b, 0)),
        out_shape=jax.ShapeDtypeStruct((NP_TOK, 768), BF16),
        compiler_params=_cparams(("arbitrary",)),
    )(p, p, p, lq1, lk1, lq2, lk2, g)


def _bias_kernel(rpb_ref, o_ref):
    h = pl.program_id(0)
    case = pl.program_id(1)
    qc = lax.broadcasted_iota(jnp.int32, (GRID_W, GRID_W), 0)
    kc = lax.broadcasted_iota(jnp.int32, (GRID_W, GRID_W), 1)
    delta = jnp.clip(kc - qc + (WIN_COLS - 1), 0, 2 * WIN_COLS - 2)
    qs = jnp.clip(qc - WIN_COLS // 2, 0, GRID_W - WIN_COLS)
    in_win = (kc >= qs) & (kc < qs + WIN_COLS)
    for i in range(WIN_ROWS):
        dr = i - case + (WIN_ROWS - 1)
        base = (h * (2 * WIN_ROWS - 1) + dr) * (2 * WIN_COLS - 1)
        acc = jnp.zeros((GRID_W, GRID_W), F32)
        for d in range(2 * WIN_COLS - 1):
            acc = jnp.where(delta == d, rpb_ref[base + d], acc)
        o_ref[:, i * GRID_W:(i + 1) * GRID_W] = jnp.where(in_win, acc, -jnp.inf)


def _na_bias(rpb):
    return pl.pallas_call(
        _bias_kernel,
        grid=(NA_HEADS, WIN_ROWS),
        in_specs=[pl.BlockSpec(memory_space=pltpu.SMEM)],
        out_specs=pl.BlockSpec((None, None, GRID_W, WIN_ROWS * GRID_W), lambda h, c: (h, c, 0, 0)),
        out_shape=jax.ShapeDtypeStruct((NA_HEADS, WIN_ROWS, GRID_W, WIN_ROWS * GRID_W), F32),
        compiler_params=_cparams(("arbitrary", "arbitrary")),
    )(rpb.reshape(-1))


def _na_kernel(q_ref, kv_ref, kc_ref, vc_ref, bias_ref, o_ref):
    r = pl.program_id(1)
    start = jnp.clip(r - WIN_ROWS // 2, 0, GRID_ROWS - WIN_ROWS)
    case = r - start
    row0 = pl.multiple_of(start * GRID_W, GRID_W)
    nloc = WIN_ROWS * GRID_W
    scale = HEAD_DIM ** -0.5
    for h in range(NA_HEADS):
        q = q_ref[:, C_NAQ + 64 * h:C_NAQ + 64 * h + 64].astype(BF16)
        k = kv_ref[pl.ds(row0, nloc), C_NAK + 64 * h:C_NAK + 64 * h + 64].astype(BF16)
        v = kv_ref[pl.ds(row0, nloc), C_NAV + 64 * h:C_NAV + 64 * h + 64].astype(BF16)
        s_loc = _dot_nt(q, k) * scale + bias_ref[h, pl.ds(case, 1)][0]
        s_ctx = _dot_nt(q, kc_ref[h].astype(BF16)) * scale
        m = jnp.maximum(jnp.max(s_loc, axis=-1, keepdims=True), jnp.max(s_ctx, axis=-1, keepdims=True))
        e_loc = jnp.exp(s_loc - m)
        e_ctx = jnp.exp(s_ctx - m)
        l = jnp.sum(e_loc, axis=-1, keepdims=True) + jnp.sum(e_ctx, axis=-1, keepdims=True)
        o = (_dot(e_loc.astype(BF16), v) + _dot(e_ctx.astype(BF16), vc_ref[h].astype(BF16))) * (1.0 / l)
        o_ref[:, 64 * h:64 * h + 64] = o.astype(BF16)


def _na_attn(p, cache_k, cache_v, bias, l):
    qblk0 = NP_TOK // GRID_W
    kvblk0 = NP_TOK // DEC_SEQ
    return pl.pallas_call(
        _na_kernel,
        grid=(DEC_BATCH, GRID_ROWS),
        in_specs=[pl.BlockSpec((GRID_W, 768), lambda b, r: (qblk0 + b * GRID_ROWS + r, 0)),
                  pl.BlockSpec((DEC_SEQ, 768), lambda b, r: (kvblk0 + b, 0)),
                  pl.BlockSpec((None, None, NA_HEADS, PAST, HEAD_DIM), lambda b, r: (b, l, 0, 0, 0)),
                  pl.BlockSpec((None, None, NA_HEADS, PAST, HEAD_DIM), lambda b, r: (b, l, 0, 0, 0)),
                  pl.BlockSpec((NA_HEADS, WIN_ROWS, GRID_W, WIN_ROWS * GRID_W), lambda b, r: (0, 0, 0, 0))],
        out_specs=pl.BlockSpec((GRID_W, 256), lambda b, r: (b * GRID_ROWS + r, 0)),
        out_shape=jax.ShapeDtypeStruct((NS_TOK, 256), BF16),
        compiler_params=_cparams(("arbitrary", "arbitrary")),
    )(p, p, cache_k, cache_v, bias)


@functools.lru_cache(None)
def _rope_tables():
    t = np.arange(DEC_SEQ)
    lane = np.arange(128)
    dd = lane % HEAD_DIM
    pos = np.where(dd[None, :] < 32, (t // GRID_W)[:, None], (t % GRID_W)[:, None]).astype(np.float64)
    inv = ROPE_BASE ** (-(dd % 16).astype(np.float64) * 2.0 / 32.0)
    ang = pos * inv[None, :]
    first = (dd % 32) < 16
    cos = np.cos(ang)
    s_up = np.where(first[None, :], -np.sin(ang), 0.0)
    s_dn = np.where(first[None, :], 0.0, np.sin(ang))
    return tuple(np.asarray(a, np.float32) for a in (cos, s_up, s_dn))


def _rope(x, cos, s_up, s_dn):
    return x * cos + pltpu.roll(x, 112, axis=1) * s_up + pltpu.roll(x, 16, axis=1) * s_dn


def _sdiff_kernel(q_ref, k_ref, v_ref, ck_ref, cv_ref, cos_ref, sup_ref, sdn_ref,
                  lq1, lk1, lq2, lk2, g_ref, o_ref, k1_scr, k2_scr, v_scr, *, lam_init, tq):
    qb = pl.program_id(2)

    @pl.when(qb == 0)
    def _():
        kr = _rope(k_ref[...], cos_ref[...], sup_ref[...], sdn_ref[...])
        k1_scr[0:DEC_SEQ, :] = kr[:, :64].astype(BF16)
        k2_scr[0:DEC_SEQ, :] = kr[:, 64:].astype(BF16)
        k1_scr[DEC_SEQ:, :] = ck_ref[0].astype(BF16)
        k2_scr[DEC_SEQ:, :] = ck_ref[1].astype(BF16)
        v_scr[0:DEC_SEQ, :] = v_ref[...].astype(BF16)
        v_scr[DEC_SEQ:, :] = cv_ref[...].astype(BF16)

    r0 = pl.multiple_of(qb * tq, tq)
    rows = pl.ds(r0, tq)
    qr = _rope(q_ref[...], cos_ref[rows, :], sup_ref[rows, :], sdn_ref[rows, :])
    lam = _lam(lq1, lk1, lq2, lk2, lam_init)
    o = _diff_head(qr[:, :64].astype(BF16), qr[:, 64:].astype(BF16), k1_scr[...], k2_scr[...],
                   v_scr[...], lam, g_ref[...], lam_init)
    o_ref[...] = o.astype(BF16)


def _sdiff_attn(p, cache_k, cache_v, lq1, lk1, lq2, lk2, g, lam_init, l):
    tq = 256
    nq = DEC_SEQ // tq
    cos, s_up, s_dn = (jnp.asarray(a) for a in _rope_tables())
    vec = lambda n: pl.BlockSpec((1, n), lambda b, h, q: (0, 0))
    tab = pl.BlockSpec((DEC_SEQ, 128), lambda b, h, q: (0, 0))
    kvblk0 = NP_TOK // DEC_SEQ
    return pl.pallas_call(
        functools.partial(_sdiff_kernel, lam_init=lam_init, tq=tq),
        grid=(DEC_BATCH, DIFF_HEADS, nq),
        in_specs=[pl.BlockSpec((tq, 128), lambda b, h, q: (NP_TOK // tq + b * nq + q, C_DQ // 128 + h)),
                  pl.BlockSpec((DEC_SEQ, 128), lambda b, h, q: (kvblk0 + b, C_DK // 128 + h)),
                  pl.BlockSpec((DEC_SEQ, 128), lambda b, h, q: (kvblk0 + b, C_DV // 128 + h)),
                  pl.BlockSpec((None, None, None, 2, PAST, HEAD_DIM), lambda b, h, q: (b, l, h, 0, 0, 0)),
                  pl.BlockSpec((None, None, None, PAST, DIFF_V), lambda b, h, q: (b, l, h, 0, 0)),
                  tab, tab, tab, vec(64), vec(64), vec(64), vec(64), vec(128)],
        out_specs=pl.BlockSpec((tq, 128), lambda b, h, q: (b * nq + q, h)),
        out_shape=jax.ShapeDtypeStruct((NS_TOK, DIFF_HEADS * DIFF_V), BF16),
        scratch_shapes=[pltpu.VMEM((DEC_SEQ + PAST, HEAD_DIM), BF16),
                        pltpu.VMEM((DEC_SEQ + PAST, HEAD_DIM), BF16),
                        pltpu.VMEM((DEC_SEQ + PAST, DIFF_V), BF16)],
        compiler_params=_cparams(("arbitrary", "arbitrary", "arbitrary")),
    )(p, p, p, cache_k, cache_v, cos, s_up, s_dn, lq1, lk1, lq2, lk2, g)


@functools.lru_cache(None)
def _dft_consts(L):
    n = 2 * L
    k = np.arange(L)
    ang = 2.0 * np.pi * ((k[:, None] * k[None, :]) % n) / n
    alt = (-1.0) ** k
    fa = np.cos(ang)
    fb = -np.sin(ang)
    fb[0, :] = alt
    wgt = np.full((L,), 2.0 / n)
    wgt[0] = 1.0 / n
    ga = fa * wgt[:, None]
    gb = fb * wgt[:, None]
    gb[0, :] = alt / n
    f = np.concatenate([fa, fb], axis=0)
    g = np.concatenate([ga.T, gb.T], axis=1)
    return np.asarray(f, dtype=BF16), np.asarray(g, dtype=BF16)


@functools.lru_cache(None)
def _filter_consts(L):
    f32 = np.float32
    t = np.linspace(0.0, 1.0, L, dtype=f32)[:, None]
    pos = np.arange(L, dtype=f32)[:, None]
    bands = np.linspace(1e-4, HY_BANDS - 1, HY_BANDS, dtype=f32)[None, :]
    ang = f32(2.0 * math.pi) * bands * pos / f32(L)
    z = np.zeros((L, 128), f32)
    z[:, 0:1] = t
    z[:, 1:1 + HY_BANDS] = np.cos(ang)
    z[:, 1 + HY_BANDS:HY_EMB] = -np.sin(ang)
    min_decay = math.log(1e-2) / 1.5
    max_decay = math.log(1e-2) / 0.3
    deltas = np.abs(np.linspace(min_decay, max_decay, HY_W, dtype=f32))
    decay = np.exp(-t * deltas[None, :]).astype(f32)
    return z, decay


def _spectra_kernel(z_ref, w1_ref, b1_ref, w2_ref, b2_ref, fr_ref, w3_ref, dec_ref, fa_ref, fb_ref,
                    sa_ref, sb_ref, filt_scr, *, L, kc):
    j = pl.program_id(0)

    @pl.when(j == 0)
    def _():
        fr = fr_ref[...]
        hdn = jnp.sin(fr * (_dot3(z_ref[...], w1_ref[...]) + b1_ref[...]))
        hdn = jnp.sin(fr * (_dot3(hdn, w2_ref[...]) + b2_ref[...]))
        dec = dec_ref[...]
        not_first = lax.broadcasted_iota(jnp.int32, (L, HY_W), 0) > 0
        for o in range(2):
            hf = _dot3(hdn, w3_ref[:, 512 * o:512 * o + 256]) * dec
            hb = jnp.where(not_first, _dot3(hdn, w3_ref[:, 512 * o + 256:512 * o + 512]) * dec, 0.0)
            nrm = (jnp.sum(jnp.abs(hf), axis=0, keepdims=True)
                   + jnp.sum(jnp.abs(hb), axis=0, keepdims=True))
            filt_scr[:, 512 * o:512 * o + 256] = (hf / nrm).astype(BF16)
            filt_scr[:, 512 * o + 256:512 * o + 512] = (hb / nrm).astype(BF16)

    ta = _dot(fa_ref[...], filt_scr[...])
    tb = _dot(fb_ref[...], filt_scr[...])
    first = (lax.broadcasted_iota(jnp.int32, (kc, HY_W), 0) + j * kc) == 0
    for o in range(2):
        af, ab = ta[:, 512 * o:512 * o + 256], ta[:, 512 * o + 256:512 * o + 512]
        bf, bb = tb[:, 512 * o:512 * o + 256], tb[:, 512 * o + 256:512 * o + 512]
        sa_ref[:, 256 * o:256 * o + 256] = af + ab
        sb_ref[:, 256 * o:256 * o + 256] = jnp.where(first, bf + bb, bf - bb)


def _hy_spectra(L, kc, w1p, b1, w2, b2, fr, w3):
    z, decay = _filter_consts(L)
    f = jnp.asarray(_dft_consts(L)[0])
    nj = L // kc
    full = lambda shape: pl.BlockSpec(shape, lambda j: tuple(0 for _ in shape))
    return pl.pallas_call(
        functools.partial(_spectra_kernel, L=L, kc=kc),
        grid=(nj,),
        in_specs=[full((L, 128)), full((128, HY_HID)), full((1, HY_HID)), full((HY_HID, HY_HID)),
                  full((1, HY_HID)), full((1, HY_HID)), full((HY_HID, 4 * HY_W)), full((L, HY_W)),
                  pl.BlockSpec((kc, L), lambda j: (j, 0)),
                  pl.BlockSpec((kc, L), lambda j: (j + nj, 0))],
        out_specs=[pl.BlockSpec((kc, 2 * HY_W), lambda j: (j, 0)),
                   pl.BlockSpec((kc, 2 * HY_W), lambda j: (j, 0))],
        out_shape=[jax.ShapeDtypeStruct((L, 2 * HY_W), F32)] * 2,
        scratch_shapes=[pltpu.VMEM((L, 4 * HY_W), BF16)],
        compiler_params=_cparams(("arbitrary",)),
    )(jnp.asarray(z), w1p, b1, w2, b2, fr, w3, jnp.asarray(decay), f, f)


def _hyconv_kernel(*refs, L, nb, kc, nj):
    u_refs = refs[:nb]
    (cw_ref, cb_ref, d_ref, fa_ref, fb_ref, ga_ref, gb_ref, sa_ref, sb_ref, o_ref,
     xin_scr, x_scr, g1_scr, g2_scr, y_scr) = refs[nb:]
    o = pl.program_id(1)
    j = pl.program_id(2)

    @pl.when((o == 0) & (j == 0))
    def _():
        row = lax.broadcasted_iota(jnp.int32, (L, HY_W), 0)
        for i in range(nb):
            cols = slice(HY_W * i, HY_W * (i + 1))
            for part, dst in enumerate((x_scr, g1_scr, g2_scr)):
                pc = slice(HY_W * part, HY_W * (part + 1))
                u = u_refs[i][:, pc]
                up = jnp.where(row == 0, 0.0, pltpu.roll(u, 1, axis=0))
                un = jnp.where(row == L - 1, 0.0, pltpu.roll(u, L - 1, axis=0))
                dst[:, cols] = (up * cw_ref[0:1, pc] + u * cw_ref[1:2, pc] + un * cw_ref[2:3, pc]
                                + cb_ref[:, pc])
            xin_scr[:, cols] = x_scr[:, cols].astype(BF16)

    @pl.when(j == 0)
    def _():
        y_scr[...] = jnp.zeros_like(y_scr)

    xa = _dot(fa_ref[...], xin_scr[...])
    xb = _dot(fb_ref[...], xin_scr[...])
    sa = jnp.concatenate([sa_ref[...]] * nb, axis=1)
    sb = jnp.concatenate([sb_ref[...]] * nb, axis=1)
    first = (lax.broadcasted_iota(jnp.int32, (kc, nb * HY_W), 0) + j * kc) == 0
    ya = jnp.where(first, xa * sa, xa * sa - xb * sb)
    yb = jnp.where(first, xb * sb, xa * sb + xb * sa)
    y_scr[...] += _dot(ga_ref[...], ya.astype(BF16)) + _dot(gb_ref[...], yb.astype(BF16))

    @pl.when(j == nj - 1)
    def _():
        dvec = jnp.concatenate([d_ref[pl.ds(o, 1), :]] * nb, axis=1)
        y = y_scr[...] + x_scr[...] * dvec

        @pl.when(o == 0)
        def _():
            zz = g1_scr[...] * y
            x_scr[...] = zz
            xin_scr[...] = zz.astype(BF16)

        @pl.when(o == 1)
        def _():
            res = g2_scr[...] * y
            for i in range(nb):
                o_ref[L * i:L * (i + 1), :] = res[:, HY_W * i:HY_W * (i + 1)].astype(BF16)


def _hy_conv(p, L, nb, kc, row_blk0, n_seq, conv_w, conv_b, d, sa, sb):
    f, g = (jnp.asarray(a) for a in _dft_consts(L))
    nj = L // kc
    u_spec = lambda i: pl.BlockSpec((L, 3 * HY_W), lambda bg, o, j: (row_blk0 + bg * nb + i, C_HY // 768),
                                    pipeline_mode=pl.Buffered(1))
    small = lambda shape: pl.BlockSpec(shape, lambda bg, o, j: (0, 0))
    return pl.pallas_call(
        functools.partial(_hyconv_kernel, L=L, nb=nb, kc=kc, nj=nj),
        grid=(n_seq // nb, 2, nj),
        in_specs=[u_spec(i) for i in range(nb)] + [
            small((3, 3 * HY_W)), small((1, 3 * HY_W)), small((2, HY_W)),
            pl.BlockSpec((kc, L), lambda bg, o, j: (j, 0)),
            pl.BlockSpec((kc, L), lambda bg, o, j: (j + nj, 0)),
            pl.BlockSpec((L, kc), lambda bg, o, j: (0, j)),
            pl.BlockSpec((L, kc), lambda bg, o, j: (0, j + nj)),
            pl.BlockSpec((kc, HY_W), lambda bg, o, j: (j, o)),
            pl.BlockSpec((kc, HY_W), lambda bg, o, j: (j, o))],
        out_specs=pl.BlockSpec((nb * L, HY_W), lambda bg, o, j: (bg, 0)),
        out_shape=jax.ShapeDtypeStruct((n_seq * L, HY_W), BF16),
        scratch_shapes=[pltpu.VMEM((L, nb * HY_W), BF16)] + [pltpu.VMEM((L, nb * HY_W), F32)] * 4,
        compiler_params=_cparams(("arbitrary", "arbitrary", "arbitrary")),
    )(*([p] * nb), conv_w, conv_b, d, f, f, g, g, sa, sb)


def _outproj_kernel(x_ref, mix_ref, g1_ref, w_ref, o_ref):
    o_ref[...] = x_ref[...] + g1_ref[...] * _dot(mix_ref[...], w_ref[...].astype(BF16))


def _outproj(x, mix, mod6, w_out, l):
    return pl.pallas_call(
        _outproj_kernel,
        grid=(N_TOK // TM,),
        in_specs=[pl.BlockSpec((TM, D), lambda i: (i, 0)),
                  pl.BlockSpec((TM, D), lambda i: (i, 0)),
                  _mod_spec(2, TM),
                  pl.BlockSpec((None, D, D), lambda i: (l, 0, 0))],
        out_specs=pl.BlockSpec((TM, D), lambda i: (i, 0)),
        out_shape=jax.ShapeDtypeStruct((N_TOK, D), F32),
        compiler_params=_cparams(("arbitrary",)),
    )(x, mix, mod6, w_out)


def _ffn_kernel(x_ref, g_ref, sc_ref, sh_ref, gate_ref, wg_ref, wu_ref, wd_ref, o_ref, h_scr, *, nj):
    j = pl.program_id(1)

    @pl.when(j == 0)
    def _():
        h_scr[...] = _norm_mod(x_ref[...], g_ref[...], sc_ref[...], sh_ref[...]).astype(BF16)
        o_ref[...] = jnp.zeros_like(o_ref)

    h = h_scr[...]
    a = _silu(_dot(h, wg_ref[...].astype(BF16))) * _dot(h, wu_ref[...].astype(BF16))
    o_ref[...] += _dot(a.astype(BF16), wd_ref[...].astype(BF16))

    @pl.when(j == nj - 1)
    def _():
        o_ref[...] = x_ref[...] + gate_ref[...] * o_ref[...]


def _dense_ffn(x, g, mod6, wg, wu, wd, i_ffn):
    tf = 256
    nj = FFN // tf
    return pl.pallas_call(
        functools.partial(_ffn_kernel, nj=nj),
        grid=(N_TOK // TM, nj),
        in_specs=[pl.BlockSpec((TM, D), lambda i, j: (i, 0)),
                  pl.BlockSpec((1, D), lambda i, j: (0, 0)),
                  _mod_spec(4, TM), _mod_spec(3, TM), _mod_spec(5, TM),
                  pl.BlockSpec((None, D, tf), lambda i, j: (i_ffn, 0, j)),
                  pl.BlockSpec((None, D, tf), lambda i, j: (i_ffn, 0, j)),
                  pl.BlockSpec((None, tf, D), lambda i, j: (i_ffn, j, 0))],
        out_specs=pl.BlockSpec((TM, D), lambda i, j: (i, 0)),
        out_shape=jax.ShapeDtypeStruct((N_TOK, D), F32),
        scratch_shapes=[pltpu.VMEM((TM, D), BF16)],
        compiler_params=_cparams(("arbitrary", "arbitrary")),
    )(x, g, mod6, mod6, mod6, wg, wu, wd)


def _router_kernel(x_ref, g_ref, sc_ref, sh_ref, wr_ref, h_ref, r_ref):
    h = _norm_mod(x_ref[...], g_ref[...], sc_ref[...], sh_ref[...])
    h_ref[...] = h.astype(BF16)
    lane = lax.broadcasted_iota(jnp.int32, (TM, 128), 1)
    lg = jnp.where(lane < N_EXP, _dot3(h, wr_ref[...]), -jnp.inf)
    m1 = jnp.max(lg, axis=-1, keepdims=True)
    i1 = jnp.min(jnp.where(lg == m1, lane, 128), axis=-1, keepdims=True)
    lg2 = jnp.where(lane == i1, -jnp.inf, lg)
    m2 = jnp.max(lg2, axis=-1, keepdims=True)
    i2 = jnp.min(jnp.where(lg2 == m2, lane, 128), axis=-1, keepdims=True)
    e = jnp.exp(m2 - m1)
    w1 = 1.0 / (1.0 + e)
    w2 = e / (1.0 + e)
    r_ref[...] = jnp.where(lane == 0, i1.astype(F32),
                           jnp.where(lane == 1, i2.astype(F32),
                                     jnp.where(lane == 2, w1, jnp.where(lane == 3, w2, 0.0))))


def _router(x, g, mod6, wr_pad):
    return pl.pallas_call(
        _router_kernel,
        grid=(N_TOK // TM,),
        in_specs=[pl.BlockSpec((TM, D), lambda i: (i, 0)),
                  pl.BlockSpec((1, D), lambda i: (0, 0)),
                  _mod_spec(4, TM), _mod_spec(3, TM),
                  pl.BlockSpec((D, 128), lambda i: (0, 0))],
        out_specs=[pl.BlockSpec((TM, D), lambda i: (i, 0)),
                   pl.BlockSpec((TM, 128), lambda i: (i, 0))],
        out_shape=[jax.ShapeDtypeStruct((N_TOK, D), BF16), jax.ShapeDtypeStruct((N_TOK, 128), F32)],
        compiler_params=_cparams(("arbitrary",)),
    )(x, g, mod6, mod6, wr_pad)


def _dispatch_kernel(blo_ref, bhi_ref, sexp_ref, h_hbm, rp_ref, o_ref, h_scr, sem):
    s = pl.program_id(0)

    @pl.when(s == 0)
    def _():
        cp = pltpu.make_async_copy(h_hbm, h_scr, sem)
        cp.start()
        cp.wait()

    e = sexp_ref[s]
    rows = s * MOE_TS + lax.broadcasted_iota(jnp.int32, (MOE_TS, MOE_TB), 0)

    def body(b, acc):
        sel = (rows == rp_ref[e, pl.ds(b, 1), :]).astype(BF16)
        hb = h_scr[pl.ds(pl.multiple_of(b * MOE_TB, MOE_TB), MOE_TB), :]
        return acc + _dot(sel, hb)

    acc = lax.fori_loop(blo_ref[s], bhi_ref[s] + 1, body, jnp.zeros((MOE_TS, D), F32))
    o_ref[...] = acc.astype(BF16)


def _dispatch(blo, bhi, sexp, h, rp_t):
    return pl.pallas_call(
        _dispatch_kernel,
        grid_spec=pltpu.PrefetchScalarGridSpec(
            num_scalar_prefetch=3,
            grid=(MOE_ROWS // MOE_TS,),
            in_specs=[pl.BlockSpec(memory_space=pl.ANY),
                      pl.BlockSpec((N_EXP, N_TOK // MOE_TB, MOE_TB), lambda s, *_: (0, 0, 0))],
            out_specs=pl.BlockSpec((MOE_TS, D), lambda s, *_: (s, 0)),
            scratch_shapes=[pltpu.VMEM((N_TOK, D), BF16), pltpu.SemaphoreType.DMA(())]),
        out_shape=jax.ShapeDtypeStruct((MOE_ROWS, D), BF16),
        compiler_params=_cparams(("arbitrary",)),
    )(blo, bhi, sexp, h, rp_t)


def _experts_kernel(te_ref, used_ref, xs_ref, wg_ref, wu_ref, wd_ref, o_ref, acc_scr, *, nj):
    i = pl.program_id(0)
    j = pl.program_id(1)
    live = i < used_ref[0]

    @pl.when(live)
    def _():
        @pl.when(j == 0)
        def _():
            acc_scr[...] = jnp.zeros_like(acc_scr)

        xb = xs_ref[...]
        a = _silu(_dot(xb, wg_ref[...].astype(BF16))) * _dot(xb, wu_ref[...].astype(BF16))
        acc_scr[...] += _dot(a.astype(BF16), wd_ref[...].astype(BF16))

    @pl.when(j == nj - 1)
    def _():
        o_ref[...] = jnp.where(live, acc_scr[...], 0.0).astype(BF16)


def _experts(tile_expert, used, xs, wg, wu, wd, i_moe):
    nj = EXP_DIM // MOE_TF

    def tile(i, used):
        return jnp.minimum(i, used[0] - 1)

    def chunk(i, j, used):
        return jnp.where(i < used[0], j, nj - 1)

    return pl.pallas_call(
        functools.partial(_experts_kernel, nj=nj),
        grid_spec=pltpu.PrefetchScalarGridSpec(
            num_scalar_prefetch=2,
            grid=(MOE_YROWS // MOE_TM, nj),
            in_specs=[pl.BlockSpec((MOE_TM, D), lambda i, j, te, used: (tile(i, used), 0)),
                      pl.BlockSpec((None, None, D, MOE_TF),
                                   lambda i, j, te, used: (i_moe, te[tile(i, used)], 0, chunk(i, j, used))),
                      pl.BlockSpec((None, None, D, MOE_TF),
                                   lambda i, j, te, used: (i_moe, te[tile(i, used)], 0, chunk(i, j, used))),
                      pl.BlockSpec((None, None, MOE_TF, D),
                                   lambda i, j, te, used: (i_moe, te[tile(i, used)], chunk(i, j, used), 0))],
            out_specs=pl.BlockSpec((MOE_TM, D), lambda i, j, te, used: (i, 0)),
            scratch_shapes=[pltpu.VMEM((MOE_TM, D), F32)]),
        out_shape=jax.ShapeDtypeStruct((MOE_YROWS, D), BF16),
        compiler_params=_cparams(("arbitrary", "arbitrary")),
    )(tile_expert, used, xs, wg, wu, wd)


def _combine_kernel(ws_ref, x_ref, rp_ref, comb_ref, gate_ref, fg_ref, ys_hbm, o_ref, win_scr, sem, *, final):
    b = pl.program_id(0)
    nb = pl.num_programs(0)
    slot = b % 2

    def win_copy(blk, sl, e):
        start = pl.multiple_of(ws_ref[blk * N_EXP + e], MOE_ALIGN)
        return pltpu.make_async_copy(ys_hbm.at[pl.ds(start, MOE_WIN)], win_scr.at[sl, e], sem.at[sl, e])

    @pl.when(b == 0)
    def _():
        for e in range(N_EXP):
            win_copy(0, 0, e).start()

    @pl.when(b + 1 < nb)
    def _():
        for e in range(N_EXP):
            win_copy(b + 1, 1 - slot, e).start()

    col = lax.broadcasted_iota(jnp.int32, (MOE_TB, MOE_WIN), 1)
    y = jnp.zeros((MOE_TB, D), F32)
    for e in range(N_EXP):
        win_copy(b, slot, e).wait()
        sel = (rp_ref[:, e:e + 1] - ws_ref[b * N_EXP + e] == col).astype(BF16)
        y = y + comb_ref[:, e:e + 1] * _dot(sel, win_scr[slot, e])
    x = x_ref[...] + gate_ref[...] * y
    if final:
        x = (x * lax.rsqrt(jnp.mean(x * x, axis=-1, keepdims=True) + EPS)) * fg_ref[...]
    o_ref[...] = x


def _combine(ws, x, rp8, comb, mod6, final_g, ys, final):
    return pl.pallas_call(
        functools.partial(_combine_kernel, final=final),
        grid_spec=pltpu.PrefetchScalarGridSpec(
            num_scalar_prefetch=1,
            grid=(N_TOK // MOE_TB,),
            in_specs=[pl.BlockSpec((MOE_TB, D), lambda b, ws: (b, 0)),
                      pl.BlockSpec((MOE_TB, N_EXP), lambda b, ws: (b, 0)),
                      pl.BlockSpec((MOE_TB, N_EXP), lambda b, ws: (b, 0)),
                      _mod_spec(5, MOE_TB),
                      pl.BlockSpec((1, D), lambda b, ws: (0, 0)),
                      pl.BlockSpec(memory_space=pl.ANY)],
            out_specs=pl.BlockSpec((MOE_TB, D), lambda b, ws: (b, 0)),
            scratch_shapes=[pltpu.VMEM((2, N_EXP, MOE_WIN, D), BF16),
                            pltpu.SemaphoreType.DMA((2, N_EXP))]),
        out_shape=jax.ShapeDtypeStruct((N_TOK, D), F32),
        compiler_params=_cparams(("arbitrary",)),
    )(ws, x, rp8, comb, mod6, final_g, ys)


def _moe(x, g, mod6, router, wg, wu, wd, i_moe, final_g, final):
    wr_pad = jnp.pad(router, ((0, 0), (0, 128 - N_EXP)))
    h, r = _router(x, g, mod6, wr_pad)

    i32 = jnp.int32
    i12 = r[:, 0:2].astype(i32)
    earange = jnp.arange(N_EXP, dtype=i32)
    hit1 = i12[:, 0:1] == earange[None, :]
    hit2 = i12[:, 1:2] == earange[None, :]
    comb = jnp.where(hit1, r[:, 2:3], 0.0) + jnp.where(hit2, r[:, 3:4], 0.0)
    mask = (hit1 | hit2).astype(i32)
    csum = jnp.cumsum(mask, axis=0)
    counts = csum[-1]
    padded = ((counts + MOE_TM - 1) // MOE_TM) * MOE_TM
    ends = jnp.cumsum(padded)
    starts = ends - padded
    rp8 = jnp.where(mask > 0, starts[None, :] + csum - 1, -1).astype(i32)
    n_tiles = MOE_ROWS // MOE_TM
    tile_expert = jnp.minimum(
        jnp.searchsorted(ends, jnp.arange(n_tiles, dtype=i32) * MOE_TM, side="right"),
        N_EXP - 1).astype(i32)
    used = (ends[-1:] // MOE_TM).astype(i32)

    sub_row0 = jnp.arange(MOE_ROWS // MOE_TS, dtype=i32) * MOE_TS
    sexp = tile_expert[sub_row0 // MOE_TM]
    qlo = sub_row0 - starts[sexp]
    qend = jnp.minimum(qlo + MOE_TS, counts[sexp])
    cb = csum[MOE_TB - 1::MOE_TB, :]
    cbe = cb.T[sexp]
    blo = jnp.sum((cbe <= qlo[:, None]).astype(i32), axis=1)
    bhi = jnp.minimum(jnp.sum((cbe < qend[:, None]).astype(i32), axis=1), N_TOK // MOE_TB - 1)
    empty = qend <= qlo
    blo = jnp.where(empty, 1, blo).astype(i32)
    bhi = jnp.where(empty, 0, bhi).astype(i32)

    cprev = jnp.concatenate([jnp.zeros((1, N_EXP), i32), cb[:-1]], axis=0)
    ws = (((starts[None, :] + cprev) // MOE_ALIGN) * MOE_ALIGN).reshape(-1).astype(i32)

    rp_t = rp8.T.reshape(N_EXP, N_TOK // MOE_TB, MOE_TB)
    xs = _dispatch(blo, bhi, sexp, h, rp_t)
    ys = _experts(tile_expert, used, xs, wg, wu, wd, i_moe)
    return _combine(ws, x, rp8, comb, mod6, final_g, ys, final)


def _final_norm_kernel(x_ref, g_ref, o_ref):
    x = x_ref[...]
    o_ref[...] = (x * lax.rsqrt(jnp.mean(x * x, axis=-1, keepdims=True) + EPS)) * g_ref[...]


def _final_norm(x, g):
    return pl.pallas_call(
        _final_norm_kernel,
        grid=(N_TOK // TM,),
        in_specs=[pl.BlockSpec((TM, D), lambda i: (i, 0)), pl.BlockSpec((1, D), lambda i: (0, 0))],
        out_specs=pl.BlockSpec((TM, D), lambda i: (i, 0)),
        out_shape=jax.ShapeDtypeStruct((N_TOK, D), F32),
        compiler_params=_cparams(("arbitrary",)),
    )(x, g)


def _cache_leaves(p):
    pp = p[:NP_TOK].reshape(BATCH, SEQ, PROJ)
    na = lambda a: a.reshape(BATCH, SEQ, NA_HEADS, HEAD_DIM).transpose(0, 2, 1, 3)
    dk = pp[:, :, C_DK:C_DV].reshape(BATCH, SEQ, DIFF_HEADS, 2, HEAD_DIM).transpose(0, 2, 3, 1, 4)
    dv = pp[:, :, C_DV:C_HY].reshape(BATCH, SEQ, DIFF_HEADS, DIFF_V).transpose(0, 2, 1, 3)
    return na(pp[:, :, C_NAK:C_NAV]), na(pp[:, :, C_NAV:C_DQ]), dk, dv


def kernel(x_prompt, x_sample, cache_na_k, cache_na_v, cache_diff_k, cache_diff_v, c, c_ctx, w_in, w_out, ada_w, ada_b, norm_mix_g, norm_ffn_g, na_rpb, diff_lq1, diff_lk1, diff_lq2, diff_lk2, diff_subln_g, hy_conv_w, hy_conv_b, hy_d, hy_f_w1, hy_f_b1, hy_f_w2, hy_f_b2, hy_f_freq, hy_f_w3, ffn_w_gate, ffn_w_up, ffn_w_down, moe_router, moe_w_gate, moe_w_up, moe_w_down, final_norm_g):
    x = jnp.concatenate([x_prompt.reshape(NP_TOK, D), x_sample.reshape(NS_TOK, D)], axis=0)
    cond8 = jnp.concatenate([c_ctx[None, :], c, jnp.zeros((5, D), F32)], axis=0)
    mods = _modulation(cond8, ada_w, ada_b)
    final_g = final_norm_g.reshape(1, D)

    leaves = []
    for l in range(DEPTH):
        lam_init = 0.8 - 0.6 * math.exp(-0.3 * l)
        mod6 = mods[l].reshape(8, 6, D).transpose(1, 0, 2).reshape(6, 8, 1, D)
        row = lambda a: a[l].reshape(1, -1)
        lq1, lk1, lq2, lk2, subg = row(diff_lq1), row(diff_lk1), row(diff_lq2), row(diff_lk2), row(diff_subln_g)

        p = _inproj(x, row(norm_mix_g), mod6, w_in, l)
        leaves.append(_cache_leaves(p))

        mix_p = _prompt_attn(p, lq1, lk1, lq2, lk2, subg, lam_init)
        bias = _na_bias(na_rpb[l])
        na_s = _na_attn(p, cache_na_k, cache_na_v, bias, l)
        d_s = _sdiff_attn(p, cache_diff_k, cache_diff_v, lq1, lk1, lq2, lk2, subg, lam_init, l)

        w1p = jnp.pad(hy_f_w1[l], ((0, 128 - HY_EMB), (0, 0)))
        fargs = (w1p, row(hy_f_b1), hy_f_w2[l], row(hy_f_b2), row(hy_f_freq), hy_f_w3[l])
        cargs = (hy_conv_w[l], row(hy_conv_b), hy_d[l])
        sa_p, sb_p = _hy_spectra(SEQ, SEQ, *fargs)
        hy_p = _hy_conv(p, SEQ, 2, SEQ, 0, BATCH, *cargs, sa_p, sb_p)
        sa_s, sb_s = _hy_spectra(DEC_SEQ, 512, *fargs)
        hy_s = _hy_conv(p, DEC_SEQ, 1, 512, NP_TOK // DEC_SEQ, DEC_BATCH, *cargs, sa_s, sb_s)

        mix = jnp.concatenate([jnp.concatenate([mix_p, hy_p], axis=1),
                               jnp.concatenate([na_s, d_s, hy_s], axis=1)], axis=0)
        x = _outproj(x, mix, mod6, w_out, l)

        if l % 2 == 0:
            x = _dense_ffn(x, row(norm_ffn_g), mod6, ffn_w_gate, ffn_w_up, ffn_w_down, l // 2)
            if l == DEPTH - 1:
                x = _final_norm(x, final_g)
        else:
            x = _moe(x, row(norm_ffn_g), mod6, moe_router[l // 2], moe_w_gate, moe_w_up, moe_w_down,
                     l // 2, final_g, l == DEPTH - 1)

    y_prompt = x[:NP_TOK].reshape(BATCH, SEQ, D)
    y_sample = x[NP_TOK:].reshape(DEC_BATCH, DEC_SEQ, D)
    stack = lambda k: jnp.stack([leaves[l][k] for l in range(DEPTH)], axis=1)
    return (y_prompt, y_sample, stack(0), stack(1), stack(2), stack(3))
```

```python
import functools
import math

import numpy as np
import jax
import jax.numpy as jnp
from jax import lax
from jax.experimental import pallas as pl
from jax.experimental.pallas import tpu as pltpu

F32 = jnp.float32
BF16 = jnp.bfloat16

D = 1024
BATCH, SEQ = 32, 256
DEC_BATCH, DEC_SEQ = 2, 2048
DEPTH = 2
PAST = 512
GRID_W = 64
GRID_ROWS = DEC_SEQ // GRID_W
HEAD_DIM = 64
NA_HEADS = 4
DIFF_HEADS = 4
DIFF_V = 128
WIN_ROWS, WIN_COLS = 8, 16
HY_W = 256
HY_EMB = 33
HY_BANDS = 16
HY_HID = 64
PROJ = 3072
FFN = 2816
N_EXP = 8
EXP_DIM = 3584
EPS = 1e-6
ROPE_BASE = 10000.0

NP_TOK = BATCH * SEQ
NS_TOK = DEC_BATCH * DEC_SEQ
N_TOK = NP_TOK + NS_TOK

C_NAQ, C_NAK, C_NAV = 0, 256, 512
C_DQ, C_DK, C_DV = 768, 1280, 1792
C_HY = 2304

TM = 1024
VMEM_LIMIT = 56 * 1024 * 1024

MOE_TM = 1024
MOE_TF = 512
MOE_ROWS = 2 * N_TOK + N_EXP * MOE_TM
MOE_TS = 512
MOE_DTB = 256
MOE_TB = 512
MOE_ALIGN = 16
MOE_WIN = MOE_TB + MOE_ALIGN
MOE_YROWS = MOE_ROWS + MOE_TM


def _cparams(sem):
    return pltpu.CompilerParams(dimension_semantics=sem, vmem_limit_bytes=VMEM_LIMIT)


def _dot(a, b):
    return jnp.dot(a, b, preferred_element_type=F32)


def _dot_nt(a, b):
    return lax.dot_general(a, b, (((1,), (1,)), ((), ())), preferred_element_type=F32)


def _split(a):
    hi = a.astype(BF16)
    lo = (a - hi.astype(F32)).astype(BF16)
    return hi, lo


def _dot3(a, b):
    ah, al = _split(a)
    bh, bl = _split(b)
    return _dot(ah, bh) + (_dot(ah, bl) + _dot(al, bh))


def _silu(x):
    return x / (1.0 + jnp.exp(-x))


def _mod_row(i, tm):
    t = i * tm
    return jnp.where(t < NP_TOK, 0, 1 + (t - NP_TOK) // DEC_SEQ)


def _mod_spec(k, tm):
    return pl.BlockSpec((None, None, 1, D), lambda i, *_: (k, _mod_row(i, tm), 0, 0))


def _norm_mod(x, g, sc, sh):
    y = x * lax.rsqrt(jnp.mean(x * x, axis=-1, keepdims=True) + EPS)
    return (y * g) * (1.0 + sc) + sh


def _mod_kernel(c_ref, w_ref, b_ref, o_ref):
    o_ref[...] = _dot3(_silu(c_ref[...]), w_ref[...]) + b_ref[...]


def _modulation(cond8, ada_w, ada_b):
    tn = 1536
    return pl.pallas_call(
        _mod_kernel,
        grid=(DEPTH, 6 * D // tn),
        in_specs=[pl.BlockSpec((8, D), lambda l, j: (0, 0)),
                  pl.BlockSpec((None, D, tn), lambda l, j: (l, 0, j)),
                  pl.BlockSpec((None, 1, tn), lambda l, j: (l, 0, j))],
        out_specs=pl.BlockSpec((None, 8, tn), lambda l, j: (l, 0, j)),
        out_shape=jax.ShapeDtypeStruct((DEPTH, 8, 6 * D), F32),
        compiler_params=_cparams(("arbitrary", "arbitrary")),
    )(cond8, ada_w, ada_b.reshape(DEPTH, 1, 6 * D))


def _inproj_kernel(x_ref, g_ref, sc_ref, sh_ref, w_ref, o_ref, h_scr):
    @pl.when(pl.program_id(1) == 0)
    def _():
        h_scr[...] = _norm_mod(x_ref[...], g_ref[...], sc_ref[...], sh_ref[...]).astype(BF16)

    o_ref[...] = _dot(h_scr[...], w_ref[...].astype(BF16))


def _inproj(x, g, mod6, w_in, l):
    tn = 768
    return pl.pallas_call(
        _inproj_kernel,
        grid=(N_TOK // TM, PROJ // tn),
        in_specs=[pl.BlockSpec((TM, D), lambda i, j: (i, 0)),
                  pl.BlockSpec((1, D), lambda i, j: (0, 0)),
                  _mod_spec(1, TM), _mod_spec(0, TM),
                  pl.BlockSpec((None, D, tn), lambda i, j: (l, 0, j))],
        out_specs=pl.BlockSpec((TM, tn), lambda i, j: (i, j)),
        out_shape=jax.ShapeDtypeStruct((N_TOK, PROJ), F32),
        scratch_shapes=[pltpu.VMEM((TM, D), BF16)],
        compiler_params=_cparams(("arbitrary", "arbitrary")),
    )(x, g, mod6, mod6, w_in)


def _lam(lq1, lk1, lq2, lk2, lam_init):
    return (jnp.exp(jnp.sum(lq1[...] * lk1[...], axis=-1, keepdims=True))
            - jnp.exp(jnp.sum(lq2[...] * lk2[...], axis=-1, keepdims=True)) + lam_init)


def _softmax_parts(s):
    m = jnp.max(s, axis=-1, keepdims=True)
    e = jnp.exp(s - m)
    return e, jnp.sum(e, axis=-1, keepdims=True)


def _diff_head(q1, q2, k1, k2, v, lam, g, lam_init):
    scale = HEAD_DIM ** -0.5
    e1, l1 = _softmax_parts(_dot_nt(q1, k1) * scale)
    e2, l2 = _softmax_parts(_dot_nt(q2, k2) * scale)
    a = e1 * (1.0 / l1) - (lam * (1.0 / l2)) * e2
    o = _dot(a.astype(BF16), v)
    o = o * lax.rsqrt(jnp.mean(o * o, axis=-1, keepdims=True) + EPS)
    return (o * g) * (1.0 - lam_init)


def _prompt_attn_kernel(pa_ref, pb_ref, pc_ref, lq1, lk1, lq2, lk2, g_ref, o_ref, *, lam_init):
    lam = _lam(lq1, lk1, lq2, lk2, lam_init)
    g = g_ref[...]
    scale = HEAD_DIM ** -0.5

    def col(c0, w):
        ref = (pa_ref, pb_ref, pc_ref)[c0 // 768]
        o = c0 % 768
        return ref[:, o:o + w].astype(BF16)

    for h in range(NA_HEADS):
        q = col(C_NAQ + 64 * h, 64)
        k = col(C_NAK + 64 * h, 64)
        v = col(C_NAV + 64 * h, 64)
        e, l = _softmax_parts(_dot_nt(q, k) * scale)
        o = _dot(e.astype(BF16), v) * (1.0 / l)
        o_ref[:, 64 * h:64 * h + 64] = o.astype(BF16)
    for h in range(DIFF_HEADS):
        q1 = col(C_DQ + 128 * h, 64)
        q2 = col(C_DQ + 128 * h + 64, 64)
        k1 = col(C_DK + 128 * h, 64)
        k2 = col(C_DK + 128 * h + 64, 64)
        v = col(C_DV + 128 * h, 128)
        o = _diff_head(q1, q2, k1, k2, v, lam, g, lam_init)
        o_ref[:, 256 + 128 * h:384 + 128 * h] = o.astype(BF16)


def _prompt_attn(p, lq1, lk1, lq2, lk2, g, lam_init):
    vec = lambda n: pl.BlockSpec((1, n), lambda b: (0, 0))
    return pl.pallas_call(
        functools.partial(_prompt_attn_kernel, lam_init=lam_init),
        grid=(BATCH,),
        in_specs=[pl.BlockSpec((SEQ, 768), lambda b: (b, 0)),
                  pl.BlockSpec((SEQ, 768), lambda b: (b, 1)),
                  pl.BlockSpec((SEQ, 768), lambda b: (b, 2)),
                  vec(64), vec(64), vec(64), vec(64), vec(128)],
        out_specs=pl.BlockSpec((SEQ, 768), lambda b: (b, 0)),
        out_shape=jax.ShapeDtypeStruct((NP_TOK, 768), BF16),
        compiler_params=_cparams(("arbitrary",)),
    )(p, p, p, lq1, lk1, lq2, lk2, g)


def _bias_kernel(rpb_ref, o_ref):
    h = pl.program_id(0)
    case = pl.program_id(1)
    qc = lax.broadcasted_iota(jnp.int32, (GRID_W, GRID_W), 0)
    kc = lax.broadcasted_iota(jnp.int32, (GRID_W, GRID_W), 1)
    delta = jnp.clip(kc - qc + (WIN_COLS - 1), 0, 2 * WIN_COLS - 2)
    qs = jnp.clip(qc - WIN_COLS // 2, 0, GRID_W - WIN_COLS)
    in_win = (kc >= qs) & (kc < qs + WIN_COLS)
    for i in range(WIN_ROWS):
        dr = i - case + (WIN_ROWS - 1)
        base = (h * (2 * WIN_ROWS - 1) + dr) * (2 * WIN_COLS - 1)
        acc = jnp.zeros((GRID_W, GRID_W), F32)
        for d in range(2 * WIN_COLS - 1):
            acc = jnp.where(delta == d, rpb_ref[base + d], acc)
        o_ref[:, i * GRID_W:(i + 1) * GRID_W] = jnp.where(in_win, acc, -jnp.inf)


def _na_bias(rpb):
    return pl.pallas_call(
        _bias_kernel,
        grid=(NA_HEADS, WIN_ROWS),
        in_specs=[pl.BlockSpec(memory_space=pltpu.SMEM)],
        out_specs=pl.BlockSpec((None, None, GRID_W, WIN_ROWS * GRID_W), lambda h, c: (h, c, 0, 0)),
        out_shape=jax.ShapeDtypeStruct((NA_HEADS, WIN_ROWS, GRID_W, WIN_ROWS * GRID_W), F32),
        compiler_params=_cparams(("arbitrary", "arbitrary")),
    )(rpb.reshape(-1))


def _na_kernel(q_ref, kv_ref, kc_ref, vc_ref, bias_ref, o_ref):
    r = pl.program_id(1)
    start = jnp.clip(r - WIN_ROWS // 2, 0, GRID_ROWS - WIN_ROWS)
    case = r - start
    row0 = pl.multiple_of(start * GRID_W, GRID_W)
    nloc = WIN_ROWS * GRID_W
    scale = HEAD_DIM ** -0.5
    for h in range(NA_HEADS):
        q = q_ref[:, C_NAQ + 64 * h:C_NAQ + 64 * h + 64].astype(BF16)
        k = kv_ref[pl.ds(row0, nloc), C_NAK + 64 * h:C_NAK + 64 * h + 64].astype(BF16)
        v = kv_ref[pl.ds(row0, nloc), C_NAV + 64 * h:C_NAV + 64 * h + 64].astype(BF16)
        s_loc = _dot_nt(q, k) * scale + bias_ref[h, pl.ds(case, 1)][0]
        s_ctx = _dot_nt(q, kc_ref[h].astype(BF16)) * scale
        m = jnp.maximum(jnp.max(s_loc, axis=-1, keepdims=True), jnp.max(s_ctx, axis=-1, keepdims=True))
        e_loc = jnp.exp(s_loc - m)
        e_ctx = jnp.exp(s_ctx - m)
        l = jnp.sum(e_loc, axis=-1, keepdims=True) + jnp.sum(e_ctx, axis=-1, keepdims=True)
        o = (_dot(e_loc.astype(BF16), v) + _dot(e_ctx.astype(BF16), vc_ref[h].astype(BF16))) * (1.0 / l)
        o_ref[:, 64 * h:64 * h + 64] = o.astype(BF16)


def _na_attn(p, cache_k, cache_v, bias, l):
    qblk0 = NP_TOK // GRID_W
    kvblk0 = NP_TOK // DEC_SEQ
    return pl.pallas_call(
        _na_kernel,
        grid=(DEC_BATCH, GRID_ROWS),
        in_specs=[pl.BlockSpec((GRID_W, 768), lambda b, r: (qblk0 + b * GRID_ROWS + r, 0)),
                  pl.BlockSpec((DEC_SEQ, 768), lambda b, r: (kvblk0 + b, 0)),
                  pl.BlockSpec((None, None, NA_HEADS, PAST, HEAD_DIM), lambda b, r: (b, l, 0, 0, 0)),
                  pl.BlockSpec((None, None, NA_HEADS, PAST, HEAD_DIM), lambda b, r: (b, l, 0, 0, 0)),
                  pl.BlockSpec((NA_HEADS, WIN_ROWS, GRID_W, WIN_ROWS * GRID_W), lambda b, r: (0, 0, 0, 0))],
        out_specs=pl.BlockSpec((GRID_W, 256), lambda b, r: (b * GRID_ROWS + r, 0)),
        out_shape=jax.ShapeDtypeStruct((NS_TOK, 256), BF16),
        compiler_params=_cparams(("arbitrary", "arbitrary")),
    )(p, p, cache_k, cache_v, bias)


@functools.lru_cache(None)
def _rope_tables():
    t = np.arange(DEC_SEQ)
    lane = np.arange(128)
    dd = lane % HEAD_DIM
    pos = np.where(dd[None, :] < 32, (t // GRID_W)[:, None], (t % GRID_W)[:, None]).astype(np.float64)
    inv = ROPE_BASE ** (-(dd % 16).astype(np.float64) * 2.0 / 32.0)
    ang = pos * inv[None, :]
    first = (dd % 32) < 16
    cos = np.cos(ang)
    s_up = np.where(first[None, :], -np.sin(ang), 0.0)
    s_dn = np.where(first[None, :], 0.0, np.sin(ang))
    return tuple(np.asarray(a, np.float32) for a in (cos, s_up, s_dn))


def _rope(x, cos, s_up, s_dn):
    return x * cos + pltpu.roll(x, 112, axis=1) * s_up + pltpu.roll(x, 16, axis=1) * s_dn


def _sdiff_kernel(q_ref, k_ref, v_ref, ck_ref, cv_ref, cos_ref, sup_ref, sdn_ref,
                  lq1, lk1, lq2, lk2, g_ref, o_ref, k1_scr, k2_scr, v_scr, *, lam_init, tq):
    qb = pl.program_id(2)

    @pl.when(qb == 0)
    def _():
        kr = _rope(k_ref[...], cos_ref[...], sup_ref[...], sdn_ref[...])
        k1_scr[0:DEC_SEQ, :] = kr[:, :64].astype(BF16)
        k2_scr[0:DEC_SEQ, :] = kr[:, 64:].astype(BF16)
        k1_scr[DEC_SEQ:, :] = ck_ref[0].astype(BF16)
        k2_scr[DEC_SEQ:, :] = ck_ref[1].astype(BF16)
        v_scr[0:DEC_SEQ, :] = v_ref[...].astype(BF16)
        v_scr[DEC_SEQ:, :] = cv_ref[...].astype(BF16)

    r0 = pl.multiple_of(qb * tq, tq)
    rows = pl.ds(r0, tq)
    qr = _rope(q_ref[...], cos_ref[rows, :], sup_ref[rows, :], sdn_ref[rows, :])
    lam = _lam(lq1, lk1, lq2, lk2, lam_init)
    o = _diff_head(qr[:, :64].astype(BF16), qr[:, 64:].astype(BF16), k1_scr[...], k2_scr[...],
                   v_scr[...], lam, g_ref[...], lam_init)
    o_ref[...] = o.astype(BF16)


def _sdiff_attn(p, cache_k, cache_v, lq1, lk1, lq2, lk2, g, lam_init, l):
    tq = 256
    nq = DEC_SEQ // tq
    cos, s_up, s_dn = (jnp.asarray(a) for a in _rope_tables())
    vec = lambda n: pl.BlockSpec((1, n), lambda b, h, q: (0, 0))
    tab = pl.BlockSpec((DEC_SEQ, 128), lambda b, h, q: (0, 0))
    kvblk0 = NP_TOK // DEC_SEQ
    return pl.pallas_call(
        functools.partial(_sdiff_kernel, lam_init=lam_init, tq=tq),
        grid=(DEC_BATCH, DIFF_HEADS, nq),
        in_specs=[pl.BlockSpec((tq, 128), lambda b, h, q: (NP_TOK // tq + b * nq + q, C_DQ // 128 + h)),
                  pl.BlockSpec((DEC_SEQ, 128), lambda b, h, q: (kvblk0 + b, C_DK // 128 + h)),
                  pl.BlockSpec((DEC_SEQ, 128), lambda b, h, q: (kvblk0 + b, C_DV // 128 + h)),
                  pl.BlockSpec((None, None, None, 2, PAST, HEAD_DIM), lambda b, h, q: (b, l, h, 0, 0, 0)),
                  pl.BlockSpec((None, None, None, PAST, DIFF_V), lambda b, h, q: (b, l, h, 0, 0)),
                  tab, tab, tab, vec(64), vec(64), vec(64), vec(64), vec(128)],
        out_specs=pl.BlockSpec((tq, 128), lambda b, h, q: (b * nq + q, h)),
        out_shape=jax.ShapeDtypeStruct((NS_TOK, DIFF_HEADS * DIFF_V), BF16),
        scratch_shapes=[pltpu.VMEM((DEC_SEQ + PAST, HEAD_DIM), BF16),
                        pltpu.VMEM((DEC_SEQ + PAST, HEAD_DIM), BF16),
                        pltpu.VMEM((DEC_SEQ + PAST, DIFF_V), BF16)],
        compiler_params=_cparams(("arbitrary", "arbitrary", "arbitrary")),
    )(p, p, p, cache_k, cache_v, cos, s_up, s_dn, lq1, lk1, lq2, lk2, g)


@functools.lru_cache(None)
def _dft_consts(L):
    n = 2 * L
    k = np.arange(L)
    ang = 2.0 * np.pi * ((k[:, None] * k[None, :]) % n) / n
    alt = (-1.0) ** k
    fa = np.cos(ang)
    fb = -np.sin(ang)
    fb[0, :] = alt
    wgt = np.full((L,), 2.0 / n)
    wgt[0] = 1.0 / n
    ga = fa * wgt[:, None]
    gb = fb * wgt[:, None]
    gb[0, :] = alt / n
    f = np.concatenate([fa, fb], axis=0)
    g = np.concatenate([ga.T, gb.T], axis=1)
    return np.asarray(f, dtype=BF16), np.asarray(g, dtype=BF16)


@functools.lru_cache(None)
def _filter_consts(L):
    f32 = np.float32
    t = np.linspace(0.0, 1.0, L, dtype=f32)[:, None]
    pos = np.arange(L, dtype=f32)[:, None]
    bands = np.linspace(1e-4, HY_BANDS - 1, HY_BANDS, dtype=f32)[None, :]
    ang = f32(2.0 * math.pi) * bands * pos / f32(L)
    z = np.zeros((L, 128), f32)
    z[:, 0:1] = t
    z[:, 1:1 + HY_BANDS] = np.cos(ang)
    z[:, 1 + HY_BANDS:HY_EMB] = -np.sin(ang)
    min_decay = math.log(1e-2) / 1.5
    max_decay = math.log(1e-2) / 0.3
    deltas = np.abs(np.linspace(min_decay, max_decay, HY_W, dtype=f32))
    decay = np.exp(-t * deltas[None, :]).astype(f32)
    return z, decay


def _spectra_kernel(z_ref, w1_ref, b1_ref, w2_ref, b2_ref, fr_ref, w3_ref, dec_ref, fa_ref, fb_ref,
                    sa_ref, sb_ref, filt_scr, *, L, kc):
    j = pl.program_id(0)

    @pl.when(j == 0)
    def _():
        fr = fr_ref[...]
        hdn = jnp.sin(fr * (_dot3(z_ref[...], w1_ref[...]) + b1_ref[...]))
        hdn = jnp.sin(fr * (_dot3(hdn, w2_ref[...]) + b2_ref[...]))
        dec = dec_ref[...]
        not_first = lax.broadcasted_iota(jnp.int32, (L, HY_W), 0) > 0
        for o in range(2):
            hf = _dot3(hdn, w3_ref[:, 512 * o:512 * o + 256]) * dec
            hb = jnp.where(not_first, _dot3(hdn, w3_ref[:, 512 * o + 256:512 * o + 512]) * dec, 0.0)
            nrm = (jnp.sum(jnp.abs(hf), axis=0, keepdims=True)
                   + jnp.sum(jnp.abs(hb), axis=0, keepdims=True))
            filt_scr[:, 512 * o:512 * o + 256] = (hf / nrm).astype(BF16)
            filt_scr[:, 512 * o + 256:512 * o + 512] = (hb / nrm).astype(BF16)

    ta = _dot(fa_ref[...], filt_scr[...])
    tb = _dot(fb_ref[...], filt_scr[...])
    first = (lax.broadcasted_iota(jnp.int32, (kc, HY_W), 0) + j * kc) == 0
    for o in range(2):
        af, ab = ta[:, 512 * o:512 * o + 256], ta[:, 512 * o + 256:512 * o + 512]
        bf, bb = tb[:, 512 * o:512 * o + 256], tb[:, 512 * o + 256:512 * o + 512]
        sa_ref[:, 256 * o:256 * o + 256] = af + ab
        sb_ref[:, 256 * o:256 * o + 256] = jnp.where(first, bf + bb, bf - bb)


def _hy_spectra(L, kc, w1p, b1, w2, b2, fr, w3):
    z, decay = _filter_consts(L)
    f = jnp.asarray(_dft_consts(L)[0])
    nj = L // kc
    full = lambda shape: pl.BlockSpec(shape, lambda j: tuple(0 for _ in shape))
    return pl.pallas_call(
        functools.partial(_spectra_kernel, L=L, kc=kc),
        grid=(nj,),
        in_specs=[full((L, 128)), full((128, HY_HID)), full((1, HY_HID)), full((HY_HID, HY_HID)),
                  full((1, HY_HID)), full((1, HY_HID)), full((HY_HID, 4 * HY_W)), full((L, HY_W)),
                  pl.BlockSpec((kc, L), lambda j: (j, 0)),
                  pl.BlockSpec((kc, L), lambda j: (j + nj, 0))],
        out_specs=[pl.BlockSpec((kc, 2 * HY_W), lambda j: (j, 0)),
                   pl.BlockSpec((kc, 2 * HY_W), lambda j: (j, 0))],
        out_shape=[jax.ShapeDtypeStruct((L, 2 * HY_W), F32)] * 2,
        scratch_shapes=[pltpu.VMEM((L, 4 * HY_W), BF16)],
        compiler_params=_cparams(("arbitrary",)),
    )(jnp.asarray(z), w1p, b1, w2, b2, fr, w3, jnp.asarray(decay), f, f)


def _hyconv_kernel(*refs, L, nb, kc, nj):
    u_refs = refs[:nb]
    (cw_ref, cb_ref, d_ref, fa_ref, fb_ref, ga_ref, gb_ref, sa_ref, sb_ref, o_ref,
     xin_scr, x_scr, g1_scr, g2_scr, y_scr) = refs[nb:]
    o = pl.program_id(1)
    j = pl.program_id(2)

    @pl.when((o == 0) & (j == 0))
    def _():
        row = lax.broadcasted_iota(jnp.int32, (L, HY_W), 0)
        for i in range(nb):
            cols = slice(HY_W * i, HY_W * (i + 1))
            for part, dst in enumerate((x_scr, g1_scr, g2_scr)):
                pc = slice(HY_W * part, HY_W * (part + 1))
                u = u_refs[i][:, pc]
                up = jnp.where(row == 0, 0.0, pltpu.roll(u, 1, axis=0))
                un = jnp.where(row == L - 1, 0.0, pltpu.roll(u, L - 1, axis=0))
                dst[:, cols] = (up * cw_ref[0:1, pc] + u * cw_ref[1:2, pc] + un * cw_ref[2:3, pc]
                                + cb_ref[:, pc])
            xin_scr[:, cols] = x_scr[:, cols].astype(BF16)

    @pl.when(j == 0)
    def _():
        y_scr[...] = jnp.zeros_like(y_scr)

    xa = _dot(fa_ref[...], xin_scr[...])
    xb = _dot(fb_ref[...], xin_scr[...])
    sa = jnp.concatenate([sa_ref[...]] * nb, axis=1)
    sb = jnp.concatenate([sb_ref[...]] * nb, axis=1)
    first = (lax.broadcasted_iota(jnp.int32, (kc, nb * HY_W), 0) + j * kc) == 0
    ya = jnp.where(first, xa * sa, xa * sa - xb * sb)
    yb = jnp.where(first, xb * sb, xa * sb + xb * sa)
    y_scr[...] += _dot(ga_ref[...], ya.astype(BF16)) + _dot(gb_ref[...], yb.astype(BF16))

    @pl.when(j == nj - 1)
    def _():
        dvec = jnp.concatenate([d_ref[pl.ds(o, 1), :]] * nb, axis=1)
        y = y_scr[...] + x_scr[...] * dvec

        @pl.when(o == 0)
        def _():
            zz = g1_scr[...] * y
            x_scr[...] = zz
            xin_scr[...] = zz.astype(BF16)

        @pl.when(o == 1)
        def _():
            res = g2_scr[...] * y
            for i in range(nb):
                o_ref[L * i:L * (i + 1), :] = res[:, HY_W * i:HY_W * (i + 1)].astype(BF16)


def _hy_conv(p, L, nb, kc, row_blk0, n_seq, conv_w, conv_b, d, sa, sb):
    f, g = (jnp.asarray(a) for a in _dft_consts(L))
    nj = L // kc
    u_spec = lambda i: pl.BlockSpec((L, 3 * HY_W), lambda bg, o, j: (row_blk0 + bg * nb + i, C_HY // 768),
                                    pipeline_mode=pl.Buffered(1))
    small = lambda shape: pl.BlockSpec(shape, lambda bg, o, j: (0, 0))
    return pl.pallas_call(
        functools.partial(_hyconv_kernel, L=L, nb=nb, kc=kc, nj=nj),
        grid=(n_seq // nb, 2, nj),
        in_specs=[u_spec(i) for i in range(nb)] + [
            small((3, 3 * HY_W)), small((1, 3 * HY_W)), small((2, HY_W)),
            pl.BlockSpec((kc, L), lambda bg, o, j: (j, 0)),
            pl.BlockSpec((kc, L), lambda bg, o, j: (j + nj, 0)),
            pl.BlockSpec((L, kc), lambda bg, o, j: (0, j)),
            pl.BlockSpec((L, kc), lambda bg, o, j: (0, j + nj)),
            pl.BlockSpec((kc, HY_W), lambda bg, o, j: (j, o)),
            pl.BlockSpec((kc, HY_W), lambda bg, o, j: (j, o))],
        out_specs=pl.BlockSpec((nb * L, HY_W), lambda bg, o, j: (bg, 0)),
        out_shape=jax.ShapeDtypeStruct((n_seq * L, HY_W), BF16),
        scratch_shapes=[pltpu.VMEM((L, nb * HY_W), BF16)] + [pltpu.VMEM((L, nb * HY_W), F32)] * 4,
        compiler_params=_cparams(("arbitrary", "arbitrary", "arbitrary")),
    )(*([p] * nb), conv_w, conv_b, d, f, f, g, g, sa, sb)


def _outproj_kernel(x_ref, mixp_ref, hyp_ref, nas_ref, ds_ref, hys_ref, g1_ref, w_ref, o_ref):
    i = pl.program_id(0)
    w = lambda a, b: w_ref[a:b, :].astype(BF16)

    @pl.when(i < NP_TOK // TM)
    def _():
        y = _dot(mixp_ref[...], w(0, 768)) + _dot(hyp_ref[...], w(768, D))
        o_ref[...] = x_ref[...] + g1_ref[...] * y

    @pl.when(i >= NP_TOK // TM)
    def _():
        y = _dot(nas_ref[...], w(0, 256)) + _dot(ds_ref[...], w(256, 768)) + _dot(hys_ref[...], w(768, D))
        o_ref[...] = x_ref[...] + g1_ref[...] * y


def _outproj(x, mix_p, hy_p, na_s, d_s, hy_s, mod6, w_out, l):
    npt = NP_TOK // TM
    pidx = lambda i: (jnp.minimum(i, npt - 1), 0)
    sidx = lambda i: (jnp.maximum(i - npt, 0), 0)
    return pl.pallas_call(
        _outproj_kernel,
        grid=(N_TOK // TM,),
        in_specs=[pl.BlockSpec((TM, D), lambda i: (i, 0)),
                  pl.BlockSpec((TM, 768), pidx),
                  pl.BlockSpec((TM, HY_W), pidx),
                  pl.BlockSpec((TM, 256), sidx),
                  pl.BlockSpec((TM, 512), sidx),
                  pl.BlockSpec((TM, HY_W), sidx),
                  _mod_spec(2, TM),
                  pl.BlockSpec((None, D, D), lambda i: (l, 0, 0))],
        out_specs=pl.BlockSpec((TM, D), lambda i: (i, 0)),
        out_shape=jax.ShapeDtypeStruct((N_TOK, D), F32),
        compiler_params=_cparams(("arbitrary",)),
    )(x, mix_p, hy_p, na_s, d_s, hy_s, mod6, w_out)


def _ffn_kernel(x_ref, g_ref, sc_ref, sh_ref, gate_ref, wg_ref, wu_ref, wd_ref, o_ref, h_scr, *, nj):
    j = pl.program_id(1)

    @pl.when(j == 0)
    def _():
        h_scr[...] = _norm_mod(x_ref[...], g_ref[...], sc_ref[...], sh_ref[...]).astype(BF16)
        o_ref[...] = jnp.zeros_like(o_ref)

    h = h_scr[...]
    a = _silu(_dot(h, wg_ref[...].astype(BF16))) * _dot(h, wu_ref[...].astype(BF16))
    o_ref[...] += _dot(a.astype(BF16), wd_ref[...].astype(BF16))

    @pl.when(j == nj - 1)
    def _():
        o_ref[...] = x_ref[...] + gate_ref[...] * o_ref[...]


def _dense_ffn(x, g, mod6, wg, wu, wd, i_ffn):
    tf = 256
    tm = DEC_SEQ
    nj = FFN // tf
    return pl.pallas_call(
        functools.partial(_ffn_kernel, nj=nj),
        grid=(N_TOK // tm, nj),
        in_specs=[pl.BlockSpec((tm, D), lambda i, j: (i, 0)),
                  pl.BlockSpec((1, D), lambda i, j: (0, 0)),
                  _mod_spec(4, tm), _mod_spec(3, tm), _mod_spec(5, tm),
                  pl.BlockSpec((None, D, tf), lambda i, j: (i_ffn, 0, j)),
                  pl.BlockSpec((None, D, tf), lambda i, j: (i_ffn, 0, j)),
                  pl.BlockSpec((None, tf, D), lambda i, j: (i_ffn, j, 0))],
        out_specs=pl.BlockSpec((tm, D), lambda i, j: (i, 0)),
        out_shape=jax.ShapeDtypeStruct((N_TOK, D), F32),
        scratch_shapes=[pltpu.VMEM((tm, D), BF16)],
        compiler_params=_cparams(("arbitrary", "arbitrary")),
    )(x, g, mod6, mod6, mod6, wg, wu, wd)


def _router_kernel(x_ref, g_ref, sc_ref, sh_ref, wr_ref, h_ref, r_ref):
    h = _norm_mod(x_ref[...], g_ref[...], sc_ref[...], sh_ref[...])
    h_ref[...] = h.astype(BF16)
    lane = lax.broadcasted_iota(jnp.int32, (TM, 128), 1)
    lg = jnp.where(lane < N_EXP, _dot3(h, wr_ref[...]), -jnp.inf)
    m1 = jnp.max(lg, axis=-1, keepdims=True)
    i1 = jnp.min(jnp.where(lg == m1, lane, 128), axis=-1, keepdims=True)
    lg2 = jnp.where(lane == i1, -jnp.inf, lg)
    m2 = jnp.max(lg2, axis=-1, keepdims=True)
    i2 = jnp.min(jnp.where(lg2 == m2, lane, 128), axis=-1, keepdims=True)
    e = jnp.exp(m2 - m1)
    w1 = 1.0 / (1.0 + e)
    w2 = e / (1.0 + e)
    r_ref[...] = jnp.where(lane == 0, i1.astype(F32),
                           jnp.where(lane == 1, i2.astype(F32),
                                     jnp.where(lane == 2, w1, jnp.where(lane == 3, w2, 0.0))))


def _router(x, g, mod6, wr_pad):
    return pl.pallas_call(
        _router_kernel,
        grid=(N_TOK // TM,),
        in_specs=[pl.BlockSpec((TM, D), lambda i: (i, 0)),
                  pl.BlockSpec((1, D), lambda i: (0, 0)),
                  _mod_spec(4, TM), _mod_spec(3, TM),
                  pl.BlockSpec((D, 128), lambda i: (0, 0))],
        out_specs=[pl.BlockSpec((TM, D), lambda i: (i, 0)),
                   pl.BlockSpec((TM, 128), lambda i: (i, 0))],
        out_shape=[jax.ShapeDtypeStruct((N_TOK, D), BF16), jax.ShapeDtypeStruct((N_TOK, 128), F32)],
        compiler_params=_cparams(("arbitrary",)),
    )(x, g, mod6, mod6, wr_pad)


def _dispatch_kernel(blo_ref, bhi_ref, sexp_ref, h_hbm, rp_ref, o_ref, h_scr, sem):
    s = pl.program_id(0)

    @pl.when(s == 0)
    def _():
        cp = pltpu.make_async_copy(h_hbm, h_scr, sem)
        cp.start()
        cp.wait()

    e = sexp_ref[s]
    rows = s * MOE_TS + lax.broadcasted_iota(jnp.int32, (MOE_TS, MOE_DTB), 0)

    def body(b, acc):
        sel = (rows == rp_ref[e, pl.ds(b, 1), :]).astype(BF16)
        hb = h_scr[pl.ds(pl.multiple_of(b * MOE_DTB, MOE_DTB), MOE_DTB), :]
        return acc + _dot(sel, hb)

    acc = lax.fori_loop(blo_ref[s], bhi_ref[s] + 1, body, jnp.zeros((MOE_TS, D), F32))
    o_ref[...] = acc.astype(BF16)


def _dispatch(blo, bhi, sexp, h, rp_t):
    return pl.pallas_call(
        _dispatch_kernel,
        grid_spec=pltpu.PrefetchScalarGridSpec(
            num_scalar_prefetch=3,
            grid=(MOE_ROWS // MOE_TS,),
            in_specs=[pl.BlockSpec(memory_space=pl.ANY),
                      pl.BlockSpec((N_EXP, N_TOK // MOE_DTB, MOE_DTB), lambda s, *_: (0, 0, 0))],
            out_specs=pl.BlockSpec((MOE_TS, D), lambda s, *_: (s, 0)),
            scratch_shapes=[pltpu.VMEM((N_TOK, D), BF16), pltpu.SemaphoreType.DMA(())]),
        out_shape=jax.ShapeDtypeStruct((MOE_ROWS, D), BF16),
        compiler_params=_cparams(("arbitrary",)),
    )(blo, bhi, sexp, h, rp_t)


def _experts_kernel(te_ref, used_ref, xs_ref, wg_ref, wu_ref, wd_ref, o_ref, acc_scr, *, nj):
    i = pl.program_id(0)
    j = pl.program_id(1)
    live = i < used_ref[0]

    @pl.when(live)
    def _():
        @pl.when(j == 0)
        def _():
            acc_scr[...] = jnp.zeros_like(acc_scr)

        xb = xs_ref[...]
        a = _silu(_dot(xb, wg_ref[...].astype(BF16))) * _dot(xb, wu_ref[...].astype(BF16))
        acc_scr[...] += _dot(a.astype(BF16), wd_ref[...].astype(BF16))

    @pl.when(j == nj - 1)
    def _():
        o_ref[...] = jnp.where(live, acc_scr[...], 0.0).astype(BF16)


def _experts(tile_expert, used, xs, wg, wu, wd, i_moe):
    nj = EXP_DIM // MOE_TF

    def tile(i, used):
        return jnp.minimum(i, used[0] - 1)

    def chunk(i, j, used):
        return jnp.where(i < used[0], j, nj - 1)

    return pl.pallas_call(
        functools.partial(_experts_kernel, nj=nj),
        grid_spec=pltpu.PrefetchScalarGridSpec(
            num_scalar_prefetch=2,
            grid=(MOE_YROWS // MOE_TM, nj),
            in_specs=[pl.BlockSpec((MOE_TM, D), lambda i, j, te, used: (tile(i, used), 0)),
                      pl.BlockSpec((None, None, D, MOE_TF),
                                   lambda i, j, te, used: (i_moe, te[tile(i, used)], 0, chunk(i, j, used))),
                      pl.BlockSpec((None, None, D, MOE_TF),
                                   lambda i, j, te, used: (i_moe, te[tile(i, used)], 0, chunk(i, j, used))),
                      pl.BlockSpec((None, None, MOE_TF, D),
                                   lambda i, j, te, used: (i_moe, te[tile(i, used)], chunk(i, j, used), 0))],
            out_specs=pl.BlockSpec((MOE_TM, D), lambda i, j, te, used: (i, 0)),
            scratch_shapes=[pltpu.VMEM((MOE_TM, D), F32)]),
        out_shape=jax.ShapeDtypeStruct((MOE_YROWS, D), BF16),
        compiler_params=_cparams(("arbitrary", "arbitrary")),
    )(tile_expert, used, xs, wg, wu, wd)


def _combine_kernel(ws_ref, x_ref, rp_ref, comb_ref, gate_ref, fg_ref, ys_hbm, op_ref, os_ref, win_scr, sem):
    b = pl.program_id(0)
    nb = pl.num_programs(0)
    slot = b % 2

    def win_copy(blk, sl, e):
        start = pl.multiple_of(ws_ref[blk * N_EXP + e], MOE_ALIGN)
        return pltpu.make_async_copy(ys_hbm.at[pl.ds(start, MOE_WIN)], win_scr.at[sl, e], sem.at[sl, e])

    @pl.when(b == 0)
    def _():
        for e in range(N_EXP):
            win_copy(0, 0, e).start()

    @pl.when(b + 1 < nb)
    def _():
        for e in range(N_EXP):
            win_copy(b + 1, 1 - slot, e).start()

    col = lax.broadcasted_iota(jnp.int32, (MOE_TB, MOE_WIN), 1)
    y = jnp.zeros((MOE_TB, D), F32)
    for e in range(N_EXP):
        win_copy(b, slot, e).wait()
        sel = (rp_ref[:, e:e + 1] - ws_ref[b * N_EXP + e] == col).astype(BF16)
        y = y + comb_ref[:, e:e + 1] * _dot(sel, win_scr[slot, e])
    x = x_ref[...] + gate_ref[...] * y
    x = (x * lax.rsqrt(jnp.mean(x * x, axis=-1, keepdims=True) + EPS)) * fg_ref[...]

    @pl.when(b < NP_TOK // MOE_TB)
    def _():
        op_ref[...] = x

    @pl.when(b >= NP_TOK // MOE_TB)
    def _():
        os_ref[...] = x


def _combine(ws, x, rp8, comb, mod6, final_g, ys):
    npb = NP_TOK // MOE_TB
    return pl.pallas_call(
        _combine_kernel,
        grid_spec=pltpu.PrefetchScalarGridSpec(
            num_scalar_prefetch=1,
            grid=(N_TOK // MOE_TB,),
            in_specs=[pl.BlockSpec((MOE_TB, D), lambda b, ws: (b, 0)),
                      pl.BlockSpec((MOE_TB, N_EXP), lambda b, ws: (b, 0)),
                      pl.BlockSpec((MOE_TB, N_EXP), lambda b, ws: (b, 0)),
                      _mod_spec(5, MOE_TB),
                      pl.BlockSpec((1, D), lambda b, ws: (0, 0)),
                      pl.BlockSpec(memory_space=pl.ANY)],
            out_specs=[pl.BlockSpec((MOE_TB, D), lambda b, ws: (jnp.minimum(b, npb - 1), 0)),
                       pl.BlockSpec((MOE_TB, D), lambda b, ws: (jnp.maximum(b - npb, 0), 0))],
            scratch_shapes=[pltpu.VMEM((2, N_EXP, MOE_WIN, D), BF16),
                            pltpu.SemaphoreType.DMA((2, N_EXP))]),
        out_shape=[jax.ShapeDtypeStruct((NP_TOK, D), F32), jax.ShapeDtypeStruct((NS_TOK, D), F32)],
        compiler_params=_cparams(("arbitrary",)),
    )(ws, x, rp8, comb, mod6, final_g, ys)


def _moe(x, g, mod6, router, wg, wu, wd, i_moe, final_g):
    wr_pad = jnp.pad(router, ((0, 0), (0, 128 - N_EXP)))
    h, r = _router(x, g, mod6, wr_pad)

    i32 = jnp.int32
    i12 = r[:, 0:2].astype(i32)
    earange = jnp.arange(N_EXP, dtype=i32)
    hit1 = i12[:, 0:1] == earange[None, :]
    hit2 = i12[:, 1:2] == earange[None, :]
    comb = jnp.where(hit1, r[:, 2:3], 0.0) + jnp.where(hit2, r[:, 3:4], 0.0)
    mask = (hit1 | hit2).astype(i32)
    csum = jnp.cumsum(mask, axis=0)
    counts = csum[-1]
    padded = ((counts + MOE_TM - 1) // MOE_TM) * MOE_TM
    ends = jnp.cumsum(padded)
    starts = ends - padded
    rp8 = jnp.where(mask > 0, starts[None, :] + csum - 1, -1).astype(i32)
    n_tiles = MOE_ROWS // MOE_TM
    tile_expert = jnp.minimum(
        jnp.searchsorted(ends, jnp.arange(n_tiles, dtype=i32) * MOE_TM, side="right"),
        N_EXP - 1).astype(i32)
    used = (ends[-1:] // MOE_TM).astype(i32)

    sub_row0 = jnp.arange(MOE_ROWS // MOE_TS, dtype=i32) * MOE_TS
    sexp = tile_expert[sub_row0 // MOE_TM]
    qlo = sub_row0 - starts[sexp]
    qend = jnp.minimum(qlo + MOE_TS, counts[sexp])
    cbe = csum[MOE_DTB - 1::MOE_DTB, :].T[sexp]
    blo = jnp.sum((cbe <= qlo[:, None]).astype(i32), axis=1)
    bhi = jnp.minimum(jnp.sum((cbe < qend[:, None]).astype(i32), axis=1), N_TOK // MOE_DTB - 1)
    empty = qend <= qlo
    blo = jnp.where(empty, 1, blo).astype(i32)
    bhi = jnp.where(empty, 0, bhi).astype(i32)

    cb = csum[MOE_TB - 1::MOE_TB, :]
    cprev = jnp.concatenate([jnp.zeros((1, N_EXP), i32), cb[:-1]], axis=0)
    ws = (((starts[None, :] + cprev) // MOE_ALIGN) * MOE_ALIGN).reshape(-1).astype(i32)

    rp_t = rp8.T.reshape(N_EXP, N_TOK // MOE_DTB, MOE_DTB)
    xs = _dispatch(blo, bhi, sexp, h, rp_t)
    ys = _experts(tile_expert, used, xs, wg, wu, wd, i_moe)
    return _combine(ws, x, rp8, comb, mod6, final_g, ys)


assert DEPTH == 2

def _cache_leaves(p):
    cols = lambda c0, c1: lax.slice(p, (0, c0), (NP_TOK, c1))
    na = lambda a: a.reshape(BATCH, SEQ, NA_HEADS, HEAD_DIM).transpose(0, 2, 1, 3)
    dk = cols(C_DK, C_DV).reshape(BATCH, SEQ, DIFF_HEADS, 2, HEAD_DIM).transpose(0, 2, 3, 1, 4)
    dv = cols(C_DV, C_HY).reshape(BATCH, SEQ, DIFF_HEADS, DIFF_V).transpose(0, 2, 1, 3)
    return na(cols(C_NAK, C_NAV)), na(cols(C_NAV, C_DQ)), dk, dv


def kernel(x_prompt, x_sample, cache_na_k, cache_na_v, cache_diff_k, cache_diff_v, c, c_ctx, w_in, w_out, ada_w, ada_b, norm_mix_g, norm_ffn_g, na_rpb, diff_lq1, diff_lk1, diff_lq2, diff_lk2, diff_subln_g, hy_conv_w, hy_conv_b, hy_d, hy_f_w1, hy_f_b1, hy_f_w2, hy_f_b2, hy_f_freq, hy_f_w3, ffn_w_gate, ffn_w_up, ffn_w_down, moe_router, moe_w_gate, moe_w_up, moe_w_down, final_norm_g):
    x = jnp.concatenate([x_prompt.reshape(NP_TOK, D), x_sample.reshape(NS_TOK, D)], axis=0)
    cond8 = jnp.concatenate([c_ctx[None, :], c, jnp.zeros((5, D), F32)], axis=0)
    mods = _modulation(cond8, ada_w, ada_b)
    final_g = final_norm_g.reshape(1, D)

    leaves = []
    for l in range(DEPTH):
        lam_init = 0.8 - 0.6 * math.exp(-0.3 * l)
        mod6 = mods[l].reshape(8, 6, D).transpose(1, 0, 2).reshape(6, 8, 1, D)
        row = lambda a: a[l].reshape(1, -1)
        lq1, lk1, lq2, lk2, subg = row(diff_lq1), row(diff_lk1), row(diff_lq2), row(diff_lk2), row(diff_subln_g)

        p = _inproj(x, row(norm_mix_g), mod6, w_in, l)
        leaves.append(_cache_leaves(p))

        mix_p = _prompt_attn(p, lq1, lk1, lq2, lk2, subg, lam_init)
        bias = _na_bias(na_rpb[l])
        na_s = _na_attn(p, cache_na_k, cache_na_v, bias, l)
        d_s = _sdiff_attn(p, cache_diff_k, cache_diff_v, lq1, lk1, lq2, lk2, subg, lam_init, l)

        w1p = jnp.pad(hy_f_w1[l], ((0, 128 - HY_EMB), (0, 0)))
        fargs = (w1p, row(hy_f_b1), hy_f_w2[l], row(hy_f_b2), row(hy_f_freq), hy_f_w3[l])
        cargs = (hy_conv_w[l], row(hy_conv_b), hy_d[l])
        sa_p, sb_p = _hy_spectra(SEQ, SEQ, *fargs)
        hy_p = _hy_conv(p, SEQ, 2, SEQ, 0, BATCH, *cargs, sa_p, sb_p)
        sa_s, sb_s = _hy_spectra(DEC_SEQ, 512, *fargs)
        hy_s = _hy_conv(p, DEC_SEQ, 1, 512, NP_TOK // DEC_SEQ, DEC_BATCH, *cargs, sa_s, sb_s)

        x = _outproj(x, mix_p, hy_p, na_s, d_s, hy_s, mod6, w_out, l)

        if l == 0:
            x = _dense_ffn(x, row(norm_ffn_g), mod6, ffn_w_gate, ffn_w_up, ffn_w_down, 0)
        else:
            yp, ys = _moe(x, row(norm_ffn_g), mod6, moe_router[0], moe_w_gate, moe_w_up, moe_w_down,
                          0, final_g)

    stack = lambda k: jnp.stack([leaves[l][k] for l in range(DEPTH)], axis=1)
    return (yp.reshape(BATCH, SEQ, D), ys.reshape(DEC_BATCH, DEC_SEQ, D),
            stack(0), stack(1), stack(2), stack(3))
```

```python
import functools
import math

import numpy as np
import jax
import jax.numpy as jnp
from jax import lax
from jax.experimental import pallas as pl
from jax.experimental.pallas import tpu as pltpu

F32 = jnp.float32
BF16 = jnp.bfloat16

D = 1024
BATCH, SEQ = 32, 256
DEC_BATCH, DEC_SEQ = 2, 2048
DEPTH = 2
PAST = 512
GRID_W = 64
GRID_ROWS = DEC_SEQ // GRID_W
HEAD_DIM = 64
NA_HEADS = 4
DIFF_HEADS = 4
DIFF_V = 128
WIN_ROWS, WIN_COLS = 8, 16
HY_W = 256
HY_EMB = 33
HY_BANDS = 16
HY_HID = 64
PROJ = 3072
FFN = 2816
N_EXP = 8
EXP_DIM = 3584
EPS = 1e-6
ROPE_BASE = 10000.0

NP_TOK = BATCH * SEQ
NS_TOK = DEC_BATCH * DEC_SEQ
N_TOK = NP_TOK + NS_TOK

C_NAQ, C_NAK, C_NAV = 0, 256, 512
C_DQ, C_DK, C_DV = 768, 1280, 1792
C_HY = 2304

TM = 1024
VMEM_LIMIT = 56 * 1024 * 1024

MOE_TM = 1024
MOE_TF = 512
MOE_ROWS = 2 * N_TOK + N_EXP * MOE_TM
MOE_TS = 256
MOE_DTB = 512
MOE_TB = 512
MOE_ALIGN = 16
MOE_WIN = MOE_TB + MOE_ALIGN
MOE_WIN_SMALL = MOE_TB // 2 + MOE_ALIGN
MOE_YROWS = MOE_ROWS + MOE_TM


def _cparams(sem):
    return pltpu.CompilerParams(dimension_semantics=sem, vmem_limit_bytes=VMEM_LIMIT)


def _dot(a, b):
    return jnp.dot(a, b, preferred_element_type=F32)


def _dot_nt(a, b):
    return lax.dot_general(a, b, (((1,), (1,)), ((), ())), preferred_element_type=F32)


def _split(a):
    hi = a.astype(BF16)
    lo = (a - hi.astype(F32)).astype(BF16)
    return hi, lo


def _dot3(a, b):
    ah, al = _split(a)
    bh, bl = _split(b)
    return _dot(ah, bh) + (_dot(ah, bl) + _dot(al, bh))


def _silu(x):
    return x / (1.0 + jnp.exp(-x))


def _mod_row(i, tm):
    t = i * tm
    return jnp.where(t < NP_TOK, 0, 1 + (t - NP_TOK) // DEC_SEQ)


def _mod_spec(k, tm):
    return pl.BlockSpec((None, None, 1, D), lambda i, *_: (k, _mod_row(i, tm), 0, 0))


def _norm_mod(x, g, sc, sh):
    y = x * lax.rsqrt(jnp.mean(x * x, axis=-1, keepdims=True) + EPS)
    return (y * g) * (1.0 + sc) + sh


def _mod_kernel(c_ref, w_ref, b_ref, o_ref):
    o_ref[...] = _dot3(_silu(c_ref[...]), w_ref[...]) + b_ref[...]


def _modulation(cond8, ada_w, ada_b):
    tn = 1536
    return pl.pallas_call(
        _mod_kernel,
        grid=(DEPTH, 6 * D // tn),
        in_specs=[pl.BlockSpec((8, D), lambda l, j: (0, 0)),
                  pl.BlockSpec((None, D, tn), lambda l, j: (l, 0, j)),
                  pl.BlockSpec((None, 1, tn), lambda l, j: (l, 0, j))],
        out_specs=pl.BlockSpec((None, 8, tn), lambda l, j: (l, 0, j)),
        out_shape=jax.ShapeDtypeStruct((DEPTH, 8, 6 * D), F32),
        compiler_params=_cparams(("arbitrary", "arbitrary")),
    )(cond8, ada_w, ada_b.reshape(DEPTH, 1, 6 * D))


def _inproj_kernel(x_ref, g_ref, sc_ref, sh_ref, w_ref, o_ref, h_scr):
    @pl.when(pl.program_id(1) == 0)
    def _():
        h_scr[...] = _norm_mod(x_ref[...], g_ref[...], sc_ref[...], sh_ref[...]).astype(BF16)

    o_ref[...] = _dot(h_scr[...], w_ref[...].astype(BF16))


def _inproj(x, g, mod6, w_in, l):
    tn = 768
    return pl.pallas_call(
        _inproj_kernel,
        grid=(N_TOK // TM, PROJ // tn),
        in_specs=[pl.BlockSpec((TM, D), lambda i, j: (i, 0)),
                  pl.BlockSpec((1, D), lambda i, j: (0, 0)),
                  _mod_spec(1, TM), _mod_spec(0, TM),
                  pl.BlockSpec((None, D, tn), lambda i, j: (l, 0, j))],
        out_specs=pl.BlockSpec((TM, tn), lambda i, j: (i, j)),
        out_shape=jax.ShapeDtypeStruct((N_TOK, PROJ), F32),
        scratch_shapes=[pltpu.VMEM((TM, D), BF16)],
        compiler_params=_cparams(("arbitrary", "arbitrary")),
    )(x, g, mod6, mod6, w_in)


def _lam(lq1, lk1, lq2, lk2, lam_init):
    return (jnp.exp(jnp.sum(lq1[...] * lk1[...], axis=-1, keepdims=True))
            - jnp.exp(jnp.sum(lq2[...] * lk2[...], axis=-1, keepdims=True)) + lam_init)


def _softmax_parts(s):
    m = jnp.max(s, axis=-1, keepdims=True)
    e = jnp.exp(s - m)
    return e, jnp.sum(e, axis=-1, keepdims=True)


def _diff_head(q1, q2, k1, k2, v, lam, g, lam_init):
    scale = HEAD_DIM ** -0.5
    e1, l1 = _softmax_parts(_dot_nt(q1, k1) * scale)
    e2, l2 = _softmax_parts(_dot_nt(q2, k2) * scale)
    a = e1 * (1.0 / l1) - (lam * (1.0 / l2)) * e2
    o = _dot(a.astype(BF16), v)
    o = o * lax.rsqrt(jnp.mean(o * o, axis=-1, keepdims=True) + EPS)
    return (o * g) * (1.0 - lam_init)


def _prompt_attn_kernel(pa_ref, pb_ref, pc_ref, lq1, lk1, lq2, lk2, g_ref, o_ref, *, lam_init):
    lam = _lam(lq1, lk1, lq2, lk2, lam_init)
    g = g_ref[...]
    scale = HEAD_DIM ** -0.5

    def col(c0, w):
        ref = (pa_ref, pb_ref, pc_ref)[c0 // 768]
        o = c0 % 768
        return ref[:, o:o + w].astype(BF16)

    for h in range(NA_HEADS):
        q = col(C_NAQ + 64 * h, 64)
        k = col(C_NAK + 64 * h, 64)
        v = col(C_NAV + 64 * h, 64)
        e, l = _softmax_parts(_dot_nt(q, k) * scale)
        o = _dot(e.astype(BF16), v) * (1.0 / l)
        o_ref[:, 64 * h:64 * h + 64] = o.astype(BF16)
    for h in range(DIFF_HEADS):
        q1 = col(C_DQ + 128 * h, 64)
        q2 = col(C_DQ + 128 * h + 64, 64)
        k1 = col(C_DK + 128 * h, 64)
        k2 = col(C_DK + 128 * h + 64, 64)
        v = col(C_DV + 128 * h, 128)
        o = _diff_head(q1, q2, k1, k2, v, lam, g, lam_init)
        o_ref[:, 256 + 128 * h:384 + 128 * h] = o.astype(BF16)


def _prompt_attn(p, lq1, lk1, lq2, lk2, g, lam_init):
    vec = lambda n: pl.BlockSpec((1, n), lambda b: (0, 0))
    return pl.pallas_call(
        functools.partial(_prompt_attn_kernel, lam_init=lam_init),
        grid=(BATCH,),
        in_specs=[pl.BlockSpec((SEQ, 768), lambda b: (b, 0)),
                  pl.BlockSpec((SEQ, 768), lambda b: (b, 1)),
                  pl.BlockSpec((SEQ, 768), lambda b: (b, 2)),
                  vec(64), vec(64), vec(64), vec(64), vec(128)],
        out_specs=pl.BlockSpec((SEQ, 768), lambda b: (b, 0)),
        out_shape=jax.ShapeDtypeStruct((NP_TOK, 768), BF16),
        compiler_params=_cparams(("arbitrary",)),
    )(p, p, p, lq1, lk1, lq2, lk2, g)


def _bias_kernel(rpb_ref, o_ref):
    h = pl.program_id(0)
    case = pl.program_id(1)
    qc = lax.broadcasted_iota(jnp.int32, (GRID_W, GRID_W), 0)
    kc = lax.broadcasted_iota(jnp.int32, (GRID_W, GRID_W), 1)
    delta = jnp.clip(kc - qc + (WIN_COLS - 1), 0, 2 * WIN_COLS - 2)
    qs = jnp.clip(qc - WIN_COLS // 2, 0, GRID_W - WIN_COLS)
    in_win = (kc >= qs) & (kc < qs + WIN_COLS)
    for i in range(WIN_ROWS):
        dr = i - case + (WIN_ROWS - 1)
        base = (h * (2 * WIN_ROWS - 1) + dr) * (2 * WIN_COLS - 1)
        acc = jnp.zeros((GRID_W, GRID_W), F32)
        for d in range(2 * WIN_COLS - 1):
            acc = jnp.where(delta == d, rpb_ref[base + d], acc)
        o_ref[:, i * GRID_W:(i + 1) * GRID_W] = jnp.where(in_win, acc, -jnp.inf)


def _na_bias(rpb):
    return pl.pallas_call(
        _bias_kernel,
        grid=(NA_HEADS, WIN_ROWS),
        in_specs=[pl.BlockSpec(memory_space=pltpu.SMEM)],
        out_specs=pl.BlockSpec((None, None, GRID_W, WIN_ROWS * GRID_W), lambda h, c: (h, c, 0, 0)),
        out_shape=jax.ShapeDtypeStruct((NA_HEADS, WIN_ROWS, GRID_W, WIN_ROWS * GRID_W), F32),
        compiler_params=_cparams(("arbitrary", "arbitrary")),
    )(rpb.reshape(-1))


def _na_kernel(q_ref, kv_ref, kc_ref, vc_ref, bias_ref, o_ref):
    nloc = WIN_ROWS * GRID_W
    scale = HEAD_DIM ** -0.5
    for rr in range(NA_RB):
        r = pl.program_id(1) * NA_RB + rr
        start = jnp.clip(r - WIN_ROWS // 2, 0, GRID_ROWS - WIN_ROWS)
        case = r - start
        row0 = pl.multiple_of(start * GRID_W, GRID_W)
        qrows = slice(rr * GRID_W, (rr + 1) * GRID_W)
        for h in range(NA_HEADS):
            q = q_ref[qrows, C_NAQ + 64 * h:C_NAQ + 64 * h + 64].astype(BF16)
            k = kv_ref[pl.ds(row0, nloc), C_NAK + 64 * h:C_NAK + 64 * h + 64].astype(BF16)
            v = kv_ref[pl.ds(row0, nloc), C_NAV + 64 * h:C_NAV + 64 * h + 64].astype(BF16)
            s_loc = _dot_nt(q, k) * scale + bias_ref[h, pl.ds(case, 1)][0]
            s_ctx = _dot_nt(q, kc_ref[h].astype(BF16)) * scale
            m = jnp.maximum(jnp.max(s_loc, axis=-1, keepdims=True), jnp.max(s_ctx, axis=-1, keepdims=True))
            e_loc = jnp.exp(s_loc - m)
            e_ctx = jnp.exp(s_ctx - m)
            l = jnp.sum(e_loc, axis=-1, keepdims=True) + jnp.sum(e_ctx, axis=-1, keepdims=True)
            o = (_dot(e_loc.astype(BF16), v) + _dot(e_ctx.astype(BF16), vc_ref[h].astype(BF16))) * (1.0 / l)
            o_ref[qrows, 64 * h:64 * h + 64] = o.astype(BF16)


NA_RB = 4


def _na_attn(p, cache_k, cache_v, bias, l):
    qblk0 = NP_TOK // (NA_RB * GRID_W)
    kvblk0 = NP_TOK // DEC_SEQ
    nrg = GRID_ROWS // NA_RB
    return pl.pallas_call(
        _na_kernel,
        grid=(DEC_BATCH, nrg),
        in_specs=[pl.BlockSpec((NA_RB * GRID_W, 768), lambda b, r: (qblk0 + b * nrg + r, 0)),
                  pl.BlockSpec((DEC_SEQ, 768), lambda b, r: (kvblk0 + b, 0)),
                  pl.BlockSpec((None, None, NA_HEADS, PAST, HEAD_DIM), lambda b, r: (b, l, 0, 0, 0)),
                  pl.BlockSpec((None, None, NA_HEADS, PAST, HEAD_DIM), lambda b, r: (b, l, 0, 0, 0)),
                  pl.BlockSpec((NA_HEADS, WIN_ROWS, GRID_W, WIN_ROWS * GRID_W), lambda b, r: (0, 0, 0, 0))],
        out_specs=pl.BlockSpec((NA_RB * GRID_W, 256), lambda b, r: (b * nrg + r, 0)),
        out_shape=jax.ShapeDtypeStruct((NS_TOK, 256), BF16),
        compiler_params=_cparams(("arbitrary", "arbitrary")),
    )(p, p, cache_k, cache_v, bias)


@functools.lru_cache(None)
def _rope_tables():
    t = np.arange(DEC_SEQ)
    lane = np.arange(128)
    dd = lane % HEAD_DIM
    pos = np.where(dd[None, :] < 32, (t // GRID_W)[:, None], (t % GRID_W)[:, None]).astype(np.float64)
    inv = ROPE_BASE ** (-(dd % 16).astype(np.float64) * 2.0 / 32.0)
    ang = pos * inv[None, :]
    first = (dd % 32) < 16
    cos = np.cos(ang)
    s_up = np.where(first[None, :], -np.sin(ang), 0.0)
    s_dn = np.where(first[None, :], 0.0, np.sin(ang))
    return tuple(np.asarray(a, np.float32) for a in (cos, s_up, s_dn))


def _rope(x, cos, s_up, s_dn):
    return x * cos + pltpu.roll(x, 112, axis=1) * s_up + pltpu.roll(x, 16, axis=1) * s_dn


def _sdiff_kernel(q_ref, k_ref, v_ref, ck_ref, cv_ref, cos_ref, sup_ref, sdn_ref,
                  lq1, lk1, lq2, lk2, g_ref, o_ref, k1_scr, k2_scr, v_scr, *, lam_init, tq):
    qb = pl.program_id(2)

    @pl.when(qb == 0)
    def _():
        kr = _rope(k_ref[...], cos_ref[...], sup_ref[...], sdn_ref[...])
        k1_scr[0:DEC_SEQ, :] = kr[:, :64].astype(BF16)
        k2_scr[0:DEC_SEQ, :] = kr[:, 64:].astype(BF16)
        k1_scr[DEC_SEQ:, :] = ck_ref[0].astype(BF16)
        k2_scr[DEC_SEQ:, :] = ck_ref[1].astype(BF16)
        v_scr[0:DEC_SEQ, :] = v_ref[...].astype(BF16)
        v_scr[DEC_SEQ:, :] = cv_ref[...].astype(BF16)

    r0 = pl.multiple_of(qb * tq, tq)
    rows = pl.ds(r0, tq)
    qr = _rope(q_ref[...], cos_ref[rows, :], sup_ref[rows, :], sdn_ref[rows, :])
    lam = _lam(lq1, lk1, lq2, lk2, lam_init)
    o = _diff_head(qr[:, :64].astype(BF16), qr[:, 64:].astype(BF16), k1_scr[...], k2_scr[...],
                   v_scr[...], lam, g_ref[...], lam_init)
    o_ref[...] = o.astype(BF16)


def _sdiff_attn(p, cache_k, cache_v, lq1, lk1, lq2, lk2, g, lam_init, l):
    tq = 256
    nq = DEC_SEQ // tq
    cos, s_up, s_dn = (jnp.asarray(a) for a in _rope_tables())
    vec = lambda n: pl.BlockSpec((1, n), lambda b, h, q: (0, 0))
    tab = pl.BlockSpec((DEC_SEQ, 128), lambda b, h, q: (0, 0))
    kvblk0 = NP_TOK // DEC_SEQ
    return pl.pallas_call(
        functools.partial(_sdiff_kernel, lam_init=lam_init, tq=tq),
        grid=(DEC_BATCH, DIFF_HEADS, nq),
        in_specs=[pl.BlockSpec((tq, 128), lambda b, h, q: (NP_TOK // tq + b * nq + q, C_DQ // 128 + h)),
                  pl.BlockSpec((DEC_SEQ, 128), lambda b, h, q: (kvblk0 + b, C_DK // 128 + h)),
                  pl.BlockSpec((DEC_SEQ, 128), lambda b, h, q: (kvblk0 + b, C_DV // 128 + h)),
                  pl.BlockSpec((None, None, None, 2, PAST, HEAD_DIM), lambda b, h, q: (b, l, h, 0, 0, 0)),
                  pl.BlockSpec((None, None, None, PAST, DIFF_V), lambda b, h, q: (b, l, h, 0, 0)),
                  tab, tab, tab, vec(64), vec(64), vec(64), vec(64), vec(128)],
        out_specs=pl.BlockSpec((tq, 128), lambda b, h, q: (b * nq + q, h)),
        out_shape=jax.ShapeDtypeStruct((NS_TOK, DIFF_HEADS * DIFF_V), BF16),
        scratch_shapes=[pltpu.VMEM((DEC_SEQ + PAST, HEAD_DIM), BF16),
                        pltpu.VMEM((DEC_SEQ + PAST, HEAD_DIM), BF16),
                        pltpu.VMEM((DEC_SEQ + PAST, DIFF_V), BF16)],
        compiler_params=_cparams(("arbitrary", "arbitrary", "arbitrary")),
    )(p, p, p, cache_k, cache_v, cos, s_up, s_dn, lq1, lk1, lq2, lk2, g)


@functools.lru_cache(None)
def _dft_consts(L):
    n = 2 * L
    k = np.arange(L)
    ang = 2.0 * np.pi * ((k[:, None] * k[None, :]) % n) / n
    alt = (-1.0) ** k
    fa = np.cos(ang)
    fb = -np.sin(ang)
    fb[0, :] = alt
    wgt = np.full((L,), 2.0 / n)
    wgt[0] = 1.0 / n
    ga = fa * wgt[:, None]
    gb = fb * wgt[:, None]
    gb[0, :] = alt / n
    f = np.concatenate([fa, fb], axis=0)
    g = np.concatenate([ga.T, gb.T], axis=1)
    return np.asarray(f, dtype=BF16), np.asarray(g, dtype=BF16)


@functools.lru_cache(None)
def _filter_consts(L):
    f32 = np.float32
    t = np.linspace(0.0, 1.0, L, dtype=f32)[:, None]
    pos = np.arange(L, dtype=f32)[:, None]
    bands = np.linspace(1e-4, HY_BANDS - 1, HY_BANDS, dtype=f32)[None, :]
    ang = f32(2.0 * math.pi) * bands * pos / f32(L)
    z = np.zeros((L, 128), f32)
    z[:, 0:1] = t
    z[:, 1:1 + HY_BANDS] = np.cos(ang)
    z[:, 1 + HY_BANDS:HY_EMB] = -np.sin(ang)
    min_decay = math.log(1e-2) / 1.5
    max_decay = math.log(1e-2) / 0.3
    deltas = np.abs(np.linspace(min_decay, max_decay, HY_W, dtype=f32))
    decay = np.exp(-t * deltas[None, :]).astype(f32)
    return z, decay


def _spectra_kernel(z_ref, w1_ref, b1_ref, w2_ref, b2_ref, fr_ref, w3_ref, dec_ref, fa_ref, fb_ref,
                    sa_ref, sb_ref, filt_scr, *, L, kc):
    j = pl.program_id(0)

    @pl.when(j == 0)
    def _():
        fr = fr_ref[...]
        hdn = jnp.sin(fr * (_dot3(z_ref[...], w1_ref[...]) + b1_ref[...]))
        hdn = jnp.sin(fr * (_dot3(hdn, w2_ref[...]) + b2_ref[...]))
        dec = dec_ref[...]
        not_first = lax.broadcasted_iota(jnp.int32, (L, HY_W), 0) > 0
        for o in range(2):
            hf = _dot3(hdn, w3_ref[:, 512 * o:512 * o + 256]) * dec
            hb = jnp.where(not_first, _dot3(hdn, w3_ref[:, 512 * o + 256:512 * o + 512]) * dec, 0.0)
            nrm = (jnp.sum(jnp.abs(hf), axis=0, keepdims=True)
                   + jnp.sum(jnp.abs(hb), axis=0, keepdims=True))
            filt_scr[:, 512 * o:512 * o + 256] = (hf / nrm).astype(BF16)
            filt_scr[:, 512 * o + 256:512 * o + 512] = (hb / nrm).astype(BF16)

    ta = _dot(fa_ref[...], filt_scr[...])
    tb = _dot(fb_ref[...], filt_scr[...])
    first = (lax.broadcasted_iota(jnp.int32, (kc, HY_W), 0) + j * kc) == 0
    for o in range(2):
        af, ab = ta[:, 512 * o:512 * o + 256], ta[:, 512 * o + 256:512 * o + 512]
        bf, bb = tb[:, 512 * o:512 * o + 256], tb[:, 512 * o + 256:512 * o + 512]
        sa_ref[:, 256 * o:256 * o + 256] = af + ab
        sb_ref[:, 256 * o:256 * o + 256] = jnp.where(first, bf + bb, bf - bb)


def _hy_spectra(L, kc, w1p, b1, w2, b2, fr, w3):
    z, decay = _filter_consts(L)
    f = jnp.asarray(_dft_consts(L)[0])
    nj = L // kc
    full = lambda shape: pl.BlockSpec(shape, lambda j: tuple(0 for _ in shape))
    return pl.pallas_call(
        functools.partial(_spectra_kernel, L=L, kc=kc),
        grid=(nj,),
        in_specs=[full((L, 128)), full((128, HY_HID)), full((1, HY_HID)), full((HY_HID, HY_HID)),
                  full((1, HY_HID)), full((1, HY_HID)), full((HY_HID, 4 * HY_W)), full((L, HY_W)),
                  pl.BlockSpec((kc, L), lambda j: (j, 0)),
                  pl.BlockSpec((kc, L), lambda j: (j + nj, 0))],
        out_specs=[pl.BlockSpec((kc, 2 * HY_W), lambda j: (j, 0)),
                   pl.BlockSpec((kc, 2 * HY_W), lambda j: (j, 0))],
        out_shape=[jax.ShapeDtypeStruct((L, 2 * HY_W), F32)] * 2,
        scratch_shapes=[pltpu.VMEM((L, 4 * HY_W), BF16)],
        compiler_params=_cparams(("arbitrary",)),
    )(jnp.asarray(z), w1p, b1, w2, b2, fr, w3, jnp.asarray(decay), f, f)


def _hyconv_kernel(*refs, L, nb, kc, nj):
    u_refs = refs[:nb]
    (cw_ref, cb_ref, d_ref, fa_ref, fb_ref, ga_ref, gb_ref, sa_ref, sb_ref, o_ref,
     xin_scr, x_scr, g1_scr, g2_scr, y_scr) = refs[nb:]
    o = pl.program_id(1)
    j = pl.program_id(2)

    @pl.when((o == 0) & (j == 0))
    def _():
        row = lax.broadcasted_iota(jnp.int32, (L, HY_W), 0)
        for i in range(nb):
            cols = slice(HY_W * i, HY_W * (i + 1))
            for part, dst in enumerate((x_scr, g1_scr, g2_scr)):
                pc = slice(HY_W * part, HY_W * (part + 1))
                u = u_refs[i][:, pc]
                up = jnp.where(row == 0, 0.0, pltpu.roll(u, 1, axis=0))
                un = jnp.where(row == L - 1, 0.0, pltpu.roll(u, L - 1, axis=0))
                dst[:, cols] = (up * cw_ref[0:1, pc] + u * cw_ref[1:2, pc] + un * cw_ref[2:3, pc]
                                + cb_ref[:, pc])
            xin_scr[:, cols] = x_scr[:, cols].astype(BF16)

    @pl.when(j == 0)
    def _():
        y_scr[...] = jnp.zeros_like(y_scr)

    xa = _dot(fa_ref[...], xin_scr[...])
    xb = _dot(fb_ref[...], xin_scr[...])
    sa = jnp.concatenate([sa_ref[...]] * nb, axis=1)
    sb = jnp.concatenate([sb_ref[...]] * nb, axis=1)
    first = (lax.broadcasted_iota(jnp.int32, (kc, nb * HY_W), 0) + j * kc) == 0
    ya = jnp.where(first, xa * sa, xa * sa - xb * sb)
    yb = jnp.where(first, xb * sb, xa * sb + xb * sa)
    y_scr[...] += _dot(ga_ref[...], ya.astype(BF16)) + _dot(gb_ref[...], yb.astype(BF16))

    @pl.when(j == nj - 1)
    def _():
        dvec = jnp.concatenate([d_ref[pl.ds(o, 1), :]] * nb, axis=1)
        y = y_scr[...] + x_scr[...] * dvec

        @pl.when(o == 0)
        def _():
            zz = g1_scr[...] * y
            x_scr[...] = zz
            xin_scr[...] = zz.astype(BF16)

        @pl.when(o == 1)
        def _():
            res = g2_scr[...] * y
            for i in range(nb):
                o_ref[L * i:L * (i + 1), :] = res[:, HY_W * i:HY_W * (i + 1)].astype(BF16)


def _hy_conv(p, L, nb, kc, row_blk0, n_seq, conv_w, conv_b, d, sa, sb):
    f, g = (jnp.asarray(a) for a in _dft_consts(L))
    nj = L // kc
    u_spec = lambda i: pl.BlockSpec((L, 3 * HY_W), lambda bg, o, j: (row_blk0 + bg * nb + i, C_HY // 768),
                                    pipeline_mode=pl.Buffered(1))
    small = lambda shape: pl.BlockSpec(shape, lambda bg, o, j: (0, 0))
    return pl.pallas_call(
        functools.partial(_hyconv_kernel, L=L, nb=nb, kc=kc, nj=nj),
        grid=(n_seq // nb, 2, nj),
        in_specs=[u_spec(i) for i in range(nb)] + [
            small((3, 3 * HY_W)), small((1, 3 * HY_W)), small((2, HY_W)),
            pl.BlockSpec((kc, L), lambda bg, o, j: (j, 0)),
            pl.BlockSpec((kc, L), lambda bg, o, j: (j + nj, 0)),
            pl.BlockSpec((L, kc), lambda bg, o, j: (0, j)),
            pl.BlockSpec((L, kc), lambda bg, o, j: (0, j + nj)),
            pl.BlockSpec((kc, HY_W), lambda bg, o, j: (j, o)),
            pl.BlockSpec((kc, HY_W), lambda bg, o, j: (j, o))],
        out_specs=pl.BlockSpec((nb * L, HY_W), lambda bg, o, j: (bg, 0)),
        out_shape=jax.ShapeDtypeStruct((n_seq * L, HY_W), BF16),
        scratch_shapes=[pltpu.VMEM((L, nb * HY_W), BF16)] + [pltpu.VMEM((L, nb * HY_W), F32)] * 4,
        compiler_params=_cparams(("arbitrary", "arbitrary", "arbitrary")),
    )(*([p] * nb), conv_w, conv_b, d, f, f, g, g, sa, sb)


def _outproj_kernel(x_ref, mixp_ref, hyp_ref, nas_ref, ds_ref, hys_ref, g1_ref, w_ref, o_ref):
    i = pl.program_id(0)
    w = lambda a, b: w_ref[a:b, :].astype(BF16)

    @pl.when(i < NP_TOK // TM)
    def _():
        y = _dot(mixp_ref[...], w(0, 768)) + _dot(hyp_ref[...], w(768, D))
        o_ref[...] = x_ref[...] + g1_ref[...] * y

    @pl.when(i >= NP_TOK // TM)
    def _():
        y = _dot(nas_ref[...], w(0, 256)) + _dot(ds_ref[...], w(256, 768)) + _dot(hys_ref[...], w(768, D))
        o_ref[...] = x_ref[...] + g1_ref[...] * y


def _outproj(x, mix_p, hy_p, na_s, d_s, hy_s, mod6, w_out, l):
    npt = NP_TOK // TM
    pidx = lambda i: (jnp.minimum(i, npt - 1), 0)
    sidx = lambda i: (jnp.maximum(i - npt, 0), 0)
    return pl.pallas_call(
        _outproj_kernel,
        grid=(N_TOK // TM,),
        in_specs=[pl.BlockSpec((TM, D), lambda i: (i, 0)),
                  pl.BlockSpec((TM, 768), pidx),
                  pl.BlockSpec((TM, HY_W), pidx),
                  pl.BlockSpec((TM, 256), sidx),
                  pl.BlockSpec((TM, 512), sidx),
                  pl.BlockSpec((TM, HY_W), sidx),
                  _mod_spec(2, TM),
                  pl.BlockSpec((None, D, D), lambda i: (l, 0, 0))],
        out_specs=pl.BlockSpec((TM, D), lambda i: (i, 0)),
        out_shape=jax.ShapeDtypeStruct((N_TOK, D), F32),
        compiler_params=_cparams(("arbitrary",)),
    )(x, mix_p, hy_p, na_s, d_s, hy_s, mod6, w_out)


def _ffn_kernel(x_ref, g_ref, sc_ref, sh_ref, gate_ref, wg_ref, wu_ref, wd_ref, o_ref, h_scr, *, nj):
    j = pl.program_id(1)

    @pl.when(j == 0)
    def _():
        h_scr[...] = _norm_mod(x_ref[...], g_ref[...], sc_ref[...], sh_ref[...]).astype(BF16)
        o_ref[...] = jnp.zeros_like(o_ref)

    h = h_scr[...]
    a = _silu(_dot(h, wg_ref[...].astype(BF16))) * _dot(h, wu_ref[...].astype(BF16))
    o_ref[...] += _dot(a.astype(BF16), wd_ref[...].astype(BF16))

    @pl.when(j == nj - 1)
    def _():
        o_ref[...] = x_ref[...] + gate_ref[...] * o_ref[...]


def _dense_ffn(x, g, mod6, wg, wu, wd, i_ffn):
    tf = 256
    tm = DEC_SEQ
    nj = FFN // tf
    return pl.pallas_call(
        functools.partial(_ffn_kernel, nj=nj),
        grid=(N_TOK // tm, nj),
        in_specs=[pl.BlockSpec((tm, D), lambda i, j: (i, 0)),
                  pl.BlockSpec((1, D), lambda i, j: (0, 0)),
                  _mod_spec(4, tm), _mod_spec(3, tm), _mod_spec(5, tm),
                  pl.BlockSpec((None, D, tf), lambda i, j: (i_ffn, 0, j)),
                  pl.BlockSpec((None, D, tf), lambda i, j: (i_ffn, 0, j)),
                  pl.BlockSpec((None, tf, D), lambda i, j: (i_ffn, j, 0))],
        out_specs=pl.BlockSpec((tm, D), lambda i, j: (i, 0)),
        out_shape=jax.ShapeDtypeStruct((N_TOK, D), F32),
        scratch_shapes=[pltpu.VMEM((tm, D), BF16)],
        compiler_params=_cparams(("arbitrary", "arbitrary")),
    )(x, g, mod6, mod6, mod6, wg, wu, wd)


def _router_kernel(x_ref, g_ref, sc_ref, sh_ref, wr_ref, h_ref, r_ref):
    h = _norm_mod(x_ref[...], g_ref[...], sc_ref[...], sh_ref[...])
    h_ref[...] = h.astype(BF16)
    lane = lax.broadcasted_iota(jnp.int32, (TM, 128), 1)
    lg = jnp.where(lane < N_EXP, _dot3(h, wr_ref[...]), -jnp.inf)
    m1 = jnp.max(lg, axis=-1, keepdims=True)
    i1 = jnp.min(jnp.where(lg == m1, lane, 128), axis=-1, keepdims=True)
    lg2 = jnp.where(lane == i1, -jnp.inf, lg)
    m2 = jnp.max(lg2, axis=-1, keepdims=True)
    i2 = jnp.min(jnp.where(lg2 == m2, lane, 128), axis=-1, keepdims=True)
    e = jnp.exp(m2 - m1)
    w1 = 1.0 / (1.0 + e)
    w2 = e / (1.0 + e)
    r_ref[...] = jnp.where(lane == 0, i1.astype(F32),
                           jnp.where(lane == 1, i2.astype(F32),
                                     jnp.where(lane == 2, w1, jnp.where(lane == 3, w2, 0.0))))


def _router(x, g, mod6, wr_pad):
    return pl.pallas_call(
        _router_kernel,
        grid=(N_TOK // TM,),
        in_specs=[pl.BlockSpec((TM, D), lambda i: (i, 0)),
                  pl.BlockSpec((1, D), lambda i: (0, 0)),
                  _mod_spec(4, TM), _mod_spec(3, TM),
                  pl.BlockSpec((D, 128), lambda i: (0, 0))],
        out_specs=[pl.BlockSpec((TM, D), lambda i: (i, 0)),
                   pl.BlockSpec((TM, 128), lambda i: (i, 0))],
        out_shape=[jax.ShapeDtypeStruct((N_TOK, D), BF16), jax.ShapeDtypeStruct((N_TOK, 128), F32)],
        compiler_params=_cparams(("arbitrary",)),
    )(x, g, mod6, mod6, wr_pad)


def _dispatch_kernel(blo_ref, bhi_ref, sexp_ref, h_hbm, rp_ref, o_ref, h_scr, sem):
    s = pl.program_id(0)

    @pl.when(s == 0)
    def _():
        cp = pltpu.make_async_copy(h_hbm, h_scr, sem)
        cp.start()
        cp.wait()

    e = sexp_ref[s]
    rows = s * MOE_TS + lax.broadcasted_iota(jnp.int32, (MOE_TS, MOE_DTB), 0)

    def body(b, acc):
        sel = (rows == rp_ref[e, pl.ds(b, 1), :]).astype(BF16)
        hb = h_scr[pl.ds(pl.multiple_of(b * MOE_DTB, MOE_DTB), MOE_DTB), :]
        return acc + _dot(sel, hb)

    acc = lax.fori_loop(blo_ref[s], bhi_ref[s] + 1, body, jnp.zeros((MOE_TS, D), F32))
    o_ref[...] = acc.astype(BF16)


def _dispatch(blo, bhi, sexp, h, rp_t):
    return pl.pallas_call(
        _dispatch_kernel,
        grid_spec=pltpu.PrefetchScalarGridSpec(
            num_scalar_prefetch=3,
            grid=(MOE_ROWS // MOE_TS,),
            in_specs=[pl.BlockSpec(memory_space=pl.ANY),
                      pl.BlockSpec((N_EXP, N_TOK // MOE_DTB, MOE_DTB), lambda s, *_: (0, 0, 0))],
            out_specs=pl.BlockSpec((MOE_TS, D), lambda s, *_: (s, 0)),
            scratch_shapes=[pltpu.VMEM((N_TOK, D), BF16), pltpu.SemaphoreType.DMA(())]),
        out_shape=jax.ShapeDtypeStruct((MOE_ROWS, D), BF16),
        compiler_params=_cparams(("arbitrary",)),
    )(blo, bhi, sexp, h, rp_t)


def _experts_kernel(te_ref, used_ref, xs_ref, wg_ref, wu_ref, wd_ref, o_ref, acc_scr, *, nj):
    i = pl.program_id(0)
    j = pl.program_id(1)
    live = i < used_ref[0]

    @pl.when(live)
    def _():
        @pl.when(j == 0)
        def _():
            acc_scr[...] = jnp.zeros_like(acc_scr)

        xb = xs_ref[...]
        a = _silu(_dot(xb, wg_ref[...].astype(BF16))) * _dot(xb, wu_ref[...].astype(BF16))
        acc_scr[...] += _dot(a.astype(BF16), wd_ref[...].astype(BF16))

    @pl.when(j == nj - 1)
    def _():
        o_ref[...] = jnp.where(live, acc_scr[...], 0.0).astype(BF16)


def _experts(tile_expert, used, xs, wg, wu, wd, i_moe):
    nj = EXP_DIM // MOE_TF

    def tile(i, used):
        return jnp.minimum(i, used[0] - 1)

    def chunk(i, j, used):
        return jnp.where(i < used[0], j, nj - 1)

    return pl.pallas_call(
        functools.partial(_experts_kernel, nj=nj),
        grid_spec=pltpu.PrefetchScalarGridSpec(
            num_scalar_prefetch=2,
            grid=(MOE_YROWS // MOE_TM, nj),
            in_specs=[pl.BlockSpec((MOE_TM, D), lambda i, j, te, used: (tile(i, used), 0)),
                      pl.BlockSpec((None, None, D, MOE_TF),
                                   lambda i, j, te, used: (i_moe, te[tile(i, used)], 0, chunk(i, j, used))),
                      pl.BlockSpec((None, None, D, MOE_TF),
                                   lambda i, j, te, used: (i_moe, te[tile(i, used)], 0, chunk(i, j, used))),
                      pl.BlockSpec((None, None, MOE_TF, D),
                                   lambda i, j, te, used: (i_moe, te[tile(i, used)], chunk(i, j, used), 0))],
            out_specs=pl.BlockSpec((MOE_TM, D), lambda i, j, te, used: (i, 0)),
            scratch_shapes=[pltpu.VMEM((MOE_TM, D), F32)]),
        out_shape=jax.ShapeDtypeStruct((MOE_YROWS, D), BF16),
        compiler_params=_cparams(("arbitrary", "arbitrary")),
    )(tile_expert, used, xs, wg, wu, wd)


def _combine_kernel(ws_ref, kind_ref, x_ref, rp_ref, comb_ref, gate_ref, fg_ref, ys_hbm, op_ref, os_ref,
                    win_scr, y_scr, sem):
    b = pl.program_id(0)
    nb = pl.num_programs(0)
    slot = b % 2

    def win_copy(blk, sl, e):
        start = pl.multiple_of(ws_ref[blk * N_EXP + e], MOE_ALIGN)
        return pltpu.make_async_copy(ys_hbm.at[pl.ds(start, MOE_WIN)], win_scr.at[sl, e], sem.at[sl, e])

    @pl.when(b == 0)
    def _():
        for e in range(N_EXP):
            win_copy(0, 0, e).start()

    @pl.when(b + 1 < nb)
    def _():
        for e in range(N_EXP):
            win_copy(b + 1, 1 - slot, e).start()

    col = lax.broadcasted_iota(jnp.int32, (MOE_TB, MOE_WIN), 1)
    y_scr[...] = jnp.zeros_like(y_scr)
    for e in range(N_EXP):
        win_copy(b, slot, e).wait()
        rel = rp_ref[:, e:e + 1] - ws_ref[b * N_EXP + e]
        cw = comb_ref[:, e:e + 1]
        kind = kind_ref[b * N_EXP + e]

        @pl.when(kind == 1)
        def _():
            sel = (rel == lax.broadcasted_iota(jnp.int32, (MOE_TB, MOE_WIN_SMALL), 1)).astype(BF16)
            y_scr[...] += cw * _dot(sel, win_scr[slot, e, 0:MOE_WIN_SMALL, :])

        @pl.when(kind == 0)
        def _():
            y_scr[...] += cw * _dot((rel == col).astype(BF16), win_scr[slot, e])

    x = x_ref[...] + gate_ref[...] * y_scr[...]
    x = (x * lax.rsqrt(jnp.mean(x * x, axis=-1, keepdims=True) + EPS)) * fg_ref[...]

    @pl.when(b < NP_TOK // MOE_TB)
    def _():
        op_ref[...] = x

    @pl.when(b >= NP_TOK // MOE_TB)
    def _():
        os_ref[...] = x


def _combine(ws, kind, x, rp8, comb, mod6, final_g, ys):
    npb = NP_TOK // MOE_TB
    return pl.pallas_call(
        _combine_kernel,
        grid_spec=pltpu.PrefetchScalarGridSpec(
            num_scalar_prefetch=2,
            grid=(N_TOK // MOE_TB,),
            in_specs=[pl.BlockSpec((MOE_TB, D), lambda b, *_: (b, 0)),
                      pl.BlockSpec((MOE_TB, N_EXP), lambda b, *_: (b, 0)),
                      pl.BlockSpec((MOE_TB, N_EXP), lambda b, *_: (b, 0)),
                      _mod_spec(5, MOE_TB),
                      pl.BlockSpec((1, D), lambda b, *_: (0, 0)),
                      pl.BlockSpec(memory_space=pl.ANY)],
            out_specs=[pl.BlockSpec((MOE_TB, D), lambda b, *_: (jnp.minimum(b, npb - 1), 0)),
                       pl.BlockSpec((MOE_TB, D), lambda b, *_: (jnp.maximum(b - npb, 0), 0))],
            scratch_shapes=[pltpu.VMEM((2, N_EXP, MOE_WIN, D), BF16),
                            pltpu.VMEM((MOE_TB, D), F32),
                            pltpu.SemaphoreType.DMA((2, N_EXP))]),
        out_shape=[jax.ShapeDtypeStruct((NP_TOK, D), F32), jax.ShapeDtypeStruct((NS_TOK, D), F32)],
        compiler_params=_cparams(("arbitrary",)),
    )(ws, kind, x, rp8, comb, mod6, final_g, ys)


def _moe(x, g, mod6, router, wg, wu, wd, i_moe, final_g):
    wr_pad = jnp.pad(router, ((0, 0), (0, 128 - N_EXP)))
    h, r = _router(x, g, mod6, wr_pad)

    i32 = jnp.int32
    i12 = r[:, 0:2].astype(i32)
    earange = jnp.arange(N_EXP, dtype=i32)
    hit1 = i12[:, 0:1] == earange[None, :]
    hit2 = i12[:, 1:2] == earange[None, :]
    comb = jnp.where(hit1, r[:, 2:3], 0.0) + jnp.where(hit2, r[:, 3:4], 0.0)
    mask = (hit1 | hit2).astype(i32)
    csum = jnp.cumsum(mask, axis=0)
    counts = csum[-1]
    padded = ((counts + MOE_TM - 1) // MOE_TM) * MOE_TM
    ends = jnp.cumsum(padded)
    starts = ends - padded
    rp8 = jnp.where(mask > 0, starts[None, :] + csum - 1, -1).astype(i32)
    n_tiles = MOE_ROWS // MOE_TM
    tile_expert = jnp.minimum(
        jnp.searchsorted(ends, jnp.arange(n_tiles, dtype=i32) * MOE_TM, side="right"),
        N_EXP - 1).astype(i32)
    used = (ends[-1:] // MOE_TM).astype(i32)

    sub_row0 = jnp.arange(MOE_ROWS // MOE_TS, dtype=i32) * MOE_TS
    sexp = tile_expert[sub_row0 // MOE_TM]
    qlo = sub_row0 - starts[sexp]
    qend = jnp.minimum(qlo + MOE_TS, counts[sexp])
    cbe = csum[MOE_DTB - 1::MOE_DTB, :].T[sexp]
    blo = jnp.sum((cbe <= qlo[:, None]).astype(i32), axis=1)
    bhi = jnp.minimum(jnp.sum((cbe < qend[:, None]).astype(i32), axis=1), N_TOK // MOE_DTB - 1)
    empty = qend <= qlo
    blo = jnp.where(empty, 1, blo).astype(i32)
    bhi = jnp.where(empty, 0, bhi).astype(i32)

    cb = csum[MOE_TB - 1::MOE_TB, :]
    cprev = jnp.concatenate([jnp.zeros((1, N_EXP), i32), cb[:-1]], axis=0)
    ws = (((starts[None, :] + cprev) // MOE_ALIGN) * MOE_ALIGN).reshape(-1).astype(i32)
    n_be = cb - cprev
    kind = jnp.where(n_be == 0, 2, jnp.where(n_be <= MOE_WIN_SMALL - MOE_ALIGN, 1, 0)).reshape(-1).astype(i32)

    rp_t = rp8.T.reshape(N_EXP, N_TOK // MOE_DTB, MOE_DTB)
    xs = _dispatch(blo, bhi, sexp, h, rp_t)
    ys = _experts(tile_expert, used, xs, wg, wu, wd, i_moe)
    return _combine(ws, kind, x, rp8, comb, mod6, final_g, ys)


assert DEPTH == 2

def _cache_leaves(p):
    cols = lambda c0, c1: lax.slice(p, (0, c0), (NP_TOK, c1))
    na = lambda a: a.reshape(BATCH, SEQ, NA_HEADS, HEAD_DIM).transpose(0, 2, 1, 3)
    dk = cols(C_DK, C_DV).reshape(BATCH, SEQ, DIFF_HEADS, 2, HEAD_DIM).transpose(0, 2, 3, 1, 4)
    dv = cols(C_DV, C_HY).reshape(BATCH, SEQ, DIFF_HEADS, DIFF_V).transpose(0, 2, 1, 3)
    return na(cols(C_NAK, C_NAV)), na(cols(C_NAV, C_DQ)), dk, dv


def kernel(x_prompt, x_sample, cache_na_k, cache_na_v, cache_diff_k, cache_diff_v, c, c_ctx, w_in, w_out, ada_w, ada_b, norm_mix_g, norm_ffn_g, na_rpb, diff_lq1, diff_lk1, diff_lq2, diff_lk2, diff_subln_g, hy_conv_w, hy_conv_b, hy_d, hy_f_w1, hy_f_b1, hy_f_w2, hy_f_b2, hy_f_freq, hy_f_w3, ffn_w_gate, ffn_w_up, ffn_w_down, moe_router, moe_w_gate, moe_w_up, moe_w_down, final_norm_g):
    x = jnp.concatenate([x_prompt.reshape(NP_TOK, D), x_sample.reshape(NS_TOK, D)], axis=0)
    cond8 = jnp.concatenate([c_ctx[None, :], c, jnp.zeros((5, D), F32)], axis=0)
    mods = _modulation(cond8, ada_w, ada_b)
    final_g = final_norm_g.reshape(1, D)

    leaves = []
    for l in range(DEPTH):
        lam_init = 0.8 - 0.6 * math.exp(-0.3 * l)
        mod6 = mods[l].reshape(8, 6, D).transpose(1, 0, 2).reshape(6, 8, 1, D)
        row = lambda a: a[l].reshape(1, -1)
        lq1, lk1, lq2, lk2, subg = row(diff_lq1), row(diff_lk1), row(diff_lq2), row(diff_lk2), row(diff_subln_g)

        p = _inproj(x, row(norm_mix_g), mod6, w_in, l)
        leaves.append(_cache_leaves(p))

        mix_p = _prompt_attn(p, lq1, lk1, lq2, lk2, subg, lam_init)
        bias = _na_bias(na_rpb[l])
        na_s = _na_attn(p, cache_na_k, cache_na_v, bias, l)
        d_s = _sdiff_attn(p, cache_diff_k, cache_diff_v, lq1, lk1, lq2, lk2, subg, lam_init, l)

        w1p = jnp.pad(hy_f_w1[l], ((0, 128 - HY_EMB), (0, 0)))
        fargs = (w1p, row(hy_f_b1), hy_f_w2[l], row(hy_f_b2), row(hy_f_freq), hy_f_w3[l])
        cargs = (hy_conv_w[l], row(hy_conv_b), hy_d[l])
        sa_p, sb_p = _hy_spectra(SEQ, SEQ, *fargs)
        hy_p = _hy_conv(p, SEQ, 2, SEQ, 0, BATCH, *cargs, sa_p, sb_p)
        sa_s, sb_s = _hy_spectra(DEC_SEQ, 512, *fargs)
        hy_s = _hy_conv(p, DEC_SEQ, 1, 512, NP_TOK // DEC_SEQ, DEC_BATCH, *cargs, sa_s, sb_s)

        x = _outproj(x, mix_p, hy_p, na_s, d_s, hy_s, mod6, w_out, l)

        if l == 0:
            x = _dense_ffn(x, row(norm_ffn_g), mod6, ffn_w_gate, ffn_w_up, ffn_w_down, 0)
        else:
            yp, ys = _moe(x, row(norm_ffn_g), mod6, moe_router[0], moe_w_gate, moe_w_up, moe_w_down,
                          0, final_g)

    stack = lambda k: jnp.stack([leaves[l][k] for l in range(DEPTH)], axis=1)
    return (yp.reshape(BATCH, SEQ, D), ys.reshape(DEC_BATCH, DEC_SEQ, D),
            stack(0), stack(1), stack(2), stack(3))
```

```python
import functools
import math

import numpy as np
import jax
import jax.numpy as jnp
from jax import lax
from jax.experimental import pallas as pl
from jax.experimental.pallas import tpu as pltpu

F32 = jnp.float32
BF16 = jnp.bfloat16

D = 1024
BATCH, SEQ = 32, 256
DEC_BATCH, DEC_SEQ = 2, 2048
DEPTH = 2
PAST = 512
GRID_W = 64
GRID_ROWS = DEC_SEQ // GRID_W
HEAD_DIM = 64
NA_HEADS = 4
DIFF_HEADS = 4
DIFF_V = 128
WIN_ROWS, WIN_COLS = 8, 16
HY_W = 256
HY_EMB = 33
HY_BANDS = 16
HY_HID = 64
PROJ = 3072
FFN = 2816
N_EXP = 8
EXP_DIM = 3584
EPS = 1e-6
ROPE_BASE = 10000.0

NP_TOK = BATCH * SEQ
NS_TOK = DEC_BATCH * DEC_SEQ
N_TOK = NP_TOK + NS_TOK

C_NAQ, C_NAK, C_NAV = 0, 256, 512
C_DQ, C_DK, C_DV = 768, 1280, 1792
C_HY = 2304

TM = 1024
VMEM_LIMIT = 56 * 1024 * 1024

MOE_TM = 1024
MOE_TF = 512
MOE_ROWS = 2 * N_TOK + N_EXP * MOE_TM
MOE_TS = 256
MOE_DTB = 512
MOE_DWIN = 3
MOE_TB = 512
MOE_ALIGN = 16
MOE_WIN = MOE_TB + MOE_ALIGN
MOE_WIN_SMALL = MOE_TB // 2 + MOE_ALIGN
MOE_YROWS = MOE_ROWS + MOE_TM


def _cparams(sem):
    return pltpu.CompilerParams(dimension_semantics=sem, vmem_limit_bytes=VMEM_LIMIT)


def _dot(a, b):
    return jnp.dot(a, b, preferred_element_type=F32)


def _dot_nt(a, b):
    return lax.dot_general(a, b, (((1,), (1,)), ((), ())), preferred_element_type=F32)


def _split(a):
    hi = a.astype(BF16)
    lo = (a - hi.astype(F32)).astype(BF16)
    return hi, lo


def _dot3(a, b):
    ah, al = _split(a)
    bh, bl = _split(b)
    return _dot(ah, bh) + (_dot(ah, bl) + _dot(al, bh))


def _silu(x):
    return x / (1.0 + jnp.exp(-x))


def _mod_row(i, tm):
    t = i * tm
    return jnp.where(t < NP_TOK, 0, 1 + (t - NP_TOK) // DEC_SEQ)


def _mod_spec(k, tm):
    return pl.BlockSpec((None, None, 1, D), lambda i, *_: (k, _mod_row(i, tm), 0, 0))


def _norm_mod(x, g, sc, sh):
    y = x * lax.rsqrt(jnp.mean(x * x, axis=-1, keepdims=True) + EPS)
    return (y * g) * (1.0 + sc) + sh


def _mod_kernel(c_ref, w_ref, b_ref, o_ref):
    o_ref[...] = _dot3(_silu(c_ref[...]), w_ref[...]) + b_ref[...]


def _modulation(cond8, ada_w, ada_b):
    tn = 1536
    return pl.pallas_call(
        _mod_kernel,
        grid=(DEPTH, 6 * D // tn),
        in_specs=[pl.BlockSpec((8, D), lambda l, j: (0, 0)),
                  pl.BlockSpec((None, D, tn), lambda l, j: (l, 0, j)),
                  pl.BlockSpec((None, 1, tn), lambda l, j: (l, 0, j))],
        out_specs=pl.BlockSpec((None, 8, tn), lambda l, j: (l, 0, j)),
        out_shape=jax.ShapeDtypeStruct((DEPTH, 8, 6 * D), F32),
        compiler_params=_cparams(("arbitrary", "arbitrary")),
    )(cond8, ada_w, ada_b.reshape(DEPTH, 1, 6 * D))


def _inproj_kernel(x_ref, g_ref, sc_ref, sh_ref, w_hbm, o_ref, h_scr, w_scr, stage, sem, *, l, tn):
    i = pl.program_id(0)
    j = pl.program_id(1)

    @pl.when((i == 0) & (j == 0))
    def _():
        for c in range(PROJ // tn):
            cp = pltpu.make_async_copy(w_hbm.at[l, :, c * tn:(c + 1) * tn], stage, sem)
            cp.start()
            cp.wait()
            w_scr[c] = stage[...].astype(BF16)

    @pl.when(j == 0)
    def _():
        h_scr[...] = _norm_mod(x_ref[...], g_ref[...], sc_ref[...], sh_ref[...]).astype(BF16)

    o_ref[...] = _dot(h_scr[...], w_scr[j])


def _inproj(x, g, mod6, w_in, l):
    tn = 768
    return pl.pallas_call(
        functools.partial(_inproj_kernel, l=l, tn=tn),
        grid=(N_TOK // TM, PROJ // tn),
        in_specs=[pl.BlockSpec((TM, D), lambda i, j: (i, 0)),
                  pl.BlockSpec((1, D), lambda i, j: (0, 0)),
                  _mod_spec(1, TM), _mod_spec(0, TM),
                  pl.BlockSpec(memory_space=pl.ANY)],
        out_specs=pl.BlockSpec((TM, tn), lambda i, j: (i, j)),
        out_shape=jax.ShapeDtypeStruct((N_TOK, PROJ), F32),
        scratch_shapes=[pltpu.VMEM((TM, D), BF16), pltpu.VMEM((PROJ // tn, D, tn), BF16),
                        pltpu.VMEM((D, tn), F32), pltpu.SemaphoreType.DMA(())],
        compiler_params=_cparams(("arbitrary", "arbitrary")),
    )(x, g, mod6, mod6, w_in)


def _lam(lq1, lk1, lq2, lk2, lam_init):
    return (jnp.exp(jnp.sum(lq1[...] * lk1[...], axis=-1, keepdims=True))
            - jnp.exp(jnp.sum(lq2[...] * lk2[...], axis=-1, keepdims=True)) + lam_init)


def _softmax_parts(s):
    m = jnp.max(s, axis=-1, keepdims=True)
    e = jnp.exp(s - m)
    return e, jnp.sum(e, axis=-1, keepdims=True)


QK_SCALE = HEAD_DIM ** -0.5


def _diff_head(q1, q2, k1, k2, v, lam, g, lam_init):
    e1, l1 = _softmax_parts(_dot_nt(q1, k1))
    e2, l2 = _softmax_parts(_dot_nt(q2, k2))
    a = e1 - (lam * l1 * (1.0 / l2)) * e2
    o = _dot(a.astype(BF16), v) * (1.0 / l1)
    o = o * lax.rsqrt(jnp.mean(o * o, axis=-1, keepdims=True) + EPS)
    return (o * g) * (1.0 - lam_init)


def _prompt_attn_kernel(pa_ref, pb_ref, pc_ref, lq1, lk1, lq2, lk2, g_ref, o_ref, *, lam_init):
    lam = _lam(lq1, lk1, lq2, lk2, lam_init)
    g = g_ref[...]

    def col(c0, w, scale=None):
        ref = (pa_ref, pb_ref, pc_ref)[c0 // 768]
        o = c0 % 768
        a = ref[:, o:o + w]
        return (a if scale is None else a * scale).astype(BF16)

    for h in range(NA_HEADS):
        q = col(C_NAQ + 64 * h, 64, QK_SCALE)
        k = col(C_NAK + 64 * h, 64)
        v = col(C_NAV + 64 * h, 64)
        e, l = _softmax_parts(_dot_nt(q, k))
        o = _dot(e.astype(BF16), v) * (1.0 / l)
        o_ref[:, 64 * h:64 * h + 64] = o.astype(BF16)
    for h in range(DIFF_HEADS):
        q1 = col(C_DQ + 128 * h, 64, QK_SCALE)
        q2 = col(C_DQ + 128 * h + 64, 64, QK_SCALE)
        k1 = col(C_DK + 128 * h, 64)
        k2 = col(C_DK + 128 * h + 64, 64)
        v = col(C_DV + 128 * h, 128)
        o = _diff_head(q1, q2, k1, k2, v, lam, g, lam_init)
        o_ref[:, 256 + 128 * h:384 + 128 * h] = o.astype(BF16)


def _prompt_attn(p, lq1, lk1, lq2, lk2, g, lam_init):
    vec = lambda n: pl.BlockSpec((1, n), lambda b: (0, 0))
    return pl.pallas_call(
        functools.partial(_prompt_attn_kernel, lam_init=lam_init),
        grid=(BATCH,),
        in_specs=[pl.BlockSpec((SEQ, 768), lambda b: (b, 0)),
                  pl.BlockSpec((SEQ, 768), lambda b: (b, 1)),
                  pl.BlockSpec((SEQ, 768), lambda b: (b, 2)),
                  vec(64), vec(64), vec(64), vec(64), vec(128)],
        out_specs=pl.BlockSpec((SEQ, 768), lambda b: (b, 0)),
        out_shape=jax.ShapeDtypeStruct((NP_TOK, 768), BF16),
        compiler_params=_cparams(("arbitrary",)),
    )(p, p, p, lq1, lk1, lq2, lk2, g)


def _bias_kernel(rpb_ref, o_ref):
    h = pl.program_id(0)
    case = pl.program_id(1)
    qc = lax.broadcasted_iota(jnp.int32, (GRID_W, GRID_W), 0)
    kc = lax.broadcasted_iota(jnp.int32, (GRID_W, GRID_W), 1)
    delta = jnp.clip(kc - qc + (WIN_COLS - 1), 0, 2 * WIN_COLS - 2)
    qs = jnp.clip(qc - WIN_COLS // 2, 0, GRID_W - WIN_COLS)
    in_win = (kc >= qs) & (kc < qs + WIN_COLS)
    for i in range(WIN_ROWS):
        dr = i - case + (WIN_ROWS - 1)
        base = (h * (2 * WIN_ROWS - 1) + dr) * (2 * WIN_COLS - 1)
        acc = jnp.zeros((GRID_W, GRID_W), F32)
        for d in range(2 * WIN_COLS - 1):
            acc = jnp.where(delta == d, rpb_ref[base + d], acc)
        o_ref[:, i * GRID_W:(i + 1) * GRID_W] = jnp.where(in_win, acc, -jnp.inf)


def _na_bias(rpb):
    return pl.pallas_call(
        _bias_kernel,
        grid=(NA_HEADS, WIN_ROWS),
        in_specs=[pl.BlockSpec(memory_space=pltpu.SMEM)],
        out_specs=pl.BlockSpec((None, None, GRID_W, WIN_ROWS * GRID_W), lambda h, c: (h, c, 0, 0)),
        out_shape=jax.ShapeDtypeStruct((NA_HEADS, WIN_ROWS, GRID_W, WIN_ROWS * GRID_W), F32),
        compiler_params=_cparams(("arbitrary", "arbitrary")),
    )(rpb.reshape(-1))


def _na_kernel(q_ref, kv_ref, kc_ref, vc_ref, bias_ref, o_ref):
    nloc = WIN_ROWS * GRID_W
    for rr in range(NA_RB):
        r = pl.program_id(1) * NA_RB + rr
        start = jnp.clip(r - WIN_ROWS // 2, 0, GRID_ROWS - WIN_ROWS)
        case = r - start
        row0 = pl.multiple_of(start * GRID_W, GRID_W)
        qrows = slice(rr * GRID_W, (rr + 1) * GRID_W)
        for h in range(NA_HEADS):
            q = (q_ref[qrows, C_NAQ + 64 * h:C_NAQ + 64 * h + 64] * QK_SCALE).astype(BF16)
            k = kv_ref[pl.ds(row0, nloc), C_NAK + 64 * h:C_NAK + 64 * h + 64].astype(BF16)
            v = kv_ref[pl.ds(row0, nloc), C_NAV + 64 * h:C_NAV + 64 * h + 64].astype(BF16)
            s_loc = _dot_nt(q, k) + bias_ref[h, pl.ds(case, 1)][0]
            s_ctx = _dot_nt(q, kc_ref[h].astype(BF16))
            m = jnp.maximum(jnp.max(s_loc, axis=-1, keepdims=True), jnp.max(s_ctx, axis=-1, keepdims=True))
            e_loc = jnp.exp(s_loc - m)
            e_ctx = jnp.exp(s_ctx - m)
            l = jnp.sum(e_loc, axis=-1, keepdims=True) + jnp.sum(e_ctx, axis=-1, keepdims=True)
            o = (_dot(e_loc.astype(BF16), v) + _dot(e_ctx.astype(BF16), vc_ref[h].astype(BF16))) * (1.0 / l)
            o_ref[qrows, 64 * h:64 * h + 64] = o.astype(BF16)


NA_RB = 4


def _na_attn(p, cache_k, cache_v, bias, l):
    qblk0 = NP_TOK // (NA_RB * GRID_W)
    kvblk0 = NP_TOK // DEC_SEQ
    nrg = GRID_ROWS // NA_RB
    return pl.pallas_call(
        _na_kernel,
        grid=(DEC_BATCH, nrg),
        in_specs=[pl.BlockSpec((NA_RB * GRID_W, 768), lambda b, r: (qblk0 + b * nrg + r, 0)),
                  pl.BlockSpec((DEC_SEQ, 768), lambda b, r: (kvblk0 + b, 0)),
                  pl.BlockSpec((None, None, NA_HEADS, PAST, HEAD_DIM), lambda b, r: (b, l, 0, 0, 0)),
                  pl.BlockSpec((None, None, NA_HEADS, PAST, HEAD_DIM), lambda b, r: (b, l, 0, 0, 0)),
                  pl.BlockSpec((NA_HEADS, WIN_ROWS, GRID_W, WIN_ROWS * GRID_W), lambda b, r: (0, 0, 0, 0))],
        out_specs=pl.BlockSpec((NA_RB * GRID_W, 256), lambda b, r: (b * nrg + r, 0)),
        out_shape=jax.ShapeDtypeStruct((NS_TOK, 256), BF16),
        compiler_params=_cparams(("arbitrary", "arbitrary")),
    )(p, p, cache_k, cache_v, bias)


@functools.lru_cache(None)
def _rope_tables():
    t = np.arange(DEC_SEQ)
    lane = np.arange(128)
    dd = lane % HEAD_DIM
    pos = np.where(dd[None, :] < 32, (t // GRID_W)[:, None], (t % GRID_W)[:, None]).astype(np.float64)
    inv = ROPE_BASE ** (-(dd % 16).astype(np.float64) * 2.0 / 32.0)
    ang = pos * inv[None, :]
    first = (dd % 32) < 16
    cos = np.cos(ang)
    s_up = np.where(first[None, :], -np.sin(ang), 0.0)
    s_dn = np.where(first[None, :], 0.0, np.sin(ang))
    return tuple(np.asarray(a, np.float32) for a in (cos, s_up, s_dn))


def _rope(x, cos, s_up, s_dn):
    return x * cos + pltpu.roll(x, 112, axis=1) * s_up + pltpu.roll(x, 16, axis=1) * s_dn


def _sdiff_kernel(q_ref, k_ref, v_ref, ck_ref, cv_ref, cos_ref, sup_ref, sdn_ref,
                  lq1, lk1, lq2, lk2, g_ref, o_ref, k1_scr, k2_scr, v_scr, *, lam_init, tq):
    qb = pl.program_id(2)

    @pl.when(qb == 0)
    def _():
        kr = _rope(k_ref[...], cos_ref[...], sup_ref[...], sdn_ref[...])
        k1_scr[0:DEC_SEQ, :] = kr[:, :64].astype(BF16)
        k2_scr[0:DEC_SEQ, :] = kr[:, 64:].astype(BF16)
        k1_scr[DEC_SEQ:, :] = ck_ref[0].astype(BF16)
        k2_scr[DEC_SEQ:, :] = ck_ref[1].astype(BF16)
        v_scr[0:DEC_SEQ, :] = v_ref[...].astype(BF16)
        v_scr[DEC_SEQ:, :] = cv_ref[...].astype(BF16)

    r0 = pl.multiple_of(qb * tq, tq)
    rows = pl.ds(r0, tq)
    qr = _rope(q_ref[...], cos_ref[rows, :], sup_ref[rows, :], sdn_ref[rows, :]) * QK_SCALE
    lam = _lam(lq1, lk1, lq2, lk2, lam_init)
    o = _diff_head(qr[:, :64].astype(BF16), qr[:, 64:].astype(BF16), k1_scr[...], k2_scr[...],
                   v_scr[...], lam, g_ref[...], lam_init)
    o_ref[...] = o.astype(BF16)


def _sdiff_attn(p, cache_k, cache_v, lq1, lk1, lq2, lk2, g, lam_init, l):
    tq = 256
    nq = DEC_SEQ // tq
    cos, s_up, s_dn = (jnp.asarray(a) for a in _rope_tables())
    vec = lambda n: pl.BlockSpec((1, n), lambda b, h, q: (0, 0))
    tab = pl.BlockSpec((DEC_SEQ, 128), lambda b, h, q: (0, 0))
    kvblk0 = NP_TOK // DEC_SEQ
    return pl.pallas_call(
        functools.partial(_sdiff_kernel, lam_init=lam_init, tq=tq),
        grid=(DEC_BATCH, DIFF_HEADS, nq),
        in_specs=[pl.BlockSpec((tq, 128), lambda b, h, q: (NP_TOK // tq + b * nq + q, C_DQ // 128 + h)),
                  pl.BlockSpec((DEC_SEQ, 128), lambda b, h, q: (kvblk0 + b, C_DK // 128 + h)),
                  pl.BlockSpec((DEC_SEQ, 128), lambda b, h, q: (kvblk0 + b, C_DV // 128 + h)),
                  pl.BlockSpec((None, None, None, 2, PAST, HEAD_DIM), lambda b, h, q: (b, l, h, 0, 0, 0)),
                  pl.BlockSpec((None, None, None, PAST, DIFF_V), lambda b, h, q: (b, l, h, 0, 0)),
                  tab, tab, tab, vec(64), vec(64), vec(64), vec(64), vec(128)],
        out_specs=pl.BlockSpec((tq, 128), lambda b, h, q: (b * nq + q, h)),
        out_shape=jax.ShapeDtypeStruct((NS_TOK, DIFF_HEADS * DIFF_V), BF16),
        scratch_shapes=[pltpu.VMEM((DEC_SEQ + PAST, HEAD_DIM), BF16),
                        pltpu.VMEM((DEC_SEQ + PAST, HEAD_DIM), BF16),
                        pltpu.VMEM((DEC_SEQ + PAST, DIFF_V), BF16)],
        compiler_params=_cparams(("arbitrary", "arbitrary", "arbitrary")),
    )(p, p, p, cache_k, cache_v, cos, s_up, s_dn, lq1, lk1, lq2, lk2, g)


@functools.lru_cache(None)
def _dft_consts(L):
    n = 2 * L
    k = np.arange(L)
    ang = 2.0 * np.pi * ((k[:, None] * k[None, :]) % n) / n
    alt = (-1.0) ** k
    fa = np.cos(ang)
    fb = -np.sin(ang)
    fb[0, :] = alt
    wgt = np.full((L,), 2.0 / n)
    wgt[0] = 1.0 / n
    ga = fa * wgt[:, None]
    gb = fb * wgt[:, None]
    gb[0, :] = alt / n
    f = np.concatenate([fa, fb], axis=0)
    g = np.concatenate([ga.T, gb.T], axis=1)
    return np.asarray(f, dtype=BF16), np.asarray(g, dtype=BF16)


@functools.lru_cache(None)
def _filter_consts(L):
    f32 = np.float32
    t = np.linspace(0.0, 1.0, L, dtype=f32)[:, None]
    pos = np.arange(L, dtype=f32)[:, None]
    bands = np.linspace(1e-4, HY_BANDS - 1, HY_BANDS, dtype=f32)[None, :]
    ang = f32(2.0 * math.pi) * bands * pos / f32(L)
    z = np.zeros((L, 128), f32)
    z[:, 0:1] = t
    z[:, 1:1 + HY_BANDS] = np.cos(ang)
    z[:, 1 + HY_BANDS:HY_EMB] = -np.sin(ang)
    min_decay = math.log(1e-2) / 1.5
    max_decay = math.log(1e-2) / 0.3
    deltas = np.abs(np.linspace(min_decay, max_decay, HY_W, dtype=f32))
    decay = np.exp(-t * deltas[None, :]).astype(f32)
    return z, decay


def _spectra_kernel(z_ref, w1_ref, b1_ref, w2_ref, b2_ref, fr_ref, w3_ref, dec_ref, fa_ref, fb_ref,
                    sa_ref, sb_ref, filt_scr, *, L, kc):
    j = pl.program_id(0)

    @pl.when(j == 0)
    def _():
        fr = fr_ref[...]
        hdn = jnp.sin(fr * (_dot3(z_ref[...], w1_ref[...]) + b1_ref[...]))
        hdn = jnp.sin(fr * (_dot3(hdn, w2_ref[...]) + b2_ref[...]))
        dec = dec_ref[...]
        not_first = lax.broadcasted_iota(jnp.int32, (L, HY_W), 0) > 0
        for o in range(2):
            hf = _dot3(hdn, w3_ref[:, 512 * o:512 * o + 256]) * dec
            hb = jnp.where(not_first, _dot3(hdn, w3_ref[:, 512 * o + 256:512 * o + 512]) * dec, 0.0)
            nrm = (jnp.sum(jnp.abs(hf), axis=0, keepdims=True)
                   + jnp.sum(jnp.abs(hb), axis=0, keepdims=True))
            filt_scr[:, 512 * o:512 * o + 256] = (hf / nrm).astype(BF16)
            filt_scr[:, 512 * o + 256:512 * o + 512] = (hb / nrm).astype(BF16)

    ta = _dot(fa_ref[...], filt_scr[...])
    tb = _dot(fb_ref[...], filt_scr[...])
    first = (lax.broadcasted_iota(jnp.int32, (kc, HY_W), 0) + j * kc) == 0
    for o in range(2):
        af, ab = ta[:, 512 * o:512 * o + 256], ta[:, 512 * o + 256:512 * o + 512]
        bf, bb = tb[:, 512 * o:512 * o + 256], tb[:, 512 * o + 256:512 * o + 512]
        sa_ref[:, 256 * o:256 * o + 256] = af + ab
        sb_ref[:, 256 * o:256 * o + 256] = jnp.where(first, bf + bb, bf - bb)


def _hy_spectra(L, kc, w1p, b1, w2, b2, fr, w3):
    z, decay = _filter_consts(L)
    f = jnp.asarray(_dft_consts(L)[0])
    nj = L // kc
    full = lambda shape: pl.BlockSpec(shape, lambda j: tuple(0 for _ in shape))
    return pl.pallas_call(
        functools.partial(_spectra_kernel, L=L, kc=kc),
        grid=(nj,),
        in_specs=[full((L, 128)), full((128, HY_HID)), full((1, HY_HID)), full((HY_HID, HY_HID)),
                  full((1, HY_HID)), full((1, HY_HID)), full((HY_HID, 4 * HY_W)), full((L, HY_W)),
                  pl.BlockSpec((kc, L), lambda j: (j, 0)),
                  pl.BlockSpec((kc, L), lambda j: (j + nj, 0))],
        out_specs=[pl.BlockSpec((kc, 2 * HY_W), lambda j: (j, 0)),
                   pl.BlockSpec((kc, 2 * HY_W), lambda j: (j, 0))],
        out_shape=[jax.ShapeDtypeStruct((L, 2 * HY_W), F32)] * 2,
        scratch_shapes=[pltpu.VMEM((L, 4 * HY_W), BF16)],
        compiler_params=_cparams(("arbitrary",)),
    )(jnp.asarray(z), w1p, b1, w2, b2, fr, w3, jnp.asarray(decay), f, f)


def _hyconv_kernel(*refs, L, nb, kc, nj):
    u_refs = refs[:nb]
    (cw_ref, cb_ref, d_ref, fa_ref, fb_ref, ga_ref, gb_ref, sa_ref, sb_ref, o_ref,
     xin_scr, x_scr, g1_scr, g2_scr, y_scr) = refs[nb:]
    o = pl.program_id(1)
    j = pl.program_id(2)

    @pl.when((o == 0) & (j == 0))
    def _():
        row = lax.broadcasted_iota(jnp.int32, (L, HY_W), 0)
        for i in range(nb):
            cols = slice(HY_W * i, HY_W * (i + 1))
            for part, dst in enumerate((x_scr, g1_scr, g2_scr)):
                pc = slice(HY_W * part, HY_W * (part + 1))
                u = u_refs[i][:, pc]
                up = jnp.where(row == 0, 0.0, pltpu.roll(u, 1, axis=0))
                un = jnp.where(row == L - 1, 0.0, pltpu.roll(u, L - 1, axis=0))
                dst[:, cols] = (up * cw_ref[0:1, pc] + u * cw_ref[1:2, pc] + un * cw_ref[2:3, pc]
                                + cb_ref[:, pc])
            xin_scr[:, cols] = x_scr[:, cols].astype(BF16)

    @pl.when(j == 0)
    def _():
        y_scr[...] = jnp.zeros_like(y_scr)

    xa = _dot(fa_ref[...], xin_scr[...])
    xb = _dot(fb_ref[...], xin_scr[...])
    sa = jnp.concatenate([sa_ref[...]] * nb, axis=1)
    sb = jnp.concatenate([sb_ref[...]] * nb, axis=1)
    first = (lax.broadcasted_iota(jnp.int32, (kc, nb * HY_W), 0) + j * kc) == 0
    ya = jnp.where(first, xa * sa, xa * sa - xb * sb)
    yb = jnp.where(first, xb * sb, xa * sb + xb * sa)
    y_scr[...] += _dot(ga_ref[...], ya.astype(BF16)) + _dot(gb_ref[...], yb.astype(BF16))

    @pl.when(j == nj - 1)
    def _():
        dvec = jnp.concatenate([d_ref[pl.ds(o, 1), :]] * nb, axis=1)
        y = y_scr[...] + x_scr[...] * dvec

        @pl.when(o == 0)
        def _():
            zz = g1_scr[...] * y
            x_scr[...] = zz
            xin_scr[...] = zz.astype(BF16)

        @pl.when(o == 1)
        def _():
            res = g2_scr[...] * y
            for i in range(nb):
                o_ref[L * i:L * (i + 1), :] = res[:, HY_W * i:HY_W * (i + 1)].astype(BF16)


def _hy_conv(p, L, nb, kc, row_blk0, n_seq, conv_w, conv_b, d, sa, sb):
    f, g = (jnp.asarray(a) for a in _dft_consts(L))
    nj = L // kc
    u_spec = lambda i: pl.BlockSpec((L, 3 * HY_W), lambda bg, o, j: (row_blk0 + bg * nb + i, C_HY // 768),
                                    pipeline_mode=pl.Buffered(1))
    small = lambda shape: pl.BlockSpec(shape, lambda bg, o, j: (0, 0))
    return pl.pallas_call(
        functools.partial(_hyconv_kernel, L=L, nb=nb, kc=kc, nj=nj),
        grid=(n_seq // nb, 2, nj),
        in_specs=[u_spec(i) for i in range(nb)] + [
            small((3, 3 * HY_W)), small((1, 3 * HY_W)), small((2, HY_W)),
            pl.BlockSpec((kc, L), lambda bg, o, j: (j, 0)),
            pl.BlockSpec((kc, L), lambda bg, o, j: (j + nj, 0)),
            pl.BlockSpec((L, kc), lambda bg, o, j: (0, j)),
            pl.BlockSpec((L, kc), lambda bg, o, j: (0, j + nj)),
            pl.BlockSpec((kc, HY_W), lambda bg, o, j: (j, o)),
            pl.BlockSpec((kc, HY_W), lambda bg, o, j: (j, o))],
        out_specs=pl.BlockSpec((nb * L, HY_W), lambda bg, o, j: (bg, 0)),
        out_shape=jax.ShapeDtypeStruct((n_seq * L, HY_W), BF16),
        scratch_shapes=[pltpu.VMEM((L, nb * HY_W), BF16)] + [pltpu.VMEM((L, nb * HY_W), F32)] * 4,
        compiler_params=_cparams(("arbitrary", "arbitrary", "arbitrary")),
    )(*([p] * nb), conv_w, conv_b, d, f, f, g, g, sa, sb)


def _outproj_kernel(x_ref, mixp_ref, hyp_ref, nas_ref, ds_ref, hys_ref, g1_ref, w_ref, o_ref):
    i = pl.program_id(0)
    w = lambda a, b: w_ref[a:b, :].astype(BF16)

    @pl.when(i < NP_TOK // TM)
    def _():
        y = _dot(mixp_ref[...], w(0, 768)) + _dot(hyp_ref[...], w(768, D))
        o_ref[...] = x_ref[...] + g1_ref[...] * y

    @pl.when(i >= NP_TOK // TM)
    def _():
        y = _dot(nas_ref[...], w(0, 256)) + _dot(ds_ref[...], w(256, 768)) + _dot(hys_ref[...], w(768, D))
        o_ref[...] = x_ref[...] + g1_ref[...] * y


def _outproj(x, mix_p, hy_p, na_s, d_s, hy_s, mod6, w_out, l):
    npt = NP_TOK // TM
    pidx = lambda i: (jnp.minimum(i, npt - 1), 0)
    sidx = lambda i: (jnp.maximum(i - npt, 0), 0)
    return pl.pallas_call(
        _outproj_kernel,
        grid=(N_TOK // TM,),
        in_specs=[pl.BlockSpec((TM, D), lambda i: (i, 0)),
                  pl.BlockSpec((TM, 768), pidx),
                  pl.BlockSpec((TM, HY_W), pidx),
                  pl.BlockSpec((TM, 256), sidx),
                  pl.BlockSpec((TM, 512), sidx),
                  pl.BlockSpec((TM, HY_W), sidx),
                  _mod_spec(2, TM),
                  pl.BlockSpec((None, D, D), lambda i: (l, 0, 0))],
        out_specs=pl.BlockSpec((TM, D), lambda i: (i, 0)),
        out_shape=jax.ShapeDtypeStruct((N_TOK, D), F32),
        compiler_params=_cparams(("arbitrary",)),
    )(x, mix_p, hy_p, na_s, d_s, hy_s, mod6, w_out)


def _ffn_kernel(x_ref, g_ref, sc_ref, sh_ref, gate_ref, wg_ref, wu_ref, wd_ref, o_ref, h_scr, *, nj):
    j = pl.program_id(1)

    @pl.when(j == 0)
    def _():
        h_scr[...] = _norm_mod(x_ref[...], g_ref[...], sc_ref[...], sh_ref[...]).astype(BF16)
        o_ref[...] = jnp.zeros_like(o_ref)

    h = h_scr[...]
    a = _silu(_dot(h, wg_ref[...].astype(BF16))) * _dot(h, wu_ref[...].astype(BF16))
    o_ref[...] += _dot(a.astype(BF16), wd_ref[...].astype(BF16))

    @pl.when(j == nj - 1)
    def _():
        o_ref[...] = x_ref[...] + gate_ref[...] * o_ref[...]


def _dense_ffn(x, g, mod6, wg, wu, wd, i_ffn):
    tf = 256
    tm = DEC_SEQ
    nj = FFN // tf
    return pl.pallas_call(
        functools.partial(_ffn_kernel, nj=nj),
        grid=(N_TOK // tm, nj),
        in_specs=[pl.BlockSpec((tm, D), lambda i, j: (i, 0)),
                  pl.BlockSpec((1, D), lambda i, j: (0, 0)),
                  _mod_spec(4, tm), _mod_spec(3, tm), _mod_spec(5, tm),
                  pl.BlockSpec((None, D, tf), lambda i, j: (i_ffn, 0, j)),
                  pl.BlockSpec((None, D, tf), lambda i, j: (i_ffn, 0, j)),
                  pl.BlockSpec((None, tf, D), lambda i, j: (i_ffn, j, 0))],
        out_specs=pl.BlockSpec((tm, D), lambda i, j: (i, 0)),
        out_shape=jax.ShapeDtypeStruct((N_TOK, D), F32),
        scratch_shapes=[pltpu.VMEM((tm, D), BF16)],
        compiler_params=_cparams(("arbitrary", "arbitrary")),
    )(x, g, mod6, mod6, mod6, wg, wu, wd)


def _router_kernel(x_ref, g_ref, sc_ref, sh_ref, wr_ref, h_ref, r_ref):
    h = _norm_mod(x_ref[...], g_ref[...], sc_ref[...], sh_ref[...])
    h_ref[...] = h.astype(BF16)
    lane = lax.broadcasted_iota(jnp.int32, (TM, 128), 1)
    lg = jnp.where(lane < N_EXP, _dot3(h, wr_ref[...]), -jnp.inf)
    m1 = jnp.max(lg, axis=-1, keepdims=True)
    i1 = jnp.min(jnp.where(lg == m1, lane, 128), axis=-1, keepdims=True)
    lg2 = jnp.where(lane == i1, -jnp.inf, lg)
    m2 = jnp.max(lg2, axis=-1, keepdims=True)
    i2 = jnp.min(jnp.where(lg2 == m2, lane, 128), axis=-1, keepdims=True)
    e = jnp.exp(m2 - m1)
    w1 = 1.0 / (1.0 + e)
    w2 = e / (1.0 + e)
    r_ref[...] = jnp.where(lane == 0, i1.astype(F32),
                           jnp.where(lane == 1, i2.astype(F32),
                                     jnp.where(lane == 2, w1, jnp.where(lane == 3, w2, 0.0))))


def _router(x, g, mod6, wr_pad):
    return pl.pallas_call(
        _router_kernel,
        grid=(N_TOK // TM,),
        in_specs=[pl.BlockSpec((TM, D), lambda i: (i, 0)),
                  pl.BlockSpec((1, D), lambda i: (0, 0)),
                  _mod_spec(4, TM), _mod_spec(3, TM),
                  pl.BlockSpec((D, 128), lambda i: (0, 0))],
        out_specs=[pl.BlockSpec((TM, D), lambda i: (i, 0)),
                   pl.BlockSpec((TM, 128), lambda i: (i, 0))],
        out_shape=[jax.ShapeDtypeStruct((N_TOK, D), BF16), jax.ShapeDtypeStruct((N_TOK, 128), F32)],
        compiler_params=_cparams(("arbitrary",)),
    )(x, g, mod6, mod6, wr_pad)


def _dispatch_kernel(blo_ref, bhi_ref, sexp_ref, h_hbm, rp_ref, o_ref, h_scr, sem):
    s = pl.program_id(0)

    @pl.when(s == 0)
    def _():
        cp = pltpu.make_async_copy(h_hbm, h_scr, sem)
        cp.start()
        cp.wait()

    e = sexp_ref[s]
    rows = s * MOE_TS + lax.broadcasted_iota(jnp.int32, (MOE_TS, MOE_DTB), 0)

    def sel(b):
        return (rows == rp_ref[e, pl.ds(b, 1), :]).astype(BF16)

    b0 = jnp.minimum(blo_ref[s], N_TOK // MOE_DTB - MOE_DWIN)
    hwin = h_scr[pl.ds(pl.multiple_of(b0 * MOE_DTB, MOE_DTB), MOE_DWIN * MOE_DTB), :]
    acc = _dot(jnp.concatenate([sel(b0 + k) for k in range(MOE_DWIN)], axis=1), hwin)

    def body(b, acc):
        hb = h_scr[pl.ds(pl.multiple_of(b * MOE_DTB, MOE_DTB), MOE_DTB), :]
        return acc + _dot(sel(b), hb)

    acc = lax.fori_loop(b0 + MOE_DWIN, bhi_ref[s] + 1, body, acc)
    o_ref[...] = acc.astype(BF16)


def _dispatch(blo, bhi, sexp, h, rp_t):
    return pl.pallas_call(
        _dispatch_kernel,
        grid_spec=pltpu.PrefetchScalarGridSpec(
            num_scalar_prefetch=3,
            grid=(MOE_ROWS // MOE_TS,),
            in_specs=[pl.BlockSpec(memory_space=pl.ANY),
                      pl.BlockSpec((N_EXP, N_TOK // MOE_DTB, MOE_DTB), lambda s, *_: (0, 0, 0))],
            out_specs=pl.BlockSpec((MOE_TS, D), lambda s, *_: (s, 0)),
            scratch_shapes=[pltpu.VMEM((N_TOK, D), BF16), pltpu.SemaphoreType.DMA(())]),
        out_shape=jax.ShapeDtypeStruct((MOE_ROWS, D), BF16),
        compiler_params=_cparams(("arbitrary",)),
    )(blo, bhi, sexp, h, rp_t)


def _experts_kernel(te_ref, used_ref, xs_ref, wg_ref, wu_ref, wd_ref, o_ref, acc_scr, *, nj):
    i = pl.program_id(0)
    j = pl.program_id(1)
    live = i < used_ref[0]

    @pl.when(live)
    def _():
        @pl.when(j == 0)
        def _():
            acc_scr[...] = jnp.zeros_like(acc_scr)

        xb = xs_ref[...]
        a = _silu(_dot(xb, wg_ref[...].astype(BF16))) * _dot(xb, wu_ref[...].astype(BF16))
        acc_scr[...] += _dot(a.astype(BF16), wd_ref[...].astype(BF16))

    @pl.when(j == nj - 1)
    def _():
        o_ref[...] = jnp.where(live, acc_scr[...], 0.0).astype(BF16)


def _experts(tile_expert, used, xs, wg, wu, wd, i_moe):
    nj = EXP_DIM // MOE_TF

    def tile(i, used):
        return jnp.minimum(i, used[0] - 1)

    def chunk(i, j, used):
        return jnp.where(i < used[0], j, nj - 1)

    return pl.pallas_call(
        functools.partial(_experts_kernel, nj=nj),
        grid_spec=pltpu.PrefetchScalarGridSpec(
            num_scalar_prefetch=2,
            grid=(MOE_YROWS // MOE_TM, nj),
            in_specs=[pl.BlockSpec((MOE_TM, D), lambda i, j, te, used: (tile(i, used), 0)),
                      pl.BlockSpec((None, None, D, MOE_TF),
                                   lambda i, j, te, used: (i_moe, te[tile(i, used)], 0, chunk(i, j, used))),
                      pl.BlockSpec((None, None, D, MOE_TF),
                                   lambda i, j, te, used: (i_moe, te[tile(i, used)], 0, chunk(i, j, used))),
                      pl.BlockSpec((None, None, MOE_TF, D),
                                   lambda i, j, te, used: (i_moe, te[tile(i, used)], chunk(i, j, used), 0))],
            out_specs=pl.BlockSpec((MOE_TM, D), lambda i, j, te, used: (i, 0)),
            scratch_shapes=[pltpu.VMEM((MOE_TM, D), F32)]),
        out_shape=jax.ShapeDtypeStruct((MOE_YROWS, D), BF16),
        compiler_params=_cparams(("arbitrary", "arbitrary")),
    )(tile_expert, used, xs, wg, wu, wd)


def _combine_kernel(ws_ref, kind_ref, x_ref, rp_ref, comb_ref, gate_ref, fg_ref, ys_hbm, op_ref, os_ref,
                    win_scr, y_scr, sem):
    b = pl.program_id(0)
    nb = pl.num_programs(0)
    slot = b % 2

    def win_copy(blk, sl, e):
        start = pl.multiple_of(ws_ref[blk * N_EXP + e], MOE_ALIGN)
        return pltpu.make_async_copy(ys_hbm.at[pl.ds(start, MOE_WIN)], win_scr.at[sl, e], sem.at[sl, e])

    @pl.when(b == 0)
    def _():
        for e in range(N_EXP):
            win_copy(0, 0, e).start()

    @pl.when(b + 1 < nb)
    def _():
        for e in range(N_EXP):
            win_copy(b + 1, 1 - slot, e).start()

    col = lax.broadcasted_iota(jnp.int32, (MOE_TB, MOE_WIN), 1)
    y_scr[...] = jnp.zeros_like(y_scr)
    for e in range(N_EXP):
        win_copy(b, slot, e).wait()
        rel = rp_ref[:, e:e + 1] - ws_ref[b * N_EXP + e]
        cw = comb_ref[:, e:e + 1]
        kind = kind_ref[b * N_EXP + e]

        @pl.when(kind == 1)
        def _():
            sel = (rel == lax.broadcasted_iota(jnp.int32, (MOE_TB, MOE_WIN_SMALL), 1)).astype(BF16)
            y_scr[...] += cw * _dot(sel, win_scr[slot, e, 0:MOE_WIN_SMALL, :])

        @pl.when(kind == 0)
        def _():
            y_scr[...] += cw * _dot((rel == col).astype(BF16), win_scr[slot, e])

    x = x_ref[...] + gate_ref[...] * y_scr[...]
    x = (x * lax.rsqrt(jnp.mean(x * x, axis=-1, keepdims=True) + EPS)) * fg_ref[...]

    @pl.when(b < NP_TOK // MOE_TB)
    def _():
        op_ref[...] = x

    @pl.when(b >= NP_TOK // MOE_TB)
    def _():
        os_ref[...] = x


def _combine(ws, kind, x, rp8, comb, mod6, final_g, ys):
    npb = NP_TOK // MOE_TB
    return pl.pallas_call(
        _combine_kernel,
        grid_spec=pltpu.PrefetchScalarGridSpec(
            num_scalar_prefetch=2,
            grid=(N_TOK // MOE_TB,),
            in_specs=[pl.BlockSpec((MOE_TB, D), lambda b, *_: (b, 0)),
                      pl.BlockSpec((MOE_TB, N_EXP), lambda b, *_: (b, 0)),
                      pl.BlockSpec((MOE_TB, N_EXP), lambda b, *_: (b, 0)),
                      _mod_spec(5, MOE_TB),
                      pl.BlockSpec((1, D), lambda b, *_: (0, 0)),
                      pl.BlockSpec(memory_space=pl.ANY)],
            out_specs=[pl.BlockSpec((MOE_TB, D), lambda b, *_: (jnp.minimum(b, npb - 1), 0)),
                       pl.BlockSpec((MOE_TB, D), lambda b, *_: (jnp.maximum(b - npb, 0), 0))],
            scratch_shapes=[pltpu.VMEM((2, N_EXP, MOE_WIN, D), BF16),
                            pltpu.VMEM((MOE_TB, D), F32),
                            pltpu.SemaphoreType.DMA((2, N_EXP))]),
        out_shape=[jax.ShapeDtypeStruct((NP_TOK, D), F32), jax.ShapeDtypeStruct((NS_TOK, D), F32)],
        compiler_params=_cparams(("arbitrary",)),
    )(ws, kind, x, rp8, comb, mod6, final_g, ys)


def _moe(x, g, mod6, router, wg, wu, wd, i_moe, final_g):
    wr_pad = jnp.pad(router, ((0, 0), (0, 128 - N_EXP)))
    h, r = _router(x, g, mod6, wr_pad)

    i32 = jnp.int32
    i12 = r[:, 0:2].astype(i32)
    earange = jnp.arange(N_EXP, dtype=i32)
    hit1 = i12[:, 0:1] == earange[None, :]
    hit2 = i12[:, 1:2] == earange[None, :]
    comb = jnp.where(hit1, r[:, 2:3], 0.0) + jnp.where(hit2, r[:, 3:4], 0.0)
    mask = (hit1 | hit2).astype(i32)
    csum = jnp.cumsum(mask, axis=0)
    counts = csum[-1]
    padded = ((counts + MOE_TM - 1) // MOE_TM) * MOE_TM
    ends = jnp.cumsum(padded)
    starts = ends - padded
    rp8 = jnp.where(mask > 0, starts[None, :] + csum - 1, -1).astype(i32)
    n_tiles = MOE_ROWS // MOE_TM
    tile_expert = jnp.minimum(
        jnp.searchsorted(ends, jnp.arange(n_tiles, dtype=i32) * MOE_TM, side="right"),
        N_EXP - 1).astype(i32)
    used = (ends[-1:] // MOE_TM).astype(i32)

    sub_row0 = jnp.arange(MOE_ROWS // MOE_TS, dtype=i32) * MOE_TS
    sexp = tile_expert[sub_row0 // MOE_TM]
    qlo = sub_row0 - starts[sexp]
    qend = jnp.minimum(qlo + MOE_TS, counts[sexp])
    cbe = csum[MOE_DTB - 1::MOE_DTB, :].T[sexp]
    blo = jnp.sum((cbe <= qlo[:, None]).astype(i32), axis=1)
    bhi = jnp.minimum(jnp.sum((cbe < qend[:, None]).astype(i32), axis=1), N_TOK // MOE_DTB - 1)
    empty = qend <= qlo
    blo = jnp.where(empty, 1, blo).astype(i32)
    bhi = jnp.where(empty, 0, bhi).astype(i32)

    cb = csum[MOE_TB - 1::MOE_TB, :]
    cprev = jnp.concatenate([jnp.zeros((1, N_EXP), i32), cb[:-1]], axis=0)
    ws = (((starts[None, :] + cprev) // MOE_ALIGN) * MOE_ALIGN).reshape(-1).astype(i32)
    n_be = cb - cprev
    kind = jnp.where(n_be == 0, 2, jnp.where(n_be <= MOE_WIN_SMALL - MOE_ALIGN, 1, 0)).reshape(-1).astype(i32)

    rp_t = rp8.T.reshape(N_EXP, N_TOK // MOE_DTB, MOE_DTB)
    xs = _dispatch(blo, bhi, sexp, h, rp_t)
    ys = _experts(tile_expert, used, xs, wg, wu, wd, i_moe)
    return _combine(ws, kind, x, rp8, comb, mod6, final_g, ys)


assert DEPTH == 2

def _cache_leaves(p):
    cols = lambda c0, c1: lax.slice(p, (0, c0), (NP_TOK, c1))
    na = lambda a: a.reshape(BATCH, SEQ, NA_HEADS, HEAD_DIM).transpose(0, 2, 1, 3)
    dk = cols(C_DK, C_DV).reshape(BATCH, SEQ, DIFF_HEADS, 2, HEAD_DIM).transpose(0, 2, 3, 1, 4)
    dv = cols(C_DV, C_HY).reshape(BATCH, SEQ, DIFF_HEADS, DIFF_V).transpose(0, 2, 1, 3)
    return na(cols(C_NAK, C_NAV)), na(cols(C_NAV, C_DQ)), dk, dv


def kernel(x_prompt, x_sample, cache_na_k, cache_na_v, cache_diff_k, cache_diff_v, c, c_ctx, w_in, w_out, ada_w, ada_b, norm_mix_g, norm_ffn_g, na_rpb, diff_lq1, diff_lk1, diff_lq2, diff_lk2, diff_subln_g, hy_conv_w, hy_conv_b, hy_d, hy_f_w1, hy_f_b1, hy_f_w2, hy_f_b2, hy_f_freq, hy_f_w3, ffn_w_gate, ffn_w_up, ffn_w_down, moe_router, moe_w_gate, moe_w_up, moe_w_down, final_norm_g):
    x = jnp.concatenate([x_prompt.reshape(NP_TOK, D), x_sample.reshape(NS_TOK, D)], axis=0)
    cond8 = jnp.concatenate([c_ctx[None, :], c, jnp.zeros((5, D), F32)], axis=0)
    mods = _modulation(cond8, ada_w, ada_b)
    final_g = final_norm_g.reshape(1, D)

    leaves = []
    for l in range(DEPTH):
        lam_init = 0.8 - 0.6 * math.exp(-0.3 * l)
        mod6 = mods[l].reshape(8, 6, D).transpose(1, 0, 2).reshape(6, 8, 1, D)
        row = lambda a: a[l].reshape(1, -1)
        lq1, lk1, lq2, lk2, subg = row(diff_lq1), row(diff_lk1), row(diff_lq2), row(diff_lk2), row(diff_subln_g)

        p = _inproj(x, row(norm_mix_g), mod6, w_in, l)
        leaves.append(_cache_leaves(p))

        mix_p = _prompt_attn(p, lq1, lk1, lq2, lk2, subg, lam_init)
        bias = _na_bias(na_rpb[l])
        na_s = _na_attn(p, cache_na_k, cache_na_v, bias, l)
        d_s = _sdiff_attn(p, cache_diff_k, cache_diff_v, lq1, lk1, lq2, lk2, subg, lam_init, l)

        w1p = jnp.pad(hy_f_w1[l], ((0, 128 - HY_EMB), (0, 0)))
        fargs = (w1p, row(hy_f_b1), hy_f_w2[l], row(hy_f_b2), row(hy_f_freq), hy_f_w3[l])
        cargs = (hy_conv_w[l], row(hy_conv_b), hy_d[l])
        sa_p, sb_p = _hy_spectra(SEQ, SEQ, *fargs)
        hy_p = _hy_conv(p, SEQ, 2, SEQ, 0, BATCH, *cargs, sa_p, sb_p)
        sa_s, sb_s = _hy_spectra(DEC_SEQ, 512, *fargs)
        hy_s = _hy_conv(p, DEC_SEQ, 1, 512, NP_TOK // DEC_SEQ, DEC_BATCH, *cargs, sa_s, sb_s)

        x = _outproj(x, mix_p, hy_p, na_s, d_s, hy_s, mod6, w_out, l)

        if l == 0:
            x = _dense_ffn(x, row(norm_ffn_g), mod6, ffn_w_gate, ffn_w_up, ffn_w_down, 0)
        else:
            yp, ys = _moe(x, row(norm_ffn_g), mod6, moe_router[0], moe_w_gate, moe_w_up, moe_w_down,
                          0, final_g)

    stack = lambda k: jnp.stack([leaves[l][k] for l in range(DEPTH)], axis=1)
    return (yp.reshape(BATCH, SEQ, D), ys.reshape(DEC_BATCH, DEC_SEQ, D),
            stack(0), stack(1), stack(2), stack(3))
```

```python
import functools
import math

import numpy as np
import jax
import jax.numpy as jnp
from jax import lax
from jax.experimental import pallas as pl
from jax.experimental.pallas import tpu as pltpu

F32 = jnp.float32
BF16 = jnp.bfloat16

D = 1024
BATCH, SEQ = 32, 256
DEC_BATCH, DEC_SEQ = 2, 2048
DEPTH = 2
PAST = 512
GRID_W = 64
GRID_ROWS = DEC_SEQ // GRID_W
HEAD_DIM = 64
NA_HEADS = 4
DIFF_HEADS = 4
DIFF_V = 128
WIN_ROWS, WIN_COLS = 8, 16
HY_W = 256
HY_EMB = 33
HY_BANDS = 16
HY_HID = 64
PROJ = 3072
FFN = 2816
N_EXP = 8
EXP_DIM = 3584
EPS = 1e-6
ROPE_BASE = 10000.0

NP_TOK = BATCH * SEQ
NS_TOK = DEC_BATCH * DEC_SEQ
N_TOK = NP_TOK + NS_TOK

C_NAQ, C_NAK, C_NAV = 0, 256, 512
C_DQ, C_DK, C_DV = 768, 1280, 1792
C_HY = 2304

TM = 1024
VMEM_LIMIT = 56 * 1024 * 1024

MOE_TM = 1024
MOE_TF = 512
MOE_ROWS = 2 * N_TOK + N_EXP * MOE_TM
MOE_TS = 256
MOE_DTB = 512
MOE_DWIN = 3
MOE_TB = 512
MOE_ALIGN = 16
MOE_WIN = MOE_TB + MOE_ALIGN
MOE_WIN_SMALL = MOE_TB // 2 + MOE_ALIGN
MOE_YROWS = MOE_ROWS + MOE_TM


def _cparams(sem):
    return pltpu.CompilerParams(dimension_semantics=sem, vmem_limit_bytes=VMEM_LIMIT)


def _dot(a, b):
    return jnp.dot(a, b, preferred_element_type=F32)


def _dot_nt(a, b):
    return lax.dot_general(a, b, (((1,), (1,)), ((), ())), preferred_element_type=F32)


def _split(a):
    hi = a.astype(BF16)
    lo = (a - hi.astype(F32)).astype(BF16)
    return hi, lo


def _dot3(a, b):
    ah, al = _split(a)
    bh, bl = _split(b)
    return _dot(ah, bh) + (_dot(ah, bl) + _dot(al, bh))


def _silu(x):
    return x / (1.0 + jnp.exp(-x))


def _mod_row(i, tm):
    t = i * tm
    return jnp.where(t < NP_TOK, 0, 1 + (t - NP_TOK) // DEC_SEQ)


def _mod_spec(k, tm):
    return pl.BlockSpec((None, None, 1, D), lambda i, *_: (k, _mod_row(i, tm), 0, 0))


def _norm_mod(x, g, sc, sh):
    y = x * lax.rsqrt(jnp.mean(x * x, axis=-1, keepdims=True) + EPS)
    return (y * g) * (1.0 + sc) + sh


def _mod_kernel(c_ref, w_ref, b_ref, o_ref):
    o_ref[...] = _dot3(_silu(c_ref[...]), w_ref[...]) + b_ref[...]


def _modulation(cond8, ada_w, ada_b):
    tn = 1536
    return pl.pallas_call(
        _mod_kernel,
        grid=(DEPTH, 6 * D // tn),
        in_specs=[pl.BlockSpec((8, D), lambda l, j: (0, 0)),
                  pl.BlockSpec((None, D, tn), lambda l, j: (l, 0, j)),
                  pl.BlockSpec((None, 1, tn), lambda l, j: (l, 0, j))],
        out_specs=pl.BlockSpec((None, 8, tn), lambda l, j: (l, 0, j)),
        out_shape=jax.ShapeDtypeStruct((DEPTH, 8, 6 * D), F32),
        compiler_params=_cparams(("arbitrary", "arbitrary")),
    )(cond8, ada_w, ada_b.reshape(DEPTH, 1, 6 * D))


def _inproj_kernel(xp_ref, xs_ref, g_ref, sc_ref, sh_ref, w_hbm, nak_in, nav_in, dk_in, dv_in,
                   o_ref, nak_ref, nav_ref, dk_ref, dv_ref, h_scr, w_scr, stage, sem, *, l, tn):
    del nak_in, nav_in, dk_in, dv_in
    i = pl.program_id(0)
    j = pl.program_id(1)

    @pl.when((i == 0) & (j == 0))
    def _():
        for c in range(PROJ // tn):
            cp = pltpu.make_async_copy(w_hbm.at[l, :, c * tn:(c + 1) * tn], stage, sem)
            cp.start()
            cp.wait()
            w_scr[c] = stage[...].astype(BF16)

    is_ctx = i < NP_TOK // TM

    @pl.when((j == 0) & is_ctx)
    def _():
        h_scr[...] = _norm_mod(xp_ref[...], g_ref[...], sc_ref[...], sh_ref[...]).astype(BF16)

    @pl.when((j == 0) & jnp.logical_not(is_ctx))
    def _():
        h_scr[...] = _norm_mod(xs_ref[...], g_ref[...], sc_ref[...], sh_ref[...]).astype(BF16)

    o_ref[...] = _dot(h_scr[...], w_scr[j])


    def rows(bb):
        return slice(bb * SEQ, (bb + 1) * SEQ)

    def copy_heads(dst_ref, c0, width, heads, sub=None):
        for bb in range(TM // SEQ):
            for h in heads:
                if sub is None:
                    dst_ref[bb, h] = o_ref[rows(bb), c0(h):c0(h) + width]
                else:
                    for s in range(2):
                        dst_ref[bb, h, s] = o_ref[rows(bb), c0(h) + s * width:c0(h) + (s + 1) * width]

    @pl.when(is_ctx & (j == 0))
    def _():
        copy_heads(nak_ref, lambda h: C_NAK + HEAD_DIM * h, HEAD_DIM, range(NA_HEADS))
        copy_heads(nav_ref, lambda h: C_NAV + HEAD_DIM * h, HEAD_DIM, range(NA_HEADS))

    @pl.when(is_ctx & (j == 1))
    def _():
        copy_heads(dk_ref, lambda h: C_DK - tn + 2 * HEAD_DIM * h, HEAD_DIM, range(0, 2), sub=True)

    @pl.when(is_ctx & (j == 2))
    def _():
        copy_heads(dk_ref, lambda h: C_DK - 2 * tn + 2 * HEAD_DIM * h, HEAD_DIM, range(2, 4), sub=True)
        copy_heads(dv_ref, lambda h: C_DV - 2 * tn + DIFF_V * h, DIFF_V, range(DIFF_HEADS))


def _inproj(xp, xs, xs_block0, g, mod6, w_in, l, leaves):
    tn = 768
    nb = TM // SEQ
    ctx = lambda i: jnp.minimum(i, NP_TOK // TM - 1)
    leaf_specs = [pl.BlockSpec((nb, None, NA_HEADS, SEQ, HEAD_DIM), lambda i, j: (ctx(i), l, 0, 0, 0)),
                  pl.BlockSpec((nb, None, NA_HEADS, SEQ, HEAD_DIM), lambda i, j: (ctx(i), l, 0, 0, 0)),
                  pl.BlockSpec((nb, None, DIFF_HEADS, 2, SEQ, HEAD_DIM), lambda i, j: (ctx(i), l, 0, 0, 0, 0)),
                  pl.BlockSpec((nb, None, DIFF_HEADS, SEQ, DIFF_V), lambda i, j: (ctx(i), l, 0, 0, 0))]
    return pl.pallas_call(
        functools.partial(_inproj_kernel, l=l, tn=tn),
        grid=(N_TOK // TM, PROJ // tn),
        in_specs=_x_specs(xs_block0) + [
                  pl.BlockSpec((1, D), lambda i, j: (0, 0)),
                  _mod_spec(1, TM), _mod_spec(0, TM),
                  pl.BlockSpec(memory_space=pl.ANY)] + [pl.BlockSpec(memory_space=pl.ANY)] * 4,
        out_specs=[pl.BlockSpec((TM, tn), lambda i, j: (i, j))] + leaf_specs,
        out_shape=[jax.ShapeDtypeStruct((N_TOK, PROJ), F32)]
        + [jax.ShapeDtypeStruct(a.shape, a.dtype) for a in leaves],
        input_output_aliases={6: 1, 7: 2, 8: 3, 9: 4},
        scratch_shapes=[pltpu.VMEM((TM, D), BF16), pltpu.VMEM((PROJ // tn, D, tn), BF16),
                        pltpu.VMEM((D, tn), F32), pltpu.SemaphoreType.DMA(())],
        compiler_params=_cparams(("arbitrary", "arbitrary")),
    )(xp, xs, g, mod6, mod6, w_in, *leaves)


def _lam(lq1, lk1, lq2, lk2, lam_init):
    return (jnp.exp(jnp.sum(lq1[...] * lk1[...], axis=-1, keepdims=True))
            - jnp.exp(jnp.sum(lq2[...] * lk2[...], axis=-1, keepdims=True)) + lam_init)


def _softmax_parts(s):
    m = jnp.max(s, axis=-1, keepdims=True)
    e = jnp.exp(s - m)
    return e, jnp.sum(e, axis=-1, keepdims=True)


QK_SCALE = HEAD_DIM ** -0.5


def _diff_head(q1, q2, k1, k2, v, lam, g, lam_init):
    e1, l1 = _softmax_parts(_dot_nt(q1, k1))
    e2, l2 = _softmax_parts(_dot_nt(q2, k2))
    a = e1 - (lam * l1 * (1.0 / l2)) * e2
    o = _dot(a.astype(BF16), v) * (1.0 / l1)
    o = o * lax.rsqrt(jnp.mean(o * o, axis=-1, keepdims=True) + EPS)
    return (o * g) * (1.0 - lam_init)


def _prompt_attn_kernel(pa_ref, pb_ref, pc_ref, lq1, lk1, lq2, lk2, g_ref, o_ref, *, lam_init):
    lam = _lam(lq1, lk1, lq2, lk2, lam_init)
    g = g_ref[...]

    def col(c0, w, scale=None):
        ref = (pa_ref, pb_ref, pc_ref)[c0 // 768]
        o = c0 % 768
        a = ref[:, o:o + w]
        return (a if scale is None else a * scale).astype(BF16)

    for h in range(NA_HEADS):
        q = col(C_NAQ + 64 * h, 64, QK_SCALE)
        k = col(C_NAK + 64 * h, 64)
        v = col(C_NAV + 64 * h, 64)
        e, l = _softmax_parts(_dot_nt(q, k))
        o = _dot(e.astype(BF16), v) * (1.0 / l)
        o_ref[:, 64 * h:64 * h + 64] = o.astype(BF16)
    for h in range(DIFF_HEADS):
        q1 = col(C_DQ + 128 * h, 64, QK_SCALE)
        q2 = col(C_DQ + 128 * h + 64, 64, QK_SCALE)
        k1 = col(C_DK + 128 * h, 64)
        k2 = col(C_DK + 128 * h + 64, 64)
        v = col(C_DV + 128 * h, 128)
        o = _diff_head(q1, q2, k1, k2, v, lam, g, lam_init)
        o_ref[:, 256 + 128 * h:384 + 128 * h] = o.astype(BF16)


def _prompt_attn(p, lq1, lk1, lq2, lk2, g, lam_init):
    vec = lambda n: pl.BlockSpec((1, n), lambda b: (0, 0))
    return pl.pallas_call(
        functools.partial(_prompt_attn_kernel, lam_init=lam_init),
        grid=(BATCH,),
        in_specs=[pl.BlockSpec((SEQ, 768), lambda b: (b, 0)),
                  pl.BlockSpec((SEQ, 768), lambda b: (b, 1)),
                  pl.BlockSpec((SEQ, 768), lambda b: (b, 2)),
                  vec(64), vec(64), vec(64), vec(64), vec(128)],
        out_specs=pl.BlockSpec((SEQ, 768), lambda b: (b, 0)),
        out_shape=jax.ShapeDtypeStruct((NP_TOK, 768), BF16),
        compiler_params=_cparams(("arbitrary",)),
    )(p, p, p, lq1, lk1, lq2, lk2, g)


def _bias_kernel(rpb_ref, o_ref):
    h = pl.program_id(0)
    case = pl.program_id(1)
    qc = lax.broadcasted_iota(jnp.int32, (GRID_W, GRID_W), 0)
    kc = lax.broadcasted_iota(jnp.int32, (GRID_W, GRID_W), 1)
    delta = jnp.clip(kc - qc + (WIN_COLS - 1), 0, 2 * WIN_COLS - 2)
    qs = jnp.clip(qc - WIN_COLS // 2, 0, GRID_W - WIN_COLS)
    in_win = (kc >= qs) & (kc < qs + WIN_COLS)
    for i in range(WIN_ROWS):
        dr = i - case + (WIN_ROWS - 1)
        base = (h * (2 * WIN_ROWS - 1) + dr) * (2 * WIN_COLS - 1)
        acc = jnp.zeros((GRID_W, GRID_W), F32)
        for d in range(2 * WIN_COLS - 1):
            acc = jnp.where(delta == d, rpb_ref[base + d], acc)
        o_ref[:, i * GRID_W:(i + 1) * GRID_W] = jnp.where(in_win, acc, -jnp.inf)


def _na_bias(rpb):
    return pl.pallas_call(
        _bias_kernel,
        grid=(NA_HEADS, WIN_ROWS),
        in_specs=[pl.BlockSpec(memory_space=pltpu.SMEM)],
        out_specs=pl.BlockSpec((None, None, GRID_W, WIN_ROWS * GRID_W), lambda h, c: (h, c, 0, 0)),
        out_shape=jax.ShapeDtypeStruct((NA_HEADS, WIN_ROWS, GRID_W, WIN_ROWS * GRID_W), F32),
        compiler_params=_cparams(("arbitrary", "arbitrary")),
    )(rpb.reshape(-1))


def _na_kernel(q_ref, kv_ref, kc_ref, vc_ref, bias_ref, o_ref):
    nloc = WIN_ROWS * GRID_W
    for rr in range(NA_RB):
        r = pl.program_id(1) * NA_RB + rr
        start = jnp.clip(r - WIN_ROWS // 2, 0, GRID_ROWS - WIN_ROWS)
        case = r - start
        row0 = pl.multiple_of(start * GRID_W, GRID_W)
        qrows = slice(rr * GRID_W, (rr + 1) * GRID_W)
        for h in range(NA_HEADS):
            q = (q_ref[qrows, C_NAQ + 64 * h:C_NAQ + 64 * h + 64] * QK_SCALE).astype(BF16)
            k = kv_ref[pl.ds(row0, nloc), C_NAK + 64 * h:C_NAK + 64 * h + 64].astype(BF16)
            v = kv_ref[pl.ds(row0, nloc), C_NAV + 64 * h:C_NAV + 64 * h + 64].astype(BF16)
            s_loc = _dot_nt(q, k) + bias_ref[h, pl.ds(case, 1)][0]
            s_ctx = _dot_nt(q, kc_ref[h].astype(BF16))
            m = jnp.maximum(jnp.max(s_loc, axis=-1, keepdims=True), jnp.max(s_ctx, axis=-1, keepdims=True))
            e_loc = jnp.exp(s_loc - m)
            e_ctx = jnp.exp(s_ctx - m)
            l = jnp.sum(e_loc, axis=-1, keepdims=True) + jnp.sum(e_ctx, axis=-1, keepdims=True)
            o = (_dot(e_loc.astype(BF16), v) + _dot(e_ctx.astype(BF16), vc_ref[h].astype(BF16))) * (1.0 / l)
            o_ref[qrows, 64 * h:64 * h + 64] = o.astype(BF16)


NA_RB = 4


def _na_attn(p, cache_k, cache_v, bias, l):
    qblk0 = NP_TOK // (NA_RB * GRID_W)
    kvblk0 = NP_TOK // DEC_SEQ
    nrg = GRID_ROWS // NA_RB
    return pl.pallas_call(
        _na_kernel,
        grid=(DEC_BATCH, nrg),
        in_specs=[pl.BlockSpec((NA_RB * GRID_W, 768), lambda b, r: (qblk0 + b * nrg + r, 0)),
                  pl.BlockSpec((DEC_SEQ, 768), lambda b, r: (kvblk0 + b, 0)),
                  pl.BlockSpec((None, None, NA_HEADS, PAST, HEAD_DIM), lambda b, r: (b, l, 0, 0, 0)),
                  pl.BlockSpec((None, None, NA_HEADS, PAST, HEAD_DIM), lambda b, r: (b, l, 0, 0, 0)),
                  pl.BlockSpec((NA_HEADS, WIN_ROWS, GRID_W, WIN_ROWS * GRID_W), lambda b, r: (0, 0, 0, 0))],
        out_specs=pl.BlockSpec((NA_RB * GRID_W, 256), lambda b, r: (b * nrg + r, 0)),
        out_shape=jax.ShapeDtypeStruct((NS_TOK, 256), BF16),
        compiler_params=_cparams(("arbitrary", "arbitrary")),
    )(p, p, cache_k, cache_v, bias)


@functools.lru_cache(None)
def _rope_tables():
    t = np.arange(DEC_SEQ)
    lane = np.arange(128)
    dd = lane % HEAD_DIM
    pos = np.where(dd[None, :] < 32, (t // GRID_W)[:, None], (t % GRID_W)[:, None]).astype(np.float64)
    inv = ROPE_BASE ** (-(dd % 16).astype(np.float64) * 2.0 / 32.0)
    ang = pos * inv[None, :]
    first = (dd % 32) < 16
    cos = np.cos(ang)
    s_up = np.where(first[None, :], -np.sin(ang), 0.0)
    s_dn = np.where(first[None, :], 0.0, np.sin(ang))
    return tuple(np.asarray(a, np.float32) for a in (cos, s_up, s_dn))


def _rope(x, cos, s_up, s_dn):
    return x * cos + pltpu.roll(x, 112, axis=1) * s_up + pltpu.roll(x, 16, axis=1) * s_dn


def _sdiff_kernel(q_ref, k_ref, v_ref, ck_ref, cv_ref, cos_ref, sup_ref, sdn_ref,
                  lq1, lk1, lq2, lk2, g_ref, o_ref, k1_scr, k2_scr, v_scr, *, lam_init, tq):
    qb = pl.program_id(2)

    @pl.when(qb == 0)
    def _():
        kr = _rope(k_ref[...], cos_ref[...], sup_ref[...], sdn_ref[...])
        k1_scr[0:DEC_SEQ, :] = kr[:, :64].astype(BF16)
        k2_scr[0:DEC_SEQ, :] = kr[:, 64:].astype(BF16)
        k1_scr[DEC_SEQ:, :] = ck_ref[0].astype(BF16)
        k2_scr[DEC_SEQ:, :] = ck_ref[1].astype(BF16)
        v_scr[0:DEC_SEQ, :] = v_ref[...].astype(BF16)
        v_scr[DEC_SEQ:, :] = cv_ref[...].astype(BF16)

    r0 = pl.multiple_of(qb * tq, tq)
    rows = pl.ds(r0, tq)
    qr = _rope(q_ref[...], cos_ref[rows, :], sup_ref[rows, :], sdn_ref[rows, :]) * QK_SCALE
    lam = _lam(lq1, lk1, lq2, lk2, lam_init)
    o = _diff_head(qr[:, :64].astype(BF16), qr[:, 64:].astype(BF16), k1_scr[...], k2_scr[...],
                   v_scr[...], lam, g_ref[...], lam_init)
    o_ref[...] = o.astype(BF16)


def _sdiff_attn(p, cache_k, cache_v, lq1, lk1, lq2, lk2, g, lam_init, l):
    tq = 256
    nq = DEC_SEQ // tq
    cos, s_up, s_dn = (jnp.asarray(a) for a in _rope_tables())
    vec = lambda n: pl.BlockSpec((1, n), lambda b, h, q: (0, 0))
    tab = pl.BlockSpec((DEC_SEQ, 128), lambda b, h, q: (0, 0))
    kvblk0 = NP_TOK // DEC_SEQ
    return pl.pallas_call(
        functools.partial(_sdiff_kernel, lam_init=lam_init, tq=tq),
        grid=(DEC_BATCH, DIFF_HEADS, nq),
        in_specs=[pl.BlockSpec((tq, 128), lambda b, h, q: (NP_TOK // tq + b * nq + q, C_DQ // 128 + h)),
                  pl.BlockSpec((DEC_SEQ, 128), lambda b, h, q: (kvblk0 + b, C_DK // 128 + h)),
                  pl.BlockSpec((DEC_SEQ, 128), lambda b, h, q: (kvblk0 + b, C_DV // 128 + h)),
                  pl.BlockSpec((None, None, None, 2, PAST, HEAD_DIM), lambda b, h, q: (b, l, h, 0, 0, 0)),
                  pl.BlockSpec((None, None, None, PAST, DIFF_V), lambda b, h, q: (b, l, h, 0, 0)),
                  tab, tab, tab, vec(64), vec(64), vec(64), vec(64), vec(128)],
        out_specs=pl.BlockSpec((tq, 128), lambda b, h, q: (b * nq + q, h)),
        out_shape=jax.ShapeDtypeStruct((NS_TOK, DIFF_HEADS * DIFF_V), BF16),
        scratch_shapes=[pltpu.VMEM((DEC_SEQ + PAST, HEAD_DIM), BF16),
                        pltpu.VMEM((DEC_SEQ + PAST, HEAD_DIM), BF16),
                        pltpu.VMEM((DEC_SEQ + PAST, DIFF_V), BF16)],
        compiler_params=_cparams(("arbitrary", "arbitrary", "arbitrary")),
    )(p, p, p, cache_k, cache_v, cos, s_up, s_dn, lq1, lk1, lq2, lk2, g)


@functools.lru_cache(None)
def _dft_consts(L):
    n = 2 * L
    k = np.arange(L)
    ang = 2.0 * np.pi * ((k[:, None] * k[None, :]) % n) / n
    alt = (-1.0) ** k
    fa = np.cos(ang)
    fb = -np.sin(ang)
    fb[0, :] = alt
    wgt = np.full((L,), 2.0 / n)
    wgt[0] = 1.0 / n
    ga = fa * wgt[:, None]
    gb = fb * wgt[:, None]
    gb[0, :] = alt / n
    f = np.concatenate([fa, fb], axis=0)
    g = np.concatenate([ga.T, gb.T], axis=1)
    return np.asarray(f, dtype=BF16), np.asarray(g, dtype=BF16)


@functools.lru_cache(None)
def _filter_consts(L):
    f32 = np.float32
    t = np.linspace(0.0, 1.0, L, dtype=f32)[:, None]
    pos = np.arange(L, dtype=f32)[:, None]
    bands = np.linspace(1e-4, HY_BANDS - 1, HY_BANDS, dtype=f32)[None, :]
    ang = f32(2.0 * math.pi) * bands * pos / f32(L)
    z = np.zeros((L, 128), f32)
    z[:, 0:1] = t
    z[:, 1:1 + HY_BANDS] = np.cos(ang)
    z[:, 1 + HY_BANDS:HY_EMB] = -np.sin(ang)
    min_decay = math.log(1e-2) / 1.5
    max_decay = math.log(1e-2) / 0.3
    deltas = np.abs(np.linspace(min_decay, max_decay, HY_W, dtype=f32))
    decay = np.exp(-t * deltas[None, :]).astype(f32)
    return z, decay


def _spectra_kernel(z_ref, w1_ref, b1_ref, w2_ref, b2_ref, fr_ref, w3_ref, dec_ref, fa_ref, fb_ref,
                    sa_ref, sb_ref, filt_scr, *, L, kc):
    j = pl.program_id(0)

    @pl.when(j == 0)
    def _():
        fr = fr_ref[...]
        hdn = jnp.sin(fr * (_dot3(z_ref[...], w1_ref[...]) + b1_ref[...]))
        hdn = jnp.sin(fr * (_dot3(hdn, w2_ref[...]) + b2_ref[...]))
        dec = dec_ref[...]
        not_first = lax.broadcasted_iota(jnp.int32, (L, HY_W), 0) > 0
        for o in range(2):
            hf = _dot3(hdn, w3_ref[:, 512 * o:512 * o + 256]) * dec
            hb = jnp.where(not_first, _dot3(hdn, w3_ref[:, 512 * o + 256:512 * o + 512]) * dec, 0.0)
            nrm = (jnp.sum(jnp.abs(hf), axis=0, keepdims=True)
                   + jnp.sum(jnp.abs(hb), axis=0, keepdims=True))
            filt_scr[:, 512 * o:512 * o + 256] = (hf / nrm).astype(BF16)
            filt_scr[:, 512 * o + 256:512 * o + 512] = (hb / nrm).astype(BF16)

    ta = _dot(fa_ref[...], filt_scr[...])
    tb = _dot(fb_ref[...], filt_scr[...])
    first = (lax.broadcasted_iota(jnp.int32, (kc, HY_W), 0) + j * kc) == 0
    for o in range(2):
        af, ab = ta[:, 512 * o:512 * o + 256], ta[:, 512 * o + 256:512 * o + 512]
        bf, bb = tb[:, 512 * o:512 * o + 256], tb[:, 512 * o + 256:512 * o + 512]
        sa_ref[:, 256 * o:256 * o + 256] = af + ab
        sb_ref[:, 256 * o:256 * o + 256] = jnp.where(first, bf + bb, bf - bb)


def _hy_spectra(L, kc, w1p, b1, w2, b2, fr, w3):
    z, decay = _filter_consts(L)
    f = jnp.asarray(_dft_consts(L)[0])
    nj = L // kc
    full = lambda shape: pl.BlockSpec(shape, lambda j: tuple(0 for _ in shape))
    return pl.pallas_call(
        functools.partial(_spectra_kernel, L=L, kc=kc),
        grid=(nj,),
        in_specs=[full((L, 128)), full((128, HY_HID)), full((1, HY_HID)), full((HY_HID, HY_HID)),
                  full((1, HY_HID)), full((1, HY_HID)), full((HY_HID, 4 * HY_W)), full((L, HY_W)),
                  pl.BlockSpec((kc, L), lambda j: (j, 0)),
                  pl.BlockSpec((kc, L), lambda j: (j + nj, 0))],
        out_specs=[pl.BlockSpec((kc, 2 * HY_W), lambda j: (j, 0)),
                   pl.BlockSpec((kc, 2 * HY_W), lambda j: (j, 0))],
        out_shape=[jax.ShapeDtypeStruct((L, 2 * HY_W), F32)] * 2,
        scratch_shapes=[pltpu.VMEM((L, 4 * HY_W), BF16)],
        compiler_params=_cparams(("arbitrary",)),
    )(jnp.asarray(z), w1p, b1, w2, b2, fr, w3, jnp.asarray(decay), f, f)


def _hyconv_kernel(*refs, L, nb, kc, nj):
    u_refs = refs[:nb]
    (cw_ref, cb_ref, d_ref, fa_ref, fb_ref, ga_ref, gb_ref, sa_ref, sb_ref, o_ref,
     xin_scr, x_scr, g1_scr, g2_scr, y_scr) = refs[nb:]
    o = pl.program_id(1)
    j = pl.program_id(2)

    @pl.when((o == 0) & (j == 0))
    def _():
        row = lax.broadcasted_iota(jnp.int32, (L, HY_W), 0)
        for i in range(nb):
            cols = slice(HY_W * i, HY_W * (i + 1))
            for part, dst in enumerate((x_scr, g1_scr, g2_scr)):
                pc = slice(HY_W * part, HY_W * (part + 1))
                u = u_refs[i][:, pc]
                up = jnp.where(row == 0, 0.0, pltpu.roll(u, 1, axis=0))
                un = jnp.where(row == L - 1, 0.0, pltpu.roll(u, L - 1, axis=0))
                dst[:, cols] = (up * cw_ref[0:1, pc] + u * cw_ref[1:2, pc] + un * cw_ref[2:3, pc]
                                + cb_ref[:, pc])
            xin_scr[:, cols] = x_scr[:, cols].astype(BF16)

    @pl.when(j == 0)
    def _():
        y_scr[...] = jnp.zeros_like(y_scr)

    xa = _dot(fa_ref[...], xin_scr[...])
    xb = _dot(fb_ref[...], xin_scr[...])
    sa = jnp.concatenate([sa_ref[...]] * nb, axis=1)
    sb = jnp.concatenate([sb_ref[...]] * nb, axis=1)
    first = (lax.broadcasted_iota(jnp.int32, (kc, nb * HY_W), 0) + j * kc) == 0
    ya = jnp.where(first, xa * sa, xa * sa - xb * sb)
    yb = jnp.where(first, xb * sb, xa * sb + xb * sa)
    y_scr[...] += _dot(ga_ref[...], ya.astype(BF16)) + _dot(gb_ref[...], yb.astype(BF16))

    @pl.when(j == nj - 1)
    def _():
        dvec = jnp.concatenate([d_ref[pl.ds(o, 1), :]] * nb, axis=1)
        y = y_scr[...] + x_scr[...] * dvec

        @pl.when(o == 0)
        def _():
            zz = g1_scr[...] * y
            x_scr[...] = zz
            xin_scr[...] = zz.astype(BF16)

        @pl.when(o == 1)
        def _():
            res = g2_scr[...] * y
            for i in range(nb):
                o_ref[L * i:L * (i + 1), :] = res[:, HY_W * i:HY_W * (i + 1)].astype(BF16)


def _hy_conv(p, L, nb, kc, row_blk0, n_seq, conv_w, conv_b, d, sa, sb):
    f, g = (jnp.asarray(a) for a in _dft_consts(L))
    nj = L // kc
    u_spec = lambda i: pl.BlockSpec((L, 3 * HY_W), lambda bg, o, j: (row_blk0 + bg * nb + i, C_HY // 768),
                                    pipeline_mode=pl.Buffered(1))
    small = lambda shape: pl.BlockSpec(shape, lambda bg, o, j: (0, 0))
    return pl.pallas_call(
        functools.partial(_hyconv_kernel, L=L, nb=nb, kc=kc, nj=nj),
        grid=(n_seq // nb, 2, nj),
        in_specs=[u_spec(i) for i in range(nb)] + [
            small((3, 3 * HY_W)), small((1, 3 * HY_W)), small((2, HY_W)),
            pl.BlockSpec((kc, L), lambda bg, o, j: (j, 0)),
            pl.BlockSpec((kc, L), lambda bg, o, j: (j + nj, 0)),
            pl.BlockSpec((L, kc), lambda bg, o, j: (0, j)),
            pl.BlockSpec((L, kc), lambda bg, o, j: (0, j + nj)),
            pl.BlockSpec((kc, HY_W), lambda bg, o, j: (j, o)),
            pl.BlockSpec((kc, HY_W), lambda bg, o, j: (j, o))],
        out_specs=pl.BlockSpec((nb * L, HY_W), lambda bg, o, j: (bg, 0)),
        out_shape=jax.ShapeDtypeStruct((n_seq * L, HY_W), BF16),
        scratch_shapes=[pltpu.VMEM((L, nb * HY_W), BF16)] + [pltpu.VMEM((L, nb * HY_W), F32)] * 4,
        compiler_params=_cparams(("arbitrary", "arbitrary", "arbitrary")),
    )(*([p] * nb), conv_w, conv_b, d, f, f, g, g, sa, sb)


def _outproj_kernel(xp_ref, xs_ref, mixp_ref, hyp_ref, nas_ref, ds_ref, hys_ref, g1_ref, w_ref, o_ref):
    i = pl.program_id(0)
    w = lambda a, b: w_ref[a:b, :].astype(BF16)

    @pl.when(i < NP_TOK // TM)
    def _():
        y = _dot(mixp_ref[...], w(0, 768)) + _dot(hyp_ref[...], w(768, D))
        o_ref[...] = xp_ref[...] + g1_ref[...] * y

    @pl.when(i >= NP_TOK // TM)
    def _():
        y = _dot(nas_ref[...], w(0, 256)) + _dot(ds_ref[...], w(256, 768)) + _dot(hys_ref[...], w(768, D))
        o_ref[...] = xs_ref[...] + g1_ref[...] * y


def _x_specs(xs_block0):
    npt = NP_TOK // TM
    return [pl.BlockSpec((TM, D), lambda i, *_: (jnp.minimum(i, npt - 1), 0)),
            pl.BlockSpec((TM, D), lambda i, *_: (jnp.maximum(i - npt, 0) + xs_block0, 0))]


def _outproj(xp, xs, xs_block0, mix_p, hy_p, na_s, d_s, hy_s, mod6, w_out, l):
    npt = NP_TOK // TM
    pidx = lambda i: (jnp.minimum(i, npt - 1), 0)
    sidx = lambda i: (jnp.maximum(i - npt, 0), 0)
    return pl.pallas_call(
        _outproj_kernel,
        grid=(N_TOK // TM,),
        in_specs=_x_specs(xs_block0) + [
                  pl.BlockSpec((TM, 768), pidx),
                  pl.BlockSpec((TM, HY_W), pidx),
                  pl.BlockSpec((TM, 256), sidx),
                  pl.BlockSpec((TM, 512), sidx),
                  pl.BlockSpec((TM, HY_W), sidx),
                  _mod_spec(2, TM),
                  pl.BlockSpec((None, D, D), lambda i: (l, 0, 0))],
        out_specs=pl.BlockSpec((TM, D), lambda i: (i, 0)),
        out_shape=jax.ShapeDtypeStruct((N_TOK, D), F32),
        compiler_params=_cparams(("arbitrary",)),
    )(xp, xs, mix_p, hy_p, na_s, d_s, hy_s, mod6, w_out)


def _ffn_kernel(x_ref, g_ref, sc_ref, sh_ref, gate_ref, wg_ref, wu_ref, wd_ref, o_ref, h_scr, *, nj):
    j = pl.program_id(1)

    @pl.when(j == 0)
    def _():
        h_scr[...] = _norm_mod(x_ref[...], g_ref[...], sc_ref[...], sh_ref[...]).astype(BF16)
        o_ref[...] = jnp.zeros_like(o_ref)

    h = h_scr[...]
    a = _silu(_dot(h, wg_ref[...].astype(BF16))) * _dot(h, wu_ref[...].astype(BF16))
    o_ref[...] += _dot(a.astype(BF16), wd_ref[...].astype(BF16))

    @pl.when(j == nj - 1)
    def _():
        o_ref[...] = x_ref[...] + gate_ref[...] * o_ref[...]


def _dense_ffn(x, g, mod6, wg, wu, wd, i_ffn):
    tf = 256
    tm = DEC_SEQ
    nj = FFN // tf
    return pl.pallas_call(
        functools.partial(_ffn_kernel, nj=nj),
        grid=(N_TOK // tm, nj),
        in_specs=[pl.BlockSpec((tm, D), lambda i, j: (i, 0)),
                  pl.BlockSpec((1, D), lambda i, j: (0, 0)),
                  _mod_spec(4, tm), _mod_spec(3, tm), _mod_spec(5, tm),
                  pl.BlockSpec((None, D, tf), lambda i, j: (i_ffn, 0, j)),
                  pl.BlockSpec((None, D, tf), lambda i, j: (i_ffn, 0, j)),
                  pl.BlockSpec((None, tf, D), lambda i, j: (i_ffn, j, 0))],
        out_specs=pl.BlockSpec((tm, D), lambda i, j: (i, 0)),
        out_shape=jax.ShapeDtypeStruct((N_TOK, D), F32),
        scratch_shapes=[pltpu.VMEM((tm, D), BF16)],
        compiler_params=_cparams(("arbitrary", "arbitrary")),
    )(x, g, mod6, mod6, mod6, wg, wu, wd)


def _router_kernel(x_ref, g_ref, sc_ref, sh_ref, wr_ref, h_ref, r_ref):
    h = _norm_mod(x_ref[...], g_ref[...], sc_ref[...], sh_ref[...])
    h_ref[...] = h.astype(BF16)
    lane = lax.broadcasted_iota(jnp.int32, (TM, 128), 1)
    lg = jnp.where(lane < N_EXP, _dot3(h, wr_ref[...]), -jnp.inf)
    m1 = jnp.max(lg, axis=-1, keepdims=True)
    i1 = jnp.min(jnp.where(lg == m1, lane, 128), axis=-1, keepdims=True)
    lg2 = jnp.where(lane == i1, -jnp.inf, lg)
    m2 = jnp.max(lg2, axis=-1, keepdims=True)
    i2 = jnp.min(jnp.where(lg2 == m2, lane, 128), axis=-1, keepdims=True)
    e = jnp.exp(m2 - m1)
    w1 = 1.0 / (1.0 + e)
    w2 = e / (1.0 + e)
    r_ref[...] = jnp.where(lane == 0, i1.astype(F32),
                           jnp.where(lane == 1, i2.astype(F32),
                                     jnp.where(lane == 2, w1, jnp.where(lane == 3, w2, 0.0))))


def _router(x, g, mod6, wr_pad):
    return pl.pallas_call(
        _router_kernel,
        grid=(N_TOK // TM,),
        in_specs=[pl.BlockSpec((TM, D), lambda i: (i, 0)),
                  pl.BlockSpec((1, D), lambda i: (0, 0)),
                  _mod_spec(4, TM), _mod_spec(3, TM),
                  pl.BlockSpec((D, 128), lambda i: (0, 0))],
        out_specs=[pl.BlockSpec((TM, D), lambda i: (i, 0)),
                   pl.BlockSpec((TM, 128), lambda i: (i, 0))],
        out_shape=[jax.ShapeDtypeStruct((N_TOK, D), BF16), jax.ShapeDtypeStruct((N_TOK, 128), F32)],
        compiler_params=_cparams(("arbitrary",)),
    )(x, g, mod6, mod6, wr_pad)


def _dispatch_kernel(blo_ref, bhi_ref, sexp_ref, h_hbm, rp_ref, o_ref, h_scr, sem):
    s = pl.program_id(0)

    @pl.when(s == 0)
    def _():
        cp = pltpu.make_async_copy(h_hbm, h_scr, sem)
        cp.start()
        cp.wait()

    e = sexp_ref[s]
    rows = s * MOE_TS + lax.broadcasted_iota(jnp.int32, (MOE_TS, MOE_DTB), 0)

    def sel(b):
        return (rows == rp_ref[e, pl.ds(b, 1), :]).astype(BF16)

    b0 = jnp.minimum(blo_ref[s], N_TOK // MOE_DTB - MOE_DWIN)
    hwin = h_scr[pl.ds(pl.multiple_of(b0 * MOE_DTB, MOE_DTB), MOE_DWIN * MOE_DTB), :]
    acc = _dot(jnp.concatenate([sel(b0 + k) for k in range(MOE_DWIN)], axis=1), hwin)

    def body(b, acc):
        hb = h_scr[pl.ds(pl.multiple_of(b * MOE_DTB, MOE_DTB), MOE_DTB), :]
        return acc + _dot(sel(b), hb)

    acc = lax.fori_loop(b0 + MOE_DWIN, bhi_ref[s] + 1, body, acc)
    o_ref[...] = acc.astype(BF16)


def _dispatch(blo, bhi, sexp, h, rp_t):
    return pl.pallas_call(
        _dispatch_kernel,
        grid_spec=pltpu.PrefetchScalarGridSpec(
            num_scalar_prefetch=3,
            grid=(MOE_ROWS // MOE_TS,),
            in_specs=[pl.BlockSpec(memory_space=pl.ANY),
                      pl.BlockSpec((N_EXP, N_TOK // MOE_DTB, MOE_DTB), lambda s, *_: (0, 0, 0))],
            out_specs=pl.BlockSpec((MOE_TS, D), lambda s, *_: (s, 0)),
            scratch_shapes=[pltpu.VMEM((N_TOK, D), BF16), pltpu.SemaphoreType.DMA(())]),
        out_shape=jax.ShapeDtypeStruct((MOE_ROWS, D), BF16),
        compiler_params=_cparams(("arbitrary",)),
    )(blo, bhi, sexp, h, rp_t)


def _experts_kernel(te_ref, used_ref, xs_ref, wg_ref, wu_ref, wd_ref, o_ref, acc_scr, *, nj):
    i = pl.program_id(0)
    j = pl.program_id(1)
    live = i < used_ref[0]

    @pl.when(live)
    def _():
        @pl.when(j == 0)
        def _():
            acc_scr[...] = jnp.zeros_like(acc_scr)

        xb = xs_ref[...]
        a = _silu(_dot(xb, wg_ref[...].astype(BF16))) * _dot(xb, wu_ref[...].astype(BF16))
        acc_scr[...] += _dot(a.astype(BF16), wd_ref[...].astype(BF16))

    @pl.when(j == nj - 1)
    def _():
        o_ref[...] = jnp.where(live, acc_scr[...], 0.0).astype(BF16)


def _experts(tile_expert, used, xs, wg, wu, wd, i_moe):
    nj = EXP_DIM // MOE_TF

    def tile(i, used):
        return jnp.minimum(i, used[0] - 1)

    def chunk(i, j, used):
        return jnp.where(i < used[0], j, nj - 1)

    return pl.pallas_call(
        functools.partial(_experts_kernel, nj=nj),
        grid_spec=pltpu.PrefetchScalarGridSpec(
            num_scalar_prefetch=2,
            grid=(MOE_YROWS // MOE_TM, nj),
            in_specs=[pl.BlockSpec((MOE_TM, D), lambda i, j, te, used: (tile(i, used), 0)),
                      pl.BlockSpec((None, None, D, MOE_TF),
                                   lambda i, j, te, used: (i_moe, te[tile(i, used)], 0, chunk(i, j, used))),
                      pl.BlockSpec((None, None, D, MOE_TF),
                                   lambda i, j, te, used: (i_moe, te[tile(i, used)], 0, chunk(i, j, used))),
                      pl.BlockSpec((None, None, MOE_TF, D),
                                   lambda i, j, te, used: (i_moe, te[tile(i, used)], chunk(i, j, used), 0))],
            out_specs=pl.BlockSpec((MOE_TM, D), lambda i, j, te, used: (i, 0)),
            scratch_shapes=[pltpu.VMEM((MOE_TM, D), F32)]),
        out_shape=jax.ShapeDtypeStruct((MOE_YROWS, D), BF16),
        compiler_params=_cparams(("arbitrary", "arbitrary")),
    )(tile_expert, used, xs, wg, wu, wd)


def _combine_kernel(ws_ref, kind_ref, x_ref, rp_ref, comb_ref, gate_ref, fg_ref, ys_hbm, op_ref, os_ref,
                    win_scr, y_scr, sem):
    b = pl.program_id(0)
    nb = pl.num_programs(0)
    slot = b % 2

    def win_copy(blk, sl, e):
        start = pl.multiple_of(ws_ref[blk * N_EXP + e], MOE_ALIGN)
        return pltpu.make_async_copy(ys_hbm.at[pl.ds(start, MOE_WIN)], win_scr.at[sl, e], sem.at[sl, e])

    @pl.when(b == 0)
    def _():
        for e in range(N_EXP):
            win_copy(0, 0, e).start()

    @pl.when(b + 1 < nb)
    def _():
        for e in range(N_EXP):
            win_copy(b + 1, 1 - slot, e).start()

    col = lax.broadcasted_iota(jnp.int32, (MOE_TB, MOE_WIN), 1)
    y_scr[...] = jnp.zeros_like(y_scr)
    for e in range(N_EXP):
        win_copy(b, slot, e).wait()
        rel = rp_ref[:, e:e + 1] - ws_ref[b * N_EXP + e]
        cw = comb_ref[:, e:e + 1]
        kind = kind_ref[b * N_EXP + e]

        @pl.when(kind == 1)
        def _():
            sel = (rel == lax.broadcasted_iota(jnp.int32, (MOE_TB, MOE_WIN_SMALL), 1)).astype(BF16)
            y_scr[...] += cw * _dot(sel, win_scr[slot, e, 0:MOE_WIN_SMALL, :])

        @pl.when(kind == 0)
        def _():
            y_scr[...] += cw * _dot((rel == col).astype(BF16), win_scr[slot, e])

    x = x_ref[...] + gate_ref[...] * y_scr[...]
    x = (x * lax.rsqrt(jnp.mean(x * x, axis=-1, keepdims=True) + EPS)) * fg_ref[...]

    @pl.when(b < NP_TOK // MOE_TB)
    def _():
        op_ref[...] = x

    @pl.when(b >= NP_TOK // MOE_TB)
    def _():
        os_ref[...] = x


def _combine(ws, kind, x, rp8, comb, mod6, final_g, ys):
    npb = NP_TOK // MOE_TB
    return pl.pallas_call(
        _combine_kernel,
        grid_spec=pltpu.PrefetchScalarGridSpec(
            num_scalar_prefetch=2,
            grid=(N_TOK // MOE_TB,),
            in_specs=[pl.BlockSpec((MOE_TB, D), lambda b, *_: (b, 0)),
                      pl.BlockSpec((MOE_TB, N_EXP), lambda b, *_: (b, 0)),
                      pl.BlockSpec((MOE_TB, N_EXP), lambda b, *_: (b, 0)),
                      _mod_spec(5, MOE_TB),
                      pl.BlockSpec((1, D), lambda b, *_: (0, 0)),
                      pl.BlockSpec(memory_space=pl.ANY)],
            out_specs=[pl.BlockSpec((MOE_TB, D), lambda b, *_: (jnp.minimum(b, npb - 1), 0)),
                       pl.BlockSpec((MOE_TB, D), lambda b, *_: (jnp.maximum(b - npb, 0), 0))],
            scratch_shapes=[pltpu.VMEM((2, N_EXP, MOE_WIN, D), BF16),
                            pltpu.VMEM((MOE_TB, D), F32),
                            pltpu.SemaphoreType.DMA((2, N_EXP))]),
        out_shape=[jax.ShapeDtypeStruct((NP_TOK, D), F32), jax.ShapeDtypeStruct((NS_TOK, D), F32)],
        compiler_params=_cparams(("arbitrary",)),
    )(ws, kind, x, rp8, comb, mod6, final_g, ys)


def _moe(x, g, mod6, router, wg, wu, wd, i_moe, final_g):
    wr_pad = jnp.pad(router, ((0, 0), (0, 128 - N_EXP)))
    h, r = _router(x, g, mod6, wr_pad)

    i32 = jnp.int32
    i12 = r[:, 0:2].astype(i32)
    earange = jnp.arange(N_EXP, dtype=i32)
    hit1 = i12[:, 0:1] == earange[None, :]
    hit2 = i12[:, 1:2] == earange[None, :]
    comb = jnp.where(hit1, r[:, 2:3], 0.0) + jnp.where(hit2, r[:, 3:4], 0.0)
    mask = (hit1 | hit2).astype(i32)
    csum = jnp.cumsum(mask, axis=0)
    counts = csum[-1]
    padded = ((counts + MOE_TM - 1) // MOE_TM) * MOE_TM
    ends = jnp.cumsum(padded)
    starts = ends - padded
    rp8 = jnp.where(mask > 0, starts[None, :] + csum - 1, -1).astype(i32)
    n_tiles = MOE_ROWS // MOE_TM
    tile_expert = jnp.minimum(
        jnp.searchsorted(ends, jnp.arange(n_tiles, dtype=i32) * MOE_TM, side="right"),
        N_EXP - 1).astype(i32)
    used = (ends[-1:] // MOE_TM).astype(i32)

    sub_row0 = jnp.arange(MOE_ROWS // MOE_TS, dtype=i32) * MOE_TS
    sexp = tile_expert[sub_row0 // MOE_TM]
    qlo = sub_row0 - starts[sexp]
    qend = jnp.minimum(qlo + MOE_TS, counts[sexp])
    cbe = csum[MOE_DTB - 1::MOE_DTB, :].T[sexp]
    blo = jnp.sum((cbe <= qlo[:, None]).astype(i32), axis=1)
    bhi = jnp.minimum(jnp.sum((cbe < qend[:, None]).astype(i32), axis=1), N_TOK // MOE_DTB - 1)
    empty = qend <= qlo
    blo = jnp.where(empty, 1, blo).astype(i32)
    bhi = jnp.where(empty, 0, bhi).astype(i32)

    cb = csum[MOE_TB - 1::MOE_TB, :]
    cprev = jnp.concatenate([jnp.zeros((1, N_EXP), i32), cb[:-1]], axis=0)
    ws = (((starts[None, :] + cprev) // MOE_ALIGN) * MOE_ALIGN).reshape(-1).astype(i32)
    n_be = cb - cprev
    kind = jnp.where(n_be == 0, 2, jnp.where(n_be <= MOE_WIN_SMALL - MOE_ALIGN, 1, 0)).reshape(-1).astype(i32)

    rp_t = rp8.T.reshape(N_EXP, N_TOK // MOE_DTB, MOE_DTB)
    xs = _dispatch(blo, bhi, sexp, h, rp_t)
    ys = _experts(tile_expert, used, xs, wg, wu, wd, i_moe)
    return _combine(ws, kind, x, rp8, comb, mod6, final_g, ys)


assert DEPTH == 2

def kernel(x_prompt, x_sample, cache_na_k, cache_na_v, cache_diff_k, cache_diff_v, c, c_ctx, w_in, w_out, ada_w, ada_b, norm_mix_g, norm_ffn_g, na_rpb, diff_lq1, diff_lk1, diff_lq2, diff_lk2, diff_subln_g, hy_conv_w, hy_conv_b, hy_d, hy_f_w1, hy_f_b1, hy_f_w2, hy_f_b2, hy_f_freq, hy_f_w3, ffn_w_gate, ffn_w_up, ffn_w_down, moe_router, moe_w_gate, moe_w_up, moe_w_down, final_norm_g):
    xparts = (x_prompt.reshape(NP_TOK, D), x_sample.reshape(NS_TOK, D), 0)
    cond8 = jnp.concatenate([c_ctx[None, :], c, jnp.zeros((5, D), F32)], axis=0)
    mods = _modulation(cond8, ada_w, ada_b)
    final_g = final_norm_g.reshape(1, D)

    leaves = [jnp.zeros((BATCH, DEPTH, NA_HEADS, SEQ, HEAD_DIM), F32),
              jnp.zeros((BATCH, DEPTH, NA_HEADS, SEQ, HEAD_DIM), F32),
              jnp.zeros((BATCH, DEPTH, DIFF_HEADS, 2, SEQ, HEAD_DIM), F32),
              jnp.zeros((BATCH, DEPTH, DIFF_HEADS, SEQ, DIFF_V), F32)]
    for l in range(DEPTH):
        lam_init = 0.8 - 0.6 * math.exp(-0.3 * l)
        mod6 = mods[l].reshape(8, 6, D).transpose(1, 0, 2).reshape(6, 8, 1, D)
        row = lambda a: a[l].reshape(1, -1)
        lq1, lk1, lq2, lk2, subg = row(diff_lq1), row(diff_lk1), row(diff_lq2), row(diff_lk2), row(diff_subln_g)

        p, *leaves = _inproj(*xparts, row(norm_mix_g), mod6, w_in, l, leaves)

        mix_p = _prompt_attn(p, lq1, lk1, lq2, lk2, subg, lam_init)
        bias = _na_bias(na_rpb[l])
        na_s = _na_attn(p, cache_na_k, cache_na_v, bias, l)
        d_s = _sdiff_attn(p, cache_diff_k, cache_diff_v, lq1, lk1, lq2, lk2, subg, lam_init, l)

        w1p = jnp.pad(hy_f_w1[l], ((0, 128 - HY_EMB), (0, 0)))
        fargs = (w1p, row(hy_f_b1), hy_f_w2[l], row(hy_f_b2), row(hy_f_freq), hy_f_w3[l])
        cargs = (hy_conv_w[l], row(hy_conv_b), hy_d[l])
        sa_p, sb_p = _hy_spectra(SEQ, SEQ, *fargs)
        hy_p = _hy_conv(p, SEQ, 2, SEQ, 0, BATCH, *cargs, sa_p, sb_p)
        sa_s, sb_s = _hy_spectra(DEC_SEQ, 512, *fargs)
        hy_s = _hy_conv(p, DEC_SEQ, 1, 512, NP_TOK // DEC_SEQ, DEC_BATCH, *cargs, sa_s, sb_s)

        x = _outproj(*xparts, mix_p, hy_p, na_s, d_s, hy_s, mod6, w_out, l)

        if l == 0:
            x = _dense_ffn(x, row(norm_ffn_g), mod6, ffn_w_gate, ffn_w_up, ffn_w_down, 0)
            xparts = (x, x, NP_TOK // TM)
        else:
            yp, ys = _moe(x, row(norm_ffn_g), mod6, moe_router[0], moe_w_gate, moe_w_up, moe_w_down,
                          0, final_g)

    return (yp.reshape(BATCH, SEQ, D), ys.reshape(DEC_BATCH, DEC_SEQ, D), *leaves)
```

```python
import functools
import math

import numpy as np
import jax
import jax.numpy as jnp
from jax import lax
from jax.experimental import pallas as pl
from jax.experimental.pallas import tpu as pltpu

F32 = jnp.float32
BF16 = jnp.bfloat16

D = 1024
BATCH, SEQ = 32, 256
DEC_BATCH, DEC_SEQ = 2, 2048
DEPTH = 2
PAST = 512
GRID_W = 64
GRID_ROWS = DEC_SEQ // GRID_W
HEAD_DIM = 64
NA_HEADS = 4
DIFF_HEADS = 4
DIFF_V = 128
WIN_ROWS, WIN_COLS = 8, 16
HY_W = 256
HY_EMB = 33
HY_BANDS = 16
HY_HID = 64
PROJ = 3072
FFN = 2816
N_EXP = 8
EXP_DIM = 3584
EPS = 1e-6
ROPE_BASE = 10000.0

NP_TOK = BATCH * SEQ
NS_TOK = DEC_BATCH * DEC_SEQ
N_TOK = NP_TOK + NS_TOK

C_NAQ, C_NAK, C_NAV = 0, 256, 512
C_DQ, C_DK, C_DV = 768, 1280, 1792
C_HY = 2304

TM = 1024
VMEM_LIMIT = 56 * 1024 * 1024

MOE_TM = 1024
MOE_TF = 512
MOE_ROWS = 2 * N_TOK + N_EXP * MOE_TM
MOE_TS = 256
MOE_DTB = 512
MOE_DWIN = 3
MOE_TB = 512
MOE_ALIGN = 16
MOE_WIN = MOE_TB + MOE_ALIGN
MOE_WIN_SMALL = MOE_TB // 2 + MOE_ALIGN
MOE_YROWS = MOE_ROWS + MOE_TM


def _cparams(sem):
    return pltpu.CompilerParams(dimension_semantics=sem, vmem_limit_bytes=VMEM_LIMIT)


def _dot(a, b):
    return jnp.dot(a, b, preferred_element_type=F32)


def _dot_nt(a, b):
    return lax.dot_general(a, b, (((1,), (1,)), ((), ())), preferred_element_type=F32)


def _split(a):
    hi = a.astype(BF16)
    lo = (a - hi.astype(F32)).astype(BF16)
    return hi, lo


def _dot3(a, b):
    ah, al = _split(a)
    bh, bl = _split(b)
    return _dot(ah, bh) + (_dot(ah, bl) + _dot(al, bh))


def _silu(x):
    return x / (1.0 + jnp.exp(-x))


def _mod_row(i, tm):
    t = i * tm
    return jnp.where(t < NP_TOK, 0, 1 + (t - NP_TOK) // DEC_SEQ)


def _mod_spec(k, tm):
    return pl.BlockSpec((None, None, 1, D), lambda i, *_: (k, _mod_row(i, tm), 0, 0))


def _norm_mod(x, g, sc, sh):
    y = x * lax.rsqrt(jnp.mean(x * x, axis=-1, keepdims=True) + EPS)
    return (y * g) * (1.0 + sc) + sh


def _mod_kernel(c_ref, w_ref, b_ref, o_ref):
    o_ref[...] = _dot3(_silu(c_ref[...]), w_ref[...]) + b_ref[...]


def _modulation(cond8, ada_w, ada_b):
    tn = 1536
    return pl.pallas_call(
        _mod_kernel,
        grid=(DEPTH, 6 * D // tn),
        in_specs=[pl.BlockSpec((8, D), lambda l, j: (0, 0)),
                  pl.BlockSpec((None, D, tn), lambda l, j: (l, 0, j)),
                  pl.BlockSpec((None, 1, tn), lambda l, j: (l, 0, j))],
        out_specs=pl.BlockSpec((None, 8, tn), lambda l, j: (l, 0, j)),
        out_shape=jax.ShapeDtypeStruct((DEPTH, 8, 6 * D), F32),
        compiler_params=_cparams(("arbitrary", "arbitrary")),
    )(cond8, ada_w, ada_b.reshape(DEPTH, 1, 6 * D))


def _inproj_kernel(xp_ref, xs_ref, g_ref, sc_ref, sh_ref, w_hbm, nak_in, nav_in, dk_in, dv_in,
                   o_ref, nak_ref, nav_ref, dk_ref, dv_ref, h_scr, w_scr, stage, sem, *, l, tn):
    del nak_in, nav_in, dk_in, dv_in
    i = pl.program_id(0)
    j = pl.program_id(1)

    @pl.when((i == 0) & (j == 0))
    def _():
        for c in range(PROJ // tn):
            cp = pltpu.make_async_copy(w_hbm.at[l, :, c * tn:(c + 1) * tn], stage, sem)
            cp.start()
            cp.wait()
            w_scr[c] = stage[...].astype(BF16)

    is_ctx = i < NP_TOK // TM

    @pl.when((j == 0) & is_ctx)
    def _():
        h_scr[...] = _norm_mod(xp_ref[...], g_ref[...], sc_ref[...], sh_ref[...]).astype(BF16)

    @pl.when((j == 0) & jnp.logical_not(is_ctx))
    def _():
        h_scr[...] = _norm_mod(xs_ref[...], g_ref[...], sc_ref[...], sh_ref[...]).astype(BF16)

    o_ref[...] = _dot(h_scr[...], w_scr[j])


    def rows(bb):
        return slice(bb * SEQ, (bb + 1) * SEQ)

    def copy_heads(dst_ref, c0, width, heads, sub=None):
        for bb in range(TM // SEQ):
            for h in heads:
                if sub is None:
                    dst_ref[bb, h] = o_ref[rows(bb), c0(h):c0(h) + width]
                else:
                    for s in range(2):
                        dst_ref[bb, h, s] = o_ref[rows(bb), c0(h) + s * width:c0(h) + (s + 1) * width]

    @pl.when(is_ctx & (j == 0))
    def _():
        copy_heads(nak_ref, lambda h: C_NAK + HEAD_DIM * h, HEAD_DIM, range(NA_HEADS))
        copy_heads(nav_ref, lambda h: C_NAV + HEAD_DIM * h, HEAD_DIM, range(NA_HEADS))

    @pl.when(is_ctx & (j == 1))
    def _():
        copy_heads(dk_ref, lambda h: C_DK - tn + 2 * HEAD_DIM * h, HEAD_DIM, range(0, 2), sub=True)

    @pl.when(is_ctx & (j == 2))
    def _():
        copy_heads(dk_ref, lambda h: C_DK - 2 * tn + 2 * HEAD_DIM * h, HEAD_DIM, range(2, 4), sub=True)
        copy_heads(dv_ref, lambda h: C_DV - 2 * tn + DIFF_V * h, DIFF_V, range(DIFF_HEADS))


def _inproj(xp, xs, xs_block0, g, mod6, w_in, l, leaves):
    tn = 768
    nb = TM // SEQ
    ctx = lambda i: jnp.minimum(i, NP_TOK // TM - 1)
    leaf_specs = [pl.BlockSpec((nb, None, NA_HEADS, SEQ, HEAD_DIM), lambda i, j: (ctx(i), l, 0, 0, 0)),
                  pl.BlockSpec((nb, None, NA_HEADS, SEQ, HEAD_DIM), lambda i, j: (ctx(i), l, 0, 0, 0)),
                  pl.BlockSpec((nb, None, DIFF_HEADS, 2, SEQ, HEAD_DIM), lambda i, j: (ctx(i), l, 0, 0, 0, 0)),
                  pl.BlockSpec((nb, None, DIFF_HEADS, SEQ, DIFF_V), lambda i, j: (ctx(i), l, 0, 0, 0))]
    return pl.pallas_call(
        functools.partial(_inproj_kernel, l=l, tn=tn),
        grid=(N_TOK // TM, PROJ // tn),
        in_specs=_x_specs(xs_block0) + [
                  pl.BlockSpec((1, D), lambda i, j: (0, 0)),
                  _mod_spec(1, TM), _mod_spec(0, TM),
                  pl.BlockSpec(memory_space=pl.ANY)] + [pl.BlockSpec(memory_space=pl.ANY)] * 4,
        out_specs=[pl.BlockSpec((TM, tn), lambda i, j: (i, j))] + leaf_specs,
        out_shape=[jax.ShapeDtypeStruct((N_TOK, PROJ), F32)]
        + [jax.ShapeDtypeStruct(a.shape, a.dtype) for a in leaves],
        input_output_aliases={6: 1, 7: 2, 8: 3, 9: 4},
        scratch_shapes=[pltpu.VMEM((TM, D), BF16), pltpu.VMEM((PROJ // tn, D, tn), BF16),
                        pltpu.VMEM((D, tn), F32), pltpu.SemaphoreType.DMA(())],
        compiler_params=_cparams(("arbitrary", "arbitrary")),
    )(xp, xs, g, mod6, mod6, w_in, *leaves)


def _lam(lq1, lk1, lq2, lk2, lam_init):
    return (jnp.exp(jnp.sum(lq1[...] * lk1[...], axis=-1, keepdims=True))
            - jnp.exp(jnp.sum(lq2[...] * lk2[...], axis=-1, keepdims=True)) + lam_init)


def _softmax_parts(s):
    m = jnp.max(s, axis=-1, keepdims=True)
    e = jnp.exp(s - m)
    return e, jnp.sum(e, axis=-1, keepdims=True)


QK_SCALE = HEAD_DIM ** -0.5


def _diff_head(q1, q2, k1, k2, v, lam, g, lam_init):
    e1, l1 = _softmax_parts(_dot_nt(q1, k1))
    e2, l2 = _softmax_parts(_dot_nt(q2, k2))
    a = e1 - (lam * l1 * (1.0 / l2)) * e2
    o = _dot(a.astype(BF16), v) * (1.0 / l1)
    o = o * lax.rsqrt(jnp.mean(o * o, axis=-1, keepdims=True) + EPS)
    return (o * g) * (1.0 - lam_init)


def _prompt_attn_kernel(pa_ref, pb_ref, pc_ref, lq1, lk1, lq2, lk2, g_ref, o_ref, *, lam_init):
    lam = _lam(lq1, lk1, lq2, lk2, lam_init)
    g = g_ref[...]

    def col(c0, w, scale=None):
        ref = (pa_ref, pb_ref, pc_ref)[c0 // 768]
        o = c0 % 768
        a = ref[:, o:o + w]
        return (a if scale is None else a * scale).astype(BF16)

    for h in range(NA_HEADS):
        q = col(C_NAQ + 64 * h, 64, QK_SCALE)
        k = col(C_NAK + 64 * h, 64)
        v = col(C_NAV + 64 * h, 64)
        e, l = _softmax_parts(_dot_nt(q, k))
        o = _dot(e.astype(BF16), v) * (1.0 / l)
        o_ref[:, 64 * h:64 * h + 64] = o.astype(BF16)
    for h in range(DIFF_HEADS):
        q1 = col(C_DQ + 128 * h, 64, QK_SCALE)
        q2 = col(C_DQ + 128 * h + 64, 64, QK_SCALE)
        k1 = col(C_DK + 128 * h, 64)
        k2 = col(C_DK + 128 * h + 64, 64)
        v = col(C_DV + 128 * h, 128)
        o = _diff_head(q1, q2, k1, k2, v, lam, g, lam_init)
        o_ref[:, 256 + 128 * h:384 + 128 * h] = o.astype(BF16)


def _prompt_attn(p, lq1, lk1, lq2, lk2, g, lam_init):
    vec = lambda n: pl.BlockSpec((1, n), lambda b: (0, 0))
    return pl.pallas_call(
        functools.partial(_prompt_attn_kernel, lam_init=lam_init),
        grid=(BATCH,),
        in_specs=[pl.BlockSpec((SEQ, 768), lambda b: (b, 0)),
                  pl.BlockSpec((SEQ, 768), lambda b: (b, 1)),
                  pl.BlockSpec((SEQ, 768), lambda b: (b, 2)),
                  vec(64), vec(64), vec(64), vec(64), vec(128)],
        out_specs=pl.BlockSpec((SEQ, 768), lambda b: (b, 0)),
        out_shape=jax.ShapeDtypeStruct((NP_TOK, 768), BF16),
        compiler_params=_cparams(("arbitrary",)),
    )(p, p, p, lq1, lk1, lq2, lk2, g)


def _bias_kernel(rpb_ref, o_ref):
    lh = pl.program_id(0)
    qc = lax.broadcasted_iota(jnp.int32, (GRID_W, GRID_W), 0)
    kc = lax.broadcasted_iota(jnp.int32, (GRID_W, GRID_W), 1)
    delta = jnp.clip(kc - qc + (WIN_COLS - 1), 0, 2 * WIN_COLS - 2)
    qs = jnp.clip(qc - WIN_COLS // 2, 0, GRID_W - WIN_COLS)
    in_win = (kc >= qs) & (kc < qs + WIN_COLS)
    for dr in range(2 * WIN_ROWS - 1):
        base = (lh * (2 * WIN_ROWS - 1) + dr) * (2 * WIN_COLS - 1)
        acc = jnp.zeros((GRID_W, GRID_W), F32)
        for d in range(2 * WIN_COLS - 1):
            acc = jnp.where(delta == d, rpb_ref[base + d], acc)
        piece = jnp.where(in_win, acc, -jnp.inf)
        for case in range(WIN_ROWS):
            i = dr + case - (WIN_ROWS - 1)
            if 0 <= i < WIN_ROWS:
                o_ref[case, :, i * GRID_W:(i + 1) * GRID_W] = piece


def _na_bias(rpb):
    return pl.pallas_call(
        _bias_kernel,
        grid=(DEPTH * NA_HEADS,),
        in_specs=[pl.BlockSpec(memory_space=pltpu.SMEM)],
        out_specs=pl.BlockSpec((None, WIN_ROWS, GRID_W, WIN_ROWS * GRID_W), lambda lh: (lh, 0, 0, 0)),
        out_shape=jax.ShapeDtypeStruct((DEPTH * NA_HEADS, WIN_ROWS, GRID_W, WIN_ROWS * GRID_W), F32),
        compiler_params=_cparams(("arbitrary",)),
    )(rpb.reshape(-1))


def _na_kernel(q_ref, kv_ref, kc_ref, vc_ref, bias_ref, o_ref):
    nloc = WIN_ROWS * GRID_W
    for rr in range(NA_RB):
        r = pl.program_id(1) * NA_RB + rr
        start = jnp.clip(r - WIN_ROWS // 2, 0, GRID_ROWS - WIN_ROWS)
        case = r - start
        row0 = pl.multiple_of(start * GRID_W, GRID_W)
        qrows = slice(rr * GRID_W, (rr + 1) * GRID_W)
        for h in range(NA_HEADS):
            q = (q_ref[qrows, C_NAQ + 64 * h:C_NAQ + 64 * h + 64] * QK_SCALE).astype(BF16)
            k = kv_ref[pl.ds(row0, nloc), C_NAK + 64 * h:C_NAK + 64 * h + 64].astype(BF16)
            v = kv_ref[pl.ds(row0, nloc), C_NAV + 64 * h:C_NAV + 64 * h + 64].astype(BF16)
            s_loc = _dot_nt(q, k) + bias_ref[h, pl.ds(case, 1)][0]
            s_ctx = _dot_nt(q, kc_ref[h].astype(BF16))
            m = jnp.maximum(jnp.max(s_loc, axis=-1, keepdims=True), jnp.max(s_ctx, axis=-1, keepdims=True))
            e_loc = jnp.exp(s_loc - m)
            e_ctx = jnp.exp(s_ctx - m)
            l = jnp.sum(e_loc, axis=-1, keepdims=True) + jnp.sum(e_ctx, axis=-1, keepdims=True)
            o = (_dot(e_loc.astype(BF16), v) + _dot(e_ctx.astype(BF16), vc_ref[h].astype(BF16))) * (1.0 / l)
            o_ref[qrows, 64 * h:64 * h + 64] = o.astype(BF16)


NA_RB = 4


def _na_attn(p, cache_k, cache_v, bias, l):
    qblk0 = NP_TOK // (NA_RB * GRID_W)
    kvblk0 = NP_TOK // DEC_SEQ
    nrg = GRID_ROWS // NA_RB
    return pl.pallas_call(
        _na_kernel,
        grid=(DEC_BATCH, nrg),
        in_specs=[pl.BlockSpec((NA_RB * GRID_W, 768), lambda b, r: (qblk0 + b * nrg + r, 0)),
                  pl.BlockSpec((DEC_SEQ, 768), lambda b, r: (kvblk0 + b, 0)),
                  pl.BlockSpec((None, None, NA_HEADS, PAST, HEAD_DIM), lambda b, r: (b, l, 0, 0, 0)),
                  pl.BlockSpec((None, None, NA_HEADS, PAST, HEAD_DIM), lambda b, r: (b, l, 0, 0, 0)),
                  pl.BlockSpec((NA_HEADS, WIN_ROWS, GRID_W, WIN_ROWS * GRID_W), lambda b, r: (l, 0, 0, 0))],
        out_specs=pl.BlockSpec((NA_RB * GRID_W, 256), lambda b, r: (b * nrg + r, 0)),
        out_shape=jax.ShapeDtypeStruct((NS_TOK, 256), BF16),
        compiler_params=_cparams(("arbitrary", "arbitrary")),
    )(p, p, cache_k, cache_v, bias)


@functools.lru_cache(None)
def _rope_tables():
    t = np.arange(DEC_SEQ)
    lane = np.arange(128)
    dd = lane % HEAD_DIM
    pos = np.where(dd[None, :] < 32, (t // GRID_W)[:, None], (t % GRID_W)[:, None]).astype(np.float64)
    inv = ROPE_BASE ** (-(dd % 16).astype(np.float64) * 2.0 / 32.0)
    ang = pos * inv[None, :]
    first = (dd % 32) < 16
    cos = np.cos(ang)
    s_up = np.where(first[None, :], -np.sin(ang), 0.0)
    s_dn = np.where(first[None, :], 0.0, np.sin(ang))
    return tuple(np.asarray(a, np.float32) for a in (cos, s_up, s_dn))


def _rope(x, cos, s_up, s_dn):
    return x * cos + pltpu.roll(x, 112, axis=1) * s_up + pltpu.roll(x, 16, axis=1) * s_dn


def _sdiff_kernel(q_ref, k_ref, v_ref, ck_ref, cv_ref, cos_ref, sup_ref, sdn_ref,
                  lq1, lk1, lq2, lk2, g_ref, o_ref, k1_scr, k2_scr, v_scr, *, lam_init, tq):
    qb = pl.program_id(2)

    @pl.when(qb == 0)
    def _():
        kr = _rope(k_ref[...], cos_ref[...], sup_ref[...], sdn_ref[...])
        k1_scr[0:DEC_SEQ, :] = kr[:, :64].astype(BF16)
        k2_scr[0:DEC_SEQ, :] = kr[:, 64:].astype(BF16)
        k1_scr[DEC_SEQ:, :] = ck_ref[0].astype(BF16)
        k2_scr[DEC_SEQ:, :] = ck_ref[1].astype(BF16)
        v_scr[0:DEC_SEQ, :] = v_ref[...].astype(BF16)
        v_scr[DEC_SEQ:, :] = cv_ref[...].astype(BF16)

    lam = _lam(lq1, lk1, lq2, lk2, lam_init)
    half = tq // 2
    for c in range(2):
        rows = pl.ds(pl.multiple_of(qb * tq + c * half, half), half)
        qr = _rope(q_ref[c * half:(c + 1) * half, :], cos_ref[rows, :], sup_ref[rows, :],
                   sdn_ref[rows, :]) * QK_SCALE
        o = _diff_head(qr[:, :64].astype(BF16), qr[:, 64:].astype(BF16), k1_scr[...], k2_scr[...],
                       v_scr[...], lam, g_ref[...], lam_init)
        o_ref[c * half:(c + 1) * half, :] = o.astype(BF16)


def _sdiff_attn(p, cache_k, cache_v, lq1, lk1, lq2, lk2, g, lam_init, l):
    tq = 512
    nq = DEC_SEQ // tq
    cos, s_up, s_dn = (jnp.asarray(a) for a in _rope_tables())
    vec = lambda n: pl.BlockSpec((1, n), lambda b, h, q: (0, 0))
    tab = pl.BlockSpec((DEC_SEQ, 128), lambda b, h, q: (0, 0))
    kvblk0 = NP_TOK // DEC_SEQ
    return pl.pallas_call(
        functools.partial(_sdiff_kernel, lam_init=lam_init, tq=tq),
        grid=(DEC_BATCH, DIFF_HEADS, nq),
        in_specs=[pl.BlockSpec((tq, 128), lambda b, h, q: (NP_TOK // tq + b * nq + q, C_DQ // 128 + h)),
                  pl.BlockSpec((DEC_SEQ, 128), lambda b, h, q: (kvblk0 + b, C_DK // 128 + h)),
                  pl.BlockSpec((DEC_SEQ, 128), lambda b, h, q: (kvblk0 + b, C_DV // 128 + h)),
                  pl.BlockSpec((None, None, None, 2, PAST, HEAD_DIM), lambda b, h, q: (b, l, h, 0, 0, 0)),
                  pl.BlockSpec((None, None, None, PAST, DIFF_V), lambda b, h, q: (b, l, h, 0, 0)),
                  tab, tab, tab, vec(64), vec(64), vec(64), vec(64), vec(128)],
        out_specs=pl.BlockSpec((tq, 128), lambda b, h, q: (b * nq + q, h)),
        out_shape=jax.ShapeDtypeStruct((NS_TOK, DIFF_HEADS * DIFF_V), BF16),
        scratch_shapes=[pltpu.VMEM((DEC_SEQ + PAST, HEAD_DIM), BF16),
                        pltpu.VMEM((DEC_SEQ + PAST, HEAD_DIM), BF16),
                        pltpu.VMEM((DEC_SEQ + PAST, DIFF_V), BF16)],
        compiler_params=_cparams(("arbitrary", "arbitrary", "arbitrary")),
    )(p, p, p, cache_k, cache_v, cos, s_up, s_dn, lq1, lk1, lq2, lk2, g)


@functools.lru_cache(None)
def _dft_consts(L):
    n = 2 * L
    k = np.arange(L)
    ang = 2.0 * np.pi * ((k[:, None] * k[None, :]) % n) / n
    alt = (-1.0) ** k
    fa = np.cos(ang)
    fb = -np.sin(ang)
    fb[0, :] = alt
    wgt = np.full((L,), 2.0 / n)
    wgt[0] = 1.0 / n
    ga = fa * wgt[:, None]
    gb = fb * wgt[:, None]
    gb[0, :] = alt / n
    f = np.concatenate([fa, fb], axis=0)
    g = np.concatenate([ga.T, gb.T], axis=1)
    return np.asarray(f, dtype=BF16), np.asarray(g, dtype=BF16)


@functools.lru_cache(None)
def _filter_consts(L):
    f32 = np.float32
    t = np.linspace(0.0, 1.0, L, dtype=f32)[:, None]
    pos = np.arange(L, dtype=f32)[:, None]
    bands = np.linspace(1e-4, HY_BANDS - 1, HY_BANDS, dtype=f32)[None, :]
    ang = f32(2.0 * math.pi) * bands * pos / f32(L)
    z = np.zeros((L, 128), f32)
    z[:, 0:1] = t
    z[:, 1:1 + HY_BANDS] = np.cos(ang)
    z[:, 1 + HY_BANDS:HY_EMB] = -np.sin(ang)
    min_decay = math.log(1e-2) / 1.5
    max_decay = math.log(1e-2) / 0.3
    deltas = np.abs(np.linspace(min_decay, max_decay, HY_W, dtype=f32))
    decay = np.exp(-t * deltas[None, :]).astype(f32)
    return z, decay


def _spectra_kernel(z_ref, w1_ref, b1_ref, w2_ref, b2_ref, fr_ref, w3_ref, dec_ref, fa_ref, fb_ref,
                    sa_ref, sb_ref, filt_scr, *, L, kc):
    j = pl.program_id(0)

    @pl.when(j == 0)
    def _():
        fr = fr_ref[...]
        hdn = jnp.sin(fr * (_dot3(z_ref[...], w1_ref[...]) + b1_ref[...]))
        hdn = jnp.sin(fr * (_dot3(hdn, w2_ref[...]) + b2_ref[...]))
        dec = dec_ref[...]
        not_first = lax.broadcasted_iota(jnp.int32, (L, HY_W), 0) > 0
        for o in range(2):
            hf = _dot3(hdn, w3_ref[:, 512 * o:512 * o + 256]) * dec
            hb = jnp.where(not_first, _dot3(hdn, w3_ref[:, 512 * o + 256:512 * o + 512]) * dec, 0.0)
            nrm = (jnp.sum(jnp.abs(hf), axis=0, keepdims=True)
                   + jnp.sum(jnp.abs(hb), axis=0, keepdims=True))
            filt_scr[:, 512 * o:512 * o + 256] = (hf / nrm).astype(BF16)
            filt_scr[:, 512 * o + 256:512 * o + 512] = (hb / nrm).astype(BF16)

    ta = _dot(fa_ref[...], filt_scr[...])
    tb = _dot(fb_ref[...], filt_scr[...])
    first = (lax.broadcasted_iota(jnp.int32, (kc, HY_W), 0) + j * kc) == 0
    for o in range(2):
        af, ab = ta[:, 512 * o:512 * o + 256], ta[:, 512 * o + 256:512 * o + 512]
        bf, bb = tb[:, 512 * o:512 * o + 256], tb[:, 512 * o + 256:512 * o + 512]
        sa_ref[:, 256 * o:256 * o + 256] = af + ab
        sb_ref[:, 256 * o:256 * o + 256] = jnp.where(first, bf + bb, bf - bb)


def _hy_spectra(L, kc, w1p, b1, w2, b2, fr, w3):
    z, decay = _filter_consts(L)
    f = jnp.asarray(_dft_consts(L)[0])
    nj = L // kc
    full = lambda shape: pl.BlockSpec(shape, lambda j: tuple(0 for _ in shape))
    return pl.pallas_call(
        functools.partial(_spectra_kernel, L=L, kc=kc),
        grid=(nj,),
        in_specs=[full((L, 128)), full((128, HY_HID)), full((1, HY_HID)), full((HY_HID, HY_HID)),
                  full((1, HY_HID)), full((1, HY_HID)), full((HY_HID, 4 * HY_W)), full((L, HY_W)),
                  pl.BlockSpec((kc, L), lambda j: (j, 0)),
                  pl.BlockSpec((kc, L), lambda j: (j + nj, 0))],
        out_specs=[pl.BlockSpec((kc, 2 * HY_W), lambda j: (j, 0)),
                   pl.BlockSpec((kc, 2 * HY_W), lambda j: (j, 0))],
        out_shape=[jax.ShapeDtypeStruct((L, 2 * HY_W), F32)] * 2,
        scratch_shapes=[pltpu.VMEM((L, 4 * HY_W), BF16)],
        compiler_params=_cparams(("arbitrary",)),
    )(jnp.asarray(z), w1p, b1, w2, b2, fr, w3, jnp.asarray(decay), f, f)


def _hyconv_kernel(*refs, L, nb, kc, nj):
    u_refs = refs[:nb]
    (cw_ref, cb_ref, d_ref, fa_ref, fb_ref, ga_ref, gb_ref, sa_ref, sb_ref, o_ref,
     xin_scr, x_scr, g1_scr, g2_scr, y_scr) = refs[nb:]
    o = pl.program_id(1)
    j = pl.program_id(2)

    @pl.when((o == 0) & (j == 0))
    def _():
        row = lax.broadcasted_iota(jnp.int32, (L, HY_W), 0)
        for i in range(nb):
            cols = slice(HY_W * i, HY_W * (i + 1))
            for part, dst in enumerate((x_scr, g1_scr, g2_scr)):
                pc = slice(HY_W * part, HY_W * (part + 1))
                u = u_refs[i][:, pc]
                up = jnp.where(row == 0, 0.0, pltpu.roll(u, 1, axis=0))
                un = jnp.where(row == L - 1, 0.0, pltpu.roll(u, L - 1, axis=0))
                dst[:, cols] = (up * cw_ref[0:1, pc] + u * cw_ref[1:2, pc] + un * cw_ref[2:3, pc]
                                + cb_ref[:, pc])
            xin_scr[:, cols] = x_scr[:, cols].astype(BF16)

    @pl.when(j == 0)
    def _():
        y_scr[...] = jnp.zeros_like(y_scr)

    xa = _dot(fa_ref[...], xin_scr[...])
    xb = _dot(fb_ref[...], xin_scr[...])
    sa = jnp.concatenate([sa_ref[...]] * nb, axis=1)
    sb = jnp.concatenate([sb_ref[...]] * nb, axis=1)
    first = (lax.broadcasted_iota(jnp.int32, (kc, nb * HY_W), 0) + j * kc) == 0
    ya = jnp.where(first, xa * sa, xa * sa - xb * sb)
    yb = jnp.where(first, xb * sb, xa * sb + xb * sa)
    y_scr[...] += _dot(ga_ref[...], ya.astype(BF16)) + _dot(gb_ref[...], yb.astype(BF16))

    @pl.when(j == nj - 1)
    def _():
        dvec = jnp.concatenate([d_ref[pl.ds(o, 1), :]] * nb, axis=1)
        y = y_scr[...] + x_scr[...] * dvec

        @pl.when(o == 0)
        def _():
            zz = g1_scr[...] * y
            x_scr[...] = zz
            xin_scr[...] = zz.astype(BF16)

        @pl.when(o == 1)
        def _():
            res = g2_scr[...] * y
            for i in range(nb):
                o_ref[L * i:L * (i + 1), :] = res[:, HY_W * i:HY_W * (i + 1)].astype(BF16)


def _hy_conv(p, L, nb, kc, row_blk0, n_seq, conv_w, conv_b, d, sa, sb):
    f, g = (jnp.asarray(a) for a in _dft_consts(L))
    nj = L // kc
    u_spec = lambda i: pl.BlockSpec((L, 3 * HY_W), lambda bg, o, j: (row_blk0 + bg * nb + i, C_HY // 768),
                                    pipeline_mode=pl.Buffered(1))
    small = lambda shape: pl.BlockSpec(shape, lambda bg, o, j: (0, 0))
    return pl.pallas_call(
        functools.partial(_hyconv_kernel, L=L, nb=nb, kc=kc, nj=nj),
        grid=(n_seq // nb, 2, nj),
        in_specs=[u_spec(i) for i in range(nb)] + [
            small((3, 3 * HY_W)), small((1, 3 * HY_W)), small((2, HY_W)),
            pl.BlockSpec((kc, L), lambda bg, o, j: (j, 0)),
            pl.BlockSpec((kc, L), lambda bg, o, j: (j + nj, 0)),
            pl.BlockSpec((L, kc), lambda bg, o, j: (0, j)),
            pl.BlockSpec((L, kc), lambda bg, o, j: (0, j + nj)),
            pl.BlockSpec((kc, HY_W), lambda bg, o, j: (j, o)),
            pl.BlockSpec((kc, HY_W), lambda bg, o, j: (j, o))],
        out_specs=pl.BlockSpec((nb * L, HY_W), lambda bg, o, j: (bg, 0)),
        out_shape=jax.ShapeDtypeStruct((n_seq * L, HY_W), BF16),
        scratch_shapes=[pltpu.VMEM((L, nb * HY_W), BF16)] + [pltpu.VMEM((L, nb * HY_W), F32)] * 4,
        compiler_params=_cparams(("arbitrary", "arbitrary", "arbitrary")),
    )(*([p] * nb), conv_w, conv_b, d, f, f, g, g, sa, sb)


def _outproj_kernel(xp_ref, xs_ref, mixp_ref, hyp_ref, nas_ref, ds_ref, hys_ref, g1_ref, w_ref, o_ref):
    i = pl.program_id(0)
    w = lambda a, b: w_ref[a:b, :].astype(BF16)

    @pl.when(i < NP_TOK // TM)
    def _():
        y = _dot(mixp_ref[...], w(0, 768)) + _dot(hyp_ref[...], w(768, D))
        o_ref[...] = xp_ref[...] + g1_ref[...] * y

    @pl.when(i >= NP_TOK // TM)
    def _():
        y = _dot(nas_ref[...], w(0, 256)) + _dot(ds_ref[...], w(256, 768)) + _dot(hys_ref[...], w(768, D))
        o_ref[...] = xs_ref[...] + g1_ref[...] * y


def _x_specs(xs_block0):
    npt = NP_TOK // TM
    return [pl.BlockSpec((TM, D), lambda i, *_: (jnp.minimum(i, npt - 1), 0)),
            pl.BlockSpec((TM, D), lambda i, *_: (jnp.maximum(i - npt, 0) + xs_block0, 0))]


def _outproj(xp, xs, xs_block0, mix_p, hy_p, na_s, d_s, hy_s, mod6, w_out, l):
    npt = NP_TOK // TM
    pidx = lambda i: (jnp.minimum(i, npt - 1), 0)
    sidx = lambda i: (jnp.maximum(i - npt, 0), 0)
    return pl.pallas_call(
        _outproj_kernel,
        grid=(N_TOK // TM,),
        in_specs=_x_specs(xs_block0) + [
                  pl.BlockSpec((TM, 768), pidx),
                  pl.BlockSpec((TM, HY_W), pidx),
                  pl.BlockSpec((TM, 256), sidx),
                  pl.BlockSpec((TM, 512), sidx),
                  pl.BlockSpec((TM, HY_W), sidx),
                  _mod_spec(2, TM),
                  pl.BlockSpec((None, D, D), lambda i: (l, 0, 0))],
        out_specs=pl.BlockSpec((TM, D), lambda i: (i, 0)),
        out_shape=jax.ShapeDtypeStruct((N_TOK, D), F32),
        compiler_params=_cparams(("arbitrary",)),
    )(xp, xs, mix_p, hy_p, na_s, d_s, hy_s, mod6, w_out)


def _ffn_kernel(x_ref, g_ref, sc_ref, sh_ref, gate_ref, wg_ref, wu_ref, wd_ref, o_ref, h_scr, *, nj):
    j = pl.program_id(1)

    @pl.when(j == 0)
    def _():
        h_scr[...] = _norm_mod(x_ref[...], g_ref[...], sc_ref[...], sh_ref[...]).astype(BF16)
        o_ref[...] = jnp.zeros_like(o_ref)

    h = h_scr[...]
    a = _silu(_dot(h, wg_ref[...].astype(BF16))) * _dot(h, wu_ref[...].astype(BF16))
    o_ref[...] += _dot(a.astype(BF16), wd_ref[...].astype(BF16))

    @pl.when(j == nj - 1)
    def _():
        o_ref[...] = x_ref[...] + gate_ref[...] * o_ref[...]


def _dense_ffn(x, g, mod6, wg, wu, wd, i_ffn):
    tf = 256
    tm = DEC_SEQ
    nj = FFN // tf
    return pl.pallas_call(
        functools.partial(_ffn_kernel, nj=nj),
        grid=(N_TOK // tm, nj),
        in_specs=[pl.BlockSpec((tm, D), lambda i, j: (i, 0)),
                  pl.BlockSpec((1, D), lambda i, j: (0, 0)),
                  _mod_spec(4, tm), _mod_spec(3, tm), _mod_spec(5, tm),
                  pl.BlockSpec((None, D, tf), lambda i, j: (i_ffn, 0, j)),
                  pl.BlockSpec((None, D, tf), lambda i, j: (i_ffn, 0, j)),
                  pl.BlockSpec((None, tf, D), lambda i, j: (i_ffn, j, 0))],
        out_specs=pl.BlockSpec((tm, D), lambda i, j: (i, 0)),
        out_shape=jax.ShapeDtypeStruct((N_TOK, D), F32),
        scratch_shapes=[pltpu.VMEM((tm, D), BF16)],
        compiler_params=_cparams(("arbitrary", "arbitrary")),
    )(x, g, mod6, mod6, mod6, wg, wu, wd)


def _router_kernel(x_ref, g_ref, sc_ref, sh_ref, wr_ref, h_ref, r_ref):
    h = _norm_mod(x_ref[...], g_ref[...], sc_ref[...], sh_ref[...])
    h_ref[...] = h.astype(BF16)
    lane = lax.broadcasted_iota(jnp.int32, (TM, 128), 1)
    lg = jnp.where(lane < N_EXP, _dot3(h, wr_ref[...]), -jnp.inf)
    m1 = jnp.max(lg, axis=-1, keepdims=True)
    i1 = jnp.min(jnp.where(lg == m1, lane, 128), axis=-1, keepdims=True)
    lg2 = jnp.where(lane == i1, -jnp.inf, lg)
    m2 = jnp.max(lg2, axis=-1, keepdims=True)
    i2 = jnp.min(jnp.where(lg2 == m2, lane, 128), axis=-1, keepdims=True)
    e = jnp.exp(m2 - m1)
    w1 = 1.0 / (1.0 + e)
    w2 = e / (1.0 + e)
    r_ref[...] = jnp.where(lane == 0, i1.astype(F32),
                           jnp.where(lane == 1, i2.astype(F32),
                                     jnp.where(lane == 2, w1, jnp.where(lane == 3, w2, 0.0))))


def _router(x, g, mod6, wr_pad):
    return pl.pallas_call(
        _router_kernel,
        grid=(N_TOK // TM,),
        in_specs=[pl.BlockSpec((TM, D), lambda i: (i, 0)),
                  pl.BlockSpec((1, D), lambda i: (0, 0)),
                  _mod_spec(4, TM), _mod_spec(3, TM),
                  pl.BlockSpec((D, 128), lambda i: (0, 0))],
        out_specs=[pl.BlockSpec((TM, D), lambda i: (i, 0)),
                   pl.BlockSpec((TM, 128), lambda i: (i, 0))],
        out_shape=[jax.ShapeDtypeStruct((N_TOK, D), BF16), jax.ShapeDtypeStruct((N_TOK, 128), F32)],
        compiler_params=_cparams(("arbitrary",)),
    )(x, g, mod6, mod6, wr_pad)


def _dispatch_kernel(blo_ref, bhi_ref, sexp_ref, h_hbm, rp_ref, o_ref, h_scr, sem):
    s = pl.program_id(0)

    @pl.when(s == 0)
    def _():
        cp = pltpu.make_async_copy(h_hbm, h_scr, sem)
        cp.start()
        cp.wait()

    e = sexp_ref[s]
    rows = s * MOE_TS + lax.broadcasted_iota(jnp.int32, (MOE_TS, MOE_DTB), 0)

    def sel(b):
        return (rows == rp_ref[e, pl.ds(b, 1), :]).astype(BF16)

    is_empty = blo_ref[s] > bhi_ref[s]

    @pl.when(is_empty)
    def _():
        o_ref[...] = jnp.zeros_like(o_ref)

    @pl.when(jnp.logical_not(is_empty))
    def _():
        b0 = jnp.minimum(blo_ref[s], N_TOK // MOE_DTB - MOE_DWIN)
        hwin = h_scr[pl.ds(pl.multiple_of(b0 * MOE_DTB, MOE_DTB), MOE_DWIN * MOE_DTB), :]
        acc = _dot(jnp.concatenate([sel(b0 + k) for k in range(MOE_DWIN)], axis=1), hwin)

        def body(b, acc):
            hb = h_scr[pl.ds(pl.multiple_of(b * MOE_DTB, MOE_DTB), MOE_DTB), :]
            return acc + _dot(sel(b), hb)

        acc = lax.fori_loop(b0 + MOE_DWIN, bhi_ref[s] + 1, body, acc)
        o_ref[...] = acc.astype(BF16)


def _dispatch(blo, bhi, sexp, h, rp_t):
    return pl.pallas_call(
        _dispatch_kernel,
        grid_spec=pltpu.PrefetchScalarGridSpec(
            num_scalar_prefetch=3,
            grid=(MOE_ROWS // MOE_TS,),
            in_specs=[pl.BlockSpec(memory_space=pl.ANY),
                      pl.BlockSpec((N_EXP, N_TOK // MOE_DTB, MOE_DTB), lambda s, *_: (0, 0, 0))],
            out_specs=pl.BlockSpec((MOE_TS, D), lambda s, *_: (s, 0)),
            scratch_shapes=[pltpu.VMEM((N_TOK, D), BF16), pltpu.SemaphoreType.DMA(())]),
        out_shape=jax.ShapeDtypeStruct((MOE_ROWS, D), BF16),
        compiler_params=_cparams(("arbitrary",)),
    )(blo, bhi, sexp, h, rp_t)


def _experts_kernel(te_ref, used_ref, xs_ref, wg_ref, wu_ref, wd_ref, o_ref, acc_scr, *, nj):
    i = pl.program_id(0)
    j = pl.program_id(1)
    live = i < used_ref[0]

    @pl.when(live)
    def _():
        @pl.when(j == 0)
        def _():
            acc_scr[...] = jnp.zeros_like(acc_scr)

        xb = xs_ref[...]
        a = _silu(_dot(xb, wg_ref[...].astype(BF16))) * _dot(xb, wu_ref[...].astype(BF16))
        acc_scr[...] += _dot(a.astype(BF16), wd_ref[...].astype(BF16))

    @pl.when(j == nj - 1)
    def _():
        o_ref[...] = jnp.where(live, acc_scr[...], 0.0).astype(BF16)


def _experts(tile_expert, used, xs, wg, wu, wd, i_moe):
    nj = EXP_DIM // MOE_TF

    def tile(i, used):
        return jnp.minimum(i, used[0] - 1)

    def chunk(i, j, used):
        return jnp.where(i < used[0], j, nj - 1)

    return pl.pallas_call(
        functools.partial(_experts_kernel, nj=nj),
        grid_spec=pltpu.PrefetchScalarGridSpec(
            num_scalar_prefetch=2,
            grid=(MOE_YROWS // MOE_TM, nj),
            in_specs=[pl.BlockSpec((MOE_TM, D), lambda i, j, te, used: (tile(i, used), 0)),
                      pl.BlockSpec((None, None, D, MOE_TF),
                                   lambda i, j, te, used: (i_moe, te[tile(i, used)], 0, chunk(i, j, used))),
                      pl.BlockSpec((None, None, D, MOE_TF),
                                   lambda i, j, te, used: (i_moe, te[tile(i, used)], 0, chunk(i, j, used))),
                      pl.BlockSpec((None, None, MOE_TF, D),
                                   lambda i, j, te, used: (i_moe, te[tile(i, used)], chunk(i, j, used), 0))],
            out_specs=pl.BlockSpec((MOE_TM, D), lambda i, j, te, used: (i, 0)),
            scratch_shapes=[pltpu.VMEM((MOE_TM, D), F32)]),
        out_shape=jax.ShapeDtypeStruct((MOE_YROWS, D), BF16),
        compiler_params=_cparams(("arbitrary", "arbitrary")),
    )(tile_expert, used, xs, wg, wu, wd)


def _combine_kernel(ws_ref, kind_ref, x_ref, rp_ref, comb_ref, gate_ref, fg_ref, ys_hbm, op_ref, os_ref,
                    win_scr, y_scr, sem):
    b = pl.program_id(0)
    nb = pl.num_programs(0)
    slot = b % 2

    def win_copy(blk, sl, e):
        start = pl.multiple_of(ws_ref[blk * N_EXP + e], MOE_ALIGN)
        return pltpu.make_async_copy(ys_hbm.at[pl.ds(start, MOE_WIN)], win_scr.at[sl, e], sem.at[sl, e])

    @pl.when(b == 0)
    def _():
        for e in range(N_EXP):
            win_copy(0, 0, e).start()

    @pl.when(b + 1 < nb)
    def _():
        for e in range(N_EXP):
            win_copy(b + 1, 1 - slot, e).start()

    col = lax.broadcasted_iota(jnp.int32, (MOE_TB, MOE_WIN), 1)
    y_scr[...] = jnp.zeros_like(y_scr)
    for e in range(N_EXP):
        win_copy(b, slot, e).wait()
        rel = rp_ref[:, e:e + 1] - ws_ref[b * N_EXP + e]
        cw = comb_ref[:, e:e + 1]
        kind = kind_ref[b * N_EXP + e]

        @pl.when(kind == 1)
        def _():
            sel = (rel == lax.broadcasted_iota(jnp.int32, (MOE_TB, MOE_WIN_SMALL), 1)).astype(BF16)
            y_scr[...] += cw * _dot(sel, win_scr[slot, e, 0:MOE_WIN_SMALL, :])

        @pl.when(kind == 0)
        def _():
            y_scr[...] += cw * _dot((rel == col).astype(BF16), win_scr[slot, e])

    x = x_ref[...] + gate_ref[...] * y_scr[...]
    x = (x * lax.rsqrt(jnp.mean(x * x, axis=-1, keepdims=True) + EPS)) * fg_ref[...]

    @pl.when(b < NP_TOK // MOE_TB)
    def _():
        op_ref[...] = x

    @pl.when(b >= NP_TOK // MOE_TB)
    def _():
        os_ref[...] = x


def _combine(ws, kind, x, rp8, comb, mod6, final_g, ys):
    npb = NP_TOK // MOE_TB
    return pl.pallas_call(
        _combine_kernel,
        grid_spec=pltpu.PrefetchScalarGridSpec(
            num_scalar_prefetch=2,
            grid=(N_TOK // MOE_TB,),
            in_specs=[pl.BlockSpec((MOE_TB, D), lambda b, *_: (b, 0)),
                      pl.BlockSpec((MOE_TB, N_EXP), lambda b, *_: (b, 0)),
                      pl.BlockSpec((MOE_TB, N_EXP), lambda b, *_: (b, 0)),
                      _mod_spec(5, MOE_TB),
                      pl.BlockSpec((1, D), lambda b, *_: (0, 0)),
                      pl.BlockSpec(memory_space=pl.ANY)],
            out_specs=[pl.BlockSpec((MOE_TB, D), lambda b, *_: (jnp.minimum(b, npb - 1), 0)),
                       pl.BlockSpec((MOE_TB, D), lambda b, *_: (jnp.maximum(b - npb, 0), 0))],
            scratch_shapes=[pltpu.VMEM((2, N_EXP, MOE_WIN, D), BF16),
                            pltpu.VMEM((MOE_TB, D), F32),
                            pltpu.SemaphoreType.DMA((2, N_EXP))]),
        out_shape=[jax.ShapeDtypeStruct((NP_TOK, D), F32), jax.ShapeDtypeStruct((NS_TOK, D), F32)],
        compiler_params=_cparams(("arbitrary",)),
    )(ws, kind, x, rp8, comb, mod6, final_g, ys)


def _moe(x, g, mod6, router, wg, wu, wd, i_moe, final_g):
    wr_pad = jnp.pad(router, ((0, 0), (0, 128 - N_EXP)))
    h, r = _router(x, g, mod6, wr_pad)

    i32 = jnp.int32
    i12 = r[:, 0:2].astype(i32)
    earange = jnp.arange(N_EXP, dtype=i32)
    hit1 = i12[:, 0:1] == earange[None, :]
    hit2 = i12[:, 1:2] == earange[None, :]
    comb = jnp.where(hit1, r[:, 2:3], 0.0) + jnp.where(hit2, r[:, 3:4], 0.0)
    mask = (hit1 | hit2).astype(i32)
    csum = jnp.cumsum(mask, axis=0)
    counts = csum[-1]
    padded = ((counts + MOE_TM - 1) // MOE_TM) * MOE_TM
    ends = jnp.cumsum(padded)
    starts = ends - padded
    rp8 = jnp.where(mask > 0, starts[None, :] + csum - 1, -1).astype(i32)
    n_tiles = MOE_ROWS // MOE_TM
    tile_expert = jnp.minimum(
        jnp.searchsorted(ends, jnp.arange(n_tiles, dtype=i32) * MOE_TM, side="right"),
        N_EXP - 1).astype(i32)
    used = (ends[-1:] // MOE_TM).astype(i32)

    sub_row0 = jnp.arange(MOE_ROWS // MOE_TS, dtype=i32) * MOE_TS
    sexp = tile_expert[sub_row0 // MOE_TM]
    qlo = sub_row0 - starts[sexp]
    qend = jnp.minimum(qlo + MOE_TS, counts[sexp])
    cbe = csum[MOE_DTB - 1::MOE_DTB, :].T[sexp]
    blo = jnp.sum((cbe <= qlo[:, None]).astype(i32), axis=1)
    bhi = jnp.minimum(jnp.sum((cbe < qend[:, None]).astype(i32), axis=1), N_TOK // MOE_DTB - 1)
    empty = qend <= qlo
    blo = jnp.where(empty, 1, blo).astype(i32)
    bhi = jnp.where(empty, 0, bhi).astype(i32)

    cb = csum[MOE_TB - 1::MOE_TB, :]
    cprev = jnp.concatenate([jnp.zeros((1, N_EXP), i32), cb[:-1]], axis=0)
    ws = (((starts[None, :] + cprev) // MOE_ALIGN) * MOE_ALIGN).reshape(-1).astype(i32)
    n_be = cb - cprev
    kind = jnp.where(n_be == 0, 2, jnp.where(n_be <= MOE_WIN_SMALL - MOE_ALIGN, 1, 0)).reshape(-1).astype(i32)

    rp_t = rp8.T.reshape(N_EXP, N_TOK // MOE_DTB, MOE_DTB)
    xs = _dispatch(blo, bhi, sexp, h, rp_t)
    ys = _experts(tile_expert, used, xs, wg, wu, wd, i_moe)
    return _combine(ws, kind, x, rp8, comb, mod6, final_g, ys)


assert DEPTH == 2

def kernel(x_prompt, x_sample, cache_na_k, cache_na_v, cache_diff_k, cache_diff_v, c, c_ctx, w_in, w_out, ada_w, ada_b, norm_mix_g, norm_ffn_g, na_rpb, diff_lq1, diff_lk1, diff_lq2, diff_lk2, diff_subln_g, hy_conv_w, hy_conv_b, hy_d, hy_f_w1, hy_f_b1, hy_f_w2, hy_f_b2, hy_f_freq, hy_f_w3, ffn_w_gate, ffn_w_up, ffn_w_down, moe_router, moe_w_gate, moe_w_up, moe_w_down, final_norm_g):
    xparts = (x_prompt.reshape(NP_TOK, D), x_sample.reshape(NS_TOK, D), 0)
    cond8 = jnp.concatenate([c_ctx[None, :], c, jnp.zeros((5, D), F32)], axis=0)
    mods = _modulation(cond8, ada_w, ada_b)
    final_g = final_norm_g.reshape(1, D)
    bias = _na_bias(na_rpb)

    leaves = [jnp.zeros((BATCH, DEPTH, NA_HEADS, SEQ, HEAD_DIM), F32),
              jnp.zeros((BATCH, DEPTH, NA_HEADS, SEQ, HEAD_DIM), F32),
              jnp.zeros((BATCH, DEPTH, DIFF_HEADS, 2, SEQ, HEAD_DIM), F32),
              jnp.zeros((BATCH, DEPTH, DIFF_HEADS, SEQ, DIFF_V), F32)]
    for l in range(DEPTH):
        lam_init = 0.8 - 0.6 * math.exp(-0.3 * l)
        mod6 = mods[l].reshape(8, 6, D).transpose(1, 0, 2).reshape(6, 8, 1, D)
        row = lambda a: a[l].reshape(1, -1)
        lq1, lk1, lq2, lk2, subg = row(diff_lq1), row(diff_lk1), row(diff_lq2), row(diff_lk2), row(diff_subln_g)

        p, *leaves = _inproj(*xparts, row(norm_mix_g), mod6, w_in, l, leaves)

        mix_p = _prompt_attn(p, lq1, lk1, lq2, lk2, subg, lam_init)
        na_s = _na_attn(p, cache_na_k, cache_na_v, bias, l)
        d_s = _sdiff_attn(p, cache_diff_k, cache_diff_v, lq1, lk1, lq2, lk2, subg, lam_init, l)

        w1p = jnp.pad(hy_f_w1[l], ((0, 128 - HY_EMB), (0, 0)))
        fargs = (w1p, row(hy_f_b1), hy_f_w2[l], row(hy_f_b2), row(hy_f_freq), hy_f_w3[l])
        cargs = (hy_conv_w[l], row(hy_conv_b), hy_d[l])
        sa_p, sb_p = _hy_spectra(SEQ, SEQ, *fargs)
        hy_p = _hy_conv(p, SEQ, 4, SEQ, 0, BATCH, *cargs, sa_p, sb_p)
        sa_s, sb_s = _hy_spectra(DEC_SEQ, 512, *fargs)
        hy_s = _hy_conv(p, DEC_SEQ, 1, 512, NP_TOK // DEC_SEQ, DEC_BATCH, *cargs, sa_s, sb_s)

        x = _outproj(*xparts, mix_p, hy_p, na_s, d_s, hy_s, mod6, w_out, l)

        if l == 0:
            x = _dense_ffn(x, row(norm_ffn_g), mod6, ffn_w_gate, ffn_w_up, ffn_w_down, 0)
            xparts = (x, x, NP_TOK // TM)
        else:
            yp, ys = _moe(x, row(norm_ffn_g), mod6, moe_router[0], moe_w_gate, moe_w_up, moe_w_down,
                          0, final_g)

    return (yp.reshape(BATCH, SEQ, D), ys.reshape(DEC_BATCH, DEC_SEQ, D), *leaves)
```

```python
import functools
import math

import numpy as np
import jax
import jax.numpy as jnp
from jax import lax
from jax.experimental import pallas as pl
from jax.experimental.pallas import tpu as pltpu

F32 = jnp.float32
BF16 = jnp.bfloat16

D = 1024
BATCH, SEQ = 32, 256
DEC_BATCH, DEC_SEQ = 2, 2048
DEPTH = 2
PAST = 512
GRID_W = 64
GRID_ROWS = DEC_SEQ // GRID_W
HEAD_DIM = 64
NA_HEADS = 4
DIFF_HEADS = 4
DIFF_V = 128
WIN_ROWS, WIN_COLS = 8, 16
HY_W = 256
HY_EMB = 33
HY_BANDS = 16
HY_HID = 64
PROJ = 3072
FFN = 2816
N_EXP = 8
EXP_DIM = 3584
EPS = 1e-6
ROPE_BASE = 10000.0

NP_TOK = BATCH * SEQ
NS_TOK = DEC_BATCH * DEC_SEQ
N_TOK = NP_TOK + NS_TOK

C_NAQ, C_NAK, C_NAV = 0, 256, 512
C_DQ, C_DK, C_DV = 768, 1280, 1792
C_HY = 2304

TM = 1024
VMEM_LIMIT = 56 * 1024 * 1024

MOE_TM = 1024
MOE_TF = 512
MOE_TQ = 256
MOE_ROWS = 2 * N_TOK + N_EXP * MOE_TM
MOE_TS = 256
MOE_DTB = 512
MOE_DWIN = 3
MOE_TB = 512
MOE_ALIGN = 16
MOE_WIN = MOE_TB + MOE_ALIGN
MOE_WINS = (MOE_TB // 4 + MOE_ALIGN, MOE_TB // 2 + MOE_ALIGN, MOE_WIN)
MOE_YROWS = MOE_ROWS + MOE_TM


def _cparams(sem):
    return pltpu.CompilerParams(dimension_semantics=sem, vmem_limit_bytes=VMEM_LIMIT)


def _dot(a, b):
    return jnp.dot(a, b, preferred_element_type=F32)


def _dot_nt(a, b):
    return lax.dot_general(a, b, (((1,), (1,)), ((), ())), preferred_element_type=F32)


def _split(a):
    hi = a.astype(BF16)
    lo = (a - hi.astype(F32)).astype(BF16)
    return hi, lo


def _dot3(a, b):
    ah, al = _split(a)
    bh, bl = _split(b)
    return _dot(ah, bh) + (_dot(ah, bl) + _dot(al, bh))


def _silu(x):
    return x / (1.0 + jnp.exp(-x))


def _mod_row(i, tm):
    t = i * tm
    return jnp.where(t < NP_TOK, 0, 1 + (t - NP_TOK) // DEC_SEQ)


def _mod_spec(k, tm):
    return pl.BlockSpec((None, None, 1, D), lambda i, *_: (k, _mod_row(i, tm), 0, 0))


def _norm_mod(x, g, sc, sh):
    y = x * lax.rsqrt(jnp.mean(x * x, axis=-1, keepdims=True) + EPS)
    return (y * g) * (1.0 + sc) + sh


def _mod_kernel(c_ref, w_ref, b_ref, o_ref):
    o_ref[...] = _dot3(_silu(c_ref[...]), w_ref[...]) + b_ref[...]


def _modulation(cond8, ada_w, ada_b):
    tn = 1536
    return pl.pallas_call(
        _mod_kernel,
        grid=(DEPTH, 6 * D // tn),
        in_specs=[pl.BlockSpec((8, D), lambda l, j: (0, 0)),
                  pl.BlockSpec((None, D, tn), lambda l, j: (l, 0, j)),
                  pl.BlockSpec((None, 1, tn), lambda l, j: (l, 0, j))],
        out_specs=pl.BlockSpec((None, 8, tn), lambda l, j: (l, 0, j)),
        out_shape=jax.ShapeDtypeStruct((DEPTH, 8, 6 * D), F32),
        compiler_params=_cparams(("arbitrary", "arbitrary")),
    )(cond8, ada_w, ada_b.reshape(DEPTH, 1, 6 * D))


def _inproj_kernel(xp_ref, xs_ref, g_ref, sc_ref, sh_ref, w_hbm, nak_in, nav_in, dk_in, dv_in,
                   o_ref, nak_ref, nav_ref, dk_ref, dv_ref, h_scr, w_scr, stage, sem, *, l, tn):
    del nak_in, nav_in, dk_in, dv_in
    i = pl.program_id(0)
    j = pl.program_id(1)

    @pl.when((i == 0) & (j == 0))
    def _():
        for c in range(PROJ // tn):
            cp = pltpu.make_async_copy(w_hbm.at[l, :, c * tn:(c + 1) * tn], stage, sem)
            cp.start()
            cp.wait()
            w_scr[c] = stage[...].astype(BF16)

    is_ctx = i < NP_TOK // TM

    @pl.when((j == 0) & is_ctx)
    def _():
        h_scr[...] = _norm_mod(xp_ref[...], g_ref[...], sc_ref[...], sh_ref[...]).astype(BF16)

    @pl.when((j == 0) & jnp.logical_not(is_ctx))
    def _():
        h_scr[...] = _norm_mod(xs_ref[...], g_ref[...], sc_ref[...], sh_ref[...]).astype(BF16)

    o_ref[...] = _dot(h_scr[...], w_scr[j])


    def rows(bb):
        return slice(bb * SEQ, (bb + 1) * SEQ)

    def copy_heads(dst_ref, c0, width, heads, sub=None):
        for bb in range(TM // SEQ):
            for h in heads:
                if sub is None:
                    dst_ref[bb, h] = o_ref[rows(bb), c0(h):c0(h) + width]
                else:
                    for s in range(2):
                        dst_ref[bb, h, s] = o_ref[rows(bb), c0(h) + s * width:c0(h) + (s + 1) * width]

    @pl.when(is_ctx & (j == 0))
    def _():
        copy_heads(nak_ref, lambda h: C_NAK + HEAD_DIM * h, HEAD_DIM, range(NA_HEADS))
        copy_heads(nav_ref, lambda h: C_NAV + HEAD_DIM * h, HEAD_DIM, range(NA_HEADS))

    @pl.when(is_ctx & (j == 1))
    def _():
        copy_heads(dk_ref, lambda h: C_DK - tn + 2 * HEAD_DIM * h, HEAD_DIM, range(0, 2), sub=True)

    @pl.when(is_ctx & (j == 2))
    def _():
        copy_heads(dk_ref, lambda h: C_DK - 2 * tn + 2 * HEAD_DIM * h, HEAD_DIM, range(2, 4), sub=True)
        copy_heads(dv_ref, lambda h: C_DV - 2 * tn + DIFF_V * h, DIFF_V, range(DIFF_HEADS))


def _inproj(xp, xs, xs_block0, g, mod6, w_in, l, leaves):
    tn = 768
    nb = TM // SEQ
    ctx = lambda i: jnp.minimum(i, NP_TOK // TM - 1)
    leaf_specs = [pl.BlockSpec((nb, None, NA_HEADS, SEQ, HEAD_DIM), lambda i, j: (ctx(i), l, 0, 0, 0)),
                  pl.BlockSpec((nb, None, NA_HEADS, SEQ, HEAD_DIM), lambda i, j: (ctx(i), l, 0, 0, 0)),
                  pl.BlockSpec((nb, None, DIFF_HEADS, 2, SEQ, HEAD_DIM), lambda i, j: (ctx(i), l, 0, 0, 0, 0)),
                  pl.BlockSpec((nb, None, DIFF_HEADS, SEQ, DIFF_V), lambda i, j: (ctx(i), l, 0, 0, 0))]
    return pl.pallas_call(
        functools.partial(_inproj_kernel, l=l, tn=tn),
        grid=(N_TOK // TM, PROJ // tn),
        in_specs=_x_specs(xs_block0) + [
                  pl.BlockSpec((1, D), lambda i, j: (0, 0)),
                  _mod_spec(1, TM), _mod_spec(0, TM),
                  pl.BlockSpec(memory_space=pl.ANY)] + [pl.BlockSpec(memory_space=pl.ANY)] * 4,
        out_specs=[pl.BlockSpec((TM, tn), lambda i, j: (i, j))] + leaf_specs,
        out_shape=[jax.ShapeDtypeStruct((N_TOK, PROJ), F32)]
        + [jax.ShapeDtypeStruct(a.shape, a.dtype) for a in leaves],
        input_output_aliases={6: 1, 7: 2, 8: 3, 9: 4},
        scratch_shapes=[pltpu.VMEM((TM, D), BF16), pltpu.VMEM((PROJ // tn, D, tn), BF16),
                        pltpu.VMEM((D, tn), F32), pltpu.SemaphoreType.DMA(())],
        compiler_params=_cparams(("arbitrary", "arbitrary")),
    )(xp, xs, g, mod6, mod6, w_in, *leaves)


def _lam(lq1, lk1, lq2, lk2, lam_init):
    return (jnp.exp(jnp.sum(lq1[...] * lk1[...], axis=-1, keepdims=True))
            - jnp.exp(jnp.sum(lq2[...] * lk2[...], axis=-1, keepdims=True)) + lam_init)


def _softmax_parts(s):
    m = jnp.max(s, axis=-1, keepdims=True)
    e = jnp.exp(s - m)
    return e, jnp.sum(e, axis=-1, keepdims=True)


QK_SCALE = HEAD_DIM ** -0.5


def _diff_head(q1, q2, k1, k2, v, lam, g, lam_init):
    e1, l1 = _softmax_parts(_dot_nt(q1, k1))
    e2, l2 = _softmax_parts(_dot_nt(q2, k2))
    a = e1 - (lam * l1 * (1.0 / l2)) * e2
    o = _dot(a.astype(BF16), v) * (1.0 / l1)
    o = o * lax.rsqrt(jnp.mean(o * o, axis=-1, keepdims=True) + EPS)
    return (o * g) * (1.0 - lam_init)


def _prompt_attn_kernel(pa_ref, pb_ref, pc_ref, lq1, lk1, lq2, lk2, g_ref, o_ref, *, lam_init):
    lam = _lam(lq1, lk1, lq2, lk2, lam_init)
    g = g_ref[...]

    def col(c0, w, scale=None):
        ref = (pa_ref, pb_ref, pc_ref)[c0 // 768]
        o = c0 % 768
        a = ref[:, o:o + w]
        return (a if scale is None else a * scale).astype(BF16)

    for h in range(NA_HEADS):
        q = col(C_NAQ + 64 * h, 64, QK_SCALE)
        k = col(C_NAK + 64 * h, 64)
        v = col(C_NAV + 64 * h, 64)
        e, l = _softmax_parts(_dot_nt(q, k))
        o = _dot(e.astype(BF16), v) * (1.0 / l)
        o_ref[:, 64 * h:64 * h + 64] = o.astype(BF16)
    for h in range(DIFF_HEADS):
        q1 = col(C_DQ + 128 * h, 64, QK_SCALE)
        q2 = col(C_DQ + 128 * h + 64, 64, QK_SCALE)
        k1 = col(C_DK + 128 * h, 64)
        k2 = col(C_DK + 128 * h + 64, 64)
        v = col(C_DV + 128 * h, 128)
        o = _diff_head(q1, q2, k1, k2, v, lam, g, lam_init)
        o_ref[:, 256 + 128 * h:384 + 128 * h] = o.astype(BF16)


def _prompt_attn(p, lq1, lk1, lq2, lk2, g, lam_init):
    vec = lambda n: pl.BlockSpec((1, n), lambda b: (0, 0))
    return pl.pallas_call(
        functools.partial(_prompt_attn_kernel, lam_init=lam_init),
        grid=(BATCH,),
        in_specs=[pl.BlockSpec((SEQ, 768), lambda b: (b, 0)),
                  pl.BlockSpec((SEQ, 768), lambda b: (b, 1)),
                  pl.BlockSpec((SEQ, 768), lambda b: (b, 2)),
                  vec(64), vec(64), vec(64), vec(64), vec(128)],
        out_specs=pl.BlockSpec((SEQ, 768), lambda b: (b, 0)),
        out_shape=jax.ShapeDtypeStruct((NP_TOK, 768), BF16),
        compiler_params=_cparams(("arbitrary",)),
    )(p, p, p, lq1, lk1, lq2, lk2, g)


def _bias_kernel(rpb_ref, o_ref):
    lh = pl.program_id(0)
    qc = lax.broadcasted_iota(jnp.int32, (GRID_W, GRID_W), 0)
    kc = lax.broadcasted_iota(jnp.int32, (GRID_W, GRID_W), 1)
    delta = jnp.clip(kc - qc + (WIN_COLS - 1), 0, 2 * WIN_COLS - 2)
    qs = jnp.clip(qc - WIN_COLS // 2, 0, GRID_W - WIN_COLS)
    in_win = (kc >= qs) & (kc < qs + WIN_COLS)
    for dr in range(2 * WIN_ROWS - 1):
        base = (lh * (2 * WIN_ROWS - 1) + dr) * (2 * WIN_COLS - 1)
        acc = jnp.zeros((GRID_W, GRID_W), F32)
        for d in range(2 * WIN_COLS - 1):
            acc = jnp.where(delta == d, rpb_ref[base + d], acc)
        piece = jnp.where(in_win, acc, -jnp.inf)
        for case in range(WIN_ROWS):
            i = dr + case - (WIN_ROWS - 1)
            if 0 <= i < WIN_ROWS:
                o_ref[case, :, i * GRID_W:(i + 1) * GRID_W] = piece


def _na_bias(rpb):
    return pl.pallas_call(
        _bias_kernel,
        grid=(DEPTH * NA_HEADS,),
        in_specs=[pl.BlockSpec(memory_space=pltpu.SMEM)],
        out_specs=pl.BlockSpec((None, WIN_ROWS, GRID_W, WIN_ROWS * GRID_W), lambda lh: (lh, 0, 0, 0)),
        out_shape=jax.ShapeDtypeStruct((DEPTH * NA_HEADS, WIN_ROWS, GRID_W, WIN_ROWS * GRID_W), F32),
        compiler_params=_cparams(("arbitrary",)),
    )(rpb.reshape(-1))


def _na_kernel(q_ref, kv_ref, kc_ref, vc_ref, bias_ref, o_ref):
    nloc = WIN_ROWS * GRID_W
    for rr in range(NA_RB):
        r = pl.program_id(1) * NA_RB + rr
        start = jnp.clip(r - WIN_ROWS // 2, 0, GRID_ROWS - WIN_ROWS)
        case = r - start
        row0 = pl.multiple_of(start * GRID_W, GRID_W)
        qrows = slice(rr * GRID_W, (rr + 1) * GRID_W)
        for h in range(NA_HEADS):
            q = (q_ref[qrows, C_NAQ + 64 * h:C_NAQ + 64 * h + 64] * QK_SCALE).astype(BF16)
            k = kv_ref[pl.ds(row0, nloc), C_NAK + 64 * h:C_NAK + 64 * h + 64].astype(BF16)
            v = kv_ref[pl.ds(row0, nloc), C_NAV + 64 * h:C_NAV + 64 * h + 64].astype(BF16)
            s_loc = _dot_nt(q, k) + bias_ref[h, pl.ds(case, 1)][0]
            s_ctx = _dot_nt(q, kc_ref[h].astype(BF16))
            m = jnp.maximum(jnp.max(s_loc, axis=-1, keepdims=True), jnp.max(s_ctx, axis=-1, keepdims=True))
            e_loc = jnp.exp(s_loc - m)
            e_ctx = jnp.exp(s_ctx - m)
            l = jnp.sum(e_loc, axis=-1, keepdims=True) + jnp.sum(e_ctx, axis=-1, keepdims=True)
            o = (_dot(e_loc.astype(BF16), v) + _dot(e_ctx.astype(BF16), vc_ref[h].astype(BF16))) * (1.0 / l)
            o_ref[qrows, 64 * h:64 * h + 64] = o.astype(BF16)


NA_RB = 4


def _na_attn(p, cache_k, cache_v, bias, l):
    qblk0 = NP_TOK // (NA_RB * GRID_W)
    kvblk0 = NP_TOK // DEC_SEQ
    nrg = GRID_ROWS // NA_RB
    return pl.pallas_call(
        _na_kernel,
        grid=(DEC_BATCH, nrg),
        in_specs=[pl.BlockSpec((NA_RB * GRID_W, 768), lambda b, r: (qblk0 + b * nrg + r, 0)),
                  pl.BlockSpec((DEC_SEQ, 768), lambda b, r: (kvblk0 + b, 0)),
                  pl.BlockSpec((None, None, NA_HEADS, PAST, HEAD_DIM), lambda b, r: (b, l, 0, 0, 0)),
                  pl.BlockSpec((None, None, NA_HEADS, PAST, HEAD_DIM), lambda b, r: (b, l, 0, 0, 0)),
                  pl.BlockSpec((NA_HEADS, WIN_ROWS, GRID_W, WIN_ROWS * GRID_W), lambda b, r: (l, 0, 0, 0))],
        out_specs=pl.BlockSpec((NA_RB * GRID_W, 256), lambda b, r: (b * nrg + r, 0)),
        out_shape=jax.ShapeDtypeStruct((NS_TOK, 256), BF16),
        compiler_params=_cparams(("arbitrary", "arbitrary")),
    )(p, p, cache_k, cache_v, bias)


@functools.lru_cache(None)
def _rope_tables():
    t = np.arange(DEC_SEQ)
    lane = np.arange(128)
    dd = lane % HEAD_DIM
    pos = np.where(dd[None, :] < 32, (t // GRID_W)[:, None], (t % GRID_W)[:, None]).astype(np.float64)
    inv = ROPE_BASE ** (-(dd % 16).astype(np.float64) * 2.0 / 32.0)
    ang = pos * inv[None, :]
    first = (dd % 32) < 16
    cos = np.cos(ang)
    s_up = np.where(first[None, :], -np.sin(ang), 0.0)
    s_dn = np.where(first[None, :], 0.0, np.sin(ang))
    return tuple(np.asarray(a, np.float32) for a in (cos, s_up, s_dn))


def _rope(x, cos, s_up, s_dn):
    return x * cos + pltpu.roll(x, 112, axis=1) * s_up + pltpu.roll(x, 16, axis=1) * s_dn


def _sdiff_kernel(q_ref, k_ref, v_ref, ck_ref, cv_ref, cos_ref, sup_ref, sdn_ref,
                  lq1, lk1, lq2, lk2, g_ref, o_ref, k1_scr, k2_scr, v_scr, *, lam_init, tq):
    qb = pl.program_id(2)

    @pl.when(qb == 0)
    def _():
        kr = _rope(k_ref[...], cos_ref[...], sup_ref[...], sdn_ref[...])
        k1_scr[0:DEC_SEQ, :] = kr[:, :64].astype(BF16)
        k2_scr[0:DEC_SEQ, :] = kr[:, 64:].astype(BF16)
        k1_scr[DEC_SEQ:, :] = ck_ref[0].astype(BF16)
        k2_scr[DEC_SEQ:, :] = ck_ref[1].astype(BF16)
        v_scr[0:DEC_SEQ, :] = v_ref[...].astype(BF16)
        v_scr[DEC_SEQ:, :] = cv_ref[...].astype(BF16)

    lam = _lam(lq1, lk1, lq2, lk2, lam_init)
    half = tq // 2
    for c in range(2):
        rows = pl.ds(pl.multiple_of(qb * tq + c * half, half), half)
        qr = _rope(q_ref[c * half:(c + 1) * half, :], cos_ref[rows, :], sup_ref[rows, :],
                   sdn_ref[rows, :]) * QK_SCALE
        o = _diff_head(qr[:, :64].astype(BF16), qr[:, 64:].astype(BF16), k1_scr[...], k2_scr[...],
                       v_scr[...], lam, g_ref[...], lam_init)
        o_ref[c * half:(c + 1) * half, :] = o.astype(BF16)


def _sdiff_attn(p, cache_k, cache_v, lq1, lk1, lq2, lk2, g, lam_init, l):
    tq = 512
    nq = DEC_SEQ // tq
    cos, s_up, s_dn = (jnp.asarray(a) for a in _rope_tables())
    vec = lambda n: pl.BlockSpec((1, n), lambda b, h, q: (0, 0))
    tab = pl.BlockSpec((DEC_SEQ, 128), lambda b, h, q: (0, 0))
    kvblk0 = NP_TOK // DEC_SEQ
    return pl.pallas_call(
        functools.partial(_sdiff_kernel, lam_init=lam_init, tq=tq),
        grid=(DEC_BATCH, DIFF_HEADS, nq),
        in_specs=[pl.BlockSpec((tq, 128), lambda b, h, q: (NP_TOK // tq + b * nq + q, C_DQ // 128 + h)),
                  pl.BlockSpec((DEC_SEQ, 128), lambda b, h, q: (kvblk0 + b, C_DK // 128 + h)),
                  pl.BlockSpec((DEC_SEQ, 128), lambda b, h, q: (kvblk0 + b, C_DV // 128 + h)),
                  pl.BlockSpec((None, None, None, 2, PAST, HEAD_DIM), lambda b, h, q: (b, l, h, 0, 0, 0)),
                  pl.BlockSpec((None, None, None, PAST, DIFF_V), lambda b, h, q: (b, l, h, 0, 0)),
                  tab, tab, tab, vec(64), vec(64), vec(64), vec(64), vec(128)],
        out_specs=pl.BlockSpec((tq, 128), lambda b, h, q: (b * nq + q, h)),
        out_shape=jax.ShapeDtypeStruct((NS_TOK, DIFF_HEADS * DIFF_V), BF16),
        scratch_shapes=[pltpu.VMEM((DEC_SEQ + PAST, HEAD_DIM), BF16),
                        pltpu.VMEM((DEC_SEQ + PAST, HEAD_DIM), BF16),
                        pltpu.VMEM((DEC_SEQ + PAST, DIFF_V), BF16)],
        compiler_params=_cparams(("arbitrary", "arbitrary", "arbitrary")),
    )(p, p, p, cache_k, cache_v, cos, s_up, s_dn, lq1, lk1, lq2, lk2, g)


@functools.lru_cache(None)
def _dft_consts(L):
    n = 2 * L
    k = np.arange(L)
    ang = 2.0 * np.pi * ((k[:, None] * k[None, :]) % n) / n
    alt = (-1.0) ** k
    fa = np.cos(ang)
    fb = -np.sin(ang)
    fb[0, :] = alt
    wgt = np.full((L,), 2.0 / n)
    wgt[0] = 1.0 / n
    ga = fa * wgt[:, None]
    gb = fb * wgt[:, None]
    gb[0, :] = alt / n
    f = np.concatenate([fa, fb], axis=0)
    g = np.concatenate([ga.T, gb.T], axis=1)
    return np.asarray(f, dtype=BF16), np.asarray(g, dtype=BF16)


@functools.lru_cache(None)
def _filter_consts(L):
    f32 = np.float32
    t = np.linspace(0.0, 1.0, L, dtype=f32)[:, None]
    pos = np.arange(L, dtype=f32)[:, None]
    bands = np.linspace(1e-4, HY_BANDS - 1, HY_BANDS, dtype=f32)[None, :]
    ang = f32(2.0 * math.pi) * bands * pos / f32(L)
    z = np.zeros((L, 128), f32)
    z[:, 0:1] = t
    z[:, 1:1 + HY_BANDS] = np.cos(ang)
    z[:, 1 + HY_BANDS:HY_EMB] = -np.sin(ang)
    min_decay = math.log(1e-2) / 1.5
    max_decay = math.log(1e-2) / 0.3
    deltas = np.abs(np.linspace(min_decay, max_decay, HY_W, dtype=f32))
    decay = np.exp(-t * deltas[None, :]).astype(f32)
    return z, decay


def _spectra_kernel(z_ref, w1_ref, b1_ref, w2_ref, b2_ref, fr_ref, w3_ref, dec_ref, fa_ref, fb_ref,
                    sa_ref, sb_ref, filt_scr, *, L, kc):
    j = pl.program_id(0)

    @pl.when(j == 0)
    def _():
        fr = fr_ref[...]
        hdn = jnp.sin(fr * (_dot3(z_ref[...], w1_ref[...]) + b1_ref[...]))
        hdn = jnp.sin(fr * (_dot3(hdn, w2_ref[...]) + b2_ref[...]))
        dec = dec_ref[...]
        not_first = lax.broadcasted_iota(jnp.int32, (L, HY_W), 0) > 0
        for o in range(2):
            hf = _dot3(hdn, w3_ref[:, 512 * o:512 * o + 256]) * dec
            hb = jnp.where(not_first, _dot3(hdn, w3_ref[:, 512 * o + 256:512 * o + 512]) * dec, 0.0)
            nrm = (jnp.sum(jnp.abs(hf), axis=0, keepdims=True)
                   + jnp.sum(jnp.abs(hb), axis=0, keepdims=True))
            filt_scr[:, 512 * o:512 * o + 256] = (hf / nrm).astype(BF16)
            filt_scr[:, 512 * o + 256:512 * o + 512] = (hb / nrm).astype(BF16)

    ta = _dot(fa_ref[...], filt_scr[...])
    tb = _dot(fb_ref[...], filt_scr[...])
    first = (lax.broadcasted_iota(jnp.int32, (kc, HY_W), 0) + j * kc) == 0
    for o in range(2):
        af, ab = ta[:, 512 * o:512 * o + 256], ta[:, 512 * o + 256:512 * o + 512]
        bf, bb = tb[:, 512 * o:512 * o + 256], tb[:, 512 * o + 256:512 * o + 512]
        sa_ref[:, 256 * o:256 * o + 256] = af + ab
        sb_ref[:, 256 * o:256 * o + 256] = jnp.where(first, bf + bb, bf - bb)


def _hy_spectra(L, kc, w1p, b1, w2, b2, fr, w3):
    z, decay = _filter_consts(L)
    f = jnp.asarray(_dft_consts(L)[0])
    nj = L // kc
    full = lambda shape: pl.BlockSpec(shape, lambda j: tuple(0 for _ in shape))
    return pl.pallas_call(
        functools.partial(_spectra_kernel, L=L, kc=kc),
        grid=(nj,),
        in_specs=[full((L, 128)), full((128, HY_HID)), full((1, HY_HID)), full((HY_HID, HY_HID)),
                  full((1, HY_HID)), full((1, HY_HID)), full((HY_HID, 4 * HY_W)), full((L, HY_W)),
                  pl.BlockSpec((kc, L), lambda j: (j, 0)),
                  pl.BlockSpec((kc, L), lambda j: (j + nj, 0))],
        out_specs=[pl.BlockSpec((kc, 2 * HY_W), lambda j: (j, 0)),
                   pl.BlockSpec((kc, 2 * HY_W), lambda j: (j, 0))],
        out_shape=[jax.ShapeDtypeStruct((L, 2 * HY_W), F32)] * 2,
        scratch_shapes=[pltpu.VMEM((L, 4 * HY_W), BF16)],
        compiler_params=_cparams(("arbitrary",)),
    )(jnp.asarray(z), w1p, b1, w2, b2, fr, w3, jnp.asarray(decay), f, f)


def _hyconv_kernel(*refs, L, nb, kc, nj):
    u_refs = refs[:nb]
    (cw_ref, cb_ref, d_ref, fa_ref, fb_ref, ga_ref, gb_ref, sa_ref, sb_ref, o_ref,
     xin_scr, x_scr, g1_scr, g2_scr, y_scr) = refs[nb:]
    o = pl.program_id(1)
    j = pl.program_id(2)

    @pl.when((o == 0) & (j == 0))
    def _():
        row = lax.broadcasted_iota(jnp.int32, (L, HY_W), 0)
        for i in range(nb):
            cols = slice(HY_W * i, HY_W * (i + 1))
            for part, dst in enumerate((x_scr, g1_scr, g2_scr)):
                pc = slice(HY_W * part, HY_W * (part + 1))
                u = u_refs[i][:, pc]
                up = jnp.where(row == 0, 0.0, pltpu.roll(u, 1, axis=0))
                un = jnp.where(row == L - 1, 0.0, pltpu.roll(u, L - 1, axis=0))
                dst[:, cols] = (up * cw_ref[0:1, pc] + u * cw_ref[1:2, pc] + un * cw_ref[2:3, pc]
                                + cb_ref[:, pc])
            xin_scr[:, cols] = x_scr[:, cols].astype(BF16)

    @pl.when(j == 0)
    def _():
        y_scr[...] = jnp.zeros_like(y_scr)

    xa = _dot(fa_ref[...], xin_scr[...])
    xb = _dot(fb_ref[...], xin_scr[...])
    sa = jnp.concatenate([sa_ref[...]] * nb, axis=1)
    sb = jnp.concatenate([sb_ref[...]] * nb, axis=1)
    first = (lax.broadcasted_iota(jnp.int32, (kc, nb * HY_W), 0) + j * kc) == 0
    ya = jnp.where(first, xa * sa, xa * sa - xb * sb)
    yb = jnp.where(first, xb * sb, xa * sb + xb * sa)
    y_scr[...] += _dot(ga_ref[...], ya.astype(BF16)) + _dot(gb_ref[...], yb.astype(BF16))

    @pl.when(j == nj - 1)
    def _():
        dvec = jnp.concatenate([d_ref[pl.ds(o, 1), :]] * nb, axis=1)
        y = y_scr[...] + x_scr[...] * dvec

        @pl.when(o == 0)
        def _():
            zz = g1_scr[...] * y
            x_scr[...] = zz
            xin_scr[...] = zz.astype(BF16)

        @pl.when(o == 1)
        def _():
            res = g2_scr[...] * y
            for i in range(nb):
                o_ref[L * i:L * (i + 1), :] = res[:, HY_W * i:HY_W * (i + 1)].astype(BF16)


def _hy_conv(p, L, nb, kc, row_blk0, n_seq, conv_w, conv_b, d, sa, sb):
    f, g = (jnp.asarray(a) for a in _dft_consts(L))
    nj = L // kc
    u_spec = lambda i: pl.BlockSpec((L, 3 * HY_W), lambda bg, o, j: (row_blk0 + bg * nb + i, C_HY // 768),
                                    pipeline_mode=pl.Buffered(1))
    small = lambda shape: pl.BlockSpec(shape, lambda bg, o, j: (0, 0))
    return pl.pallas_call(
        functools.partial(_hyconv_kernel, L=L, nb=nb, kc=kc, nj=nj),
        grid=(n_seq // nb, 2, nj),
        in_specs=[u_spec(i) for i in range(nb)] + [
            small((3, 3 * HY_W)), small((1, 3 * HY_W)), small((2, HY_W)),
            pl.BlockSpec((kc, L), lambda bg, o, j: (j, 0)),
            pl.BlockSpec((kc, L), lambda bg, o, j: (j + nj, 0)),
            pl.BlockSpec((L, kc), lambda bg, o, j: (0, j)),
            pl.BlockSpec((L, kc), lambda bg, o, j: (0, j + nj)),
            pl.BlockSpec((kc, HY_W), lambda bg, o, j: (j, o)),
            pl.BlockSpec((kc, HY_W), lambda bg, o, j: (j, o))],
        out_specs=pl.BlockSpec((nb * L, HY_W), lambda bg, o, j: (bg, 0)),
        out_shape=jax.ShapeDtypeStruct((n_seq * L, HY_W), BF16),
        scratch_shapes=[pltpu.VMEM((L, nb * HY_W), BF16)] + [pltpu.VMEM((L, nb * HY_W), F32)] * 4,
        compiler_params=_cparams(("arbitrary", "arbitrary", "arbitrary")),
    )(*([p] * nb), conv_w, conv_b, d, f, f, g, g, sa, sb)


def _outproj_kernel(xp_ref, xs_ref, mixp_ref, hyp_ref, nas_ref, ds_ref, hys_ref, g1_ref, w_ref, o_ref):
    i = pl.program_id(0)
    w = lambda a, b: w_ref[a:b, :].astype(BF16)

    @pl.when(i < NP_TOK // TM)
    def _():
        y = _dot(mixp_ref[...], w(0, 768)) + _dot(hyp_ref[...], w(768, D))
        o_ref[...] = xp_ref[...] + g1_ref[...] * y

    @pl.when(i >= NP_TOK // TM)
    def _():
        y = _dot(nas_ref[...], w(0, 256)) + _dot(ds_ref[...], w(256, 768)) + _dot(hys_ref[...], w(768, D))
        o_ref[...] = xs_ref[...] + g1_ref[...] * y


def _x_specs(xs_block0):
    npt = NP_TOK // TM
    return [pl.BlockSpec((TM, D), lambda i, *_: (jnp.minimum(i, npt - 1), 0)),
            pl.BlockSpec((TM, D), lambda i, *_: (jnp.maximum(i - npt, 0) + xs_block0, 0))]


def _outproj(xp, xs, xs_block0, mix_p, hy_p, na_s, d_s, hy_s, mod6, w_out, l):
    npt = NP_TOK // TM
    pidx = lambda i: (jnp.minimum(i, npt - 1), 0)
    sidx = lambda i: (jnp.maximum(i - npt, 0), 0)
    return pl.pallas_call(
        _outproj_kernel,
        grid=(N_TOK // TM,),
        in_specs=_x_specs(xs_block0) + [
                  pl.BlockSpec((TM, 768), pidx),
                  pl.BlockSpec((TM, HY_W), pidx),
                  pl.BlockSpec((TM, 256), sidx),
                  pl.BlockSpec((TM, 512), sidx),
                  pl.BlockSpec((TM, HY_W), sidx),
                  _mod_spec(2, TM),
                  pl.BlockSpec((None, D, D), lambda i: (l, 0, 0))],
        out_specs=pl.BlockSpec((TM, D), lambda i: (i, 0)),
        out_shape=jax.ShapeDtypeStruct((N_TOK, D), F32),
        compiler_params=_cparams(("arbitrary",)),
    )(xp, xs, mix_p, hy_p, na_s, d_s, hy_s, mod6, w_out)


def _ffn_kernel(x_ref, g_ref, sc_ref, sh_ref, gate_ref, wg_ref, wu_ref, wd_ref, o_ref, h_scr, *, nj):
    j = pl.program_id(1)

    @pl.when(j == 0)
    def _():
        h_scr[...] = _norm_mod(x_ref[...], g_ref[...], sc_ref[...], sh_ref[...]).astype(BF16)
        o_ref[...] = jnp.zeros_like(o_ref)

    h = h_scr[...]
    a = _silu(_dot(h, wg_ref[...].astype(BF16))) * _dot(h, wu_ref[...].astype(BF16))
    o_ref[...] += _dot(a.astype(BF16), wd_ref[...].astype(BF16))

    @pl.when(j == nj - 1)
    def _():
        o_ref[...] = x_ref[...] + gate_ref[...] * o_ref[...]


def _dense_ffn(x, g, mod6, wg, wu, wd, i_ffn):
    tf = 256
    tm = DEC_SEQ
    nj = FFN // tf
    return pl.pallas_call(
        functools.partial(_ffn_kernel, nj=nj),
        grid=(N_TOK // tm, nj),
        in_specs=[pl.BlockSpec((tm, D), lambda i, j: (i, 0)),
                  pl.BlockSpec((1, D), lambda i, j: (0, 0)),
                  _mod_spec(4, tm), _mod_spec(3, tm), _mod_spec(5, tm),
                  pl.BlockSpec((None, D, tf), lambda i, j: (i_ffn, 0, j)),
                  pl.BlockSpec((None, D, tf), lambda i, j: (i_ffn, 0, j)),
                  pl.BlockSpec((None, tf, D), lambda i, j: (i_ffn, j, 0))],
        out_specs=pl.BlockSpec((tm, D), lambda i, j: (i, 0)),
        out_shape=jax.ShapeDtypeStruct((N_TOK, D), F32),
        scratch_shapes=[pltpu.VMEM((tm, D), BF16)],
        compiler_params=_cparams(("arbitrary", "arbitrary")),
    )(x, g, mod6, mod6, mod6, wg, wu, wd)


def _router_kernel(x_ref, g_ref, sc_ref, sh_ref, wr_ref, h_ref, r_ref):
    h = _norm_mod(x_ref[...], g_ref[...], sc_ref[...], sh_ref[...])
    h_ref[...] = h.astype(BF16)
    lane = lax.broadcasted_iota(jnp.int32, (TM, 128), 1)
    lg = jnp.where(lane < N_EXP, _dot3(h, wr_ref[...]), -jnp.inf)
    m1 = jnp.max(lg, axis=-1, keepdims=True)
    i1 = jnp.min(jnp.where(lg == m1, lane, 128), axis=-1, keepdims=True)
    lg2 = jnp.where(lane == i1, -jnp.inf, lg)
    m2 = jnp.max(lg2, axis=-1, keepdims=True)
    i2 = jnp.min(jnp.where(lg2 == m2, lane, 128), axis=-1, keepdims=True)
    e = jnp.exp(m2 - m1)
    w1 = 1.0 / (1.0 + e)
    w2 = e / (1.0 + e)
    r_ref[...] = jnp.where(lane == 0, i1.astype(F32),
                           jnp.where(lane == 1, i2.astype(F32),
                                     jnp.where(lane == 2, w1, jnp.where(lane == 3, w2, 0.0))))


def _router(x, g, mod6, wr_pad):
    return pl.pallas_call(
        _router_kernel,
        grid=(N_TOK // TM,),
        in_specs=[pl.BlockSpec((TM, D), lambda i: (i, 0)),
                  pl.BlockSpec((1, D), lambda i: (0, 0)),
                  _mod_spec(4, TM), _mod_spec(3, TM),
                  pl.BlockSpec((D, 128), lambda i: (0, 0))],
        out_specs=[pl.BlockSpec((TM, D), lambda i: (i, 0)),
                   pl.BlockSpec((TM, 128), lambda i: (i, 0))],
        out_shape=[jax.ShapeDtypeStruct((N_TOK, D), BF16), jax.ShapeDtypeStruct((N_TOK, 128), F32)],
        compiler_params=_cparams(("arbitrary",)),
    )(x, g, mod6, mod6, wr_pad)


def _dispatch_kernel(blo_ref, bhi_ref, sexp_ref, h_hbm, rp_ref, o_ref, h_scr, sem):
    s = pl.program_id(0)

    @pl.when(s == 0)
    def _():
        cp = pltpu.make_async_copy(h_hbm, h_scr, sem)
        cp.start()
        cp.wait()

    e = sexp_ref[s]
    rows = s * MOE_TS + lax.broadcasted_iota(jnp.int32, (MOE_TS, MOE_DTB), 0)

    def sel(b):
        return (rows == rp_ref[e, pl.ds(b, 1), :]).astype(BF16)

    is_empty = blo_ref[s] > bhi_ref[s]

    @pl.when(is_empty)
    def _():
        o_ref[...] = jnp.zeros_like(o_ref)

    @pl.when(jnp.logical_not(is_empty))
    def _():
        b0 = jnp.minimum(blo_ref[s], N_TOK // MOE_DTB - MOE_DWIN)
        hwin = h_scr[pl.ds(pl.multiple_of(b0 * MOE_DTB, MOE_DTB), MOE_DWIN * MOE_DTB), :]
        acc = _dot(jnp.concatenate([sel(b0 + k) for k in range(MOE_DWIN)], axis=1), hwin)

        def body(b, acc):
            hb = h_scr[pl.ds(pl.multiple_of(b * MOE_DTB, MOE_DTB), MOE_DTB), :]
            return acc + _dot(sel(b), hb)

        acc = lax.fori_loop(b0 + MOE_DWIN, bhi_ref[s] + 1, body, acc)
        o_ref[...] = acc.astype(BF16)


def _dispatch(blo, bhi, sexp, h, rp_t):
    return pl.pallas_call(
        _dispatch_kernel,
        grid_spec=pltpu.PrefetchScalarGridSpec(
            num_scalar_prefetch=3,
            grid=(MOE_ROWS // MOE_TS,),
            in_specs=[pl.BlockSpec(memory_space=pl.ANY),
                      pl.BlockSpec((N_EXP, N_TOK // MOE_DTB, MOE_DTB), lambda s, *_: (0, 0, 0))],
            out_specs=pl.BlockSpec((MOE_TS, D), lambda s, *_: (s, 0)),
            scratch_shapes=[pltpu.VMEM((N_TOK, D), BF16), pltpu.SemaphoreType.DMA(())]),
        out_shape=jax.ShapeDtypeStruct((MOE_ROWS, D), BF16),
        compiler_params=_cparams(("arbitrary",)),
    )(blo, bhi, sexp, h, rp_t)


def _experts_kernel(te_ref, used_ref, rows_ref, xs_ref, wg_ref, wu_ref, wd_ref, o_ref, acc_scr, *, nj):
    i = pl.program_id(0)
    j = pl.program_id(1)
    nrows = rows_ref[i]

    @pl.when(j == 0)
    def _():
        acc_scr[...] = jnp.zeros_like(acc_scr)

    for sz in range(MOE_TQ, MOE_TM + 1, MOE_TQ):
        @pl.when(nrows == sz)
        def _():
            xb = xs_ref[0:sz, :]
            a = _silu(_dot(xb, wg_ref[...].astype(BF16))) * _dot(xb, wu_ref[...].astype(BF16))
            acc_scr[0:sz, :] += _dot(a.astype(BF16), wd_ref[...].astype(BF16))

    @pl.when(j == nj - 1)
    def _():
        o_ref[...] = acc_scr[...].astype(BF16)


def _experts(tile_expert, used, tile_rows, xs, wg, wu, wd, i_moe):
    nj = EXP_DIM // MOE_TF

    def tile(i, used):
        return jnp.minimum(i, used[0] - 1)

    def chunk(i, j, used):
        return jnp.where(i < used[0], j, nj - 1)

    return pl.pallas_call(
        functools.partial(_experts_kernel, nj=nj),
        grid_spec=pltpu.PrefetchScalarGridSpec(
            num_scalar_prefetch=3,
            grid=(MOE_YROWS // MOE_TM, nj),
            in_specs=[pl.BlockSpec((MOE_TM, D), lambda i, j, te, used, tr: (tile(i, used), 0)),
                      pl.BlockSpec((None, None, D, MOE_TF),
                                   lambda i, j, te, used, tr: (i_moe, te[tile(i, used)], 0, chunk(i, j, used))),
                      pl.BlockSpec((None, None, D, MOE_TF),
                                   lambda i, j, te, used, tr: (i_moe, te[tile(i, used)], 0, chunk(i, j, used))),
                      pl.BlockSpec((None, None, MOE_TF, D),
                                   lambda i, j, te, used, tr: (i_moe, te[tile(i, used)], chunk(i, j, used), 0))],
            out_specs=pl.BlockSpec((MOE_TM, D), lambda i, j, te, used, tr: (i, 0)),
            scratch_shapes=[pltpu.VMEM((MOE_TM, D), F32)]),
        out_shape=jax.ShapeDtypeStruct((MOE_YROWS, D), BF16),
        compiler_params=_cparams(("arbitrary", "arbitrary")),
    )(tile_expert, used, tile_rows, xs, wg, wu, wd)


def _combine_kernel(ws_ref, kind_ref, x_ref, rp_ref, comb_ref, gate_ref, fg_ref, ys_hbm, op_ref, os_ref,
                    win_scr, y_scr, sem):
    b = pl.program_id(0)
    nb = pl.num_programs(0)
    slot = b % 2

    def win_copy(blk, sl, e):
        start = pl.multiple_of(ws_ref[blk * N_EXP + e], MOE_ALIGN)
        return pltpu.make_async_copy(ys_hbm.at[pl.ds(start, MOE_WIN)], win_scr.at[sl, e], sem.at[sl, e])

    @pl.when(b == 0)
    def _():
        for e in range(N_EXP):
            win_copy(0, 0, e).start()

    @pl.when(b + 1 < nb)
    def _():
        for e in range(N_EXP):
            win_copy(b + 1, 1 - slot, e).start()

    col = lax.broadcasted_iota(jnp.int32, (MOE_TB, MOE_WIN), 1)
    y_scr[...] = jnp.zeros_like(y_scr)
    for e in range(N_EXP):
        win_copy(b, slot, e).wait()
        rel = rp_ref[:, e:e + 1] - ws_ref[b * N_EXP + e]
        cw = comb_ref[:, e:e + 1]
        kind = kind_ref[b * N_EXP + e]

        for k, wn in enumerate(MOE_WINS):
            @pl.when(kind == k)
            def _():
                sel = (rel == col[:, :wn]).astype(BF16) if wn == MOE_WIN else (
                    rel == lax.broadcasted_iota(jnp.int32, (MOE_TB, wn), 1)).astype(BF16)
                y_scr[...] += cw * _dot(sel, win_scr[slot, e, 0:wn, :])

    x = x_ref[...] + gate_ref[...] * y_scr[...]
    x = (x * lax.rsqrt(jnp.mean(x * x, axis=-1, keepdims=True) + EPS)) * fg_ref[...]

    @pl.when(b < NP_TOK // MOE_TB)
    def _():
        op_ref[...] = x

    @pl.when(b >= NP_TOK // MOE_TB)
    def _():
        os_ref[...] = x


def _combine(ws, kind, x, rp8, comb, mod6, final_g, ys):
    npb = NP_TOK // MOE_TB
    return pl.pallas_call(
        _combine_kernel,
        grid_spec=pltpu.PrefetchScalarGridSpec(
            num_scalar_prefetch=2,
            grid=(N_TOK // MOE_TB,),
            in_specs=[pl.BlockSpec((MOE_TB, D), lambda b, *_: (b, 0)),
                      pl.BlockSpec((MOE_TB, N_EXP), lambda b, *_: (b, 0)),
                      pl.BlockSpec((MOE_TB, N_EXP), lambda b, *_: (b, 0)),
                      _mod_spec(5, MOE_TB),
                      pl.BlockSpec((1, D), lambda b, *_: (0, 0)),
                      pl.BlockSpec(memory_space=pl.ANY)],
            out_specs=[pl.BlockSpec((MOE_TB, D), lambda b, *_: (jnp.minimum(b, npb - 1), 0)),
                       pl.BlockSpec((MOE_TB, D), lambda b, *_: (jnp.maximum(b - npb, 0), 0))],
            scratch_shapes=[pltpu.VMEM((2, N_EXP, MOE_WIN, D), BF16),
                            pltpu.VMEM((MOE_TB, D), F32),
                            pltpu.SemaphoreType.DMA((2, N_EXP))]),
        out_shape=[jax.ShapeDtypeStruct((NP_TOK, D), F32), jax.ShapeDtypeStruct((NS_TOK, D), F32)],
        compiler_params=_cparams(("arbitrary",)),
    )(ws, kind, x, rp8, comb, mod6, final_g, ys)


def _moe(x, g, mod6, router, wg, wu, wd, i_moe, final_g):
    wr_pad = jnp.pad(router, ((0, 0), (0, 128 - N_EXP)))
    h, r = _router(x, g, mod6, wr_pad)

    i32 = jnp.int32
    i12 = r[:, 0:2].astype(i32)
    earange = jnp.arange(N_EXP, dtype=i32)
    hit1 = i12[:, 0:1] == earange[None, :]
    hit2 = i12[:, 1:2] == earange[None, :]
    comb = jnp.where(hit1, r[:, 2:3], 0.0) + jnp.where(hit2, r[:, 3:4], 0.0)
    mask = (hit1 | hit2).astype(i32)
    csum = jnp.cumsum(mask, axis=0)
    counts = csum[-1]
    padded = ((counts + MOE_TM - 1) // MOE_TM) * MOE_TM
    ends = jnp.cumsum(padded)
    starts = ends - padded
    rp8 = jnp.where(mask > 0, starts[None, :] + csum - 1, -1).astype(i32)
    n_tiles = MOE_ROWS // MOE_TM
    tile_expert = jnp.minimum(
        jnp.searchsorted(ends, jnp.arange(n_tiles, dtype=i32) * MOE_TM, side="right"),
        N_EXP - 1).astype(i32)
    used = (ends[-1:] // MOE_TM).astype(i32)
    tile_row0 = jnp.arange(MOE_YROWS // MOE_TM, dtype=i32) * MOE_TM
    te_all = jnp.minimum(jnp.searchsorted(ends, tile_row0, side="right"), N_EXP - 1)
    valid = jnp.clip(counts[te_all] - (tile_row0 - starts[te_all]), 0, MOE_TM)
    valid = jnp.where(tile_row0 < ends[-1], valid, 0)
    tile_rows = (((valid + MOE_TQ - 1) // MOE_TQ) * MOE_TQ).astype(i32)

    sub_row0 = jnp.arange(MOE_ROWS // MOE_TS, dtype=i32) * MOE_TS
    sexp = tile_expert[sub_row0 // MOE_TM]
    qlo = sub_row0 - starts[sexp]
    qend = jnp.minimum(qlo + MOE_TS, counts[sexp])
    cbe = csum[MOE_DTB - 1::MOE_DTB, :].T[sexp]
    blo = jnp.sum((cbe <= qlo[:, None]).astype(i32), axis=1)
    bhi = jnp.minimum(jnp.sum((cbe < qend[:, None]).astype(i32), axis=1), N_TOK // MOE_DTB - 1)
    empty = qend <= qlo
    blo = jnp.where(empty, 1, blo).astype(i32)
    bhi = jnp.where(empty, 0, bhi).astype(i32)

    cb = csum[MOE_TB - 1::MOE_TB, :]
    cprev = jnp.concatenate([jnp.zeros((1, N_EXP), i32), cb[:-1]], axis=0)
    ws = (((starts[None, :] + cprev) // MOE_ALIGN) * MOE_ALIGN).reshape(-1).astype(i32)
    n_be = cb - cprev
    kind = sum((n_be > wn - MOE_ALIGN).astype(i32) for wn in MOE_WINS[:-1])
    kind = jnp.where(n_be == 0, len(MOE_WINS), kind).reshape(-1).astype(i32)

    rp_t = rp8.T.reshape(N_EXP, N_TOK // MOE_DTB, MOE_DTB)
    xs = _dispatch(blo, bhi, sexp, h, rp_t)
    ys = _experts(tile_expert, used, tile_rows, xs, wg, wu, wd, i_moe)
    return _combine(ws, kind, x, rp8, comb, mod6, final_g, ys)


assert DEPTH == 2

def kernel(x_prompt, x_sample, cache_na_k, cache_na_v, cache_diff_k, cache_diff_v, c, c_ctx, w_in, w_out, ada_w, ada_b, norm_mix_g, norm_ffn_g, na_rpb, diff_lq1, diff_lk1, diff_lq2, diff_lk2, diff_subln_g, hy_conv_w, hy_conv_b, hy_d, hy_f_w1, hy_f_b1, hy_f_w2, hy_f_b2, hy_f_freq, hy_f_w3, ffn_w_gate, ffn_w_up, ffn_w_down, moe_router, moe_w_gate, moe_w_up, moe_w_down, final_norm_g):
    xparts = (x_prompt.reshape(NP_TOK, D), x_sample.reshape(NS_TOK, D), 0)
    cond8 = jnp.concatenate([c_ctx[None, :], c, jnp.zeros((5, D), F32)], axis=0)
    mods = _modulation(cond8, ada_w, ada_b)
    final_g = final_norm_g.reshape(1, D)
    bias = _na_bias(na_rpb)

    leaves = [jnp.zeros((BATCH, DEPTH, NA_HEADS, SEQ, HEAD_DIM), F32),
              jnp.zeros((BATCH, DEPTH, NA_HEADS, SEQ, HEAD_DIM), F32),
              jnp.zeros((BATCH, DEPTH, DIFF_HEADS, 2, SEQ, HEAD_DIM), F32),
              jnp.zeros((BATCH, DEPTH, DIFF_HEADS, SEQ, DIFF_V), F32)]
    for l in range(DEPTH):
        lam_init = 0.8 - 0.6 * math.exp(-0.3 * l)
        mod6 = mods[l].reshape(8, 6, D).transpose(1, 0, 2).reshape(6, 8, 1, D)
        row = lambda a: a[l].reshape(1, -1)
        lq1, lk1, lq2, lk2, subg = row(diff_lq1), row(diff_lk1), row(diff_lq2), row(diff_lk2), row(diff_subln_g)

        p, *leaves = _inproj(*xparts, row(norm_mix_g), mod6, w_in, l, leaves)

        mix_p = _prompt_attn(p, lq1, lk1, lq2, lk2, subg, lam_init)
        na_s = _na_attn(p, cache_na_k, cache_na_v, bias, l)
        d_s = _sdiff_attn(p, cache_diff_k, cache_diff_v, lq1, lk1, lq2, lk2, subg, lam_init, l)

        w1p = jnp.pad(hy_f_w1[l], ((0, 128 - HY_EMB), (0, 0)))
        fargs = (w1p, row(hy_f_b1), hy_f_w2[l], row(hy_f_b2), row(hy_f_freq), hy_f_w3[l])
        cargs = (hy_conv_w[l], row(hy_conv_b), hy_d[l])
        sa_p, sb_p = _hy_spectra(SEQ, SEQ, *fargs)
        hy_p = _hy_conv(p, SEQ, 4, SEQ, 0, BATCH, *cargs, sa_p, sb_p)
        sa_s, sb_s = _hy_spectra(DEC_SEQ, 512, *fargs)
        hy_s = _hy_conv(p, DEC_SEQ, 1, 512, NP_TOK // DEC_SEQ, DEC_BATCH, *cargs, sa_s, sb_s)

        x = _outproj(*xparts, mix_p, hy_p, na_s, d_s, hy_s, mod6, w_out, l)

        if l == 0:
            x = _dense_ffn(x, row(norm_ffn_g), mod6, ffn_w_gate, ffn_w_up, ffn_w_down, 0)
            xparts = (x, x, NP_TOK // TM)
        else:
            yp, ys = _moe(x, row(norm_ffn_g), mod6, moe_router[0], moe_w_gate, moe_w_up, moe_w_down,
                          0, final_g)

    return (yp.reshape(BATCH, SEQ, D), ys.reshape(DEC_BATCH, DEC_SEQ, D), *leaves)
```

```python
import functools
import math

import numpy as np
import jax
import jax.numpy as jnp
from jax import lax
from jax.experimental import pallas as pl
from jax.experimental.pallas import tpu as pltpu

F32 = jnp.float32
BF16 = jnp.bfloat16

D = 1024
BATCH, SEQ = 32, 256
DEC_BATCH, DEC_SEQ = 2, 2048
DEPTH = 2
PAST = 512
GRID_W = 64
GRID_ROWS = DEC_SEQ // GRID_W
HEAD_DIM = 64
NA_HEADS = 4
DIFF_HEADS = 4
DIFF_V = 128
WIN_ROWS, WIN_COLS = 8, 16
HY_W = 256
HY_EMB = 33
HY_BANDS = 16
HY_HID = 64
PROJ = 3072
FFN = 2816
N_EXP = 8
EXP_DIM = 3584
EPS = 1e-6
ROPE_BASE = 10000.0

NP_TOK = BATCH * SEQ
NS_TOK = DEC_BATCH * DEC_SEQ
N_TOK = NP_TOK + NS_TOK

C_NAQ, C_NAK, C_NAV = 0, 256, 512
C_DQ, C_DK, C_DV = 768, 1280, 1792
C_HY = 2304

TM = 1024
VMEM_LIMIT = 56 * 1024 * 1024

MOE_TM = 1024
MOE_TF = 512
MOE_TQ = 256
MOE_ROWS = 2 * N_TOK + N_EXP * MOE_TM
MOE_TS = 256
MOE_DTB = 512
MOE_DWIN = 3
MOE_TB = 512
MOE_ALIGN = 16
MOE_WIN = MOE_TB + MOE_ALIGN
MOE_WINS = (MOE_TB // 4 + MOE_ALIGN, MOE_TB // 2 + MOE_ALIGN, MOE_WIN)
MOE_YROWS = MOE_ROWS + MOE_TM


def _cparams(sem):
    return pltpu.CompilerParams(dimension_semantics=sem, vmem_limit_bytes=VMEM_LIMIT)


def _dot(a, b):
    return jnp.dot(a, b, preferred_element_type=F32)


def _dot_nt(a, b):
    return lax.dot_general(a, b, (((1,), (1,)), ((), ())), preferred_element_type=F32)


def _split(a):
    hi = a.astype(BF16)
    lo = (a - hi.astype(F32)).astype(BF16)
    return hi, lo


def _dot3(a, b):
    ah, al = _split(a)
    bh, bl = _split(b)
    return _dot(ah, bh) + (_dot(ah, bl) + _dot(al, bh))


def _silu(x):
    return x / (1.0 + jnp.exp(-x))


def _mod_row(i, tm):
    t = i * tm
    return jnp.where(t < NP_TOK, 0, 1 + (t - NP_TOK) // DEC_SEQ)


def _mod_spec(k, tm):
    return pl.BlockSpec((None, None, 1, D), lambda i, *_: (k, _mod_row(i, tm), 0, 0))


def _norm_mod(x, g, sc, sh):
    y = x * lax.rsqrt(jnp.mean(x * x, axis=-1, keepdims=True) + EPS)
    return (y * g) * (1.0 + sc) + sh


def _mod_kernel(c_ref, w_ref, b_ref, o_ref):
    o_ref[...] = _dot3(_silu(c_ref[...]), w_ref[...]) + b_ref[...]


def _modulation(cond8, ada_w, ada_b):
    tn = 1536
    return pl.pallas_call(
        _mod_kernel,
        grid=(DEPTH, 6 * D // tn),
        in_specs=[pl.BlockSpec((8, D), lambda l, j: (0, 0)),
                  pl.BlockSpec((None, D, tn), lambda l, j: (l, 0, j)),
                  pl.BlockSpec((None, 1, tn), lambda l, j: (l, 0, j))],
        out_specs=pl.BlockSpec((None, 8, tn), lambda l, j: (l, 0, j)),
        out_shape=jax.ShapeDtypeStruct((DEPTH, 8, 6 * D), F32),
        compiler_params=_cparams(("arbitrary", "arbitrary")),
    )(cond8, ada_w, ada_b.reshape(DEPTH, 1, 6 * D))


def _inproj_kernel(xp_ref, xs_ref, g_ref, sc_ref, sh_ref, w_hbm, nak_in, nav_in, dk_in, dv_in,
                   o_ref, nak_ref, nav_ref, dk_ref, dv_ref, h_scr, w_scr, stage, sem, *, l, tn):
    del nak_in, nav_in, dk_in, dv_in
    i = pl.program_id(0)
    j = pl.program_id(1)

    @pl.when((i == 0) & (j == 0))
    def _():
        for c in range(PROJ // tn):
            cp = pltpu.make_async_copy(w_hbm.at[l, :, c * tn:(c + 1) * tn], stage, sem)
            cp.start()
            cp.wait()
            w_scr[c] = stage[...].astype(BF16)

    is_ctx = i < NP_TOK // TM

    @pl.when((j == 0) & is_ctx)
    def _():
        h_scr[...] = _norm_mod(xp_ref[...], g_ref[...], sc_ref[...], sh_ref[...]).astype(BF16)

    @pl.when((j == 0) & jnp.logical_not(is_ctx))
    def _():
        h_scr[...] = _norm_mod(xs_ref[...], g_ref[...], sc_ref[...], sh_ref[...]).astype(BF16)

    o_ref[...] = _dot(h_scr[...], w_scr[j])


    def rows(bb):
        return slice(bb * SEQ, (bb + 1) * SEQ)

    def copy_heads(dst_ref, c0, width, heads, sub=None):
        for bb in range(TM // SEQ):
            for h in heads:
                if sub is None:
                    dst_ref[bb, h] = o_ref[rows(bb), c0(h):c0(h) + width]
                else:
                    for s in range(2):
                        dst_ref[bb, h, s] = o_ref[rows(bb), c0(h) + s * width:c0(h) + (s + 1) * width]

    @pl.when(is_ctx & (j == 0))
    def _():
        copy_heads(nak_ref, lambda h: C_NAK + HEAD_DIM * h, HEAD_DIM, range(NA_HEADS))
        copy_heads(nav_ref, lambda h: C_NAV + HEAD_DIM * h, HEAD_DIM, range(NA_HEADS))

    @pl.when(is_ctx & (j == 1))
    def _():
        copy_heads(dk_ref, lambda h: C_DK - tn + 2 * HEAD_DIM * h, HEAD_DIM, range(0, 2), sub=True)

    @pl.when(is_ctx & (j == 2))
    def _():
        copy_heads(dk_ref, lambda h: C_DK - 2 * tn + 2 * HEAD_DIM * h, HEAD_DIM, range(2, 4), sub=True)
        copy_heads(dv_ref, lambda h: C_DV - 2 * tn + DIFF_V * h, DIFF_V, range(DIFF_HEADS))


def _inproj(xp, xs, xs_block0, g, mod6, w_in, l, leaves):
    tn = 768
    nb = TM // SEQ
    ctx = lambda i: jnp.minimum(i, NP_TOK // TM - 1)
    leaf_specs = [pl.BlockSpec((nb, None, NA_HEADS, SEQ, HEAD_DIM), lambda i, j: (ctx(i), l, 0, 0, 0)),
                  pl.BlockSpec((nb, None, NA_HEADS, SEQ, HEAD_DIM), lambda i, j: (ctx(i), l, 0, 0, 0)),
                  pl.BlockSpec((nb, None, DIFF_HEADS, 2, SEQ, HEAD_DIM), lambda i, j: (ctx(i), l, 0, 0, 0, 0)),
                  pl.BlockSpec((nb, None, DIFF_HEADS, SEQ, DIFF_V), lambda i, j: (ctx(i), l, 0, 0, 0))]
    return pl.pallas_call(
        functools.partial(_inproj_kernel, l=l, tn=tn),
        grid=(N_TOK // TM, PROJ // tn),
        in_specs=_x_specs(xs_block0) + [
                  pl.BlockSpec((1, D), lambda i, j: (0, 0)),
                  _mod_spec(1, TM), _mod_spec(0, TM),
                  pl.BlockSpec(memory_space=pl.ANY)] + [pl.BlockSpec(memory_space=pl.ANY)] * 4,
        out_specs=[pl.BlockSpec((TM, tn), lambda i, j: (i, j))] + leaf_specs,
        out_shape=[jax.ShapeDtypeStruct((N_TOK, PROJ), F32)]
        + [jax.ShapeDtypeStruct(a.shape, a.dtype) for a in leaves],
        input_output_aliases={6: 1, 7: 2, 8: 3, 9: 4},
        scratch_shapes=[pltpu.VMEM((TM, D), BF16), pltpu.VMEM((PROJ // tn, D, tn), BF16),
                        pltpu.VMEM((D, tn), F32), pltpu.SemaphoreType.DMA(())],
        compiler_params=_cparams(("arbitrary", "arbitrary")),
    )(xp, xs, g, mod6, mod6, w_in, *leaves)


def _lam(lq1, lk1, lq2, lk2, lam_init):
    return (jnp.exp(jnp.sum(lq1[...] * lk1[...], axis=-1, keepdims=True))
            - jnp.exp(jnp.sum(lq2[...] * lk2[...], axis=-1, keepdims=True)) + lam_init)


def _softmax_parts(s):
    m = jnp.max(s, axis=-1, keepdims=True)
    e = jnp.exp(s - m)
    return e, jnp.sum(e, axis=-1, keepdims=True)


QK_SCALE = HEAD_DIM ** -0.5


def _diff_head(q1, q2, k1, k2, v, lam, g, lam_init):
    e1, l1 = _softmax_parts(_dot_nt(q1, k1))
    e2, l2 = _softmax_parts(_dot_nt(q2, k2))
    a = e1 - (lam * l1 * (1.0 / l2)) * e2
    o = _dot(a.astype(BF16), v) * (1.0 / l1)
    o = o * lax.rsqrt(jnp.mean(o * o, axis=-1, keepdims=True) + EPS)
    return (o * g) * (1.0 - lam_init)


def _prompt_attn_kernel(pa_ref, pb_ref, pc_ref, lq1, lk1, lq2, lk2, g_ref, o_ref, *, lam_init):
    lam = _lam(lq1, lk1, lq2, lk2, lam_init)
    g = g_ref[...]

    def col(c0, w, scale=None):
        ref = (pa_ref, pb_ref, pc_ref)[c0 // 768]
        o = c0 % 768
        a = ref[:, o:o + w]
        return (a if scale is None else a * scale).astype(BF16)

    for h in range(NA_HEADS):
        q = col(C_NAQ + 64 * h, 64, QK_SCALE)
        k = col(C_NAK + 64 * h, 64)
        v = col(C_NAV + 64 * h, 64)
        e, l = _softmax_parts(_dot_nt(q, k))
        o = _dot(e.astype(BF16), v) * (1.0 / l)
        o_ref[:, 64 * h:64 * h + 64] = o.astype(BF16)
    for h in range(DIFF_HEADS):
        q1 = col(C_DQ + 128 * h, 64, QK_SCALE)
        q2 = col(C_DQ + 128 * h + 64, 64, QK_SCALE)
        k1 = col(C_DK + 128 * h, 64)
        k2 = col(C_DK + 128 * h + 64, 64)
        v = col(C_DV + 128 * h, 128)
        o = _diff_head(q1, q2, k1, k2, v, lam, g, lam_init)
        o_ref[:, 256 + 128 * h:384 + 128 * h] = o.astype(BF16)


def _prompt_attn(p, lq1, lk1, lq2, lk2, g, lam_init):
    vec = lambda n: pl.BlockSpec((1, n), lambda b: (0, 0))
    return pl.pallas_call(
        functools.partial(_prompt_attn_kernel, lam_init=lam_init),
        grid=(BATCH,),
        in_specs=[pl.BlockSpec((SEQ, 768), lambda b: (b, 0)),
                  pl.BlockSpec((SEQ, 768), lambda b: (b, 1)),
                  pl.BlockSpec((SEQ, 768), lambda b: (b, 2)),
                  vec(64), vec(64), vec(64), vec(64), vec(128)],
        out_specs=pl.BlockSpec((SEQ, 768), lambda b: (b, 0)),
        out_shape=jax.ShapeDtypeStruct((NP_TOK, 768), BF16),
        compiler_params=_cparams(("arbitrary",)),
    )(p, p, p, lq1, lk1, lq2, lk2, g)


def _bias_kernel(rpb_ref, o_ref):
    lh = pl.program_id(0)
    qc = lax.broadcasted_iota(jnp.int32, (GRID_W, GRID_W), 0)
    kc = lax.broadcasted_iota(jnp.int32, (GRID_W, GRID_W), 1)
    delta = jnp.clip(kc - qc + (WIN_COLS - 1), 0, 2 * WIN_COLS - 2)
    qs = jnp.clip(qc - WIN_COLS // 2, 0, GRID_W - WIN_COLS)
    in_win = (kc >= qs) & (kc < qs + WIN_COLS)
    for dr in range(2 * WIN_ROWS - 1):
        base = (lh * (2 * WIN_ROWS - 1) + dr) * (2 * WIN_COLS - 1)
        acc = jnp.zeros((GRID_W, GRID_W), F32)
        for d in range(2 * WIN_COLS - 1):
            acc = jnp.where(delta == d, rpb_ref[base + d], acc)
        piece = jnp.where(in_win, acc, -jnp.inf)
        for case in range(WIN_ROWS):
            i = dr + case - (WIN_ROWS - 1)
            if 0 <= i < WIN_ROWS:
                o_ref[case, :, i * GRID_W:(i + 1) * GRID_W] = piece


def _na_bias(rpb):
    return pl.pallas_call(
        _bias_kernel,
        grid=(DEPTH * NA_HEADS,),
        in_specs=[pl.BlockSpec(memory_space=pltpu.SMEM)],
        out_specs=pl.BlockSpec((None, WIN_ROWS, GRID_W, WIN_ROWS * GRID_W), lambda lh: (lh, 0, 0, 0)),
        out_shape=jax.ShapeDtypeStruct((DEPTH * NA_HEADS, WIN_ROWS, GRID_W, WIN_ROWS * GRID_W), F32),
        compiler_params=_cparams(("arbitrary",)),
    )(rpb.reshape(-1))


def _na_kernel(q_ref, kv_ref, kc_ref, vc_ref, bias_ref, o_ref):
    nloc = WIN_ROWS * GRID_W
    for rr in range(NA_RB):
        r = pl.program_id(1) * NA_RB + rr
        start = jnp.clip(r - WIN_ROWS // 2, 0, GRID_ROWS - WIN_ROWS)
        case = r - start
        row0 = pl.multiple_of(start * GRID_W, GRID_W)
        qrows = slice(rr * GRID_W, (rr + 1) * GRID_W)
        for h in range(NA_HEADS):
            q = (q_ref[qrows, C_NAQ + 64 * h:C_NAQ + 64 * h + 64] * QK_SCALE).astype(BF16)
            k = kv_ref[pl.ds(row0, nloc), C_NAK + 64 * h:C_NAK + 64 * h + 64].astype(BF16)
            v = kv_ref[pl.ds(row0, nloc), C_NAV + 64 * h:C_NAV + 64 * h + 64].astype(BF16)
            s_loc = _dot_nt(q, k) + bias_ref[h, pl.ds(case, 1)][0]
            s_ctx = _dot_nt(q, kc_ref[h].astype(BF16))
            m = jnp.maximum(jnp.max(s_loc, axis=-1, keepdims=True), jnp.max(s_ctx, axis=-1, keepdims=True))
            e_loc = jnp.exp(s_loc - m)
            e_ctx = jnp.exp(s_ctx - m)
            l = jnp.sum(e_loc, axis=-1, keepdims=True) + jnp.sum(e_ctx, axis=-1, keepdims=True)
            o = (_dot(e_loc.astype(BF16), v) + _dot(e_ctx.astype(BF16), vc_ref[h].astype(BF16))) * (1.0 / l)
            o_ref[qrows, 64 * h:64 * h + 64] = o.astype(BF16)


NA_RB = 4


def _na_attn(p, cache_k, cache_v, bias, l):
    qblk0 = NP_TOK // (NA_RB * GRID_W)
    kvblk0 = NP_TOK // DEC_SEQ
    nrg = GRID_ROWS // NA_RB
    return pl.pallas_call(
        _na_kernel,
        grid=(DEC_BATCH, nrg),
        in_specs=[pl.BlockSpec((NA_RB * GRID_W, 768), lambda b, r: (qblk0 + b * nrg + r, 0)),
                  pl.BlockSpec((DEC_SEQ, 768), lambda b, r: (kvblk0 + b, 0)),
                  pl.BlockSpec((None, None, NA_HEADS, PAST, HEAD_DIM), lambda b, r: (b, l, 0, 0, 0)),
                  pl.BlockSpec((None, None, NA_HEADS, PAST, HEAD_DIM), lambda b, r: (b, l, 0, 0, 0)),
                  pl.BlockSpec((NA_HEADS, WIN_ROWS, GRID_W, WIN_ROWS * GRID_W), lambda b, r: (l, 0, 0, 0))],
        out_specs=pl.BlockSpec((NA_RB * GRID_W, 256), lambda b, r: (b * nrg + r, 0)),
        out_shape=jax.ShapeDtypeStruct((NS_TOK, 256), BF16),
        compiler_params=_cparams(("arbitrary", "arbitrary")),
    )(p, p, cache_k, cache_v, bias)


@functools.lru_cache(None)
def _rope_tables():
    t = np.arange(DEC_SEQ)
    lane = np.arange(128)
    dd = lane % HEAD_DIM
    pos = np.where(dd[None, :] < 32, (t // GRID_W)[:, None], (t % GRID_W)[:, None]).astype(np.float64)
    inv = ROPE_BASE ** (-(dd % 16).astype(np.float64) * 2.0 / 32.0)
    ang = pos * inv[None, :]
    first = (dd % 32) < 16
    cos = np.cos(ang)
    s_up = np.where(first[None, :], -np.sin(ang), 0.0)
    s_dn = np.where(first[None, :], 0.0, np.sin(ang))
    return tuple(np.asarray(a, np.float32) for a in (cos, s_up, s_dn))


def _rope(x, cos, s_up, s_dn):
    return x * cos + pltpu.roll(x, 112, axis=1) * s_up + pltpu.roll(x, 16, axis=1) * s_dn


def _sdiff_kernel(q_ref, k_ref, v_ref, ck_ref, cv_ref, cos_ref, sup_ref, sdn_ref,
                  lq1, lk1, lq2, lk2, g_ref, o_ref, k1_scr, k2_scr, v_scr, *, lam_init, tq):
    qb = pl.program_id(2)

    @pl.when(qb == 0)
    def _():
        kr = _rope(k_ref[...], cos_ref[...], sup_ref[...], sdn_ref[...])
        k1_scr[0:DEC_SEQ, :] = kr[:, :64].astype(BF16)
        k2_scr[0:DEC_SEQ, :] = kr[:, 64:].astype(BF16)
        k1_scr[DEC_SEQ:, :] = ck_ref[0].astype(BF16)
        k2_scr[DEC_SEQ:, :] = ck_ref[1].astype(BF16)
        v_scr[0:DEC_SEQ, :] = v_ref[...].astype(BF16)
        v_scr[DEC_SEQ:, :] = cv_ref[...].astype(BF16)

    lam = _lam(lq1, lk1, lq2, lk2, lam_init)
    half = tq // 2
    for c in range(2):
        rows = pl.ds(pl.multiple_of(qb * tq + c * half, half), half)
        qr = _rope(q_ref[c * half:(c + 1) * half, :], cos_ref[rows, :], sup_ref[rows, :],
                   sdn_ref[rows, :]) * QK_SCALE
        o = _diff_head(qr[:, :64].astype(BF16), qr[:, 64:].astype(BF16), k1_scr[...], k2_scr[...],
                       v_scr[...], lam, g_ref[...], lam_init)
        o_ref[c * half:(c + 1) * half, :] = o.astype(BF16)


def _sdiff_attn(p, cache_k, cache_v, lq1, lk1, lq2, lk2, g, lam_init, l):
    tq = 512
    nq = DEC_SEQ // tq
    cos, s_up, s_dn = (jnp.asarray(a) for a in _rope_tables())
    vec = lambda n: pl.BlockSpec((1, n), lambda b, h, q: (0, 0))
    tab = pl.BlockSpec((DEC_SEQ, 128), lambda b, h, q: (0, 0))
    kvblk0 = NP_TOK // DEC_SEQ
    return pl.pallas_call(
        functools.partial(_sdiff_kernel, lam_init=lam_init, tq=tq),
        grid=(DEC_BATCH, DIFF_HEADS, nq),
        in_specs=[pl.BlockSpec((tq, 128), lambda b, h, q: (NP_TOK // tq + b * nq + q, C_DQ // 128 + h)),
                  pl.BlockSpec((DEC_SEQ, 128), lambda b, h, q: (kvblk0 + b, C_DK // 128 + h)),
                  pl.BlockSpec((DEC_SEQ, 128), lambda b, h, q: (kvblk0 + b, C_DV // 128 + h)),
                  pl.BlockSpec((None, None, None, 2, PAST, HEAD_DIM), lambda b, h, q: (b, l, h, 0, 0, 0)),
                  pl.BlockSpec((None, None, None, PAST, DIFF_V), lambda b, h, q: (b, l, h, 0, 0)),
                  tab, tab, tab, vec(64), vec(64), vec(64), vec(64), vec(128)],
        out_specs=pl.BlockSpec((tq, 128), lambda b, h, q: (b * nq + q, h)),
        out_shape=jax.ShapeDtypeStruct((NS_TOK, DIFF_HEADS * DIFF_V), BF16),
        scratch_shapes=[pltpu.VMEM((DEC_SEQ + PAST, HEAD_DIM), BF16),
                        pltpu.VMEM((DEC_SEQ + PAST, HEAD_DIM), BF16),
                        pltpu.VMEM((DEC_SEQ + PAST, DIFF_V), BF16)],
        compiler_params=_cparams(("arbitrary", "arbitrary", "arbitrary")),
    )(p, p, p, cache_k, cache_v, cos, s_up, s_dn, lq1, lk1, lq2, lk2, g)


@functools.lru_cache(None)
def _dft_consts(L):
    n = 2 * L
    k = np.arange(L)
    ang = 2.0 * np.pi * ((k[:, None] * k[None, :]) % n) / n
    alt = (-1.0) ** k
    fa = np.cos(ang)
    fb = -np.sin(ang)
    fb[0, :] = alt
    wgt = np.full((L,), 2.0 / n)
    wgt[0] = 1.0 / n
    ga = fa * wgt[:, None]
    gb = fb * wgt[:, None]
    gb[0, :] = alt / n
    f = np.concatenate([fa, fb], axis=0)
    g = np.concatenate([ga.T, gb.T], axis=1)
    return np.asarray(f, dtype=BF16), np.asarray(g, dtype=BF16)


@functools.lru_cache(None)
def _filter_consts(L):
    f32 = np.float32
    t = np.linspace(0.0, 1.0, L, dtype=f32)[:, None]
    pos = np.arange(L, dtype=f32)[:, None]
    bands = np.linspace(1e-4, HY_BANDS - 1, HY_BANDS, dtype=f32)[None, :]
    ang = f32(2.0 * math.pi) * bands * pos / f32(L)
    z = np.zeros((L, 128), f32)
    z[:, 0:1] = t
    z[:, 1:1 + HY_BANDS] = np.cos(ang)
    z[:, 1 + HY_BANDS:HY_EMB] = -np.sin(ang)
    min_decay = math.log(1e-2) / 1.5
    max_decay = math.log(1e-2) / 0.3
    deltas = np.abs(np.linspace(min_decay, max_decay, HY_W, dtype=f32))
    decay = np.exp(-t * deltas[None, :]).astype(f32)
    return z, decay


def _spectra_kernel(z_ref, w1_ref, b1_ref, w2_ref, b2_ref, fr_ref, w3_ref, dec_ref, fa_ref, fb_ref,
                    sa_ref, sb_ref, filt_scr, *, L, kc):
    j = pl.program_id(0)

    @pl.when(j == 0)
    def _():
        fr = fr_ref[...]
        hdn = jnp.sin(fr * (_dot3(z_ref[...], w1_ref[...]) + b1_ref[...]))
        hdn = jnp.sin(fr * (_dot3(hdn, w2_ref[...]) + b2_ref[...]))
        dec = dec_ref[...]
        not_first = lax.broadcasted_iota(jnp.int32, (L, HY_W), 0) > 0
        for o in range(2):
            hf = _dot3(hdn, w3_ref[:, 512 * o:512 * o + 256]) * dec
            hb = jnp.where(not_first, _dot3(hdn, w3_ref[:, 512 * o + 256:512 * o + 512]) * dec, 0.0)
            nrm = (jnp.sum(jnp.abs(hf), axis=0, keepdims=True)
                   + jnp.sum(jnp.abs(hb), axis=0, keepdims=True))
            filt_scr[:, 512 * o:512 * o + 256] = (hf / nrm).astype(BF16)
            filt_scr[:, 512 * o + 256:512 * o + 512] = (hb / nrm).astype(BF16)

    ta = _dot(fa_ref[...], filt_scr[...])
    tb = _dot(fb_ref[...], filt_scr[...])
    first = (lax.broadcasted_iota(jnp.int32, (kc, HY_W), 0) + j * kc) == 0
    for o in range(2):
        af, ab = ta[:, 512 * o:512 * o + 256], ta[:, 512 * o + 256:512 * o + 512]
        bf, bb = tb[:, 512 * o:512 * o + 256], tb[:, 512 * o + 256:512 * o + 512]
        sa_ref[:, 256 * o:256 * o + 256] = af + ab
        sb_ref[:, 256 * o:256 * o + 256] = jnp.where(first, bf + bb, bf - bb)


def _hy_spectra(L, kc, w1p, b1, w2, b2, fr, w3):
    z, decay = _filter_consts(L)
    f = jnp.asarray(_dft_consts(L)[0])
    nj = L // kc
    full = lambda shape: pl.BlockSpec(shape, lambda j: tuple(0 for _ in shape))
    return pl.pallas_call(
        functools.partial(_spectra_kernel, L=L, kc=kc),
        grid=(nj,),
        in_specs=[full((L, 128)), full((128, HY_HID)), full((1, HY_HID)), full((HY_HID, HY_HID)),
                  full((1, HY_HID)), full((1, HY_HID)), full((HY_HID, 4 * HY_W)), full((L, HY_W)),
                  pl.BlockSpec((kc, L), lambda j: (j, 0)),
                  pl.BlockSpec((kc, L), lambda j: (j + nj, 0))],
        out_specs=[pl.BlockSpec((kc, 2 * HY_W), lambda j: (j, 0)),
                   pl.BlockSpec((kc, 2 * HY_W), lambda j: (j, 0))],
        out_shape=[jax.ShapeDtypeStruct((L, 2 * HY_W), F32)] * 2,
        scratch_shapes=[pltpu.VMEM((L, 4 * HY_W), BF16)],
        compiler_params=_cparams(("arbitrary",)),
    )(jnp.asarray(z), w1p, b1, w2, b2, fr, w3, jnp.asarray(decay), f, f)


def _hyconv_kernel(p_hbm, cw_ref, cb_ref, d_ref, fa_ref, fb_ref, ga_ref, gb_ref, sa_ref, sb_ref, o_ref,
                   xin_scr, x_scr, g1_scr, g2_scr, y_scr, stage, sem, *, L, nb, kc, nj, row_blk0):
    bg = pl.program_id(0)
    o = pl.program_id(1)
    j = pl.program_id(2)

    @pl.when((o == 0) & (j == 0))
    def _():
        row = lax.broadcasted_iota(jnp.int32, (L, HY_W), 0)
        for i in range(nb):
            cols = slice(HY_W * i, HY_W * (i + 1))
            row0 = pl.multiple_of((row_blk0 + bg * nb + i) * L, L)
            for part, dst in enumerate((x_scr, g1_scr, g2_scr)):
                pc = slice(HY_W * part, HY_W * (part + 1))
                cp = pltpu.make_async_copy(
                    p_hbm.at[pl.ds(row0, L), C_HY + HY_W * part:C_HY + HY_W * (part + 1)], stage, sem)
                cp.start()
                cp.wait()
                u = stage[...]
                up = jnp.where(row == 0, 0.0, pltpu.roll(u, 1, axis=0))
                un = jnp.where(row == L - 1, 0.0, pltpu.roll(u, L - 1, axis=0))
                dst[:, cols] = (up * cw_ref[0:1, pc] + u * cw_ref[1:2, pc] + un * cw_ref[2:3, pc]
                                + cb_ref[:, pc])
            xin_scr[:, cols] = x_scr[:, cols].astype(BF16)

    @pl.when(j == 0)
    def _():
        y_scr[...] = jnp.zeros_like(y_scr)

    xa = _dot(fa_ref[...], xin_scr[...])
    xb = _dot(fb_ref[...], xin_scr[...])
    sa = jnp.concatenate([sa_ref[...]] * nb, axis=1)
    sb = jnp.concatenate([sb_ref[...]] * nb, axis=1)
    first = (lax.broadcasted_iota(jnp.int32, (kc, nb * HY_W), 0) + j * kc) == 0
    ya = jnp.where(first, xa * sa, xa * sa - xb * sb)
    yb = jnp.where(first, xb * sb, xa * sb + xb * sa)
    y_scr[...] += _dot(ga_ref[...], ya.astype(BF16)) + _dot(gb_ref[...], yb.astype(BF16))

    @pl.when(j == nj - 1)
    def _():
        dvec = jnp.concatenate([d_ref[pl.ds(o, 1), :]] * nb, axis=1)
        y = y_scr[...] + x_scr[...] * dvec

        @pl.when(o == 0)
        def _():
            zz = g1_scr[...] * y
            x_scr[...] = zz
            xin_scr[...] = zz.astype(BF16)

        @pl.when(o == 1)
        def _():
            res = g2_scr[...] * y
            for i in range(nb):
                o_ref[L * i:L * (i + 1), :] = res[:, HY_W * i:HY_W * (i + 1)].astype(BF16)


def _hy_conv(p, L, nb, kc, row_blk0, n_seq, conv_w, conv_b, d, sa, sb):
    f, g = (jnp.asarray(a) for a in _dft_consts(L))
    nj = L // kc
    small = lambda shape: pl.BlockSpec(shape, lambda bg, o, j: (0, 0))
    return pl.pallas_call(
        functools.partial(_hyconv_kernel, L=L, nb=nb, kc=kc, nj=nj, row_blk0=row_blk0),
        grid=(n_seq // nb, 2, nj),
        in_specs=[pl.BlockSpec(memory_space=pl.ANY),
            small((3, 3 * HY_W)), small((1, 3 * HY_W)), small((2, HY_W)),
            pl.BlockSpec((kc, L), lambda bg, o, j: (j, 0)),
            pl.BlockSpec((kc, L), lambda bg, o, j: (j + nj, 0)),
            pl.BlockSpec((L, kc), lambda bg, o, j: (0, j)),
            pl.BlockSpec((L, kc), lambda bg, o, j: (0, j + nj)),
            pl.BlockSpec((kc, HY_W), lambda bg, o, j: (j, o)),
            pl.BlockSpec((kc, HY_W), lambda bg, o, j: (j, o))],
        out_specs=pl.BlockSpec((nb * L, HY_W), lambda bg, o, j: (bg, 0)),
        out_shape=jax.ShapeDtypeStruct((n_seq * L, HY_W), BF16),
        scratch_shapes=[pltpu.VMEM((L, nb * HY_W), BF16)] + [pltpu.VMEM((L, nb * HY_W), F32)] * 4
        + [pltpu.VMEM((L, HY_W), F32), pltpu.SemaphoreType.DMA(())],
        compiler_params=_cparams(("arbitrary", "arbitrary", "arbitrary")),
    )(p, conv_w, conv_b, d, f, f, g, g, sa, sb)


def _outproj_kernel(xp_ref, xs_ref, mixp_ref, hyp_ref, nas_ref, ds_ref, hys_ref, g1_ref, w_ref, o_ref):
    i = pl.program_id(0)
    w = lambda a, b: w_ref[a:b, :].astype(BF16)

    @pl.when(i < NP_TOK // TM)
    def _():
        y = _dot(mixp_ref[...], w(0, 768)) + _dot(hyp_ref[...], w(768, D))
        o_ref[...] = xp_ref[...] + g1_ref[...] * y

    @pl.when(i >= NP_TOK // TM)
    def _():
        y = _dot(nas_ref[...], w(0, 256)) + _dot(ds_ref[...], w(256, 768)) + _dot(hys_ref[...], w(768, D))
        o_ref[...] = xs_ref[...] + g1_ref[...] * y


def _x_specs(xs_block0):
    npt = NP_TOK // TM
    return [pl.BlockSpec((TM, D), lambda i, *_: (jnp.minimum(i, npt - 1), 0)),
            pl.BlockSpec((TM, D), lambda i, *_: (jnp.maximum(i - npt, 0) + xs_block0, 0))]


def _outproj(xp, xs, xs_block0, mix_p, hy_p, na_s, d_s, hy_s, mod6, w_out, l):
    npt = NP_TOK // TM
    pidx = lambda i: (jnp.minimum(i, npt - 1), 0)
    sidx = lambda i: (jnp.maximum(i - npt, 0), 0)
    return pl.pallas_call(
        _outproj_kernel,
        grid=(N_TOK // TM,),
        in_specs=_x_specs(xs_block0) + [
                  pl.BlockSpec((TM, 768), pidx),
                  pl.BlockSpec((TM, HY_W), pidx),
                  pl.BlockSpec((TM, 256), sidx),
                  pl.BlockSpec((TM, 512), sidx),
                  pl.BlockSpec((TM, HY_W), sidx),
                  _mod_spec(2, TM),
                  pl.BlockSpec((None, D, D), lambda i: (l, 0, 0))],
        out_specs=pl.BlockSpec((TM, D), lambda i: (i, 0)),
        out_shape=jax.ShapeDtypeStruct((N_TOK, D), F32),
        compiler_params=_cparams(("arbitrary",)),
    )(xp, xs, mix_p, hy_p, na_s, d_s, hy_s, mod6, w_out)


def _ffn_kernel(x_ref, g_ref, sc_ref, sh_ref, gate_ref, wg_ref, wu_ref, wd_ref, o_ref, h_scr, *, nj):
    j = pl.program_id(1)

    @pl.when(j == 0)
    def _():
        h_scr[...] = _norm_mod(x_ref[...], g_ref[...], sc_ref[...], sh_ref[...]).astype(BF16)
        o_ref[...] = jnp.zeros_like(o_ref)

    h = h_scr[...]
    a = _silu(_dot(h, wg_ref[...].astype(BF16))) * _dot(h, wu_ref[...].astype(BF16))
    o_ref[...] += _dot(a.astype(BF16), wd_ref[...].astype(BF16))

    @pl.when(j == nj - 1)
    def _():
        o_ref[...] = x_ref[...] + gate_ref[...] * o_ref[...]


def _dense_ffn(x, g, mod6, wg, wu, wd, i_ffn):
    tf = 256
    tm = DEC_SEQ
    nj = FFN // tf
    return pl.pallas_call(
        functools.partial(_ffn_kernel, nj=nj),
        grid=(N_TOK // tm, nj),
        in_specs=[pl.BlockSpec((tm, D), lambda i, j: (i, 0)),
                  pl.BlockSpec((1, D), lambda i, j: (0, 0)),
                  _mod_spec(4, tm), _mod_spec(3, tm), _mod_spec(5, tm),
                  pl.BlockSpec((None, D, tf), lambda i, j: (i_ffn, 0, j)),
                  pl.BlockSpec((None, D, tf), lambda i, j: (i_ffn, 0, j)),
                  pl.BlockSpec((None, tf, D), lambda i, j: (i_ffn, j, 0))],
        out_specs=pl.BlockSpec((tm, D), lambda i, j: (i, 0)),
        out_shape=jax.ShapeDtypeStruct((N_TOK, D), F32),
        scratch_shapes=[pltpu.VMEM((tm, D), BF16)],
        compiler_params=_cparams(("arbitrary", "arbitrary")),
    )(x, g, mod6, mod6, mod6, wg, wu, wd)


def _router_kernel(x_ref, g_ref, sc_ref, sh_ref, wr_ref, h_ref, r_ref):
    h = _norm_mod(x_ref[...], g_ref[...], sc_ref[...], sh_ref[...])
    h_ref[...] = h.astype(BF16)
    lane = lax.broadcasted_iota(jnp.int32, (TM, 128), 1)
    lg = jnp.where(lane < N_EXP, _dot3(h, wr_ref[...]), -jnp.inf)
    m1 = jnp.max(lg, axis=-1, keepdims=True)
    i1 = jnp.min(jnp.where(lg == m1, lane, 128), axis=-1, keepdims=True)
    lg2 = jnp.where(lane == i1, -jnp.inf, lg)
    m2 = jnp.max(lg2, axis=-1, keepdims=True)
    i2 = jnp.min(jnp.where(lg2 == m2, lane, 128), axis=-1, keepdims=True)
    e = jnp.exp(m2 - m1)
    w1 = 1.0 / (1.0 + e)
    w2 = e / (1.0 + e)
    r_ref[...] = jnp.where(lane == 0, i1.astype(F32),
                           jnp.where(lane == 1, i2.astype(F32),
                                     jnp.where(lane == 2, w1, jnp.where(lane == 3, w2, 0.0))))


def _router(x, g, mod6, wr_pad):
    return pl.pallas_call(
        _router_kernel,
        grid=(N_TOK // TM,),
        in_specs=[pl.BlockSpec((TM, D), lambda i: (i, 0)),
                  pl.BlockSpec((1, D), lambda i: (0, 0)),
                  _mod_spec(4, TM), _mod_spec(3, TM),
                  pl.BlockSpec((D, 128), lambda i: (0, 0))],
        out_specs=[pl.BlockSpec((TM, D), lambda i: (i, 0)),
                   pl.BlockSpec((TM, 128), lambda i: (i, 0))],
        out_shape=[jax.ShapeDtypeStruct((N_TOK, D), BF16), jax.ShapeDtypeStruct((N_TOK, 128), F32)],
        compiler_params=_cparams(("arbitrary",)),
    )(x, g, mod6, mod6, wr_pad)


def _dispatch_kernel(blo_ref, bhi_ref, sexp_ref, h_hbm, rp_ref, o_ref, h_scr, sem):
    s = pl.program_id(0)

    @pl.when(s == 0)
    def _():
        cp = pltpu.make_async_copy(h_hbm, h_scr, sem)
        cp.start()
        cp.wait()

    e = sexp_ref[s]
    rows = s * MOE_TS + lax.broadcasted_iota(jnp.int32, (MOE_TS, MOE_DTB), 0)

    def sel(b):
        return (rows == rp_ref[e, pl.ds(b, 1), :]).astype(BF16)

    is_empty = blo_ref[s] > bhi_ref[s]

    @pl.when(is_empty)
    def _():
        o_ref[...] = jnp.zeros_like(o_ref)

    @pl.when(jnp.logical_not(is_empty))
    def _():
        b0 = jnp.minimum(blo_ref[s], N_TOK // MOE_DTB - MOE_DWIN)
        hwin = h_scr[pl.ds(pl.multiple_of(b0 * MOE_DTB, MOE_DTB), MOE_DWIN * MOE_DTB), :]
        acc = _dot(jnp.concatenate([sel(b0 + k) for k in range(MOE_DWIN)], axis=1), hwin)

        def body(b, acc):
            hb = h_scr[pl.ds(pl.multiple_of(b * MOE_DTB, MOE_DTB), MOE_DTB), :]
            return acc + _dot(sel(b), hb)

        acc = lax.fori_loop(b0 + MOE_DWIN, bhi_ref[s] + 1, body, acc)
        o_ref[...] = acc.astype(BF16)


def _dispatch(blo, bhi, sexp, h, rp_t):
    return pl.pallas_call(
        _dispatch_kernel,
        grid_spec=pltpu.PrefetchScalarGridSpec(
            num_scalar_prefetch=3,
            grid=(MOE_ROWS // MOE_TS,),
            in_specs=[pl.BlockSpec(memory_space=pl.ANY),
                      pl.BlockSpec((N_EXP, N_TOK // MOE_DTB, MOE_DTB), lambda s, *_: (0, 0, 0))],
            out_specs=pl.BlockSpec((MOE_TS, D), lambda s, *_: (s, 0)),
            scratch_shapes=[pltpu.VMEM((N_TOK, D), BF16), pltpu.SemaphoreType.DMA(())]),
        out_shape=jax.ShapeDtypeStruct((MOE_ROWS, D), BF16),
        compiler_params=_cparams(("arbitrary",)),
    )(blo, bhi, sexp, h, rp_t)


def _experts_kernel(te_ref, used_ref, rows_ref, xs_ref, wg_ref, wu_ref, wd_ref, o_ref, acc_scr, *, nj):
    i = pl.program_id(0)
    j = pl.program_id(1)
    nrows = rows_ref[i]

    @pl.when(j == 0)
    def _():
        acc_scr[...] = jnp.zeros_like(acc_scr)

    for sz in range(MOE_TQ, MOE_TM + 1, MOE_TQ):
        @pl.when(nrows == sz)
        def _():
            xb = xs_ref[0:sz, :]
            a = _silu(_dot(xb, wg_ref[...].astype(BF16))) * _dot(xb, wu_ref[...].astype(BF16))
            acc_scr[0:sz, :] += _dot(a.astype(BF16), wd_ref[...].astype(BF16))

    @pl.when(j == nj - 1)
    def _():
        o_ref[...] = acc_scr[...].astype(BF16)


def _experts(tile_expert, used, tile_rows, xs, wg, wu, wd, i_moe):
    nj = EXP_DIM // MOE_TF

    def tile(i, used):
        return jnp.minimum(i, used[0] - 1)

    def chunk(i, j, used):
        return jnp.where(i < used[0], j, nj - 1)

    return pl.pallas_call(
        functools.partial(_experts_kernel, nj=nj),
        grid_spec=pltpu.PrefetchScalarGridSpec(
            num_scalar_prefetch=3,
            grid=(MOE_YROWS // MOE_TM, nj),
            in_specs=[pl.BlockSpec((MOE_TM, D), lambda i, j, te, used, tr: (tile(i, used), 0)),
                      pl.BlockSpec((None, None, D, MOE_TF),
                                   lambda i, j, te, used, tr: (i_moe, te[tile(i, used)], 0, chunk(i, j, used))),
                      pl.BlockSpec((None, None, D, MOE_TF),
                                   lambda i, j, te, used, tr: (i_moe, te[tile(i, used)], 0, chunk(i, j, used))),
                      pl.BlockSpec((None, None, MOE_TF, D),
                                   lambda i, j, te, used, tr: (i_moe, te[tile(i, used)], chunk(i, j, used), 0))],
            out_specs=pl.BlockSpec((MOE_TM, D), lambda i, j, te, used, tr: (i, 0)),
            scratch_shapes=[pltpu.VMEM((MOE_TM, D), F32)]),
        out_shape=jax.ShapeDtypeStruct((MOE_YROWS, D), BF16),
        compiler_params=_cparams(("arbitrary", "arbitrary")),
    )(tile_expert, used, tile_rows, xs, wg, wu, wd)


def _combine_kernel(ws_ref, kind_ref, x_ref, rp_ref, comb_ref, gate_ref, fg_ref, ys_hbm, op_ref, os_ref,
                    win_scr, y_scr, sem):
    b = pl.program_id(0)
    nb = pl.num_programs(0)
    slot = b % 2

    def win_copy(blk, sl, e):
        start = pl.multiple_of(ws_ref[blk * N_EXP + e], MOE_ALIGN)
        return pltpu.make_async_copy(ys_hbm.at[pl.ds(start, MOE_WIN)], win_scr.at[sl, e], sem.at[sl, e])

    @pl.when(b == 0)
    def _():
        for e in range(N_EXP):
            win_copy(0, 0, e).start()

    @pl.when(b + 1 < nb)
    def _():
        for e in range(N_EXP):
            win_copy(b + 1, 1 - slot, e).start()

    col = lax.broadcasted_iota(jnp.int32, (MOE_TB, MOE_WIN), 1)
    y_scr[...] = jnp.zeros_like(y_scr)
    for e in range(N_EXP):
        win_copy(b, slot, e).wait()
        rel = rp_ref[:, e:e + 1] - ws_ref[b * N_EXP + e]
        cw = comb_ref[:, e:e + 1]
        kind = kind_ref[b * N_EXP + e]

        for k, wn in enumerate(MOE_WINS):
            @pl.when(kind == k)
            def _():
                sel = (rel == col[:, :wn]).astype(BF16) if wn == MOE_WIN else (
                    rel == lax.broadcasted_iota(jnp.int32, (MOE_TB, wn), 1)).astype(BF16)
                y_scr[...] += cw * _dot(sel, win_scr[slot, e, 0:wn, :])

    x = x_ref[...] + gate_ref[...] * y_scr[...]
    x = (x * lax.rsqrt(jnp.mean(x * x, axis=-1, keepdims=True) + EPS)) * fg_ref[...]

    @pl.when(b < NP_TOK // MOE_TB)
    def _():
        op_ref[...] = x

    @pl.when(b >= NP_TOK // MOE_TB)
    def _():
        os_ref[...] = x


def _combine(ws, kind, x, rp8, comb, mod6, final_g, ys):
    npb = NP_TOK // MOE_TB
    return pl.pallas_call(
        _combine_kernel,
        grid_spec=pltpu.PrefetchScalarGridSpec(
            num_scalar_prefetch=2,
            grid=(N_TOK // MOE_TB,),
            in_specs=[pl.BlockSpec((MOE_TB, D), lambda b, *_: (b, 0)),
                      pl.BlockSpec((MOE_TB, N_EXP), lambda b, *_: (b, 0)),
                      pl.BlockSpec((MOE_TB, N_EXP), lambda b, *_: (b, 0)),
                      _mod_spec(5, MOE_TB),
                      pl.BlockSpec((1, D), lambda b, *_: (0, 0)),
                      pl.BlockSpec(memory_space=pl.ANY)],
            out_specs=[pl.BlockSpec((MOE_TB, D), lambda b, *_: (jnp.minimum(b, npb - 1), 0)),
                       pl.BlockSpec((MOE_TB, D), lambda b, *_: (jnp.maximum(b - npb, 0), 0))],
            scratch_shapes=[pltpu.VMEM((2, N_EXP, MOE_WIN, D), BF16),
                            pltpu.VMEM((MOE_TB, D), F32),
                            pltpu.SemaphoreType.DMA((2, N_EXP))]),
        out_shape=[jax.ShapeDtypeStruct((NP_TOK, D), F32), jax.ShapeDtypeStruct((NS_TOK, D), F32)],
        compiler_params=_cparams(("arbitrary",)),
    )(ws, kind, x, rp8, comb, mod6, final_g, ys)


def _moe(x, g, mod6, router, wg, wu, wd, i_moe, final_g):
    wr_pad = jnp.pad(router, ((0, 0), (0, 128 - N_EXP)))
    h, r = _router(x, g, mod6, wr_pad)

    i32 = jnp.int32
    i12 = r[:, 0:2].astype(i32)
    earange = jnp.arange(N_EXP, dtype=i32)
    hit1 = i12[:, 0:1] == earange[None, :]
    hit2 = i12[:, 1:2] == earange[None, :]
    comb = jnp.where(hit1, r[:, 2:3], 0.0) + jnp.where(hit2, r[:, 3:4], 0.0)
    mask = (hit1 | hit2).astype(i32)
    csum = jnp.cumsum(mask, axis=0)
    counts = csum[-1]
    padded = ((counts + MOE_TM - 1) // MOE_TM) * MOE_TM
    ends = jnp.cumsum(padded)
    starts = ends - padded
    rp8 = jnp.where(mask > 0, starts[None, :] + csum - 1, -1).astype(i32)
    n_tiles = MOE_ROWS // MOE_TM
    tile_expert = jnp.minimum(
        jnp.searchsorted(ends, jnp.arange(n_tiles, dtype=i32) * MOE_TM, side="right"),
        N_EXP - 1).astype(i32)
    used = (ends[-1:] // MOE_TM).astype(i32)
    tile_row0 = jnp.arange(MOE_YROWS // MOE_TM, dtype=i32) * MOE_TM
    te_all = jnp.minimum(jnp.searchsorted(ends, tile_row0, side="right"), N_EXP - 1)
    valid = jnp.clip(counts[te_all] - (tile_row0 - starts[te_all]), 0, MOE_TM)
    valid = jnp.where(tile_row0 < ends[-1], valid, 0)
    tile_rows = (((valid + MOE_TQ - 1) // MOE_TQ) * MOE_TQ).astype(i32)

    sub_row0 = jnp.arange(MOE_ROWS // MOE_TS, dtype=i32) * MOE_TS
    sexp = tile_expert[sub_row0 // MOE_TM]
    qlo = sub_row0 - starts[sexp]
    qend = jnp.minimum(qlo + MOE_TS, counts[sexp])
    cbe = csum[MOE_DTB - 1::MOE_DTB, :].T[sexp]
    blo = jnp.sum((cbe <= qlo[:, None]).astype(i32), axis=1)
    bhi = jnp.minimum(jnp.sum((cbe < qend[:, None]).astype(i32), axis=1), N_TOK // MOE_DTB - 1)
    empty = qend <= qlo
    blo = jnp.where(empty, 1, blo).astype(i32)
    bhi = jnp.where(empty, 0, bhi).astype(i32)

    cb = csum[MOE_TB - 1::MOE_TB, :]
    cprev = jnp.concatenate([jnp.zeros((1, N_EXP), i32), cb[:-1]], axis=0)
    ws = (((starts[None, :] + cprev) // MOE_ALIGN) * MOE_ALIGN).reshape(-1).astype(i32)
    n_be = cb - cprev
    kind = sum((n_be > wn - MOE_ALIGN).astype(i32) for wn in MOE_WINS[:-1])
    kind = jnp.where(n_be == 0, len(MOE_WINS), kind).reshape(-1).astype(i32)

    rp_t = rp8.T.reshape(N_EXP, N_TOK // MOE_DTB, MOE_DTB)
    xs = _dispatch(blo, bhi, sexp, h, rp_t)
    ys = _experts(tile_expert, used, tile_rows, xs, wg, wu, wd, i_moe)
    return _combine(ws, kind, x, rp8, comb, mod6, final_g, ys)


assert DEPTH == 2

def kernel(x_prompt, x_sample, cache_na_k, cache_na_v, cache_diff_k, cache_diff_v, c, c_ctx, w_in, w_out, ada_w, ada_b, norm_mix_g, norm_ffn_g, na_rpb, diff_lq1, diff_lk1, diff_lq2, diff_lk2, diff_subln_g, hy_conv_w, hy_conv_b, hy_d, hy_f_w1, hy_f_b1, hy_f_w2, hy_f_b2, hy_f_freq, hy_f_w3, ffn_w_gate, ffn_w_up, ffn_w_down, moe_router, moe_w_gate, moe_w_up, moe_w_down, final_norm_g):
    xparts = (x_prompt.reshape(NP_TOK, D), x_sample.reshape(NS_TOK, D), 0)
    cond8 = jnp.concatenate([c_ctx[None, :], c, jnp.zeros((5, D), F32)], axis=0)
    mods = _modulation(cond8, ada_w, ada_b)
    final_g = final_norm_g.reshape(1, D)
    bias = _na_bias(na_rpb)

    leaves = [jnp.zeros((BATCH, DEPTH, NA_HEADS, SEQ, HEAD_DIM), F32),
              jnp.zeros((BATCH, DEPTH, NA_HEADS, SEQ, HEAD_DIM), F32),
              jnp.zeros((BATCH, DEPTH, DIFF_HEADS, 2, SEQ, HEAD_DIM), F32),
              jnp.zeros((BATCH, DEPTH, DIFF_HEADS, SEQ, DIFF_V), F32)]
    for l in range(DEPTH):
        lam_init = 0.8 - 0.6 * math.exp(-0.3 * l)
        mod6 = mods[l].reshape(8, 6, D).transpose(1, 0, 2).reshape(6, 8, 1, D)
        row = lambda a: a[l].reshape(1, -1)
        lq1, lk1, lq2, lk2, subg = row(diff_lq1), row(diff_lk1), row(diff_lq2), row(diff_lk2), row(diff_subln_g)

        p, *leaves = _inproj(*xparts, row(norm_mix_g), mod6, w_in, l, leaves)

        mix_p = _prompt_attn(p, lq1, lk1, lq2, lk2, subg, lam_init)
        na_s = _na_attn(p, cache_na_k, cache_na_v, bias, l)
        d_s = _sdiff_attn(p, cache_diff_k, cache_diff_v, lq1, lk1, lq2, lk2, subg, lam_init, l)

        w1p = jnp.pad(hy_f_w1[l], ((0, 128 - HY_EMB), (0, 0)))
        fargs = (w1p, row(hy_f_b1), hy_f_w2[l], row(hy_f_b2), row(hy_f_freq), hy_f_w3[l])
        cargs = (hy_conv_w[l], row(hy_conv_b), hy_d[l])
        sa_p, sb_p = _hy_spectra(SEQ, SEQ, *fargs)
        hy_p = _hy_conv(p, SEQ, 4, SEQ, 0, BATCH, *cargs, sa_p, sb_p)
        sa_s, sb_s = _hy_spectra(DEC_SEQ, 512, *fargs)
        hy_s = _hy_conv(p, DEC_SEQ, 2, 512, NP_TOK // DEC_SEQ, DEC_BATCH, *cargs, sa_s, sb_s)

        x = _outproj(*xparts, mix_p, hy_p, na_s, d_s, hy_s, mod6, w_out, l)

        if l == 0:
            x = _dense_ffn(x, row(norm_ffn_g), mod6, ffn_w_gate, ffn_w_up, ffn_w_down, 0)
            xparts = (x, x, NP_TOK // TM)
        else:
            yp, ys = _moe(x, row(norm_ffn_g), mod6, moe_router[0], moe_w_gate, moe_w_up, moe_w_down,
                          0, final_g)

    return (yp.reshape(BATCH, SEQ, D), ys.reshape(DEC_BATCH, DEC_SEQ, D), *leaves)
```

```python
import functools
import math

import numpy as np
import jax
import jax.numpy as jnp
from jax import lax
from jax.experimental import pallas as pl
from jax.experimental.pallas import tpu as pltpu

F32 = jnp.float32
BF16 = jnp.bfloat16

D = 1024
BATCH, SEQ = 32, 256
DEC_BATCH, DEC_SEQ = 2, 2048
DEPTH = 2
PAST = 512
GRID_W = 64
GRID_ROWS = DEC_SEQ // GRID_W
HEAD_DIM = 64
NA_HEADS = 4
DIFF_HEADS = 4
DIFF_V = 128
WIN_ROWS, WIN_COLS = 8, 16
HY_W = 256
HY_EMB = 33
HY_BANDS = 16
HY_HID = 64
PROJ = 3072
FFN = 2816
N_EXP = 8
EXP_DIM = 3584
EPS = 1e-6
ROPE_BASE = 10000.0

NP_TOK = BATCH * SEQ
NS_TOK = DEC_BATCH * DEC_SEQ
N_TOK = NP_TOK + NS_TOK

C_NAQ, C_NAK, C_NAV = 0, 256, 512
C_DQ, C_DK, C_DV = 768, 1280, 1792
C_HY = 2304

TM = 1024
VMEM_LIMIT = 56 * 1024 * 1024

MOE_TM = 1024
MOE_TF = 512
MOE_TQ = 256
MOE_ROWS = 2 * N_TOK + N_EXP * MOE_TM
MOE_TS = 256
MOE_DTB = 512
MOE_DWIN = 3
MOE_TB = 512
MOE_ALIGN = 16
MOE_WIN = MOE_TB + MOE_ALIGN
MOE_WINS = (MOE_TB // 4 + MOE_ALIGN, MOE_TB // 2 + MOE_ALIGN, MOE_WIN)
MOE_YROWS = MOE_ROWS + MOE_TM


def _cparams(sem):
    return pltpu.CompilerParams(dimension_semantics=sem, vmem_limit_bytes=VMEM_LIMIT)


def _dot(a, b):
    return jnp.dot(a, b, preferred_element_type=F32)


def _dot_nt(a, b):
    return lax.dot_general(a, b, (((1,), (1,)), ((), ())), preferred_element_type=F32)


def _split(a):
    hi = a.astype(BF16)
    lo = (a - hi.astype(F32)).astype(BF16)
    return hi, lo


def _dot3(a, b):
    ah, al = _split(a)
    bh, bl = _split(b)
    return _dot(ah, bh) + (_dot(ah, bl) + _dot(al, bh))


def _silu(x):
    return x / (1.0 + jnp.exp(-x))


def _mod_row(i, tm):
    t = i * tm
    return jnp.where(t < NP_TOK, 0, 1 + (t - NP_TOK) // DEC_SEQ)


def _mod_spec(k, tm):
    return pl.BlockSpec((None, None, 1, D), lambda i, *_: (k, _mod_row(i, tm), 0, 0))


def _norm_mod(x, g, sc, sh):
    y = x * lax.rsqrt(jnp.mean(x * x, axis=-1, keepdims=True) + EPS)
    return (y * g) * (1.0 + sc) + sh


def _mod_kernel(c_ref, w_ref, b_ref, o_ref):
    o_ref[...] = _dot3(_silu(c_ref[...]), w_ref[...]) + b_ref[...]


def _modulation(cond8, ada_w, ada_b):
    tn = 1536
    return pl.pallas_call(
        _mod_kernel,
        grid=(DEPTH, 6 * D // tn),
        in_specs=[pl.BlockSpec((8, D), lambda l, j: (0, 0)),
                  pl.BlockSpec((None, D, tn), lambda l, j: (l, 0, j)),
                  pl.BlockSpec((None, 1, tn), lambda l, j: (l, 0, j))],
        out_specs=pl.BlockSpec((None, 8, tn), lambda l, j: (l, 0, j)),
        out_shape=jax.ShapeDtypeStruct((DEPTH, 8, 6 * D), F32),
        compiler_params=_cparams(("arbitrary", "arbitrary")),
    )(cond8, ada_w, ada_b.reshape(DEPTH, 1, 6 * D))


def _inproj_kernel(xp_ref, xs_ref, g_ref, sc_ref, sh_ref, w_hbm, nak_in, nav_in, dk_in, dv_in,
                   o_ref, nak_ref, nav_ref, dk_ref, dv_ref, h_scr, w_scr, stage, sem, *, l, tn):
    del nak_in, nav_in, dk_in, dv_in
    i = pl.program_id(0)
    j = pl.program_id(1)

    @pl.when((i == 0) & (j == 0))
    def _():
        for c in range(PROJ // tn):
            cp = pltpu.make_async_copy(w_hbm.at[l, :, c * tn:(c + 1) * tn], stage, sem)
            cp.start()
            cp.wait()
            w_scr[c] = stage[...].astype(BF16)

    is_ctx = i < NP_TOK // TM

    @pl.when((j == 0) & is_ctx)
    def _():
        h_scr[...] = _norm_mod(xp_ref[...], g_ref[...], sc_ref[...], sh_ref[...]).astype(BF16)

    @pl.when((j == 0) & jnp.logical_not(is_ctx))
    def _():
        h_scr[...] = _norm_mod(xs_ref[...], g_ref[...], sc_ref[...], sh_ref[...]).astype(BF16)

    o_ref[...] = _dot(h_scr[...], w_scr[j])


    def rows(bb):
        return slice(bb * SEQ, (bb + 1) * SEQ)

    def copy_heads(dst_ref, c0, width, heads, sub=None):
        for bb in range(TM // SEQ):
            for h in heads:
                if sub is None:
                    dst_ref[bb, h] = o_ref[rows(bb), c0(h):c0(h) + width]
                else:
                    for s in range(2):
                        dst_ref[bb, h, s] = o_ref[rows(bb), c0(h) + s * width:c0(h) + (s + 1) * width]

    @pl.when(is_ctx & (j == 0))
    def _():
        copy_heads(nak_ref, lambda h: C_NAK + HEAD_DIM * h, HEAD_DIM, range(NA_HEADS))
        copy_heads(nav_ref, lambda h: C_NAV + HEAD_DIM * h, HEAD_DIM, range(NA_HEADS))

    @pl.when(is_ctx & (j == 1))
    def _():
        copy_heads(dk_ref, lambda h: C_DK - tn + 2 * HEAD_DIM * h, HEAD_DIM, range(0, 2), sub=True)

    @pl.when(is_ctx & (j == 2))
    def _():
        copy_heads(dk_ref, lambda h: C_DK - 2 * tn + 2 * HEAD_DIM * h, HEAD_DIM, range(2, 4), sub=True)
        copy_heads(dv_ref, lambda h: C_DV - 2 * tn + DIFF_V * h, DIFF_V, range(DIFF_HEADS))


def _inproj(xp, xs, xs_block0, g, mod6, w_in, l, leaves):
    tn = 768
    nb = TM // SEQ
    ctx = lambda i: jnp.minimum(i, NP_TOK // TM - 1)
    leaf_specs = [pl.BlockSpec((nb, None, NA_HEADS, SEQ, HEAD_DIM), lambda i, j: (ctx(i), l, 0, 0, 0)),
                  pl.BlockSpec((nb, None, NA_HEADS, SEQ, HEAD_DIM), lambda i, j: (ctx(i), l, 0, 0, 0)),
                  pl.BlockSpec((nb, None, DIFF_HEADS, 2, SEQ, HEAD_DIM), lambda i, j: (ctx(i), l, 0, 0, 0, 0)),
                  pl.BlockSpec((nb, None, DIFF_HEADS, SEQ, DIFF_V), lambda i, j: (ctx(i), l, 0, 0, 0))]
    return pl.pallas_call(
        functools.partial(_inproj_kernel, l=l, tn=tn),
        grid=(N_TOK // TM, PROJ // tn),
        in_specs=_x_specs(xs_block0) + [
                  pl.BlockSpec((1, D), lambda i, j: (0, 0)),
                  _mod_spec(1, TM), _mod_spec(0, TM),
                  pl.BlockSpec(memory_space=pl.ANY)] + [pl.BlockSpec(memory_space=pl.ANY)] * 4,
        out_specs=[pl.BlockSpec((TM, tn), lambda i, j: (i, j))] + leaf_specs,
        out_shape=[jax.ShapeDtypeStruct((N_TOK, PROJ), F32)]
        + [jax.ShapeDtypeStruct(a.shape, a.dtype) for a in leaves],
        input_output_aliases={6: 1, 7: 2, 8: 3, 9: 4},
        scratch_shapes=[pltpu.VMEM((TM, D), BF16), pltpu.VMEM((PROJ // tn, D, tn), BF16),
                        pltpu.VMEM((D, tn), F32), pltpu.SemaphoreType.DMA(())],
        compiler_params=_cparams(("arbitrary", "arbitrary")),
    )(xp, xs, g, mod6, mod6, w_in, *leaves)


def _lam(lq1, lk1, lq2, lk2, lam_init):
    return (jnp.exp(jnp.sum(lq1[...] * lk1[...], axis=-1, keepdims=True))
            - jnp.exp(jnp.sum(lq2[...] * lk2[...], axis=-1, keepdims=True)) + lam_init)


def _softmax_parts(s):
    m = jnp.max(s, axis=-1, keepdims=True)
    e = jnp.exp(s - m)
    return e, jnp.sum(e, axis=-1, keepdims=True)


QK_SCALE = HEAD_DIM ** -0.5


def _diff_head(q1, q2, k1, k2, v, lam, g, lam_init):
    e1, l1 = _softmax_parts(_dot_nt(q1, k1))
    e2, l2 = _softmax_parts(_dot_nt(q2, k2))
    a = e1 - (lam * l1 * (1.0 / l2)) * e2
    o = _dot(a.astype(BF16), v) * (1.0 / l1)
    o = o * lax.rsqrt(jnp.mean(o * o, axis=-1, keepdims=True) + EPS)
    return (o * g) * (1.0 - lam_init)


def _prompt_attn_kernel(pa_ref, pb_ref, pc_ref, lq1, lk1, lq2, lk2, g_ref, o_ref, *, lam_init):
    lam = _lam(lq1, lk1, lq2, lk2, lam_init)
    g = g_ref[...]

    def col(c0, w, scale=None):
        ref = (pa_ref, pb_ref, pc_ref)[c0 // 768]
        o = c0 % 768
        a = ref[:, o:o + w]
        return (a if scale is None else a * scale).astype(BF16)

    for h in range(NA_HEADS):
        q = col(C_NAQ + 64 * h, 64, QK_SCALE)
        k = col(C_NAK + 64 * h, 64)
        v = col(C_NAV + 64 * h, 64)
        e, l = _softmax_parts(_dot_nt(q, k))
        o = _dot(e.astype(BF16), v) * (1.0 / l)
        o_ref[:, 64 * h:64 * h + 64] = o.astype(BF16)
    for h in range(DIFF_HEADS):
        q1 = col(C_DQ + 128 * h, 64, QK_SCALE)
        q2 = col(C_DQ + 128 * h + 64, 64, QK_SCALE)
        k1 = col(C_DK + 128 * h, 64)
        k2 = col(C_DK + 128 * h + 64, 64)
        v = col(C_DV + 128 * h, 128)
        o = _diff_head(q1, q2, k1, k2, v, lam, g, lam_init)
        o_ref[:, 256 + 128 * h:384 + 128 * h] = o.astype(BF16)


def _prompt_attn(p, lq1, lk1, lq2, lk2, g, lam_init):
    vec = lambda n: pl.BlockSpec((1, n), lambda b: (0, 0))
    return pl.pallas_call(
        functools.partial(_prompt_attn_kernel, lam_init=lam_init),
        grid=(BATCH,),
        in_specs=[pl.BlockSpec((SEQ, 768), lambda b: (b, 0)),
                  pl.BlockSpec((SEQ, 768), lambda b: (b, 1)),
                  pl.BlockSpec((SEQ, 768), lambda b: (b, 2)),
                  vec(64), vec(64), vec(64), vec(64), vec(128)],
        out_specs=pl.BlockSpec((SEQ, 768), lambda b: (b, 0)),
        out_shape=jax.ShapeDtypeStruct((NP_TOK, 768), BF16),
        compiler_params=_cparams(("arbitrary",)),
    )(p, p, p, lq1, lk1, lq2, lk2, g)


def _bias_kernel(rpb_ref, o_ref):
    lh = pl.program_id(0)
    qc = lax.broadcasted_iota(jnp.int32, (GRID_W, GRID_W), 0)
    kc = lax.broadcasted_iota(jnp.int32, (GRID_W, GRID_W), 1)
    delta = jnp.clip(kc - qc + (WIN_COLS - 1), 0, 2 * WIN_COLS - 2)
    qs = jnp.clip(qc - WIN_COLS // 2, 0, GRID_W - WIN_COLS)
    in_win = (kc >= qs) & (kc < qs + WIN_COLS)
    for dr in range(2 * WIN_ROWS - 1):
        base = (lh * (2 * WIN_ROWS - 1) + dr) * (2 * WIN_COLS - 1)
        acc = jnp.zeros((GRID_W, GRID_W), F32)
        for d in range(2 * WIN_COLS - 1):
            acc = jnp.where(delta == d, rpb_ref[base + d], acc)
        piece = jnp.where(in_win, acc, -jnp.inf)
        for case in range(WIN_ROWS):
            i = dr + case - (WIN_ROWS - 1)
            if 0 <= i < WIN_ROWS:
                o_ref[case, :, i * GRID_W:(i + 1) * GRID_W] = piece


def _na_bias(rpb):
    return pl.pallas_call(
        _bias_kernel,
        grid=(DEPTH * NA_HEADS,),
        in_specs=[pl.BlockSpec(memory_space=pltpu.SMEM)],
        out_specs=pl.BlockSpec((None, WIN_ROWS, GRID_W, WIN_ROWS * GRID_W), lambda lh: (lh, 0, 0, 0)),
        out_shape=jax.ShapeDtypeStruct((DEPTH * NA_HEADS, WIN_ROWS, GRID_W, WIN_ROWS * GRID_W), F32),
        compiler_params=_cparams(("arbitrary",)),
    )(rpb.reshape(-1))


def _na_kernel(q_ref, kv_ref, kc_ref, vc_ref, bias_ref, o_ref):
    nloc = WIN_ROWS * GRID_W
    for rr in range(NA_RB):
        r = pl.program_id(1) * NA_RB + rr
        start = jnp.clip(r - WIN_ROWS // 2, 0, GRID_ROWS - WIN_ROWS)
        case = r - start
        row0 = pl.multiple_of(start * GRID_W, GRID_W)
        qrows = slice(rr * GRID_W, (rr + 1) * GRID_W)
        for h in range(NA_HEADS):
            q = (q_ref[qrows, C_NAQ + 64 * h:C_NAQ + 64 * h + 64] * QK_SCALE).astype(BF16)
            k = kv_ref[pl.ds(row0, nloc), C_NAK + 64 * h:C_NAK + 64 * h + 64].astype(BF16)
            v = kv_ref[pl.ds(row0, nloc), C_NAV + 64 * h:C_NAV + 64 * h + 64].astype(BF16)
            s_loc = _dot_nt(q, k) + bias_ref[h, pl.ds(case, 1)][0]
            s_ctx = _dot_nt(q, kc_ref[h].astype(BF16))
            m = jnp.maximum(jnp.max(s_loc, axis=-1, keepdims=True), jnp.max(s_ctx, axis=-1, keepdims=True))
            e_loc = jnp.exp(s_loc - m)
            e_ctx = jnp.exp(s_ctx - m)
            l = jnp.sum(e_loc, axis=-1, keepdims=True) + jnp.sum(e_ctx, axis=-1, keepdims=True)
            o = (_dot(e_loc.astype(BF16), v) + _dot(e_ctx.astype(BF16), vc_ref[h].astype(BF16))) * (1.0 / l)
            o_ref[qrows, 64 * h:64 * h + 64] = o.astype(BF16)


NA_RB = 4


def _na_attn(p, cache_k, cache_v, bias, l):
    qblk0 = NP_TOK // (NA_RB * GRID_W)
    kvblk0 = NP_TOK // DEC_SEQ
    nrg = GRID_ROWS // NA_RB
    return pl.pallas_call(
        _na_kernel,
        grid=(DEC_BATCH, nrg),
        in_specs=[pl.BlockSpec((NA_RB * GRID_W, 768), lambda b, r: (qblk0 + b * nrg + r, 0)),
                  pl.BlockSpec((DEC_SEQ, 768), lambda b, r: (kvblk0 + b, 0)),
                  pl.BlockSpec((None, None, NA_HEADS, PAST, HEAD_DIM), lambda b, r: (b, l, 0, 0, 0)),
                  pl.BlockSpec((None, None, NA_HEADS, PAST, HEAD_DIM), lambda b, r: (b, l, 0, 0, 0)),
                  pl.BlockSpec((NA_HEADS, WIN_ROWS, GRID_W, WIN_ROWS * GRID_W), lambda b, r: (l, 0, 0, 0))],
        out_specs=pl.BlockSpec((NA_RB * GRID_W, 256), lambda b, r: (b * nrg + r, 0)),
        out_shape=jax.ShapeDtypeStruct((NS_TOK, 256), BF16),
        compiler_params=_cparams(("arbitrary", "arbitrary")),
    )(p, p, cache_k, cache_v, bias)


@functools.lru_cache(None)
def _rope_tables():
    t = np.arange(DEC_SEQ)
    lane = np.arange(128)
    dd = lane % HEAD_DIM
    pos = np.where(dd[None, :] < 32, (t // GRID_W)[:, None], (t % GRID_W)[:, None]).astype(np.float64)
    inv = ROPE_BASE ** (-(dd % 16).astype(np.float64) * 2.0 / 32.0)
    ang = pos * inv[None, :]
    first = (dd % 32) < 16
    cos = np.cos(ang)
    s_up = np.where(first[None, :], -np.sin(ang), 0.0)
    s_dn = np.where(first[None, :], 0.0, np.sin(ang))
    return tuple(np.asarray(a, np.float32) for a in (cos, s_up, s_dn))


def _rope(x, cos, s_up, s_dn):
    return x * cos + pltpu.roll(x, 112, axis=1) * s_up + pltpu.roll(x, 16, axis=1) * s_dn


def _sdiff_kernel(q_ref, k_ref, v_ref, ck_ref, cv_ref, cos_ref, sup_ref, sdn_ref,
                  lq1, lk1, lq2, lk2, g_ref, o_ref, k1_scr, k2_scr, v_scr, *, lam_init, tq):
    qb = pl.program_id(2)

    @pl.when(qb == 0)
    def _():
        kr = _rope(k_ref[...], cos_ref[...], sup_ref[...], sdn_ref[...])
        k1_scr[0:DEC_SEQ, :] = kr[:, :64].astype(BF16)
        k2_scr[0:DEC_SEQ, :] = kr[:, 64:].astype(BF16)
        k1_scr[DEC_SEQ:, :] = ck_ref[0].astype(BF16)
        k2_scr[DEC_SEQ:, :] = ck_ref[1].astype(BF16)
        v_scr[0:DEC_SEQ, :] = v_ref[...].astype(BF16)
        v_scr[DEC_SEQ:, :] = cv_ref[...].astype(BF16)

    lam = _lam(lq1, lk1, lq2, lk2, lam_init)
    half = tq // 2
    for c in range(2):
        rows = pl.ds(pl.multiple_of(qb * tq + c * half, half), half)
        qr = _rope(q_ref[c * half:(c + 1) * half, :], cos_ref[rows, :], sup_ref[rows, :],
                   sdn_ref[rows, :]) * QK_SCALE
        o = _diff_head(qr[:, :64].astype(BF16), qr[:, 64:].astype(BF16), k1_scr[...], k2_scr[...],
                       v_scr[...], lam, g_ref[...], lam_init)
        o_ref[c * half:(c + 1) * half, :] = o.astype(BF16)


def _sdiff_attn(p, cache_k, cache_v, lq1, lk1, lq2, lk2, g, lam_init, l):
    tq = 512
    nq = DEC_SEQ // tq
    cos, s_up, s_dn = (jnp.asarray(a) for a in _rope_tables())
    vec = lambda n: pl.BlockSpec((1, n), lambda b, h, q: (0, 0))
    tab = pl.BlockSpec((DEC_SEQ, 128), lambda b, h, q: (0, 0))
    kvblk0 = NP_TOK // DEC_SEQ
    return pl.pallas_call(
        functools.partial(_sdiff_kernel, lam_init=lam_init, tq=tq),
        grid=(DEC_BATCH, DIFF_HEADS, nq),
        in_specs=[pl.BlockSpec((tq, 128), lambda b, h, q: (NP_TOK // tq + b * nq + q, C_DQ // 128 + h)),
                  pl.BlockSpec((DEC_SEQ, 128), lambda b, h, q: (kvblk0 + b, C_DK // 128 + h)),
                  pl.BlockSpec((DEC_SEQ, 128), lambda b, h, q: (kvblk0 + b, C_DV // 128 + h)),
                  pl.BlockSpec((None, None, None, 2, PAST, HEAD_DIM), lambda b, h, q: (b, l, h, 0, 0, 0)),
                  pl.BlockSpec((None, None, None, PAST, DIFF_V), lambda b, h, q: (b, l, h, 0, 0)),
                  tab, tab, tab, vec(64), vec(64), vec(64), vec(64), vec(128)],
        out_specs=pl.BlockSpec((tq, 128), lambda b, h, q: (b * nq + q, h)),
        out_shape=jax.ShapeDtypeStruct((NS_TOK, DIFF_HEADS * DIFF_V), BF16),
        scratch_shapes=[pltpu.VMEM((DEC_SEQ + PAST, HEAD_DIM), BF16),
                        pltpu.VMEM((DEC_SEQ + PAST, HEAD_DIM), BF16),
                        pltpu.VMEM((DEC_SEQ + PAST, DIFF_V), BF16)],
        compiler_params=_cparams(("arbitrary", "arbitrary", "arbitrary")),
    )(p, p, p, cache_k, cache_v, cos, s_up, s_dn, lq1, lk1, lq2, lk2, g)


@functools.lru_cache(None)
def _dft_consts(L):
    n = 2 * L
    k = np.arange(L)
    ang = 2.0 * np.pi * ((k[:, None] * k[None, :]) % n) / n
    alt = (-1.0) ** k
    fa = np.cos(ang)
    fb = -np.sin(ang)
    fb[0, :] = alt
    wgt = np.full((L,), 2.0 / n)
    wgt[0] = 1.0 / n
    ga = fa * wgt[:, None]
    gb = fb * wgt[:, None]
    gb[0, :] = alt / n
    f = np.concatenate([fa, fb], axis=0)
    g = np.concatenate([ga.T, gb.T], axis=1)
    return np.asarray(f, dtype=BF16), np.asarray(g, dtype=BF16)


@functools.lru_cache(None)
def _filter_consts(L):
    f32 = np.float32
    t = np.linspace(0.0, 1.0, L, dtype=f32)[:, None]
    pos = np.arange(L, dtype=f32)[:, None]
    bands = np.linspace(1e-4, HY_BANDS - 1, HY_BANDS, dtype=f32)[None, :]
    ang = f32(2.0 * math.pi) * bands * pos / f32(L)
    z = np.zeros((L, 128), f32)
    z[:, 0:1] = t
    z[:, 1:1 + HY_BANDS] = np.cos(ang)
    z[:, 1 + HY_BANDS:HY_EMB] = -np.sin(ang)
    min_decay = math.log(1e-2) / 1.5
    max_decay = math.log(1e-2) / 0.3
    deltas = np.abs(np.linspace(min_decay, max_decay, HY_W, dtype=f32))
    decay = np.exp(-t * deltas[None, :]).astype(f32)
    return z, decay


def _spectra_kernel(z_ref, w1_ref, b1_ref, w2_ref, b2_ref, fr_ref, w3_ref, dec_ref, fa_ref, fb_ref,
                    sa_ref, sb_ref, filt_scr, *, L, kc):
    j = pl.program_id(0)

    @pl.when(j == 0)
    def _():
        fr = fr_ref[...]
        hdn = jnp.sin(fr * (_dot3(z_ref[...], w1_ref[...]) + b1_ref[...]))
        hdn = jnp.sin(fr * (_dot3(hdn, w2_ref[...]) + b2_ref[...]))
        dec = dec_ref[...]
        not_first = lax.broadcasted_iota(jnp.int32, (L, HY_W), 0) > 0
        for o in range(2):
            hf = _dot3(hdn, w3_ref[:, 512 * o:512 * o + 256]) * dec
            hb = jnp.where(not_first, _dot3(hdn, w3_ref[:, 512 * o + 256:512 * o + 512]) * dec, 0.0)
            nrm = (jnp.sum(jnp.abs(hf), axis=0, keepdims=True)
                   + jnp.sum(jnp.abs(hb), axis=0, keepdims=True))
            filt_scr[:, 512 * o:512 * o + 256] = (hf / nrm).astype(BF16)
            filt_scr[:, 512 * o + 256:512 * o + 512] = (hb / nrm).astype(BF16)

    ta = _dot(fa_ref[...], filt_scr[...])
    tb = _dot(fb_ref[...], filt_scr[...])
    first = (lax.broadcasted_iota(jnp.int32, (kc, HY_W), 0) + j * kc) == 0
    for o in range(2):
        af, ab = ta[:, 512 * o:512 * o + 256], ta[:, 512 * o + 256:512 * o + 512]
        bf, bb = tb[:, 512 * o:512 * o + 256], tb[:, 512 * o + 256:512 * o + 512]
        sa_ref[:, 256 * o:256 * o + 256] = af + ab
        sb_ref[:, 256 * o:256 * o + 256] = jnp.where(first, bf + bb, bf - bb)


def _hy_spectra(L, kc, w1p, b1, w2, b2, fr, w3):
    z, decay = _filter_consts(L)
    f = jnp.asarray(_dft_consts(L)[0])
    nj = L // kc
    full = lambda shape: pl.BlockSpec(shape, lambda j: tuple(0 for _ in shape))
    return pl.pallas_call(
        functools.partial(_spectra_kernel, L=L, kc=kc),
        grid=(nj,),
        in_specs=[full((L, 128)), full((128, HY_HID)), full((1, HY_HID)), full((HY_HID, HY_HID)),
                  full((1, HY_HID)), full((1, HY_HID)), full((HY_HID, 4 * HY_W)), full((L, HY_W)),
                  pl.BlockSpec((kc, L), lambda j: (j, 0)),
                  pl.BlockSpec((kc, L), lambda j: (j + nj, 0))],
        out_specs=[pl.BlockSpec((kc, 2 * HY_W), lambda j: (j, 0)),
                   pl.BlockSpec((kc, 2 * HY_W), lambda j: (j, 0))],
        out_shape=[jax.ShapeDtypeStruct((L, 2 * HY_W), F32)] * 2,
        scratch_shapes=[pltpu.VMEM((L, 4 * HY_W), BF16)],
        compiler_params=_cparams(("arbitrary",)),
    )(jnp.asarray(z), w1p, b1, w2, b2, fr, w3, jnp.asarray(decay), f, f)


def _hyconv_kernel(*refs, L, nb, kc, nj, row_blk0, staged):
    n_in = 1 if staged else nb
    (cw_ref, cb_ref, d_ref, fa_ref, fb_ref, ga_ref, gb_ref, sa_ref, sb_ref, o_ref,
     xin_scr, x_scr, g1_scr, g2_scr, y_scr) = refs[n_in:n_in + 15]
    bg = pl.program_id(0)
    o = pl.program_id(1)
    j = pl.program_id(2)

    @pl.when((o == 0) & (j == 0))
    def _():
        row = lax.broadcasted_iota(jnp.int32, (L, HY_W), 0)
        pieces = [(i, part) for i in range(nb) for part in range(3)]
        if staged:
            p_hbm, (stage, sem) = refs[0], refs[n_in + 15:]

            def piece_copy(k):
                i, part = pieces[k]
                row0 = pl.multiple_of((row_blk0 + bg * nb + i) * L, L)
                return pltpu.make_async_copy(
                    p_hbm.at[pl.ds(row0, L), C_HY + HY_W * part:C_HY + HY_W * (part + 1)],
                    stage.at[k % 2], sem.at[k % 2])

            piece_copy(0).start()
            piece_copy(1).start()
        for k, (i, part) in enumerate(pieces):
            cols = slice(HY_W * i, HY_W * (i + 1))
            dst = (x_scr, g1_scr, g2_scr)[part]
            pc = slice(HY_W * part, HY_W * (part + 1))
            if staged:
                piece_copy(k).wait()
                u = stage[k % 2]
            else:
                u = refs[i][:, pc]
            up = jnp.where(row == 0, 0.0, pltpu.roll(u, 1, axis=0))
            un = jnp.where(row == L - 1, 0.0, pltpu.roll(u, L - 1, axis=0))
            dst[:, cols] = (up * cw_ref[0:1, pc] + u * cw_ref[1:2, pc] + un * cw_ref[2:3, pc]
                            + cb_ref[:, pc])
            if part == 0:
                xin_scr[:, cols] = x_scr[:, cols].astype(BF16)
            if staged and k + 2 < len(pieces):
                piece_copy(k + 2).start()

    @pl.when(j == 0)
    def _():
        y_scr[...] = jnp.zeros_like(y_scr)

    xa = _dot(fa_ref[...], xin_scr[...])
    xb = _dot(fb_ref[...], xin_scr[...])
    sa = jnp.concatenate([sa_ref[...]] * nb, axis=1)
    sb = jnp.concatenate([sb_ref[...]] * nb, axis=1)
    first = (lax.broadcasted_iota(jnp.int32, (kc, nb * HY_W), 0) + j * kc) == 0
    ya = jnp.where(first, xa * sa, xa * sa - xb * sb)
    yb = jnp.where(first, xb * sb, xa * sb + xb * sa)
    y_scr[...] += _dot(ga_ref[...], ya.astype(BF16)) + _dot(gb_ref[...], yb.astype(BF16))

    @pl.when(j == nj - 1)
    def _():
        dvec = jnp.concatenate([d_ref[pl.ds(o, 1), :]] * nb, axis=1)
        y = y_scr[...] + x_scr[...] * dvec

        @pl.when(o == 0)
        def _():
            zz = g1_scr[...] * y
            x_scr[...] = zz
            xin_scr[...] = zz.astype(BF16)

        @pl.when(o == 1)
        def _():
            res = g2_scr[...] * y
            for i in range(nb):
                o_ref[L * i:L * (i + 1), :] = res[:, HY_W * i:HY_W * (i + 1)].astype(BF16)


def _hy_conv(p, L, nb, kc, row_blk0, n_seq, conv_w, conv_b, d, sa, sb):
    f, g = (jnp.asarray(a) for a in _dft_consts(L))
    nj = L // kc
    staged = nb * L > DEC_SEQ
    if staged:
        u_specs = [pl.BlockSpec(memory_space=pl.ANY)]
        u_scratch = [pltpu.VMEM((2, L, HY_W), F32), pltpu.SemaphoreType.DMA((2,))]
    else:
        u_specs = [pl.BlockSpec((L, 3 * HY_W), lambda bg, o, j, i=i: (row_blk0 + bg * nb + i, C_HY // 768))
                   for i in range(nb)]
        u_scratch = []
    small = lambda shape: pl.BlockSpec(shape, lambda bg, o, j: (0, 0))
    return pl.pallas_call(
        functools.partial(_hyconv_kernel, L=L, nb=nb, kc=kc, nj=nj, row_blk0=row_blk0, staged=staged),
        grid=(n_seq // nb, 2, nj),
        in_specs=u_specs + [
            small((3, 3 * HY_W)), small((1, 3 * HY_W)), small((2, HY_W)),
            pl.BlockSpec((kc, L), lambda bg, o, j: (j, 0)),
            pl.BlockSpec((kc, L), lambda bg, o, j: (j + nj, 0)),
            pl.BlockSpec((L, kc), lambda bg, o, j: (0, j)),
            pl.BlockSpec((L, kc), lambda bg, o, j: (0, j + nj)),
            pl.BlockSpec((kc, HY_W), lambda bg, o, j: (j, o)),
            pl.BlockSpec((kc, HY_W), lambda bg, o, j: (j, o))],
        out_specs=pl.BlockSpec((nb * L, HY_W), lambda bg, o, j: (bg, 0)),
        out_shape=jax.ShapeDtypeStruct((n_seq * L, HY_W), BF16),
        scratch_shapes=[pltpu.VMEM((L, nb * HY_W), BF16)] + [pltpu.VMEM((L, nb * HY_W), F32)] * 4
        + u_scratch,
        compiler_params=_cparams(("arbitrary", "arbitrary", "arbitrary")),
    )(*([p] * len(u_specs)), conv_w, conv_b, d, f, f, g, g, sa, sb)


def _outproj_kernel(xp_ref, xs_ref, mixp_ref, hyp_ref, nas_ref, ds_ref, hys_ref, g1_ref, w_ref, o_ref):
    i = pl.program_id(0)
    w = lambda a, b: w_ref[a:b, :].astype(BF16)

    @pl.when(i < NP_TOK // TM)
    def _():
        y = _dot(mixp_ref[...], w(0, 768)) + _dot(hyp_ref[...], w(768, D))
        o_ref[...] = xp_ref[...] + g1_ref[...] * y

    @pl.when(i >= NP_TOK // TM)
    def _():
        y = _dot(nas_ref[...], w(0, 256)) + _dot(ds_ref[...], w(256, 768)) + _dot(hys_ref[...], w(768, D))
        o_ref[...] = xs_ref[...] + g1_ref[...] * y


def _x_specs(xs_block0):
    npt = NP_TOK // TM
    return [pl.BlockSpec((TM, D), lambda i, *_: (jnp.minimum(i, npt - 1), 0)),
            pl.BlockSpec((TM, D), lambda i, *_: (jnp.maximum(i - npt, 0) + xs_block0, 0))]


def _outproj(xp, xs, xs_block0, mix_p, hy_p, na_s, d_s, hy_s, mod6, w_out, l):
    npt = NP_TOK // TM
    pidx = lambda i: (jnp.minimum(i, npt - 1), 0)
    sidx = lambda i: (jnp.maximum(i - npt, 0), 0)
    return pl.pallas_call(
        _outproj_kernel,
        grid=(N_TOK // TM,),
        in_specs=_x_specs(xs_block0) + [
                  pl.BlockSpec((TM, 768), pidx),
                  pl.BlockSpec((TM, HY_W), pidx),
                  pl.BlockSpec((TM, 256), sidx),
                  pl.BlockSpec((TM, 512), sidx),
                  pl.BlockSpec((TM, HY_W), sidx),
                  _mod_spec(2, TM),
                  pl.BlockSpec((None, D, D), lambda i: (l, 0, 0))],
        out_specs=pl.BlockSpec((TM, D), lambda i: (i, 0)),
        out_shape=jax.ShapeDtypeStruct((N_TOK, D), F32),
        compiler_params=_cparams(("arbitrary",)),
    )(xp, xs, mix_p, hy_p, na_s, d_s, hy_s, mod6, w_out)


def _ffn_kernel(x_ref, g_ref, sc_ref, sh_ref, gate_ref, wg_ref, wu_ref, wd_ref, o_ref, h_scr, *, nj):
    j = pl.program_id(1)

    @pl.when(j == 0)
    def _():
        h_scr[...] = _norm_mod(x_ref[...], g_ref[...], sc_ref[...], sh_ref[...]).astype(BF16)
        o_ref[...] = jnp.zeros_like(o_ref)

    h = h_scr[...]
    a = _silu(_dot(h, wg_ref[...].astype(BF16))) * _dot(h, wu_ref[...].astype(BF16))
    o_ref[...] += _dot(a.astype(BF16), wd_ref[...].astype(BF16))

    @pl.when(j == nj - 1)
    def _():
        o_ref[...] = x_ref[...] + gate_ref[...] * o_ref[...]


def _dense_ffn(x, g, mod6, wg, wu, wd, i_ffn):
    tf = 256
    tm = DEC_SEQ
    nj = FFN // tf
    return pl.pallas_call(
        functools.partial(_ffn_kernel, nj=nj),
        grid=(N_TOK // tm, nj),
        in_specs=[pl.BlockSpec((tm, D), lambda i, j: (i, 0)),
                  pl.BlockSpec((1, D), lambda i, j: (0, 0)),
                  _mod_spec(4, tm), _mod_spec(3, tm), _mod_spec(5, tm),
                  pl.BlockSpec((None, D, tf), lambda i, j: (i_ffn, 0, j)),
                  pl.BlockSpec((None, D, tf), lambda i, j: (i_ffn, 0, j)),
                  pl.BlockSpec((None, tf, D), lambda i, j: (i_ffn, j, 0))],
        out_specs=pl.BlockSpec((tm, D), lambda i, j: (i, 0)),
        out_shape=jax.ShapeDtypeStruct((N_TOK, D), F32),
        scratch_shapes=[pltpu.VMEM((tm, D), BF16)],
        compiler_params=_cparams(("arbitrary", "arbitrary")),
    )(x, g, mod6, mod6, mod6, wg, wu, wd)


def _router_kernel(x_ref, g_ref, sc_ref, sh_ref, wr_ref, h_ref, r_ref):
    h = _norm_mod(x_ref[...], g_ref[...], sc_ref[...], sh_ref[...])
    h_ref[...] = h.astype(BF16)
    lane = lax.broadcasted_iota(jnp.int32, (TM, 128), 1)
    lg = jnp.where(lane < N_EXP, _dot3(h, wr_ref[...]), -jnp.inf)
    m1 = jnp.max(lg, axis=-1, keepdims=True)
    i1 = jnp.min(jnp.where(lg == m1, lane, 128), axis=-1, keepdims=True)
    lg2 = jnp.where(lane == i1, -jnp.inf, lg)
    m2 = jnp.max(lg2, axis=-1, keepdims=True)
    i2 = jnp.min(jnp.where(lg2 == m2, lane, 128), axis=-1, keepdims=True)
    e = jnp.exp(m2 - m1)
    w1 = 1.0 / (1.0 + e)
    w2 = e / (1.0 + e)
    r_ref[...] = jnp.where(lane == 0, i1.astype(F32),
                           jnp.where(lane == 1, i2.astype(F32),
                                     jnp.where(lane == 2, w1, jnp.where(lane == 3, w2, 0.0))))


def _router(x, g, mod6, wr_pad):
    return pl.pallas_call(
        _router_kernel,
        grid=(N_TOK // TM,),
        in_specs=[pl.BlockSpec((TM, D), lambda i: (i, 0)),
                  pl.BlockSpec((1, D), lambda i: (0, 0)),
                  _mod_spec(4, TM), _mod_spec(3, TM),
                  pl.BlockSpec((D, 128), lambda i: (0, 0))],
        out_specs=[pl.BlockSpec((TM, D), lambda i: (i, 0)),
                   pl.BlockSpec((TM, 128), lambda i: (i, 0))],
        out_shape=[jax.ShapeDtypeStruct((N_TOK, D), BF16), jax.ShapeDtypeStruct((N_TOK, 128), F32)],
        compiler_params=_cparams(("arbitrary",)),
    )(x, g, mod6, mod6, wr_pad)


def _dispatch_kernel(blo_ref, bhi_ref, sexp_ref, h_hbm, rp_ref, o_ref, h_scr, sem):
    s = pl.program_id(0)

    @pl.when(s == 0)
    def _():
        cp = pltpu.make_async_copy(h_hbm, h_scr, sem)
        cp.start()
        cp.wait()

    e = sexp_ref[s]
    rows = s * MOE_TS + lax.broadcasted_iota(jnp.int32, (MOE_TS, MOE_DTB), 0)

    def sel(b):
        return (rows == rp_ref[e, pl.ds(b, 1), :]).astype(BF16)

    is_empty = blo_ref[s] > bhi_ref[s]

    @pl.when(is_empty)
    def _():
        o_ref[...] = jnp.zeros_like(o_ref)

    @pl.when(jnp.logical_not(is_empty))
    def _():
        b0 = jnp.minimum(blo_ref[s], N_TOK // MOE_DTB - MOE_DWIN)
        hwin = h_scr[pl.ds(pl.multiple_of(b0 * MOE_DTB, MOE_DTB), MOE_DWIN * MOE_DTB), :]
        acc = _dot(jnp.concatenate([sel(b0 + k) for k in range(MOE_DWIN)], axis=1), hwin)

        def body(b, acc):
            hb = h_scr[pl.ds(pl.multiple_of(b * MOE_DTB, MOE_DTB), MOE_DTB), :]
            return acc + _dot(sel(b), hb)

        acc = lax.fori_loop(b0 + MOE_DWIN, bhi_ref[s] + 1, body, acc)
        o_ref[...] = acc.astype(BF16)


def _dispatch(blo, bhi, sexp, h, rp_t):
    return pl.pallas_call(
        _dispatch_kernel,
        grid_spec=pltpu.PrefetchScalarGridSpec(
            num_scalar_prefetch=3,
            grid=(MOE_ROWS // MOE_TS,),
            in_specs=[pl.BlockSpec(memory_space=pl.ANY),
                      pl.BlockSpec((N_EXP, N_TOK // MOE_DTB, MOE_DTB), lambda s, *_: (0, 0, 0))],
            out_specs=pl.BlockSpec((MOE_TS, D), lambda s, *_: (s, 0)),
            scratch_shapes=[pltpu.VMEM((N_TOK, D), BF16), pltpu.SemaphoreType.DMA(())]),
        out_shape=jax.ShapeDtypeStruct((MOE_ROWS, D), BF16),
        compiler_params=_cparams(("arbitrary",)),
    )(blo, bhi, sexp, h, rp_t)


def _experts_kernel(te_ref, used_ref, rows_ref, xs_ref, wg_ref, wu_ref, wd_ref, o_ref, acc_scr, *, nj):
    i = pl.program_id(0)
    j = pl.program_id(1)
    nrows = rows_ref[i]

    @pl.when(j == 0)
    def _():
        acc_scr[...] = jnp.zeros_like(acc_scr)

    for sz in range(MOE_TQ, MOE_TM + 1, MOE_TQ):
        @pl.when(nrows == sz)
        def _():
            xb = xs_ref[0:sz, :]
            a = _silu(_dot(xb, wg_ref[...].astype(BF16))) * _dot(xb, wu_ref[...].astype(BF16))
            acc_scr[0:sz, :] += _dot(a.astype(BF16), wd_ref[...].astype(BF16))

    @pl.when(j == nj - 1)
    def _():
        o_ref[...] = acc_scr[...].astype(BF16)


def _experts(tile_expert, used, tile_rows, xs, wg, wu, wd, i_moe):
    nj = EXP_DIM // MOE_TF

    def tile(i, used):
        return jnp.minimum(i, used[0] - 1)

    def chunk(i, j, used):
        return jnp.where(i < used[0], j, nj - 1)

    return pl.pallas_call(
        functools.partial(_experts_kernel, nj=nj),
        grid_spec=pltpu.PrefetchScalarGridSpec(
            num_scalar_prefetch=3,
            grid=(MOE_YROWS // MOE_TM, nj),
            in_specs=[pl.BlockSpec((MOE_TM, D), lambda i, j, te, used, tr: (tile(i, used), 0)),
                      pl.BlockSpec((None, None, D, MOE_TF),
                                   lambda i, j, te, used, tr: (i_moe, te[tile(i, used)], 0, chunk(i, j, used))),
                      pl.BlockSpec((None, None, D, MOE_TF),
                                   lambda i, j, te, used, tr: (i_moe, te[tile(i, used)], 0, chunk(i, j, used))),
                      pl.BlockSpec((None, None, MOE_TF, D),
                                   lambda i, j, te, used, tr: (i_moe, te[tile(i, used)], chunk(i, j, used), 0))],
            out_specs=pl.BlockSpec((MOE_TM, D), lambda i, j, te, used, tr: (i, 0)),
            scratch_shapes=[pltpu.VMEM((MOE_TM, D), F32)]),
        out_shape=jax.ShapeDtypeStruct((MOE_YROWS, D), BF16),
        compiler_params=_cparams(("arbitrary", "arbitrary")),
    )(tile_expert, used, tile_rows, xs, wg, wu, wd)


def _combine_kernel(ws_ref, kind_ref, x_ref, rp_ref, comb_ref, gate_ref, fg_ref, ys_hbm, op_ref, os_ref,
                    win_scr, y_scr, sem):
    b = pl.program_id(0)
    nb = pl.num_programs(0)
    slot = b % 2

    def win_copy(blk, sl, e):
        start = pl.multiple_of(ws_ref[blk * N_EXP + e], MOE_ALIGN)
        return pltpu.make_async_copy(ys_hbm.at[pl.ds(start, MOE_WIN)], win_scr.at[sl, e], sem.at[sl, e])

    @pl.when(b == 0)
    def _():
        for e in range(N_EXP):
            win_copy(0, 0, e).start()

    @pl.when(b + 1 < nb)
    def _():
        for e in range(N_EXP):
            win_copy(b + 1, 1 - slot, e).start()

    col = lax.broadcasted_iota(jnp.int32, (MOE_TB, MOE_WIN), 1)
    y_scr[...] = jnp.zeros_like(y_scr)
    for e in range(N_EXP):
        win_copy(b, slot, e).wait()
        rel = rp_ref[:, e:e + 1] - ws_ref[b * N_EXP + e]
        cw = comb_ref[:, e:e + 1]
        kind = kind_ref[b * N_EXP + e]

        for k, wn in enumerate(MOE_WINS):
            @pl.when(kind == k)
            def _():
                sel = (rel == col[:, :wn]).astype(BF16) if wn == MOE_WIN else (
                    rel == lax.broadcasted_iota(jnp.int32, (MOE_TB, wn), 1)).astype(BF16)
                y_scr[...] += cw * _dot(sel, win_scr[slot, e, 0:wn, :])

    x = x_ref[...] + gate_ref[...] * y_scr[...]
    x = (x * lax.rsqrt(jnp.mean(x * x, axis=-1, keepdims=True) + EPS)) * fg_ref[...]

    @pl.when(b < NP_TOK // MOE_TB)
    def _():
        op_ref[...] = x

    @pl.when(b >= NP_TOK // MOE_TB)
    def _():
        os_ref[...] = x


def _combine(ws, kind, x, rp8, comb, mod6, final_g, ys):
    npb = NP_TOK // MOE_TB
    return pl.pallas_call(
        _combine_kernel,
        grid_spec=pltpu.PrefetchScalarGridSpec(
            num_scalar_prefetch=2,
            grid=(N_TOK // MOE_TB,),
            in_specs=[pl.BlockSpec((MOE_TB, D), lambda b, *_: (b, 0)),
                      pl.BlockSpec((MOE_TB, N_EXP), lambda b, *_: (b, 0)),
                      pl.BlockSpec((MOE_TB, N_EXP), lambda b, *_: (b, 0)),
                      _mod_spec(5, MOE_TB),
                      pl.BlockSpec((1, D), lambda b, *_: (0, 0)),
                      pl.BlockSpec(memory_space=pl.ANY)],
            out_specs=[pl.BlockSpec((MOE_TB, D), lambda b, *_: (jnp.minimum(b, npb - 1), 0)),
                       pl.BlockSpec((MOE_TB, D), lambda b, *_: (jnp.maximum(b - npb, 0), 0))],
            scratch_shapes=[pltpu.VMEM((2, N_EXP, MOE_WIN, D), BF16),
                            pltpu.VMEM((MOE_TB, D), F32),
                            pltpu.SemaphoreType.DMA((2, N_EXP))]),
        out_shape=[jax.ShapeDtypeStruct((NP_TOK, D), F32), jax.ShapeDtypeStruct((NS_TOK, D), F32)],
        compiler_params=_cparams(("arbitrary",)),
    )(ws, kind, x, rp8, comb, mod6, final_g, ys)


def _moe(x, g, mod6, router, wg, wu, wd, i_moe, final_g):
    wr_pad = jnp.pad(router, ((0, 0), (0, 128 - N_EXP)))
    h, r = _router(x, g, mod6, wr_pad)

    i32 = jnp.int32
    i12 = r[:, 0:2].astype(i32)
    earange = jnp.arange(N_EXP, dtype=i32)
    hit1 = i12[:, 0:1] == earange[None, :]
    hit2 = i12[:, 1:2] == earange[None, :]
    comb = jnp.where(hit1, r[:, 2:3], 0.0) + jnp.where(hit2, r[:, 3:4], 0.0)
    mask = (hit1 | hit2).astype(i32)
    csum = jnp.cumsum(mask, axis=0)
    counts = csum[-1]
    padded = ((counts + MOE_TM - 1) // MOE_TM) * MOE_TM
    ends = jnp.cumsum(padded)
    starts = ends - padded
    rp8 = jnp.where(mask > 0, starts[None, :] + csum - 1, -1).astype(i32)
    n_tiles = MOE_ROWS // MOE_TM
    tile_expert = jnp.minimum(
        jnp.searchsorted(ends, jnp.arange(n_tiles, dtype=i32) * MOE_TM, side="right"),
        N_EXP - 1).astype(i32)
    used = (ends[-1:] // MOE_TM).astype(i32)
    tile_row0 = jnp.arange(MOE_YROWS // MOE_TM, dtype=i32) * MOE_TM
    te_all = jnp.minimum(jnp.searchsorted(ends, tile_row0, side="right"), N_EXP - 1)
    valid = jnp.clip(counts[te_all] - (tile_row0 - starts[te_all]), 0, MOE_TM)
    valid = jnp.where(tile_row0 < ends[-1], valid, 0)
    tile_rows = (((valid + MOE_TQ - 1) // MOE_TQ) * MOE_TQ).astype(i32)

    sub_row0 = jnp.arange(MOE_ROWS // MOE_TS, dtype=i32) * MOE_TS
    sexp = tile_expert[sub_row0 // MOE_TM]
    qlo = sub_row0 - starts[sexp]
    qend = jnp.minimum(qlo + MOE_TS, counts[sexp])
    cbe = csum[MOE_DTB - 1::MOE_DTB, :].T[sexp]
    blo = jnp.sum((cbe <= qlo[:, None]).astype(i32), axis=1)
    bhi = jnp.minimum(jnp.sum((cbe < qend[:, None]).astype(i32), axis=1), N_TOK // MOE_DTB - 1)
    empty = qend <= qlo
    blo = jnp.where(empty, 1, blo).astype(i32)
    bhi = jnp.where(empty, 0, bhi).astype(i32)

    cb = csum[MOE_TB - 1::MOE_TB, :]
    cprev = jnp.concatenate([jnp.zeros((1, N_EXP), i32), cb[:-1]], axis=0)
    ws = (((starts[None, :] + cprev) // MOE_ALIGN) * MOE_ALIGN).reshape(-1).astype(i32)
    n_be = cb - cprev
    kind = sum((n_be > wn - MOE_ALIGN).astype(i32) for wn in MOE_WINS[:-1])
    kind = jnp.where(n_be == 0, len(MOE_WINS), kind).reshape(-1).astype(i32)

    rp_t = rp8.T.reshape(N_EXP, N_TOK // MOE_DTB, MOE_DTB)
    xs = _dispatch(blo, bhi, sexp, h, rp_t)
    ys = _experts(tile_expert, used, tile_rows, xs, wg, wu, wd, i_moe)
    return _combine(ws, kind, x, rp8, comb, mod6, final_g, ys)


assert DEPTH == 2

def kernel(x_prompt, x_sample, cache_na_k, cache_na_v, cache_diff_k, cache_diff_v, c, c_ctx, w_in, w_out, ada_w, ada_b, norm_mix_g, norm_ffn_g, na_rpb, diff_lq1, diff_lk1, diff_lq2, diff_lk2, diff_subln_g, hy_conv_w, hy_conv_b, hy_d, hy_f_w1, hy_f_b1, hy_f_w2, hy_f_b2, hy_f_freq, hy_f_w3, ffn_w_gate, ffn_w_up, ffn_w_down, moe_router, moe_w_gate, moe_w_up, moe_w_down, final_norm_g):
    xparts = (x_prompt.reshape(NP_TOK, D), x_sample.reshape(NS_TOK, D), 0)
    cond8 = jnp.concatenate([c_ctx[None, :], c, jnp.zeros((5, D), F32)], axis=0)
    mods = _modulation(cond8, ada_w, ada_b)
    final_g = final_norm_g.reshape(1, D)
    bias = _na_bias(na_rpb)

    leaves = [jnp.zeros((BATCH, DEPTH, NA_HEADS, SEQ, HEAD_DIM), F32),
              jnp.zeros((BATCH, DEPTH, NA_HEADS, SEQ, HEAD_DIM), F32),
              jnp.zeros((BATCH, DEPTH, DIFF_HEADS, 2, SEQ, HEAD_DIM), F32),
              jnp.zeros((BATCH, DEPTH, DIFF_HEADS, SEQ, DIFF_V), F32)]
    for l in range(DEPTH):
        lam_init = 0.8 - 0.6 * math.exp(-0.3 * l)
        mod6 = mods[l].reshape(8, 6, D).transpose(1, 0, 2).reshape(6, 8, 1, D)
        row = lambda a: a[l].reshape(1, -1)
        lq1, lk1, lq2, lk2, subg = row(diff_lq1), row(diff_lk1), row(diff_lq2), row(diff_lk2), row(diff_subln_g)

        p, *leaves = _inproj(*xparts, row(norm_mix_g), mod6, w_in, l, leaves)

        mix_p = _prompt_attn(p, lq1, lk1, lq2, lk2, subg, lam_init)
        na_s = _na_attn(p, cache_na_k, cache_na_v, bias, l)
        d_s = _sdiff_attn(p, cache_diff_k, cache_diff_v, lq1, lk1, lq2, lk2, subg, lam_init, l)

        w1p = jnp.pad(hy_f_w1[l], ((0, 128 - HY_EMB), (0, 0)))
        fargs = (w1p, row(hy_f_b1), hy_f_w2[l], row(hy_f_b2), row(hy_f_freq), hy_f_w3[l])
        cargs = (hy_conv_w[l], row(hy_conv_b), hy_d[l])
        sa_p, sb_p = _hy_spectra(SEQ, SEQ, *fargs)
        hy_p = _hy_conv(p, SEQ, 4, SEQ, 0, BATCH, *cargs, sa_p, sb_p)
        sa_s, sb_s = _hy_spectra(DEC_SEQ, 512, *fargs)
        hy_s = _hy_conv(p, DEC_SEQ, 2, 512, NP_TOK // DEC_SEQ, DEC_BATCH, *cargs, sa_s, sb_s)

        x = _outproj(*xparts, mix_p, hy_p, na_s, d_s, hy_s, mod6, w_out, l)

        if l == 0:
            x = _dense_ffn(x, row(norm_ffn_g), mod6, ffn_w_gate, ffn_w_up, ffn_w_down, 0)
            xparts = (x, x, NP_TOK // TM)
        else:
            yp, ys = _moe(x, row(norm_ffn_g), mod6, moe_router[0], moe_w_gate, moe_w_up, moe_w_down,
                          0, final_g)

    return (yp.reshape(BATCH, SEQ, D), ys.reshape(DEC_BATCH, DEC_SEQ, D), *leaves)
```

```python
import functools
import math

import numpy as np
import jax
import jax.numpy as jnp
from jax import lax
from jax.experimental import pallas as pl
from jax.experimental.pallas import tpu as pltpu

F32 = jnp.float32
BF16 = jnp.bfloat16

D = 1024
BATCH, SEQ = 32, 256
DEC_BATCH, DEC_SEQ = 2, 2048
DEPTH = 2
PAST = 512
GRID_W = 64
GRID_ROWS = DEC_SEQ // GRID_W
HEAD_DIM = 64
NA_HEADS = 4
DIFF_HEADS = 4
DIFF_V = 128
WIN_ROWS, WIN_COLS = 8, 16
HY_W = 256
HY_EMB = 33
HY_BANDS = 16
HY_HID = 64
PROJ = 3072
FFN = 2816
N_EXP = 8
EXP_DIM = 3584
EPS = 1e-6
ROPE_BASE = 10000.0

NP_TOK = BATCH * SEQ
NS_TOK = DEC_BATCH * DEC_SEQ
N_TOK = NP_TOK + NS_TOK

C_NAQ, C_NAK, C_NAV = 0, 256, 512
C_DQ, C_DK, C_DV = 768, 1280, 1792
C_HY = 2304

TM = 1024
VMEM_LIMIT = 56 * 1024 * 1024

MOE_TM = 1024
MOE_TF = 896
MOE_TQ = 256
MOE_ROWS = 2 * N_TOK + N_EXP * MOE_TM
MOE_TS = 256
MOE_DTB = 512
MOE_DWIN = 3
MOE_TB = 512
MOE_ALIGN = 16
MOE_WIN = MOE_TB + MOE_ALIGN
MOE_WINS = (MOE_TB // 4 + MOE_ALIGN, MOE_TB // 2 + MOE_ALIGN, MOE_WIN)
MOE_YROWS = MOE_ROWS + MOE_TM


def _cparams(sem):
    return pltpu.CompilerParams(dimension_semantics=sem, vmem_limit_bytes=VMEM_LIMIT)


def _dot(a, b):
    return jnp.dot(a, b, preferred_element_type=F32)


def _dot_nt(a, b):
    return lax.dot_general(a, b, (((1,), (1,)), ((), ())), preferred_element_type=F32)


def _split(a):
    hi = a.astype(BF16)
    lo = (a - hi.astype(F32)).astype(BF16)
    return hi, lo


def _dot3(a, b):
    ah, al = _split(a)
    bh, bl = _split(b)
    return _dot(ah, bh) + (_dot(ah, bl) + _dot(al, bh))


def _silu(x):
    return x / (1.0 + jnp.exp(-x))


def _mod_row(i, tm):
    t = i * tm
    return jnp.where(t < NP_TOK, 0, 1 + (t - NP_TOK) // DEC_SEQ)


def _mod_spec(k, tm):
    return pl.BlockSpec((None, None, 1, D), lambda i, *_: (k, _mod_row(i, tm), 0, 0))


def _norm_mod(x, g, sc, sh):
    y = x * lax.rsqrt(jnp.mean(x * x, axis=-1, keepdims=True) + EPS)
    return (y * g) * (1.0 + sc) + sh


def _mod_kernel(c_ref, w_ref, b_ref, o_ref):
    o_ref[...] = _dot3(_silu(c_ref[...]), w_ref[...]) + b_ref[...]


def _modulation(cond8, ada_w, ada_b):
    tn = 1536
    return pl.pallas_call(
        _mod_kernel,
        grid=(DEPTH, 6 * D // tn),
        in_specs=[pl.BlockSpec((8, D), lambda l, j: (0, 0)),
                  pl.BlockSpec((None, D, tn), lambda l, j: (l, 0, j)),
                  pl.BlockSpec((None, 1, tn), lambda l, j: (l, 0, j))],
        out_specs=pl.BlockSpec((None, 8, tn), lambda l, j: (l, 0, j)),
        out_shape=jax.ShapeDtypeStruct((DEPTH, 8, 6 * D), F32),
        compiler_params=_cparams(("arbitrary", "arbitrary")),
    )(cond8, ada_w, ada_b.reshape(DEPTH, 1, 6 * D))


def _inproj_kernel(xp_ref, xs_ref, g_ref, sc_ref, sh_ref, w_hbm, nak_in, nav_in, dk_in, dv_in,
                   o_ref, nak_ref, nav_ref, dk_ref, dv_ref, h_scr, w_scr, stage, sem, *, l, tn):
    del nak_in, nav_in, dk_in, dv_in
    i = pl.program_id(0)
    j = pl.program_id(1)

    @pl.when((i == 0) & (j == 0))
    def _():
        for c in range(PROJ // tn):
            cp = pltpu.make_async_copy(w_hbm.at[l, :, c * tn:(c + 1) * tn], stage, sem)
            cp.start()
            cp.wait()
            w_scr[c] = stage[...].astype(BF16)

    is_ctx = i < NP_TOK // TM

    @pl.when((j == 0) & is_ctx)
    def _():
        h_scr[...] = _norm_mod(xp_ref[...], g_ref[...], sc_ref[...], sh_ref[...]).astype(BF16)

    @pl.when((j == 0) & jnp.logical_not(is_ctx))
    def _():
        h_scr[...] = _norm_mod(xs_ref[...], g_ref[...], sc_ref[...], sh_ref[...]).astype(BF16)

    o_ref[...] = _dot(h_scr[...], w_scr[j])


    def rows(bb):
        return slice(bb * SEQ, (bb + 1) * SEQ)

    def copy_heads(dst_ref, c0, width, heads, sub=None):
        for bb in range(TM // SEQ):
            for h in heads:
                if sub is None:
                    dst_ref[bb, h] = o_ref[rows(bb), c0(h):c0(h) + width]
                else:
                    for s in range(2):
                        dst_ref[bb, h, s] = o_ref[rows(bb), c0(h) + s * width:c0(h) + (s + 1) * width]

    @pl.when(is_ctx & (j == 0))
    def _():
        copy_heads(nak_ref, lambda h: C_NAK + HEAD_DIM * h, HEAD_DIM, range(NA_HEADS))
        copy_heads(nav_ref, lambda h: C_NAV + HEAD_DIM * h, HEAD_DIM, range(NA_HEADS))

    @pl.when(is_ctx & (j == 1))
    def _():
        copy_heads(dk_ref, lambda h: C_DK - tn + 2 * HEAD_DIM * h, HEAD_DIM, range(0, 2), sub=True)

    @pl.when(is_ctx & (j == 2))
    def _():
        copy_heads(dk_ref, lambda h: C_DK - 2 * tn + 2 * HEAD_DIM * h, HEAD_DIM, range(2, 4), sub=True)
        copy_heads(dv_ref, lambda h: C_DV - 2 * tn + DIFF_V * h, DIFF_V, range(DIFF_HEADS))


def _inproj(xp, xs, xs_block0, g, mod6, w_in, l, leaves):
    tn = 768
    nb = TM // SEQ
    ctx = lambda i: jnp.minimum(i, NP_TOK // TM - 1)
    leaf_specs = [pl.BlockSpec((nb, None, NA_HEADS, SEQ, HEAD_DIM), lambda i, j: (ctx(i), l, 0, 0, 0)),
                  pl.BlockSpec((nb, None, NA_HEADS, SEQ, HEAD_DIM), lambda i, j: (ctx(i), l, 0, 0, 0)),
                  pl.BlockSpec((nb, None, DIFF_HEADS, 2, SEQ, HEAD_DIM), lambda i, j: (ctx(i), l, 0, 0, 0, 0)),
                  pl.BlockSpec((nb, None, DIFF_HEADS, SEQ, DIFF_V), lambda i, j: (ctx(i), l, 0, 0, 0))]
    return pl.pallas_call(
        functools.partial(_inproj_kernel, l=l, tn=tn),
        grid=(N_TOK // TM, PROJ // tn),
        in_specs=_x_specs(xs_block0) + [
                  pl.BlockSpec((1, D), lambda i, j: (0, 0)),
                  _mod_spec(1, TM), _mod_spec(0, TM),
                  pl.BlockSpec(memory_space=pl.ANY)] + [pl.BlockSpec(memory_space=pl.ANY)] * 4,
        out_specs=[pl.BlockSpec((TM, tn), lambda i, j: (i, j))] + leaf_specs,
        out_shape=[jax.ShapeDtypeStruct((N_TOK, PROJ), F32)]
        + [jax.ShapeDtypeStruct(a.shape, a.dtype) for a in leaves],
        input_output_aliases={6: 1, 7: 2, 8: 3, 9: 4},
        scratch_shapes=[pltpu.VMEM((TM, D), BF16), pltpu.VMEM((PROJ // tn, D, tn), BF16),
                        pltpu.VMEM((D, tn), F32), pltpu.SemaphoreType.DMA(())],
        compiler_params=_cparams(("arbitrary", "arbitrary")),
    )(xp, xs, g, mod6, mod6, w_in, *leaves)


def _lam(lq1, lk1, lq2, lk2, lam_init):
    return (jnp.exp(jnp.sum(lq1[...] * lk1[...], axis=-1, keepdims=True))
            - jnp.exp(jnp.sum(lq2[...] * lk2[...], axis=-1, keepdims=True)) + lam_init)


def _softmax_parts(s):
    m = jnp.max(s, axis=-1, keepdims=True)
    e = jnp.exp(s - m)
    return e, jnp.sum(e, axis=-1, keepdims=True)


QK_SCALE = HEAD_DIM ** -0.5


def _diff_head(q1, q2, k1, k2, v, lam, g, lam_init):
    e1, l1 = _softmax_parts(_dot_nt(q1, k1))
    e2, l2 = _softmax_parts(_dot_nt(q2, k2))
    a = e1 - (lam * l1 * (1.0 / l2)) * e2
    o = _dot(a.astype(BF16), v) * (1.0 / l1)
    o = o * lax.rsqrt(jnp.mean(o * o, axis=-1, keepdims=True) + EPS)
    return (o * g) * (1.0 - lam_init)


def _prompt_attn_kernel(pa_ref, pb_ref, pc_ref, lq1, lk1, lq2, lk2, g_ref, o_ref, *, lam_init):
    lam = _lam(lq1, lk1, lq2, lk2, lam_init)
    g = g_ref[...]

    def col(c0, w, scale=None):
        ref = (pa_ref, pb_ref, pc_ref)[c0 // 768]
        o = c0 % 768
        a = ref[:, o:o + w]
        return (a if scale is None else a * scale).astype(BF16)

    for h in range(NA_HEADS):
        q = col(C_NAQ + 64 * h, 64, QK_SCALE)
        k = col(C_NAK + 64 * h, 64)
        v = col(C_NAV + 64 * h, 64)
        e, l = _softmax_parts(_dot_nt(q, k))
        o = _dot(e.astype(BF16), v) * (1.0 / l)
        o_ref[:, 64 * h:64 * h + 64] = o.astype(BF16)
    for h in range(DIFF_HEADS):
        q1 = col(C_DQ + 128 * h, 64, QK_SCALE)
        q2 = col(C_DQ + 128 * h + 64, 64, QK_SCALE)
        k1 = col(C_DK + 128 * h, 64)
        k2 = col(C_DK + 128 * h + 64, 64)
        v = col(C_DV + 128 * h, 128)
        o = _diff_head(q1, q2, k1, k2, v, lam, g, lam_init)
        o_ref[:, 256 + 128 * h:384 + 128 * h] = o.astype(BF16)


def _prompt_attn(p, lq1, lk1, lq2, lk2, g, lam_init):
    vec = lambda n: pl.BlockSpec((1, n), lambda b: (0, 0))
    return pl.pallas_call(
        functools.partial(_prompt_attn_kernel, lam_init=lam_init),
        grid=(BATCH,),
        in_specs=[pl.BlockSpec((SEQ, 768), lambda b: (b, 0)),
                  pl.BlockSpec((SEQ, 768), lambda b: (b, 1)),
                  pl.BlockSpec((SEQ, 768), lambda b: (b, 2)),
                  vec(64), vec(64), vec(64), vec(64), vec(128)],
        out_specs=pl.BlockSpec((SEQ, 768), lambda b: (b, 0)),
        out_shape=jax.ShapeDtypeStruct((NP_TOK, 768), BF16),
        compiler_params=_cparams(("arbitrary",)),
    )(p, p, p, lq1, lk1, lq2, lk2, g)


def _bias_kernel(rpb_ref, o_ref):
    lh = pl.program_id(0)
    qc = lax.broadcasted_iota(jnp.int32, (GRID_W, GRID_W), 0)
    kc = lax.broadcasted_iota(jnp.int32, (GRID_W, GRID_W), 1)
    delta = jnp.clip(kc - qc + (WIN_COLS - 1), 0, 2 * WIN_COLS - 2)
    qs = jnp.clip(qc - WIN_COLS // 2, 0, GRID_W - WIN_COLS)
    in_win = (kc >= qs) & (kc < qs + WIN_COLS)
    for dr in range(2 * WIN_ROWS - 1):
        base = (lh * (2 * WIN_ROWS - 1) + dr) * (2 * WIN_COLS - 1)
        acc = jnp.zeros((GRID_W, GRID_W), F32)
        for d in range(2 * WIN_COLS - 1):
            acc = jnp.where(delta == d, rpb_ref[base + d], acc)
        piece = jnp.where(in_win, acc, -jnp.inf)
        for case in range(WIN_ROWS):
            i = dr + case - (WIN_ROWS - 1)
            if 0 <= i < WIN_ROWS:
                o_ref[case, :, i * GRID_W:(i + 1) * GRID_W] = piece


def _na_bias(rpb):
    return pl.pallas_call(
        _bias_kernel,
        grid=(DEPTH * NA_HEADS,),
        in_specs=[pl.BlockSpec(memory_space=pltpu.SMEM)],
        out_specs=pl.BlockSpec((None, WIN_ROWS, GRID_W, WIN_ROWS * GRID_W), lambda lh: (lh, 0, 0, 0)),
        out_shape=jax.ShapeDtypeStruct((DEPTH * NA_HEADS, WIN_ROWS, GRID_W, WIN_ROWS * GRID_W), F32),
        compiler_params=_cparams(("arbitrary",)),
    )(rpb.reshape(-1))


def _na_kernel(q_ref, kv_ref, kc_ref, vc_ref, bias_ref, o_ref):
    nloc = WIN_ROWS * GRID_W
    for rr in range(NA_RB):
        r = pl.program_id(1) * NA_RB + rr
        start = jnp.clip(r - WIN_ROWS // 2, 0, GRID_ROWS - WIN_ROWS)
        case = r - start
        row0 = pl.multiple_of(start * GRID_W, GRID_W)
        qrows = slice(rr * GRID_W, (rr + 1) * GRID_W)
        for h in range(NA_HEADS):
            q = (q_ref[qrows, C_NAQ + 64 * h:C_NAQ + 64 * h + 64] * QK_SCALE).astype(BF16)
            k = kv_ref[pl.ds(row0, nloc), C_NAK + 64 * h:C_NAK + 64 * h + 64].astype(BF16)
            v = kv_ref[pl.ds(row0, nloc), C_NAV + 64 * h:C_NAV + 64 * h + 64].astype(BF16)
            s_loc = _dot_nt(q, k) + bias_ref[h, pl.ds(case, 1)][0]
            s_ctx = _dot_nt(q, kc_ref[h].astype(BF16))
            m = jnp.maximum(jnp.max(s_loc, axis=-1, keepdims=True), jnp.max(s_ctx, axis=-1, keepdims=True))
            e_loc = jnp.exp(s_loc - m)
            e_ctx = jnp.exp(s_ctx - m)
            l = jnp.sum(e_loc, axis=-1, keepdims=True) + jnp.sum(e_ctx, axis=-1, keepdims=True)
            o = (_dot(e_loc.astype(BF16), v) + _dot(e_ctx.astype(BF16), vc_ref[h].astype(BF16))) * (1.0 / l)
            o_ref[qrows, 64 * h:64 * h + 64] = o.astype(BF16)


NA_RB = 4


def _na_attn(p, cache_k, cache_v, bias, l):
    qblk0 = NP_TOK // (NA_RB * GRID_W)
    kvblk0 = NP_TOK // DEC_SEQ
    nrg = GRID_ROWS // NA_RB
    return pl.pallas_call(
        _na_kernel,
        grid=(DEC_BATCH, nrg),
        in_specs=[pl.BlockSpec((NA_RB * GRID_W, 768), lambda b, r: (qblk0 + b * nrg + r, 0)),
                  pl.BlockSpec((DEC_SEQ, 768), lambda b, r: (kvblk0 + b, 0)),
                  pl.BlockSpec((None, None, NA_HEADS, PAST, HEAD_DIM), lambda b, r: (b, l, 0, 0, 0)),
                  pl.BlockSpec((None, None, NA_HEADS, PAST, HEAD_DIM), lambda b, r: (b, l, 0, 0, 0)),
                  pl.BlockSpec((NA_HEADS, WIN_ROWS, GRID_W, WIN_ROWS * GRID_W), lambda b, r: (l, 0, 0, 0))],
        out_specs=pl.BlockSpec((NA_RB * GRID_W, 256), lambda b, r: (b * nrg + r, 0)),
        out_shape=jax.ShapeDtypeStruct((NS_TOK, 256), BF16),
        compiler_params=_cparams(("arbitrary", "arbitrary")),
    )(p, p, cache_k, cache_v, bias)


@functools.lru_cache(None)
def _rope_tables():
    t = np.arange(DEC_SEQ)
    lane = np.arange(128)
    dd = lane % HEAD_DIM
    pos = np.where(dd[None, :] < 32, (t // GRID_W)[:, None], (t % GRID_W)[:, None]).astype(np.float64)
    inv = ROPE_BASE ** (-(dd % 16).astype(np.float64) * 2.0 / 32.0)
    ang = pos * inv[None, :]
    first = (dd % 32) < 16
    cos = np.cos(ang)
    s_up = np.where(first[None, :], -np.sin(ang), 0.0)
    s_dn = np.where(first[None, :], 0.0, np.sin(ang))
    return tuple(np.asarray(a, np.float32) for a in (cos, s_up, s_dn))


def _rope(x, cos, s_up, s_dn):
    return x * cos + pltpu.roll(x, 112, axis=1) * s_up + pltpu.roll(x, 16, axis=1) * s_dn


def _sdiff_kernel(q_ref, k_ref, v_ref, ck_ref, cv_ref, cos_ref, sup_ref, sdn_ref,
                  lq1, lk1, lq2, lk2, g_ref, o_ref, k1_scr, k2_scr, v_scr, *, lam_init, tq):
    qb = pl.program_id(2)

    @pl.when(qb == 0)
    def _():
        kr = _rope(k_ref[...], cos_ref[...], sup_ref[...], sdn_ref[...])
        k1_scr[0:DEC_SEQ, :] = kr[:, :64].astype(BF16)
        k2_scr[0:DEC_SEQ, :] = kr[:, 64:].astype(BF16)
        k1_scr[DEC_SEQ:, :] = ck_ref[0].astype(BF16)
        k2_scr[DEC_SEQ:, :] = ck_ref[1].astype(BF16)
        v_scr[0:DEC_SEQ, :] = v_ref[...].astype(BF16)
        v_scr[DEC_SEQ:, :] = cv_ref[...].astype(BF16)

    lam = _lam(lq1, lk1, lq2, lk2, lam_init)
    half = tq // 2
    for c in range(2):
        rows = pl.ds(pl.multiple_of(qb * tq + c * half, half), half)
        qr = _rope(q_ref[c * half:(c + 1) * half, :], cos_ref[rows, :], sup_ref[rows, :],
                   sdn_ref[rows, :]) * QK_SCALE
        o = _diff_head(qr[:, :64].astype(BF16), qr[:, 64:].astype(BF16), k1_scr[...], k2_scr[...],
                       v_scr[...], lam, g_ref[...], lam_init)
        o_ref[c * half:(c + 1) * half, :] = o.astype(BF16)


def _sdiff_attn(p, cache_k, cache_v, lq1, lk1, lq2, lk2, g, lam_init, l):
    tq = 512
    nq = DEC_SEQ // tq
    cos, s_up, s_dn = (jnp.asarray(a) for a in _rope_tables())
    vec = lambda n: pl.BlockSpec((1, n), lambda b, h, q: (0, 0))
    tab = pl.BlockSpec((DEC_SEQ, 128), lambda b, h, q: (0, 0))
    kvblk0 = NP_TOK // DEC_SEQ
    return pl.pallas_call(
        functools.partial(_sdiff_kernel, lam_init=lam_init, tq=tq),
        grid=(DEC_BATCH, DIFF_HEADS, nq),
        in_specs=[pl.BlockSpec((tq, 128), lambda b, h, q: (NP_TOK // tq + b * nq + q, C_DQ // 128 + h)),
                  pl.BlockSpec((DEC_SEQ, 128), lambda b, h, q: (kvblk0 + b, C_DK // 128 + h)),
                  pl.BlockSpec((DEC_SEQ, 128), lambda b, h, q: (kvblk0 + b, C_DV // 128 + h)),
                  pl.BlockSpec((None, None, None, 2, PAST, HEAD_DIM), lambda b, h, q: (b, l, h, 0, 0, 0)),
                  pl.BlockSpec((None, None, None, PAST, DIFF_V), lambda b, h, q: (b, l, h, 0, 0)),
                  tab, tab, tab, vec(64), vec(64), vec(64), vec(64), vec(128)],
        out_specs=pl.BlockSpec((tq, 128), lambda b, h, q: (b * nq + q, h)),
        out_shape=jax.ShapeDtypeStruct((NS_TOK, DIFF_HEADS * DIFF_V), BF16),
        scratch_shapes=[pltpu.VMEM((DEC_SEQ + PAST, HEAD_DIM), BF16),
                        pltpu.VMEM((DEC_SEQ + PAST, HEAD_DIM), BF16),
                        pltpu.VMEM((DEC_SEQ + PAST, DIFF_V), BF16)],
        compiler_params=_cparams(("arbitrary", "arbitrary", "arbitrary")),
    )(p, p, p, cache_k, cache_v, cos, s_up, s_dn, lq1, lk1, lq2, lk2, g)


@functools.lru_cache(None)
def _dft_consts(L):
    n = 2 * L
    k = np.arange(L)
    ang = 2.0 * np.pi * ((k[:, None] * k[None, :]) % n) / n
    alt = (-1.0) ** k
    fa = np.cos(ang)
    fb = -np.sin(ang)
    fb[0, :] = alt
    wgt = np.full((L,), 2.0 / n)
    wgt[0] = 1.0 / n
    ga = fa * wgt[:, None]
    gb = fb * wgt[:, None]
    gb[0, :] = alt / n
    f = np.concatenate([fa, fb], axis=0)
    g = np.concatenate([ga.T, gb.T], axis=1)
    return np.asarray(f, dtype=BF16), np.asarray(g, dtype=BF16)


@functools.lru_cache(None)
def _filter_consts(L):
    f32 = np.float32
    t = np.linspace(0.0, 1.0, L, dtype=f32)[:, None]
    pos = np.arange(L, dtype=f32)[:, None]
    bands = np.linspace(1e-4, HY_BANDS - 1, HY_BANDS, dtype=f32)[None, :]
    ang = f32(2.0 * math.pi) * bands * pos / f32(L)
    z = np.zeros((L, 128), f32)
    z[:, 0:1] = t
    z[:, 1:1 + HY_BANDS] = np.cos(ang)
    z[:, 1 + HY_BANDS:HY_EMB] = -np.sin(ang)
    min_decay = math.log(1e-2) / 1.5
    max_decay = math.log(1e-2) / 0.3
    deltas = np.abs(np.linspace(min_decay, max_decay, HY_W, dtype=f32))
    decay = np.exp(-t * deltas[None, :]).astype(f32)
    return z, decay


def _spectra_kernel(z_ref, w1_ref, b1_ref, w2_ref, b2_ref, fr_ref, w3_ref, dec_ref, fa_ref, fb_ref,
                    sa_ref, sb_ref, filt_scr, *, L, kc):
    j = pl.program_id(0)

    @pl.when(j == 0)
    def _():
        fr = fr_ref[...]
        hdn = jnp.sin(fr * (_dot3(z_ref[...], w1_ref[...]) + b1_ref[...]))
        hdn = jnp.sin(fr * (_dot3(hdn, w2_ref[...]) + b2_ref[...]))
        dec = dec_ref[...]
        not_first = lax.broadcasted_iota(jnp.int32, (L, HY_W), 0) > 0
        for o in range(2):
            hf = _dot3(hdn, w3_ref[:, 512 * o:512 * o + 256]) * dec
            hb = jnp.where(not_first, _dot3(hdn, w3_ref[:, 512 * o + 256:512 * o + 512]) * dec, 0.0)
            nrm = (jnp.sum(jnp.abs(hf), axis=0, keepdims=True)
                   + jnp.sum(jnp.abs(hb), axis=0, keepdims=True))
            filt_scr[:, 512 * o:512 * o + 256] = (hf / nrm).astype(BF16)
            filt_scr[:, 512 * o + 256:512 * o + 512] = (hb / nrm).astype(BF16)

    ta = _dot(fa_ref[...], filt_scr[...])
    tb = _dot(fb_ref[...], filt_scr[...])
    first = (lax.broadcasted_iota(jnp.int32, (kc, HY_W), 0) + j * kc) == 0
    for o in range(2):
        af, ab = ta[:, 512 * o:512 * o + 256], ta[:, 512 * o + 256:512 * o + 512]
        bf, bb = tb[:, 512 * o:512 * o + 256], tb[:, 512 * o + 256:512 * o + 512]
        sa_ref[:, 256 * o:256 * o + 256] = af + ab
        sb_ref[:, 256 * o:256 * o + 256] = jnp.where(first, bf + bb, bf - bb)


def _hy_spectra(L, kc, w1p, b1, w2, b2, fr, w3):
    z, decay = _filter_consts(L)
    f = jnp.asarray(_dft_consts(L)[0])
    nj = L // kc
    full = lambda shape: pl.BlockSpec(shape, lambda j: tuple(0 for _ in shape))
    return pl.pallas_call(
        functools.partial(_spectra_kernel, L=L, kc=kc),
        grid=(nj,),
        in_specs=[full((L, 128)), full((128, HY_HID)), full((1, HY_HID)), full((HY_HID, HY_HID)),
                  full((1, HY_HID)), full((1, HY_HID)), full((HY_HID, 4 * HY_W)), full((L, HY_W)),
                  pl.BlockSpec((kc, L), lambda j: (j, 0)),
                  pl.BlockSpec((kc, L), lambda j: (j + nj, 0))],
        out_specs=[pl.BlockSpec((kc, 2 * HY_W), lambda j: (j, 0)),
                   pl.BlockSpec((kc, 2 * HY_W), lambda j: (j, 0))],
        out_shape=[jax.ShapeDtypeStruct((L, 2 * HY_W), F32)] * 2,
        scratch_shapes=[pltpu.VMEM((L, 4 * HY_W), BF16)],
        compiler_params=_cparams(("arbitrary",)),
    )(jnp.asarray(z), w1p, b1, w2, b2, fr, w3, jnp.asarray(decay), f, f)


def _hyconv_kernel(*refs, L, nb, kc, nj, row_blk0, staged):
    n_in = 1 if staged else nb
    (cw_ref, cb_ref, d_ref, fa_ref, fb_ref, ga_ref, gb_ref, sa_ref, sb_ref, o_ref,
     xin_scr, x_scr, g1_scr, g2_scr, y_scr) = refs[n_in:n_in + 15]
    bg = pl.program_id(0)
    o = pl.program_id(1)
    j = pl.program_id(2)

    @pl.when((o == 0) & (j == 0))
    def _():
        row = lax.broadcasted_iota(jnp.int32, (L, HY_W), 0)
        pieces = [(i, part) for i in range(nb) for part in range(3)]
        if staged:
            p_hbm, (stage, sem) = refs[0], refs[n_in + 15:]

            def piece_copy(k):
                i, part = pieces[k]
                row0 = pl.multiple_of((row_blk0 + bg * nb + i) * L, L)
                return pltpu.make_async_copy(
                    p_hbm.at[pl.ds(row0, L), C_HY + HY_W * part:C_HY + HY_W * (part + 1)],
                    stage.at[k % 2], sem.at[k % 2])

            piece_copy(0).start()
            piece_copy(1).start()
        for k, (i, part) in enumerate(pieces):
            cols = slice(HY_W * i, HY_W * (i + 1))
            dst = (x_scr, g1_scr, g2_scr)[part]
            pc = slice(HY_W * part, HY_W * (part + 1))
            if staged:
                piece_copy(k).wait()
                u = stage[k % 2]
            else:
                u = refs[i][:, pc]
            up = jnp.where(row == 0, 0.0, pltpu.roll(u, 1, axis=0))
            un = jnp.where(row == L - 1, 0.0, pltpu.roll(u, L - 1, axis=0))
            dst[:, cols] = (up * cw_ref[0:1, pc] + u * cw_ref[1:2, pc] + un * cw_ref[2:3, pc]
                            + cb_ref[:, pc])
            if part == 0:
                xin_scr[:, cols] = x_scr[:, cols].astype(BF16)
            if staged and k + 2 < len(pieces):
                piece_copy(k + 2).start()

    @pl.when(j == 0)
    def _():
        y_scr[...] = jnp.zeros_like(y_scr)

    xa = _dot(fa_ref[...], xin_scr[...])
    xb = _dot(fb_ref[...], xin_scr[...])
    sa = jnp.concatenate([sa_ref[...]] * nb, axis=1)
    sb = jnp.concatenate([sb_ref[...]] * nb, axis=1)
    first = (lax.broadcasted_iota(jnp.int32, (kc, nb * HY_W), 0) + j * kc) == 0
    ya = jnp.where(first, xa * sa, xa * sa - xb * sb)
    yb = jnp.where(first, xb * sb, xa * sb + xb * sa)
    y_scr[...] += _dot(ga_ref[...], ya.astype(BF16)) + _dot(gb_ref[...], yb.astype(BF16))

    @pl.when(j == nj - 1)
    def _():
        dvec = jnp.concatenate([d_ref[pl.ds(o, 1), :]] * nb, axis=1)
        y = y_scr[...] + x_scr[...] * dvec

        @pl.when(o == 0)
        def _():
            zz = g1_scr[...] * y
            x_scr[...] = zz
            xin_scr[...] = zz.astype(BF16)

        @pl.when(o == 1)
        def _():
            res = g2_scr[...] * y
            for i in range(nb):
                o_ref[L * i:L * (i + 1), :] = res[:, HY_W * i:HY_W * (i + 1)].astype(BF16)


def _hy_conv(p, L, nb, kc, row_blk0, n_seq, conv_w, conv_b, d, sa, sb):
    f, g = (jnp.asarray(a) for a in _dft_consts(L))
    nj = L // kc
    staged = nb * L > DEC_SEQ
    if staged:
        u_specs = [pl.BlockSpec(memory_space=pl.ANY)]
        u_scratch = [pltpu.VMEM((2, L, HY_W), F32), pltpu.SemaphoreType.DMA((2,))]
    else:
        u_specs = [pl.BlockSpec((L, 3 * HY_W), lambda bg, o, j, i=i: (row_blk0 + bg * nb + i, C_HY // 768))
                   for i in range(nb)]
        u_scratch = []
    small = lambda shape: pl.BlockSpec(shape, lambda bg, o, j: (0, 0))
    return pl.pallas_call(
        functools.partial(_hyconv_kernel, L=L, nb=nb, kc=kc, nj=nj, row_blk0=row_blk0, staged=staged),
        grid=(n_seq // nb, 2, nj),
        in_specs=u_specs + [
            small((3, 3 * HY_W)), small((1, 3 * HY_W)), small((2, HY_W)),
            pl.BlockSpec((kc, L), lambda bg, o, j: (j, 0)),
            pl.BlockSpec((kc, L), lambda bg, o, j: (j + nj, 0)),
            pl.BlockSpec((L, kc), lambda bg, o, j: (0, j)),
            pl.BlockSpec((L, kc), lambda bg, o, j: (0, j + nj)),
            pl.BlockSpec((kc, HY_W), lambda bg, o, j: (j, o)),
            pl.BlockSpec((kc, HY_W), lambda bg, o, j: (j, o))],
        out_specs=pl.BlockSpec((nb * L, HY_W), lambda bg, o, j: (bg, 0)),
        out_shape=jax.ShapeDtypeStruct((n_seq * L, HY_W), BF16),
        scratch_shapes=[pltpu.VMEM((L, nb * HY_W), BF16)] + [pltpu.VMEM((L, nb * HY_W), F32)] * 4
        + u_scratch,
        compiler_params=_cparams(("arbitrary", "arbitrary", "arbitrary")),
    )(*([p] * len(u_specs)), conv_w, conv_b, d, f, f, g, g, sa, sb)


def _outproj_kernel(xp_ref, xs_ref, mixp_ref, hyp_ref, nas_ref, ds_ref, hys_ref, g1_ref, w_ref, o_ref, w_scr):
    i = pl.program_id(0)

    @pl.when(i == 0)
    def _():
        w_scr[...] = w_ref[...].astype(BF16)

    w = lambda a, b: w_scr[a:b, :]

    @pl.when(i < NP_TOK // TM)
    def _():
        y = _dot(mixp_ref[...], w(0, 768)) + _dot(hyp_ref[...], w(768, D))
        o_ref[...] = xp_ref[...] + g1_ref[...] * y

    @pl.when(i >= NP_TOK // TM)
    def _():
        y = _dot(nas_ref[...], w(0, 256)) + _dot(ds_ref[...], w(256, 768)) + _dot(hys_ref[...], w(768, D))
        o_ref[...] = xs_ref[...] + g1_ref[...] * y


def _x_specs(xs_block0):
    npt = NP_TOK // TM
    return [pl.BlockSpec((TM, D), lambda i, *_: (jnp.minimum(i, npt - 1), 0)),
            pl.BlockSpec((TM, D), lambda i, *_: (jnp.maximum(i - npt, 0) + xs_block0, 0))]


def _outproj(xp, xs, xs_block0, mix_p, hy_p, na_s, d_s, hy_s, mod6, w_out, l):
    npt = NP_TOK // TM
    pidx = lambda i: (jnp.minimum(i, npt - 1), 0)
    sidx = lambda i: (jnp.maximum(i - npt, 0), 0)
    return pl.pallas_call(
        _outproj_kernel,
        grid=(N_TOK // TM,),
        in_specs=_x_specs(xs_block0) + [
                  pl.BlockSpec((TM, 768), pidx),
                  pl.BlockSpec((TM, HY_W), pidx),
                  pl.BlockSpec((TM, 256), sidx),
                  pl.BlockSpec((TM, 512), sidx),
                  pl.BlockSpec((TM, HY_W), sidx),
                  _mod_spec(2, TM),
                  pl.BlockSpec((None, D, D), lambda i: (l, 0, 0))],
        out_specs=pl.BlockSpec((TM, D), lambda i: (i, 0)),
        out_shape=jax.ShapeDtypeStruct((N_TOK, D), F32),
        scratch_shapes=[pltpu.VMEM((D, D), BF16)],
        compiler_params=_cparams(("arbitrary",)),
    )(xp, xs, mix_p, hy_p, na_s, d_s, hy_s, mod6, w_out)


def _ffn_kernel(x_ref, g_ref, sc_ref, sh_ref, gate_ref, wg_ref, wu_ref, wd_ref, o_ref, h_scr, *, nj):
    j = pl.program_id(1)

    @pl.when(j == 0)
    def _():
        h_scr[...] = _norm_mod(x_ref[...], g_ref[...], sc_ref[...], sh_ref[...]).astype(BF16)
        o_ref[...] = jnp.zeros_like(o_ref)

    h = h_scr[...]
    a = _silu(_dot(h, wg_ref[...].astype(BF16))) * _dot(h, wu_ref[...].astype(BF16))
    o_ref[...] += _dot(a.astype(BF16), wd_ref[...].astype(BF16))

    @pl.when(j == nj - 1)
    def _():
        o_ref[...] = x_ref[...] + gate_ref[...] * o_ref[...]


def _dense_ffn(x, g, mod6, wg, wu, wd, i_ffn):
    tf = 256
    tm = DEC_SEQ
    nj = FFN // tf
    return pl.pallas_call(
        functools.partial(_ffn_kernel, nj=nj),
        grid=(N_TOK // tm, nj),
        in_specs=[pl.BlockSpec((tm, D), lambda i, j: (i, 0)),
                  pl.BlockSpec((1, D), lambda i, j: (0, 0)),
                  _mod_spec(4, tm), _mod_spec(3, tm), _mod_spec(5, tm),
                  pl.BlockSpec((None, D, tf), lambda i, j: (i_ffn, 0, j)),
                  pl.BlockSpec((None, D, tf), lambda i, j: (i_ffn, 0, j)),
                  pl.BlockSpec((None, tf, D), lambda i, j: (i_ffn, j, 0))],
        out_specs=pl.BlockSpec((tm, D), lambda i, j: (i, 0)),
        out_shape=jax.ShapeDtypeStruct((N_TOK, D), F32),
        scratch_shapes=[pltpu.VMEM((tm, D), BF16)],
        compiler_params=_cparams(("arbitrary", "arbitrary")),
    )(x, g, mod6, mod6, mod6, wg, wu, wd)


def _router_kernel(x_ref, g_ref, sc_ref, sh_ref, wr_ref, h_ref, r_ref):
    h = _norm_mod(x_ref[...], g_ref[...], sc_ref[...], sh_ref[...])
    h_ref[...] = h.astype(BF16)
    lane = lax.broadcasted_iota(jnp.int32, (TM, 128), 1)
    lg = jnp.where(lane < N_EXP, _dot3(h, wr_ref[...]), -jnp.inf)
    m1 = jnp.max(lg, axis=-1, keepdims=True)
    i1 = jnp.min(jnp.where(lg == m1, lane, 128), axis=-1, keepdims=True)
    lg2 = jnp.where(lane == i1, -jnp.inf, lg)
    m2 = jnp.max(lg2, axis=-1, keepdims=True)
    i2 = jnp.min(jnp.where(lg2 == m2, lane, 128), axis=-1, keepdims=True)
    e = jnp.exp(m2 - m1)
    w1 = 1.0 / (1.0 + e)
    w2 = e / (1.0 + e)
    r_ref[...] = jnp.where(lane == 0, i1.astype(F32),
                           jnp.where(lane == 1, i2.astype(F32),
                                     jnp.where(lane == 2, w1, jnp.where(lane == 3, w2, 0.0))))


def _router(x, g, mod6, wr_pad):
    return pl.pallas_call(
        _router_kernel,
        grid=(N_TOK // TM,),
        in_specs=[pl.BlockSpec((TM, D), lambda i: (i, 0)),
                  pl.BlockSpec((1, D), lambda i: (0, 0)),
                  _mod_spec(4, TM), _mod_spec(3, TM),
                  pl.BlockSpec((D, 128), lambda i: (0, 0))],
        out_specs=[pl.BlockSpec((TM, D), lambda i: (i, 0)),
                   pl.BlockSpec((TM, 128), lambda i: (i, 0))],
        out_shape=[jax.ShapeDtypeStruct((N_TOK, D), BF16), jax.ShapeDtypeStruct((N_TOK, 128), F32)],
        compiler_params=_cparams(("arbitrary",)),
    )(x, g, mod6, mod6, wr_pad)


def _dispatch_kernel(blo_ref, bhi_ref, sexp_ref, h_hbm, rp_ref, o_ref, h_scr, sem):
    s = pl.program_id(0)

    @pl.when(s == 0)
    def _():
        cp = pltpu.make_async_copy(h_hbm, h_scr, sem)
        cp.start()
        cp.wait()

    e = sexp_ref[s]
    rows = s * MOE_TS + lax.broadcasted_iota(jnp.int32, (MOE_TS, MOE_DTB), 0)

    def sel(b):
        return (rows == rp_ref[e, pl.ds(b, 1), :]).astype(BF16)

    is_empty = blo_ref[s] > bhi_ref[s]

    @pl.when(is_empty)
    def _():
        o_ref[...] = jnp.zeros_like(o_ref)

    @pl.when(jnp.logical_not(is_empty))
    def _():
        b0 = jnp.minimum(blo_ref[s], N_TOK // MOE_DTB - MOE_DWIN)
        hwin = h_scr[pl.ds(pl.multiple_of(b0 * MOE_DTB, MOE_DTB), MOE_DWIN * MOE_DTB), :]
        acc = _dot(jnp.concatenate([sel(b0 + k) for k in range(MOE_DWIN)], axis=1), hwin)

        def body(b, acc):
            hb = h_scr[pl.ds(pl.multiple_of(b * MOE_DTB, MOE_DTB), MOE_DTB), :]
            return acc + _dot(sel(b), hb)

        acc = lax.fori_loop(b0 + MOE_DWIN, bhi_ref[s] + 1, body, acc)
        o_ref[...] = acc.astype(BF16)


def _dispatch(blo, bhi, sexp, h, rp_t):
    return pl.pallas_call(
        _dispatch_kernel,
        grid_spec=pltpu.PrefetchScalarGridSpec(
            num_scalar_prefetch=3,
            grid=(MOE_ROWS // MOE_TS,),
            in_specs=[pl.BlockSpec(memory_space=pl.ANY),
                      pl.BlockSpec((N_EXP, N_TOK // MOE_DTB, MOE_DTB), lambda s, *_: (0, 0, 0))],
            out_specs=pl.BlockSpec((MOE_TS, D), lambda s, *_: (s, 0)),
            scratch_shapes=[pltpu.VMEM((N_TOK, D), BF16), pltpu.SemaphoreType.DMA(())]),
        out_shape=jax.ShapeDtypeStruct((MOE_ROWS, D), BF16),
        compiler_params=_cparams(("arbitrary",)),
    )(blo, bhi, sexp, h, rp_t)


def _experts_kernel(te_ref, used_ref, rows_ref, xs_ref, wg_ref, wu_ref, wd_ref, o_ref, acc_scr, *, nj):
    i = pl.program_id(0)
    j = pl.program_id(1)
    nrows = rows_ref[i]

    @pl.when(j == 0)
    def _():
        acc_scr[...] = jnp.zeros_like(acc_scr)

    for sz in range(MOE_TQ, MOE_TM + 1, MOE_TQ):
        @pl.when(nrows == sz)
        def _():
            xb = xs_ref[0:sz, :]
            a = _silu(_dot(xb, wg_ref[...].astype(BF16))) * _dot(xb, wu_ref[...].astype(BF16))
            acc_scr[0:sz, :] += _dot(a.astype(BF16), wd_ref[...].astype(BF16))

    @pl.when(j == nj - 1)
    def _():
        o_ref[...] = acc_scr[...].astype(BF16)


def _experts(tile_expert, used, tile_rows, xs, wg, wu, wd, i_moe):
    nj = EXP_DIM // MOE_TF

    def tile(i, used):
        return jnp.minimum(i, used[0] - 1)

    def chunk(i, j, used):
        return jnp.where(i < used[0], j, nj - 1)

    return pl.pallas_call(
        functools.partial(_experts_kernel, nj=nj),
        grid_spec=pltpu.PrefetchScalarGridSpec(
            num_scalar_prefetch=3,
            grid=(MOE_YROWS // MOE_TM, nj),
            in_specs=[pl.BlockSpec((MOE_TM, D), lambda i, j, te, used, tr: (tile(i, used), 0)),
                      pl.BlockSpec((None, None, D, MOE_TF),
                                   lambda i, j, te, used, tr: (i_moe, te[tile(i, used)], 0, chunk(i, j, used))),
                      pl.BlockSpec((None, None, D, MOE_TF),
                                   lambda i, j, te, used, tr: (i_moe, te[tile(i, used)], 0, chunk(i, j, used))),
                      pl.BlockSpec((None, None, MOE_TF, D),
                                   lambda i, j, te, used, tr: (i_moe, te[tile(i, used)], chunk(i, j, used), 0))],
            out_specs=pl.BlockSpec((MOE_TM, D), lambda i, j, te, used, tr: (i, 0)),
            scratch_shapes=[pltpu.VMEM((MOE_TM, D), F32)]),
        out_shape=jax.ShapeDtypeStruct((MOE_YROWS, D), BF16),
        compiler_params=_cparams(("arbitrary", "arbitrary")),
    )(tile_expert, used, tile_rows, xs, wg, wu, wd)


def _combine_kernel(ws_ref, kind_ref, x_ref, rp_ref, comb_ref, gate_ref, fg_ref, ys_hbm, op_ref, os_ref,
                    win_scr, y_scr, sem):
    b = pl.program_id(0)
    nb = pl.num_programs(0)
    slot = b % 2

    def win_copy(blk, sl, e):
        start = pl.multiple_of(ws_ref[blk * N_EXP + e], MOE_ALIGN)
        return pltpu.make_async_copy(ys_hbm.at[pl.ds(start, MOE_WIN)], win_scr.at[sl, e], sem.at[sl, e])

    @pl.when(b == 0)
    def _():
        for e in range(N_EXP):
            win_copy(0, 0, e).start()

    @pl.when(b + 1 < nb)
    def _():
        for e in range(N_EXP):
            win_copy(b + 1, 1 - slot, e).start()

    col = lax.broadcasted_iota(jnp.int32, (MOE_TB, MOE_WIN), 1)
    y_scr[...] = jnp.zeros_like(y_scr)
    for e in range(N_EXP):
        win_copy(b, slot, e).wait()
        rel = rp_ref[:, e:e + 1] - ws_ref[b * N_EXP + e]
        cw = comb_ref[:, e:e + 1]
        kind = kind_ref[b * N_EXP + e]

        for k, wn in enumerate(MOE_WINS):
            @pl.when(kind == k)
            def _():
                sel = (rel == col[:, :wn]).astype(BF16) if wn == MOE_WIN else (
                    rel == lax.broadcasted_iota(jnp.int32, (MOE_TB, wn), 1)).astype(BF16)
                y_scr[...] += cw * _dot(sel, win_scr[slot, e, 0:wn, :])

    x = x_ref[...] + gate_ref[...] * y_scr[...]
    x = (x * lax.rsqrt(jnp.mean(x * x, axis=-1, keepdims=True) + EPS)) * fg_ref[...]

    @pl.when(b < NP_TOK // MOE_TB)
    def _():
        op_ref[...] = x

    @pl.when(b >= NP_TOK // MOE_TB)
    def _():
        os_ref[...] = x


def _combine(ws, kind, x, rp8, comb, mod6, final_g, ys):
    npb = NP_TOK // MOE_TB
    return pl.pallas_call(
        _combine_kernel,
        grid_spec=pltpu.PrefetchScalarGridSpec(
            num_scalar_prefetch=2,
            grid=(N_TOK // MOE_TB,),
            in_specs=[pl.BlockSpec((MOE_TB, D), lambda b, *_: (b, 0)),
                      pl.BlockSpec((MOE_TB, N_EXP), lambda b, *_: (b, 0)),
                      pl.BlockSpec((MOE_TB, N_EXP), lambda b, *_: (b, 0)),
                      _mod_spec(5, MOE_TB),
                      pl.BlockSpec((1, D), lambda b, *_: (0, 0)),
                      pl.BlockSpec(memory_space=pl.ANY)],
            out_specs=[pl.BlockSpec((MOE_TB, D), lambda b, *_: (jnp.minimum(b, npb - 1), 0)),
                       pl.BlockSpec((MOE_TB, D), lambda b, *_: (jnp.maximum(b - npb, 0), 0))],
            scratch_shapes=[pltpu.VMEM((2, N_EXP, MOE_WIN, D), BF16),
                            pltpu.VMEM((MOE_TB, D), F32),
                            pltpu.SemaphoreType.DMA((2, N_EXP))]),
        out_shape=[jax.ShapeDtypeStruct((NP_TOK, D), F32), jax.ShapeDtypeStruct((NS_TOK, D), F32)],
        compiler_params=_cparams(("arbitrary",)),
    )(ws, kind, x, rp8, comb, mod6, final_g, ys)


def _moe(x, g, mod6, router, wg, wu, wd, i_moe, final_g):
    wr_pad = jnp.pad(router, ((0, 0), (0, 128 - N_EXP)))
    h, r = _router(x, g, mod6, wr_pad)

    i32 = jnp.int32
    i12 = r[:, 0:2].astype(i32)
    earange = jnp.arange(N_EXP, dtype=i32)
    hit1 = i12[:, 0:1] == earange[None, :]
    hit2 = i12[:, 1:2] == earange[None, :]
    comb = jnp.where(hit1, r[:, 2:3], 0.0) + jnp.where(hit2, r[:, 3:4], 0.0)
    mask = (hit1 | hit2).astype(i32)
    csum = jnp.cumsum(mask, axis=0)
    counts = csum[-1]
    padded = ((counts + MOE_TM - 1) // MOE_TM) * MOE_TM
    ends = jnp.cumsum(padded)
    starts = ends - padded
    rp8 = jnp.where(mask > 0, starts[None, :] + csum - 1, -1).astype(i32)
    n_tiles = MOE_ROWS // MOE_TM
    tile_expert = jnp.minimum(
        jnp.searchsorted(ends, jnp.arange(n_tiles, dtype=i32) * MOE_TM, side="right"),
        N_EXP - 1).astype(i32)
    used = (ends[-1:] // MOE_TM).astype(i32)
    tile_row0 = jnp.arange(MOE_YROWS // MOE_TM, dtype=i32) * MOE_TM
    te_all = jnp.minimum(jnp.searchsorted(ends, tile_row0, side="right"), N_EXP - 1)
    valid = jnp.clip(counts[te_all] - (tile_row0 - starts[te_all]), 0, MOE_TM)
    valid = jnp.where(tile_row0 < ends[-1], valid, 0)
    tile_rows = (((valid + MOE_TQ - 1) // MOE_TQ) * MOE_TQ).astype(i32)

    sub_row0 = jnp.arange(MOE_ROWS // MOE_TS, dtype=i32) * MOE_TS
    sexp = tile_expert[sub_row0 // MOE_TM]
    qlo = sub_row0 - starts[sexp]
    qend = jnp.minimum(qlo + MOE_TS, counts[sexp])
    cbe = csum[MOE_DTB - 1::MOE_DTB, :].T[sexp]
    blo = jnp.sum((cbe <= qlo[:, None]).astype(i32), axis=1)
    bhi = jnp.minimum(jnp.sum((cbe < qend[:, None]).astype(i32), axis=1), N_TOK // MOE_DTB - 1)
    empty = qend <= qlo
    blo = jnp.where(empty, 1, blo).astype(i32)
    bhi = jnp.where(empty, 0, bhi).astype(i32)

    cb = csum[MOE_TB - 1::MOE_TB, :]
    cprev = jnp.concatenate([jnp.zeros((1, N_EXP), i32), cb[:-1]], axis=0)
    ws = (((starts[None, :] + cprev) // MOE_ALIGN) * MOE_ALIGN).reshape(-1).astype(i32)
    n_be = cb - cprev
    kind = sum((n_be > wn - MOE_ALIGN).astype(i32) for wn in MOE_WINS[:-1])
    kind = jnp.where(n_be == 0, len(MOE_WINS), kind).reshape(-1).astype(i32)

    rp_t = rp8.T.reshape(N_EXP, N_TOK // MOE_DTB, MOE_DTB)
    xs = _dispatch(blo, bhi, sexp, h, rp_t)
    ys = _experts(tile_expert, used, tile_rows, xs, wg, wu, wd, i_moe)
    return _combine(ws, kind, x, rp8, comb, mod6, final_g, ys)


assert DEPTH == 2

def kernel(x_prompt, x_sample, cache_na_k, cache_na_v, cache_diff_k, cache_diff_v, c, c_ctx, w_in, w_out, ada_w, ada_b, norm_mix_g, norm_ffn_g, na_rpb, diff_lq1, diff_lk1, diff_lq2, diff_lk2, diff_subln_g, hy_conv_w, hy_conv_b, hy_d, hy_f_w1, hy_f_b1, hy_f_w2, hy_f_b2, hy_f_freq, hy_f_w3, ffn_w_gate, ffn_w_up, ffn_w_down, moe_router, moe_w_gate, moe_w_up, moe_w_down, final_norm_g):
    xparts = (x_prompt.reshape(NP_TOK, D), x_sample.reshape(NS_TOK, D), 0)
    cond8 = jnp.concatenate([c_ctx[None, :], c, jnp.zeros((5, D), F32)], axis=0)
    mods = _modulation(cond8, ada_w, ada_b)
    final_g = final_norm_g.reshape(1, D)
    bias = _na_bias(na_rpb)

    leaves = [jnp.zeros((BATCH, DEPTH, NA_HEADS, SEQ, HEAD_DIM), F32),
              jnp.zeros((BATCH, DEPTH, NA_HEADS, SEQ, HEAD_DIM), F32),
              jnp.zeros((BATCH, DEPTH, DIFF_HEADS, 2, SEQ, HEAD_DIM), F32),
              jnp.zeros((BATCH, DEPTH, DIFF_HEADS, SEQ, DIFF_V), F32)]
    for l in range(DEPTH):
        lam_init = 0.8 - 0.6 * math.exp(-0.3 * l)
        mod6 = mods[l].reshape(8, 6, D).transpose(1, 0, 2).reshape(6, 8, 1, D)
        row = lambda a: a[l].reshape(1, -1)
        lq1, lk1, lq2, lk2, subg = row(diff_lq1), row(diff_lk1), row(diff_lq2), row(diff_lk2), row(diff_subln_g)

        p, *leaves = _inproj(*xparts, row(norm_mix_g), mod6, w_in, l, leaves)

        mix_p = _prompt_attn(p, lq1, lk1, lq2, lk2, subg, lam_init)
        na_s = _na_attn(p, cache_na_k, cache_na_v, bias, l)
        d_s = _sdiff_attn(p, cache_diff_k, cache_diff_v, lq1, lk1, lq2, lk2, subg, lam_init, l)

        w1p = jnp.pad(hy_f_w1[l], ((0, 128 - HY_EMB), (0, 0)))
        fargs = (w1p, row(hy_f_b1), hy_f_w2[l], row(hy_f_b2), row(hy_f_freq), hy_f_w3[l])
        cargs = (hy_conv_w[l], row(hy_conv_b), hy_d[l])
        sa_p, sb_p = _hy_spectra(SEQ, SEQ, *fargs)
        hy_p = _hy_conv(p, SEQ, 4, SEQ, 0, BATCH, *cargs, sa_p, sb_p)
        sa_s, sb_s = _hy_spectra(DEC_SEQ, 512, *fargs)
        hy_s = _hy_conv(p, DEC_SEQ, 2, 512, NP_TOK // DEC_SEQ, DEC_BATCH, *cargs, sa_s, sb_s)

        x = _outproj(*xparts, mix_p, hy_p, na_s, d_s, hy_s, mod6, w_out, l)

        if l == 0:
            x = _dense_ffn(x, row(norm_ffn_g), mod6, ffn_w_gate, ffn_w_up, ffn_w_down, 0)
            xparts = (x, x, NP_TOK // TM)
        else:
            yp, ys = _moe(x, row(norm_ffn_g), mod6, moe_router[0], moe_w_gate, moe_w_up, moe_w_down,
                          0, final_g)

    return (yp.reshape(BATCH, SEQ, D), ys.reshape(DEC_BATCH, DEC_SEQ, D), *leaves)
```

```python
import functools
import math

import numpy as np
import jax
import jax.numpy as jnp
from jax import lax
from jax.experimental import pallas as pl
from jax.experimental.pallas import tpu as pltpu

F32 = jnp.float32
BF16 = jnp.bfloat16

D = 1024
BATCH, SEQ = 32, 256
DEC_BATCH, DEC_SEQ = 2, 2048
DEPTH = 2
PAST = 512
GRID_W = 64
GRID_ROWS = DEC_SEQ // GRID_W
HEAD_DIM = 64
NA_HEADS = 4
DIFF_HEADS = 4
DIFF_V = 128
WIN_ROWS, WIN_COLS = 8, 16
HY_W = 256
HY_EMB = 33
HY_BANDS = 16
HY_HID = 64
PROJ = 3072
FFN = 2816
N_EXP = 8
EXP_DIM = 3584
EPS = 1e-6
ROPE_BASE = 10000.0

NP_TOK = BATCH * SEQ
NS_TOK = DEC_BATCH * DEC_SEQ
N_TOK = NP_TOK + NS_TOK

C_NAQ, C_NAK, C_NAV = 0, 256, 512
C_DQ, C_DK, C_DV = 768, 1280, 1792
C_HY = 2304

TM = 1024
VMEM_LIMIT = 56 * 1024 * 1024

MOE_TM = 1024
MOE_TF = 512
MOE_TQ = 256
MOE_ROWS = 2 * N_TOK + N_EXP * MOE_TM
MOE_TS = 256
MOE_DTB = 512
MOE_DWIN = 3
MOE_TB = 512
MOE_ALIGN = 16
MOE_WIN = MOE_TB + MOE_ALIGN
MOE_WINS = (MOE_TB // 4 + MOE_ALIGN, MOE_TB // 2 + MOE_ALIGN, MOE_WIN)
MOE_YROWS = MOE_ROWS + MOE_TM


def _cparams(sem):
    return pltpu.CompilerParams(dimension_semantics=sem, vmem_limit_bytes=VMEM_LIMIT)


def _dot(a, b):
    return jnp.dot(a, b, preferred_element_type=F32)


def _dot_nt(a, b):
    return lax.dot_general(a, b, (((1,), (1,)), ((), ())), preferred_element_type=F32)


def _split(a):
    hi = a.astype(BF16)
    lo = (a - hi.astype(F32)).astype(BF16)
    return hi, lo


def _dot3(a, b):
    ah, al = _split(a)
    bh, bl = _split(b)
    return _dot(ah, bh) + (_dot(ah, bl) + _dot(al, bh))


def _silu(x):
    return x / (1.0 + jnp.exp(-x))


def _mod_row(i, tm):
    t = i * tm
    return jnp.where(t < NP_TOK, 0, 1 + (t - NP_TOK) // DEC_SEQ)


def _mod_spec(k, tm):
    return pl.BlockSpec((None, None, 1, D), lambda i, *_: (k, _mod_row(i, tm), 0, 0))


def _norm_mod(x, g, sc, sh):
    y = x * lax.rsqrt(jnp.mean(x * x, axis=-1, keepdims=True) + EPS)
    return (y * g) * (1.0 + sc) + sh


def _mod_kernel(c_ref, w_ref, b_ref, o_ref):
    o_ref[...] = _dot3(_silu(c_ref[...]), w_ref[...]) + b_ref[...]


def _modulation(cond8, ada_w, ada_b):
    tn = 1536
    return pl.pallas_call(
        _mod_kernel,
        grid=(DEPTH, 6 * D // tn),
        in_specs=[pl.BlockSpec((8, D), lambda l, j: (0, 0)),
                  pl.BlockSpec((None, D, tn), lambda l, j: (l, 0, j)),
                  pl.BlockSpec((None, 1, tn), lambda l, j: (l, 0, j))],
        out_specs=pl.BlockSpec((None, 8, tn), lambda l, j: (l, 0, j)),
        out_shape=jax.ShapeDtypeStruct((DEPTH, 8, 6 * D), F32),
        compiler_params=_cparams(("arbitrary", "arbitrary")),
    )(cond8, ada_w, ada_b.reshape(DEPTH, 1, 6 * D))


def _inproj_kernel(xp_ref, xs_ref, g_ref, sc_ref, sh_ref, w_hbm, nak_in, nav_in, dk_in, dv_in,
                   o_ref, nak_ref, nav_ref, dk_ref, dv_ref, h_scr, w_scr, stage, acc_scr, sem, *, l, tn):
    del nak_in, nav_in, dk_in, dv_in
    i = pl.program_id(0)
    j = pl.program_id(1)

    @pl.when((i == 0) & (j == 0))
    def _():
        for c in range(PROJ // tn):
            cp = pltpu.make_async_copy(w_hbm.at[l, :, c * tn:(c + 1) * tn], stage, sem)
            cp.start()
            cp.wait()
            w_scr[c] = stage[...].astype(BF16)

    is_ctx = i < NP_TOK // TM

    @pl.when((j == 0) & is_ctx)
    def _():
        h_scr[...] = _norm_mod(xp_ref[...], g_ref[...], sc_ref[...], sh_ref[...]).astype(BF16)

    @pl.when((j == 0) & jnp.logical_not(is_ctx))
    def _():
        h_scr[...] = _norm_mod(xs_ref[...], g_ref[...], sc_ref[...], sh_ref[...]).astype(BF16)

    acc_scr[...] = _dot(h_scr[...], w_scr[j])
    o_ref[...] = acc_scr[...].astype(BF16)


    def rows(bb):
        return slice(bb * SEQ, (bb + 1) * SEQ)

    def copy_heads(dst_ref, c0, width, heads, sub=None):
        for bb in range(TM // SEQ):
            for h in heads:
                if sub is None:
                    dst_ref[bb, h] = acc_scr[rows(bb), c0(h):c0(h) + width]
                else:
                    for s in range(2):
                        dst_ref[bb, h, s] = acc_scr[rows(bb), c0(h) + s * width:c0(h) + (s + 1) * width]

    @pl.when(is_ctx & (j == 0))
    def _():
        copy_heads(nak_ref, lambda h: C_NAK + HEAD_DIM * h, HEAD_DIM, range(NA_HEADS))
        copy_heads(nav_ref, lambda h: C_NAV + HEAD_DIM * h, HEAD_DIM, range(NA_HEADS))

    @pl.when(is_ctx & (j == 1))
    def _():
        copy_heads(dk_ref, lambda h: C_DK - tn + 2 * HEAD_DIM * h, HEAD_DIM, range(0, 2), sub=True)

    @pl.when(is_ctx & (j == 2))
    def _():
        copy_heads(dk_ref, lambda h: C_DK - 2 * tn + 2 * HEAD_DIM * h, HEAD_DIM, range(2, 4), sub=True)
        copy_heads(dv_ref, lambda h: C_DV - 2 * tn + DIFF_V * h, DIFF_V, range(DIFF_HEADS))


def _inproj(xp, xs, xs_block0, g, mod6, w_in, l, leaves):
    tn = 768
    nb = TM // SEQ
    ctx = lambda i: jnp.minimum(i, NP_TOK // TM - 1)
    leaf_specs = [pl.BlockSpec((nb, None, NA_HEADS, SEQ, HEAD_DIM), lambda i, j: (ctx(i), l, 0, 0, 0)),
                  pl.BlockSpec((nb, None, NA_HEADS, SEQ, HEAD_DIM), lambda i, j: (ctx(i), l, 0, 0, 0)),
                  pl.BlockSpec((nb, None, DIFF_HEADS, 2, SEQ, HEAD_DIM), lambda i, j: (ctx(i), l, 0, 0, 0, 0)),
                  pl.BlockSpec((nb, None, DIFF_HEADS, SEQ, DIFF_V), lambda i, j: (ctx(i), l, 0, 0, 0))]
    return pl.pallas_call(
        functools.partial(_inproj_kernel, l=l, tn=tn),
        grid=(N_TOK // TM, PROJ // tn),
        in_specs=_x_specs(xs_block0) + [
                  pl.BlockSpec((1, D), lambda i, j: (0, 0)),
                  _mod_spec(1, TM), _mod_spec(0, TM),
                  pl.BlockSpec(memory_space=pl.ANY)] + [pl.BlockSpec(memory_space=pl.ANY)] * 4,
        out_specs=[pl.BlockSpec((TM, tn), lambda i, j: (i, j))] + leaf_specs,
        out_shape=[jax.ShapeDtypeStruct((N_TOK, PROJ), BF16)]
        + [jax.ShapeDtypeStruct(a.shape, a.dtype) for a in leaves],
        input_output_aliases={6: 1, 7: 2, 8: 3, 9: 4},
        scratch_shapes=[pltpu.VMEM((TM, D), BF16), pltpu.VMEM((PROJ // tn, D, tn), BF16),
                        pltpu.VMEM((D, tn), F32), pltpu.VMEM((TM, tn), F32), pltpu.SemaphoreType.DMA(())],
        compiler_params=_cparams(("arbitrary", "arbitrary")),
    )(xp, xs, g, mod6, mod6, w_in, *leaves)


def _lam(lq1, lk1, lq2, lk2, lam_init):
    return (jnp.exp(jnp.sum(lq1[...] * lk1[...], axis=-1, keepdims=True))
            - jnp.exp(jnp.sum(lq2[...] * lk2[...], axis=-1, keepdims=True)) + lam_init)


def _softmax_parts(s):
    m = jnp.max(s, axis=-1, keepdims=True)
    e = jnp.exp(s - m)
    return e, jnp.sum(e, axis=-1, keepdims=True)


QK_SCALE = HEAD_DIM ** -0.5


def _diff_head(q1, q2, k1, k2, v, lam, g, lam_init):
    e1, l1 = _softmax_parts(_dot_nt(q1, k1))
    e2, l2 = _softmax_parts(_dot_nt(q2, k2))
    a = e1 - (lam * l1 * (1.0 / l2)) * e2
    o = _dot(a.astype(BF16), v) * (1.0 / l1)
    o = o * lax.rsqrt(jnp.mean(o * o, axis=-1, keepdims=True) + EPS)
    return (o * g) * (1.0 - lam_init)


def _prompt_attn_kernel(pa_ref, pb_ref, pc_ref, lq1, lk1, lq2, lk2, g_ref, o_ref, *, lam_init):
    lam = _lam(lq1, lk1, lq2, lk2, lam_init)
    g = g_ref[...]

    def col(c0, w, scale=None):
        ref = (pa_ref, pb_ref, pc_ref)[c0 // 768]
        o = c0 % 768
        a = ref[:, o:o + w]
        return (a if scale is None else a * scale).astype(BF16)

    for h in range(NA_HEADS):
        q = col(C_NAQ + 64 * h, 64, QK_SCALE)
        k = col(C_NAK + 64 * h, 64)
        v = col(C_NAV + 64 * h, 64)
        e, l = _softmax_parts(_dot_nt(q, k))
        o = _dot(e.astype(BF16), v) * (1.0 / l)
        o_ref[:, 64 * h:64 * h + 64] = o.astype(BF16)
    for h in range(DIFF_HEADS):
        q1 = col(C_DQ + 128 * h, 64, QK_SCALE)
        q2 = col(C_DQ + 128 * h + 64, 64, QK_SCALE)
        k1 = col(C_DK + 128 * h, 64)
        k2 = col(C_DK + 128 * h + 64, 64)
        v = col(C_DV + 128 * h, 128)
        o = _diff_head(q1, q2, k1, k2, v, lam, g, lam_init)
        o_ref[:, 256 + 128 * h:384 + 128 * h] = o.astype(BF16)


def _prompt_attn(p, lq1, lk1, lq2, lk2, g, lam_init):
    vec = lambda n: pl.BlockSpec((1, n), lambda b: (0, 0))
    return pl.pallas_call(
        functools.partial(_prompt_attn_kernel, lam_init=lam_init),
        grid=(BATCH,),
        in_specs=[pl.BlockSpec((SEQ, 768), lambda b: (b, 0)),
                  pl.BlockSpec((SEQ, 768), lambda b: (b, 1)),
                  pl.BlockSpec((SEQ, 768), lambda b: (b, 2)),
                  vec(64), vec(64), vec(64), vec(64), vec(128)],
        out_specs=pl.BlockSpec((SEQ, 768), lambda b: (b, 0)),
        out_shape=jax.ShapeDtypeStruct((NP_TOK, 768), BF16),
        compiler_params=_cparams(("arbitrary",)),
    )(p, p, p, lq1, lk1, lq2, lk2, g)


def _bias_kernel(rpb_ref, o_ref):
    lh = pl.program_id(0)
    qc = lax.broadcasted_iota(jnp.int32, (GRID_W, GRID_W), 0)
    kc = lax.broadcasted_iota(jnp.int32, (GRID_W, GRID_W), 1)
    delta = jnp.clip(kc - qc + (WIN_COLS - 1), 0, 2 * WIN_COLS - 2)
    qs = jnp.clip(qc - WIN_COLS // 2, 0, GRID_W - WIN_COLS)
    in_win = (kc >= qs) & (kc < qs + WIN_COLS)
    for dr in range(2 * WIN_ROWS - 1):
        base = (lh * (2 * WIN_ROWS - 1) + dr) * (2 * WIN_COLS - 1)
        acc = jnp.zeros((GRID_W, GRID_W), F32)
        for d in range(2 * WIN_COLS - 1):
            acc = jnp.where(delta == d, rpb_ref[base + d], acc)
        piece = jnp.where(in_win, acc, -jnp.inf)
        for case in range(WIN_ROWS):
            i = dr + case - (WIN_ROWS - 1)
            if 0 <= i < WIN_ROWS:
                o_ref[case, :, i * GRID_W:(i + 1) * GRID_W] = piece


def _na_bias(rpb):
    return pl.pallas_call(
        _bias_kernel,
        grid=(DEPTH * NA_HEADS,),
        in_specs=[pl.BlockSpec(memory_space=pltpu.SMEM)],
        out_specs=pl.BlockSpec((None, WIN_ROWS, GRID_W, WIN_ROWS * GRID_W), lambda lh: (lh, 0, 0, 0)),
        out_shape=jax.ShapeDtypeStruct((DEPTH * NA_HEADS, WIN_ROWS, GRID_W, WIN_ROWS * GRID_W), F32),
        compiler_params=_cparams(("arbitrary",)),
    )(rpb.reshape(-1))


def _na_kernel(q_ref, kv_ref, kc_ref, vc_ref, bias_ref, o_ref):
    nloc = WIN_ROWS * GRID_W
    for rr in range(NA_RB):
        r = pl.program_id(1) * NA_RB + rr
        start = jnp.clip(r - WIN_ROWS // 2, 0, GRID_ROWS - WIN_ROWS)
        case = r - start
        row0 = pl.multiple_of(start * GRID_W, GRID_W)
        qrows = slice(rr * GRID_W, (rr + 1) * GRID_W)
        for h in range(NA_HEADS):
            q = (q_ref[qrows, C_NAQ + 64 * h:C_NAQ + 64 * h + 64] * QK_SCALE).astype(BF16)
            k = kv_ref[pl.ds(row0, nloc), C_NAK + 64 * h:C_NAK + 64 * h + 64].astype(BF16)
            v = kv_ref[pl.ds(row0, nloc), C_NAV + 64 * h:C_NAV + 64 * h + 64].astype(BF16)
            s_loc = _dot_nt(q, k) + bias_ref[h, pl.ds(case, 1)][0]
            s_ctx = _dot_nt(q, kc_ref[h].astype(BF16))
            m = jnp.maximum(jnp.max(s_loc, axis=-1, keepdims=True), jnp.max(s_ctx, axis=-1, keepdims=True))
            e_loc = jnp.exp(s_loc - m)
            e_ctx = jnp.exp(s_ctx - m)
            l = jnp.sum(e_loc, axis=-1, keepdims=True) + jnp.sum(e_ctx, axis=-1, keepdims=True)
            o = (_dot(e_loc.astype(BF16), v) + _dot(e_ctx.astype(BF16), vc_ref[h].astype(BF16))) * (1.0 / l)
            o_ref[qrows, 64 * h:64 * h + 64] = o.astype(BF16)


NA_RB = 4


def _na_attn(p, cache_k, cache_v, bias, l):
    qblk0 = NP_TOK // (NA_RB * GRID_W)
    kvblk0 = NP_TOK // DEC_SEQ
    nrg = GRID_ROWS // NA_RB
    return pl.pallas_call(
        _na_kernel,
        grid=(DEC_BATCH, nrg),
        in_specs=[pl.BlockSpec((NA_RB * GRID_W, 768), lambda b, r: (qblk0 + b * nrg + r, 0)),
                  pl.BlockSpec((DEC_SEQ, 768), lambda b, r: (kvblk0 + b, 0)),
                  pl.BlockSpec((None, None, NA_HEADS, PAST, HEAD_DIM), lambda b, r: (b, l, 0, 0, 0)),
                  pl.BlockSpec((None, None, NA_HEADS, PAST, HEAD_DIM), lambda b, r: (b, l, 0, 0, 0)),
                  pl.BlockSpec((NA_HEADS, WIN_ROWS, GRID_W, WIN_ROWS * GRID_W), lambda b, r: (l, 0, 0, 0))],
        out_specs=pl.BlockSpec((NA_RB * GRID_W, 256), lambda b, r: (b * nrg + r, 0)),
        out_shape=jax.ShapeDtypeStruct((NS_TOK, 256), BF16),
        compiler_params=_cparams(("arbitrary", "arbitrary")),
    )(p, p, cache_k, cache_v, bias)


@functools.lru_cache(None)
def _rope_tables():
    t = np.arange(DEC_SEQ)
    lane = np.arange(128)
    dd = lane % HEAD_DIM
    pos = np.where(dd[None, :] < 32, (t // GRID_W)[:, None], (t % GRID_W)[:, None]).astype(np.float64)
    inv = ROPE_BASE ** (-(dd % 16).astype(np.float64) * 2.0 / 32.0)
    ang = pos * inv[None, :]
    first = (dd % 32) < 16
    cos = np.cos(ang)
    s_up = np.where(first[None, :], -np.sin(ang), 0.0)
    s_dn = np.where(first[None, :], 0.0, np.sin(ang))
    return tuple(np.asarray(a, np.float32) for a in (cos, s_up, s_dn))


def _rope(x, cos, s_up, s_dn):
    return x * cos + pltpu.roll(x, 112, axis=1) * s_up + pltpu.roll(x, 16, axis=1) * s_dn


def _sdiff_kernel(q_ref, k_ref, v_ref, ck_ref, cv_ref, cos_ref, sup_ref, sdn_ref,
                  lq1, lk1, lq2, lk2, g_ref, o_ref, k1_scr, k2_scr, v_scr, *, lam_init, tq):
    qb = pl.program_id(2)

    @pl.when(qb == 0)
    def _():
        kr = _rope(k_ref[...].astype(F32), cos_ref[...], sup_ref[...], sdn_ref[...])
        k1_scr[0:DEC_SEQ, :] = kr[:, :64].astype(BF16)
        k2_scr[0:DEC_SEQ, :] = kr[:, 64:].astype(BF16)
        k1_scr[DEC_SEQ:, :] = ck_ref[0].astype(BF16)
        k2_scr[DEC_SEQ:, :] = ck_ref[1].astype(BF16)
        v_scr[0:DEC_SEQ, :] = v_ref[...].astype(BF16)
        v_scr[DEC_SEQ:, :] = cv_ref[...].astype(BF16)

    lam = _lam(lq1, lk1, lq2, lk2, lam_init)
    half = tq // 2
    for c in range(2):
        rows = pl.ds(pl.multiple_of(qb * tq + c * half, half), half)
        qr = _rope(q_ref[c * half:(c + 1) * half, :].astype(F32), cos_ref[rows, :], sup_ref[rows, :],
                   sdn_ref[rows, :]) * QK_SCALE
        o = _diff_head(qr[:, :64].astype(BF16), qr[:, 64:].astype(BF16), k1_scr[...], k2_scr[...],
                       v_scr[...], lam, g_ref[...], lam_init)
        o_ref[c * half:(c + 1) * half, :] = o.astype(BF16)


def _sdiff_attn(p, cache_k, cache_v, lq1, lk1, lq2, lk2, g, lam_init, l):
    tq = 512
    nq = DEC_SEQ // tq
    cos, s_up, s_dn = (jnp.asarray(a) for a in _rope_tables())
    vec = lambda n: pl.BlockSpec((1, n), lambda b, h, q: (0, 0))
    tab = pl.BlockSpec((DEC_SEQ, 128), lambda b, h, q: (0, 0))
    kvblk0 = NP_TOK // DEC_SEQ
    return pl.pallas_call(
        functools.partial(_sdiff_kernel, lam_init=lam_init, tq=tq),
        grid=(DEC_BATCH, DIFF_HEADS, nq),
        in_specs=[pl.BlockSpec((tq, 128), lambda b, h, q: (NP_TOK // tq + b * nq + q, C_DQ // 128 + h)),
                  pl.BlockSpec((DEC_SEQ, 128), lambda b, h, q: (kvblk0 + b, C_DK // 128 + h)),
                  pl.BlockSpec((DEC_SEQ, 128), lambda b, h, q: (kvblk0 + b, C_DV // 128 + h)),
                  pl.BlockSpec((None, None, None, 2, PAST, HEAD_DIM), lambda b, h, q: (b, l, h, 0, 0, 0)),
                  pl.BlockSpec((None, None, None, PAST, DIFF_V), lambda b, h, q: (b, l, h, 0, 0)),
                  tab, tab, tab, vec(64), vec(64), vec(64), vec(64), vec(128)],
        out_specs=pl.BlockSpec((tq, 128), lambda b, h, q: (b * nq + q, h)),
        out_shape=jax.ShapeDtypeStruct((NS_TOK, DIFF_HEADS * DIFF_V), BF16),
        scratch_shapes=[pltpu.VMEM((DEC_SEQ + PAST, HEAD_DIM), BF16),
                        pltpu.VMEM((DEC_SEQ + PAST, HEAD_DIM), BF16),
                        pltpu.VMEM((DEC_SEQ + PAST, DIFF_V), BF16)],
        compiler_params=_cparams(("arbitrary", "arbitrary", "arbitrary")),
    )(p, p, p, cache_k, cache_v, cos, s_up, s_dn, lq1, lk1, lq2, lk2, g)


@functools.lru_cache(None)
def _dft_consts(L):
    n = 2 * L
    k = np.arange(L)
    ang = 2.0 * np.pi * ((k[:, None] * k[None, :]) % n) / n
    alt = (-1.0) ** k
    fa = np.cos(ang)
    fb = -np.sin(ang)
    fb[0, :] = alt
    wgt = np.full((L,), 2.0 / n)
    wgt[0] = 1.0 / n
    ga = fa * wgt[:, None]
    gb = fb * wgt[:, None]
    gb[0, :] = alt / n
    f = np.concatenate([fa, fb], axis=0)
    g = np.concatenate([ga.T, gb.T], axis=1)
    return np.asarray(f, dtype=BF16), np.asarray(g, dtype=BF16)


@functools.lru_cache(None)
def _filter_consts(L):
    f32 = np.float32
    t = np.linspace(0.0, 1.0, L, dtype=f32)[:, None]
    pos = np.arange(L, dtype=f32)[:, None]
    bands = np.linspace(1e-4, HY_BANDS - 1, HY_BANDS, dtype=f32)[None, :]
    ang = f32(2.0 * math.pi) * bands * pos / f32(L)
    z = np.zeros((L, 128), f32)
    z[:, 0:1] = t
    z[:, 1:1 + HY_BANDS] = np.cos(ang)
    z[:, 1 + HY_BANDS:HY_EMB] = -np.sin(ang)
    min_decay = math.log(1e-2) / 1.5
    max_decay = math.log(1e-2) / 0.3
    deltas = np.abs(np.linspace(min_decay, max_decay, HY_W, dtype=f32))
    decay = np.exp(-t * deltas[None, :]).astype(f32)
    return z, decay


def _spectra_kernel(z_ref, w1_ref, b1_ref, w2_ref, b2_ref, fr_ref, w3_ref, dec_ref, fa_ref, fb_ref,
                    sa_ref, sb_ref, filt_scr, *, L, kc):
    j = pl.program_id(0)

    @pl.when(j == 0)
    def _():
        fr = fr_ref[...]
        hdn = jnp.sin(fr * (_dot3(z_ref[...], w1_ref[...]) + b1_ref[...]))
        hdn = jnp.sin(fr * (_dot3(hdn, w2_ref[...]) + b2_ref[...]))
        dec = dec_ref[...]
        not_first = lax.broadcasted_iota(jnp.int32, (L, HY_W), 0) > 0
        for o in range(2):
            hf = _dot3(hdn, w3_ref[:, 512 * o:512 * o + 256]) * dec
            hb = jnp.where(not_first, _dot3(hdn, w3_ref[:, 512 * o + 256:512 * o + 512]) * dec, 0.0)
            nrm = (jnp.sum(jnp.abs(hf), axis=0, keepdims=True)
                   + jnp.sum(jnp.abs(hb), axis=0, keepdims=True))
            filt_scr[:, 512 * o:512 * o + 256] = (hf / nrm).astype(BF16)
            filt_scr[:, 512 * o + 256:512 * o + 512] = (hb / nrm).astype(BF16)

    ta = _dot(fa_ref[...], filt_scr[...])
    tb = _dot(fb_ref[...], filt_scr[...])
    first = (lax.broadcasted_iota(jnp.int32, (kc, HY_W), 0) + j * kc) == 0
    for o in range(2):
        af, ab = ta[:, 512 * o:512 * o + 256], ta[:, 512 * o + 256:512 * o + 512]
        bf, bb = tb[:, 512 * o:512 * o + 256], tb[:, 512 * o + 256:512 * o + 512]
        sa_ref[:, 256 * o:256 * o + 256] = af + ab
        sb_ref[:, 256 * o:256 * o + 256] = jnp.where(first, bf + bb, bf - bb)


def _hy_spectra(L, kc, w1p, b1, w2, b2, fr, w3):
    z, decay = _filter_consts(L)
    f = jnp.asarray(_dft_consts(L)[0])
    nj = L // kc
    full = lambda shape: pl.BlockSpec(shape, lambda j: tuple(0 for _ in shape))
    return pl.pallas_call(
        functools.partial(_spectra_kernel, L=L, kc=kc),
        grid=(nj,),
        in_specs=[full((L, 128)), full((128, HY_HID)), full((1, HY_HID)), full((HY_HID, HY_HID)),
                  full((1, HY_HID)), full((1, HY_HID)), full((HY_HID, 4 * HY_W)), full((L, HY_W)),
                  pl.BlockSpec((kc, L), lambda j: (j, 0)),
                  pl.BlockSpec((kc, L), lambda j: (j + nj, 0))],
        out_specs=[pl.BlockSpec((kc, 2 * HY_W), lambda j: (j, 0)),
                   pl.BlockSpec((kc, 2 * HY_W), lambda j: (j, 0))],
        out_shape=[jax.ShapeDtypeStruct((L, 2 * HY_W), F32)] * 2,
        scratch_shapes=[pltpu.VMEM((L, 4 * HY_W), BF16)],
        compiler_params=_cparams(("arbitrary",)),
    )(jnp.asarray(z), w1p, b1, w2, b2, fr, w3, jnp.asarray(decay), f, f)


def _hyconv_kernel(*refs, L, nb, kc, nj, row_blk0, staged):
    n_in = 1 if staged else nb
    (cw_ref, cb_ref, d_ref, fa_ref, fb_ref, ga_ref, gb_ref, sa_ref, sb_ref, o_ref,
     xin_scr, x_scr, g1_scr, g2_scr, y_scr) = refs[n_in:n_in + 15]
    bg = pl.program_id(0)
    o = pl.program_id(1)
    j = pl.program_id(2)

    @pl.when((o == 0) & (j == 0))
    def _():
        row = lax.broadcasted_iota(jnp.int32, (L, HY_W), 0)
        pieces = [(i, part) for i in range(nb) for part in range(3)]
        if staged:
            p_hbm, (stage, sem) = refs[0], refs[n_in + 15:]

            def piece_copy(k):
                i, part = pieces[k]
                row0 = pl.multiple_of((row_blk0 + bg * nb + i) * L, L)
                return pltpu.make_async_copy(
                    p_hbm.at[pl.ds(row0, L), C_HY + HY_W * part:C_HY + HY_W * (part + 1)],
                    stage.at[k % 2], sem.at[k % 2])

            piece_copy(0).start()
            piece_copy(1).start()
        for k, (i, part) in enumerate(pieces):
            cols = slice(HY_W * i, HY_W * (i + 1))
            dst = (x_scr, g1_scr, g2_scr)[part]
            pc = slice(HY_W * part, HY_W * (part + 1))
            if staged:
                piece_copy(k).wait()
                u = stage[k % 2].astype(F32)
            else:
                u = refs[i][:, pc].astype(F32)
            up = jnp.where(row == 0, 0.0, pltpu.roll(u, 1, axis=0))
            un = jnp.where(row == L - 1, 0.0, pltpu.roll(u, L - 1, axis=0))
            dst[:, cols] = (up * cw_ref[0:1, pc] + u * cw_ref[1:2, pc] + un * cw_ref[2:3, pc]
                            + cb_ref[:, pc])
            if part == 0:
                xin_scr[:, cols] = x_scr[:, cols].astype(BF16)
            if staged and k + 2 < len(pieces):
                piece_copy(k + 2).start()

    @pl.when(j == 0)
    def _():
        y_scr[...] = jnp.zeros_like(y_scr)

    xa = _dot(fa_ref[...], xin_scr[...])
    xb = _dot(fb_ref[...], xin_scr[...])
    sa = jnp.concatenate([sa_ref[...]] * nb, axis=1)
    sb = jnp.concatenate([sb_ref[...]] * nb, axis=1)
    first = (lax.broadcasted_iota(jnp.int32, (kc, nb * HY_W), 0) + j * kc) == 0
    ya = jnp.where(first, xa * sa, xa * sa - xb * sb)
    yb = jnp.where(first, xb * sb, xa * sb + xb * sa)
    y_scr[...] += _dot(ga_ref[...], ya.astype(BF16)) + _dot(gb_ref[...], yb.astype(BF16))

    @pl.when(j == nj - 1)
    def _():
        dvec = jnp.concatenate([d_ref[pl.ds(o, 1), :]] * nb, axis=1)
        y = y_scr[...] + x_scr[...] * dvec

        @pl.when(o == 0)
        def _():
            zz = g1_scr[...] * y
            x_scr[...] = zz
            xin_scr[...] = zz.astype(BF16)

        @pl.when(o == 1)
        def _():
            res = g2_scr[...] * y
            for i in range(nb):
                o_ref[L * i:L * (i + 1), :] = res[:, HY_W * i:HY_W * (i + 1)].astype(BF16)


def _hy_conv(p, L, nb, kc, row_blk0, n_seq, conv_w, conv_b, d, sa, sb):
    f, g = (jnp.asarray(a) for a in _dft_consts(L))
    nj = L // kc
    staged = nb * L > DEC_SEQ
    if staged:
        u_specs = [pl.BlockSpec(memory_space=pl.ANY)]
        u_scratch = [pltpu.VMEM((2, L, HY_W), p.dtype), pltpu.SemaphoreType.DMA((2,))]
    else:
        u_specs = [pl.BlockSpec((L, 3 * HY_W), lambda bg, o, j, i=i: (row_blk0 + bg * nb + i, C_HY // 768))
                   for i in range(nb)]
        u_scratch = []
    small = lambda shape: pl.BlockSpec(shape, lambda bg, o, j: (0, 0))
    return pl.pallas_call(
        functools.partial(_hyconv_kernel, L=L, nb=nb, kc=kc, nj=nj, row_blk0=row_blk0, staged=staged),
        grid=(n_seq // nb, 2, nj),
        in_specs=u_specs + [
            small((3, 3 * HY_W)), small((1, 3 * HY_W)), small((2, HY_W)),
            pl.BlockSpec((kc, L), lambda bg, o, j: (j, 0)),
            pl.BlockSpec((kc, L), lambda bg, o, j: (j + nj, 0)),
            pl.BlockSpec((L, kc), lambda bg, o, j: (0, j)),
            pl.BlockSpec((L, kc), lambda bg, o, j: (0, j + nj)),
            pl.BlockSpec((kc, HY_W), lambda bg, o, j: (j, o)),
            pl.BlockSpec((kc, HY_W), lambda bg, o, j: (j, o))],
        out_specs=pl.BlockSpec((nb * L, HY_W), lambda bg, o, j: (bg, 0)),
        out_shape=jax.ShapeDtypeStruct((n_seq * L, HY_W), BF16),
        scratch_shapes=[pltpu.VMEM((L, nb * HY_W), BF16)] + [pltpu.VMEM((L, nb * HY_W), F32)] * 4
        + u_scratch,
        compiler_params=_cparams(("arbitrary", "arbitrary", "arbitrary")),
    )(*([p] * len(u_specs)), conv_w, conv_b, d, f, f, g, g, sa, sb)


def _outproj_kernel(xp_ref, xs_ref, mixp_ref, hyp_ref, nas_ref, ds_ref, hys_ref, g1_ref, w_ref, o_ref):
    i = pl.program_id(0)
    w = lambda a, b: w_ref[a:b, :].astype(BF16)

    @pl.when(i < NP_TOK // TM)
    def _():
        y = _dot(mixp_ref[...], w(0, 768)) + _dot(hyp_ref[...], w(768, D))
        o_ref[...] = xp_ref[...] + g1_ref[...] * y

    @pl.when(i >= NP_TOK // TM)
    def _():
        y = _dot(nas_ref[...], w(0, 256)) + _dot(ds_ref[...], w(256, 768)) + _dot(hys_ref[...], w(768, D))
        o_ref[...] = xs_ref[...] + g1_ref[...] * y


def _x_specs(xs_block0):
    npt = NP_TOK // TM
    return [pl.BlockSpec((TM, D), lambda i, *_: (jnp.minimum(i, npt - 1), 0)),
            pl.BlockSpec((TM, D), lambda i, *_: (jnp.maximum(i - npt, 0) + xs_block0, 0))]


def _outproj(xp, xs, xs_block0, mix_p, hy_p, na_s, d_s, hy_s, mod6, w_out, l):
    npt = NP_TOK // TM
    pidx = lambda i: (jnp.minimum(i, npt - 1), 0)
    sidx = lambda i: (jnp.maximum(i - npt, 0), 0)
    return pl.pallas_call(
        _outproj_kernel,
        grid=(N_TOK // TM,),
        in_specs=_x_specs(xs_block0) + [
                  pl.BlockSpec((TM, 768), pidx),
                  pl.BlockSpec((TM, HY_W), pidx),
                  pl.BlockSpec((TM, 256), sidx),
                  pl.BlockSpec((TM, 512), sidx),
                  pl.BlockSpec((TM, HY_W), sidx),
                  _mod_spec(2, TM),
                  pl.BlockSpec((None, D, D), lambda i: (l, 0, 0))],
        out_specs=pl.BlockSpec((TM, D), lambda i: (i, 0)),
        out_shape=jax.ShapeDtypeStruct((N_TOK, D), F32),
        compiler_params=_cparams(("arbitrary",)),
    )(xp, xs, mix_p, hy_p, na_s, d_s, hy_s, mod6, w_out)


def _ffn_kernel(x_ref, g_ref, sc_ref, sh_ref, gate_ref, wg_ref, wu_ref, wd_ref, o_ref, h_scr, *, nj):
    j = pl.program_id(1)

    @pl.when(j == 0)
    def _():
        h_scr[...] = _norm_mod(x_ref[...], g_ref[...], sc_ref[...], sh_ref[...]).astype(BF16)
        o_ref[...] = jnp.zeros_like(o_ref)

    h = h_scr[...]
    a = _silu(_dot(h, wg_ref[...].astype(BF16))) * _dot(h, wu_ref[...].astype(BF16))
    o_ref[...] += _dot(a.astype(BF16), wd_ref[...].astype(BF16))

    @pl.when(j == nj - 1)
    def _():
        o_ref[...] = x_ref[...] + gate_ref[...] * o_ref[...]


def _dense_ffn(x, g, mod6, wg, wu, wd, i_ffn):
    tf = 256
    tm = DEC_SEQ
    nj = FFN // tf
    return pl.pallas_call(
        functools.partial(_ffn_kernel, nj=nj),
        grid=(N_TOK // tm, nj),
        in_specs=[pl.BlockSpec((tm, D), lambda i, j: (i, 0)),
                  pl.BlockSpec((1, D), lambda i, j: (0, 0)),
                  _mod_spec(4, tm), _mod_spec(3, tm), _mod_spec(5, tm),
                  pl.BlockSpec((None, D, tf), lambda i, j: (i_ffn, 0, j)),
                  pl.BlockSpec((None, D, tf), lambda i, j: (i_ffn, 0, j)),
                  pl.BlockSpec((None, tf, D), lambda i, j: (i_ffn, j, 0))],
        out_specs=pl.BlockSpec((tm, D), lambda i, j: (i, 0)),
        out_shape=jax.ShapeDtypeStruct((N_TOK, D), F32),
        scratch_shapes=[pltpu.VMEM((tm, D), BF16)],
        compiler_params=_cparams(("arbitrary", "arbitrary")),
    )(x, g, mod6, mod6, mod6, wg, wu, wd)


def _router_kernel(x_ref, g_ref, sc_ref, sh_ref, wr_ref, h_ref, r_ref):
    h = _norm_mod(x_ref[...], g_ref[...], sc_ref[...], sh_ref[...])
    h_ref[...] = h.astype(BF16)
    lane = lax.broadcasted_iota(jnp.int32, (TM, 128), 1)
    lg = jnp.where(lane < N_EXP, _dot3(h, wr_ref[...]), -jnp.inf)
    m1 = jnp.max(lg, axis=-1, keepdims=True)
    i1 = jnp.min(jnp.where(lg == m1, lane, 128), axis=-1, keepdims=True)
    lg2 = jnp.where(lane == i1, -jnp.inf, lg)
    m2 = jnp.max(lg2, axis=-1, keepdims=True)
    i2 = jnp.min(jnp.where(lg2 == m2, lane, 128), axis=-1, keepdims=True)
    e = jnp.exp(m2 - m1)
    w1 = 1.0 / (1.0 + e)
    w2 = e / (1.0 + e)
    r_ref[...] = jnp.where(lane == 0, i1.astype(F32),
                           jnp.where(lane == 1, i2.astype(F32),
                                     jnp.where(lane == 2, w1, jnp.where(lane == 3, w2, 0.0))))


def _router(x, g, mod6, wr_pad):
    return pl.pallas_call(
        _router_kernel,
        grid=(N_TOK // TM,),
        in_specs=[pl.BlockSpec((TM, D), lambda i: (i, 0)),
                  pl.BlockSpec((1, D), lambda i: (0, 0)),
                  _mod_spec(4, TM), _mod_spec(3, TM),
                  pl.BlockSpec((D, 128), lambda i: (0, 0))],
        out_specs=[pl.BlockSpec((TM, D), lambda i: (i, 0)),
                   pl.BlockSpec((TM, 128), lambda i: (i, 0))],
        out_shape=[jax.ShapeDtypeStruct((N_TOK, D), BF16), jax.ShapeDtypeStruct((N_TOK, 128), F32)],
        compiler_params=_cparams(("arbitrary",)),
    )(x, g, mod6, mod6, wr_pad)


def _dispatch_kernel(blo_ref, bhi_ref, sexp_ref, h_hbm, rp_ref, o_ref, h_scr, sem):
    s = pl.program_id(0)

    @pl.when(s == 0)
    def _():
        cp = pltpu.make_async_copy(h_hbm, h_scr, sem)
        cp.start()
        cp.wait()

    e = sexp_ref[s]
    rows = s * MOE_TS + lax.broadcasted_iota(jnp.int32, (MOE_TS, MOE_DTB), 0)

    def sel(b):
        return (rows == rp_ref[e, pl.ds(b, 1), :]).astype(BF16)

    is_empty = blo_ref[s] > bhi_ref[s]

    @pl.when(is_empty)
    def _():
        o_ref[...] = jnp.zeros_like(o_ref)

    @pl.when(jnp.logical_not(is_empty))
    def _():
        b0 = jnp.minimum(blo_ref[s], N_TOK // MOE_DTB - MOE_DWIN)
        hwin = h_scr[pl.ds(pl.multiple_of(b0 * MOE_DTB, MOE_DTB), MOE_DWIN * MOE_DTB), :]
        acc = _dot(jnp.concatenate([sel(b0 + k) for k in range(MOE_DWIN)], axis=1), hwin)

        def body(b, acc):
            hb = h_scr[pl.ds(pl.multiple_of(b * MOE_DTB, MOE_DTB), MOE_DTB), :]
            return acc + _dot(sel(b), hb)

        acc = lax.fori_loop(b0 + MOE_DWIN, bhi_ref[s] + 1, body, acc)
        o_ref[...] = acc.astype(BF16)


def _dispatch(blo, bhi, sexp, h, rp_t):
    return pl.pallas_call(
        _dispatch_kernel,
        grid_spec=pltpu.PrefetchScalarGridSpec(
            num_scalar_prefetch=3,
            grid=(MOE_ROWS // MOE_TS,),
            in_specs=[pl.BlockSpec(memory_space=pl.ANY),
                      pl.BlockSpec((N_EXP, N_TOK // MOE_DTB, MOE_DTB), lambda s, *_: (0, 0, 0))],
            out_specs=pl.BlockSpec((MOE_TS, D), lambda s, *_: (s, 0)),
            scratch_shapes=[pltpu.VMEM((N_TOK, D), BF16), pltpu.SemaphoreType.DMA(())]),
        out_shape=jax.ShapeDtypeStruct((MOE_ROWS, D), BF16),
        compiler_params=_cparams(("arbitrary",)),
    )(blo, bhi, sexp, h, rp_t)


def _experts_kernel(te_ref, used_ref, rows_ref, xs_ref, wg_ref, wu_ref, wd_ref, o_ref, acc_scr, *, nj):
    i = pl.program_id(0)
    j = pl.program_id(1)
    nrows = rows_ref[i]

    @pl.when(j == 0)
    def _():
        acc_scr[...] = jnp.zeros_like(acc_scr)

    for sz in range(MOE_TQ, MOE_TM + 1, MOE_TQ):
        @pl.when(nrows == sz)
        def _():
            xb = xs_ref[0:sz, :]
            a = _silu(_dot(xb, wg_ref[...].astype(BF16))) * _dot(xb, wu_ref[...].astype(BF16))
            acc_scr[0:sz, :] += _dot(a.astype(BF16), wd_ref[...].astype(BF16))

    @pl.when(j == nj - 1)
    def _():
        o_ref[...] = acc_scr[...].astype(BF16)


def _experts(tile_expert, used, tile_rows, xs, wg, wu, wd, i_moe):
    nj = EXP_DIM // MOE_TF

    def tile(i, used):
        return jnp.minimum(i, used[0] - 1)

    def chunk(i, j, used):
        return jnp.where(i < used[0], j, nj - 1)

    return pl.pallas_call(
        functools.partial(_experts_kernel, nj=nj),
        grid_spec=pltpu.PrefetchScalarGridSpec(
            num_scalar_prefetch=3,
            grid=(MOE_YROWS // MOE_TM, nj),
            in_specs=[pl.BlockSpec((MOE_TM, D), lambda i, j, te, used, tr: (tile(i, used), 0)),
                      pl.BlockSpec((None, None, D, MOE_TF),
                                   lambda i, j, te, used, tr: (i_moe, te[tile(i, used)], 0, chunk(i, j, used))),
                      pl.BlockSpec((None, None, D, MOE_TF),
                                   lambda i, j, te, used, tr: (i_moe, te[tile(i, used)], 0, chunk(i, j, used))),
                      pl.BlockSpec((None, None, MOE_TF, D),
                                   lambda i, j, te, used, tr: (i_moe, te[tile(i, used)], chunk(i, j, used), 0))],
            out_specs=pl.BlockSpec((MOE_TM, D), lambda i, j, te, used, tr: (i, 0)),
            scratch_shapes=[pltpu.VMEM((MOE_TM, D), F32)]),
        out_shape=jax.ShapeDtypeStruct((MOE_YROWS, D), BF16),
        compiler_params=_cparams(("arbitrary", "arbitrary")),
    )(tile_expert, used, tile_rows, xs, wg, wu, wd)


def _combine_kernel(ws_ref, kind_ref, x_ref, rp_ref, comb_ref, gate_ref, fg_ref, ys_hbm, op_ref, os_ref,
                    win_scr, y_scr, sem):
    b = pl.program_id(0)
    nb = pl.num_programs(0)
    slot = b % 2

    def win_copy(blk, sl, e):
        start = pl.multiple_of(ws_ref[blk * N_EXP + e], MOE_ALIGN)
        return pltpu.make_async_copy(ys_hbm.at[pl.ds(start, MOE_WIN)], win_scr.at[sl, e], sem.at[sl, e])

    @pl.when(b == 0)
    def _():
        for e in range(N_EXP):
            win_copy(0, 0, e).start()

    @pl.when(b + 1 < nb)
    def _():
        for e in range(N_EXP):
            win_copy(b + 1, 1 - slot, e).start()

    col = lax.broadcasted_iota(jnp.int32, (MOE_TB, MOE_WIN), 1)
    y_scr[...] = jnp.zeros_like(y_scr)
    for e in range(N_EXP):
        win_copy(b, slot, e).wait()
        rel = rp_ref[:, e:e + 1] - ws_ref[b * N_EXP + e]
        cw = comb_ref[:, e:e + 1]
        kind = kind_ref[b * N_EXP + e]

        for k, wn in enumerate(MOE_WINS):
            @pl.when(kind == k)
            def _():
                sel = (rel == col[:, :wn]).astype(BF16) if wn == MOE_WIN else (
                    rel == lax.broadcasted_iota(jnp.int32, (MOE_TB, wn), 1)).astype(BF16)
                y_scr[...] += cw * _dot(sel, win_scr[slot, e, 0:wn, :])

    x = x_ref[...] + gate_ref[...] * y_scr[...]
    x = (x * lax.rsqrt(jnp.mean(x * x, axis=-1, keepdims=True) + EPS)) * fg_ref[...]

    @pl.when(b < NP_TOK // MOE_TB)
    def _():
        op_ref[...] = x

    @pl.when(b >= NP_TOK // MOE_TB)
    def _():
        os_ref[...] = x


def _combine(ws, kind, x, rp8, comb, mod6, final_g, ys):
    npb = NP_TOK // MOE_TB
    return pl.pallas_call(
        _combine_kernel,
        grid_spec=pltpu.PrefetchScalarGridSpec(
            num_scalar_prefetch=2,
            grid=(N_TOK // MOE_TB,),
            in_specs=[pl.BlockSpec((MOE_TB, D), lambda b, *_: (b, 0)),
                      pl.BlockSpec((MOE_TB, N_EXP), lambda b, *_: (b, 0)),
                      pl.BlockSpec((MOE_TB, N_EXP), lambda b, *_: (b, 0)),
                      _mod_spec(5, MOE_TB),
                      pl.BlockSpec((1, D), lambda b, *_: (0, 0)),
                      pl.BlockSpec(memory_space=pl.ANY)],
            out_specs=[pl.BlockSpec((MOE_TB, D), lambda b, *_: (jnp.minimum(b, npb - 1), 0)),
                       pl.BlockSpec((MOE_TB, D), lambda b, *_: (jnp.maximum(b - npb, 0), 0))],
            scratch_shapes=[pltpu.VMEM((2, N_EXP, MOE_WIN, D), BF16),
                            pltpu.VMEM((MOE_TB, D), F32),
                            pltpu.SemaphoreType.DMA((2, N_EXP))]),
        out_shape=[jax.ShapeDtypeStruct((NP_TOK, D), F32), jax.ShapeDtypeStruct((NS_TOK, D), F32)],
        compiler_params=_cparams(("arbitrary",)),
    )(ws, kind, x, rp8, comb, mod6, final_g, ys)


def _moe(x, g, mod6, router, wg, wu, wd, i_moe, final_g):
    wr_pad = jnp.pad(router, ((0, 0), (0, 128 - N_EXP)))
    h, r = _router(x, g, mod6, wr_pad)

    i32 = jnp.int32
    i12 = r[:, 0:2].astype(i32)
    earange = jnp.arange(N_EXP, dtype=i32)
    hit1 = i12[:, 0:1] == earange[None, :]
    hit2 = i12[:, 1:2] == earange[None, :]
    comb = jnp.where(hit1, r[:, 2:3], 0.0) + jnp.where(hit2, r[:, 3:4], 0.0)
    mask = (hit1 | hit2).astype(i32)
    csum = jnp.cumsum(mask, axis=0)
    counts = csum[-1]
    padded = ((counts + MOE_TM - 1) // MOE_TM) * MOE_TM
    ends = jnp.cumsum(padded)
    starts = ends - padded
    rp8 = jnp.where(mask > 0, starts[None, :] + csum - 1, -1).astype(i32)
    n_tiles = MOE_ROWS // MOE_TM
    tile_expert = jnp.minimum(
        jnp.searchsorted(ends, jnp.arange(n_tiles, dtype=i32) * MOE_TM, side="right"),
        N_EXP - 1).astype(i32)
    used = (ends[-1:] // MOE_TM).astype(i32)
    tile_row0 = jnp.arange(MOE_YROWS // MOE_TM, dtype=i32) * MOE_TM
    te_all = jnp.minimum(jnp.searchsorted(ends, tile_row0, side="right"), N_EXP - 1)
    valid = jnp.clip(counts[te_all] - (tile_row0 - starts[te_all]), 0, MOE_TM)
    valid = jnp.where(tile_row0 < ends[-1], valid, 0)
    tile_rows = (((valid + MOE_TQ - 1) // MOE_TQ) * MOE_TQ).astype(i32)

    sub_row0 = jnp.arange(MOE_ROWS // MOE_TS, dtype=i32) * MOE_TS
    sexp = tile_expert[sub_row0 // MOE_TM]
    qlo = sub_row0 - starts[sexp]
    qend = jnp.minimum(qlo + MOE_TS, counts[sexp])
    cbe = csum[MOE_DTB - 1::MOE_DTB, :].T[sexp]
    blo = jnp.sum((cbe <= qlo[:, None]).astype(i32), axis=1)
    bhi = jnp.minimum(jnp.sum((cbe < qend[:, None]).astype(i32), axis=1), N_TOK // MOE_DTB - 1)
    empty = qend <= qlo
    blo = jnp.where(empty, 1, blo).astype(i32)
    bhi = jnp.where(empty, 0, bhi).astype(i32)

    cb = csum[MOE_TB - 1::MOE_TB, :]
    cprev = jnp.concatenate([jnp.zeros((1, N_EXP), i32), cb[:-1]], axis=0)
    ws = (((starts[None, :] + cprev) // MOE_ALIGN) * MOE_ALIGN).reshape(-1).astype(i32)
    n_be = cb - cprev
    kind = sum((n_be > wn - MOE_ALIGN).astype(i32) for wn in MOE_WINS[:-1])
    kind = jnp.where(n_be == 0, len(MOE_WINS), kind).reshape(-1).astype(i32)

    rp_t = rp8.T.reshape(N_EXP, N_TOK // MOE_DTB, MOE_DTB)
    xs = _dispatch(blo, bhi, sexp, h, rp_t)
    ys = _experts(tile_expert, used, tile_rows, xs, wg, wu, wd, i_moe)
    return _combine(ws, kind, x, rp8, comb, mod6, final_g, ys)


assert DEPTH == 2

def kernel(x_prompt, x_sample, cache_na_k, cache_na_v, cache_diff_k, cache_diff_v, c, c_ctx, w_in, w_out, ada_w, ada_b, norm_mix_g, norm_ffn_g, na_rpb, diff_lq1, diff_lk1, diff_lq2, diff_lk2, diff_subln_g, hy_conv_w, hy_conv_b, hy_d, hy_f_w1, hy_f_b1, hy_f_w2, hy_f_b2, hy_f_freq, hy_f_w3, ffn_w_gate, ffn_w_up, ffn_w_down, moe_router, moe_w_gate, moe_w_up, moe_w_down, final_norm_g):
    xparts = (x_prompt.reshape(NP_TOK, D), x_sample.reshape(NS_TOK, D), 0)
    cond8 = jnp.concatenate([c_ctx[None, :], c, jnp.zeros((5, D), F32)], axis=0)
    mods = _modulation(cond8, ada_w, ada_b)
    final_g = final_norm_g.reshape(1, D)
    bias = _na_bias(na_rpb)

    leaves = [jnp.zeros((BATCH, DEPTH, NA_HEADS, SEQ, HEAD_DIM), F32),
              jnp.zeros((BATCH, DEPTH, NA_HEADS, SEQ, HEAD_DIM), F32),
              jnp.zeros((BATCH, DEPTH, DIFF_HEADS, 2, SEQ, HEAD_DIM), F32),
              jnp.zeros((BATCH, DEPTH, DIFF_HEADS, SEQ, DIFF_V), F32)]
    for l in range(DEPTH):
        lam_init = 0.8 - 0.6 * math.exp(-0.3 * l)
        mod6 = mods[l].reshape(8, 6, D).transpose(1, 0, 2).reshape(6, 8, 1, D)
        row = lambda a: a[l].reshape(1, -1)
        lq1, lk1, lq2, lk2, subg = row(diff_lq1), row(diff_lk1), row(diff_lq2), row(diff_lk2), row(diff_subln_g)

        p, *leaves = _inproj(*xparts, row(norm_mix_g), mod6, w_in, l, leaves)

        mix_p = _prompt_attn(p, lq1, lk1, lq2, lk2, subg, lam_init)
        na_s = _na_attn(p, cache_na_k, cache_na_v, bias, l)
        d_s = _sdiff_attn(p, cache_diff_k, cache_diff_v, lq1, lk1, lq2, lk2, subg, lam_init, l)

        w1p = jnp.pad(hy_f_w1[l], ((0, 128 - HY_EMB), (0, 0)))
        fargs = (w1p, row(hy_f_b1), hy_f_w2[l], row(hy_f_b2), row(hy_f_freq), hy_f_w3[l])
        cargs = (hy_conv_w[l], row(hy_conv_b), hy_d[l])
        sa_p, sb_p = _hy_spectra(SEQ, SEQ, *fargs)
        hy_p = _hy_conv(p, SEQ, 4, SEQ, 0, BATCH, *cargs, sa_p, sb_p)
        sa_s, sb_s = _hy_spectra(DEC_SEQ, 512, *fargs)
        hy_s = _hy_conv(p, DEC_SEQ, 2, 512, NP_TOK // DEC_SEQ, DEC_BATCH, *cargs, sa_s, sb_s)

        x = _outproj(*xparts, mix_p, hy_p, na_s, d_s, hy_s, mod6, w_out, l)

        if l == 0:
            x = _dense_ffn(x, row(norm_ffn_g), mod6, ffn_w_gate, ffn_w_up, ffn_w_down, 0)
            xparts = (x, x, NP_TOK // TM)
        else:
            yp, ys = _moe(x, row(norm_ffn_g), mod6, moe_router[0], moe_w_gate, moe_w_up, moe_w_down,
                          0, final_g)

    return (yp.reshape(BATCH, SEQ, D), ys.reshape(DEC_BATCH, DEC_SEQ, D), *leaves)
```

```python
import functools
import math

import numpy as np
import jax
import jax.numpy as jnp
from jax import lax
from jax.experimental import pallas as pl
from jax.experimental.pallas import tpu as pltpu

F32 = jnp.float32
BF16 = jnp.bfloat16

D = 1024
BATCH, SEQ = 32, 256
DEC_BATCH, DEC_SEQ = 2, 2048
DEPTH = 2
PAST = 512
GRID_W = 64
GRID_ROWS = DEC_SEQ // GRID_W
HEAD_DIM = 64
NA_HEADS = 4
DIFF_HEADS = 4
DIFF_V = 128
WIN_ROWS, WIN_COLS = 8, 16
HY_W = 256
HY_EMB = 33
HY_BANDS = 16
HY_HID = 64
PROJ = 3072
FFN = 2816
N_EXP = 8
EXP_DIM = 3584
EPS = 1e-6
ROPE_BASE = 10000.0

NP_TOK = BATCH * SEQ
NS_TOK = DEC_BATCH * DEC_SEQ
N_TOK = NP_TOK + NS_TOK

C_NAQ, C_NAK, C_NAV = 0, 256, 512
C_DQ, C_DK, C_DV = 768, 1280, 1792
C_HY = 2304

TM = 1024
VMEM_LIMIT = 56 * 1024 * 1024

MOE_TM = 1024
MOE_TF = 512
MOE_TQ = 256
MOE_ROWS = 2 * N_TOK + N_EXP * MOE_TM
MOE_TS = 256
MOE_DTB = 256
MOE_DWIN = 5
MOE_TB = 512
MOE_ALIGN = 16
MOE_WIN = MOE_TB + MOE_ALIGN
MOE_WINS = (MOE_TB // 4 + MOE_ALIGN, MOE_TB // 2 + MOE_ALIGN, MOE_WIN)
MOE_YROWS = MOE_ROWS + MOE_TM


def _cparams(sem):
    return pltpu.CompilerParams(dimension_semantics=sem, vmem_limit_bytes=VMEM_LIMIT)


def _dot(a, b):
    return jnp.dot(a, b, preferred_element_type=F32)


def _dot_nt(a, b):
    return lax.dot_general(a, b, (((1,), (1,)), ((), ())), preferred_element_type=F32)


def _split(a):
    hi = a.astype(BF16)
    lo = (a - hi.astype(F32)).astype(BF16)
    return hi, lo


def _dot3(a, b):
    ah, al = _split(a)
    bh, bl = _split(b)
    return _dot(ah, bh) + (_dot(ah, bl) + _dot(al, bh))


def _silu(x):
    return x / (1.0 + jnp.exp(-x))


def _mod_row(i, tm):
    t = i * tm
    return jnp.where(t < NP_TOK, 0, 1 + (t - NP_TOK) // DEC_SEQ)


def _mod_spec(k, tm):
    return pl.BlockSpec((None, None, 1, D), lambda i, *_: (k, _mod_row(i, tm), 0, 0))


def _norm_mod(x, g, sc, sh):
    y = x * lax.rsqrt(jnp.mean(x * x, axis=-1, keepdims=True) + EPS)
    return (y * g) * (1.0 + sc) + sh


def _mod_kernel(c_ref, w_ref, b_ref, o_ref):
    o_ref[...] = _dot3(_silu(c_ref[...]), w_ref[...]) + b_ref[...]


def _modulation(cond8, ada_w, ada_b):
    tn = 1536
    return pl.pallas_call(
        _mod_kernel,
        grid=(DEPTH, 6 * D // tn),
        in_specs=[pl.BlockSpec((8, D), lambda l, j: (0, 0)),
                  pl.BlockSpec((None, D, tn), lambda l, j: (l, 0, j)),
                  pl.BlockSpec((None, 1, tn), lambda l, j: (l, 0, j))],
        out_specs=pl.BlockSpec((None, 8, tn), lambda l, j: (l, 0, j)),
        out_shape=jax.ShapeDtypeStruct((DEPTH, 8, 6 * D), F32),
        compiler_params=_cparams(("arbitrary", "arbitrary")),
    )(cond8, ada_w, ada_b.reshape(DEPTH, 1, 6 * D))


def _inproj_kernel(xp_ref, xs_ref, g_ref, sc_ref, sh_ref, w_hbm, nak_in, nav_in, dk_in, dv_in,
                   o_ref, nak_ref, nav_ref, dk_ref, dv_ref, h_scr, w_scr, stage, acc_scr, sem, *, l, tn):
    del nak_in, nav_in, dk_in, dv_in
    i = pl.program_id(0)
    j = pl.program_id(1)

    @pl.when((i == 0) & (j == 0))
    def _():
        for c in range(PROJ // tn):
            cp = pltpu.make_async_copy(w_hbm.at[l, :, c * tn:(c + 1) * tn], stage, sem)
            cp.start()
            cp.wait()
            w_scr[c] = stage[...].astype(BF16)

    is_ctx = i < NP_TOK // TM

    @pl.when((j == 0) & is_ctx)
    def _():
        h_scr[...] = _norm_mod(xp_ref[...], g_ref[...], sc_ref[...], sh_ref[...]).astype(BF16)

    @pl.when((j == 0) & jnp.logical_not(is_ctx))
    def _():
        h_scr[...] = _norm_mod(xs_ref[...], g_ref[...], sc_ref[...], sh_ref[...]).astype(BF16)

    acc_scr[...] = _dot(h_scr[...], w_scr[j])
    o_ref[...] = acc_scr[...].astype(BF16)


    def rows(bb):
        return slice(bb * SEQ, (bb + 1) * SEQ)

    def copy_heads(dst_ref, c0, width, heads, sub=None):
        for bb in range(TM // SEQ):
            for h in heads:
                if sub is None:
                    dst_ref[bb, h] = acc_scr[rows(bb), c0(h):c0(h) + width]
                else:
                    for s in range(2):
                        dst_ref[bb, h, s] = acc_scr[rows(bb), c0(h) + s * width:c0(h) + (s + 1) * width]

    @pl.when(is_ctx & (j == 0))
    def _():
        copy_heads(nak_ref, lambda h: C_NAK + HEAD_DIM * h, HEAD_DIM, range(NA_HEADS))
        copy_heads(nav_ref, lambda h: C_NAV + HEAD_DIM * h, HEAD_DIM, range(NA_HEADS))

    @pl.when(is_ctx & (j == 1))
    def _():
        copy_heads(dk_ref, lambda h: C_DK - tn + 2 * HEAD_DIM * h, HEAD_DIM, range(0, 2), sub=True)

    @pl.when(is_ctx & (j == 2))
    def _():
        copy_heads(dk_ref, lambda h: C_DK - 2 * tn + 2 * HEAD_DIM * h, HEAD_DIM, range(2, 4), sub=True)
        copy_heads(dv_ref, lambda h: C_DV - 2 * tn + DIFF_V * h, DIFF_V, range(DIFF_HEADS))


def _inproj(xp, xs, xs_block0, g, mod6, w_in, l, leaves):
    tn = 768
    nb = TM // SEQ
    ctx = lambda i: jnp.minimum(i, NP_TOK // TM - 1)
    leaf_specs = [pl.BlockSpec((nb, None, NA_HEADS, SEQ, HEAD_DIM), lambda i, j: (ctx(i), l, 0, 0, 0)),
                  pl.BlockSpec((nb, None, NA_HEADS, SEQ, HEAD_DIM), lambda i, j: (ctx(i), l, 0, 0, 0)),
                  pl.BlockSpec((nb, None, DIFF_HEADS, 2, SEQ, HEAD_DIM), lambda i, j: (ctx(i), l, 0, 0, 0, 0)),
                  pl.BlockSpec((nb, None, DIFF_HEADS, SEQ, DIFF_V), lambda i, j: (ctx(i), l, 0, 0, 0))]
    return pl.pallas_call(
        functools.partial(_inproj_kernel, l=l, tn=tn),
        grid=(N_TOK // TM, PROJ // tn),
        in_specs=_x_specs(xs_block0) + [
                  pl.BlockSpec((1, D), lambda i, j: (0, 0)),
                  _mod_spec(1, TM), _mod_spec(0, TM),
                  pl.BlockSpec(memory_space=pl.ANY)] + [pl.BlockSpec(memory_space=pl.ANY)] * 4,
        out_specs=[pl.BlockSpec((TM, tn), lambda i, j: (i, j))] + leaf_specs,
        out_shape=[jax.ShapeDtypeStruct((N_TOK, PROJ), BF16)]
        + [jax.ShapeDtypeStruct(a.shape, a.dtype) for a in leaves],
        input_output_aliases={6: 1, 7: 2, 8: 3, 9: 4},
        scratch_shapes=[pltpu.VMEM((TM, D), BF16), pltpu.VMEM((PROJ // tn, D, tn), BF16),
                        pltpu.VMEM((D, tn), F32), pltpu.VMEM((TM, tn), F32), pltpu.SemaphoreType.DMA(())],
        compiler_params=_cparams(("arbitrary", "arbitrary")),
    )(xp, xs, g, mod6, mod6, w_in, *leaves)


def _lam(lq1, lk1, lq2, lk2, lam_init):
    return (jnp.exp(jnp.sum(lq1[...] * lk1[...], axis=-1, keepdims=True))
            - jnp.exp(jnp.sum(lq2[...] * lk2[...], axis=-1, keepdims=True)) + lam_init)


def _softmax_parts(s):
    m = jnp.max(s, axis=-1, keepdims=True)
    e = jnp.exp(s - m)
    return e, jnp.sum(e, axis=-1, keepdims=True)


QK_SCALE = HEAD_DIM ** -0.5


def _diff_head(q1, q2, k1, k2, v, lam, g, lam_init):
    e1, l1 = _softmax_parts(_dot_nt(q1, k1))
    e2, l2 = _softmax_parts(_dot_nt(q2, k2))
    a = e1 - (lam * l1 * (1.0 / l2)) * e2
    o = _dot(a.astype(BF16), v) * (1.0 / l1)
    o = o * lax.rsqrt(jnp.mean(o * o, axis=-1, keepdims=True) + EPS)
    return (o * g) * (1.0 - lam_init)


def _prompt_attn_kernel(pa_ref, pb_ref, pc_ref, lq1, lk1, lq2, lk2, g_ref, o_ref, *, lam_init):
    lam = _lam(lq1, lk1, lq2, lk2, lam_init)
    g = g_ref[...]

    def col(c0, w, scale=None):
        ref = (pa_ref, pb_ref, pc_ref)[c0 // 768]
        o = c0 % 768
        a = ref[:, o:o + w]
        return (a if scale is None else a * scale).astype(BF16)

    for h in range(NA_HEADS):
        q = col(C_NAQ + 64 * h, 64, QK_SCALE)
        k = col(C_NAK + 64 * h, 64)
        v = col(C_NAV + 64 * h, 64)
        e, l = _softmax_parts(_dot_nt(q, k))
        o = _dot(e.astype(BF16), v) * (1.0 / l)
        o_ref[:, 64 * h:64 * h + 64] = o.astype(BF16)
    for h in range(DIFF_HEADS):
        q1 = col(C_DQ + 128 * h, 64, QK_SCALE)
        q2 = col(C_DQ + 128 * h + 64, 64, QK_SCALE)
        k1 = col(C_DK + 128 * h, 64)
        k2 = col(C_DK + 128 * h + 64, 64)
        v = col(C_DV + 128 * h, 128)
        o = _diff_head(q1, q2, k1, k2, v, lam, g, lam_init)
        o_ref[:, 256 + 128 * h:384 + 128 * h] = o.astype(BF16)


def _prompt_attn(p, lq1, lk1, lq2, lk2, g, lam_init):
    vec = lambda n: pl.BlockSpec((1, n), lambda b: (0, 0))
    return pl.pallas_call(
        functools.partial(_prompt_attn_kernel, lam_init=lam_init),
        grid=(BATCH,),
        in_specs=[pl.BlockSpec((SEQ, 768), lambda b: (b, 0)),
                  pl.BlockSpec((SEQ, 768), lambda b: (b, 1)),
                  pl.BlockSpec((SEQ, 768), lambda b: (b, 2)),
                  vec(64), vec(64), vec(64), vec(64), vec(128)],
        out_specs=pl.BlockSpec((SEQ, 768), lambda b: (b, 0)),
        out_shape=jax.ShapeDtypeStruct((NP_TOK, 768), BF16),
        compiler_params=_cparams(("arbitrary",)),
    )(p, p, p, lq1, lk1, lq2, lk2, g)


def _bias_kernel(rpb_ref, o_ref):
    lh = pl.program_id(0)
    qc = lax.broadcasted_iota(jnp.int32, (GRID_W, GRID_W), 0)
    kc = lax.broadcasted_iota(jnp.int32, (GRID_W, GRID_W), 1)
    delta = jnp.clip(kc - qc + (WIN_COLS - 1), 0, 2 * WIN_COLS - 2)
    qs = jnp.clip(qc - WIN_COLS // 2, 0, GRID_W - WIN_COLS)
    in_win = (kc >= qs) & (kc < qs + WIN_COLS)
    for dr in range(2 * WIN_ROWS - 1):
        base = (lh * (2 * WIN_ROWS - 1) + dr) * (2 * WIN_COLS - 1)
        acc = jnp.zeros((GRID_W, GRID_W), F32)
        for d in range(2 * WIN_COLS - 1):
            acc = jnp.where(delta == d, rpb_ref[base + d], acc)
        piece = jnp.where(in_win, acc, -jnp.inf)
        for case in range(WIN_ROWS):
            i = dr + case - (WIN_ROWS - 1)
            if 0 <= i < WIN_ROWS:
                o_ref[case, :, i * GRID_W:(i + 1) * GRID_W] = piece


def _na_bias(rpb):
    return pl.pallas_call(
        _bias_kernel,
        grid=(DEPTH * NA_HEADS,),
        in_specs=[pl.BlockSpec(memory_space=pltpu.SMEM)],
        out_specs=pl.BlockSpec((None, WIN_ROWS, GRID_W, WIN_ROWS * GRID_W), lambda lh: (lh, 0, 0, 0)),
        out_shape=jax.ShapeDtypeStruct((DEPTH * NA_HEADS, WIN_ROWS, GRID_W, WIN_ROWS * GRID_W), F32),
        compiler_params=_cparams(("arbitrary",)),
    )(rpb.reshape(-1))


def _na_kernel(q_ref, kv_ref, kc_ref, vc_ref, bias_ref, o_ref):
    nloc = WIN_ROWS * GRID_W
    for rr in range(NA_RB):
        r = pl.program_id(1) * NA_RB + rr
        start = jnp.clip(r - WIN_ROWS // 2, 0, GRID_ROWS - WIN_ROWS)
        case = r - start
        row0 = pl.multiple_of(start * GRID_W, GRID_W)
        qrows = slice(rr * GRID_W, (rr + 1) * GRID_W)
        for h in range(NA_HEADS):
            q = (q_ref[qrows, C_NAQ + 64 * h:C_NAQ + 64 * h + 64] * QK_SCALE).astype(BF16)
            k = kv_ref[pl.ds(row0, nloc), C_NAK + 64 * h:C_NAK + 64 * h + 64].astype(BF16)
            v = kv_ref[pl.ds(row0, nloc), C_NAV + 64 * h:C_NAV + 64 * h + 64].astype(BF16)
            s_loc = _dot_nt(q, k) + bias_ref[h, pl.ds(case, 1)][0]
            s_ctx = _dot_nt(q, kc_ref[h].astype(BF16))
            m = jnp.maximum(jnp.max(s_loc, axis=-1, keepdims=True), jnp.max(s_ctx, axis=-1, keepdims=True))
            e_loc = jnp.exp(s_loc - m)
            e_ctx = jnp.exp(s_ctx - m)
            l = jnp.sum(e_loc, axis=-1, keepdims=True) + jnp.sum(e_ctx, axis=-1, keepdims=True)
            o = (_dot(e_loc.astype(BF16), v) + _dot(e_ctx.astype(BF16), vc_ref[h].astype(BF16))) * (1.0 / l)
            o_ref[qrows, 64 * h:64 * h + 64] = o.astype(BF16)


NA_RB = 8


def _na_attn(p, cache_k, cache_v, bias, l):
    qblk0 = NP_TOK // (NA_RB * GRID_W)
    kvblk0 = NP_TOK // DEC_SEQ
    nrg = GRID_ROWS // NA_RB
    return pl.pallas_call(
        _na_kernel,
        grid=(DEC_BATCH, nrg),
        in_specs=[pl.BlockSpec((NA_RB * GRID_W, 768), lambda b, r: (qblk0 + b * nrg + r, 0)),
                  pl.BlockSpec((DEC_SEQ, 768), lambda b, r: (kvblk0 + b, 0)),
                  pl.BlockSpec((None, None, NA_HEADS, PAST, HEAD_DIM), lambda b, r: (b, l, 0, 0, 0)),
                  pl.BlockSpec((None, None, NA_HEADS, PAST, HEAD_DIM), lambda b, r: (b, l, 0, 0, 0)),
                  pl.BlockSpec((NA_HEADS, WIN_ROWS, GRID_W, WIN_ROWS * GRID_W), lambda b, r: (l, 0, 0, 0))],
        out_specs=pl.BlockSpec((NA_RB * GRID_W, 256), lambda b, r: (b * nrg + r, 0)),
        out_shape=jax.ShapeDtypeStruct((NS_TOK, 256), BF16),
        compiler_params=_cparams(("arbitrary", "arbitrary")),
    )(p, p, cache_k, cache_v, bias)


@functools.lru_cache(None)
def _rope_tables():
    t = np.arange(DEC_SEQ)
    lane = np.arange(128)
    dd = lane % HEAD_DIM
    pos = np.where(dd[None, :] < 32, (t // GRID_W)[:, None], (t % GRID_W)[:, None]).astype(np.float64)
    inv = ROPE_BASE ** (-(dd % 16).astype(np.float64) * 2.0 / 32.0)
    ang = pos * inv[None, :]
    first = (dd % 32) < 16
    cos = np.cos(ang)
    s_up = np.where(first[None, :], -np.sin(ang), 0.0)
    s_dn = np.where(first[None, :], 0.0, np.sin(ang))
    return tuple(np.asarray(a, np.float32) for a in (cos, s_up, s_dn))


def _rope(x, cos, s_up, s_dn):
    return x * cos + pltpu.roll(x, 112, axis=1) * s_up + pltpu.roll(x, 16, axis=1) * s_dn


def _sdiff_kernel(q_ref, k_ref, v_ref, ck_ref, cv_ref, cos_ref, sup_ref, sdn_ref,
                  lq1, lk1, lq2, lk2, g_ref, o_ref, k1_scr, k2_scr, v_scr, *, lam_init, tq):
    qb = pl.program_id(2)

    @pl.when(qb == 0)
    def _():
        kr = _rope(k_ref[...].astype(F32), cos_ref[...], sup_ref[...], sdn_ref[...])
        k1_scr[0:DEC_SEQ, :] = kr[:, :64].astype(BF16)
        k2_scr[0:DEC_SEQ, :] = kr[:, 64:].astype(BF16)
        k1_scr[DEC_SEQ:, :] = ck_ref[0].astype(BF16)
        k2_scr[DEC_SEQ:, :] = ck_ref[1].astype(BF16)
        v_scr[0:DEC_SEQ, :] = v_ref[...].astype(BF16)
        v_scr[DEC_SEQ:, :] = cv_ref[...].astype(BF16)

    lam = _lam(lq1, lk1, lq2, lk2, lam_init)
    half = tq // 2
    for c in range(2):
        rows = pl.ds(pl.multiple_of(qb * tq + c * half, half), half)
        qr = _rope(q_ref[c * half:(c + 1) * half, :].astype(F32), cos_ref[rows, :], sup_ref[rows, :],
                   sdn_ref[rows, :]) * QK_SCALE
        o = _diff_head(qr[:, :64].astype(BF16), qr[:, 64:].astype(BF16), k1_scr[...], k2_scr[...],
                       v_scr[...], lam, g_ref[...], lam_init)
        o_ref[c * half:(c + 1) * half, :] = o.astype(BF16)


def _sdiff_attn(p, cache_k, cache_v, lq1, lk1, lq2, lk2, g, lam_init, l):
    tq = 512
    nq = DEC_SEQ // tq
    cos, s_up, s_dn = (jnp.asarray(a) for a in _rope_tables())
    vec = lambda n: pl.BlockSpec((1, n), lambda b, h, q: (0, 0))
    tab = pl.BlockSpec((DEC_SEQ, 128), lambda b, h, q: (0, 0))
    kvblk0 = NP_TOK // DEC_SEQ
    return pl.pallas_call(
        functools.partial(_sdiff_kernel, lam_init=lam_init, tq=tq),
        grid=(DEC_BATCH, DIFF_HEADS, nq),
        in_specs=[pl.BlockSpec((tq, 128), lambda b, h, q: (NP_TOK // tq + b * nq + q, C_DQ // 128 + h)),
                  pl.BlockSpec((DEC_SEQ, 128), lambda b, h, q: (kvblk0 + b, C_DK // 128 + h)),
                  pl.BlockSpec((DEC_SEQ, 128), lambda b, h, q: (kvblk0 + b, C_DV // 128 + h)),
                  pl.BlockSpec((None, None, None, 2, PAST, HEAD_DIM), lambda b, h, q: (b, l, h, 0, 0, 0)),
                  pl.BlockSpec((None, None, None, PAST, DIFF_V), lambda b, h, q: (b, l, h, 0, 0)),
                  tab, tab, tab, vec(64), vec(64), vec(64), vec(64), vec(128)],
        out_specs=pl.BlockSpec((tq, 128), lambda b, h, q: (b * nq + q, h)),
        out_shape=jax.ShapeDtypeStruct((NS_TOK, DIFF_HEADS * DIFF_V), BF16),
        scratch_shapes=[pltpu.VMEM((DEC_SEQ + PAST, HEAD_DIM), BF16),
                        pltpu.VMEM((DEC_SEQ + PAST, HEAD_DIM), BF16),
                        pltpu.VMEM((DEC_SEQ + PAST, DIFF_V), BF16)],
        compiler_params=_cparams(("arbitrary", "arbitrary", "arbitrary")),
    )(p, p, p, cache_k, cache_v, cos, s_up, s_dn, lq1, lk1, lq2, lk2, g)


@functools.lru_cache(None)
def _dft_consts(L):
    n = 2 * L
    k = np.arange(L)
    ang = 2.0 * np.pi * ((k[:, None] * k[None, :]) % n) / n
    alt = (-1.0) ** k
    fa = np.cos(ang)
    fb = -np.sin(ang)
    fb[0, :] = alt
    wgt = np.full((L,), 2.0 / n)
    wgt[0] = 1.0 / n
    ga = fa * wgt[:, None]
    gb = fb * wgt[:, None]
    gb[0, :] = alt / n
    f = np.concatenate([fa, fb], axis=0)
    g = np.concatenate([ga.T, gb.T], axis=1)
    return np.asarray(f, dtype=BF16), np.asarray(g, dtype=BF16)


@functools.lru_cache(None)
def _filter_consts(L):
    f32 = np.float32
    t = np.linspace(0.0, 1.0, L, dtype=f32)[:, None]
    pos = np.arange(L, dtype=f32)[:, None]
    bands = np.linspace(1e-4, HY_BANDS - 1, HY_BANDS, dtype=f32)[None, :]
    ang = f32(2.0 * math.pi) * bands * pos / f32(L)
    z = np.zeros((L, 128), f32)
    z[:, 0:1] = t
    z[:, 1:1 + HY_BANDS] = np.cos(ang)
    z[:, 1 + HY_BANDS:HY_EMB] = -np.sin(ang)
    min_decay = math.log(1e-2) / 1.5
    max_decay = math.log(1e-2) / 0.3
    deltas = np.abs(np.linspace(min_decay, max_decay, HY_W, dtype=f32))
    decay = np.exp(-t * deltas[None, :]).astype(f32)
    return z, decay


def _spectra_kernel(z_ref, w1_ref, b1_ref, w2_ref, b2_ref, fr_ref, w3_ref, dec_ref, fa_ref, fb_ref,
                    sa_ref, sb_ref, filt_scr, *, L, kc):
    j = pl.program_id(0)

    @pl.when(j == 0)
    def _():
        fr = fr_ref[...]
        hdn = jnp.sin(fr * (_dot3(z_ref[...], w1_ref[...]) + b1_ref[...]))
        hdn = jnp.sin(fr * (_dot3(hdn, w2_ref[...]) + b2_ref[...]))
        dec = dec_ref[...]
        not_first = lax.broadcasted_iota(jnp.int32, (L, HY_W), 0) > 0
        for o in range(2):
            hf = _dot3(hdn, w3_ref[:, 512 * o:512 * o + 256]) * dec
            hb = jnp.where(not_first, _dot3(hdn, w3_ref[:, 512 * o + 256:512 * o + 512]) * dec, 0.0)
            nrm = (jnp.sum(jnp.abs(hf), axis=0, keepdims=True)
                   + jnp.sum(jnp.abs(hb), axis=0, keepdims=True))
            filt_scr[:, 512 * o:512 * o + 256] = (hf / nrm).astype(BF16)
            filt_scr[:, 512 * o + 256:512 * o + 512] = (hb / nrm).astype(BF16)

    ta = _dot(fa_ref[...], filt_scr[...])
    tb = _dot(fb_ref[...], filt_scr[...])
    first = (lax.broadcasted_iota(jnp.int32, (kc, HY_W), 0) + j * kc) == 0
    for o in range(2):
        af, ab = ta[:, 512 * o:512 * o + 256], ta[:, 512 * o + 256:512 * o + 512]
        bf, bb = tb[:, 512 * o:512 * o + 256], tb[:, 512 * o + 256:512 * o + 512]
        sa_ref[:, 256 * o:256 * o + 256] = af + ab
        sb_ref[:, 256 * o:256 * o + 256] = jnp.where(first, bf + bb, bf - bb)


def _hy_spectra(L, kc, w1p, b1, w2, b2, fr, w3):
    z, decay = _filter_consts(L)
    f = jnp.asarray(_dft_consts(L)[0])
    nj = L // kc
    full = lambda shape: pl.BlockSpec(shape, lambda j: tuple(0 for _ in shape))
    return pl.pallas_call(
        functools.partial(_spectra_kernel, L=L, kc=kc),
        grid=(nj,),
        in_specs=[full((L, 128)), full((128, HY_HID)), full((1, HY_HID)), full((HY_HID, HY_HID)),
                  full((1, HY_HID)), full((1, HY_HID)), full((HY_HID, 4 * HY_W)), full((L, HY_W)),
                  pl.BlockSpec((kc, L), lambda j: (j, 0)),
                  pl.BlockSpec((kc, L), lambda j: (j + nj, 0))],
        out_specs=[pl.BlockSpec((kc, 2 * HY_W), lambda j: (j, 0)),
                   pl.BlockSpec((kc, 2 * HY_W), lambda j: (j, 0))],
        out_shape=[jax.ShapeDtypeStruct((L, 2 * HY_W), F32)] * 2,
        scratch_shapes=[pltpu.VMEM((L, 4 * HY_W), BF16)],
        compiler_params=_cparams(("arbitrary",)),
    )(jnp.asarray(z), w1p, b1, w2, b2, fr, w3, jnp.asarray(decay), f, f)


def _hyconv_kernel(*refs, L, nb, kc, nj, row_blk0, staged):
    n_in = 1 if staged else nb
    (cw_ref, cb_ref, d_ref, fa_ref, fb_ref, ga_ref, gb_ref, sa_ref, sb_ref, o_ref,
     xin_scr, x_scr, g1_scr, g2_scr, y_scr) = refs[n_in:n_in + 15]
    bg = pl.program_id(0)
    o = pl.program_id(1)
    j = pl.program_id(2)

    @pl.when((o == 0) & (j == 0))
    def _():
        row = lax.broadcasted_iota(jnp.int32, (L, HY_W), 0)
        pieces = [(i, part) for i in range(nb) for part in range(3)]
        if staged:
            p_hbm, (stage, sem) = refs[0], refs[n_in + 15:]

            def piece_copy(k):
                i, part = pieces[k]
                row0 = pl.multiple_of((row_blk0 + bg * nb + i) * L, L)
                return pltpu.make_async_copy(
                    p_hbm.at[pl.ds(row0, L), C_HY + HY_W * part:C_HY + HY_W * (part + 1)],
                    stage.at[k % 2], sem.at[k % 2])

            piece_copy(0).start()
            piece_copy(1).start()
        for k, (i, part) in enumerate(pieces):
            cols = slice(HY_W * i, HY_W * (i + 1))
            dst = (x_scr, g1_scr, g2_scr)[part]
            pc = slice(HY_W * part, HY_W * (part + 1))
            if staged:
                piece_copy(k).wait()
                u = stage[k % 2].astype(F32)
            else:
                u = refs[i][:, pc].astype(F32)
            up = jnp.where(row == 0, 0.0, pltpu.roll(u, 1, axis=0))
            un = jnp.where(row == L - 1, 0.0, pltpu.roll(u, L - 1, axis=0))
            dst[:, cols] = (up * cw_ref[0:1, pc] + u * cw_ref[1:2, pc] + un * cw_ref[2:3, pc]
                            + cb_ref[:, pc])
            if part == 0:
                xin_scr[:, cols] = x_scr[:, cols].astype(BF16)
            if staged and k + 2 < len(pieces):
                piece_copy(k + 2).start()

    @pl.when(j == 0)
    def _():
        y_scr[...] = jnp.zeros_like(y_scr)

    xa = _dot(fa_ref[...], xin_scr[...])
    xb = _dot(fb_ref[...], xin_scr[...])
    sa = jnp.concatenate([sa_ref[...]] * nb, axis=1)
    sb = jnp.concatenate([sb_ref[...]] * nb, axis=1)
    first = (lax.broadcasted_iota(jnp.int32, (kc, nb * HY_W), 0) + j * kc) == 0
    ya = jnp.where(first, xa * sa, xa * sa - xb * sb)
    yb = jnp.where(first, xb * sb, xa * sb + xb * sa)
    y_scr[...] += _dot(ga_ref[...], ya.astype(BF16)) + _dot(gb_ref[...], yb.astype(BF16))

    @pl.when(j == nj - 1)
    def _():
        dvec = jnp.concatenate([d_ref[pl.ds(o, 1), :]] * nb, axis=1)
        y = y_scr[...] + x_scr[...] * dvec

        @pl.when(o == 0)
        def _():
            zz = g1_scr[...] * y
            x_scr[...] = zz
            xin_scr[...] = zz.astype(BF16)

        @pl.when(o == 1)
        def _():
            res = g2_scr[...] * y
            for i in range(nb):
                o_ref[L * i:L * (i + 1), :] = res[:, HY_W * i:HY_W * (i + 1)].astype(BF16)


def _hy_conv(p, L, nb, kc, row_blk0, n_seq, conv_w, conv_b, d, sa, sb):
    f, g = (jnp.asarray(a) for a in _dft_consts(L))
    nj = L // kc
    staged = nb * L > DEC_SEQ
    if staged:
        u_specs = [pl.BlockSpec(memory_space=pl.ANY)]
        u_scratch = [pltpu.VMEM((2, L, HY_W), p.dtype), pltpu.SemaphoreType.DMA((2,))]
    else:
        u_specs = [pl.BlockSpec((L, 3 * HY_W), lambda bg, o, j, i=i: (row_blk0 + bg * nb + i, C_HY // 768))
                   for i in range(nb)]
        u_scratch = []
    small = lambda shape: pl.BlockSpec(shape, lambda bg, o, j: (0, 0))
    return pl.pallas_call(
        functools.partial(_hyconv_kernel, L=L, nb=nb, kc=kc, nj=nj, row_blk0=row_blk0, staged=staged),
        grid=(n_seq // nb, 2, nj),
        in_specs=u_specs + [
            small((3, 3 * HY_W)), small((1, 3 * HY_W)), small((2, HY_W)),
            pl.BlockSpec((kc, L), lambda bg, o, j: (j, 0)),
            pl.BlockSpec((kc, L), lambda bg, o, j: (j + nj, 0)),
            pl.BlockSpec((L, kc), lambda bg, o, j: (0, j)),
            pl.BlockSpec((L, kc), lambda bg, o, j: (0, j + nj)),
            pl.BlockSpec((kc, HY_W), lambda bg, o, j: (j, o)),
            pl.BlockSpec((kc, HY_W), lambda bg, o, j: (j, o))],
        out_specs=pl.BlockSpec((nb * L, HY_W), lambda bg, o, j: (bg, 0)),
        out_shape=jax.ShapeDtypeStruct((n_seq * L, HY_W), BF16),
        scratch_shapes=[pltpu.VMEM((L, nb * HY_W), BF16)] + [pltpu.VMEM((L, nb * HY_W), F32)] * 4
        + u_scratch,
        compiler_params=_cparams(("arbitrary", "arbitrary", "arbitrary")),
    )(*([p] * len(u_specs)), conv_w, conv_b, d, f, f, g, g, sa, sb)


def _outproj_kernel(xp_ref, xs_ref, mixp_ref, hyp_ref, nas_ref, ds_ref, hys_ref, g1_ref, w_ref, o_ref):
    i = pl.program_id(0)
    w = lambda a, b: w_ref[a:b, :].astype(BF16)

    @pl.when(i < NP_TOK // TM)
    def _():
        y = _dot(mixp_ref[...], w(0, 768)) + _dot(hyp_ref[...], w(768, D))
        o_ref[...] = xp_ref[...] + g1_ref[...] * y

    @pl.when(i >= NP_TOK // TM)
    def _():
        y = _dot(nas_ref[...], w(0, 256)) + _dot(ds_ref[...], w(256, 768)) + _dot(hys_ref[...], w(768, D))
        o_ref[...] = xs_ref[...] + g1_ref[...] * y


def _x_specs(xs_block0):
    npt = NP_TOK // TM
    return [pl.BlockSpec((TM, D), lambda i, *_: (jnp.minimum(i, npt - 1), 0)),
            pl.BlockSpec((TM, D), lambda i, *_: (jnp.maximum(i - npt, 0) + xs_block0, 0))]


def _outproj(xp, xs, xs_block0, mix_p, hy_p, na_s, d_s, hy_s, mod6, w_out, l):
    npt = NP_TOK // TM
    pidx = lambda i: (jnp.minimum(i, npt - 1), 0)
    sidx = lambda i: (jnp.maximum(i - npt, 0), 0)
    return pl.pallas_call(
        _outproj_kernel,
        grid=(N_TOK // TM,),
        in_specs=_x_specs(xs_block0) + [
                  pl.BlockSpec((TM, 768), pidx),
                  pl.BlockSpec((TM, HY_W), pidx),
                  pl.BlockSpec((TM, 256), sidx),
                  pl.BlockSpec((TM, 512), sidx),
                  pl.BlockSpec((TM, HY_W), sidx),
                  _mod_spec(2, TM),
                  pl.BlockSpec((None, D, D), lambda i: (l, 0, 0))],
        out_specs=pl.BlockSpec((TM, D), lambda i: (i, 0)),
        out_shape=jax.ShapeDtypeStruct((N_TOK, D), F32),
        compiler_params=_cparams(("arbitrary",)),
    )(xp, xs, mix_p, hy_p, na_s, d_s, hy_s, mod6, w_out)


def _ffn_kernel(x_ref, g_ref, sc_ref, sh_ref, gate_ref, wg_ref, wu_ref, wd_ref, o_ref, h_scr, *, nj):
    j = pl.program_id(1)

    @pl.when(j == 0)
    def _():
        h_scr[...] = _norm_mod(x_ref[...], g_ref[...], sc_ref[...], sh_ref[...]).astype(BF16)
        o_ref[...] = jnp.zeros_like(o_ref)

    h = h_scr[...]
    a = _silu(_dot(h, wg_ref[...].astype(BF16))) * _dot(h, wu_ref[...].astype(BF16))
    o_ref[...] += _dot(a.astype(BF16), wd_ref[...].astype(BF16))

    @pl.when(j == nj - 1)
    def _():
        o_ref[...] = x_ref[...] + gate_ref[...] * o_ref[...]


def _dense_ffn(x, g, mod6, wg, wu, wd, i_ffn):
    tf = 256
    tm = DEC_SEQ
    nj = FFN // tf
    return pl.pallas_call(
        functools.partial(_ffn_kernel, nj=nj),
        grid=(N_TOK // tm, nj),
        in_specs=[pl.BlockSpec((tm, D), lambda i, j: (i, 0)),
                  pl.BlockSpec((1, D), lambda i, j: (0, 0)),
                  _mod_spec(4, tm), _mod_spec(3, tm), _mod_spec(5, tm),
                  pl.BlockSpec((None, D, tf), lambda i, j: (i_ffn, 0, j)),
                  pl.BlockSpec((None, D, tf), lambda i, j: (i_ffn, 0, j)),
                  pl.BlockSpec((None, tf, D), lambda i, j: (i_ffn, j, 0))],
        out_specs=pl.BlockSpec((tm, D), lambda i, j: (i, 0)),
        out_shape=jax.ShapeDtypeStruct((N_TOK, D), F32),
        scratch_shapes=[pltpu.VMEM((tm, D), BF16)],
        compiler_params=_cparams(("arbitrary", "arbitrary")),
    )(x, g, mod6, mod6, mod6, wg, wu, wd)


def _router_kernel(x_ref, g_ref, sc_ref, sh_ref, wr_ref, h_ref, r_ref):
    h = _norm_mod(x_ref[...], g_ref[...], sc_ref[...], sh_ref[...])
    h_ref[...] = h.astype(BF16)
    lane = lax.broadcasted_iota(jnp.int32, (TM, 128), 1)
    lg = jnp.where(lane < N_EXP, _dot3(h, wr_ref[...]), -jnp.inf)
    m1 = jnp.max(lg, axis=-1, keepdims=True)
    i1 = jnp.min(jnp.where(lg == m1, lane, 128), axis=-1, keepdims=True)
    lg2 = jnp.where(lane == i1, -jnp.inf, lg)
    m2 = jnp.max(lg2, axis=-1, keepdims=True)
    i2 = jnp.min(jnp.where(lg2 == m2, lane, 128), axis=-1, keepdims=True)
    e = jnp.exp(m2 - m1)
    w1 = 1.0 / (1.0 + e)
    w2 = e / (1.0 + e)
    r_ref[...] = jnp.where(lane == 0, i1.astype(F32),
                           jnp.where(lane == 1, i2.astype(F32),
                                     jnp.where(lane == 2, w1, jnp.where(lane == 3, w2, 0.0))))


def _router(x, g, mod6, wr_pad):
    return pl.pallas_call(
        _router_kernel,
        grid=(N_TOK // TM,),
        in_specs=[pl.BlockSpec((TM, D), lambda i: (i, 0)),
                  pl.BlockSpec((1, D), lambda i: (0, 0)),
                  _mod_spec(4, TM), _mod_spec(3, TM),
                  pl.BlockSpec((D, 128), lambda i: (0, 0))],
        out_specs=[pl.BlockSpec((TM, D), lambda i: (i, 0)),
                   pl.BlockSpec((TM, 128), lambda i: (i, 0))],
        out_shape=[jax.ShapeDtypeStruct((N_TOK, D), BF16), jax.ShapeDtypeStruct((N_TOK, 128), F32)],
        compiler_params=_cparams(("arbitrary",)),
    )(x, g, mod6, mod6, wr_pad)


def _dispatch_kernel(blo_ref, bhi_ref, sexp_ref, h_hbm, rp_ref, o_ref, h_scr, sem):
    s = pl.program_id(0)

    @pl.when(s == 0)
    def _():
        cp = pltpu.make_async_copy(h_hbm, h_scr, sem)
        cp.start()
        cp.wait()

    e = sexp_ref[s]
    rows = s * MOE_TS + lax.broadcasted_iota(jnp.int32, (MOE_TS, MOE_DTB), 0)

    def sel(b):
        return (rows == rp_ref[e, pl.ds(b, 1), :]).astype(BF16)

    is_empty = blo_ref[s] > bhi_ref[s]

    @pl.when(is_empty)
    def _():
        o_ref[...] = jnp.zeros_like(o_ref)

    @pl.when(jnp.logical_not(is_empty))
    def _():
        b0 = jnp.minimum(blo_ref[s], N_TOK // MOE_DTB - MOE_DWIN)
        hwin = h_scr[pl.ds(pl.multiple_of(b0 * MOE_DTB, MOE_DTB), MOE_DWIN * MOE_DTB), :]
        acc = _dot(jnp.concatenate([sel(b0 + k) for k in range(MOE_DWIN)], axis=1), hwin)

        def body(b, acc):
            hb = h_scr[pl.ds(pl.multiple_of(b * MOE_DTB, MOE_DTB), MOE_DTB), :]
            return acc + _dot(sel(b), hb)

        acc = lax.fori_loop(b0 + MOE_DWIN, bhi_ref[s] + 1, body, acc)
        o_ref[...] = acc.astype(BF16)


def _dispatch(blo, bhi, sexp, h, rp_t):
    return pl.pallas_call(
        _dispatch_kernel,
        grid_spec=pltpu.PrefetchScalarGridSpec(
            num_scalar_prefetch=3,
            grid=(MOE_ROWS // MOE_TS,),
            in_specs=[pl.BlockSpec(memory_space=pl.ANY),
                      pl.BlockSpec((N_EXP, N_TOK // MOE_DTB, MOE_DTB), lambda s, *_: (0, 0, 0))],
            out_specs=pl.BlockSpec((MOE_TS, D), lambda s, *_: (s, 0)),
            scratch_shapes=[pltpu.VMEM((N_TOK, D), BF16), pltpu.SemaphoreType.DMA(())]),
        out_shape=jax.ShapeDtypeStruct((MOE_ROWS, D), BF16),
        compiler_params=_cparams(("arbitrary",)),
    )(blo, bhi, sexp, h, rp_t)


def _experts_kernel(te_ref, used_ref, rows_ref, xs_ref, wg_ref, wu_ref, wd_ref, o_ref, acc_scr, *, nj):
    i = pl.program_id(0)
    j = pl.program_id(1)
    nrows = rows_ref[i]

    @pl.when(j == 0)
    def _():
        acc_scr[...] = jnp.zeros_like(acc_scr)

    for sz in range(MOE_TQ, MOE_TM + 1, MOE_TQ):
        @pl.when(nrows == sz)
        def _():
            xb = xs_ref[0:sz, :]
            a = _silu(_dot(xb, wg_ref[...].astype(BF16))) * _dot(xb, wu_ref[...].astype(BF16))
            acc_scr[0:sz, :] += _dot(a.astype(BF16), wd_ref[...].astype(BF16))

    @pl.when(j == nj - 1)
    def _():
        o_ref[...] = acc_scr[...].astype(BF16)


def _experts(tile_expert, used, tile_rows, xs, wg, wu, wd, i_moe):
    nj = EXP_DIM // MOE_TF

    def tile(i, used):
        return jnp.minimum(i, used[0] - 1)

    def chunk(i, j, used):
        return jnp.where(i < used[0], j, nj - 1)

    return pl.pallas_call(
        functools.partial(_experts_kernel, nj=nj),
        grid_spec=pltpu.PrefetchScalarGridSpec(
            num_scalar_prefetch=3,
            grid=(MOE_YROWS // MOE_TM, nj),
            in_specs=[pl.BlockSpec((MOE_TM, D), lambda i, j, te, used, tr: (tile(i, used), 0)),
                      pl.BlockSpec((None, None, D, MOE_TF),
                                   lambda i, j, te, used, tr: (i_moe, te[tile(i, used)], 0, chunk(i, j, used))),
                      pl.BlockSpec((None, None, D, MOE_TF),
                                   lambda i, j, te, used, tr: (i_moe, te[tile(i, used)], 0, chunk(i, j, used))),
                      pl.BlockSpec((None, None, MOE_TF, D),
                                   lambda i, j, te, used, tr: (i_moe, te[tile(i, used)], chunk(i, j, used), 0))],
            out_specs=pl.BlockSpec((MOE_TM, D), lambda i, j, te, used, tr: (i, 0)),
            scratch_shapes=[pltpu.VMEM((MOE_TM, D), F32)]),
        out_shape=jax.ShapeDtypeStruct((MOE_YROWS, D), BF16),
        compiler_params=_cparams(("arbitrary", "arbitrary")),
    )(tile_expert, used, tile_rows, xs, wg, wu, wd)


def _combine_kernel(ws_ref, kind_ref, x_ref, rp_ref, comb_ref, gate_ref, fg_ref, ys_hbm, op_ref, os_ref,
                    win_scr, y_scr, sem):
    b = pl.program_id(0)
    nb = pl.num_programs(0)
    slot = b % 2

    def win_copy(blk, sl, e):
        start = pl.multiple_of(ws_ref[blk * N_EXP + e], MOE_ALIGN)
        return pltpu.make_async_copy(ys_hbm.at[pl.ds(start, MOE_WIN)], win_scr.at[sl, e], sem.at[sl, e])

    @pl.when(b == 0)
    def _():
        for e in range(N_EXP):
            win_copy(0, 0, e).start()

    @pl.when(b + 1 < nb)
    def _():
        for e in range(N_EXP):
            win_copy(b + 1, 1 - slot, e).start()

    col = lax.broadcasted_iota(jnp.int32, (MOE_TB, MOE_WIN), 1)
    y_scr[...] = jnp.zeros_like(y_scr)
    for e in range(N_EXP):
        win_copy(b, slot, e).wait()
        rel = rp_ref[:, e:e + 1] - ws_ref[b * N_EXP + e]
        cw = comb_ref[:, e:e + 1]
        kind = kind_ref[b * N_EXP + e]

        for k, wn in enumerate(MOE_WINS):
            @pl.when(kind == k)
            def _():
                sel = (rel == col[:, :wn]).astype(BF16) if wn == MOE_WIN else (
                    rel == lax.broadcasted_iota(jnp.int32, (MOE_TB, wn), 1)).astype(BF16)
                y_scr[...] += cw * _dot(sel, win_scr[slot, e, 0:wn, :])

    x = x_ref[...] + gate_ref[...] * y_scr[...]
    x = (x * lax.rsqrt(jnp.mean(x * x, axis=-1, keepdims=True) + EPS)) * fg_ref[...]

    @pl.when(b < NP_TOK // MOE_TB)
    def _():
        op_ref[...] = x

    @pl.when(b >= NP_TOK // MOE_TB)
    def _():
        os_ref[...] = x


def _combine(ws, kind, x, rp8, comb, mod6, final_g, ys):
    npb = NP_TOK // MOE_TB
    return pl.pallas_call(
        _combine_kernel,
        grid_spec=pltpu.PrefetchScalarGridSpec(
            num_scalar_prefetch=2,
            grid=(N_TOK // MOE_TB,),
            in_specs=[pl.BlockSpec((MOE_TB, D), lambda b, *_: (b, 0)),
                      pl.BlockSpec((MOE_TB, N_EXP), lambda b, *_: (b, 0)),
                      pl.BlockSpec((MOE_TB, N_EXP), lambda b, *_: (b, 0)),
                      _mod_spec(5, MOE_TB),
                      pl.BlockSpec((1, D), lambda b, *_: (0, 0)),
                      pl.BlockSpec(memory_space=pl.ANY)],
            out_specs=[pl.BlockSpec((MOE_TB, D), lambda b, *_: (jnp.minimum(b, npb - 1), 0)),
                       pl.BlockSpec((MOE_TB, D), lambda b, *_: (jnp.maximum(b - npb, 0), 0))],
            scratch_shapes=[pltpu.VMEM((2, N_EXP, MOE_WIN, D), BF16),
                            pltpu.VMEM((MOE_TB, D), F32),
                            pltpu.SemaphoreType.DMA((2, N_EXP))]),
        out_shape=[jax.ShapeDtypeStruct((NP_TOK, D), F32), jax.ShapeDtypeStruct((NS_TOK, D), F32)],
        compiler_params=_cparams(("arbitrary",)),
    )(ws, kind, x, rp8, comb, mod6, final_g, ys)


def _moe(x, g, mod6, router, wg, wu, wd, i_moe, final_g):
    wr_pad = jnp.pad(router, ((0, 0), (0, 128 - N_EXP)))
    h, r = _router(x, g, mod6, wr_pad)

    i32 = jnp.int32
    i12 = r[:, 0:2].astype(i32)
    earange = jnp.arange(N_EXP, dtype=i32)
    hit1 = i12[:, 0:1] == earange[None, :]
    hit2 = i12[:, 1:2] == earange[None, :]
    comb = jnp.where(hit1, r[:, 2:3], 0.0) + jnp.where(hit2, r[:, 3:4], 0.0)
    mask = (hit1 | hit2).astype(i32)
    csum = jnp.cumsum(mask, axis=0)
    counts = csum[-1]
    padded = ((counts + MOE_TM - 1) // MOE_TM) * MOE_TM
    ends = jnp.cumsum(padded)
    starts = ends - padded
    rp8 = jnp.where(mask > 0, starts[None, :] + csum - 1, -1).astype(i32)
    n_tiles = MOE_ROWS // MOE_TM
    tile_expert = jnp.minimum(
        jnp.searchsorted(ends, jnp.arange(n_tiles, dtype=i32) * MOE_TM, side="right"),
        N_EXP - 1).astype(i32)
    used = (ends[-1:] // MOE_TM).astype(i32)
    tile_row0 = jnp.arange(MOE_YROWS // MOE_TM, dtype=i32) * MOE_TM
    te_all = jnp.minimum(jnp.searchsorted(ends, tile_row0, side="right"), N_EXP - 1)
    valid = jnp.clip(counts[te_all] - (tile_row0 - starts[te_all]), 0, MOE_TM)
    valid = jnp.where(tile_row0 < ends[-1], valid, 0)
    tile_rows = (((valid + MOE_TQ - 1) // MOE_TQ) * MOE_TQ).astype(i32)

    sub_row0 = jnp.arange(MOE_ROWS // MOE_TS, dtype=i32) * MOE_TS
    sexp = tile_expert[sub_row0 // MOE_TM]
    qlo = sub_row0 - starts[sexp]
    qend = jnp.minimum(qlo + MOE_TS, counts[sexp])
    cbe = csum[MOE_DTB - 1::MOE_DTB, :].T[sexp]
    blo = jnp.sum((cbe <= qlo[:, None]).astype(i32), axis=1)
    bhi = jnp.minimum(jnp.sum((cbe < qend[:, None]).astype(i32), axis=1), N_TOK // MOE_DTB - 1)
    empty = qend <= qlo
    blo = jnp.where(empty, 1, blo).astype(i32)
    bhi = jnp.where(empty, 0, bhi).astype(i32)

    cb = csum[MOE_TB - 1::MOE_TB, :]
    cprev = jnp.concatenate([jnp.zeros((1, N_EXP), i32), cb[:-1]], axis=0)
    ws = (((starts[None, :] + cprev) // MOE_ALIGN) * MOE_ALIGN).reshape(-1).astype(i32)
    n_be = cb - cprev
    kind = sum((n_be > wn - MOE_ALIGN).astype(i32) for wn in MOE_WINS[:-1])
    kind = jnp.where(n_be == 0, len(MOE_WINS), kind).reshape(-1).astype(i32)

    rp_t = rp8.T.reshape(N_EXP, N_TOK // MOE_DTB, MOE_DTB)
    xs = _dispatch(blo, bhi, sexp, h, rp_t)
    ys = _experts(tile_expert, used, tile_rows, xs, wg, wu, wd, i_moe)
    return _combine(ws, kind, x, rp8, comb, mod6, final_g, ys)


assert DEPTH == 2

def kernel(x_prompt, x_sample, cache_na_k, cache_na_v, cache_diff_k, cache_diff_v, c, c_ctx, w_in, w_out, ada_w, ada_b, norm_mix_g, norm_ffn_g, na_rpb, diff_lq1, diff_lk1, diff_lq2, diff_lk2, diff_subln_g, hy_conv_w, hy_conv_b, hy_d, hy_f_w1, hy_f_b1, hy_f_w2, hy_f_b2, hy_f_freq, hy_f_w3, ffn_w_gate, ffn_w_up, ffn_w_down, moe_router, moe_w_gate, moe_w_up, moe_w_down, final_norm_g):
    xparts = (x_prompt.reshape(NP_TOK, D), x_sample.reshape(NS_TOK, D), 0)
    cond8 = jnp.concatenate([c_ctx[None, :], c, jnp.zeros((5, D), F32)], axis=0)
    mods = _modulation(cond8, ada_w, ada_b)
    final_g = final_norm_g.reshape(1, D)
    bias = _na_bias(na_rpb)

    leaves = [jnp.zeros((BATCH, DEPTH, NA_HEADS, SEQ, HEAD_DIM), F32),
              jnp.zeros((BATCH, DEPTH, NA_HEADS, SEQ, HEAD_DIM), F32),
              jnp.zeros((BATCH, DEPTH, DIFF_HEADS, 2, SEQ, HEAD_DIM), F32),
              jnp.zeros((BATCH, DEPTH, DIFF_HEADS, SEQ, DIFF_V), F32)]
    for l in range(DEPTH):
        lam_init = 0.8 - 0.6 * math.exp(-0.3 * l)
        mod6 = mods[l].reshape(8, 6, D).transpose(1, 0, 2).reshape(6, 8, 1, D)
        row = lambda a: a[l].reshape(1, -1)
        lq1, lk1, lq2, lk2, subg = row(diff_lq1), row(diff_lk1), row(diff_lq2), row(diff_lk2), row(diff_subln_g)

        p, *leaves = _inproj(*xparts, row(norm_mix_g), mod6, w_in, l, leaves)

        mix_p = _prompt_attn(p, lq1, lk1, lq2, lk2, subg, lam_init)
        na_s = _na_attn(p, cache_na_k, cache_na_v, bias, l)
        d_s = _sdiff_attn(p, cache_diff_k, cache_diff_v, lq1, lk1, lq2, lk2, subg, lam_init, l)

        w1p = jnp.pad(hy_f_w1[l], ((0, 128 - HY_EMB), (0, 0)))
        fargs = (w1p, row(hy_f_b1), hy_f_w2[l], row(hy_f_b2), row(hy_f_freq), hy_f_w3[l])
        cargs = (hy_conv_w[l], row(hy_conv_b), hy_d[l])
        sa_p, sb_p = _hy_spectra(SEQ, SEQ, *fargs)
        hy_p = _hy_conv(p, SEQ, 4, SEQ, 0, BATCH, *cargs, sa_p, sb_p)
        sa_s, sb_s = _hy_spectra(DEC_SEQ, 512, *fargs)
        hy_s = _hy_conv(p, DEC_SEQ, 2, 512, NP_TOK // DEC_SEQ, DEC_BATCH, *cargs, sa_s, sb_s)

        x = _outproj(*xparts, mix_p, hy_p, na_s, d_s, hy_s, mod6, w_out, l)

        if l == 0:
            x = _dense_ffn(x, row(norm_ffn_g), mod6, ffn_w_gate, ffn_w_up, ffn_w_down, 0)
            xparts = (x, x, NP_TOK // TM)
        else:
            yp, ys = _moe(x, row(norm_ffn_g), mod6, moe_router[0], moe_w_gate, moe_w_up, moe_w_down,
                          0, final_g)

    return (yp.reshape(BATCH, SEQ, D), ys.reshape(DEC_BATCH, DEC_SEQ, D), *leaves)
```

```python
import functools
import math

import numpy as np
import jax
import jax.numpy as jnp
from jax import lax
from jax.experimental import pallas as pl
from jax.experimental.pallas import tpu as pltpu

F32 = jnp.float32
BF16 = jnp.bfloat16

D = 1024
BATCH, SEQ = 32, 256
DEC_BATCH, DEC_SEQ = 2, 2048
DEPTH = 2
PAST = 512
GRID_W = 64
GRID_ROWS = DEC_SEQ // GRID_W
HEAD_DIM = 64
NA_HEADS = 4
DIFF_HEADS = 4
DIFF_V = 128
WIN_ROWS, WIN_COLS = 8, 16
HY_W = 256
HY_EMB = 33
HY_BANDS = 16
HY_HID = 64
PROJ = 3072
FFN = 2816
N_EXP = 8
EXP_DIM = 3584
EPS = 1e-6
ROPE_BASE = 10000.0

NP_TOK = BATCH * SEQ
NS_TOK = DEC_BATCH * DEC_SEQ
N_TOK = NP_TOK + NS_TOK

C_NAQ, C_NAK, C_NAV = 0, 256, 512
C_DQ, C_DK, C_DV = 768, 1280, 1792
C_HY = 2304

TM = 1024
VMEM_LIMIT = 56 * 1024 * 1024

MOE_TM = 1024
MOE_TF = 512
MOE_TQ = 256
MOE_ROWS = 2 * N_TOK + N_EXP * MOE_TM
MOE_TS = 256
MOE_DTB = 512
MOE_DWIN = 3
MOE_TB = 512
MOE_ALIGN = 16
MOE_WIN = MOE_TB + MOE_ALIGN
MOE_WINS = (MOE_TB // 4 + MOE_ALIGN, MOE_TB // 2 + MOE_ALIGN, MOE_WIN)
MOE_YROWS = MOE_ROWS + MOE_TM


def _cparams(sem):
    return pltpu.CompilerParams(dimension_semantics=sem, vmem_limit_bytes=VMEM_LIMIT)


def _dot(a, b):
    return jnp.dot(a, b, preferred_element_type=F32)


def _dot_nt(a, b):
    return lax.dot_general(a, b, (((1,), (1,)), ((), ())), preferred_element_type=F32)


def _split(a):
    hi = a.astype(BF16)
    lo = (a - hi.astype(F32)).astype(BF16)
    return hi, lo


def _dot3(a, b):
    ah, al = _split(a)
    bh, bl = _split(b)
    return _dot(ah, bh) + (_dot(ah, bl) + _dot(al, bh))


def _silu(x):
    return x / (1.0 + jnp.exp(-x))


def _mod_row(i, tm):
    t = i * tm
    return jnp.where(t < NP_TOK, 0, 1 + (t - NP_TOK) // DEC_SEQ)


def _mod_spec(k, tm):
    return pl.BlockSpec((None, None, 1, D), lambda i, *_: (k, _mod_row(i, tm), 0, 0))


def _norm_mod(x, g, sc, sh):
    y = x * lax.rsqrt(jnp.mean(x * x, axis=-1, keepdims=True) + EPS)
    return (y * g) * (1.0 + sc) + sh


def _mod_kernel(c_ref, w_ref, b_ref, o_ref):
    o_ref[...] = _dot3(_silu(c_ref[...]), w_ref[...]) + b_ref[...]


def _modulation(cond8, ada_w, ada_b):
    tn = 1536
    return pl.pallas_call(
        _mod_kernel,
        grid=(DEPTH, 6 * D // tn),
        in_specs=[pl.BlockSpec((8, D), lambda l, j: (0, 0)),
                  pl.BlockSpec((None, D, tn), lambda l, j: (l, 0, j)),
                  pl.BlockSpec((None, 1, tn), lambda l, j: (l, 0, j))],
        out_specs=pl.BlockSpec((None, 8, tn), lambda l, j: (l, 0, j)),
        out_shape=jax.ShapeDtypeStruct((DEPTH, 8, 6 * D), F32),
        compiler_params=_cparams(("arbitrary", "arbitrary")),
    )(cond8, ada_w, ada_b.reshape(DEPTH, 1, 6 * D))


def _inproj_kernel(xp_ref, xs_ref, g_ref, sc_ref, sh_ref, w_hbm, nak_in, nav_in, dk_in, dv_in,
                   o_ref, nak_ref, nav_ref, dk_ref, dv_ref, h_scr, w_scr, stage, acc_scr, sem, *, l, tn):
    del nak_in, nav_in, dk_in, dv_in
    i = pl.program_id(0)
    j = pl.program_id(1)

    @pl.when((i == 0) & (j == 0))
    def _():
        for c in range(PROJ // tn):
            cp = pltpu.make_async_copy(w_hbm.at[l, :, c * tn:(c + 1) * tn], stage, sem)
            cp.start()
            cp.wait()
            w_scr[c] = stage[...].astype(BF16)

    is_ctx = i < NP_TOK // TM

    @pl.when((j == 0) & is_ctx)
    def _():
        h_scr[...] = _norm_mod(xp_ref[...], g_ref[...], sc_ref[...], sh_ref[...]).astype(BF16)

    @pl.when((j == 0) & jnp.logical_not(is_ctx))
    def _():
        h_scr[...] = _norm_mod(xs_ref[...], g_ref[...], sc_ref[...], sh_ref[...]).astype(BF16)

    acc_scr[...] = _dot(h_scr[...], w_scr[j])
    o_ref[...] = acc_scr[...].astype(BF16)


    def rows(bb):
        return slice(bb * SEQ, (bb + 1) * SEQ)

    def copy_heads(dst_ref, c0, width, heads, sub=None):
        for bb in range(TM // SEQ):
            for h in heads:
                if sub is None:
                    dst_ref[bb, h] = acc_scr[rows(bb), c0(h):c0(h) + width]
                else:
                    for s in range(2):
                        dst_ref[bb, h, s] = acc_scr[rows(bb), c0(h) + s * width:c0(h) + (s + 1) * width]

    @pl.when(is_ctx & (j == 0))
    def _():
        copy_heads(nak_ref, lambda h: C_NAK + HEAD_DIM * h, HEAD_DIM, range(NA_HEADS))
        copy_heads(nav_ref, lambda h: C_NAV + HEAD_DIM * h, HEAD_DIM, range(NA_HEADS))

    @pl.when(is_ctx & (j == 1))
    def _():
        copy_heads(dk_ref, lambda h: C_DK - tn + 2 * HEAD_DIM * h, HEAD_DIM, range(0, 2), sub=True)

    @pl.when(is_ctx & (j == 2))
    def _():
        copy_heads(dk_ref, lambda h: C_DK - 2 * tn + 2 * HEAD_DIM * h, HEAD_DIM, range(2, 4), sub=True)
        copy_heads(dv_ref, lambda h: C_DV - 2 * tn + DIFF_V * h, DIFF_V, range(DIFF_HEADS))


def _inproj(xp, xs, xs_block0, g, mod6, w_in, l, leaves):
    tn = 768
    nb = TM // SEQ
    ctx = lambda i: jnp.minimum(i, NP_TOK // TM - 1)
    leaf_specs = [pl.BlockSpec((nb, None, NA_HEADS, SEQ, HEAD_DIM), lambda i, j: (ctx(i), l, 0, 0, 0)),
                  pl.BlockSpec((nb, None, NA_HEADS, SEQ, HEAD_DIM), lambda i, j: (ctx(i), l, 0, 0, 0)),
                  pl.BlockSpec((nb, None, DIFF_HEADS, 2, SEQ, HEAD_DIM), lambda i, j: (ctx(i), l, 0, 0, 0, 0)),
                  pl.BlockSpec((nb, None, DIFF_HEADS, SEQ, DIFF_V), lambda i, j: (ctx(i), l, 0, 0, 0))]
    return pl.pallas_call(
        functools.partial(_inproj_kernel, l=l, tn=tn),
        grid=(N_TOK // TM, PROJ // tn),
        in_specs=_x_specs(xs_block0) + [
                  pl.BlockSpec((1, D), lambda i, j: (0, 0)),
                  _mod_spec(1, TM), _mod_spec(0, TM),
                  pl.BlockSpec(memory_space=pl.ANY)] + [pl.BlockSpec(memory_space=pl.ANY)] * 4,
        out_specs=[pl.BlockSpec((TM, tn), lambda i, j: (i, j))] + leaf_specs,
        out_shape=[jax.ShapeDtypeStruct((N_TOK, PROJ), BF16)]
        + [jax.ShapeDtypeStruct(a.shape, a.dtype) for a in leaves],
        input_output_aliases={6: 1, 7: 2, 8: 3, 9: 4},
        scratch_shapes=[pltpu.VMEM((TM, D), BF16), pltpu.VMEM((PROJ // tn, D, tn), BF16),
                        pltpu.VMEM((D, tn), F32), pltpu.VMEM((TM, tn), F32), pltpu.SemaphoreType.DMA(())],
        compiler_params=_cparams(("arbitrary", "arbitrary")),
    )(xp, xs, g, mod6, mod6, w_in, *leaves)


def _lam(lq1, lk1, lq2, lk2, lam_init):
    return (jnp.exp(jnp.sum(lq1[...] * lk1[...], axis=-1, keepdims=True))
            - jnp.exp(jnp.sum(lq2[...] * lk2[...], axis=-1, keepdims=True)) + lam_init)


def _softmax_parts(s):
    m = jnp.max(s, axis=-1, keepdims=True)
    e = jnp.exp(s - m)
    return e, jnp.sum(e, axis=-1, keepdims=True)


QK_SCALE = HEAD_DIM ** -0.5


def _diff_head(q1, q2, k1, k2, v, lam, g, lam_init):
    e1, l1 = _softmax_parts(_dot_nt(q1, k1))
    e2, l2 = _softmax_parts(_dot_nt(q2, k2))
    a = e1 - (lam * l1 * (1.0 / l2)) * e2
    o = _dot(a.astype(BF16), v) * (1.0 / l1)
    o = o * lax.rsqrt(jnp.mean(o * o, axis=-1, keepdims=True) + EPS)
    return (o * g) * (1.0 - lam_init)


def _prompt_attn_kernel(pa_ref, pb_ref, pc_ref, lq1, lk1, lq2, lk2, g_ref, o_ref, *, lam_init):
    lam = _lam(lq1, lk1, lq2, lk2, lam_init)
    g = g_ref[...]

    def col(c0, w, scale=None):
        ref = (pa_ref, pb_ref, pc_ref)[c0 // 768]
        o = c0 % 768
        a = ref[:, o:o + w]
        return (a if scale is None else a * scale).astype(BF16)

    for h in range(NA_HEADS):
        q = col(C_NAQ + 64 * h, 64, QK_SCALE)
        k = col(C_NAK + 64 * h, 64)
        v = col(C_NAV + 64 * h, 64)
        e, l = _softmax_parts(_dot_nt(q, k))
        o = _dot(e.astype(BF16), v) * (1.0 / l)
        o_ref[:, 64 * h:64 * h + 64] = o.astype(BF16)
    for h in range(DIFF_HEADS):
        q1 = col(C_DQ + 128 * h, 64, QK_SCALE)
        q2 = col(C_DQ + 128 * h + 64, 64, QK_SCALE)
        k1 = col(C_DK + 128 * h, 64)
        k2 = col(C_DK + 128 * h + 64, 64)
        v = col(C_DV + 128 * h, 128)
        o = _diff_head(q1, q2, k1, k2, v, lam, g, lam_init)
        o_ref[:, 256 + 128 * h:384 + 128 * h] = o.astype(BF16)


def _prompt_attn(p, lq1, lk1, lq2, lk2, g, lam_init):
    vec = lambda n: pl.BlockSpec((1, n), lambda b: (0, 0))
    return pl.pallas_call(
        functools.partial(_prompt_attn_kernel, lam_init=lam_init),
        grid=(BATCH,),
        in_specs=[pl.BlockSpec((SEQ, 768), lambda b: (b, 0)),
                  pl.BlockSpec((SEQ, 768), lambda b: (b, 1)),
                  pl.BlockSpec((SEQ, 768), lambda b: (b, 2)),
                  vec(64), vec(64), vec(64), vec(64), vec(128)],
        out_specs=pl.BlockSpec((SEQ, 768), lambda b: (b, 0)),
        out_shape=jax.ShapeDtypeStruct((NP_TOK, 768), BF16),
        compiler_params=_cparams(("arbitrary",)),
    )(p, p, p, lq1, lk1, lq2, lk2, g)


def _bias_kernel(rpb_ref, o_ref):
    lh = pl.program_id(0)
    qc = lax.broadcasted_iota(jnp.int32, (GRID_W, GRID_W), 0)
    kc = lax.broadcasted_iota(jnp.int32, (GRID_W, GRID_W), 1)
    delta = jnp.clip(kc - qc + (WIN_COLS - 1), 0, 2 * WIN_COLS - 2)
    qs = jnp.clip(qc - WIN_COLS // 2, 0, GRID_W - WIN_COLS)
    in_win = (kc >= qs) & (kc < qs + WIN_COLS)
    for dr in range(2 * WIN_ROWS - 1):
        base = (lh * (2 * WIN_ROWS - 1) + dr) * (2 * WIN_COLS - 1)
        acc = jnp.zeros((GRID_W, GRID_W), F32)
        for d in range(2 * WIN_COLS - 1):
            acc = jnp.where(delta == d, rpb_ref[base + d], acc)
        piece = jnp.where(in_win, acc, -jnp.inf)
        for case in range(WIN_ROWS):
            i = dr + case - (WIN_ROWS - 1)
            if 0 <= i < WIN_ROWS:
                o_ref[case, :, i * GRID_W:(i + 1) * GRID_W] = piece


def _na_bias(rpb):
    return pl.pallas_call(
        _bias_kernel,
        grid=(DEPTH * NA_HEADS,),
        in_specs=[pl.BlockSpec(memory_space=pltpu.SMEM)],
        out_specs=pl.BlockSpec((None, WIN_ROWS, GRID_W, WIN_ROWS * GRID_W), lambda lh: (lh, 0, 0, 0)),
        out_shape=jax.ShapeDtypeStruct((DEPTH * NA_HEADS, WIN_ROWS, GRID_W, WIN_ROWS * GRID_W), F32),
        compiler_params=_cparams(("arbitrary",)),
    )(rpb.reshape(-1))


def _na_kernel(q_ref, kv_ref, kc_ref, vc_ref, bias_ref, o_ref):
    nloc = WIN_ROWS * GRID_W
    for rr in range(NA_RB):
        r = pl.program_id(1) * NA_RB + rr
        start = jnp.clip(r - WIN_ROWS // 2, 0, GRID_ROWS - WIN_ROWS)
        case = r - start
        row0 = pl.multiple_of(start * GRID_W, GRID_W)
        qrows = slice(rr * GRID_W, (rr + 1) * GRID_W)
        for h in range(NA_HEADS):
            q = (q_ref[qrows, C_NAQ + 64 * h:C_NAQ + 64 * h + 64] * QK_SCALE).astype(BF16)
            k = kv_ref[pl.ds(row0, nloc), C_NAK + 64 * h:C_NAK + 64 * h + 64].astype(BF16)
            v = kv_ref[pl.ds(row0, nloc), C_NAV + 64 * h:C_NAV + 64 * h + 64].astype(BF16)
            s_loc = _dot_nt(q, k) + bias_ref[h, pl.ds(case, 1)][0]
            s_ctx = _dot_nt(q, kc_ref[h].astype(BF16))
            m = jnp.maximum(jnp.max(s_loc, axis=-1, keepdims=True), jnp.max(s_ctx, axis=-1, keepdims=True))
            e_loc = jnp.exp(s_loc - m)
            e_ctx = jnp.exp(s_ctx - m)
            l = jnp.sum(e_loc, axis=-1, keepdims=True) + jnp.sum(e_ctx, axis=-1, keepdims=True)
            o = (_dot(e_loc.astype(BF16), v) + _dot(e_ctx.astype(BF16), vc_ref[h].astype(BF16))) * (1.0 / l)
            o_ref[qrows, 64 * h:64 * h + 64] = o.astype(BF16)


NA_RB = 4


def _na_attn(p, cache_k, cache_v, bias, l):
    qblk0 = NP_TOK // (NA_RB * GRID_W)
    kvblk0 = NP_TOK // DEC_SEQ
    nrg = GRID_ROWS // NA_RB
    return pl.pallas_call(
        _na_kernel,
        grid=(DEC_BATCH, nrg),
        in_specs=[pl.BlockSpec((NA_RB * GRID_W, 768), lambda b, r: (qblk0 + b * nrg + r, 0)),
                  pl.BlockSpec((DEC_SEQ, 768), lambda b, r: (kvblk0 + b, 0)),
                  pl.BlockSpec((None, None, NA_HEADS, PAST, HEAD_DIM), lambda b, r: (b, l, 0, 0, 0)),
                  pl.BlockSpec((None, None, NA_HEADS, PAST, HEAD_DIM), lambda b, r: (b, l, 0, 0, 0)),
                  pl.BlockSpec((NA_HEADS, WIN_ROWS, GRID_W, WIN_ROWS * GRID_W), lambda b, r: (l, 0, 0, 0))],
        out_specs=pl.BlockSpec((NA_RB * GRID_W, 256), lambda b, r: (b * nrg + r, 0)),
        out_shape=jax.ShapeDtypeStruct((NS_TOK, 256), BF16),
        compiler_params=_cparams(("arbitrary", "arbitrary")),
    )(p, p, cache_k, cache_v, bias)


@functools.lru_cache(None)
def _rope_tables():
    t = np.arange(DEC_SEQ)
    lane = np.arange(128)
    dd = lane % HEAD_DIM
    pos = np.where(dd[None, :] < 32, (t // GRID_W)[:, None], (t % GRID_W)[:, None]).astype(np.float64)
    inv = ROPE_BASE ** (-(dd % 16).astype(np.float64) * 2.0 / 32.0)
    ang = pos * inv[None, :]
    first = (dd % 32) < 16
    cos = np.cos(ang)
    s_up = np.where(first[None, :], -np.sin(ang), 0.0)
    s_dn = np.where(first[None, :], 0.0, np.sin(ang))
    return tuple(np.asarray(a, np.float32) for a in (cos, s_up, s_dn))


def _rope(x, cos, s_up, s_dn):
    return x * cos + pltpu.roll(x, 112, axis=1) * s_up + pltpu.roll(x, 16, axis=1) * s_dn


def _sdiff_kernel(q_ref, k_ref, v_ref, ck_ref, cv_ref, cos_ref, sup_ref, sdn_ref,
                  lq1, lk1, lq2, lk2, g_ref, o_ref, k1_scr, k2_scr, v_scr, *, lam_init, tq):
    qb = pl.program_id(2)

    @pl.when(qb == 0)
    def _():
        kr = _rope(k_ref[...].astype(F32), cos_ref[...], sup_ref[...], sdn_ref[...])
        k1_scr[0:DEC_SEQ, :] = kr[:, :64].astype(BF16)
        k2_scr[0:DEC_SEQ, :] = kr[:, 64:].astype(BF16)
        k1_scr[DEC_SEQ:, :] = ck_ref[0].astype(BF16)
        k2_scr[DEC_SEQ:, :] = ck_ref[1].astype(BF16)
        v_scr[0:DEC_SEQ, :] = v_ref[...].astype(BF16)
        v_scr[DEC_SEQ:, :] = cv_ref[...].astype(BF16)

    lam = _lam(lq1, lk1, lq2, lk2, lam_init)
    half = tq // 2
    for c in range(2):
        rows = pl.ds(pl.multiple_of(qb * tq + c * half, half), half)
        qr = _rope(q_ref[c * half:(c + 1) * half, :].astype(F32), cos_ref[rows, :], sup_ref[rows, :],
                   sdn_ref[rows, :]) * QK_SCALE
        o = _diff_head(qr[:, :64].astype(BF16), qr[:, 64:].astype(BF16), k1_scr[...], k2_scr[...],
                       v_scr[...], lam, g_ref[...], lam_init)
        o_ref[c * half:(c + 1) * half, :] = o.astype(BF16)


def _sdiff_attn(p, cache_k, cache_v, lq1, lk1, lq2, lk2, g, lam_init, l):
    tq = 512
    nq = DEC_SEQ // tq
    cos, s_up, s_dn = (jnp.asarray(a) for a in _rope_tables())
    vec = lambda n: pl.BlockSpec((1, n), lambda b, h, q: (0, 0))
    tab = pl.BlockSpec((DEC_SEQ, 128), lambda b, h, q: (0, 0))
    kvblk0 = NP_TOK // DEC_SEQ
    return pl.pallas_call(
        functools.partial(_sdiff_kernel, lam_init=lam_init, tq=tq),
        grid=(DEC_BATCH, DIFF_HEADS, nq),
        in_specs=[pl.BlockSpec((tq, 128), lambda b, h, q: (NP_TOK // tq + b * nq + q, C_DQ // 128 + h)),
                  pl.BlockSpec((DEC_SEQ, 128), lambda b, h, q: (kvblk0 + b, C_DK // 128 + h)),
                  pl.BlockSpec((DEC_SEQ, 128), lambda b, h, q: (kvblk0 + b, C_DV // 128 + h)),
                  pl.BlockSpec((None, None, None, 2, PAST, HEAD_DIM), lambda b, h, q: (b, l, h, 0, 0, 0)),
                  pl.BlockSpec((None, None, None, PAST, DIFF_V), lambda b, h, q: (b, l, h, 0, 0)),
                  tab, tab, tab, vec(64), vec(64), vec(64), vec(64), vec(128)],
        out_specs=pl.BlockSpec((tq, 128), lambda b, h, q: (b * nq + q, h)),
        out_shape=jax.ShapeDtypeStruct((NS_TOK, DIFF_HEADS * DIFF_V), BF16),
        scratch_shapes=[pltpu.VMEM((DEC_SEQ + PAST, HEAD_DIM), BF16),
                        pltpu.VMEM((DEC_SEQ + PAST, HEAD_DIM), BF16),
                        pltpu.VMEM((DEC_SEQ + PAST, DIFF_V), BF16)],
        compiler_params=_cparams(("arbitrary", "arbitrary", "arbitrary")),
    )(p, p, p, cache_k, cache_v, cos, s_up, s_dn, lq1, lk1, lq2, lk2, g)


@functools.lru_cache(None)
def _dft_consts(L):
    n = 2 * L
    k = np.arange(L)
    ang = 2.0 * np.pi * ((k[:, None] * k[None, :]) % n) / n
    alt = (-1.0) ** k
    fa = np.cos(ang)
    fb = -np.sin(ang)
    fb[0, :] = alt
    wgt = np.full((L,), 2.0 / n)
    wgt[0] = 1.0 / n
    ga = fa * wgt[:, None]
    gb = fb * wgt[:, None]
    gb[0, :] = alt / n
    f = np.concatenate([fa, fb], axis=0)
    g = np.concatenate([ga.T, gb.T], axis=1)
    return np.asarray(f, dtype=BF16), np.asarray(g, dtype=BF16)


@functools.lru_cache(None)
def _filter_consts(L):
    f32 = np.float32
    t = np.linspace(0.0, 1.0, L, dtype=f32)[:, None]
    pos = np.arange(L, dtype=f32)[:, None]
    bands = np.linspace(1e-4, HY_BANDS - 1, HY_BANDS, dtype=f32)[None, :]
    ang = f32(2.0 * math.pi) * bands * pos / f32(L)
    z = np.zeros((L, 128), f32)
    z[:, 0:1] = t
    z[:, 1:1 + HY_BANDS] = np.cos(ang)
    z[:, 1 + HY_BANDS:HY_EMB] = -np.sin(ang)
    min_decay = math.log(1e-2) / 1.5
    max_decay = math.log(1e-2) / 0.3
    deltas = np.abs(np.linspace(min_decay, max_decay, HY_W, dtype=f32))
    decay = np.exp(-t * deltas[None, :]).astype(f32)
    return z, decay


def _spectra_kernel(z_ref, w1_ref, b1_ref, w2_ref, b2_ref, fr_ref, w3_ref, dec_ref, fa_ref, fb_ref,
                    sa_ref, sb_ref, filt_scr, *, L, kc):
    j = pl.program_id(0)

    @pl.when(j == 0)
    def _():
        fr = fr_ref[...]
        hdn = jnp.sin(fr * (_dot3(z_ref[...], w1_ref[...]) + b1_ref[...]))
        hdn = jnp.sin(fr * (_dot3(hdn, w2_ref[...]) + b2_ref[...]))
        dec = dec_ref[...]
        not_first = lax.broadcasted_iota(jnp.int32, (L, HY_W), 0) > 0
        for o in range(2):
            hf = _dot3(hdn, w3_ref[:, 512 * o:512 * o + 256]) * dec
            hb = jnp.where(not_first, _dot3(hdn, w3_ref[:, 512 * o + 256:512 * o + 512]) * dec, 0.0)
            nrm = (jnp.sum(jnp.abs(hf), axis=0, keepdims=True)
                   + jnp.sum(jnp.abs(hb), axis=0, keepdims=True))
            filt_scr[:, 512 * o:512 * o + 256] = (hf / nrm).astype(BF16)
            filt_scr[:, 512 * o + 256:512 * o + 512] = (hb / nrm).astype(BF16)

    ta = _dot(fa_ref[...], filt_scr[...])
    tb = _dot(fb_ref[...], filt_scr[...])
    first = (lax.broadcasted_iota(jnp.int32, (kc, HY_W), 0) + j * kc) == 0
    for o in range(2):
        af, ab = ta[:, 512 * o:512 * o + 256], ta[:, 512 * o + 256:512 * o + 512]
        bf, bb = tb[:, 512 * o:512 * o + 256], tb[:, 512 * o + 256:512 * o + 512]
        sa_ref[:, 256 * o:256 * o + 256] = af + ab
        sb_ref[:, 256 * o:256 * o + 256] = jnp.where(first, bf + bb, bf - bb)


def _hy_spectra(L, kc, w1p, b1, w2, b2, fr, w3):
    z, decay = _filter_consts(L)
    f = jnp.asarray(_dft_consts(L)[0])
    nj = L // kc
    full = lambda shape: pl.BlockSpec(shape, lambda j: tuple(0 for _ in shape))
    return pl.pallas_call(
        functools.partial(_spectra_kernel, L=L, kc=kc),
        grid=(nj,),
        in_specs=[full((L, 128)), full((128, HY_HID)), full((1, HY_HID)), full((HY_HID, HY_HID)),
                  full((1, HY_HID)), full((1, HY_HID)), full((HY_HID, 4 * HY_W)), full((L, HY_W)),
                  pl.BlockSpec((kc, L), lambda j: (j, 0)),
                  pl.BlockSpec((kc, L), lambda j: (j + nj, 0))],
        out_specs=[pl.BlockSpec((kc, 2 * HY_W), lambda j: (j, 0)),
                   pl.BlockSpec((kc, 2 * HY_W), lambda j: (j, 0))],
        out_shape=[jax.ShapeDtypeStruct((L, 2 * HY_W), F32)] * 2,
        scratch_shapes=[pltpu.VMEM((L, 4 * HY_W), BF16)],
        compiler_params=_cparams(("arbitrary",)),
    )(jnp.asarray(z), w1p, b1, w2, b2, fr, w3, jnp.asarray(decay), f, f)


def _hyconv_kernel(*refs, L, nb, kc, nj, row_blk0, staged):
    n_in = 1 if staged else nb
    (cw_ref, cb_ref, d_ref, fa_ref, fb_ref, ga_ref, gb_ref, sa_ref, sb_ref, o_ref,
     xin_scr, x_scr, g1_scr, g2_scr, y_scr) = refs[n_in:n_in + 15]
    bg = pl.program_id(0)
    o = pl.program_id(1)
    j = pl.program_id(2)

    @pl.when((o == 0) & (j == 0))
    def _():
        row = lax.broadcasted_iota(jnp.int32, (L, HY_W), 0)
        pieces = [(i, part) for i in range(nb) for part in range(3)]
        if staged:
            p_hbm, (stage, sem) = refs[0], refs[n_in + 15:]

            def piece_copy(k):
                i, part = pieces[k]
                row0 = pl.multiple_of((row_blk0 + bg * nb + i) * L, L)
                return pltpu.make_async_copy(
                    p_hbm.at[pl.ds(row0, L), C_HY + HY_W * part:C_HY + HY_W * (part + 1)],
                    stage.at[k % 2], sem.at[k % 2])

            piece_copy(0).start()
            piece_copy(1).start()
        for k, (i, part) in enumerate(pieces):
            cols = slice(HY_W * i, HY_W * (i + 1))
            dst = (x_scr, g1_scr, g2_scr)[part]
            pc = slice(HY_W * part, HY_W * (part + 1))
            if staged:
                piece_copy(k).wait()
                u = stage[k % 2].astype(F32)
            else:
                u = refs[i][:, pc].astype(F32)
            up = jnp.where(row == 0, 0.0, pltpu.roll(u, 1, axis=0))
            un = jnp.where(row == L - 1, 0.0, pltpu.roll(u, L - 1, axis=0))
            dst[:, cols] = (up * cw_ref[0:1, pc] + u * cw_ref[1:2, pc] + un * cw_ref[2:3, pc]
                            + cb_ref[:, pc])
            if part == 0:
                xin_scr[:, cols] = x_scr[:, cols].astype(BF16)
            if staged and k + 2 < len(pieces):
                piece_copy(k + 2).start()

    @pl.when(j == 0)
    def _():
        y_scr[...] = jnp.zeros_like(y_scr)

    xa = _dot(fa_ref[...], xin_scr[...])
    xb = _dot(fb_ref[...], xin_scr[...])
    sa = jnp.concatenate([sa_ref[...]] * nb, axis=1)
    sb = jnp.concatenate([sb_ref[...]] * nb, axis=1)
    first = (lax.broadcasted_iota(jnp.int32, (kc, nb * HY_W), 0) + j * kc) == 0
    ya = jnp.where(first, xa * sa, xa * sa - xb * sb)
    yb = jnp.where(first, xb * sb, xa * sb + xb * sa)
    y_scr[...] += _dot(ga_ref[...], ya.astype(BF16)) + _dot(gb_ref[...], yb.astype(BF16))

    @pl.when(j == nj - 1)
    def _():
        dvec = jnp.concatenate([d_ref[pl.ds(o, 1), :]] * nb, axis=1)
        y = y_scr[...] + x_scr[...] * dvec

        @pl.when(o == 0)
        def _():
            zz = g1_scr[...] * y
            x_scr[...] = zz
            xin_scr[...] = zz.astype(BF16)

        @pl.when(o == 1)
        def _():
            res = g2_scr[...] * y
            for i in range(nb):
                o_ref[L * i:L * (i + 1), :] = res[:, HY_W * i:HY_W * (i + 1)].astype(BF16)


def _hy_conv(p, L, nb, kc, row_blk0, n_seq, conv_w, conv_b, d, sa, sb):
    f, g = (jnp.asarray(a) for a in _dft_consts(L))
    nj = L // kc
    staged = nb * L > DEC_SEQ
    if staged:
        u_specs = [pl.BlockSpec(memory_space=pl.ANY)]
        u_scratch = [pltpu.VMEM((2, L, HY_W), p.dtype), pltpu.SemaphoreType.DMA((2,))]
    else:
        u_specs = [pl.BlockSpec((L, 3 * HY_W), lambda bg, o, j, i=i: (row_blk0 + bg * nb + i, C_HY // 768))
                   for i in range(nb)]
        u_scratch = []
    small = lambda shape: pl.BlockSpec(shape, lambda bg, o, j: (0, 0))
    return pl.pallas_call(
        functools.partial(_hyconv_kernel, L=L, nb=nb, kc=kc, nj=nj, row_blk0=row_blk0, staged=staged),
        grid=(n_seq // nb, 2, nj),
        in_specs=u_specs + [
            small((3, 3 * HY_W)), small((1, 3 * HY_W)), small((2, HY_W)),
            pl.BlockSpec((kc, L), lambda bg, o, j: (j, 0)),
            pl.BlockSpec((kc, L), lambda bg, o, j: (j + nj, 0)),
            pl.BlockSpec((L, kc), lambda bg, o, j: (0, j)),
            pl.BlockSpec((L, kc), lambda bg, o, j: (0, j + nj)),
            pl.BlockSpec((kc, HY_W), lambda bg, o, j: (j, o)),
            pl.BlockSpec((kc, HY_W), lambda bg, o, j: (j, o))],
        out_specs=pl.BlockSpec((nb * L, HY_W), lambda bg, o, j: (bg, 0)),
        out_shape=jax.ShapeDtypeStruct((n_seq * L, HY_W), BF16),
        scratch_shapes=[pltpu.VMEM((L, nb * HY_W), BF16)] + [pltpu.VMEM((L, nb * HY_W), F32)] * 4
        + u_scratch,
        compiler_params=_cparams(("arbitrary", "arbitrary", "arbitrary")),
    )(*([p] * len(u_specs)), conv_w, conv_b, d, f, f, g, g, sa, sb)


def _outproj_kernel(xp_ref, xs_ref, mixp_ref, hyp_ref, nas_ref, ds_ref, hys_ref, g1_ref, w_ref, o_ref):
    i = pl.program_id(0)
    w = lambda a, b: w_ref[a:b, :].astype(BF16)

    @pl.when(i < NP_TOK // TM)
    def _():
        y = _dot(mixp_ref[...], w(0, 768)) + _dot(hyp_ref[...], w(768, D))
        o_ref[...] = xp_ref[...] + g1_ref[...] * y

    @pl.when(i >= NP_TOK // TM)
    def _():
        y = _dot(nas_ref[...], w(0, 256)) + _dot(ds_ref[...], w(256, 768)) + _dot(hys_ref[...], w(768, D))
        o_ref[...] = xs_ref[...] + g1_ref[...] * y


def _x_specs(xs_block0):
    npt = NP_TOK // TM
    return [pl.BlockSpec((TM, D), lambda i, *_: (jnp.minimum(i, npt - 1), 0)),
            pl.BlockSpec((TM, D), lambda i, *_: (jnp.maximum(i - npt, 0) + xs_block0, 0))]


def _outproj(xp, xs, xs_block0, mix_p, hy_p, na_s, d_s, hy_s, mod6, w_out, l):
    npt = NP_TOK // TM
    pidx = lambda i: (jnp.minimum(i, npt - 1), 0)
    sidx = lambda i: (jnp.maximum(i - npt, 0), 0)
    return pl.pallas_call(
        _outproj_kernel,
        grid=(N_TOK // TM,),
        in_specs=_x_specs(xs_block0) + [
                  pl.BlockSpec((TM, 768), pidx),
                  pl.BlockSpec((TM, HY_W), pidx),
                  pl.BlockSpec((TM, 256), sidx),
                  pl.BlockSpec((TM, 512), sidx),
                  pl.BlockSpec((TM, HY_W), sidx),
                  _mod_spec(2, TM),
                  pl.BlockSpec((None, D, D), lambda i: (l, 0, 0))],
        out_specs=pl.BlockSpec((TM, D), lambda i: (i, 0)),
        out_shape=jax.ShapeDtypeStruct((N_TOK, D), F32),
        compiler_params=_cparams(("arbitrary",)),
    )(xp, xs, mix_p, hy_p, na_s, d_s, hy_s, mod6, w_out)


def _ffn_kernel(x_ref, g_ref, sc_ref, sh_ref, gate_ref, wg_ref, wu_ref, wd_ref, o_ref, h_scr, *, nj):
    j = pl.program_id(1)

    @pl.when(j == 0)
    def _():
        h_scr[...] = _norm_mod(x_ref[...], g_ref[...], sc_ref[...], sh_ref[...]).astype(BF16)
        o_ref[...] = jnp.zeros_like(o_ref)

    wg, wu, wd = (r[...].astype(BF16) for r in (wg_ref, wu_ref, wd_ref))
    half = h_scr.shape[0] // 2
    for c in range(2):
        rows = slice(c * half, (c + 1) * half)
        h = h_scr[rows, :]
        a = _silu(_dot(h, wg)) * _dot(h, wu)
        o_ref[rows, :] += _dot(a.astype(BF16), wd)

    @pl.when(j == nj - 1)
    def _():
        o_ref[...] = x_ref[...] + gate_ref[...] * o_ref[...]


def _dense_ffn(x, g, mod6, wg, wu, wd, i_ffn):
    tf = 256
    tm = DEC_SEQ
    nj = FFN // tf
    return pl.pallas_call(
        functools.partial(_ffn_kernel, nj=nj),
        grid=(N_TOK // tm, nj),
        in_specs=[pl.BlockSpec((tm, D), lambda i, j: (i, 0)),
                  pl.BlockSpec((1, D), lambda i, j: (0, 0)),
                  _mod_spec(4, tm), _mod_spec(3, tm), _mod_spec(5, tm),
                  pl.BlockSpec((None, D, tf), lambda i, j: (i_ffn, 0, j)),
                  pl.BlockSpec((None, D, tf), lambda i, j: (i_ffn, 0, j)),
                  pl.BlockSpec((None, tf, D), lambda i, j: (i_ffn, j, 0))],
        out_specs=pl.BlockSpec((tm, D), lambda i, j: (i, 0)),
        out_shape=jax.ShapeDtypeStruct((N_TOK, D), F32),
        scratch_shapes=[pltpu.VMEM((tm, D), BF16)],
        compiler_params=_cparams(("arbitrary", "arbitrary")),
    )(x, g, mod6, mod6, mod6, wg, wu, wd)


def _router_kernel(x_ref, g_ref, sc_ref, sh_ref, wr_ref, h_ref, r_ref):
    h = _norm_mod(x_ref[...], g_ref[...], sc_ref[...], sh_ref[...])
    h_ref[...] = h.astype(BF16)
    lane = lax.broadcasted_iota(jnp.int32, (TM, 128), 1)
    lg = jnp.where(lane < N_EXP, _dot3(h, wr_ref[...]), -jnp.inf)
    m1 = jnp.max(lg, axis=-1, keepdims=True)
    i1 = jnp.min(jnp.where(lg == m1, lane, 128), axis=-1, keepdims=True)
    lg2 = jnp.where(lane == i1, -jnp.inf, lg)
    m2 = jnp.max(lg2, axis=-1, keepdims=True)
    i2 = jnp.min(jnp.where(lg2 == m2, lane, 128), axis=-1, keepdims=True)
    e = jnp.exp(m2 - m1)
    w1 = 1.0 / (1.0 + e)
    w2 = e / (1.0 + e)
    r_ref[...] = jnp.where(lane == 0, i1.astype(F32),
                           jnp.where(lane == 1, i2.astype(F32),
                                     jnp.where(lane == 2, w1, jnp.where(lane == 3, w2, 0.0))))


def _router(x, g, mod6, wr_pad):
    return pl.pallas_call(
        _router_kernel,
        grid=(N_TOK // TM,),
        in_specs=[pl.BlockSpec((TM, D), lambda i: (i, 0)),
                  pl.BlockSpec((1, D), lambda i: (0, 0)),
                  _mod_spec(4, TM), _mod_spec(3, TM),
                  pl.BlockSpec((D, 128), lambda i: (0, 0))],
        out_specs=[pl.BlockSpec((TM, D), lambda i: (i, 0)),
                   pl.BlockSpec((TM, 128), lambda i: (i, 0))],
        out_shape=[jax.ShapeDtypeStruct((N_TOK, D), BF16), jax.ShapeDtypeStruct((N_TOK, 128), F32)],
        compiler_params=_cparams(("arbitrary",)),
    )(x, g, mod6, mod6, wr_pad)


def _dispatch_kernel(blo_ref, bhi_ref, sexp_ref, h_hbm, rp_ref, o_ref, h_scr, sem):
    s = pl.program_id(0)

    @pl.when(s == 0)
    def _():
        cp = pltpu.make_async_copy(h_hbm, h_scr, sem)
        cp.start()
        cp.wait()

    e = sexp_ref[s]
    rows = s * MOE_TS + lax.broadcasted_iota(jnp.int32, (MOE_TS, MOE_DTB), 0)

    def sel(b):
        return (rows == rp_ref[e, pl.ds(b, 1), :]).astype(BF16)

    is_empty = blo_ref[s] > bhi_ref[s]

    @pl.when(is_empty)
    def _():
        o_ref[...] = jnp.zeros_like(o_ref)

    @pl.when(jnp.logical_not(is_empty))
    def _():
        b0 = jnp.minimum(blo_ref[s], N_TOK // MOE_DTB - MOE_DWIN)
        hwin = h_scr[pl.ds(pl.multiple_of(b0 * MOE_DTB, MOE_DTB), MOE_DWIN * MOE_DTB), :]
        acc = _dot(jnp.concatenate([sel(b0 + k) for k in range(MOE_DWIN)], axis=1), hwin)

        def body(b, acc):
            hb = h_scr[pl.ds(pl.multiple_of(b * MOE_DTB, MOE_DTB), MOE_DTB), :]
            return acc + _dot(sel(b), hb)

        acc = lax.fori_loop(b0 + MOE_DWIN, bhi_ref[s] + 1, body, acc)
        o_ref[...] = acc.astype(BF16)


def _dispatch(blo, bhi, sexp, h, rp_t):
    return pl.pallas_call(
        _dispatch_kernel,
        grid_spec=pltpu.PrefetchScalarGridSpec(
            num_scalar_prefetch=3,
            grid=(MOE_ROWS // MOE_TS,),
            in_specs=[pl.BlockSpec(memory_space=pl.ANY),
                      pl.BlockSpec((N_EXP, N_TOK // MOE_DTB, MOE_DTB), lambda s, *_: (0, 0, 0))],
            out_specs=pl.BlockSpec((MOE_TS, D), lambda s, *_: (s, 0)),
            scratch_shapes=[pltpu.VMEM((N_TOK, D), BF16), pltpu.SemaphoreType.DMA(())]),
        out_shape=jax.ShapeDtypeStruct((MOE_ROWS, D), BF16),
        compiler_params=_cparams(("arbitrary",)),
    )(blo, bhi, sexp, h, rp_t)


def _experts_kernel(te_ref, used_ref, rows_ref, xs_ref, wg_ref, wu_ref, wd_ref, o_ref, acc_scr, *, nj):
    i = pl.program_id(0)
    j = pl.program_id(1)
    nrows = rows_ref[i]

    @pl.when(j == 0)
    def _():
        acc_scr[...] = jnp.zeros_like(acc_scr)

    for sz in range(MOE_TQ, MOE_TM + 1, MOE_TQ):
        @pl.when(nrows == sz)
        def _():
            wg, wu, wd = (r[...].astype(BF16) for r in (wg_ref, wu_ref, wd_ref))
            parts = 2 if sz >= 2 * MOE_TQ else 1
            for c in range(parts):
                rows = slice(c * sz // parts, (c + 1) * sz // parts)
                xb = xs_ref[rows, :]
                a = _silu(_dot(xb, wg)) * _dot(xb, wu)
                acc_scr[rows, :] += _dot(a.astype(BF16), wd)

    @pl.when(j == nj - 1)
    def _():
        o_ref[...] = acc_scr[...].astype(BF16)


def _experts(tile_expert, used, tile_rows, xs, wg, wu, wd, i_moe):
    nj = EXP_DIM // MOE_TF

    def tile(i, used):
        return jnp.minimum(i, used[0] - 1)

    def chunk(i, j, used):
        return jnp.where(i < used[0], j, nj - 1)

    return pl.pallas_call(
        functools.partial(_experts_kernel, nj=nj),
        grid_spec=pltpu.PrefetchScalarGridSpec(
            num_scalar_prefetch=3,
            grid=(MOE_YROWS // MOE_TM, nj),
            in_specs=[pl.BlockSpec((MOE_TM, D), lambda i, j, te, used, tr: (tile(i, used), 0)),
                      pl.BlockSpec((None, None, D, MOE_TF),
                                   lambda i, j, te, used, tr: (i_moe, te[tile(i, used)], 0, chunk(i, j, used))),
                      pl.BlockSpec((None, None, D, MOE_TF),
                                   lambda i, j, te, used, tr: (i_moe, te[tile(i, used)], 0, chunk(i, j, used))),
                      pl.BlockSpec((None, None, MOE_TF, D),
                                   lambda i, j, te, used, tr: (i_moe, te[tile(i, used)], chunk(i, j, used), 0))],
            out_specs=pl.BlockSpec((MOE_TM, D), lambda i, j, te, used, tr: (i, 0)),
            scratch_shapes=[pltpu.VMEM((MOE_TM, D), F32)]),
        out_shape=jax.ShapeDtypeStruct((MOE_YROWS, D), BF16),
        compiler_params=_cparams(("arbitrary", "arbitrary")),
    )(tile_expert, used, tile_rows, xs, wg, wu, wd)


def _combine_kernel(ws_ref, kind_ref, x_ref, rp_ref, comb_ref, gate_ref, fg_ref, ys_hbm, op_ref, os_ref,
                    win_scr, y_scr, sem):
    b = pl.program_id(0)
    nb = pl.num_programs(0)
    slot = b % 2

    def win_copy(blk, sl, e):
        start = pl.multiple_of(ws_ref[blk * N_EXP + e], MOE_ALIGN)
        return pltpu.make_async_copy(ys_hbm.at[pl.ds(start, MOE_WIN)], win_scr.at[sl, e], sem.at[sl, e])

    @pl.when(b == 0)
    def _():
        for e in range(N_EXP):
            win_copy(0, 0, e).start()

    @pl.when(b + 1 < nb)
    def _():
        for e in range(N_EXP):
            win_copy(b + 1, 1 - slot, e).start()

    col = lax.broadcasted_iota(jnp.int32, (MOE_TB, MOE_WIN), 1)
    y_scr[...] = jnp.zeros_like(y_scr)
    for e in range(N_EXP):
        win_copy(b, slot, e).wait()
        rel = rp_ref[:, e:e + 1] - ws_ref[b * N_EXP + e]
        cw = comb_ref[:, e:e + 1]
        kind = kind_ref[b * N_EXP + e]

        for k, wn in enumerate(MOE_WINS):
            @pl.when(kind == k)
            def _():
                sel = (rel == col[:, :wn]).astype(BF16) if wn == MOE_WIN else (
                    rel == lax.broadcasted_iota(jnp.int32, (MOE_TB, wn), 1)).astype(BF16)
                y_scr[...] += cw * _dot(sel, win_scr[slot, e, 0:wn, :])

    x = x_ref[...] + gate_ref[...] * y_scr[...]
    x = (x * lax.rsqrt(jnp.mean(x * x, axis=-1, keepdims=True) + EPS)) * fg_ref[...]

    @pl.when(b < NP_TOK // MOE_TB)
    def _():
        op_ref[...] = x

    @pl.when(b >= NP_TOK // MOE_TB)
    def _():
        os_ref[...] = x


def _combine(ws, kind, x, rp8, comb, mod6, final_g, ys):
    npb = NP_TOK // MOE_TB
    return pl.pallas_call(
        _combine_kernel,
        grid_spec=pltpu.PrefetchScalarGridSpec(
            num_scalar_prefetch=2,
            grid=(N_TOK // MOE_TB,),
            in_specs=[pl.BlockSpec((MOE_TB, D), lambda b, *_: (b, 0)),
                      pl.BlockSpec((MOE_TB, N_EXP), lambda b, *_: (b, 0)),
                      pl.BlockSpec((MOE_TB, N_EXP), lambda b, *_: (b, 0)),
                      _mod_spec(5, MOE_TB),
                      pl.BlockSpec((1, D), lambda b, *_: (0, 0)),
                      pl.BlockSpec(memory_space=pl.ANY)],
            out_specs=[pl.BlockSpec((MOE_TB, D), lambda b, *_: (jnp.minimum(b, npb - 1), 0)),
                       pl.BlockSpec((MOE_TB, D), lambda b, *_: (jnp.maximum(b - npb, 0), 0))],
            scratch_shapes=[pltpu.VMEM((2, N_EXP, MOE_WIN, D), BF16),
                            pltpu.VMEM((MOE_TB, D), F32),
                            pltpu.SemaphoreType.DMA((2, N_EXP))]),
        out_shape=[jax.ShapeDtypeStruct((NP_TOK, D), F32), jax.ShapeDtypeStruct((NS_TOK, D), F32)],
        compiler_params=_cparams(("arbitrary",)),
    )(ws, kind, x, rp8, comb, mod6, final_g, ys)


def _moe(x, g, mod6, router, wg, wu, wd, i_moe, final_g):
    wr_pad = jnp.pad(router, ((0, 0), (0, 128 - N_EXP)))
    h, r = _router(x, g, mod6, wr_pad)

    i32 = jnp.int32
    i12 = r[:, 0:2].astype(i32)
    earange = jnp.arange(N_EXP, dtype=i32)
    hit1 = i12[:, 0:1] == earange[None, :]
    hit2 = i12[:, 1:2] == earange[None, :]
    comb = jnp.where(hit1, r[:, 2:3], 0.0) + jnp.where(hit2, r[:, 3:4], 0.0)
    mask = (hit1 | hit2).astype(i32)
    csum = jnp.cumsum(mask, axis=0)
    counts = csum[-1]
    padded = ((counts + MOE_TM - 1) // MOE_TM) * MOE_TM
    ends = jnp.cumsum(padded)
    starts = ends - padded
    rp8 = jnp.where(mask > 0, starts[None, :] + csum - 1, -1).astype(i32)
    n_tiles = MOE_ROWS // MOE_TM
    tile_expert = jnp.minimum(
        jnp.searchsorted(ends, jnp.arange(n_tiles, dtype=i32) * MOE_TM, side="right"),
        N_EXP - 1).astype(i32)
    used = (ends[-1:] // MOE_TM).astype(i32)
    tile_row0 = jnp.arange(MOE_YROWS // MOE_TM, dtype=i32) * MOE_TM
    te_all = jnp.minimum(jnp.searchsorted(ends, tile_row0, side="right"), N_EXP - 1)
    valid = jnp.clip(counts[te_all] - (tile_row0 - starts[te_all]), 0, MOE_TM)
    valid = jnp.where(tile_row0 < ends[-1], valid, 0)
    tile_rows = (((valid + MOE_TQ - 1) // MOE_TQ) * MOE_TQ).astype(i32)

    sub_row0 = jnp.arange(MOE_ROWS // MOE_TS, dtype=i32) * MOE_TS
    sexp = tile_expert[sub_row0 // MOE_TM]
    qlo = sub_row0 - starts[sexp]
    qend = jnp.minimum(qlo + MOE_TS, counts[sexp])
    cbe = csum[MOE_DTB - 1::MOE_DTB, :].T[sexp]
    blo = jnp.sum((cbe <= qlo[:, None]).astype(i32), axis=1)
    bhi = jnp.minimum(jnp.sum((cbe < qend[:, None]).astype(i32), axis=1), N_TOK // MOE_DTB - 1)
    empty = qend <= qlo
    blo = jnp.where(empty, 1, blo).astype(i32)
    bhi = jnp.where(empty, 0, bhi).astype(i32)

    cb = csum[MOE_TB - 1::MOE_TB, :]
    cprev = jnp.concatenate([jnp.zeros((1, N_EXP), i32), cb[:-1]], axis=0)
    ws = (((starts[None, :] + cprev) // MOE_ALIGN) * MOE_ALIGN).reshape(-1).astype(i32)
    n_be = cb - cprev
    kind = sum((n_be > wn - MOE_ALIGN).astype(i32) for wn in MOE_WINS[:-1])
    kind = jnp.where(n_be == 0, len(MOE_WINS), kind).reshape(-1).astype(i32)

    rp_t = rp8.T.reshape(N_EXP, N_TOK // MOE_DTB, MOE_DTB)
    xs = _dispatch(blo, bhi, sexp, h, rp_t)
    ys = _experts(tile_expert, used, tile_rows, xs, wg, wu, wd, i_moe)
    return _combine(ws, kind, x, rp8, comb, mod6, final_g, ys)


assert DEPTH == 2

def kernel(x_prompt, x_sample, cache_na_k, cache_na_v, cache_diff_k, cache_diff_v, c, c_ctx, w_in, w_out, ada_w, ada_b, norm_mix_g, norm_ffn_g, na_rpb, diff_lq1, diff_lk1, diff_lq2, diff_lk2, diff_subln_g, hy_conv_w, hy_conv_b, hy_d, hy_f_w1, hy_f_b1, hy_f_w2, hy_f_b2, hy_f_freq, hy_f_w3, ffn_w_gate, ffn_w_up, ffn_w_down, moe_router, moe_w_gate, moe_w_up, moe_w_down, final_norm_g):
    xparts = (x_prompt.reshape(NP_TOK, D), x_sample.reshape(NS_TOK, D), 0)
    cond8 = jnp.concatenate([c_ctx[None, :], c, jnp.zeros((5, D), F32)], axis=0)
    mods = _modulation(cond8, ada_w, ada_b)
    final_g = final_norm_g.reshape(1, D)
    bias = _na_bias(na_rpb)

    leaves = [jnp.zeros((BATCH, DEPTH, NA_HEADS, SEQ, HEAD_DIM), F32),
              jnp.zeros((BATCH, DEPTH, NA_HEADS, SEQ, HEAD_DIM), F32),
              jnp.zeros((BATCH, DEPTH, DIFF_HEADS, 2, SEQ, HEAD_DIM), F32),
              jnp.zeros((BATCH, DEPTH, DIFF_HEADS, SEQ, DIFF_V), F32)]
    for l in range(DEPTH):
        lam_init = 0.8 - 0.6 * math.exp(-0.3 * l)
        mod6 = mods[l].reshape(8, 6, D).transpose(1, 0, 2).reshape(6, 8, 1, D)
        row = lambda a: a[l].reshape(1, -1)
        lq1, lk1, lq2, lk2, subg = row(diff_lq1), row(diff_lk1), row(diff_lq2), row(diff_lk2), row(diff_subln_g)

        p, *leaves = _inproj(*xparts, row(norm_mix_g), mod6, w_in, l, leaves)

        mix_p = _prompt_attn(p, lq1, lk1, lq2, lk2, subg, lam_init)
        na_s = _na_attn(p, cache_na_k, cache_na_v, bias, l)
        d_s = _sdiff_attn(p, cache_diff_k, cache_diff_v, lq1, lk1, lq2, lk2, subg, lam_init, l)

        w1p = jnp.pad(hy_f_w1[l], ((0, 128 - HY_EMB), (0, 0)))
        fargs = (w1p, row(hy_f_b1), hy_f_w2[l], row(hy_f_b2), row(hy_f_freq), hy_f_w3[l])
        cargs = (hy_conv_w[l], row(hy_conv_b), hy_d[l])
        sa_p, sb_p = _hy_spectra(SEQ, SEQ, *fargs)
        hy_p = _hy_conv(p, SEQ, 4, SEQ, 0, BATCH, *cargs, sa_p, sb_p)
        sa_s, sb_s = _hy_spectra(DEC_SEQ, 512, *fargs)
        hy_s = _hy_conv(p, DEC_SEQ, 2, 512, NP_TOK // DEC_SEQ, DEC_BATCH, *cargs, sa_s, sb_s)

        x = _outproj(*xparts, mix_p, hy_p, na_s, d_s, hy_s, mod6, w_out, l)

        if l == 0:
            x = _dense_ffn(x, row(norm_ffn_g), mod6, ffn_w_gate, ffn_w_up, ffn_w_down, 0)
            xparts = (x, x, NP_TOK // TM)
        else:
            yp, ys = _moe(x, row(norm_ffn_g), mod6, moe_router[0], moe_w_gate, moe_w_up, moe_w_down,
                          0, final_g)

    return (yp.reshape(BATCH, SEQ, D), ys.reshape(DEC_BATCH, DEC_SEQ, D), *leaves)
```

```python
import functools
import math

import numpy as np
import jax
import jax.numpy as jnp
from jax import lax
from jax.experimental import pallas as pl
from jax.experimental.pallas import tpu as pltpu

F32 = jnp.float32
BF16 = jnp.bfloat16

D = 1024
BATCH, SEQ = 32, 256
DEC_BATCH, DEC_SEQ = 2, 2048
DEPTH = 2
PAST = 512
GRID_W = 64
GRID_ROWS = DEC_SEQ // GRID_W
HEAD_DIM = 64
NA_HEADS = 4
DIFF_HEADS = 4
DIFF_V = 128
WIN_ROWS, WIN_COLS = 8, 16
HY_W = 256
HY_EMB = 33
HY_BANDS = 16
HY_HID = 64
PROJ = 3072
FFN = 2816
N_EXP = 8
EXP_DIM = 3584
EPS = 1e-6
ROPE_BASE = 10000.0

NP_TOK = BATCH * SEQ
NS_TOK = DEC_BATCH * DEC_SEQ
N_TOK = NP_TOK + NS_TOK

C_NAQ, C_NAK, C_NAV = 0, 256, 512
C_DQ, C_DK, C_DV = 768, 1280, 1792
C_HY = 2304

TM = 1024
NA_RB = 4
VMEM_LIMIT = 56 * 1024 * 1024

MOE_TM = 1024
MOE_TF = 512
MOE_TQ = 256
MOE_ROWS = 2 * N_TOK + N_EXP * MOE_TM
MOE_TS = 256
MOE_DTB = 512
MOE_DWIN = 3
MOE_TB = 512
MOE_ALIGN = 16
MOE_WIN = MOE_TB + MOE_ALIGN
MOE_WINS = (MOE_TB // 4 + MOE_ALIGN, MOE_TB // 2 + MOE_ALIGN, MOE_WIN)
MOE_YROWS = MOE_ROWS + MOE_TM


def _cparams(sem):
    return pltpu.CompilerParams(dimension_semantics=sem, vmem_limit_bytes=VMEM_LIMIT)


def _dot(a, b):
    return jnp.dot(a, b, preferred_element_type=F32)


def _dot_nt(a, b):
    return lax.dot_general(a, b, (((1,), (1,)), ((), ())), preferred_element_type=F32)


def _split(a):
    hi = a.astype(BF16)
    lo = (a - hi.astype(F32)).astype(BF16)
    return hi, lo


def _dot3(a, b):
    ah, al = _split(a)
    bh, bl = _split(b)
    return _dot(ah, bh) + (_dot(ah, bl) + _dot(al, bh))


def _silu(x):
    return x / (1.0 + jnp.exp(-x))


def _mod_row(i, tm):
    t = i * tm
    return jnp.where(t < NP_TOK, 0, 1 + (t - NP_TOK) // DEC_SEQ)


def _mod_spec(k, tm):
    return pl.BlockSpec((None, None, 1, D), lambda i, *_: (k, _mod_row(i, tm), 0, 0))


def _norm_mod(x, g, sc, sh):
    y = x * lax.rsqrt(jnp.mean(x * x, axis=-1, keepdims=True) + EPS)
    return (y * g) * (1.0 + sc) + sh


def _mod_kernel(c_ref, w_ref, b_ref, o_ref):
    o_ref[...] = _dot3(_silu(c_ref[...]), w_ref[...]) + b_ref[...]


def _modulation(cond8, ada_w, ada_b):
    tn = 1536
    return pl.pallas_call(
        _mod_kernel,
        grid=(DEPTH, 6 * D // tn),
        in_specs=[pl.BlockSpec((8, D), lambda l, j: (0, 0)),
                  pl.BlockSpec((None, D, tn), lambda l, j: (l, 0, j)),
                  pl.BlockSpec((None, 1, tn), lambda l, j: (l, 0, j))],
        out_specs=pl.BlockSpec((None, 8, tn), lambda l, j: (l, 0, j)),
        out_shape=jax.ShapeDtypeStruct((DEPTH, 8, 6 * D), F32),
        compiler_params=_cparams(("arbitrary", "arbitrary")),
    )(cond8, ada_w, ada_b.reshape(DEPTH, 1, 6 * D))


def _inproj_kernel(xp_ref, xs_ref, g_ref, sc_ref, sh_ref, w_hbm, nak_in, nav_in, dk_in, dv_in,
                   o_ref, nak_ref, nav_ref, dk_ref, dv_ref, h_scr, w_scr, stage, acc_scr, sem, *, l, tn):
    del nak_in, nav_in, dk_in, dv_in
    i = pl.program_id(0)
    j = pl.program_id(1)

    @pl.when((i == 0) & (j == 0))
    def _():
        for c in range(PROJ // tn):
            cp = pltpu.make_async_copy(w_hbm.at[l, :, c * tn:(c + 1) * tn], stage, sem)
            cp.start()
            cp.wait()
            w_scr[c] = stage[...].astype(BF16)

    is_ctx = i < NP_TOK // TM

    @pl.when((j == 0) & is_ctx)
    def _():
        h_scr[...] = _norm_mod(xp_ref[...], g_ref[...], sc_ref[...], sh_ref[...]).astype(BF16)

    @pl.when((j == 0) & jnp.logical_not(is_ctx))
    def _():
        h_scr[...] = _norm_mod(xs_ref[...], g_ref[...], sc_ref[...], sh_ref[...]).astype(BF16)

    acc_scr[...] = _dot(h_scr[...], w_scr[j])
    o_ref[...] = acc_scr[...].astype(BF16)


    def rows(bb):
        return slice(bb * SEQ, (bb + 1) * SEQ)

    def copy_heads(dst_ref, c0, width, heads, sub=None):
        for bb in range(TM // SEQ):
            for h in heads:
                if sub is None:
                    dst_ref[bb, h] = acc_scr[rows(bb), c0(h):c0(h) + width]
                else:
                    for s in range(2):
                        dst_ref[bb, h, s] = acc_scr[rows(bb), c0(h) + s * width:c0(h) + (s + 1) * width]

    @pl.when(is_ctx & (j == 0))
    def _():
        copy_heads(nak_ref, lambda h: C_NAK + HEAD_DIM * h, HEAD_DIM, range(NA_HEADS))
        copy_heads(nav_ref, lambda h: C_NAV + HEAD_DIM * h, HEAD_DIM, range(NA_HEADS))

    @pl.when(is_ctx & (j == 1))
    def _():
        copy_heads(dk_ref, lambda h: C_DK - tn + 2 * HEAD_DIM * h, HEAD_DIM, range(0, 2), sub=True)

    @pl.when(is_ctx & (j == 2))
    def _():
        copy_heads(dk_ref, lambda h: C_DK - 2 * tn + 2 * HEAD_DIM * h, HEAD_DIM, range(2, 4), sub=True)
        copy_heads(dv_ref, lambda h: C_DV - 2 * tn + DIFF_V * h, DIFF_V, range(DIFF_HEADS))


def _inproj(xp, xs, xs_block0, g, mod6, w_in, l, leaves):
    tn = 768
    nb = TM // SEQ
    ctx = lambda i: jnp.minimum(i, NP_TOK // TM - 1)
    leaf_specs = [pl.BlockSpec((nb, None, NA_HEADS, SEQ, HEAD_DIM), lambda i, j: (ctx(i), l, 0, 0, 0)),
                  pl.BlockSpec((nb, None, NA_HEADS, SEQ, HEAD_DIM), lambda i, j: (ctx(i), l, 0, 0, 0)),
                  pl.BlockSpec((nb, None, DIFF_HEADS, 2, SEQ, HEAD_DIM), lambda i, j: (ctx(i), l, 0, 0, 0, 0)),
                  pl.BlockSpec((nb, None, DIFF_HEADS, SEQ, DIFF_V), lambda i, j: (ctx(i), l, 0, 0, 0))]
    return pl.pallas_call(
        functools.partial(_inproj_kernel, l=l, tn=tn),
        grid=(N_TOK // TM, PROJ // tn),
        in_specs=_x_specs(xs_block0) + [
                  pl.BlockSpec((1, D), lambda i, j: (0, 0)),
                  _mod_spec(1, TM), _mod_spec(0, TM),
                  pl.BlockSpec(memory_space=pl.ANY)] + [pl.BlockSpec(memory_space=pl.ANY)] * 4,
        out_specs=[pl.BlockSpec((TM, tn), lambda i, j: (i, j))] + leaf_specs,
        out_shape=[jax.ShapeDtypeStruct((N_TOK, PROJ), BF16)]
        + [jax.ShapeDtypeStruct(a.shape, a.dtype) for a in leaves],
        input_output_aliases={6: 1, 7: 2, 8: 3, 9: 4},
        scratch_shapes=[pltpu.VMEM((TM, D), BF16), pltpu.VMEM((PROJ // tn, D, tn), BF16),
                        pltpu.VMEM((D, tn), F32), pltpu.VMEM((TM, tn), F32), pltpu.SemaphoreType.DMA(())],
        compiler_params=_cparams(("arbitrary", "arbitrary")),
    )(xp, xs, g, mod6, mod6, w_in, *leaves)


def _lam(lq1, lk1, lq2, lk2, lam_init):
    return (jnp.exp(jnp.sum(lq1[...] * lk1[...], axis=-1, keepdims=True))
            - jnp.exp(jnp.sum(lq2[...] * lk2[...], axis=-1, keepdims=True)) + lam_init)


def _softmax_parts(s):
    m = jnp.max(s, axis=-1, keepdims=True)
    e = jnp.exp(s - m)
    return e, jnp.sum(e, axis=-1, keepdims=True)


QK_SCALE = HEAD_DIM ** -0.5


def _diff_head(q1, q2, k1, k2, v, lam, g, lam_init):
    e1, l1 = _softmax_parts(_dot_nt(q1, k1))
    e2, l2 = _softmax_parts(_dot_nt(q2, k2))
    a = e1 - (lam * l1 * (1.0 / l2)) * e2
    o = _dot(a.astype(BF16), v) * (1.0 / l1)
    o = o * lax.rsqrt(jnp.mean(o * o, axis=-1, keepdims=True) + EPS)
    return (o * g) * (1.0 - lam_init)


def _prompt_attn_kernel(pa_ref, pb_ref, pc_ref, lq1, lk1, lq2, lk2, g_ref, o_ref, *, lam_init):
    lam = _lam(lq1, lk1, lq2, lk2, lam_init)
    g = g_ref[...]

    def col(c0, w, scale=None):
        ref = (pa_ref, pb_ref, pc_ref)[c0 // 768]
        o = c0 % 768
        a = ref[:, o:o + w]
        return (a if scale is None else a * scale).astype(BF16)

    for h in range(NA_HEADS):
        q = col(C_NAQ + 64 * h, 64, QK_SCALE)
        k = col(C_NAK + 64 * h, 64)
        v = col(C_NAV + 64 * h, 64)
        e, l = _softmax_parts(_dot_nt(q, k))
        o = _dot(e.astype(BF16), v) * (1.0 / l)
        o_ref[:, 64 * h:64 * h + 64] = o.astype(BF16)
    for h in range(DIFF_HEADS):
        q1 = col(C_DQ + 128 * h, 64, QK_SCALE)
        q2 = col(C_DQ + 128 * h + 64, 64, QK_SCALE)
        k1 = col(C_DK + 128 * h, 64)
        k2 = col(C_DK + 128 * h + 64, 64)
        v = col(C_DV + 128 * h, 128)
        o = _diff_head(q1, q2, k1, k2, v, lam, g, lam_init)
        o_ref[:, 256 + 128 * h:384 + 128 * h] = o.astype(BF16)


def _prompt_attn(p, lq1, lk1, lq2, lk2, g, lam_init):
    vec = lambda n: pl.BlockSpec((1, n), lambda b: (0, 0))
    return pl.pallas_call(
        functools.partial(_prompt_attn_kernel, lam_init=lam_init),
        grid=(BATCH,),
        in_specs=[pl.BlockSpec((SEQ, 768), lambda b: (b, 0)),
                  pl.BlockSpec((SEQ, 768), lambda b: (b, 1)),
                  pl.BlockSpec((SEQ, 768), lambda b: (b, 2)),
                  vec(64), vec(64), vec(64), vec(64), vec(128)],
        out_specs=pl.BlockSpec((SEQ, 768), lambda b: (b, 0)),
        out_shape=jax.ShapeDtypeStruct((NP_TOK, 768), BF16),
        compiler_params=_cparams(("arbitrary",)),
    )(p, p, p, lq1, lk1, lq2, lk2, g)


def _bias_kernel(rpb_ref, o_ref):
    lh = pl.program_id(0)
    qc = lax.broadcasted_iota(jnp.int32, (GRID_W, GRID_W), 0)
    kc = lax.broadcasted_iota(jnp.int32, (GRID_W, GRID_W), 1)
    delta = jnp.clip(kc - qc + (WIN_COLS - 1), 0, 2 * WIN_COLS - 2)
    qs = jnp.clip(qc - WIN_COLS // 2, 0, GRID_W - WIN_COLS)
    in_win = (kc >= qs) & (kc < qs + WIN_COLS)
    for dr in range(2 * WIN_ROWS - 1):
        base = (lh * (2 * WIN_ROWS - 1) + dr) * (2 * WIN_COLS - 1)
        acc = jnp.zeros((GRID_W, GRID_W), F32)
        for d in range(2 * WIN_COLS - 1):
            acc = jnp.where(delta == d, rpb_ref[base + d], acc)
        piece = jnp.where(in_win, acc, -jnp.inf)
        for case in range(WIN_ROWS):
            i = dr + case - (WIN_ROWS - 1)
            if 0 <= i < WIN_ROWS:
                o_ref[case, :, i * GRID_W:(i + 1) * GRID_W] = piece


def _na_bias(rpb):
    return pl.pallas_call(
        _bias_kernel,
        grid=(DEPTH * NA_HEADS,),
        in_specs=[pl.BlockSpec(memory_space=pltpu.SMEM)],
        out_specs=pl.BlockSpec((None, WIN_ROWS, GRID_W, WIN_ROWS * GRID_W), lambda lh: (lh, 0, 0, 0)),
        out_shape=jax.ShapeDtypeStruct((DEPTH * NA_HEADS, WIN_ROWS, GRID_W, WIN_ROWS * GRID_W), F32),
        compiler_params=_cparams(("arbitrary",)),
    )(rpb.reshape(-1))


def _na_kernel(q_ref, kv_ref, kc_ref, vc_ref, bias_ref, o_ref):
    nloc = WIN_ROWS * GRID_W
    for rr in range(NA_RB):
        r = pl.program_id(1) * NA_RB + rr
        start = jnp.clip(r - WIN_ROWS // 2, 0, GRID_ROWS - WIN_ROWS)
        case = r - start
        row0 = pl.multiple_of(start * GRID_W, GRID_W)
        qrows = slice(rr * GRID_W, (rr + 1) * GRID_W)
        for h in range(NA_HEADS):
            q = (q_ref[qrows, C_NAQ + 64 * h:C_NAQ + 64 * h + 64] * QK_SCALE).astype(BF16)
            k = kv_ref[pl.ds(row0, nloc), C_NAK + 64 * h:C_NAK + 64 * h + 64].astype(BF16)
            v = kv_ref[pl.ds(row0, nloc), C_NAV + 64 * h:C_NAV + 64 * h + 64].astype(BF16)
            s_loc = _dot_nt(q, k) + bias_ref[h, pl.ds(case, 1)][0]
            s_ctx = _dot_nt(q, kc_ref[h].astype(BF16))
            m = jnp.maximum(jnp.max(s_loc, axis=-1, keepdims=True), jnp.max(s_ctx, axis=-1, keepdims=True))
            e_loc = jnp.exp(s_loc - m)
            e_ctx = jnp.exp(s_ctx - m)
            l = jnp.sum(e_loc, axis=-1, keepdims=True) + jnp.sum(e_ctx, axis=-1, keepdims=True)
            o = (_dot(e_loc.astype(BF16), v) + _dot(e_ctx.astype(BF16), vc_ref[h].astype(BF16))) * (1.0 / l)
            o_ref[qrows, 64 * h:64 * h + 64] = o.astype(BF16)


def _na_attn(p, cache_k, cache_v, bias, l):
    qblk0 = NP_TOK // (NA_RB * GRID_W)
    kvblk0 = NP_TOK // DEC_SEQ
    nrg = GRID_ROWS // NA_RB
    return pl.pallas_call(
        _na_kernel,
        grid=(DEC_BATCH, nrg),
        in_specs=[pl.BlockSpec((NA_RB * GRID_W, 768), lambda b, r: (qblk0 + b * nrg + r, 0)),
                  pl.BlockSpec((DEC_SEQ, 768), lambda b, r: (kvblk0 + b, 0)),
                  pl.BlockSpec((None, None, NA_HEADS, PAST, HEAD_DIM), lambda b, r: (b, l, 0, 0, 0)),
                  pl.BlockSpec((None, None, NA_HEADS, PAST, HEAD_DIM), lambda b, r: (b, l, 0, 0, 0)),
                  pl.BlockSpec((NA_HEADS, WIN_ROWS, GRID_W, WIN_ROWS * GRID_W), lambda b, r: (l, 0, 0, 0))],
        out_specs=pl.BlockSpec((NA_RB * GRID_W, 256), lambda b, r: (b * nrg + r, 0)),
        out_shape=jax.ShapeDtypeStruct((NS_TOK, 256), BF16),
        compiler_params=_cparams(("arbitrary", "arbitrary")),
    )(p, p, cache_k, cache_v, bias)


@functools.lru_cache(None)
def _rope_tables():
    t = np.arange(DEC_SEQ)
    lane = np.arange(128)
    dd = lane % HEAD_DIM
    pos = np.where(dd[None, :] < 32, (t // GRID_W)[:, None], (t % GRID_W)[:, None]).astype(np.float64)
    inv = ROPE_BASE ** (-(dd % 16).astype(np.float64) * 2.0 / 32.0)
    ang = pos * inv[None, :]
    first = (dd % 32) < 16
    cos = np.cos(ang)
    s_up = np.where(first[None, :], -np.sin(ang), 0.0)
    s_dn = np.where(first[None, :], 0.0, np.sin(ang))
    return tuple(np.asarray(a, np.float32) for a in (cos, s_up, s_dn))


def _rope(x, cos, s_up, s_dn):
    return x * cos + pltpu.roll(x, 112, axis=1) * s_up + pltpu.roll(x, 16, axis=1) * s_dn


def _sdiff_kernel(q_ref, k_ref, v_ref, ck_ref, cv_ref, cos_ref, sup_ref, sdn_ref,
                  lq1, lk1, lq2, lk2, g_ref, o_ref, k1_scr, k2_scr, v_scr, *, lam_init, tq):
    qb = pl.program_id(2)

    @pl.when(qb == 0)
    def _():
        kr = _rope(k_ref[...].astype(F32), cos_ref[...], sup_ref[...], sdn_ref[...])
        k1_scr[0:DEC_SEQ, :] = kr[:, :64].astype(BF16)
        k2_scr[0:DEC_SEQ, :] = kr[:, 64:].astype(BF16)
        k1_scr[DEC_SEQ:, :] = ck_ref[0].astype(BF16)
        k2_scr[DEC_SEQ:, :] = ck_ref[1].astype(BF16)
        v_scr[0:DEC_SEQ, :] = v_ref[...].astype(BF16)
        v_scr[DEC_SEQ:, :] = cv_ref[...].astype(BF16)

    lam = _lam(lq1, lk1, lq2, lk2, lam_init)
    half = tq // 2
    for c in range(2):
        rows = pl.ds(pl.multiple_of(qb * tq + c * half, half), half)
        qr = _rope(q_ref[c * half:(c + 1) * half, :].astype(F32), cos_ref[rows, :], sup_ref[rows, :],
                   sdn_ref[rows, :]) * QK_SCALE
        o = _diff_head(qr[:, :64].astype(BF16), qr[:, 64:].astype(BF16), k1_scr[...], k2_scr[...],
                       v_scr[...], lam, g_ref[...], lam_init)
        o_ref[c * half:(c + 1) * half, :] = o.astype(BF16)


def _sdiff_attn(p, cache_k, cache_v, lq1, lk1, lq2, lk2, g, lam_init, l):
    tq = 512
    nq = DEC_SEQ // tq
    cos, s_up, s_dn = (jnp.asarray(a) for a in _rope_tables())
    vec = lambda n: pl.BlockSpec((1, n), lambda b, h, q: (0, 0))
    tab = pl.BlockSpec((DEC_SEQ, 128), lambda b, h, q: (0, 0))
    kvblk0 = NP_TOK // DEC_SEQ
    return pl.pallas_call(
        functools.partial(_sdiff_kernel, lam_init=lam_init, tq=tq),
        grid=(DEC_BATCH, DIFF_HEADS, nq),
        in_specs=[pl.BlockSpec((tq, 128), lambda b, h, q: (NP_TOK // tq + b * nq + q, C_DQ // 128 + h)),
                  pl.BlockSpec((DEC_SEQ, 128), lambda b, h, q: (kvblk0 + b, C_DK // 128 + h)),
                  pl.BlockSpec((DEC_SEQ, 128), lambda b, h, q: (kvblk0 + b, C_DV // 128 + h)),
                  pl.BlockSpec((None, None, None, 2, PAST, HEAD_DIM), lambda b, h, q: (b, l, h, 0, 0, 0)),
                  pl.BlockSpec((None, None, None, PAST, DIFF_V), lambda b, h, q: (b, l, h, 0, 0)),
                  tab, tab, tab, vec(64), vec(64), vec(64), vec(64), vec(128)],
        out_specs=pl.BlockSpec((tq, 128), lambda b, h, q: (b * nq + q, h)),
        out_shape=jax.ShapeDtypeStruct((NS_TOK, DIFF_HEADS * DIFF_V), BF16),
        scratch_shapes=[pltpu.VMEM((DEC_SEQ + PAST, HEAD_DIM), BF16),
                        pltpu.VMEM((DEC_SEQ + PAST, HEAD_DIM), BF16),
                        pltpu.VMEM((DEC_SEQ + PAST, DIFF_V), BF16)],
        compiler_params=_cparams(("arbitrary", "arbitrary", "arbitrary")),
    )(p, p, p, cache_k, cache_v, cos, s_up, s_dn, lq1, lk1, lq2, lk2, g)


@functools.lru_cache(None)
def _dft_consts(L):
    n = 2 * L
    k = np.arange(L)
    ang = 2.0 * np.pi * ((k[:, None] * k[None, :]) % n) / n
    alt = (-1.0) ** k
    fa = np.cos(ang)
    fb = -np.sin(ang)
    fb[0, :] = alt
    wgt = np.full((L,), 2.0 / n)
    wgt[0] = 1.0 / n
    ga = fa * wgt[:, None]
    gb = fb * wgt[:, None]
    gb[0, :] = alt / n
    f = np.concatenate([fa, fb], axis=0)
    g = np.concatenate([ga.T, gb.T], axis=1)
    return np.asarray(f, dtype=BF16), np.asarray(g, dtype=BF16)


@functools.lru_cache(None)
def _filter_consts(L):
    f32 = np.float32
    t = np.linspace(0.0, 1.0, L, dtype=f32)[:, None]
    pos = np.arange(L, dtype=f32)[:, None]
    bands = np.linspace(1e-4, HY_BANDS - 1, HY_BANDS, dtype=f32)[None, :]
    ang = f32(2.0 * math.pi) * bands * pos / f32(L)
    z = np.zeros((L, 128), f32)
    z[:, 0:1] = t
    z[:, 1:1 + HY_BANDS] = np.cos(ang)
    z[:, 1 + HY_BANDS:HY_EMB] = -np.sin(ang)
    min_decay = math.log(1e-2) / 1.5
    max_decay = math.log(1e-2) / 0.3
    deltas = np.abs(np.linspace(min_decay, max_decay, HY_W, dtype=f32))
    decay = np.exp(-t * deltas[None, :]).astype(f32)
    return z, decay


def _spectra_kernel(z_ref, w1_ref, b1_ref, w2_ref, b2_ref, fr_ref, w3_ref, dec_ref, fa_ref, fb_ref,
                    sa_ref, sb_ref, filt_scr, *, L, kc):
    j = pl.program_id(0)

    @pl.when(j == 0)
    def _():
        fr = fr_ref[...]
        hdn = jnp.sin(fr * (_dot3(z_ref[...], w1_ref[...]) + b1_ref[...]))
        hdn = jnp.sin(fr * (_dot3(hdn, w2_ref[...]) + b2_ref[...]))
        dec = dec_ref[...]
        not_first = lax.broadcasted_iota(jnp.int32, (L, HY_W), 0) > 0
        for o in range(2):
            hf = _dot3(hdn, w3_ref[:, 512 * o:512 * o + 256]) * dec
            hb = jnp.where(not_first, _dot3(hdn, w3_ref[:, 512 * o + 256:512 * o + 512]) * dec, 0.0)
            nrm = (jnp.sum(jnp.abs(hf), axis=0, keepdims=True)
                   + jnp.sum(jnp.abs(hb), axis=0, keepdims=True))
            filt_scr[:, 512 * o:512 * o + 256] = (hf / nrm).astype(BF16)
            filt_scr[:, 512 * o + 256:512 * o + 512] = (hb / nrm).astype(BF16)

    ta = _dot(fa_ref[...], filt_scr[...])
    tb = _dot(fb_ref[...], filt_scr[...])
    first = (lax.broadcasted_iota(jnp.int32, (kc, HY_W), 0) + j * kc) == 0
    for o in range(2):
        af, ab = ta[:, 512 * o:512 * o + 256], ta[:, 512 * o + 256:512 * o + 512]
        bf, bb = tb[:, 512 * o:512 * o + 256], tb[:, 512 * o + 256:512 * o + 512]
        sa_ref[:, 256 * o:256 * o + 256] = af + ab
        sb_ref[:, 256 * o:256 * o + 256] = jnp.where(first, bf + bb, bf - bb)


def _hy_spectra(L, kc, w1p, b1, w2, b2, fr, w3):
    z, decay = _filter_consts(L)
    f = jnp.asarray(_dft_consts(L)[0])
    nj = L // kc
    full = lambda shape: pl.BlockSpec(shape, lambda j: tuple(0 for _ in shape))
    return pl.pallas_call(
        functools.partial(_spectra_kernel, L=L, kc=kc),
        grid=(nj,),
        in_specs=[full((L, 128)), full((128, HY_HID)), full((1, HY_HID)), full((HY_HID, HY_HID)),
                  full((1, HY_HID)), full((1, HY_HID)), full((HY_HID, 4 * HY_W)), full((L, HY_W)),
                  pl.BlockSpec((kc, L), lambda j: (j, 0)),
                  pl.BlockSpec((kc, L), lambda j: (j + nj, 0))],
        out_specs=[pl.BlockSpec((kc, 2 * HY_W), lambda j: (j, 0)),
                   pl.BlockSpec((kc, 2 * HY_W), lambda j: (j, 0))],
        out_shape=[jax.ShapeDtypeStruct((L, 2 * HY_W), F32)] * 2,
        scratch_shapes=[pltpu.VMEM((L, 4 * HY_W), BF16)],
        compiler_params=_cparams(("arbitrary",)),
    )(jnp.asarray(z), w1p, b1, w2, b2, fr, w3, jnp.asarray(decay), f, f)


def _hyconv_kernel(*refs, L, nb, kc, nj, row_blk0, staged):
    n_in = 1 if staged else nb
    (cw_ref, cb_ref, d_ref, fa_ref, fb_ref, ga_ref, gb_ref, sa_ref, sb_ref, o_ref,
     xin_scr, x_scr, g1_scr, g2_scr, y_scr) = refs[n_in:n_in + 15]
    bg = pl.program_id(0)
    o = pl.program_id(1)
    j = pl.program_id(2)

    @pl.when((o == 0) & (j == 0))
    def _():
        row = lax.broadcasted_iota(jnp.int32, (L, HY_W), 0)
        pieces = [(i, part) for i in range(nb) for part in range(3)]
        if staged:
            p_hbm, (stage, sem) = refs[0], refs[n_in + 15:]

            def piece_copy(k):
                i, part = pieces[k]
                row0 = pl.multiple_of((row_blk0 + bg * nb + i) * L, L)
                return pltpu.make_async_copy(
                    p_hbm.at[pl.ds(row0, L), C_HY + HY_W * part:C_HY + HY_W * (part + 1)],
                    stage.at[k % 2], sem.at[k % 2])

            piece_copy(0).start()
            piece_copy(1).start()
        for k, (i, part) in enumerate(pieces):
            cols = slice(HY_W * i, HY_W * (i + 1))
            dst = (x_scr, g1_scr, g2_scr)[part]
            pc = slice(HY_W * part, HY_W * (part + 1))
            if staged:
                piece_copy(k).wait()
                u = stage[k % 2].astype(F32)
            else:
                u = refs[i][:, pc].astype(F32)
            up = jnp.where(row == 0, 0.0, pltpu.roll(u, 1, axis=0))
            un = jnp.where(row == L - 1, 0.0, pltpu.roll(u, L - 1, axis=0))
            dst[:, cols] = (up * cw_ref[0:1, pc] + u * cw_ref[1:2, pc] + un * cw_ref[2:3, pc]
                            + cb_ref[:, pc])
            if part == 0:
                xin_scr[:, cols] = x_scr[:, cols].astype(BF16)
            if staged and k + 2 < len(pieces):
                piece_copy(k + 2).start()

    @pl.when(j == 0)
    def _():
        y_scr[...] = jnp.zeros_like(y_scr)

    xa = _dot(fa_ref[...], xin_scr[...])
    xb = _dot(fb_ref[...], xin_scr[...])
    sa = jnp.concatenate([sa_ref[...]] * nb, axis=1)
    sb = jnp.concatenate([sb_ref[...]] * nb, axis=1)
    first = (lax.broadcasted_iota(jnp.int32, (kc, nb * HY_W), 0) + j * kc) == 0
    ya = jnp.where(first, xa * sa, xa * sa - xb * sb)
    yb = jnp.where(first, xb * sb, xa * sb + xb * sa)
    y_scr[...] += _dot(ga_ref[...], ya.astype(BF16)) + _dot(gb_ref[...], yb.astype(BF16))

    @pl.when(j == nj - 1)
    def _():
        dvec = jnp.concatenate([d_ref[pl.ds(o, 1), :]] * nb, axis=1)
        y = y_scr[...] + x_scr[...] * dvec

        @pl.when(o == 0)
        def _():
            zz = g1_scr[...] * y
            x_scr[...] = zz
            xin_scr[...] = zz.astype(BF16)

        @pl.when(o == 1)
        def _():
            res = g2_scr[...] * y
            for i in range(nb):
                o_ref[L * i:L * (i + 1), :] = res[:, HY_W * i:HY_W * (i + 1)].astype(BF16)


def _hy_conv(p, L, nb, kc, row_blk0, n_seq, conv_w, conv_b, d, sa, sb):
    f, g = (jnp.asarray(a) for a in _dft_consts(L))
    nj = L // kc
    staged = nb * L > DEC_SEQ
    if staged:
        u_specs = [pl.BlockSpec(memory_space=pl.ANY)]
        u_scratch = [pltpu.VMEM((2, L, HY_W), p.dtype), pltpu.SemaphoreType.DMA((2,))]
    else:
        u_specs = [pl.BlockSpec((L, 3 * HY_W), lambda bg, o, j, i=i: (row_blk0 + bg * nb + i, C_HY // 768))
                   for i in range(nb)]
        u_scratch = []
    small = lambda shape: pl.BlockSpec(shape, lambda bg, o, j: (0, 0))
    return pl.pallas_call(
        functools.partial(_hyconv_kernel, L=L, nb=nb, kc=kc, nj=nj, row_blk0=row_blk0, staged=staged),
        grid=(n_seq // nb, 2, nj),
        in_specs=u_specs + [
            small((3, 3 * HY_W)), small((1, 3 * HY_W)), small((2, HY_W)),
            pl.BlockSpec((kc, L), lambda bg, o, j: (j, 0)),
            pl.BlockSpec((kc, L), lambda bg, o, j: (j + nj, 0)),
            pl.BlockSpec((L, kc), lambda bg, o, j: (0, j)),
            pl.BlockSpec((L, kc), lambda bg, o, j: (0, j + nj)),
            pl.BlockSpec((kc, HY_W), lambda bg, o, j: (j, o)),
            pl.BlockSpec((kc, HY_W), lambda bg, o, j: (j, o))],
        out_specs=pl.BlockSpec((nb * L, HY_W), lambda bg, o, j: (bg, 0)),
        out_shape=jax.ShapeDtypeStruct((n_seq * L, HY_W), BF16),
        scratch_shapes=[pltpu.VMEM((L, nb * HY_W), BF16)] + [pltpu.VMEM((L, nb * HY_W), F32)] * 4
        + u_scratch,
        compiler_params=_cparams(("arbitrary", "arbitrary", "arbitrary")),
    )(*([p] * len(u_specs)), conv_w, conv_b, d, f, f, g, g, sa, sb)


def _outproj_kernel(xp_ref, xs_ref, mixp_ref, hyp_ref, nas_ref, ds_ref, hys_ref, g1_ref, w_ref, o_ref):
    i = pl.program_id(0)
    w = lambda a, b: w_ref[a:b, :].astype(BF16)

    @pl.when(i < NP_TOK // TM)
    def _():
        y = _dot(mixp_ref[...], w(0, 768)) + _dot(hyp_ref[...], w(768, D))
        o_ref[...] = xp_ref[...] + g1_ref[...] * y

    @pl.when(i >= NP_TOK // TM)
    def _():
        y = _dot(nas_ref[...], w(0, 256)) + _dot(ds_ref[...], w(256, 768)) + _dot(hys_ref[...], w(768, D))
        o_ref[...] = xs_ref[...] + g1_ref[...] * y


def _x_specs(xs_block0):
    npt = NP_TOK // TM
    return [pl.BlockSpec((TM, D), lambda i, *_: (jnp.minimum(i, npt - 1), 0)),
            pl.BlockSpec((TM, D), lambda i, *_: (jnp.maximum(i - npt, 0) + xs_block0, 0))]


def _outproj(xp, xs, xs_block0, mix_p, hy_p, na_s, d_s, hy_s, mod6, w_out, l):
    npt = NP_TOK // TM
    pidx = lambda i: (jnp.minimum(i, npt - 1), 0)
    sidx = lambda i: (jnp.maximum(i - npt, 0), 0)
    return pl.pallas_call(
        _outproj_kernel,
        grid=(N_TOK // TM,),
        in_specs=_x_specs(xs_block0) + [
                  pl.BlockSpec((TM, 768), pidx),
                  pl.BlockSpec((TM, HY_W), pidx),
                  pl.BlockSpec((TM, 256), sidx),
                  pl.BlockSpec((TM, 512), sidx),
                  pl.BlockSpec((TM, HY_W), sidx),
                  _mod_spec(2, TM),
                  pl.BlockSpec((None, D, D), lambda i: (l, 0, 0))],
        out_specs=pl.BlockSpec((TM, D), lambda i: (i, 0)),
        out_shape=jax.ShapeDtypeStruct((N_TOK, D), F32),
        compiler_params=_cparams(("arbitrary",)),
    )(xp, xs, mix_p, hy_p, na_s, d_s, hy_s, mod6, w_out)


def _ffn_kernel(x_ref, g_ref, sc_ref, sh_ref, gate_ref, wg_ref, wu_ref, wd_ref, o_ref, h_scr, *, nj):
    j = pl.program_id(1)

    @pl.when(j == 0)
    def _():
        h_scr[...] = _norm_mod(x_ref[...], g_ref[...], sc_ref[...], sh_ref[...]).astype(BF16)
        o_ref[...] = jnp.zeros_like(o_ref)

    wg, wu, wd = (r[...].astype(BF16) for r in (wg_ref, wu_ref, wd_ref))
    half = h_scr.shape[0] // 2
    for c in range(2):
        rows = slice(c * half, (c + 1) * half)
        h = h_scr[rows, :]
        a = _silu(_dot(h, wg)) * _dot(h, wu)
        o_ref[rows, :] += _dot(a.astype(BF16), wd)

    @pl.when(j == nj - 1)
    def _():
        o_ref[...] = x_ref[...] + gate_ref[...] * o_ref[...]


def _dense_ffn(x, g, mod6, wg, wu, wd, i_ffn):
    tf = 256
    tm = DEC_SEQ
    nj = FFN // tf
    return pl.pallas_call(
        functools.partial(_ffn_kernel, nj=nj),
        grid=(N_TOK // tm, nj),
        in_specs=[pl.BlockSpec((tm, D), lambda i, j: (i, 0)),
                  pl.BlockSpec((1, D), lambda i, j: (0, 0)),
                  _mod_spec(4, tm), _mod_spec(3, tm), _mod_spec(5, tm),
                  pl.BlockSpec((None, D, tf), lambda i, j: (i_ffn, 0, j)),
                  pl.BlockSpec((None, D, tf), lambda i, j: (i_ffn, 0, j)),
                  pl.BlockSpec((None, tf, D), lambda i, j: (i_ffn, j, 0))],
        out_specs=pl.BlockSpec((tm, D), lambda i, j: (i, 0)),
        out_shape=jax.ShapeDtypeStruct((N_TOK, D), F32),
        scratch_shapes=[pltpu.VMEM((tm, D), BF16)],
        compiler_params=_cparams(("arbitrary", "arbitrary")),
    )(x, g, mod6, mod6, mod6, wg, wu, wd)


def _router_kernel(x_ref, g_ref, sc_ref, sh_ref, wr_ref, h_ref, r_ref):
    h = _norm_mod(x_ref[...], g_ref[...], sc_ref[...], sh_ref[...])
    h_ref[...] = h.astype(BF16)
    lane = lax.broadcasted_iota(jnp.int32, (TM, 128), 1)
    lg = jnp.where(lane < N_EXP, _dot3(h, wr_ref[...]), -jnp.inf)
    m1 = jnp.max(lg, axis=-1, keepdims=True)
    i1 = jnp.min(jnp.where(lg == m1, lane, 128), axis=-1, keepdims=True)
    lg2 = jnp.where(lane == i1, -jnp.inf, lg)
    m2 = jnp.max(lg2, axis=-1, keepdims=True)
    i2 = jnp.min(jnp.where(lg2 == m2, lane, 128), axis=-1, keepdims=True)
    e = jnp.exp(m2 - m1)
    w1 = 1.0 / (1.0 + e)
    w2 = e / (1.0 + e)
    r_ref[...] = jnp.where(lane == 0, i1.astype(F32),
                           jnp.where(lane == 1, i2.astype(F32),
                                     jnp.where(lane == 2, w1, jnp.where(lane == 3, w2, 0.0))))


def _router(x, g, mod6, wr_pad):
    return pl.pallas_call(
        _router_kernel,
        grid=(N_TOK // TM,),
        in_specs=[pl.BlockSpec((TM, D), lambda i: (i, 0)),
                  pl.BlockSpec((1, D), lambda i: (0, 0)),
                  _mod_spec(4, TM), _mod_spec(3, TM),
                  pl.BlockSpec((D, 128), lambda i: (0, 0))],
        out_specs=[pl.BlockSpec((TM, D), lambda i: (i, 0)),
                   pl.BlockSpec((TM, 128), lambda i: (i, 0))],
        out_shape=[jax.ShapeDtypeStruct((N_TOK, D), BF16), jax.ShapeDtypeStruct((N_TOK, 128), F32)],
        compiler_params=_cparams(("arbitrary",)),
    )(x, g, mod6, mod6, wr_pad)


def _dispatch_kernel(blo_ref, bhi_ref, sexp_ref, h_hbm, rp_ref, o_ref, h_scr, sem):
    s = pl.program_id(0)

    @pl.when(s == 0)
    def _():
        cp = pltpu.make_async_copy(h_hbm, h_scr, sem)
        cp.start()
        cp.wait()

    e = sexp_ref[s]
    rows = s * MOE_TS + lax.broadcasted_iota(jnp.int32, (MOE_TS, MOE_DTB), 0)

    def sel(b):
        return (rows == rp_ref[e, pl.ds(b, 1), :]).astype(BF16)

    is_empty = blo_ref[s] > bhi_ref[s]

    @pl.when(is_empty)
    def _():
        o_ref[...] = jnp.zeros_like(o_ref)

    @pl.when(jnp.logical_not(is_empty))
    def _():
        b0 = jnp.minimum(blo_ref[s], N_TOK // MOE_DTB - MOE_DWIN)
        hwin = h_scr[pl.ds(pl.multiple_of(b0 * MOE_DTB, MOE_DTB), MOE_DWIN * MOE_DTB), :]
        acc = _dot(jnp.concatenate([sel(b0 + k) for k in range(MOE_DWIN)], axis=1), hwin)

        def body(b, acc):
            hb = h_scr[pl.ds(pl.multiple_of(b * MOE_DTB, MOE_DTB), MOE_DTB), :]
            return acc + _dot(sel(b), hb)

        acc = lax.fori_loop(b0 + MOE_DWIN, bhi_ref[s] + 1, body, acc)
        o_ref[...] = acc.astype(BF16)


def _dispatch(blo, bhi, sexp, h, rp_t):
    return pl.pallas_call(
        _dispatch_kernel,
        grid_spec=pltpu.PrefetchScalarGridSpec(
            num_scalar_prefetch=3,
            grid=(MOE_ROWS // MOE_TS,),
            in_specs=[pl.BlockSpec(memory_space=pl.ANY),
                      pl.BlockSpec((N_EXP, N_TOK // MOE_DTB, MOE_DTB), lambda s, *_: (0, 0, 0))],
            out_specs=pl.BlockSpec((MOE_TS, D), lambda s, *_: (s, 0)),
            scratch_shapes=[pltpu.VMEM((N_TOK, D), BF16), pltpu.SemaphoreType.DMA(())]),
        out_shape=jax.ShapeDtypeStruct((MOE_ROWS, D), BF16),
        compiler_params=_cparams(("arbitrary",)),
    )(blo, bhi, sexp, h, rp_t)


def _experts_kernel(te_ref, used_ref, rows_ref, xs_ref, wg_ref, wu_ref, wd_ref, o_ref, acc_scr, *, nj):
    i = pl.program_id(0)
    j = pl.program_id(1)
    nrows = rows_ref[i]

    @pl.when(j == 0)
    def _():
        acc_scr[...] = jnp.zeros_like(acc_scr)

    for sz in range(MOE_TQ, MOE_TM + 1, MOE_TQ):
        @pl.when(nrows == sz)
        def _():
            xb = xs_ref[0:sz, :]
            a = _silu(_dot(xb, wg_ref[...].astype(BF16))) * _dot(xb, wu_ref[...].astype(BF16))
            acc_scr[0:sz, :] += _dot(a.astype(BF16), wd_ref[...].astype(BF16))

    @pl.when(j == nj - 1)
    def _():
        o_ref[...] = acc_scr[...].astype(BF16)


def _experts(tile_expert, used, tile_rows, xs, wg, wu, wd, i_moe):
    nj = EXP_DIM // MOE_TF

    def tile(i, used):
        return jnp.minimum(i, used[0] - 1)

    def chunk(i, j, used):
        return jnp.where(i < used[0], j, nj - 1)

    return pl.pallas_call(
        functools.partial(_experts_kernel, nj=nj),
        grid_spec=pltpu.PrefetchScalarGridSpec(
            num_scalar_prefetch=3,
            grid=(MOE_YROWS // MOE_TM, nj),
            in_specs=[pl.BlockSpec((MOE_TM, D), lambda i, j, te, used, tr: (tile(i, used), 0)),
                      pl.BlockSpec((None, None, D, MOE_TF),
                                   lambda i, j, te, used, tr: (i_moe, te[tile(i, used)], 0, chunk(i, j, used))),
                      pl.BlockSpec((None, None, D, MOE_TF),
                                   lambda i, j, te, used, tr: (i_moe, te[tile(i, used)], 0, chunk(i, j, used))),
                      pl.BlockSpec((None, None, MOE_TF, D),
                                   lambda i, j, te, used, tr: (i_moe, te[tile(i, used)], chunk(i, j, used), 0))],
            out_specs=pl.BlockSpec((MOE_TM, D), lambda i, j, te, used, tr: (i, 0)),
            scratch_shapes=[pltpu.VMEM((MOE_TM, D), F32)]),
        out_shape=jax.ShapeDtypeStruct((MOE_YROWS, D), BF16),
        compiler_params=_cparams(("arbitrary", "arbitrary")),
    )(tile_expert, used, tile_rows, xs, wg, wu, wd)


def _combine_kernel(ws_ref, kind_ref, x_ref, rp_ref, comb_ref, gate_ref, fg_ref, ys_hbm, op_ref, os_ref,
                    win_scr, y_scr, sem):
    b = pl.program_id(0)
    nb = pl.num_programs(0)
    slot = b % 2

    def win_copy(blk, sl, e):
        start = pl.multiple_of(ws_ref[blk * N_EXP + e], MOE_ALIGN)
        return pltpu.make_async_copy(ys_hbm.at[pl.ds(start, MOE_WIN)], win_scr.at[sl, e], sem.at[sl, e])

    @pl.when(b == 0)
    def _():
        for e in range(N_EXP):
            win_copy(0, 0, e).start()

    @pl.when(b + 1 < nb)
    def _():
        for e in range(N_EXP):
            win_copy(b + 1, 1 - slot, e).start()

    col = lax.broadcasted_iota(jnp.int32, (MOE_TB, MOE_WIN), 1)
    y_scr[...] = jnp.zeros_like(y_scr)
    for e in range(N_EXP):
        win_copy(b, slot, e).wait()
        rel = rp_ref[:, e:e + 1] - ws_ref[b * N_EXP + e]
        cw = comb_ref[:, e:e + 1]
        kind = kind_ref[b * N_EXP + e]

        for k, wn in enumerate(MOE_WINS):
            @pl.when(kind == k)
            def _():
                sel = (rel == col[:, :wn]).astype(BF16) if wn == MOE_WIN else (
                    rel == lax.broadcasted_iota(jnp.int32, (MOE_TB, wn), 1)).astype(BF16)
                y_scr[...] += cw * _dot(sel, win_scr[slot, e, 0:wn, :])

    x = x_ref[...] + gate_ref[...] * y_scr[...]
    x = (x * lax.rsqrt(jnp.mean(x * x, axis=-1, keepdims=True) + EPS)) * fg_ref[...]

    @pl.when(b < NP_TOK // MOE_TB)
    def _():
        op_ref[...] = x

    @pl.when(b >= NP_TOK // MOE_TB)
    def _():
        os_ref[...] = x


def _combine(ws, kind, x, rp8, comb, mod6, final_g, ys):
    npb = NP_TOK // MOE_TB
    return pl.pallas_call(
        _combine_kernel,
        grid_spec=pltpu.PrefetchScalarGridSpec(
            num_scalar_prefetch=2,
            grid=(N_TOK // MOE_TB,),
            in_specs=[pl.BlockSpec((MOE_TB, D), lambda b, *_: (b, 0)),
                      pl.BlockSpec((MOE_TB, N_EXP), lambda b, *_: (b, 0)),
                      pl.BlockSpec((MOE_TB, N_EXP), lambda b, *_: (b, 0)),
                      _mod_spec(5, MOE_TB),
                      pl.BlockSpec((1, D), lambda b, *_: (0, 0)),
                      pl.BlockSpec(memory_space=pl.ANY)],
            out_specs=[pl.BlockSpec((MOE_TB, D), lambda b, *_: (jnp.minimum(b, npb - 1), 0)),
                       pl.BlockSpec((MOE_TB, D), lambda b, *_: (jnp.maximum(b - npb, 0), 0))],
            scratch_shapes=[pltpu.VMEM((2, N_EXP, MOE_WIN, D), BF16),
                            pltpu.VMEM((MOE_TB, D), F32),
                            pltpu.SemaphoreType.DMA((2, N_EXP))]),
        out_shape=[jax.ShapeDtypeStruct((NP_TOK, D), F32), jax.ShapeDtypeStruct((NS_TOK, D), F32)],
        compiler_params=_cparams(("arbitrary",)),
    )(ws, kind, x, rp8, comb, mod6, final_g, ys)


def _moe(x, g, mod6, router, wg, wu, wd, i_moe, final_g):
    wr_pad = jnp.pad(router, ((0, 0), (0, 128 - N_EXP)))
    h, r = _router(x, g, mod6, wr_pad)

    i32 = jnp.int32
    i12 = r[:, 0:2].astype(i32)
    earange = jnp.arange(N_EXP, dtype=i32)
    hit1 = i12[:, 0:1] == earange[None, :]
    hit2 = i12[:, 1:2] == earange[None, :]
    comb = jnp.where(hit1, r[:, 2:3], 0.0) + jnp.where(hit2, r[:, 3:4], 0.0)
    mask = (hit1 | hit2).astype(i32)
    csum = jnp.cumsum(mask, axis=0)
    counts = csum[-1]
    padded = ((counts + MOE_TM - 1) // MOE_TM) * MOE_TM
    ends = jnp.cumsum(padded)
    starts = ends - padded
    rp8 = jnp.where(mask > 0, starts[None, :] + csum - 1, -1).astype(i32)
    n_tiles = MOE_ROWS // MOE_TM
    tile_expert = jnp.minimum(
        jnp.searchsorted(ends, jnp.arange(n_tiles, dtype=i32) * MOE_TM, side="right"),
        N_EXP - 1).astype(i32)
    used = (ends[-1:] // MOE_TM).astype(i32)
    tile_row0 = jnp.arange(MOE_YROWS // MOE_TM, dtype=i32) * MOE_TM
    te_all = jnp.minimum(jnp.searchsorted(ends, tile_row0, side="right"), N_EXP - 1)
    valid = jnp.clip(counts[te_all] - (tile_row0 - starts[te_all]), 0, MOE_TM)
    valid = jnp.where(tile_row0 < ends[-1], valid, 0)
    tile_rows = (((valid + MOE_TQ - 1) // MOE_TQ) * MOE_TQ).astype(i32)

    sub_row0 = jnp.arange(MOE_ROWS // MOE_TS, dtype=i32) * MOE_TS
    sexp = tile_expert[sub_row0 // MOE_TM]
    qlo = sub_row0 - starts[sexp]
    qend = jnp.minimum(qlo + MOE_TS, counts[sexp])
    cbe = csum[MOE_DTB - 1::MOE_DTB, :].T[sexp]
    blo = jnp.sum((cbe <= qlo[:, None]).astype(i32), axis=1)
    bhi = jnp.minimum(jnp.sum((cbe < qend[:, None]).astype(i32), axis=1), N_TOK // MOE_DTB - 1)
    empty = qend <= qlo
    blo = jnp.where(empty, 1, blo).astype(i32)
    bhi = jnp.where(empty, 0, bhi).astype(i32)

    cb = csum[MOE_TB - 1::MOE_TB, :]
    cprev = jnp.concatenate([jnp.zeros((1, N_EXP), i32), cb[:-1]], axis=0)
    ws = (((starts[None, :] + cprev) // MOE_ALIGN) * MOE_ALIGN).reshape(-1).astype(i32)
    n_be = cb - cprev
    kind = sum((n_be > wn - MOE_ALIGN).astype(i32) for wn in MOE_WINS[:-1])
    kind = jnp.where(n_be == 0, len(MOE_WINS), kind).reshape(-1).astype(i32)

    rp_t = rp8.T.reshape(N_EXP, N_TOK // MOE_DTB, MOE_DTB)
    xs = _dispatch(blo, bhi, sexp, h, rp_t)
    ys = _experts(tile_expert, used, tile_rows, xs, wg, wu, wd, i_moe)
    return _combine(ws, kind, x, rp8, comb, mod6, final_g, ys)


assert DEPTH == 2

def kernel(x_prompt, x_sample, cache_na_k, cache_na_v, cache_diff_k, cache_diff_v, c, c_ctx, w_in, w_out, ada_w, ada_b, norm_mix_g, norm_ffn_g, na_rpb, diff_lq1, diff_lk1, diff_lq2, diff_lk2, diff_subln_g, hy_conv_w, hy_conv_b, hy_d, hy_f_w1, hy_f_b1, hy_f_w2, hy_f_b2, hy_f_freq, hy_f_w3, ffn_w_gate, ffn_w_up, ffn_w_down, moe_router, moe_w_gate, moe_w_up, moe_w_down, final_norm_g):
    xparts = (x_prompt.reshape(NP_TOK, D), x_sample.reshape(NS_TOK, D), 0)
    cond8 = jnp.concatenate([c_ctx[None, :], c, jnp.zeros((5, D), F32)], axis=0)
    mods = _modulation(cond8, ada_w, ada_b)
    final_g = final_norm_g.reshape(1, D)
    bias = _na_bias(na_rpb)

    leaves = [jnp.zeros((BATCH, DEPTH, NA_HEADS, SEQ, HEAD_DIM), F32),
              jnp.zeros((BATCH, DEPTH, NA_HEADS, SEQ, HEAD_DIM), F32),
              jnp.zeros((BATCH, DEPTH, DIFF_HEADS, 2, SEQ, HEAD_DIM), F32),
              jnp.zeros((BATCH, DEPTH, DIFF_HEADS, SEQ, DIFF_V), F32)]
    for l in range(DEPTH):
        lam_init = 0.8 - 0.6 * math.exp(-0.3 * l)
        mod6 = mods[l].reshape(8, 6, D).transpose(1, 0, 2).reshape(6, 8, 1, D)
        row = lambda a: a[l].reshape(1, -1)
        lq1, lk1, lq2, lk2, subg = row(diff_lq1), row(diff_lk1), row(diff_lq2), row(diff_lk2), row(diff_subln_g)

        p, *leaves = _inproj(*xparts, row(norm_mix_g), mod6, w_in, l, leaves)

        mix_p = _prompt_attn(p, lq1, lk1, lq2, lk2, subg, lam_init)
        na_s = _na_attn(p, cache_na_k, cache_na_v, bias, l)
        d_s = _sdiff_attn(p, cache_diff_k, cache_diff_v, lq1, lk1, lq2, lk2, subg, lam_init, l)

        w1p = jnp.pad(hy_f_w1[l], ((0, 128 - HY_EMB), (0, 0)))
        fargs = (w1p, row(hy_f_b1), hy_f_w2[l], row(hy_f_b2), row(hy_f_freq), hy_f_w3[l])
        cargs = (hy_conv_w[l], row(hy_conv_b), hy_d[l])
        sa_p, sb_p = _hy_spectra(SEQ, SEQ, *fargs)
        hy_p = _hy_conv(p, SEQ, 4, SEQ, 0, BATCH, *cargs, sa_p, sb_p)
        sa_s, sb_s = _hy_spectra(DEC_SEQ, 512, *fargs)
        hy_s = _hy_conv(p, DEC_SEQ, 2, 512, NP_TOK // DEC_SEQ, DEC_BATCH, *cargs, sa_s, sb_s)

        x = _outproj(*xparts, mix_p, hy_p, na_s, d_s, hy_s, mod6, w_out, l)

        if l == 0:
            x = _dense_ffn(x, row(norm_ffn_g), mod6, ffn_w_gate, ffn_w_up, ffn_w_down, 0)
            xparts = (x, x, NP_TOK // TM)
        else:
            yp, ys = _moe(x, row(norm_ffn_g), mod6, moe_router[0], moe_w_gate, moe_w_up, moe_w_down,
                          0, final_g)

    return (yp.reshape(BATCH, SEQ, D), ys.reshape(DEC_BATCH, DEC_SEQ, D), *leaves)
```

```python
import functools
import math

import numpy as np
import jax
import jax.numpy as jnp
from jax import lax
from jax.experimental import pallas as pl
from jax.experimental.pallas import tpu as pltpu

F32 = jnp.float32
BF16 = jnp.bfloat16

D = 1024
BATCH, SEQ = 32, 256
DEC_BATCH, DEC_SEQ = 2, 2048
DEPTH = 2
PAST = 512
GRID_W = 64
GRID_ROWS = DEC_SEQ // GRID_W
HEAD_DIM = 64
NA_HEADS = 4
DIFF_HEADS = 4
DIFF_V = 128
WIN_ROWS, WIN_COLS = 8, 16
HY_W = 256
HY_EMB = 33
HY_BANDS = 16
HY_HID = 64
PROJ = 3072
FFN = 2816
N_EXP = 8
EXP_DIM = 3584
EPS = 1e-6
ROPE_BASE = 10000.0

NP_TOK = BATCH * SEQ
NS_TOK = DEC_BATCH * DEC_SEQ
N_TOK = NP_TOK + NS_TOK

C_NAQ, C_NAK, C_NAV = 0, 256, 512
C_DQ, C_DK, C_DV = 768, 1280, 1792
C_HY = 2304

TM = 1024
NA_RB = 4
VMEM_LIMIT = 56 * 1024 * 1024

MOE_TM = 1024
MOE_TF = 512
MOE_TQ = 256
MOE_ROWS = 2 * N_TOK + N_EXP * MOE_TM
MOE_TS = 256
MOE_DTB = 512
MOE_DWIN = 3
MOE_TB = 512
MOE_ALIGN = 16
MOE_WIN = MOE_TB + MOE_ALIGN
MOE_WINS = (MOE_TB // 4 + MOE_ALIGN, MOE_TB // 2 + MOE_ALIGN, MOE_WIN)
MOE_YROWS = MOE_ROWS + MOE_TM


def _cparams(sem):
    return pltpu.CompilerParams(dimension_semantics=sem, vmem_limit_bytes=VMEM_LIMIT)


def _dot(a, b):
    return jnp.dot(a, b, preferred_element_type=F32)


def _dot_nt(a, b):
    return lax.dot_general(a, b, (((1,), (1,)), ((), ())), preferred_element_type=F32)


def _split(a):
    hi = a.astype(BF16)
    lo = (a - hi.astype(F32)).astype(BF16)
    return hi, lo


def _dot3(a, b):
    ah, al = _split(a)
    bh, bl = _split(b)
    return _dot(ah, bh) + (_dot(ah, bl) + _dot(al, bh))


def _silu(x):
    return x / (1.0 + jnp.exp(-x))


def _mod_row(i, tm):
    t = i * tm
    return jnp.where(t < NP_TOK, 0, 1 + (t - NP_TOK) // DEC_SEQ)


def _mod_spec(k, tm):
    return pl.BlockSpec((None, None, 1, D), lambda i, *_: (k, _mod_row(i, tm), 0, 0))


def _norm_mod(x, g, sc, sh):
    y = x * lax.rsqrt(jnp.mean(x * x, axis=-1, keepdims=True) + EPS)
    return (y * g) * (1.0 + sc) + sh


def _mod_kernel(c_ref, w_ref, b_ref, o_ref):
    o_ref[...] = _dot3(_silu(c_ref[...]), w_ref[...]) + b_ref[...]


def _modulation(cond8, ada_w, ada_b):
    tn = 1536
    return pl.pallas_call(
        _mod_kernel,
        grid=(DEPTH, 6 * D // tn),
        in_specs=[pl.BlockSpec((8, D), lambda l, j: (0, 0)),
                  pl.BlockSpec((None, D, tn), lambda l, j: (l, 0, j)),
                  pl.BlockSpec((None, 1, tn), lambda l, j: (l, 0, j))],
        out_specs=pl.BlockSpec((None, 8, tn), lambda l, j: (l, 0, j)),
        out_shape=jax.ShapeDtypeStruct((DEPTH, 8, 6 * D), F32),
        compiler_params=_cparams(("arbitrary", "arbitrary")),
    )(cond8, ada_w, ada_b.reshape(DEPTH, 1, 6 * D))


def _inproj_kernel(xp_ref, xs_ref, g_ref, sc_ref, sh_ref, w_hbm, nak_in, nav_in, dk_in, dv_in,
                   o_ref, nak_ref, nav_ref, dk_ref, dv_ref, h_scr, w_scr, stage, acc_scr, sem, *, l, tn):
    del nak_in, nav_in, dk_in, dv_in
    i = pl.program_id(0)
    j = pl.program_id(1)

    @pl.when((i == 0) & (j == 0))
    def _():
        for c in range(PROJ // tn):
            cp = pltpu.make_async_copy(w_hbm.at[l, :, c * tn:(c + 1) * tn], stage, sem)
            cp.start()
            cp.wait()
            w_scr[c] = stage[...].astype(BF16)

    is_ctx = i < NP_TOK // TM

    @pl.when((j == 0) & is_ctx)
    def _():
        h_scr[...] = _norm_mod(xp_ref[...], g_ref[...], sc_ref[...], sh_ref[...]).astype(BF16)

    @pl.when((j == 0) & jnp.logical_not(is_ctx))
    def _():
        h_scr[...] = _norm_mod(xs_ref[...], g_ref[...], sc_ref[...], sh_ref[...]).astype(BF16)

    acc_scr[...] = _dot(h_scr[...], w_scr[j])
    o_ref[...] = acc_scr[...].astype(BF16)


    def rows(bb):
        return slice(bb * SEQ, (bb + 1) * SEQ)

    def copy_heads(dst_ref, c0, width, heads, sub=None):
        for bb in range(TM // SEQ):
            for h in heads:
                if sub is None:
                    dst_ref[bb, h] = acc_scr[rows(bb), c0(h):c0(h) + width]
                else:
                    for s in range(2):
                        dst_ref[bb, h, s] = acc_scr[rows(bb), c0(h) + s * width:c0(h) + (s + 1) * width]

    @pl.when(is_ctx & (j == 0))
    def _():
        copy_heads(nak_ref, lambda h: C_NAK + HEAD_DIM * h, HEAD_DIM, range(NA_HEADS))
        copy_heads(nav_ref, lambda h: C_NAV + HEAD_DIM * h, HEAD_DIM, range(NA_HEADS))

    @pl.when(is_ctx & (j == 1))
    def _():
        copy_heads(dk_ref, lambda h: C_DK - tn + 2 * HEAD_DIM * h, HEAD_DIM, range(0, 2), sub=True)

    @pl.when(is_ctx & (j == 2))
    def _():
        copy_heads(dk_ref, lambda h: C_DK - 2 * tn + 2 * HEAD_DIM * h, HEAD_DIM, range(2, 4), sub=True)
        copy_heads(dv_ref, lambda h: C_DV - 2 * tn + DIFF_V * h, DIFF_V, range(DIFF_HEADS))


def _inproj(xp, xs, xs_block0, g, mod6, w_in, l, leaves):
    tn = 768
    nb = TM // SEQ
    ctx = lambda i: jnp.minimum(i, NP_TOK // TM - 1)
    leaf_specs = [pl.BlockSpec((nb, None, NA_HEADS, SEQ, HEAD_DIM), lambda i, j: (ctx(i), l, 0, 0, 0)),
                  pl.BlockSpec((nb, None, NA_HEADS, SEQ, HEAD_DIM), lambda i, j: (ctx(i), l, 0, 0, 0)),
                  pl.BlockSpec((nb, None, DIFF_HEADS, 2, SEQ, HEAD_DIM), lambda i, j: (ctx(i), l, 0, 0, 0, 0)),
                  pl.BlockSpec((nb, None, DIFF_HEADS, SEQ, DIFF_V), lambda i, j: (ctx(i), l, 0, 0, 0))]
    return pl.pallas_call(
        functools.partial(_inproj_kernel, l=l, tn=tn),
        grid=(N_TOK // TM, PROJ // tn),
        in_specs=_x_specs(xs_block0) + [
                  pl.BlockSpec((1, D), lambda i, j: (0, 0)),
                  _mod_spec(1, TM), _mod_spec(0, TM),
                  pl.BlockSpec(memory_space=pl.ANY)] + [pl.BlockSpec(memory_space=pl.ANY)] * 4,
        out_specs=[pl.BlockSpec((TM, tn), lambda i, j: (i, j))] + leaf_specs,
        out_shape=[jax.ShapeDtypeStruct((N_TOK, PROJ), BF16)]
        + [jax.ShapeDtypeStruct(a.shape, a.dtype) for a in leaves],
        input_output_aliases={6: 1, 7: 2, 8: 3, 9: 4},
        scratch_shapes=[pltpu.VMEM((TM, D), BF16), pltpu.VMEM((PROJ // tn, D, tn), BF16),
                        pltpu.VMEM((D, tn), F32), pltpu.VMEM((TM, tn), F32), pltpu.SemaphoreType.DMA(())],
        compiler_params=_cparams(("arbitrary", "arbitrary")),
    )(xp, xs, g, mod6, mod6, w_in, *leaves)


def _lam(lq1, lk1, lq2, lk2, lam_init):
    return (jnp.exp(jnp.sum(lq1[...] * lk1[...], axis=-1, keepdims=True))
            - jnp.exp(jnp.sum(lq2[...] * lk2[...], axis=-1, keepdims=True)) + lam_init)


def _softmax_parts(s):
    m = jnp.max(s, axis=-1, keepdims=True)
    e = jnp.exp(s - m)
    return e, jnp.sum(e, axis=-1, keepdims=True)


QK_SCALE = HEAD_DIM ** -0.5


def _diff_head(q1, q2, k1, k2, v, lam, g, lam_init):
    e1, l1 = _softmax_parts(_dot_nt(q1, k1))
    e2, l2 = _softmax_parts(_dot_nt(q2, k2))
    o = _dot(e1.astype(BF16), v) * (1.0 / l1) - _dot(e2.astype(BF16), v) * (lam * (1.0 / l2))
    o = o * lax.rsqrt(jnp.mean(o * o, axis=-1, keepdims=True) + EPS)
    return (o * g) * (1.0 - lam_init)


def _prompt_attn_kernel(pa_ref, pb_ref, pc_ref, lq1, lk1, lq2, lk2, g_ref, o_ref, *, lam_init):
    lam = _lam(lq1, lk1, lq2, lk2, lam_init)
    g = g_ref[...]

    def col(c0, w, scale=None):
        ref = (pa_ref, pb_ref, pc_ref)[c0 // 768]
        o = c0 % 768
        a = ref[:, o:o + w]
        return (a if scale is None else a * scale).astype(BF16)

    for h in range(NA_HEADS):
        q = col(C_NAQ + 64 * h, 64, QK_SCALE)
        k = col(C_NAK + 64 * h, 64)
        v = col(C_NAV + 64 * h, 64)
        e, l = _softmax_parts(_dot_nt(q, k))
        o = _dot(e.astype(BF16), v) * (1.0 / l)
        o_ref[:, 64 * h:64 * h + 64] = o.astype(BF16)
    for h in range(DIFF_HEADS):
        q1 = col(C_DQ + 128 * h, 64, QK_SCALE)
        q2 = col(C_DQ + 128 * h + 64, 64, QK_SCALE)
        k1 = col(C_DK + 128 * h, 64)
        k2 = col(C_DK + 128 * h + 64, 64)
        v = col(C_DV + 128 * h, 128)
        o = _diff_head(q1, q2, k1, k2, v, lam, g, lam_init)
        o_ref[:, 256 + 128 * h:384 + 128 * h] = o.astype(BF16)


def _prompt_attn(p, lq1, lk1, lq2, lk2, g, lam_init):
    vec = lambda n: pl.BlockSpec((1, n), lambda b: (0, 0))
    return pl.pallas_call(
        functools.partial(_prompt_attn_kernel, lam_init=lam_init),
        grid=(BATCH,),
        in_specs=[pl.BlockSpec((SEQ, 768), lambda b: (b, 0)),
                  pl.BlockSpec((SEQ, 768), lambda b: (b, 1)),
                  pl.BlockSpec((SEQ, 768), lambda b: (b, 2)),
                  vec(64), vec(64), vec(64), vec(64), vec(128)],
        out_specs=pl.BlockSpec((SEQ, 768), lambda b: (b, 0)),
        out_shape=jax.ShapeDtypeStruct((NP_TOK, 768), BF16),
        compiler_params=_cparams(("arbitrary",)),
    )(p, p, p, lq1, lk1, lq2, lk2, g)


def _bias_kernel(rpb_ref, o_ref):
    lh = pl.program_id(0)
    qc = lax.broadcasted_iota(jnp.int32, (GRID_W, GRID_W), 0)
    kc = lax.broadcasted_iota(jnp.int32, (GRID_W, GRID_W), 1)
    delta = jnp.clip(kc - qc + (WIN_COLS - 1), 0, 2 * WIN_COLS - 2)
    qs = jnp.clip(qc - WIN_COLS // 2, 0, GRID_W - WIN_COLS)
    in_win = (kc >= qs) & (kc < qs + WIN_COLS)
    for dr in range(2 * WIN_ROWS - 1):
        base = (lh * (2 * WIN_ROWS - 1) + dr) * (2 * WIN_COLS - 1)
        acc = jnp.zeros((GRID_W, GRID_W), F32)
        for d in range(2 * WIN_COLS - 1):
            acc = jnp.where(delta == d, rpb_ref[base + d], acc)
        piece = jnp.where(in_win, acc, -jnp.inf)
        for case in range(WIN_ROWS):
            i = dr + case - (WIN_ROWS - 1)
            if 0 <= i < WIN_ROWS:
                o_ref[case, :, i * GRID_W:(i + 1) * GRID_W] = piece


def _na_bias(rpb):
    return pl.pallas_call(
        _bias_kernel,
        grid=(DEPTH * NA_HEADS,),
        in_specs=[pl.BlockSpec(memory_space=pltpu.SMEM)],
        out_specs=pl.BlockSpec((None, WIN_ROWS, GRID_W, WIN_ROWS * GRID_W), lambda lh: (lh, 0, 0, 0)),
        out_shape=jax.ShapeDtypeStruct((DEPTH * NA_HEADS, WIN_ROWS, GRID_W, WIN_ROWS * GRID_W), F32),
        compiler_params=_cparams(("arbitrary",)),
    )(rpb.reshape(-1))


def _na_kernel(q_ref, kv_ref, kc_ref, vc_ref, bias_ref, o_ref):
    nloc = WIN_ROWS * GRID_W
    for rr in range(NA_RB):
        r = pl.program_id(1) * NA_RB + rr
        start = jnp.clip(r - WIN_ROWS // 2, 0, GRID_ROWS - WIN_ROWS)
        case = r - start
        row0 = pl.multiple_of(start * GRID_W, GRID_W)
        qrows = slice(rr * GRID_W, (rr + 1) * GRID_W)
        for h in range(NA_HEADS):
            q = (q_ref[qrows, C_NAQ + 64 * h:C_NAQ + 64 * h + 64] * QK_SCALE).astype(BF16)
            k = kv_ref[pl.ds(row0, nloc), C_NAK + 64 * h:C_NAK + 64 * h + 64].astype(BF16)
            v = kv_ref[pl.ds(row0, nloc), C_NAV + 64 * h:C_NAV + 64 * h + 64].astype(BF16)
            s_loc = _dot_nt(q, k) + bias_ref[h, pl.ds(case, 1)][0]
            s_ctx = _dot_nt(q, kc_ref[h].astype(BF16))
            m = jnp.maximum(jnp.max(s_loc, axis=-1, keepdims=True), jnp.max(s_ctx, axis=-1, keepdims=True))
            e_loc = jnp.exp(s_loc - m)
            e_ctx = jnp.exp(s_ctx - m)
            l = jnp.sum(e_loc, axis=-1, keepdims=True) + jnp.sum(e_ctx, axis=-1, keepdims=True)
            o = (_dot(e_loc.astype(BF16), v) + _dot(e_ctx.astype(BF16), vc_ref[h].astype(BF16))) * (1.0 / l)
            o_ref[qrows, 64 * h:64 * h + 64] = o.astype(BF16)


def _na_attn(p, cache_k, cache_v, bias, l):
    qblk0 = NP_TOK // (NA_RB * GRID_W)
    kvblk0 = NP_TOK // DEC_SEQ
    nrg = GRID_ROWS // NA_RB
    return pl.pallas_call(
        _na_kernel,
        grid=(DEC_BATCH, nrg),
        in_specs=[pl.BlockSpec((NA_RB * GRID_W, 768), lambda b, r: (qblk0 + b * nrg + r, 0)),
                  pl.BlockSpec((DEC_SEQ, 768), lambda b, r: (kvblk0 + b, 0)),
                  pl.BlockSpec((None, None, NA_HEADS, PAST, HEAD_DIM), lambda b, r: (b, l, 0, 0, 0)),
                  pl.BlockSpec((None, None, NA_HEADS, PAST, HEAD_DIM), lambda b, r: (b, l, 0, 0, 0)),
                  pl.BlockSpec((NA_HEADS, WIN_ROWS, GRID_W, WIN_ROWS * GRID_W), lambda b, r: (l, 0, 0, 0))],
        out_specs=pl.BlockSpec((NA_RB * GRID_W, 256), lambda b, r: (b * nrg + r, 0)),
        out_shape=jax.ShapeDtypeStruct((NS_TOK, 256), BF16),
        compiler_params=_cparams(("arbitrary", "arbitrary")),
    )(p, p, cache_k, cache_v, bias)


@functools.lru_cache(None)
def _rope_tables():
    t = np.arange(DEC_SEQ)
    lane = np.arange(128)
    dd = lane % HEAD_DIM
    pos = np.where(dd[None, :] < 32, (t // GRID_W)[:, None], (t % GRID_W)[:, None]).astype(np.float64)
    inv = ROPE_BASE ** (-(dd % 16).astype(np.float64) * 2.0 / 32.0)
    ang = pos * inv[None, :]
    first = (dd % 32) < 16
    cos = np.cos(ang)
    s_up = np.where(first[None, :], -np.sin(ang), 0.0)
    s_dn = np.where(first[None, :], 0.0, np.sin(ang))
    return tuple(np.asarray(a, np.float32) for a in (cos, s_up, s_dn))


def _rope(x, cos, s_up, s_dn):
    return x * cos + pltpu.roll(x, 112, axis=1) * s_up + pltpu.roll(x, 16, axis=1) * s_dn


def _sdiff_kernel(q_ref, k_ref, v_ref, ck_ref, cv_ref, cos_ref, sup_ref, sdn_ref,
                  lq1, lk1, lq2, lk2, g_ref, o_ref, k1_scr, k2_scr, v_scr, *, lam_init, tq):
    qb = pl.program_id(2)

    @pl.when(qb == 0)
    def _():
        kr = _rope(k_ref[...].astype(F32), cos_ref[...], sup_ref[...], sdn_ref[...])
        k1_scr[0:DEC_SEQ, :] = kr[:, :64].astype(BF16)
        k2_scr[0:DEC_SEQ, :] = kr[:, 64:].astype(BF16)
        k1_scr[DEC_SEQ:, :] = ck_ref[0].astype(BF16)
        k2_scr[DEC_SEQ:, :] = ck_ref[1].astype(BF16)
        v_scr[0:DEC_SEQ, :] = v_ref[...].astype(BF16)
        v_scr[DEC_SEQ:, :] = cv_ref[...].astype(BF16)

    lam = _lam(lq1, lk1, lq2, lk2, lam_init)
    half = tq // 2
    for c in range(2):
        rows = pl.ds(pl.multiple_of(qb * tq + c * half, half), half)
        qr = _rope(q_ref[c * half:(c + 1) * half, :].astype(F32), cos_ref[rows, :], sup_ref[rows, :],
                   sdn_ref[rows, :]) * QK_SCALE
        o = _diff_head(qr[:, :64].astype(BF16), qr[:, 64:].astype(BF16), k1_scr[...], k2_scr[...],
                       v_scr[...], lam, g_ref[...], lam_init)
        o_ref[c * half:(c + 1) * half, :] = o.astype(BF16)


def _sdiff_attn(p, cache_k, cache_v, lq1, lk1, lq2, lk2, g, lam_init, l):
    tq = 512
    nq = DEC_SEQ // tq
    cos, s_up, s_dn = (jnp.asarray(a) for a in _rope_tables())
    vec = lambda n: pl.BlockSpec((1, n), lambda b, h, q: (0, 0))
    tab = pl.BlockSpec((DEC_SEQ, 128), lambda b, h, q: (0, 0))
    kvblk0 = NP_TOK // DEC_SEQ
    return pl.pallas_call(
        functools.partial(_sdiff_kernel, lam_init=lam_init, tq=tq),
        grid=(DEC_BATCH, DIFF_HEADS, nq),
        in_specs=[pl.BlockSpec((tq, 128), lambda b, h, q: (NP_TOK // tq + b * nq + q, C_DQ // 128 + h)),
                  pl.BlockSpec((DEC_SEQ, 128), lambda b, h, q: (kvblk0 + b, C_DK // 128 + h)),
                  pl.BlockSpec((DEC_SEQ, 128), lambda b, h, q: (kvblk0 + b, C_DV // 128 + h)),
                  pl.BlockSpec((None, None, None, 2, PAST, HEAD_DIM), lambda b, h, q: (b, l, h, 0, 0, 0)),
                  pl.BlockSpec((None, None, None, PAST, DIFF_V), lambda b, h, q: (b, l, h, 0, 0)),
                  tab, tab, tab, vec(64), vec(64), vec(64), vec(64), vec(128)],
        out_specs=pl.BlockSpec((tq, 128), lambda b, h, q: (b * nq + q, h)),
        out_shape=jax.ShapeDtypeStruct((NS_TOK, DIFF_HEADS * DIFF_V), BF16),
        scratch_shapes=[pltpu.VMEM((DEC_SEQ + PAST, HEAD_DIM), BF16),
                        pltpu.VMEM((DEC_SEQ + PAST, HEAD_DIM), BF16),
                        pltpu.VMEM((DEC_SEQ + PAST, DIFF_V), BF16)],
        compiler_params=_cparams(("arbitrary", "arbitrary", "arbitrary")),
    )(p, p, p, cache_k, cache_v, cos, s_up, s_dn, lq1, lk1, lq2, lk2, g)


@functools.lru_cache(None)
def _dft_consts(L):
    n = 2 * L
    k = np.arange(L)
    ang = 2.0 * np.pi * ((k[:, None] * k[None, :]) % n) / n
    alt = (-1.0) ** k
    fa = np.cos(ang)
    fb = -np.sin(ang)
    fb[0, :] = alt
    wgt = np.full((L,), 2.0 / n)
    wgt[0] = 1.0 / n
    ga = fa * wgt[:, None]
    gb = fb * wgt[:, None]
    gb[0, :] = alt / n
    f = np.concatenate([fa, fb], axis=0)
    g = np.concatenate([ga.T, gb.T], axis=1)
    return np.asarray(f, dtype=BF16), np.asarray(g, dtype=BF16)


@functools.lru_cache(None)
def _filter_consts(L):
    f32 = np.float32
    t = np.linspace(0.0, 1.0, L, dtype=f32)[:, None]
    pos = np.arange(L, dtype=f32)[:, None]
    bands = np.linspace(1e-4, HY_BANDS - 1, HY_BANDS, dtype=f32)[None, :]
    ang = f32(2.0 * math.pi) * bands * pos / f32(L)
    z = np.zeros((L, 128), f32)
    z[:, 0:1] = t
    z[:, 1:1 + HY_BANDS] = np.cos(ang)
    z[:, 1 + HY_BANDS:HY_EMB] = -np.sin(ang)
    min_decay = math.log(1e-2) / 1.5
    max_decay = math.log(1e-2) / 0.3
    deltas = np.abs(np.linspace(min_decay, max_decay, HY_W, dtype=f32))
    decay = np.exp(-t * deltas[None, :]).astype(f32)
    return z, decay


def _spectra_kernel(z_ref, w1_ref, b1_ref, w2_ref, b2_ref, fr_ref, w3_ref, dec_ref, fa_ref, fb_ref,
                    sa_ref, sb_ref, filt_scr, *, L, kc):
    j = pl.program_id(0)

    @pl.when(j == 0)
    def _():
        fr = fr_ref[...]
        hdn = jnp.sin(fr * (_dot3(z_ref[...], w1_ref[...]) + b1_ref[...]))
        hdn = jnp.sin(fr * (_dot3(hdn, w2_ref[...]) + b2_ref[...]))
        dec = dec_ref[...]
        not_first = lax.broadcasted_iota(jnp.int32, (L, HY_W), 0) > 0
        for o in range(2):
            hf = _dot3(hdn, w3_ref[:, 512 * o:512 * o + 256]) * dec
            hb = jnp.where(not_first, _dot3(hdn, w3_ref[:, 512 * o + 256:512 * o + 512]) * dec, 0.0)
            nrm = (jnp.sum(jnp.abs(hf), axis=0, keepdims=True)
                   + jnp.sum(jnp.abs(hb), axis=0, keepdims=True))
            filt_scr[:, 512 * o:512 * o + 256] = (hf / nrm).astype(BF16)
            filt_scr[:, 512 * o + 256:512 * o + 512] = (hb / nrm).astype(BF16)

    ta = _dot(fa_ref[...], filt_scr[...])
    tb = _dot(fb_ref[...], filt_scr[...])
    first = (lax.broadcasted_iota(jnp.int32, (kc, HY_W), 0) + j * kc) == 0
    for o in range(2):
        af, ab = ta[:, 512 * o:512 * o + 256], ta[:, 512 * o + 256:512 * o + 512]
        bf, bb = tb[:, 512 * o:512 * o + 256], tb[:, 512 * o + 256:512 * o + 512]
        sa_ref[:, 256 * o:256 * o + 256] = af + ab
        sb_ref[:, 256 * o:256 * o + 256] = jnp.where(first, bf + bb, bf - bb)


def _hy_spectra(L, kc, w1p, b1, w2, b2, fr, w3):
    z, decay = _filter_consts(L)
    f = jnp.asarray(_dft_consts(L)[0])
    nj = L // kc
    full = lambda shape: pl.BlockSpec(shape, lambda j: tuple(0 for _ in shape))
    return pl.pallas_call(
        functools.partial(_spectra_kernel, L=L, kc=kc),
        grid=(nj,),
        in_specs=[full((L, 128)), full((128, HY_HID)), full((1, HY_HID)), full((HY_HID, HY_HID)),
                  full((1, HY_HID)), full((1, HY_HID)), full((HY_HID, 4 * HY_W)), full((L, HY_W)),
                  pl.BlockSpec((kc, L), lambda j: (j, 0)),
                  pl.BlockSpec((kc, L), lambda j: (j + nj, 0))],
        out_specs=[pl.BlockSpec((kc, 2 * HY_W), lambda j: (j, 0)),
                   pl.BlockSpec((kc, 2 * HY_W), lambda j: (j, 0))],
        out_shape=[jax.ShapeDtypeStruct((L, 2 * HY_W), F32)] * 2,
        scratch_shapes=[pltpu.VMEM((L, 4 * HY_W), BF16)],
        compiler_params=_cparams(("arbitrary",)),
    )(jnp.asarray(z), w1p, b1, w2, b2, fr, w3, jnp.asarray(decay), f, f)


def _hyconv_kernel(*refs, L, nb, kc, nj, row_blk0, staged):
    n_in = 1 if staged else nb
    (cw_ref, cb_ref, d_ref, fa_ref, fb_ref, ga_ref, gb_ref, sa_ref, sb_ref, o_ref,
     xin_scr, x_scr, g1_scr, g2_scr, y_scr) = refs[n_in:n_in + 15]
    bg = pl.program_id(0)
    o = pl.program_id(1)
    j = pl.program_id(2)

    @pl.when((o == 0) & (j == 0))
    def _():
        row = lax.broadcasted_iota(jnp.int32, (L, HY_W), 0)
        pieces = [(i, part) for i in range(nb) for part in range(3)]
        if staged:
            p_hbm, (stage, sem) = refs[0], refs[n_in + 15:]

            def piece_copy(k):
                i, part = pieces[k]
                row0 = pl.multiple_of((row_blk0 + bg * nb + i) * L, L)
                return pltpu.make_async_copy(
                    p_hbm.at[pl.ds(row0, L), C_HY + HY_W * part:C_HY + HY_W * (part + 1)],
                    stage.at[k % 2], sem.at[k % 2])

            piece_copy(0).start()
            piece_copy(1).start()
        for k, (i, part) in enumerate(pieces):
            cols = slice(HY_W * i, HY_W * (i + 1))
            dst = (x_scr, g1_scr, g2_scr)[part]
            pc = slice(HY_W * part, HY_W * (part + 1))
            if staged:
                piece_copy(k).wait()
                u = stage[k % 2].astype(F32)
            else:
                u = refs[i][:, pc].astype(F32)
            up = jnp.where(row == 0, 0.0, pltpu.roll(u, 1, axis=0))
            un = jnp.where(row == L - 1, 0.0, pltpu.roll(u, L - 1, axis=0))
            dst[:, cols] = (up * cw_ref[0:1, pc] + u * cw_ref[1:2, pc] + un * cw_ref[2:3, pc]
                            + cb_ref[:, pc])
            if part == 0:
                xin_scr[:, cols] = x_scr[:, cols].astype(BF16)
            if staged and k + 2 < len(pieces):
                piece_copy(k + 2).start()

    @pl.when(j == 0)
    def _():
        y_scr[...] = jnp.zeros_like(y_scr)

    xa = _dot(fa_ref[...], xin_scr[...])
    xb = _dot(fb_ref[...], xin_scr[...])
    sa = jnp.concatenate([sa_ref[...]] * nb, axis=1)
    sb = jnp.concatenate([sb_ref[...]] * nb, axis=1)
    first = (lax.broadcasted_iota(jnp.int32, (kc, nb * HY_W), 0) + j * kc) == 0
    ya = jnp.where(first, xa * sa, xa * sa - xb * sb)
    yb = jnp.where(first, xb * sb, xa * sb + xb * sa)
    y_scr[...] += _dot(ga_ref[...], ya.astype(BF16)) + _dot(gb_ref[...], yb.astype(BF16))

    @pl.when(j == nj - 1)
    def _():
        dvec = jnp.concatenate([d_ref[pl.ds(o, 1), :]] * nb, axis=1)
        y = y_scr[...] + x_scr[...] * dvec

        @pl.when(o == 0)
        def _():
            zz = g1_scr[...] * y
            x_scr[...] = zz
            xin_scr[...] = zz.astype(BF16)

        @pl.when(o == 1)
        def _():
            res = g2_scr[...] * y
            for i in range(nb):
                o_ref[L * i:L * (i + 1), :] = res[:, HY_W * i:HY_W * (i + 1)].astype(BF16)


def _hy_conv(p, L, nb, kc, row_blk0, n_seq, conv_w, conv_b, d, sa, sb):
    f, g = (jnp.asarray(a) for a in _dft_consts(L))
    nj = L // kc
    staged = nb * L > DEC_SEQ
    if staged:
        u_specs = [pl.BlockSpec(memory_space=pl.ANY)]
        u_scratch = [pltpu.VMEM((2, L, HY_W), p.dtype), pltpu.SemaphoreType.DMA((2,))]
    else:
        u_specs = [pl.BlockSpec((L, 3 * HY_W), lambda bg, o, j, i=i: (row_blk0 + bg * nb + i, C_HY // 768))
                   for i in range(nb)]
        u_scratch = []
    small = lambda shape: pl.BlockSpec(shape, lambda bg, o, j: (0, 0))
    return pl.pallas_call(
        functools.partial(_hyconv_kernel, L=L, nb=nb, kc=kc, nj=nj, row_blk0=row_blk0, staged=staged),
        grid=(n_seq // nb, 2, nj),
        in_specs=u_specs + [
            small((3, 3 * HY_W)), small((1, 3 * HY_W)), small((2, HY_W)),
            pl.BlockSpec((kc, L), lambda bg, o, j: (j, 0)),
            pl.BlockSpec((kc, L), lambda bg, o, j: (j + nj, 0)),
            pl.BlockSpec((L, kc), lambda bg, o, j: (0, j)),
            pl.BlockSpec((L, kc), lambda bg, o, j: (0, j + nj)),
            pl.BlockSpec((kc, HY_W), lambda bg, o, j: (j, o)),
            pl.BlockSpec((kc, HY_W), lambda bg, o, j: (j, o))],
        out_specs=pl.BlockSpec((nb * L, HY_W), lambda bg, o, j: (bg, 0)),
        out_shape=jax.ShapeDtypeStruct((n_seq * L, HY_W), BF16),
        scratch_shapes=[pltpu.VMEM((L, nb * HY_W), BF16)] + [pltpu.VMEM((L, nb * HY_W), F32)] * 4
        + u_scratch,
        compiler_params=_cparams(("arbitrary", "arbitrary", "arbitrary")),
    )(*([p] * len(u_specs)), conv_w, conv_b, d, f, f, g, g, sa, sb)


def _outproj_kernel(xp_ref, xs_ref, mixp_ref, hyp_ref, nas_ref, ds_ref, hys_ref, g1_ref, w_ref, o_ref):
    i = pl.program_id(0)
    w = lambda a, b: w_ref[a:b, :].astype(BF16)

    @pl.when(i < NP_TOK // TM)
    def _():
        y = _dot(mixp_ref[...], w(0, 768)) + _dot(hyp_ref[...], w(768, D))
        o_ref[...] = xp_ref[...] + g1_ref[...] * y

    @pl.when(i >= NP_TOK // TM)
    def _():
        y = _dot(nas_ref[...], w(0, 256)) + _dot(ds_ref[...], w(256, 768)) + _dot(hys_ref[...], w(768, D))
        o_ref[...] = xs_ref[...] + g1_ref[...] * y


def _x_specs(xs_block0):
    npt = NP_TOK // TM
    return [pl.BlockSpec((TM, D), lambda i, *_: (jnp.minimum(i, npt - 1), 0)),
            pl.BlockSpec((TM, D), lambda i, *_: (jnp.maximum(i - npt, 0) + xs_block0, 0))]


def _outproj(xp, xs, xs_block0, mix_p, hy_p, na_s, d_s, hy_s, mod6, w_out, l):
    npt = NP_TOK // TM
    pidx = lambda i: (jnp.minimum(i, npt - 1), 0)
    sidx = lambda i: (jnp.maximum(i - npt, 0), 0)
    return pl.pallas_call(
        _outproj_kernel,
        grid=(N_TOK // TM,),
        in_specs=_x_specs(xs_block0) + [
                  pl.BlockSpec((TM, 768), pidx),
                  pl.BlockSpec((TM, HY_W), pidx),
                  pl.BlockSpec((TM, 256), sidx),
                  pl.BlockSpec((TM, 512), sidx),
                  pl.BlockSpec((TM, HY_W), sidx),
                  _mod_spec(2, TM),
                  pl.BlockSpec((None, D, D), lambda i: (l, 0, 0))],
        out_specs=pl.BlockSpec((TM, D), lambda i: (i, 0)),
        out_shape=jax.ShapeDtypeStruct((N_TOK, D), F32),
        compiler_params=_cparams(("arbitrary",)),
    )(xp, xs, mix_p, hy_p, na_s, d_s, hy_s, mod6, w_out)


def _ffn_kernel(x_ref, g_ref, sc_ref, sh_ref, gate_ref, wg_ref, wu_ref, wd_ref, o_ref, h_scr, *, nj):
    j = pl.program_id(1)

    @pl.when(j == 0)
    def _():
        h_scr[...] = _norm_mod(x_ref[...], g_ref[...], sc_ref[...], sh_ref[...]).astype(BF16)
        o_ref[...] = jnp.zeros_like(o_ref)

    wg, wu, wd = (r[...].astype(BF16) for r in (wg_ref, wu_ref, wd_ref))
    half = h_scr.shape[0] // 2
    for c in range(2):
        rows = slice(c * half, (c + 1) * half)
        h = h_scr[rows, :]
        a = _silu(_dot(h, wg)) * _dot(h, wu)
        o_ref[rows, :] += _dot(a.astype(BF16), wd)

    @pl.when(j == nj - 1)
    def _():
        o_ref[...] = x_ref[...] + gate_ref[...] * o_ref[...]


def _dense_ffn(x, g, mod6, wg, wu, wd, i_ffn):
    tf = 256
    tm = DEC_SEQ
    nj = FFN // tf
    return pl.pallas_call(
        functools.partial(_ffn_kernel, nj=nj),
        grid=(N_TOK // tm, nj),
        in_specs=[pl.BlockSpec((tm, D), lambda i, j: (i, 0)),
                  pl.BlockSpec((1, D), lambda i, j: (0, 0)),
                  _mod_spec(4, tm), _mod_spec(3, tm), _mod_spec(5, tm),
                  pl.BlockSpec((None, D, tf), lambda i, j: (i_ffn, 0, j)),
                  pl.BlockSpec((None, D, tf), lambda i, j: (i_ffn, 0, j)),
                  pl.BlockSpec((None, tf, D), lambda i, j: (i_ffn, j, 0))],
        out_specs=pl.BlockSpec((tm, D), lambda i, j: (i, 0)),
        out_shape=jax.ShapeDtypeStruct((N_TOK, D), F32),
        scratch_shapes=[pltpu.VMEM((tm, D), BF16)],
        compiler_params=_cparams(("arbitrary", "arbitrary")),
    )(x, g, mod6, mod6, mod6, wg, wu, wd)


def _router_kernel(x_ref, g_ref, sc_ref, sh_ref, wr_ref, h_ref, r_ref):
    h = _norm_mod(x_ref[...], g_ref[...], sc_ref[...], sh_ref[...])
    h_ref[...] = h.astype(BF16)
    lane = lax.broadcasted_iota(jnp.int32, (TM, 128), 1)
    lg = jnp.where(lane < N_EXP, _dot3(h, wr_ref[...]), -jnp.inf)
    m1 = jnp.max(lg, axis=-1, keepdims=True)
    i1 = jnp.min(jnp.where(lg == m1, lane, 128), axis=-1, keepdims=True)
    lg2 = jnp.where(lane == i1, -jnp.inf, lg)
    m2 = jnp.max(lg2, axis=-1, keepdims=True)
    i2 = jnp.min(jnp.where(lg2 == m2, lane, 128), axis=-1, keepdims=True)
    e = jnp.exp(m2 - m1)
    w1 = 1.0 / (1.0 + e)
    w2 = e / (1.0 + e)
    r_ref[...] = jnp.where(lane == 0, i1.astype(F32),
                           jnp.where(lane == 1, i2.astype(F32),
                                     jnp.where(lane == 2, w1, jnp.where(lane == 3, w2, 0.0))))


def _router(x, g, mod6, wr_pad):
    return pl.pallas_call(
        _router_kernel,
        grid=(N_TOK // TM,),
        in_specs=[pl.BlockSpec((TM, D), lambda i: (i, 0)),
                  pl.BlockSpec((1, D), lambda i: (0, 0)),
                  _mod_spec(4, TM), _mod_spec(3, TM),
                  pl.BlockSpec((D, 128), lambda i: (0, 0))],
        out_specs=[pl.BlockSpec((TM, D), lambda i: (i, 0)),
                   pl.BlockSpec((TM, 128), lambda i: (i, 0))],
        out_shape=[jax.ShapeDtypeStruct((N_TOK, D), BF16), jax.ShapeDtypeStruct((N_TOK, 128), F32)],
        compiler_params=_cparams(("arbitrary",)),
    )(x, g, mod6, mod6, wr_pad)


def _dispatch_kernel(blo_ref, bhi_ref, sexp_ref, h_hbm, rp_ref, o_ref, h_scr, sem):
    s = pl.program_id(0)

    @pl.when(s == 0)
    def _():
        cp = pltpu.make_async_copy(h_hbm, h_scr, sem)
        cp.start()
        cp.wait()

    e = sexp_ref[s]
    rows = s * MOE_TS + lax.broadcasted_iota(jnp.int32, (MOE_TS, MOE_DTB), 0)

    def sel(b):
        return (rows == rp_ref[e, pl.ds(b, 1), :]).astype(BF16)

    is_empty = blo_ref[s] > bhi_ref[s]

    @pl.when(is_empty)
    def _():
        o_ref[...] = jnp.zeros_like(o_ref)

    @pl.when(jnp.logical_not(is_empty))
    def _():
        b0 = jnp.minimum(blo_ref[s], N_TOK // MOE_DTB - MOE_DWIN)
        hwin = h_scr[pl.ds(pl.multiple_of(b0 * MOE_DTB, MOE_DTB), MOE_DWIN * MOE_DTB), :]
        acc = _dot(jnp.concatenate([sel(b0 + k) for k in range(MOE_DWIN)], axis=1), hwin)

        def body(b, acc):
            hb = h_scr[pl.ds(pl.multiple_of(b * MOE_DTB, MOE_DTB), MOE_DTB), :]
            return acc + _dot(sel(b), hb)

        acc = lax.fori_loop(b0 + MOE_DWIN, bhi_ref[s] + 1, body, acc)
        o_ref[...] = acc.astype(BF16)


def _dispatch(blo, bhi, sexp, h, rp_t):
    return pl.pallas_call(
        _dispatch_kernel,
        grid_spec=pltpu.PrefetchScalarGridSpec(
            num_scalar_prefetch=3,
            grid=(MOE_ROWS // MOE_TS,),
            in_specs=[pl.BlockSpec(memory_space=pl.ANY),
                      pl.BlockSpec((N_EXP, N_TOK // MOE_DTB, MOE_DTB), lambda s, *_: (0, 0, 0))],
            out_specs=pl.BlockSpec((MOE_TS, D), lambda s, *_: (s, 0)),
            scratch_shapes=[pltpu.VMEM((N_TOK, D), BF16), pltpu.SemaphoreType.DMA(())]),
        out_shape=jax.ShapeDtypeStruct((MOE_ROWS, D), BF16),
        compiler_params=_cparams(("arbitrary",)),
    )(blo, bhi, sexp, h, rp_t)


def _experts_kernel(te_ref, used_ref, rows_ref, xs_ref, wg_ref, wu_ref, wd_ref, o_ref, acc_scr, *, nj):
    i = pl.program_id(0)
    j = pl.program_id(1)
    nrows = rows_ref[i]

    @pl.when(j == 0)
    def _():
        acc_scr[...] = jnp.zeros_like(acc_scr)

    for sz in range(MOE_TQ, MOE_TM + 1, MOE_TQ):
        @pl.when(nrows == sz)
        def _():
            xb = xs_ref[0:sz, :]
            a = _silu(_dot(xb, wg_ref[...].astype(BF16))) * _dot(xb, wu_ref[...].astype(BF16))
            acc_scr[0:sz, :] += _dot(a.astype(BF16), wd_ref[...].astype(BF16))

    @pl.when(j == nj - 1)
    def _():
        o_ref[...] = acc_scr[...].astype(BF16)


def _experts(tile_expert, used, tile_rows, xs, wg, wu, wd, i_moe):
    nj = EXP_DIM // MOE_TF

    def tile(i, used):
        return jnp.minimum(i, used[0] - 1)

    def chunk(i, j, used):
        return jnp.where(i < used[0], j, nj - 1)

    return pl.pallas_call(
        functools.partial(_experts_kernel, nj=nj),
        grid_spec=pltpu.PrefetchScalarGridSpec(
            num_scalar_prefetch=3,
            grid=(MOE_YROWS // MOE_TM, nj),
            in_specs=[pl.BlockSpec((MOE_TM, D), lambda i, j, te, used, tr: (tile(i, used), 0)),
                      pl.BlockSpec((None, None, D, MOE_TF),
                                   lambda i, j, te, used, tr: (i_moe, te[tile(i, used)], 0, chunk(i, j, used))),
                      pl.BlockSpec((None, None, D, MOE_TF),
                                   lambda i, j, te, used, tr: (i_moe, te[tile(i, used)], 0, chunk(i, j, used))),
                      pl.BlockSpec((None, None, MOE_TF, D),
                                   lambda i, j, te, used, tr: (i_moe, te[tile(i, used)], chunk(i, j, used), 0))],
            out_specs=pl.BlockSpec((MOE_TM, D), lambda i, j, te, used, tr: (i, 0)),
            scratch_shapes=[pltpu.VMEM((MOE_TM, D), F32)]),
        out_shape=jax.ShapeDtypeStruct((MOE_YROWS, D), BF16),
        compiler_params=_cparams(("arbitrary", "arbitrary")),
    )(tile_expert, used, tile_rows, xs, wg, wu, wd)


def _combine_kernel(ws_ref, kind_ref, x_ref, rp_ref, comb_ref, gate_ref, fg_ref, ys_hbm, op_ref, os_ref,
                    win_scr, y_scr, sem):
    b = pl.program_id(0)
    nb = pl.num_programs(0)
    slot = b % 2

    def win_copy(blk, sl, e):
        start = pl.multiple_of(ws_ref[blk * N_EXP + e], MOE_ALIGN)
        return pltpu.make_async_copy(ys_hbm.at[pl.ds(start, MOE_WIN)], win_scr.at[sl, e], sem.at[sl, e])

    @pl.when(b == 0)
    def _():
        for e in range(N_EXP):
            win_copy(0, 0, e).start()

    @pl.when(b + 1 < nb)
    def _():
        for e in range(N_EXP):
            win_copy(b + 1, 1 - slot, e).start()

    col = lax.broadcasted_iota(jnp.int32, (MOE_TB, MOE_WIN), 1)
    y_scr[...] = jnp.zeros_like(y_scr)
    for e in range(N_EXP):
        win_copy(b, slot, e).wait()
        rel = rp_ref[:, e:e + 1] - ws_ref[b * N_EXP + e]
        cw = comb_ref[:, e:e + 1]
        kind = kind_ref[b * N_EXP + e]

        for k, wn in enumerate(MOE_WINS):
            @pl.when(kind == k)
            def _():
                sel = (rel == col[:, :wn]).astype(BF16) if wn == MOE_WIN else (
                    rel == lax.broadcasted_iota(jnp.int32, (MOE_TB, wn), 1)).astype(BF16)
                y_scr[...] += cw * _dot(sel, win_scr[slot, e, 0:wn, :])

    x = x_ref[...] + gate_ref[...] * y_scr[...]
    x = (x * lax.rsqrt(jnp.mean(x * x, axis=-1, keepdims=True) + EPS)) * fg_ref[...]

    @pl.when(b < NP_TOK // MOE_TB)
    def _():
        op_ref[...] = x

    @pl.when(b >= NP_TOK // MOE_TB)
    def _():
        os_ref[...] = x


def _combine(ws, kind, x, rp8, comb, mod6, final_g, ys):
    npb = NP_TOK // MOE_TB
    return pl.pallas_call(
        _combine_kernel,
        grid_spec=pltpu.PrefetchScalarGridSpec(
            num_scalar_prefetch=2,
            grid=(N_TOK // MOE_TB,),
            in_specs=[pl.BlockSpec((MOE_TB, D), lambda b, *_: (b, 0)),
                      pl.BlockSpec((MOE_TB, N_EXP), lambda b, *_: (b, 0)),
                      pl.BlockSpec((MOE_TB, N_EXP), lambda b, *_: (b, 0)),
                      _mod_spec(5, MOE_TB),
                      pl.BlockSpec((1, D), lambda b, *_: (0, 0)),
                      pl.BlockSpec(memory_space=pl.ANY)],
            out_specs=[pl.BlockSpec((MOE_TB, D), lambda b, *_: (jnp.minimum(b, npb - 1), 0)),
                       pl.BlockSpec((MOE_TB, D), lambda b, *_: (jnp.maximum(b - npb, 0), 0))],
            scratch_shapes=[pltpu.VMEM((2, N_EXP, MOE_WIN, D), BF16),
                            pltpu.VMEM((MOE_TB, D), F32),
                            pltpu.SemaphoreType.DMA((2, N_EXP))]),
        out_shape=[jax.ShapeDtypeStruct((NP_TOK, D), F32), jax.ShapeDtypeStruct((NS_TOK, D), F32)],
        compiler_params=_cparams(("arbitrary",)),
    )(ws, kind, x, rp8, comb, mod6, final_g, ys)


def _moe(x, g, mod6, router, wg, wu, wd, i_moe, final_g):
    wr_pad = jnp.pad(router, ((0, 0), (0, 128 - N_EXP)))
    h, r = _router(x, g, mod6, wr_pad)

    i32 = jnp.int32
    i12 = r[:, 0:2].astype(i32)
    earange = jnp.arange(N_EXP, dtype=i32)
    hit1 = i12[:, 0:1] == earange[None, :]
    hit2 = i12[:, 1:2] == earange[None, :]
    comb = jnp.where(hit1, r[:, 2:3], 0.0) + jnp.where(hit2, r[:, 3:4], 0.0)
    mask = (hit1 | hit2).astype(i32)
    csum = jnp.cumsum(mask, axis=0)
    counts = csum[-1]
    padded = ((counts + MOE_TM - 1) // MOE_TM) * MOE_TM
    ends = jnp.cumsum(padded)
    starts = ends - padded
    rp8 = jnp.where(mask > 0, starts[None, :] + csum - 1, -1).astype(i32)
    n_tiles = MOE_ROWS // MOE_TM
    tile_expert = jnp.minimum(
        jnp.searchsorted(ends, jnp.arange(n_tiles, dtype=i32) * MOE_TM, side="right"),
        N_EXP - 1).astype(i32)
    used = (ends[-1:] // MOE_TM).astype(i32)
    tile_row0 = jnp.arange(MOE_YROWS // MOE_TM, dtype=i32) * MOE_TM
    te_all = jnp.minimum(jnp.searchsorted(ends, tile_row0, side="right"), N_EXP - 1)
    valid = jnp.clip(counts[te_all] - (tile_row0 - starts[te_all]), 0, MOE_TM)
    valid = jnp.where(tile_row0 < ends[-1], valid, 0)
    tile_rows = (((valid + MOE_TQ - 1) // MOE_TQ) * MOE_TQ).astype(i32)

    sub_row0 = jnp.arange(MOE_ROWS // MOE_TS, dtype=i32) * MOE_TS
    sexp = tile_expert[sub_row0 // MOE_TM]
    qlo = sub_row0 - starts[sexp]
    qend = jnp.minimum(qlo + MOE_TS, counts[sexp])
    cbe = csum[MOE_DTB - 1::MOE_DTB, :].T[sexp]
    blo = jnp.sum((cbe <= qlo[:, None]).astype(i32), axis=1)
    bhi = jnp.minimum(jnp.sum((cbe < qend[:, None]).astype(i32), axis=1), N_TOK // MOE_DTB - 1)
    empty = qend <= qlo
    blo = jnp.where(empty, 1, blo).astype(i32)
    bhi = jnp.where(empty, 0, bhi).astype(i32)

    cb = csum[MOE_TB - 1::MOE_TB, :]
    cprev = jnp.concatenate([jnp.zeros((1, N_EXP), i32), cb[:-1]], axis=0)
    ws = (((starts[None, :] + cprev) // MOE_ALIGN) * MOE_ALIGN).reshape(-1).astype(i32)
    n_be = cb - cprev
    kind = sum((n_be > wn - MOE_ALIGN).astype(i32) for wn in MOE_WINS[:-1])
    kind = jnp.where(n_be == 0, len(MOE_WINS), kind).reshape(-1).astype(i32)

    rp_t = rp8.T.reshape(N_EXP, N_TOK // MOE_DTB, MOE_DTB)
    xs = _dispatch(blo, bhi, sexp, h, rp_t)
    ys = _experts(tile_expert, used, tile_rows, xs, wg, wu, wd, i_moe)
    return _combine(ws, kind, x, rp8, comb, mod6, final_g, ys)


assert DEPTH == 2

def kernel(x_prompt, x_sample, cache_na_k, cache_na_v, cache_diff_k, cache_diff_v, c, c_ctx, w_in, w_out, ada_w, ada_b, norm_mix_g, norm_ffn_g, na_rpb, diff_lq1, diff_lk1, diff_lq2, diff_lk2, diff_subln_g, hy_conv_w, hy_conv_b, hy_d, hy_f_w1, hy_f_b1, hy_f_w2, hy_f_b2, hy_f_freq, hy_f_w3, ffn_w_gate, ffn_w_up, ffn_w_down, moe_router, moe_w_gate, moe_w_up, moe_w_down, final_norm_g):
    xparts = (x_prompt.reshape(NP_TOK, D), x_sample.reshape(NS_TOK, D), 0)
    cond8 = jnp.concatenate([c_ctx[None, :], c, jnp.zeros((5, D), F32)], axis=0)
    mods = _modulation(cond8, ada_w, ada_b)
    final_g = final_norm_g.reshape(1, D)
    bias = _na_bias(na_rpb)

    leaves = [jnp.zeros((BATCH, DEPTH, NA_HEADS, SEQ, HEAD_DIM), F32),
              jnp.zeros((BATCH, DEPTH, NA_HEADS, SEQ, HEAD_DIM), F32),
              jnp.zeros((BATCH, DEPTH, DIFF_HEADS, 2, SEQ, HEAD_DIM), F32),
              jnp.zeros((BATCH, DEPTH, DIFF_HEADS, SEQ, DIFF_V), F32)]
    for l in range(DEPTH):
        lam_init = 0.8 - 0.6 * math.exp(-0.3 * l)
        mod6 = mods[l].reshape(8, 6, D).transpose(1, 0, 2).reshape(6, 8, 1, D)
        row = lambda a: a[l].reshape(1, -1)
        lq1, lk1, lq2, lk2, subg = row(diff_lq1), row(diff_lk1), row(diff_lq2), row(diff_lk2), row(diff_subln_g)

        p, *leaves = _inproj(*xparts, row(norm_mix_g), mod6, w_in, l, leaves)

        mix_p = _prompt_attn(p, lq1, lk1, lq2, lk2, subg, lam_init)
        na_s = _na_attn(p, cache_na_k, cache_na_v, bias, l)
        d_s = _sdiff_attn(p, cache_diff_k, cache_diff_v, lq1, lk1, lq2, lk2, subg, lam_init, l)

        w1p = jnp.pad(hy_f_w1[l], ((0, 128 - HY_EMB), (0, 0)))
        fargs = (w1p, row(hy_f_b1), hy_f_w2[l], row(hy_f_b2), row(hy_f_freq), hy_f_w3[l])
        cargs = (hy_conv_w[l], row(hy_conv_b), hy_d[l])
        sa_p, sb_p = _hy_spectra(SEQ, SEQ, *fargs)
        hy_p = _hy_conv(p, SEQ, 4, SEQ, 0, BATCH, *cargs, sa_p, sb_p)
        sa_s, sb_s = _hy_spectra(DEC_SEQ, 512, *fargs)
        hy_s = _hy_conv(p, DEC_SEQ, 2, 512, NP_TOK // DEC_SEQ, DEC_BATCH, *cargs, sa_s, sb_s)

        x = _outproj(*xparts, mix_p, hy_p, na_s, d_s, hy_s, mod6, w_out, l)

        if l == 0:
            x = _dense_ffn(x, row(norm_ffn_g), mod6, ffn_w_gate, ffn_w_up, ffn_w_down, 0)
            xparts = (x, x, NP_TOK // TM)
        else:
            yp, ys = _moe(x, row(norm_ffn_g), mod6, moe_router[0], moe_w_gate, moe_w_up, moe_w_down,
                          0, final_g)

    return (yp.reshape(BATCH, SEQ, D), ys.reshape(DEC_BATCH, DEC_SEQ, D), *leaves)
```

```python
import functools
import math

import numpy as np
import jax
import jax.numpy as jnp
from jax import lax
from jax.experimental import pallas as pl
from jax.experimental.pallas import tpu as pltpu

F32 = jnp.float32
BF16 = jnp.bfloat16

D = 1024
BATCH, SEQ = 32, 256
DEC_BATCH, DEC_SEQ = 2, 2048
DEPTH = 2
PAST = 512
GRID_W = 64
GRID_ROWS = DEC_SEQ // GRID_W
HEAD_DIM = 64
NA_HEADS = 4
DIFF_HEADS = 4
DIFF_V = 128
WIN_ROWS, WIN_COLS = 8, 16
HY_W = 256
HY_EMB = 33
HY_BANDS = 16
HY_HID = 64
PROJ = 3072
FFN = 2816
N_EXP = 8
EXP_DIM = 3584
EPS = 1e-6
ROPE_BASE = 10000.0

NP_TOK = BATCH * SEQ
NS_TOK = DEC_BATCH * DEC_SEQ
N_TOK = NP_TOK + NS_TOK

C_NAQ, C_NAK, C_NAV = 0, 256, 512
C_DQ, C_DK, C_DV = 768, 1280, 1792
C_HY = 2304

TM = 1024
NA_RB = 4
VMEM_LIMIT = 56 * 1024 * 1024

MOE_TM = 1024
MOE_TF = 512
MOE_TQ = 256
MOE_ROWS = 2 * N_TOK + N_EXP * MOE_TM
MOE_TS = 256
MOE_DTB = 512
MOE_DWIN = 3
MOE_TB = 512
MOE_ALIGN = 16
MOE_WIN = MOE_TB + MOE_ALIGN
MOE_WINS = (MOE_TB // 4 + MOE_ALIGN, MOE_TB // 2 + MOE_ALIGN, MOE_WIN)
MOE_YROWS = MOE_ROWS + MOE_TM


def _cparams(sem):
    return pltpu.CompilerParams(dimension_semantics=sem, vmem_limit_bytes=VMEM_LIMIT)


def _dot(a, b):
    return jnp.dot(a, b, preferred_element_type=F32)


def _dot_nt(a, b):
    return lax.dot_general(a, b, (((1,), (1,)), ((), ())), preferred_element_type=F32)


def _split(a):
    hi = a.astype(BF16)
    lo = (a - hi.astype(F32)).astype(BF16)
    return hi, lo


def _dot3(a, b):
    ah, al = _split(a)
    bh, bl = _split(b)
    return _dot(ah, bh) + (_dot(ah, bl) + _dot(al, bh))


def _silu(x):
    return x / (1.0 + jnp.exp(-x))


def _mod_row(i, tm):
    t = i * tm
    return jnp.where(t < NP_TOK, 0, 1 + (t - NP_TOK) // DEC_SEQ)


def _mod_spec(k, tm):
    return pl.BlockSpec((None, None, 1, D), lambda i, *_: (k, _mod_row(i, tm), 0, 0))


def _norm_mod(x, g, sc, sh):
    y = x * lax.rsqrt(jnp.mean(x * x, axis=-1, keepdims=True) + EPS)
    return (y * g) * (1.0 + sc) + sh


def _mod_kernel(c_ref, w_ref, b_ref, o_ref):
    o_ref[...] = _dot3(_silu(c_ref[...]), w_ref[...]) + b_ref[...]


def _modulation(cond8, ada_w, ada_b):
    tn = 1536
    return pl.pallas_call(
        _mod_kernel,
        grid=(DEPTH, 6 * D // tn),
        in_specs=[pl.BlockSpec((8, D), lambda l, j: (0, 0)),
                  pl.BlockSpec((None, D, tn), lambda l, j: (l, 0, j)),
                  pl.BlockSpec((None, 1, tn), lambda l, j: (l, 0, j))],
        out_specs=pl.BlockSpec((None, 8, tn), lambda l, j: (l, 0, j)),
        out_shape=jax.ShapeDtypeStruct((DEPTH, 8, 6 * D), F32),
        compiler_params=_cparams(("arbitrary", "arbitrary")),
    )(cond8, ada_w, ada_b.reshape(DEPTH, 1, 6 * D))


def _inproj_kernel(xp_ref, xs_ref, g_ref, sc_ref, sh_ref, w_hbm, nak_in, nav_in, dk_in, dv_in,
                   o_ref, nak_ref, nav_ref, dk_ref, dv_ref, h_scr, w_scr, stage, acc_scr, sem, *, l, tn):
    del nak_in, nav_in, dk_in, dv_in
    i = pl.program_id(0)
    j = pl.program_id(1)

    @pl.when((i == 0) & (j == 0))
    def _():
        for c in range(PROJ // tn):
            cp = pltpu.make_async_copy(w_hbm.at[l, :, c * tn:(c + 1) * tn], stage, sem)
            cp.start()
            cp.wait()
            w_scr[c] = stage[...].astype(BF16)

    is_ctx = i < NP_TOK // TM

    @pl.when((j == 0) & is_ctx)
    def _():
        h_scr[...] = _norm_mod(xp_ref[...], g_ref[...], sc_ref[...], sh_ref[...]).astype(BF16)

    @pl.when((j == 0) & jnp.logical_not(is_ctx))
    def _():
        h_scr[...] = _norm_mod(xs_ref[...], g_ref[...], sc_ref[...], sh_ref[...]).astype(BF16)

    acc_scr[...] = _dot(h_scr[...], w_scr[j])
    o_ref[...] = acc_scr[...].astype(BF16)


    def rows(bb):
        return slice(bb * SEQ, (bb + 1) * SEQ)

    def copy_heads(dst_ref, c0, width, heads, sub=None):
        for bb in range(TM // SEQ):
            for h in heads:
                if sub is None:
                    dst_ref[bb, h] = acc_scr[rows(bb), c0(h):c0(h) + width]
                else:
                    for s in range(2):
                        dst_ref[bb, h, s] = acc_scr[rows(bb), c0(h) + s * width:c0(h) + (s + 1) * width]

    @pl.when(is_ctx & (j == 0))
    def _():
        copy_heads(nak_ref, lambda h: C_NAK + HEAD_DIM * h, HEAD_DIM, range(NA_HEADS))
        copy_heads(nav_ref, lambda h: C_NAV + HEAD_DIM * h, HEAD_DIM, range(NA_HEADS))

    @pl.when(is_ctx & (j == 1))
    def _():
        copy_heads(dk_ref, lambda h: C_DK - tn + 2 * HEAD_DIM * h, HEAD_DIM, range(0, 2), sub=True)

    @pl.when(is_ctx & (j == 2))
    def _():
        copy_heads(dk_ref, lambda h: C_DK - 2 * tn + 2 * HEAD_DIM * h, HEAD_DIM, range(2, 4), sub=True)
        copy_heads(dv_ref, lambda h: C_DV - 2 * tn + DIFF_V * h, DIFF_V, range(DIFF_HEADS))


def _inproj(xp, xs, xs_block0, g, mod6, w_in, l, leaves):
    tn = 768
    nb = TM // SEQ
    ctx = lambda i: jnp.minimum(i, NP_TOK // TM - 1)
    leaf_specs = [pl.BlockSpec((nb, None, NA_HEADS, SEQ, HEAD_DIM), lambda i, j: (ctx(i), l, 0, 0, 0)),
                  pl.BlockSpec((nb, None, NA_HEADS, SEQ, HEAD_DIM), lambda i, j: (ctx(i), l, 0, 0, 0)),
                  pl.BlockSpec((nb, None, DIFF_HEADS, 2, SEQ, HEAD_DIM), lambda i, j: (ctx(i), l, 0, 0, 0, 0)),
                  pl.BlockSpec((nb, None, DIFF_HEADS, SEQ, DIFF_V), lambda i, j: (ctx(i), l, 0, 0, 0))]
    return pl.pallas_call(
        functools.partial(_inproj_kernel, l=l, tn=tn),
        grid=(N_TOK // TM, PROJ // tn),
        in_specs=_x_specs(xs_block0) + [
                  pl.BlockSpec((1, D), lambda i, j: (0, 0)),
                  _mod_spec(1, TM), _mod_spec(0, TM),
                  pl.BlockSpec(memory_space=pl.ANY)] + [pl.BlockSpec(memory_space=pl.ANY)] * 4,
        out_specs=[pl.BlockSpec((TM, tn), lambda i, j: (i, j))] + leaf_specs,
        out_shape=[jax.ShapeDtypeStruct((N_TOK, PROJ), BF16)]
        + [jax.ShapeDtypeStruct(a.shape, a.dtype) for a in leaves],
        input_output_aliases={6: 1, 7: 2, 8: 3, 9: 4},
        scratch_shapes=[pltpu.VMEM((TM, D), BF16), pltpu.VMEM((PROJ // tn, D, tn), BF16),
                        pltpu.VMEM((D, tn), F32), pltpu.VMEM((TM, tn), F32), pltpu.SemaphoreType.DMA(())],
        compiler_params=_cparams(("arbitrary", "arbitrary")),
    )(xp, xs, g, mod6, mod6, w_in, *leaves)


def _lam(lq1, lk1, lq2, lk2, lam_init):
    return (jnp.exp(jnp.sum(lq1[...] * lk1[...], axis=-1, keepdims=True))
            - jnp.exp(jnp.sum(lq2[...] * lk2[...], axis=-1, keepdims=True)) + lam_init)


def _softmax_parts(s, base2=False):
    m = jnp.max(s, axis=-1, keepdims=True)
    e = jnp.exp2(s - m) if base2 else jnp.exp(s - m)
    return e, jnp.sum(e, axis=-1, keepdims=True)


QK_SCALE = HEAD_DIM ** -0.5
QK_SCALE_LOG2 = QK_SCALE * math.log2(math.e)


def _diff_head(q1, q2, k1, k2, v, lam, g, lam_init):
    e1, l1 = _softmax_parts(_dot_nt(q1, k1), base2=True)
    e2, l2 = _softmax_parts(_dot_nt(q2, k2), base2=True)
    o = _dot(e1.astype(BF16), v) * (1.0 / l1) - _dot(e2.astype(BF16), v) * (lam * (1.0 / l2))
    o = o * lax.rsqrt(jnp.mean(o * o, axis=-1, keepdims=True) + EPS)
    return (o * g) * (1.0 - lam_init)


def _prompt_attn_kernel(pa_ref, pb_ref, pc_ref, lq1, lk1, lq2, lk2, g_ref, o_ref, *, lam_init):
    lam = _lam(lq1, lk1, lq2, lk2, lam_init)
    g = g_ref[...]

    def col(c0, w, scale=None):
        ref = (pa_ref, pb_ref, pc_ref)[c0 // 768]
        o = c0 % 768
        a = ref[:, o:o + w]
        return (a if scale is None else a * scale).astype(BF16)

    for h in range(NA_HEADS):
        q = col(C_NAQ + 64 * h, 64, QK_SCALE)
        k = col(C_NAK + 64 * h, 64)
        v = col(C_NAV + 64 * h, 64)
        e, l = _softmax_parts(_dot_nt(q, k))
        o = _dot(e.astype(BF16), v) * (1.0 / l)
        o_ref[:, 64 * h:64 * h + 64] = o.astype(BF16)
    for h in range(DIFF_HEADS):
        q1 = col(C_DQ + 128 * h, 64, QK_SCALE_LOG2)
        q2 = col(C_DQ + 128 * h + 64, 64, QK_SCALE_LOG2)
        k1 = col(C_DK + 128 * h, 64)
        k2 = col(C_DK + 128 * h + 64, 64)
        v = col(C_DV + 128 * h, 128)
        o = _diff_head(q1, q2, k1, k2, v, lam, g, lam_init)
        o_ref[:, 256 + 128 * h:384 + 128 * h] = o.astype(BF16)


def _prompt_attn(p, lq1, lk1, lq2, lk2, g, lam_init):
    vec = lambda n: pl.BlockSpec((1, n), lambda b: (0, 0))
    return pl.pallas_call(
        functools.partial(_prompt_attn_kernel, lam_init=lam_init),
        grid=(BATCH,),
        in_specs=[pl.BlockSpec((SEQ, 768), lambda b: (b, 0)),
                  pl.BlockSpec((SEQ, 768), lambda b: (b, 1)),
                  pl.BlockSpec((SEQ, 768), lambda b: (b, 2)),
                  vec(64), vec(64), vec(64), vec(64), vec(128)],
        out_specs=pl.BlockSpec((SEQ, 768), lambda b: (b, 0)),
        out_shape=jax.ShapeDtypeStruct((NP_TOK, 768), BF16),
        compiler_params=_cparams(("arbitrary",)),
    )(p, p, p, lq1, lk1, lq2, lk2, g)


def _bias_kernel(rpb_ref, o_ref):
    lh = pl.program_id(0)
    qc = lax.broadcasted_iota(jnp.int32, (GRID_W, GRID_W), 0)
    kc = lax.broadcasted_iota(jnp.int32, (GRID_W, GRID_W), 1)
    delta = jnp.clip(kc - qc + (WIN_COLS - 1), 0, 2 * WIN_COLS - 2)
    qs = jnp.clip(qc - WIN_COLS // 2, 0, GRID_W - WIN_COLS)
    in_win = (kc >= qs) & (kc < qs + WIN_COLS)
    for dr in range(2 * WIN_ROWS - 1):
        base = (lh * (2 * WIN_ROWS - 1) + dr) * (2 * WIN_COLS - 1)
        acc = jnp.zeros((GRID_W, GRID_W), F32)
        for d in range(2 * WIN_COLS - 1):
            acc = jnp.where(delta == d, rpb_ref[base + d], acc)
        piece = jnp.where(in_win, acc, -jnp.inf)
        for case in range(WIN_ROWS):
            i = dr + case - (WIN_ROWS - 1)
            if 0 <= i < WIN_ROWS:
                o_ref[case, :, i * GRID_W:(i + 1) * GRID_W] = piece


def _na_bias(rpb):
    return pl.pallas_call(
        _bias_kernel,
        grid=(DEPTH * NA_HEADS,),
        in_specs=[pl.BlockSpec(memory_space=pltpu.SMEM)],
        out_specs=pl.BlockSpec((None, WIN_ROWS, GRID_W, WIN_ROWS * GRID_W), lambda lh: (lh, 0, 0, 0)),
        out_shape=jax.ShapeDtypeStruct((DEPTH * NA_HEADS, WIN_ROWS, GRID_W, WIN_ROWS * GRID_W), F32),
        compiler_params=_cparams(("arbitrary",)),
    )(rpb.reshape(-1))


def _na_kernel(q_ref, kv_ref, kc_ref, vc_ref, bias_ref, o_ref):
    nloc = WIN_ROWS * GRID_W
    for rr in range(NA_RB):
        r = pl.program_id(1) * NA_RB + rr
        start = jnp.clip(r - WIN_ROWS // 2, 0, GRID_ROWS - WIN_ROWS)
        case = r - start
        row0 = pl.multiple_of(start * GRID_W, GRID_W)
        qrows = slice(rr * GRID_W, (rr + 1) * GRID_W)
        for h in range(NA_HEADS):
            q = (q_ref[qrows, C_NAQ + 64 * h:C_NAQ + 64 * h + 64] * QK_SCALE).astype(BF16)
            k = kv_ref[pl.ds(row0, nloc), C_NAK + 64 * h:C_NAK + 64 * h + 64].astype(BF16)
            v = kv_ref[pl.ds(row0, nloc), C_NAV + 64 * h:C_NAV + 64 * h + 64].astype(BF16)
            s_loc = _dot_nt(q, k) + bias_ref[h, pl.ds(case, 1)][0]
            s_ctx = _dot_nt(q, kc_ref[h].astype(BF16))
            m = jnp.maximum(jnp.max(s_loc, axis=-1, keepdims=True), jnp.max(s_ctx, axis=-1, keepdims=True))
            e_loc = jnp.exp(s_loc - m)
            e_ctx = jnp.exp(s_ctx - m)
            l = jnp.sum(e_loc, axis=-1, keepdims=True) + jnp.sum(e_ctx, axis=-1, keepdims=True)
            o = (_dot(e_loc.astype(BF16), v) + _dot(e_ctx.astype(BF16), vc_ref[h].astype(BF16))) * (1.0 / l)
            o_ref[qrows, 64 * h:64 * h + 64] = o.astype(BF16)


def _na_attn(p, cache_k, cache_v, bias, l):
    qblk0 = NP_TOK // (NA_RB * GRID_W)
    kvblk0 = NP_TOK // DEC_SEQ
    nrg = GRID_ROWS // NA_RB
    return pl.pallas_call(
        _na_kernel,
        grid=(DEC_BATCH, nrg),
        in_specs=[pl.BlockSpec((NA_RB * GRID_W, 768), lambda b, r: (qblk0 + b * nrg + r, 0)),
                  pl.BlockSpec((DEC_SEQ, 768), lambda b, r: (kvblk0 + b, 0)),
                  pl.BlockSpec((None, None, NA_HEADS, PAST, HEAD_DIM), lambda b, r: (b, l, 0, 0, 0)),
                  pl.BlockSpec((None, None, NA_HEADS, PAST, HEAD_DIM), lambda b, r: (b, l, 0, 0, 0)),
                  pl.BlockSpec((NA_HEADS, WIN_ROWS, GRID_W, WIN_ROWS * GRID_W), lambda b, r: (l, 0, 0, 0))],
        out_specs=pl.BlockSpec((NA_RB * GRID_W, 256), lambda b, r: (b * nrg + r, 0)),
        out_shape=jax.ShapeDtypeStruct((NS_TOK, 256), BF16),
        compiler_params=_cparams(("arbitrary", "arbitrary")),
    )(p, p, cache_k, cache_v, bias)


@functools.lru_cache(None)
def _rope_tables():
    t = np.arange(DEC_SEQ)
    lane = np.arange(128)
    dd = lane % HEAD_DIM
    pos = np.where(dd[None, :] < 32, (t // GRID_W)[:, None], (t % GRID_W)[:, None]).astype(np.float64)
    inv = ROPE_BASE ** (-(dd % 16).astype(np.float64) * 2.0 / 32.0)
    ang = pos * inv[None, :]
    first = (dd % 32) < 16
    cos = np.cos(ang)
    s_up = np.where(first[None, :], -np.sin(ang), 0.0)
    s_dn = np.where(first[None, :], 0.0, np.sin(ang))
    return tuple(np.asarray(a, np.float32) for a in (cos, s_up, s_dn))


def _rope(x, cos, s_up, s_dn):
    return x * cos + pltpu.roll(x, 112, axis=1) * s_up + pltpu.roll(x, 16, axis=1) * s_dn


def _sdiff_kernel(q_ref, k_ref, v_ref, ck_ref, cv_ref, cos_ref, sup_ref, sdn_ref,
                  lq1, lk1, lq2, lk2, g_ref, o_ref, k1_scr, k2_scr, v_scr, *, lam_init, tq):
    qb = pl.program_id(2)

    @pl.when(qb == 0)
    def _():
        kr = _rope(k_ref[...].astype(F32), cos_ref[...], sup_ref[...], sdn_ref[...])
        k1_scr[0:DEC_SEQ, :] = kr[:, :64].astype(BF16)
        k2_scr[0:DEC_SEQ, :] = kr[:, 64:].astype(BF16)
        k1_scr[DEC_SEQ:, :] = ck_ref[0].astype(BF16)
        k2_scr[DEC_SEQ:, :] = ck_ref[1].astype(BF16)
        v_scr[0:DEC_SEQ, :] = v_ref[...].astype(BF16)
        v_scr[DEC_SEQ:, :] = cv_ref[...].astype(BF16)

    lam = _lam(lq1, lk1, lq2, lk2, lam_init)
    half = tq // 2
    for c in range(2):
        rows = pl.ds(pl.multiple_of(qb * tq + c * half, half), half)
        qr = _rope(q_ref[c * half:(c + 1) * half, :].astype(F32), cos_ref[rows, :], sup_ref[rows, :],
                   sdn_ref[rows, :]) * QK_SCALE_LOG2
        o = _diff_head(qr[:, :64].astype(BF16), qr[:, 64:].astype(BF16), k1_scr[...], k2_scr[...],
                       v_scr[...], lam, g_ref[...], lam_init)
        o_ref[c * half:(c + 1) * half, :] = o.astype(BF16)


def _sdiff_attn(p, cache_k, cache_v, lq1, lk1, lq2, lk2, g, lam_init, l):
    tq = 512
    nq = DEC_SEQ // tq
    cos, s_up, s_dn = (jnp.asarray(a) for a in _rope_tables())
    vec = lambda n: pl.BlockSpec((1, n), lambda b, h, q: (0, 0))
    tab = pl.BlockSpec((DEC_SEQ, 128), lambda b, h, q: (0, 0))
    kvblk0 = NP_TOK // DEC_SEQ
    return pl.pallas_call(
        functools.partial(_sdiff_kernel, lam_init=lam_init, tq=tq),
        grid=(DEC_BATCH, DIFF_HEADS, nq),
        in_specs=[pl.BlockSpec((tq, 128), lambda b, h, q: (NP_TOK // tq + b * nq + q, C_DQ // 128 + h)),
                  pl.BlockSpec((DEC_SEQ, 128), lambda b, h, q: (kvblk0 + b, C_DK // 128 + h)),
                  pl.BlockSpec((DEC_SEQ, 128), lambda b, h, q: (kvblk0 + b, C_DV // 128 + h)),
                  pl.BlockSpec((None, None, None, 2, PAST, HEAD_DIM), lambda b, h, q: (b, l, h, 0, 0, 0)),
                  pl.BlockSpec((None, None, None, PAST, DIFF_V), lambda b, h, q: (b, l, h, 0, 0)),
                  tab, tab, tab, vec(64), vec(64), vec(64), vec(64), vec(128)],
        out_specs=pl.BlockSpec((tq, 128), lambda b, h, q: (b * nq + q, h)),
        out_shape=jax.ShapeDtypeStruct((NS_TOK, DIFF_HEADS * DIFF_V), BF16),
        scratch_shapes=[pltpu.VMEM((DEC_SEQ + PAST, HEAD_DIM), BF16),
                        pltpu.VMEM((DEC_SEQ + PAST, HEAD_DIM), BF16),
                        pltpu.VMEM((DEC_SEQ + PAST, DIFF_V), BF16)],
        compiler_params=_cparams(("arbitrary", "arbitrary", "arbitrary")),
    )(p, p, p, cache_k, cache_v, cos, s_up, s_dn, lq1, lk1, lq2, lk2, g)


@functools.lru_cache(None)
def _dft_consts(L):
    n = 2 * L
    k = np.arange(L)
    ang = 2.0 * np.pi * ((k[:, None] * k[None, :]) % n) / n
    alt = (-1.0) ** k
    fa = np.cos(ang)
    fb = -np.sin(ang)
    fb[0, :] = alt
    wgt = np.full((L,), 2.0 / n)
    wgt[0] = 1.0 / n
    ga = fa * wgt[:, None]
    gb = fb * wgt[:, None]
    gb[0, :] = alt / n
    f = np.concatenate([fa, fb], axis=0)
    g = np.concatenate([ga.T, gb.T], axis=1)
    return np.asarray(f, dtype=BF16), np.asarray(g, dtype=BF16)


@functools.lru_cache(None)
def _filter_consts(L):
    f32 = np.float32
    t = np.linspace(0.0, 1.0, L, dtype=f32)[:, None]
    pos = np.arange(L, dtype=f32)[:, None]
    bands = np.linspace(1e-4, HY_BANDS - 1, HY_BANDS, dtype=f32)[None, :]
    ang = f32(2.0 * math.pi) * bands * pos / f32(L)
    z = np.zeros((L, 128), f32)
    z[:, 0:1] = t
    z[:, 1:1 + HY_BANDS] = np.cos(ang)
    z[:, 1 + HY_BANDS:HY_EMB] = -np.sin(ang)
    min_decay = math.log(1e-2) / 1.5
    max_decay = math.log(1e-2) / 0.3
    deltas = np.abs(np.linspace(min_decay, max_decay, HY_W, dtype=f32))
    decay = np.exp(-t * deltas[None, :]).astype(f32)
    return z, decay


def _spectra_kernel(z_ref, w1_ref, b1_ref, w2_ref, b2_ref, fr_ref, w3_ref, dec_ref, fa_ref, fb_ref,
                    sa_ref, sb_ref, filt_scr, *, L, kc):
    j = pl.program_id(0)

    @pl.when(j == 0)
    def _():
        fr = fr_ref[...]
        hdn = jnp.sin(fr * (_dot3(z_ref[...], w1_ref[...]) + b1_ref[...]))
        hdn = jnp.sin(fr * (_dot3(hdn, w2_ref[...]) + b2_ref[...]))
        dec = dec_ref[...]
        not_first = lax.broadcasted_iota(jnp.int32, (L, HY_W), 0) > 0
        for o in range(2):
            hf = _dot3(hdn, w3_ref[:, 512 * o:512 * o + 256]) * dec
            hb = jnp.where(not_first, _dot3(hdn, w3_ref[:, 512 * o + 256:512 * o + 512]) * dec, 0.0)
            nrm = (jnp.sum(jnp.abs(hf), axis=0, keepdims=True)
                   + jnp.sum(jnp.abs(hb), axis=0, keepdims=True))
            filt_scr[:, 512 * o:512 * o + 256] = (hf / nrm).astype(BF16)
            filt_scr[:, 512 * o + 256:512 * o + 512] = (hb / nrm).astype(BF16)

    ta = _dot(fa_ref[...], filt_scr[...])
    tb = _dot(fb_ref[...], filt_scr[...])
    first = (lax.broadcasted_iota(jnp.int32, (kc, HY_W), 0) + j * kc) == 0
    for o in range(2):
        af, ab = ta[:, 512 * o:512 * o + 256], ta[:, 512 * o + 256:512 * o + 512]
        bf, bb = tb[:, 512 * o:512 * o + 256], tb[:, 512 * o + 256:512 * o + 512]
        sa_ref[:, 256 * o:256 * o + 256] = af + ab
        sb_ref[:, 256 * o:256 * o + 256] = jnp.where(first, bf + bb, bf - bb)


def _hy_spectra(L, kc, w1p, b1, w2, b2, fr, w3):
    z, decay = _filter_consts(L)
    f = jnp.asarray(_dft_consts(L)[0])
    nj = L // kc
    full = lambda shape: pl.BlockSpec(shape, lambda j: tuple(0 for _ in shape))
    return pl.pallas_call(
        functools.partial(_spectra_kernel, L=L, kc=kc),
        grid=(nj,),
        in_specs=[full((L, 128)), full((128, HY_HID)), full((1, HY_HID)), full((HY_HID, HY_HID)),
                  full((1, HY_HID)), full((1, HY_HID)), full((HY_HID, 4 * HY_W)), full((L, HY_W)),
                  pl.BlockSpec((kc, L), lambda j: (j, 0)),
                  pl.BlockSpec((kc, L), lambda j: (j + nj, 0))],
        out_specs=[pl.BlockSpec((kc, 2 * HY_W), lambda j: (j, 0)),
                   pl.BlockSpec((kc, 2 * HY_W), lambda j: (j, 0))],
        out_shape=[jax.ShapeDtypeStruct((L, 2 * HY_W), F32)] * 2,
        scratch_shapes=[pltpu.VMEM((L, 4 * HY_W), BF16)],
        compiler_params=_cparams(("arbitrary",)),
    )(jnp.asarray(z), w1p, b1, w2, b2, fr, w3, jnp.asarray(decay), f, f)


def _hyconv_kernel(*refs, L, nb, kc, nj, row_blk0, staged):
    n_in = 1 if staged else nb
    (cw_ref, cb_ref, d_ref, fa_ref, fb_ref, ga_ref, gb_ref, sa_ref, sb_ref, o_ref,
     xin_scr, x_scr, g1_scr, g2_scr, y_scr) = refs[n_in:n_in + 15]
    bg = pl.program_id(0)
    o = pl.program_id(1)
    j = pl.program_id(2)

    @pl.when((o == 0) & (j == 0))
    def _():
        row = lax.broadcasted_iota(jnp.int32, (L, HY_W), 0)
        pieces = [(i, part) for i in range(nb) for part in range(3)]
        if staged:
            p_hbm, (stage, sem) = refs[0], refs[n_in + 15:]

            def piece_copy(k):
                i, part = pieces[k]
                row0 = pl.multiple_of((row_blk0 + bg * nb + i) * L, L)
                return pltpu.make_async_copy(
                    p_hbm.at[pl.ds(row0, L), C_HY + HY_W * part:C_HY + HY_W * (part + 1)],
                    stage.at[k % 2], sem.at[k % 2])

            piece_copy(0).start()
            piece_copy(1).start()
        for k, (i, part) in enumerate(pieces):
            cols = slice(HY_W * i, HY_W * (i + 1))
            dst = (x_scr, g1_scr, g2_scr)[part]
            pc = slice(HY_W * part, HY_W * (part + 1))
            if staged:
                piece_copy(k).wait()
                u = stage[k % 2].astype(F32)
            else:
                u = refs[i][:, pc].astype(F32)
            up = jnp.where(row == 0, 0.0, pltpu.roll(u, 1, axis=0))
            un = jnp.where(row == L - 1, 0.0, pltpu.roll(u, L - 1, axis=0))
            dst[:, cols] = (up * cw_ref[0:1, pc] + u * cw_ref[1:2, pc] + un * cw_ref[2:3, pc]
                            + cb_ref[:, pc])
            if part == 0:
                xin_scr[:, cols] = x_scr[:, cols].astype(BF16)
            if staged and k + 2 < len(pieces):
                piece_copy(k + 2).start()

    @pl.when(j == 0)
    def _():
        y_scr[...] = jnp.zeros_like(y_scr)

    xa = _dot(fa_ref[...], xin_scr[...])
    xb = _dot(fb_ref[...], xin_scr[...])
    sa = jnp.concatenate([sa_ref[...]] * nb, axis=1)
    sb = jnp.concatenate([sb_ref[...]] * nb, axis=1)
    first = (lax.broadcasted_iota(jnp.int32, (kc, nb * HY_W), 0) + j * kc) == 0
    ya = jnp.where(first, xa * sa, xa * sa - xb * sb)
    yb = jnp.where(first, xb * sb, xa * sb + xb * sa)
    y_scr[...] += _dot(ga_ref[...], ya.astype(BF16)) + _dot(gb_ref[...], yb.astype(BF16))

    @pl.when(j == nj - 1)
    def _():
        dvec = jnp.concatenate([d_ref[pl.ds(o, 1), :]] * nb, axis=1)
        y = y_scr[...] + x_scr[...] * dvec

        @pl.when(o == 0)
        def _():
            zz = g1_scr[...] * y
            x_scr[...] = zz
            xin_scr[...] = zz.astype(BF16)

        @pl.when(o == 1)
        def _():
            res = g2_scr[...] * y
            for i in range(nb):
                o_ref[L * i:L * (i + 1), :] = res[:, HY_W * i:HY_W * (i + 1)].astype(BF16)


def _hy_conv(p, L, nb, kc, row_blk0, n_seq, conv_w, conv_b, d, sa, sb):
    f, g = (jnp.asarray(a) for a in _dft_consts(L))
    nj = L // kc
    staged = nb * L > DEC_SEQ
    if staged:
        u_specs = [pl.BlockSpec(memory_space=pl.ANY)]
        u_scratch = [pltpu.VMEM((2, L, HY_W), p.dtype), pltpu.SemaphoreType.DMA((2,))]
    else:
        u_specs = [pl.BlockSpec((L, 3 * HY_W), lambda bg, o, j, i=i: (row_blk0 + bg * nb + i, C_HY // 768))
                   for i in range(nb)]
        u_scratch = []
    small = lambda shape: pl.BlockSpec(shape, lambda bg, o, j: (0, 0))
    return pl.pallas_call(
        functools.partial(_hyconv_kernel, L=L, nb=nb, kc=kc, nj=nj, row_blk0=row_blk0, staged=staged),
        grid=(n_seq // nb, 2, nj),
        in_specs=u_specs + [
            small((3, 3 * HY_W)), small((1, 3 * HY_W)), small((2, HY_W)),
            pl.BlockSpec((kc, L), lambda bg, o, j: (j, 0)),
            pl.BlockSpec((kc, L), lambda bg, o, j: (j + nj, 0)),
            pl.BlockSpec((L, kc), lambda bg, o, j: (0, j)),
            pl.BlockSpec((L, kc), lambda bg, o, j: (0, j + nj)),
            pl.BlockSpec((kc, HY_W), lambda bg, o, j: (j, o)),
            pl.BlockSpec((kc, HY_W), lambda bg, o, j: (j, o))],
        out_specs=pl.BlockSpec((nb * L, HY_W), lambda bg, o, j: (bg, 0)),
        out_shape=jax.ShapeDtypeStruct((n_seq * L, HY_W), BF16),
        scratch_shapes=[pltpu.VMEM((L, nb * HY_W), BF16)] + [pltpu.VMEM((L, nb * HY_W), F32)] * 4
        + u_scratch,
        compiler_params=_cparams(("arbitrary", "arbitrary", "arbitrary")),
    )(*([p] * len(u_specs)), conv_w, conv_b, d, f, f, g, g, sa, sb)


def _outproj_kernel(xp_ref, xs_ref, mixp_ref, hyp_ref, nas_ref, ds_ref, hys_ref, g1_ref, w_ref, o_ref):
    i = pl.program_id(0)
    w = lambda a, b: w_ref[a:b, :].astype(BF16)

    @pl.when(i < NP_TOK // TM)
    def _():
        y = _dot(mixp_ref[...], w(0, 768)) + _dot(hyp_ref[...], w(768, D))
        o_ref[...] = xp_ref[...] + g1_ref[...] * y

    @pl.when(i >= NP_TOK // TM)
    def _():
        y = _dot(nas_ref[...], w(0, 256)) + _dot(ds_ref[...], w(256, 768)) + _dot(hys_ref[...], w(768, D))
        o_ref[...] = xs_ref[...] + g1_ref[...] * y


def _x_specs(xs_block0):
    npt = NP_TOK // TM
    return [pl.BlockSpec((TM, D), lambda i, *_: (jnp.minimum(i, npt - 1), 0)),
            pl.BlockSpec((TM, D), lambda i, *_: (jnp.maximum(i - npt, 0) + xs_block0, 0))]


def _outproj(xp, xs, xs_block0, mix_p, hy_p, na_s, d_s, hy_s, mod6, w_out, l):
    npt = NP_TOK // TM
    pidx = lambda i: (jnp.minimum(i, npt - 1), 0)
    sidx = lambda i: (jnp.maximum(i - npt, 0), 0)
    return pl.pallas_call(
        _outproj_kernel,
        grid=(N_TOK // TM,),
        in_specs=_x_specs(xs_block0) + [
                  pl.BlockSpec((TM, 768), pidx),
                  pl.BlockSpec((TM, HY_W), pidx),
                  pl.BlockSpec((TM, 256), sidx),
                  pl.BlockSpec((TM, 512), sidx),
                  pl.BlockSpec((TM, HY_W), sidx),
                  _mod_spec(2, TM),
                  pl.BlockSpec((None, D, D), lambda i: (l, 0, 0))],
        out_specs=pl.BlockSpec((TM, D), lambda i: (i, 0)),
        out_shape=jax.ShapeDtypeStruct((N_TOK, D), F32),
        compiler_params=_cparams(("arbitrary",)),
    )(xp, xs, mix_p, hy_p, na_s, d_s, hy_s, mod6, w_out)


def _ffn_kernel(x_ref, g_ref, sc_ref, sh_ref, gate_ref, wg_ref, wu_ref, wd_ref, o_ref, h_scr, *, nj):
    j = pl.program_id(1)

    @pl.when(j == 0)
    def _():
        h_scr[...] = _norm_mod(x_ref[...], g_ref[...], sc_ref[...], sh_ref[...]).astype(BF16)
        o_ref[...] = jnp.zeros_like(o_ref)

    wg, wu, wd = (r[...].astype(BF16) for r in (wg_ref, wu_ref, wd_ref))
    half = h_scr.shape[0] // 2
    for c in range(2):
        rows = slice(c * half, (c + 1) * half)
        h = h_scr[rows, :]
        a = _silu(_dot(h, wg)) * _dot(h, wu)
        o_ref[rows, :] += _dot(a.astype(BF16), wd)

    @pl.when(j == nj - 1)
    def _():
        o_ref[...] = x_ref[...] + gate_ref[...] * o_ref[...]


def _dense_ffn(x, g, mod6, wg, wu, wd, i_ffn):
    tf = 256
    tm = DEC_SEQ
    nj = FFN // tf
    return pl.pallas_call(
        functools.partial(_ffn_kernel, nj=nj),
        grid=(N_TOK // tm, nj),
        in_specs=[pl.BlockSpec((tm, D), lambda i, j: (i, 0)),
                  pl.BlockSpec((1, D), lambda i, j: (0, 0)),
                  _mod_spec(4, tm), _mod_spec(3, tm), _mod_spec(5, tm),
                  pl.BlockSpec((None, D, tf), lambda i, j: (i_ffn, 0, j)),
                  pl.BlockSpec((None, D, tf), lambda i, j: (i_ffn, 0, j)),
                  pl.BlockSpec((None, tf, D), lambda i, j: (i_ffn, j, 0))],
        out_specs=pl.BlockSpec((tm, D), lambda i, j: (i, 0)),
        out_shape=jax.ShapeDtypeStruct((N_TOK, D), F32),
        scratch_shapes=[pltpu.VMEM((tm, D), BF16)],
        compiler_params=_cparams(("arbitrary", "arbitrary")),
    )(x, g, mod6, mod6, mod6, wg, wu, wd)


def _router_kernel(x_ref, g_ref, sc_ref, sh_ref, wr_ref, h_ref, r_ref):
    h = _norm_mod(x_ref[...], g_ref[...], sc_ref[...], sh_ref[...])
    h_ref[...] = h.astype(BF16)
    lane = lax.broadcasted_iota(jnp.int32, (TM, 128), 1)
    lg = jnp.where(lane < N_EXP, _dot3(h, wr_ref[...]), -jnp.inf)
    m1 = jnp.max(lg, axis=-1, keepdims=True)
    i1 = jnp.min(jnp.where(lg == m1, lane, 128), axis=-1, keepdims=True)
    lg2 = jnp.where(lane == i1, -jnp.inf, lg)
    m2 = jnp.max(lg2, axis=-1, keepdims=True)
    i2 = jnp.min(jnp.where(lg2 == m2, lane, 128), axis=-1, keepdims=True)
    e = jnp.exp(m2 - m1)
    w1 = 1.0 / (1.0 + e)
    w2 = e / (1.0 + e)
    r_ref[...] = jnp.where(lane == 0, i1.astype(F32),
                           jnp.where(lane == 1, i2.astype(F32),
                                     jnp.where(lane == 2, w1, jnp.where(lane == 3, w2, 0.0))))


def _router(x, g, mod6, wr_pad):
    return pl.pallas_call(
        _router_kernel,
        grid=(N_TOK // TM,),
        in_specs=[pl.BlockSpec((TM, D), lambda i: (i, 0)),
                  pl.BlockSpec((1, D), lambda i: (0, 0)),
                  _mod_spec(4, TM), _mod_spec(3, TM),
                  pl.BlockSpec((D, 128), lambda i: (0, 0))],
        out_specs=[pl.BlockSpec((TM, D), lambda i: (i, 0)),
                   pl.BlockSpec((TM, 128), lambda i: (i, 0))],
        out_shape=[jax.ShapeDtypeStruct((N_TOK, D), BF16), jax.ShapeDtypeStruct((N_TOK, 128), F32)],
        compiler_params=_cparams(("arbitrary",)),
    )(x, g, mod6, mod6, wr_pad)


def _dispatch_kernel(blo_ref, bhi_ref, sexp_ref, h_hbm, rp_ref, o_ref, h_scr, sem):
    s = pl.program_id(0)

    @pl.when(s == 0)
    def _():
        cp = pltpu.make_async_copy(h_hbm, h_scr, sem)
        cp.start()
        cp.wait()

    e = sexp_ref[s]
    rows = s * MOE_TS + lax.broadcasted_iota(jnp.int32, (MOE_TS, MOE_DTB), 0)

    def sel(b):
        return (rows == rp_ref[e, pl.ds(b, 1), :]).astype(BF16)

    is_empty = blo_ref[s] > bhi_ref[s]

    @pl.when(is_empty)
    def _():
        o_ref[...] = jnp.zeros_like(o_ref)

    @pl.when(jnp.logical_not(is_empty))
    def _():
        b0 = jnp.minimum(blo_ref[s], N_TOK // MOE_DTB - MOE_DWIN)
        hwin = h_scr[pl.ds(pl.multiple_of(b0 * MOE_DTB, MOE_DTB), MOE_DWIN * MOE_DTB), :]
        acc = _dot(jnp.concatenate([sel(b0 + k) for k in range(MOE_DWIN)], axis=1), hwin)

        def body(b, acc):
            hb = h_scr[pl.ds(pl.multiple_of(b * MOE_DTB, MOE_DTB), MOE_DTB), :]
            return acc + _dot(sel(b), hb)

        acc = lax.fori_loop(b0 + MOE_DWIN, bhi_ref[s] + 1, body, acc)
        o_ref[...] = acc.astype(BF16)


def _dispatch(blo, bhi, sexp, h, rp_t):
    return pl.pallas_call(
        _dispatch_kernel,
        grid_spec=pltpu.PrefetchScalarGridSpec(
            num_scalar_prefetch=3,
            grid=(MOE_ROWS // MOE_TS,),
            in_specs=[pl.BlockSpec(memory_space=pl.ANY),
                      pl.BlockSpec((N_EXP, N_TOK // MOE_DTB, MOE_DTB), lambda s, *_: (0, 0, 0))],
            out_specs=pl.BlockSpec((MOE_TS, D), lambda s, *_: (s, 0)),
            scratch_shapes=[pltpu.VMEM((N_TOK, D), BF16), pltpu.SemaphoreType.DMA(())]),
        out_shape=jax.ShapeDtypeStruct((MOE_ROWS, D), BF16),
        compiler_params=_cparams(("arbitrary",)),
    )(blo, bhi, sexp, h, rp_t)


def _experts_kernel(te_ref, used_ref, rows_ref, xs_ref, wg_ref, wu_ref, wd_ref, o_ref, acc_scr, *, nj):
    i = pl.program_id(0)
    j = pl.program_id(1)
    nrows = rows_ref[i]

    @pl.when(j == 0)
    def _():
        acc_scr[...] = jnp.zeros_like(acc_scr)

    for sz in range(MOE_TQ, MOE_TM + 1, MOE_TQ):
        @pl.when(nrows == sz)
        def _():
            xb = xs_ref[0:sz, :]
            a = _silu(_dot(xb, wg_ref[...].astype(BF16))) * _dot(xb, wu_ref[...].astype(BF16))
            acc_scr[0:sz, :] += _dot(a.astype(BF16), wd_ref[...].astype(BF16))

    @pl.when(j == nj - 1)
    def _():
        o_ref[...] = acc_scr[...].astype(BF16)


def _experts(tile_expert, used, tile_rows, xs, wg, wu, wd, i_moe):
    nj = EXP_DIM // MOE_TF

    def tile(i, used):
        return jnp.minimum(i, used[0] - 1)

    def chunk(i, j, used):
        return jnp.where(i < used[0], j, nj - 1)

    return pl.pallas_call(
        functools.partial(_experts_kernel, nj=nj),
        grid_spec=pltpu.PrefetchScalarGridSpec(
            num_scalar_prefetch=3,
            grid=(MOE_YROWS // MOE_TM, nj),
            in_specs=[pl.BlockSpec((MOE_TM, D), lambda i, j, te, used, tr: (tile(i, used), 0)),
                      pl.BlockSpec((None, None, D, MOE_TF),
                                   lambda i, j, te, used, tr: (i_moe, te[tile(i, used)], 0, chunk(i, j, used))),
                      pl.BlockSpec((None, None, D, MOE_TF),
                                   lambda i, j, te, used, tr: (i_moe, te[tile(i, used)], 0, chunk(i, j, used))),
                      pl.BlockSpec((None, None, MOE_TF, D),
                                   lambda i, j, te, used, tr: (i_moe, te[tile(i, used)], chunk(i, j, used), 0))],
            out_specs=pl.BlockSpec((MOE_TM, D), lambda i, j, te, used, tr: (i, 0)),
            scratch_shapes=[pltpu.VMEM((MOE_TM, D), F32)]),
        out_shape=jax.ShapeDtypeStruct((MOE_YROWS, D), BF16),
        compiler_params=_cparams(("arbitrary", "arbitrary")),
    )(tile_expert, used, tile_rows, xs, wg, wu, wd)


def _combine_kernel(ws_ref, kind_ref, x_ref, rp_ref, comb_ref, gate_ref, fg_ref, ys_hbm, op_ref, os_ref,
                    win_scr, y_scr, sem):
    b = pl.program_id(0)
    nb = pl.num_programs(0)
    slot = b % 2

    def win_copy(blk, sl, e):
        start = pl.multiple_of(ws_ref[blk * N_EXP + e], MOE_ALIGN)
        return pltpu.make_async_copy(ys_hbm.at[pl.ds(start, MOE_WIN)], win_scr.at[sl, e], sem.at[sl, e])

    @pl.when(b == 0)
    def _():
        for e in range(N_EXP):
            win_copy(0, 0, e).start()

    @pl.when(b + 1 < nb)
    def _():
        for e in range(N_EXP):
            win_copy(b + 1, 1 - slot, e).start()

    col = lax.broadcasted_iota(jnp.int32, (MOE_TB, MOE_WIN), 1)
    y_scr[...] = jnp.zeros_like(y_scr)
    for e in range(N_EXP):
        win_copy(b, slot, e).wait()
        rel = rp_ref[:, e:e + 1] - ws_ref[b * N_EXP + e]
        cw = comb_ref[:, e:e + 1]
        kind = kind_ref[b * N_EXP + e]

        for k, wn in enumerate(MOE_WINS):
            @pl.when(kind == k)
            def _():
                sel = (rel == col[:, :wn]).astype(BF16) if wn == MOE_WIN else (
                    rel == lax.broadcasted_iota(jnp.int32, (MOE_TB, wn), 1)).astype(BF16)
                y_scr[...] += cw * _dot(sel, win_scr[slot, e, 0:wn, :])

    x = x_ref[...] + gate_ref[...] * y_scr[...]
    x = (x * lax.rsqrt(jnp.mean(x * x, axis=-1, keepdims=True) + EPS)) * fg_ref[...]

    @pl.when(b < NP_TOK // MOE_TB)
    def _():
        op_ref[...] = x

    @pl.when(b >= NP_TOK // MOE_TB)
    def _():
        os_ref[...] = x


def _combine(ws, kind, x, rp8, comb, mod6, final_g, ys):
    npb = NP_TOK // MOE_TB
    return pl.pallas_call(
        _combine_kernel,
        grid_spec=pltpu.PrefetchScalarGridSpec(
            num_scalar_prefetch=2,
            grid=(N_TOK // MOE_TB,),
            in_specs=[pl.BlockSpec((MOE_TB, D), lambda b, *_: (b, 0)),
                      pl.BlockSpec((MOE_TB, N_EXP), lambda b, *_: (b, 0)),
                      pl.BlockSpec((MOE_TB, N_EXP), lambda b, *_: (b, 0)),
                      _mod_spec(5, MOE_TB),
                      pl.BlockSpec((1, D), lambda b, *_: (0, 0)),
                      pl.BlockSpec(memory_space=pl.ANY)],
            out_specs=[pl.BlockSpec((MOE_TB, D), lambda b, *_: (jnp.minimum(b, npb - 1), 0)),
                       pl.BlockSpec((MOE_TB, D), lambda b, *_: (jnp.maximum(b - npb, 0), 0))],
            scratch_shapes=[pltpu.VMEM((2, N_EXP, MOE_WIN, D), BF16),
                            pltpu.VMEM((MOE_TB, D), F32),
                            pltpu.SemaphoreType.DMA((2, N_EXP))]),
        out_shape=[jax.ShapeDtypeStruct((NP_TOK, D), F32), jax.ShapeDtypeStruct((NS_TOK, D), F32)],
        compiler_params=_cparams(("arbitrary",)),
    )(ws, kind, x, rp8, comb, mod6, final_g, ys)


def _moe(x, g, mod6, router, wg, wu, wd, i_moe, final_g):
    wr_pad = jnp.pad(router, ((0, 0), (0, 128 - N_EXP)))
    h, r = _router(x, g, mod6, wr_pad)

    i32 = jnp.int32
    i12 = r[:, 0:2].astype(i32)
    earange = jnp.arange(N_EXP, dtype=i32)
    hit1 = i12[:, 0:1] == earange[None, :]
    hit2 = i12[:, 1:2] == earange[None, :]
    comb = jnp.where(hit1, r[:, 2:3], 0.0) + jnp.where(hit2, r[:, 3:4], 0.0)
    mask = (hit1 | hit2).astype(i32)
    csum = jnp.cumsum(mask, axis=0)
    counts = csum[-1]
    padded = ((counts + MOE_TM - 1) // MOE_TM) * MOE_TM
    ends = jnp.cumsum(padded)
    starts = ends - padded
    rp8 = jnp.where(mask > 0, starts[None, :] + csum - 1, -1).astype(i32)
    n_tiles = MOE_ROWS // MOE_TM
    tile_expert = jnp.minimum(
        jnp.searchsorted(ends, jnp.arange(n_tiles, dtype=i32) * MOE_TM, side="right"),
        N_EXP - 1).astype(i32)
    used = (ends[-1:] // MOE_TM).astype(i32)
    tile_row0 = jnp.arange(MOE_YROWS // MOE_TM, dtype=i32) * MOE_TM
    te_all = jnp.minimum(jnp.searchsorted(ends, tile_row0, side="right"), N_EXP - 1)
    valid = jnp.clip(counts[te_all] - (tile_row0 - starts[te_all]), 0, MOE_TM)
    valid = jnp.where(tile_row0 < ends[-1], valid, 0)
    tile_rows = (((valid + MOE_TQ - 1) // MOE_TQ) * MOE_TQ).astype(i32)

    sub_row0 = jnp.arange(MOE_ROWS // MOE_TS, dtype=i32) * MOE_TS
    sexp = tile_expert[sub_row0 // MOE_TM]
    qlo = sub_row0 - starts[sexp]
    qend = jnp.minimum(qlo + MOE_TS, counts[sexp])
    cbe = csum[MOE_DTB - 1::MOE_DTB, :].T[sexp]
    blo = jnp.sum((cbe <= qlo[:, None]).astype(i32), axis=1)
    bhi = jnp.minimum(jnp.sum((cbe < qend[:, None]).astype(i32), axis=1), N_TOK // MOE_DTB - 1)
    empty = qend <= qlo
    blo = jnp.where(empty, 1, blo).astype(i32)
    bhi = jnp.where(empty, 0, bhi).astype(i32)

    cb = csum[MOE_TB - 1::MOE_TB, :]
    cprev = jnp.concatenate([jnp.zeros((1, N_EXP), i32), cb[:-1]], axis=0)
    ws = (((starts[None, :] + cprev) // MOE_ALIGN) * MOE_ALIGN).reshape(-1).astype(i32)
    n_be = cb - cprev
    kind = sum((n_be > wn - MOE_ALIGN).astype(i32) for wn in MOE_WINS[:-1])
    kind = jnp.where(n_be == 0, len(MOE_WINS), kind).reshape(-1).astype(i32)

    rp_t = rp8.T.reshape(N_EXP, N_TOK // MOE_DTB, MOE_DTB)
    xs = _dispatch(blo, bhi, sexp, h, rp_t)
    ys = _experts(tile_expert, used, tile_rows, xs, wg, wu, wd, i_moe)
    return _combine(ws, kind, x, rp8, comb, mod6, final_g, ys)


assert DEPTH == 2

def kernel(x_prompt, x_sample, cache_na_k, cache_na_v, cache_diff_k, cache_diff_v, c, c_ctx, w_in, w_out, ada_w, ada_b, norm_mix_g, norm_ffn_g, na_rpb, diff_lq1, diff_lk1, diff_lq2, diff_lk2, diff_subln_g, hy_conv_w, hy_conv_b, hy_d, hy_f_w1, hy_f_b1, hy_f_w2, hy_f_b2, hy_f_freq, hy_f_w3, ffn_w_gate, ffn_w_up, ffn_w_down, moe_router, moe_w_gate, moe_w_up, moe_w_down, final_norm_g):
    xparts = (x_prompt.reshape(NP_TOK, D), x_sample.reshape(NS_TOK, D), 0)
    cond8 = jnp.concatenate([c_ctx[None, :], c, jnp.zeros((5, D), F32)], axis=0)
    mods = _modulation(cond8, ada_w, ada_b)
    final_g = final_norm_g.reshape(1, D)
    bias = _na_bias(na_rpb)

    leaves = [jnp.zeros((BATCH, DEPTH, NA_HEADS, SEQ, HEAD_DIM), F32),
              jnp.zeros((BATCH, DEPTH, NA_HEADS, SEQ, HEAD_DIM), F32),
              jnp.zeros((BATCH, DEPTH, DIFF_HEADS, 2, SEQ, HEAD_DIM), F32),
              jnp.zeros((BATCH, DEPTH, DIFF_HEADS, SEQ, DIFF_V), F32)]
    for l in range(DEPTH):
        lam_init = 0.8 - 0.6 * math.exp(-0.3 * l)
        mod6 = mods[l].reshape(8, 6, D).transpose(1, 0, 2).reshape(6, 8, 1, D)
        row = lambda a: a[l].reshape(1, -1)
        lq1, lk1, lq2, lk2, subg = row(diff_lq1), row(diff_lk1), row(diff_lq2), row(diff_lk2), row(diff_subln_g)

        p, *leaves = _inproj(*xparts, row(norm_mix_g), mod6, w_in, l, leaves)

        mix_p = _prompt_attn(p, lq1, lk1, lq2, lk2, subg, lam_init)
        na_s = _na_attn(p, cache_na_k, cache_na_v, bias, l)
        d_s = _sdiff_attn(p, cache_diff_k, cache_diff_v, lq1, lk1, lq2, lk2, subg, lam_init, l)

        w1p = jnp.pad(hy_f_w1[l], ((0, 128 - HY_EMB), (0, 0)))
        fargs = (w1p, row(hy_f_b1), hy_f_w2[l], row(hy_f_b2), row(hy_f_freq), hy_f_w3[l])
        cargs = (hy_conv_w[l], row(hy_conv_b), hy_d[l])
        sa_p, sb_p = _hy_spectra(SEQ, SEQ, *fargs)
        hy_p = _hy_conv(p, SEQ, 4, SEQ, 0, BATCH, *cargs, sa_p, sb_p)
        sa_s, sb_s = _hy_spectra(DEC_SEQ, 512, *fargs)
        hy_s = _hy_conv(p, DEC_SEQ, 2, 512, NP_TOK // DEC_SEQ, DEC_BATCH, *cargs, sa_s, sb_s)

        x = _outproj(*xparts, mix_p, hy_p, na_s, d_s, hy_s, mod6, w_out, l)

        if l == 0:
            x = _dense_ffn(x, row(norm_ffn_g), mod6, ffn_w_gate, ffn_w_up, ffn_w_down, 0)
            xparts = (x, x, NP_TOK // TM)
        else:
            yp, ys = _moe(x, row(norm_ffn_g), mod6, moe_router[0], moe_w_gate, moe_w_up, moe_w_down,
                          0, final_g)

    return (yp.reshape(BATCH, SEQ, D), ys.reshape(DEC_BATCH, DEC_SEQ, D), *leaves)
```
